```python
import jax, jax.numpy as jnp
from jax import lax
import numpy as np

D_MODEL = 2048
BATCH = 8
SEQ = 2048
DEPTH = 1

CHUNK = 64
PLE_DIM = 256
EPS = 1e-6

MIX_A = D_MODEL // 2
CONV_GROUPS = 8
CONV_WIDTH = 3

MIX_B = D_MODEL // 2
GLA_HEADS = 4
GLA_DK_TOTAL = MIX_B // 2
GLA_DK = GLA_DK_TOTAL // GLA_HEADS
GLA_DV = MIX_B // GLA_HEADS
GLA_GATE_RANK = 16
GLA_TAU = 16.0

D_FF = ((8 * D_MODEL // 3 + 255) // 256) * 256

IN_SPLITS = (MIX_A, MIX_A, MIX_A,
             GLA_DK_TOTAL, GLA_DK_TOTAL, MIX_B, MIX_B, GLA_GATE_RANK,
             D_MODEL, D_MODEL)
IN_COLS = sum(IN_SPLITS)

kernel_name = "hybrid_shortconv_gla_gated_merge_block"


def rms_norm(x, gain):
    xf = x.astype(jnp.float32)
    y = xf * lax.rsqrt(jnp.mean(xf * xf, axis=-1, keepdims=True) + EPS)
    return (y * gain.astype(jnp.float32)).astype(x.dtype)


def short_conv_branch(u_x, u_b, u_c, conv_w, w_out):
    u = u_c * u_x
    y = lax.conv_general_dilated(
        u, conv_w[:, None, :].astype(u.dtype), window_strides=(1,),
        padding=[(CONV_WIDTH - 1, 0)],
        dimension_numbers=('NWC', 'WIO', 'NWC'),
        feature_group_count=MIX_A)
    return (u_b * y) @ w_out


def gla_branch(q, k, v, og, a_lr, w_alpha_up, b_alpha_up, head_gain, w_out):
    out_dtype = v.dtype
    bsz, seq = q.shape[0], q.shape[1]
    n = seq // CHUNK
    f32 = jnp.float32
    z = a_lr.astype(f32) @ w_alpha_up.astype(f32) + b_alpha_up.astype(f32)
    log_a = jax.nn.log_sigmoid(z) / GLA_TAU
    shp_k = (bsz, n, CHUNK, GLA_HEADS, GLA_DK)
    q = q.astype(f32).reshape(shp_k) * (GLA_DK ** -0.5)
    k = k.astype(f32).reshape(shp_k)
    v = v.astype(f32).reshape(bsz, n, CHUNK, GLA_HEADS, GLA_DV)
    b = jnp.cumsum(log_a.reshape(shp_k), axis=2)
    b_last = b[:, :, -1:]
    mid = b[:, :, CHUNK // 2:CHUNK // 2 + 1]
    a_fwd = jnp.einsum('bnthd,bnshd->bnhts', q * jnp.exp(b - mid), k * jnp.exp(mid - b))
    a_rev = jnp.einsum('bnthd,bnshd->bnhts', q * jnp.exp(mid - b), k * jnp.exp(b - mid))
    lower = jnp.tril(jnp.ones((CHUNK, CHUNK), dtype=bool))
    att = jnp.where(lower, a_fwd, a_rev)
    o_intra = jnp.einsum('bnhts,bnshv->bnthv', att, v)
    u_c = jnp.einsum('bnshd,bnshv->bnhdv', k * jnp.exp(b_last - b), v)
    decay = jnp.exp(b_last[:, :, 0])

    def step(state, inp):
        dec, upd = inp
        return dec[..., None] * state + upd, state

    s0 = jnp.zeros((bsz, GLA_HEADS, GLA_DK, GLA_DV), f32)
    _, s_in = lax.scan(step, s0, (jnp.moveaxis(decay, 1, 0), jnp.moveaxis(u_c, 1, 0)))
    s_in = jnp.moveaxis(s_in, 0, 1)
    o_inter = jnp.einsum('bnthd,bnhdv->bnthv', q * jnp.exp(b), s_in)
    o = (o_intra + o_inter).reshape(bsz, seq, GLA_HEADS, GLA_DV)
    o = o * lax.rsqrt(jnp.mean(o * o, axis=-1, keepdims=True) + EPS) * head_gain.astype(f32)
    o = o.reshape(bsz, seq, MIX_B) * jax.nn.silu(og.astype(f32))
    return o.astype(out_dtype) @ w_out


def _fwd_setup_inputs(seed: int = 0) -> dict:
    key = jax.random.key(seed)
    ks = jax.random.split(key, 24)
    f32 = jnp.float32

    def nrm(k, shape, scale):
        return jax.random.normal(k, shape, f32) * scale

    def gain(k, dim):
        return 1.0 + 0.05 * jax.random.normal(k, (DEPTH, dim), f32)

    return {
        "x": nrm(ks[0], (BATCH, SEQ, D_MODEL), 1.0),
        "p": nrm(ks[1], (DEPTH, BATCH, SEQ, PLE_DIM), 1.0),
        "w_in": nrm(ks[2], (DEPTH, D_MODEL, IN_COLS), D_MODEL ** -0.5),
        "conv_w": nrm(ks[3], (DEPTH, CONV_WIDTH, MIX_A), CONV_WIDTH ** -0.5),
        "w_a_out": nrm(ks[4], (DEPTH, MIX_A, D_MODEL), MIX_A ** -0.5),
        "w_alpha_up": nrm(ks[5], (DEPTH, GLA_GATE_RANK, GLA_DK_TOTAL), GLA_GATE_RANK ** -0.5),
        "b_alpha_up": nrm(ks[6], (DEPTH, GLA_DK_TOTAL), 0.1),
        "gla_head_gain": gain(ks[7], GLA_DV),
        "w_b_out": nrm(ks[8], (DEPTH, MIX_B, D_MODEL), MIX_B ** -0.5),
        "w_mix_out": nrm(ks[9], (DEPTH, D_MODEL, D_MODEL), D_MODEL ** -0.5),
        "g_pre_mix": gain(ks[10], D_MODEL),
        "g_post_mix": gain(ks[11], D_MODEL),
        "g_pre_ffn": gain(ks[12], D_MODEL),
        "g_post_ffn": gain(ks[13], D_MODEL),
        "w_ff_gate": nrm(ks[14], (DEPTH, D_MODEL, D_FF), D_MODEL ** -0.5),
        "w_ff_up": nrm(ks[15], (DEPTH, D_MODEL, D_FF), D_MODEL ** -0.5),
        "w_ff_down": nrm(ks[16], (DEPTH, D_FF, D_MODEL), D_FF ** -0.5),
        "g_pre_ple": gain(ks[17], D_MODEL),
        "g_post_ple": gain(ks[18], D_MODEL),
        "w_ple_gate": nrm(ks[19], (DEPTH, D_MODEL, D_MODEL), D_MODEL ** -0.5),
        "w_ple_proj": nrm(ks[20], (DEPTH, PLE_DIM, D_MODEL), PLE_DIM ** -0.5),
    }


def _fwd_reference(x, p, w_in, conv_w, w_a_out, w_alpha_up, b_alpha_up, gla_head_gain,
              w_b_out, w_mix_out, g_pre_mix, g_post_mix, g_pre_ffn, g_post_ffn,
              w_ff_gate, w_ff_up, w_ff_down, g_pre_ple, g_post_ple,
              w_ple_gate, w_ple_proj):
    split_idx = [int(c) for c in np.cumsum(IN_SPLITS)[:-1]]
    for i in range(DEPTH):
        h = rms_norm(x, g_pre_mix[i])
        proj = h @ w_in[i]
        (a_x, a_b, a_c, q, k, v, og, a_lr, gate_a, gate_b) = jnp.split(proj, split_idx, axis=-1)
        y_a = short_conv_branch(a_x, a_b, a_c, conv_w[i], w_a_out[i])
        y_b = gla_branch(q, k, v, og, a_lr, w_alpha_up[i], b_alpha_up[i],
                         gla_head_gain[i], w_b_out[i])
        mix = jax.nn.sigmoid(gate_a) * y_a + jax.nn.sigmoid(gate_b) * y_b
        x = x + rms_norm(mix @ w_mix_out[i], g_post_mix[i])
        h = rms_norm(x, g_pre_ffn[i])
        f = (jax.nn.silu(h @ w_ff_gate[i]) * (h @ w_ff_up[i])) @ w_ff_down[i]
        x = x + rms_norm(f, g_post_ffn[i])
        h = rms_norm(x, g_pre_ple[i])
        e = jax.nn.sigmoid(h @ w_ple_gate[i]) * (p[i] @ w_ple_proj[i])
        x = x + rms_norm(e, g_post_ple[i])
    return x


import jax as _jax
import jax.numpy as _jnp

TWIN_FORMAT = 'train_step'
FWD_PARAMS = ['x', 'p', 'w_in', 'conv_w', 'w_a_out', 'w_alpha_up', 'b_alpha_up', 'gla_head_gain', 'w_b_out', 'w_mix_out', 'g_pre_mix', 'g_post_mix', 'g_pre_ffn', 'g_post_ffn', 'w_ff_gate', 'w_ff_up', 'w_ff_down', 'g_pre_ple', 'g_post_ple', 'w_ple_gate', 'w_ple_proj']
TWIN_WEIGHTS = ['w_in', 'conv_w', 'w_a_out', 'w_alpha_up', 'b_alpha_up', 'gla_head_gain', 'w_b_out', 'w_mix_out', 'g_pre_mix', 'g_post_mix', 'g_pre_ffn', 'g_post_ffn', 'w_ff_gate', 'w_ff_up', 'w_ff_down', 'g_pre_ple', 'g_post_ple', 'w_ple_gate', 'w_ple_proj']
TWIN_DIFF_INPUT = 'x'
TWIN_INPUTS = ['x', 'p', 'w_in', 'conv_w', 'w_a_out', 'w_alpha_up', 'b_alpha_up', 'gla_head_gain', 'w_b_out', 'w_mix_out', 'g_pre_mix', 'g_post_mix', 'g_pre_ffn', 'g_post_ffn', 'w_ff_gate', 'w_ff_up', 'w_ff_down', 'g_pre_ple', 'g_post_ple', 'w_ple_gate', 'w_ple_proj', 'loss_target', 'm_w_in', 'm_conv_w', 'm_w_a_out', 'm_w_alpha_up', 'm_b_alpha_up', 'm_gla_head_gain', 'm_w_b_out', 'm_w_mix_out', 'm_g_pre_mix', 'm_g_post_mix', 'm_g_pre_ffn', 'm_g_post_ffn', 'm_w_ff_gate', 'm_w_ff_up', 'm_w_ff_down', 'm_g_pre_ple', 'm_g_post_ple', 'm_w_ple_gate', 'm_w_ple_proj', 'v_w_in', 'v_conv_w', 'v_w_a_out', 'v_w_alpha_up', 'v_b_alpha_up', 'v_gla_head_gain', 'v_w_b_out', 'v_w_mix_out', 'v_g_pre_mix', 'v_g_post_mix', 'v_g_pre_ffn', 'v_g_post_ffn', 'v_w_ff_gate', 'v_w_ff_up', 'v_w_ff_down', 'v_g_pre_ple', 'v_g_post_ple', 'v_w_ple_gate', 'v_w_ple_proj']
TWIN_OUTPUTS = ['loss', 'grad_x', 'grad_w_in', 'grad_conv_w', 'grad_w_a_out', 'grad_w_alpha_up', 'grad_b_alpha_up', 'grad_gla_head_gain', 'grad_w_b_out', 'grad_w_mix_out', 'grad_g_pre_mix', 'grad_g_post_mix', 'grad_g_pre_ffn', 'grad_g_post_ffn', 'grad_w_ff_gate', 'grad_w_ff_up', 'grad_w_ff_down', 'grad_g_pre_ple', 'grad_g_post_ple', 'grad_w_ple_gate', 'grad_w_ple_proj', 'delta_w_in', 'delta_conv_w', 'delta_w_a_out', 'delta_w_alpha_up', 'delta_b_alpha_up', 'delta_gla_head_gain', 'delta_w_b_out', 'delta_w_mix_out', 'delta_g_pre_mix', 'delta_g_post_mix', 'delta_g_pre_ffn', 'delta_g_post_ffn', 'delta_w_ff_gate', 'delta_w_ff_up', 'delta_w_ff_down', 'delta_g_pre_ple', 'delta_g_post_ple', 'delta_w_ple_gate', 'delta_w_ple_proj', 'new_m_w_in', 'new_m_conv_w', 'new_m_w_a_out', 'new_m_w_alpha_up', 'new_m_b_alpha_up', 'new_m_gla_head_gain', 'new_m_w_b_out', 'new_m_w_mix_out', 'new_m_g_pre_mix', 'new_m_g_post_mix', 'new_m_g_pre_ffn', 'new_m_g_post_ffn', 'new_m_w_ff_gate', 'new_m_w_ff_up', 'new_m_w_ff_down', 'new_m_g_pre_ple', 'new_m_g_post_ple', 'new_m_w_ple_gate', 'new_m_w_ple_proj', 'new_v_w_in', 'new_v_conv_w', 'new_v_w_a_out', 'new_v_w_alpha_up', 'new_v_b_alpha_up', 'new_v_gla_head_gain', 'new_v_w_b_out', 'new_v_w_mix_out', 'new_v_g_pre_mix', 'new_v_g_post_mix', 'new_v_g_pre_ffn', 'new_v_g_post_ffn', 'new_v_w_ff_gate', 'new_v_w_ff_up', 'new_v_w_ff_down', 'new_v_g_pre_ple', 'new_v_g_post_ple', 'new_v_w_ple_gate', 'new_v_w_ple_proj']
TWIN_LEAF_KINDS = {'loss': 'loss', 'grad_x': 'grad_x', 'grad_w_in': 'grad_w', 'grad_conv_w': 'grad_w', 'grad_w_a_out': 'grad_w', 'grad_w_alpha_up': 'grad_w', 'grad_b_alpha_up': 'grad_w', 'grad_gla_head_gain': 'grad_w', 'grad_w_b_out': 'grad_w', 'grad_w_mix_out': 'grad_w', 'grad_g_pre_mix': 'grad_w', 'grad_g_post_mix': 'grad_w', 'grad_g_pre_ffn': 'grad_w', 'grad_g_post_ffn': 'grad_w', 'grad_w_ff_gate': 'grad_w', 'grad_w_ff_up': 'grad_w', 'grad_w_ff_down': 'grad_w', 'grad_g_pre_ple': 'grad_w', 'grad_g_post_ple': 'grad_w', 'grad_w_ple_gate': 'grad_w', 'grad_w_ple_proj': 'grad_w', 'delta_w_in': 'delta_w', 'delta_conv_w': 'delta_w', 'delta_w_a_out': 'delta_w', 'delta_w_alpha_up': 'delta_w', 'delta_b_alpha_up': 'delta_w', 'delta_gla_head_gain': 'delta_w', 'delta_w_b_out': 'delta_w', 'delta_w_mix_out': 'delta_w', 'delta_g_pre_mix': 'delta_w', 'delta_g_post_mix': 'delta_w', 'delta_g_pre_ffn': 'delta_w', 'delta_g_post_ffn': 'delta_w', 'delta_w_ff_gate': 'delta_w', 'delta_w_ff_up': 'delta_w', 'delta_w_ff_down': 'delta_w', 'delta_g_pre_ple': 'delta_w', 'delta_g_post_ple': 'delta_w', 'delta_w_ple_gate': 'delta_w', 'delta_w_ple_proj': 'delta_w', 'new_m_w_in': 'new_m', 'new_m_conv_w': 'new_m', 'new_m_w_a_out': 'new_m', 'new_m_w_alpha_up': 'new_m', 'new_m_b_alpha_up': 'new_m', 'new_m_gla_head_gain': 'new_m', 'new_m_w_b_out': 'new_m', 'new_m_w_mix_out': 'new_m', 'new_m_g_pre_mix': 'new_m', 'new_m_g_post_mix': 'new_m', 'new_m_g_pre_ffn': 'new_m', 'new_m_g_post_ffn': 'new_m', 'new_m_w_ff_gate': 'new_m', 'new_m_w_ff_up': 'new_m', 'new_m_w_ff_down': 'new_m', 'new_m_g_pre_ple': 'new_m', 'new_m_g_post_ple': 'new_m', 'new_m_w_ple_gate': 'new_m', 'new_m_w_ple_proj': 'new_m', 'new_v_w_in': 'new_v', 'new_v_conv_w': 'new_v', 'new_v_w_a_out': 'new_v', 'new_v_w_alpha_up': 'new_v', 'new_v_b_alpha_up': 'new_v', 'new_v_gla_head_gain': 'new_v', 'new_v_w_b_out': 'new_v', 'new_v_w_mix_out': 'new_v', 'new_v_g_pre_mix': 'new_v', 'new_v_g_post_mix': 'new_v', 'new_v_g_pre_ffn': 'new_v', 'new_v_g_post_ffn': 'new_v', 'new_v_w_ff_gate': 'new_v', 'new_v_w_ff_up': 'new_v', 'new_v_w_ff_down': 'new_v', 'new_v_g_pre_ple': 'new_v', 'new_v_g_post_ple': 'new_v', 'new_v_w_ple_gate': 'new_v', 'new_v_w_ple_proj': 'new_v'}


def _forward(args):
    return _fwd_reference(*[args[k] for k in FWD_PARAMS])


def _output_shape():
    out = _jax.eval_shape(lambda: _forward(_fwd_setup_inputs(0)))
    return out.shape, out.dtype

N_MICROBATCH = 1
ADAM_LR = 0.001
ADAM_B1 = 0.9
ADAM_B2 = 0.999
ADAM_EPS = 1e-08
ADAM_WD = 0.01
ADAM_STEP = 10
PER_EXAMPLE_BATCH_AXIS = {'x': 0, 'p': 1, 'loss_target': 0}
SHARED_INPUTS = []
_WEIGHT_DTYPES = {'w_in': _jnp.float32, 'conv_w': _jnp.float32, 'w_a_out': _jnp.float32, 'w_alpha_up': _jnp.float32, 'b_alpha_up': _jnp.float32, 'gla_head_gain': _jnp.float32, 'w_b_out': _jnp.float32, 'w_mix_out': _jnp.float32, 'g_pre_mix': _jnp.float32, 'g_post_mix': _jnp.float32, 'g_pre_ffn': _jnp.float32, 'g_post_ffn': _jnp.float32, 'w_ff_gate': _jnp.float32, 'w_ff_up': _jnp.float32, 'w_ff_down': _jnp.float32, 'g_pre_ple': _jnp.float32, 'g_post_ple': _jnp.float32, 'w_ple_gate': _jnp.float32, 'w_ple_proj': _jnp.float32}
MOMENT_SCALE = {'w_in': 1.532403e-01, 'conv_w': 2.323956e-01, 'w_a_out': 1.645616e-01, 'w_alpha_up': 2.096072e-02, 'b_alpha_up': 7.981808e-02, 'gla_head_gain': 2.956707e-01, 'w_b_out': 9.629912e-02, 'w_mix_out': 1.954617e-01, 'g_pre_mix': 3.438316e-01, 'g_post_mix': 8.020890e+00, 'g_pre_ffn': 1.963474e-01, 'g_post_ffn': 8.016919e+00, 'w_ff_gate': 8.061584e-02, 'w_ff_up': 9.006056e-02, 'w_ff_down': 1.491702e-01, 'g_pre_ple': 4.934190e-02, 'g_post_ple': 8.035263e+00, 'w_ple_gate': 4.930182e-02, 'w_ple_proj': 1.304401e-01}


def _to_microbatches(a, axis):
    t = _jnp.moveaxis(a, axis, 0)
    t = t.reshape((N_MICROBATCH, t.shape[0] // N_MICROBATCH) + t.shape[1:])
    return _jnp.moveaxis(t, 1, axis + 1)


def setup_inputs(seed: int = 0) -> dict:
    inp = _fwd_setup_inputs(seed)
    key = _jax.random.fold_in(_jax.random.key(seed), 7919)
    shape, _ = _output_shape()
    out = dict(inp)
    out["loss_target"] = _jax.random.normal(_jax.random.fold_in(key, 0), shape, _jnp.float32)
    for i, name in enumerate(TWIN_WEIGHTS):
        w = inp[name].astype(_jnp.float32)
        if MOMENT_SCALE is None:
            s = _jnp.sqrt(_jnp.mean(_jnp.square(w)) + 1e-30)
        else:
            s = MOMENT_SCALE[name]
        km, kv = _jax.random.split(_jax.random.fold_in(key, i + 1))
        out[name] = w
        out["m_" + name] = s * _jax.random.normal(km, w.shape, _jnp.float32)
        out["v_" + name] = (s * s) * _jax.random.uniform(kv, w.shape, _jnp.float32, 0.5, 1.5)
    if N_MICROBATCH > 1:
        for name, axis in PER_EXAMPLE_BATCH_AXIS.items():
            out[name] = _to_microbatches(out[name], axis)
    return {'x': out['x'], 'p': out['p'], 'w_in': out['w_in'], 'conv_w': out['conv_w'], 'w_a_out': out['w_a_out'], 'w_alpha_up': out['w_alpha_up'], 'b_alpha_up': out['b_alpha_up'], 'gla_head_gain': out['gla_head_gain'], 'w_b_out': out['w_b_out'], 'w_mix_out': out['w_mix_out'], 'g_pre_mix': out['g_pre_mix'], 'g_post_mix': out['g_post_mix'], 'g_pre_ffn': out['g_pre_ffn'], 'g_post_ffn': out['g_post_ffn'], 'w_ff_gate': out['w_ff_gate'], 'w_ff_up': out['w_ff_up'], 'w_ff_down': out['w_ff_down'], 'g_pre_ple': out['g_pre_ple'], 'g_post_ple': out['g_post_ple'], 'w_ple_gate': out['w_ple_gate'], 'w_ple_proj': out['w_ple_proj'], 'loss_target': out['loss_target'], 'm_w_in': out['m_w_in'], 'm_conv_w': out['m_conv_w'], 'm_w_a_out': out['m_w_a_out'], 'm_w_alpha_up': out['m_w_alpha_up'], 'm_b_alpha_up': out['m_b_alpha_up'], 'm_gla_head_gain': out['m_gla_head_gain'], 'm_w_b_out': out['m_w_b_out'], 'm_w_mix_out': out['m_w_mix_out'], 'm_g_pre_mix': out['m_g_pre_mix'], 'm_g_post_mix': out['m_g_post_mix'], 'm_g_pre_ffn': out['m_g_pre_ffn'], 'm_g_post_ffn': out['m_g_post_ffn'], 'm_w_ff_gate': out['m_w_ff_gate'], 'm_w_ff_up': out['m_w_ff_up'], 'm_w_ff_down': out['m_w_ff_down'], 'm_g_pre_ple': out['m_g_pre_ple'], 'm_g_post_ple': out['m_g_post_ple'], 'm_w_ple_gate': out['m_w_ple_gate'], 'm_w_ple_proj': out['m_w_ple_proj'], 'v_w_in': out['v_w_in'], 'v_conv_w': out['v_conv_w'], 'v_w_a_out': out['v_w_a_out'], 'v_w_alpha_up': out['v_w_alpha_up'], 'v_b_alpha_up': out['v_b_alpha_up'], 'v_gla_head_gain': out['v_gla_head_gain'], 'v_w_b_out': out['v_w_b_out'], 'v_w_mix_out': out['v_w_mix_out'], 'v_g_pre_mix': out['v_g_pre_mix'], 'v_g_post_mix': out['v_g_post_mix'], 'v_g_pre_ffn': out['v_g_pre_ffn'], 'v_g_post_ffn': out['v_g_post_ffn'], 'v_w_ff_gate': out['v_w_ff_gate'], 'v_w_ff_up': out['v_w_ff_up'], 'v_w_ff_down': out['v_w_ff_down'], 'v_g_pre_ple': out['v_g_pre_ple'], 'v_g_post_ple': out['v_g_post_ple'], 'v_w_ple_gate': out['v_w_ple_gate'], 'v_w_ple_proj': out['v_w_ple_proj']}


def _loss(weights, diff, rest, loss_target):
    with _jax.named_scope("forward"):
        args = {**rest, TWIN_DIFF_INPUT: diff, **{k: w.astype(_WEIGHT_DTYPES[k]) for k, w in weights.items()}}
        y = _forward(args)
    with _jax.named_scope("loss_head"):
        err = _jnp.square(y.astype(_jnp.float32) - loss_target)
        return 0.5 * _jnp.sum(_jnp.mean(err, axis=-1)) if err.ndim else 0.5 * err


def _adamw(w, g, m, v):
    m = ADAM_B1 * m + (1.0 - ADAM_B1) * g
    v = ADAM_B2 * v + (1.0 - ADAM_B2) * _jnp.square(g)
    m_hat = m / (1.0 - ADAM_B1 ** ADAM_STEP)
    v_hat = v / (1.0 - ADAM_B2 ** ADAM_STEP)
    delta = -ADAM_LR * (m_hat / (_jnp.sqrt(v_hat) + ADAM_EPS) + ADAM_WD * w)
    return delta, m, v


def reference(x, p, w_in, conv_w, w_a_out, w_alpha_up, b_alpha_up, gla_head_gain, w_b_out, w_mix_out, g_pre_mix, g_post_mix, g_pre_ffn, g_post_ffn, w_ff_gate, w_ff_up, w_ff_down, g_pre_ple, g_post_ple, w_ple_gate, w_ple_proj, loss_target, m_w_in, m_conv_w, m_w_a_out, m_w_alpha_up, m_b_alpha_up, m_gla_head_gain, m_w_b_out, m_w_mix_out, m_g_pre_mix, m_g_post_mix, m_g_pre_ffn, m_g_post_ffn, m_w_ff_gate, m_w_ff_up, m_w_ff_down, m_g_pre_ple, m_g_post_ple, m_w_ple_gate, m_w_ple_proj, v_w_in, v_conv_w, v_w_a_out, v_w_alpha_up, v_b_alpha_up, v_gla_head_gain, v_w_b_out, v_w_mix_out, v_g_pre_mix, v_g_post_mix, v_g_pre_ffn, v_g_post_ffn, v_w_ff_gate, v_w_ff_up, v_w_ff_down, v_g_pre_ple, v_g_post_ple, v_w_ple_gate, v_w_ple_proj):
    given = dict(x=x, p=p, w_in=w_in, conv_w=conv_w, w_a_out=w_a_out, w_alpha_up=w_alpha_up, b_alpha_up=b_alpha_up, gla_head_gain=gla_head_gain, w_b_out=w_b_out, w_mix_out=w_mix_out, g_pre_mix=g_pre_mix, g_post_mix=g_post_mix, g_pre_ffn=g_pre_ffn, g_post_ffn=g_post_ffn, w_ff_gate=w_ff_gate, w_ff_up=w_ff_up, w_ff_down=w_ff_down, g_pre_ple=g_pre_ple, g_post_ple=g_post_ple, w_ple_gate=w_ple_gate, w_ple_proj=w_ple_proj, loss_target=loss_target, m_w_in=m_w_in, m_conv_w=m_conv_w, m_w_a_out=m_w_a_out, m_w_alpha_up=m_w_alpha_up, m_b_alpha_up=m_b_alpha_up, m_gla_head_gain=m_gla_head_gain, m_w_b_out=m_w_b_out, m_w_mix_out=m_w_mix_out, m_g_pre_mix=m_g_pre_mix, m_g_post_mix=m_g_post_mix, m_g_pre_ffn=m_g_pre_ffn, m_g_post_ffn=m_g_post_ffn, m_w_ff_gate=m_w_ff_gate, m_w_ff_up=m_w_ff_up, m_w_ff_down=m_w_ff_down, m_g_pre_ple=m_g_pre_ple, m_g_post_ple=m_g_post_ple, m_w_ple_gate=m_w_ple_gate, m_w_ple_proj=m_w_ple_proj, v_w_in=v_w_in, v_conv_w=v_conv_w, v_w_a_out=v_w_a_out, v_w_alpha_up=v_w_alpha_up, v_b_alpha_up=v_b_alpha_up, v_gla_head_gain=v_gla_head_gain, v_w_b_out=v_w_b_out, v_w_mix_out=v_w_mix_out, v_g_pre_mix=v_g_pre_mix, v_g_post_mix=v_g_post_mix, v_g_pre_ffn=v_g_pre_ffn, v_g_post_ffn=v_g_post_ffn, v_w_ff_gate=v_w_ff_gate, v_w_ff_up=v_w_ff_up, v_w_ff_down=v_w_ff_down, v_g_pre_ple=v_g_pre_ple, v_g_post_ple=v_g_post_ple, v_w_ple_gate=v_w_ple_gate, v_w_ple_proj=v_w_ple_proj)
    weights = {n: given[n] for n in TWIN_WEIGHTS}
    shared = {n: given[n] for n in SHARED_INPUTS}
    per_example = {n: given[n] for n in ['x', 'p']}
    grad_fn = _jax.value_and_grad(_loss, argnums=(0, 1))

    def one_microbatch(ex, loss_target):
        ex = dict(ex)
        diff = ex.pop(TWIN_DIFF_INPUT)
        return grad_fn(weights, diff, {**shared, **ex}, loss_target)

    if N_MICROBATCH == 1:
        loss, (grad_w, grad_x) = one_microbatch(per_example, given["loss_target"])
    else:
        def body(carry, xs):
            loss_sum, grad_sum = carry
            l_k, (gw_k, gx_k) = one_microbatch(xs[0], xs[1])
            with _jax.named_scope("update"):
                return (loss_sum + l_k, _jax.tree.map(_jnp.add, grad_sum, gw_k)), gx_k

        init = (_jnp.zeros((), _jnp.float32), _jax.tree.map(_jnp.zeros_like, weights))
        (loss, grad_w), grad_x = _jax.lax.scan(body, init, (per_example, given["loss_target"]))
    with _jax.named_scope("update"):
        delta_w, new_m, new_v = {}, {}, {}
        for n in TWIN_WEIGHTS:
            delta_w[n], new_m[n], new_v[n] = _adamw(weights[n], grad_w[n], given["m_" + n], given["v_" + n])
    return (loss, grad_x, *[grad_w[n] for n in TWIN_WEIGHTS], *[delta_w[n] for n in TWIN_WEIGHTS],
            *[new_m[n] for n in TWIN_WEIGHTS], *[new_v[n] for n in TWIN_WEIGHTS])
```

```python
import jax
import jax.numpy as jnp
from jax import lax
from jax.experimental import pallas as pl
from jax.experimental.pallas import tpu as pltpu

F32, BF16 = jnp.float32, jnp.bfloat16
EPS = 1e-6
CHUNK = 64
HEADS, DK, DV = 4, 128, 256
GATE_RANK = 16
TAU = 16.0
LR, B1, B2, ADAM_EPS, WD, STEP = 0.001, 0.9, 0.999, 1e-08, 0.01, 10
N_DEV = 8
MESH = pl.DeviceIdType.MESH
VMEM_LIMIT = 56 * 1024 * 1024

C_GA, C_GB, C_AX, C_AB, C_AC, C_Q, C_K, C_V, C_OG, C_ALR = 0, 2048, 4096, 5120, 6144, 7168, 7680, 8192, 9216, 10240
IN_PAD = 10368
R_AX, R_AB, R_AC, R_Q, R_K, R_V, R_OG, R_ALR, R_GA, R_GB, R_END = 0, 1024, 2048, 3072, 3584, 4096, 5120, 6144, 6160, 8208, 10256


def _params(sem):
    return pltpu.CompilerParams(dimension_semantics=sem, vmem_limit_bytes=VMEM_LIMIT)


def _pick(n, cands):
    for c in cands:
        if n % c == 0:
            return c
    return n


def _mm(a, b, mode, out_dtype, name):
    if mode == "nn":
        (m, k), (k2, n) = a.shape, b.shape
    elif mode == "nt":
        (m, k), (n, k2) = a.shape, b.shape
    else:
        (k, m), (k2, n) = a.shape, b.shape
    assert k == k2 and a.dtype == BF16 and b.dtype == BF16, (name, a.shape, b.shape, a.dtype, b.dtype)
    tm = _pick(m, (1024, 512, 256))
    tn = _pick(n, (1152, 1024, 1408, 512, 256))
    tk = _pick(k, (512, 1152, 256))
    nk = k // tk
    dims = {"nn": (((1,), (0,)), ((), ())), "nt": (((1,), (1,)), ((), ())), "tn": (((0,), (0,)), ((), ()))}[mode]

    def body(a_ref, b_ref, o_ref, acc_ref):
        kk = pl.program_id(2)

        @pl.when(kk == 0)
        def _():
            acc_ref[...] = jnp.zeros_like(acc_ref)

        acc_ref[...] += lax.dot_general(a_ref[...], b_ref[...], dims, preferred_element_type=F32)

        @pl.when(kk == nk - 1)
        def _():
            o_ref[...] = acc_ref[...].astype(o_ref.dtype)

    a_spec = pl.BlockSpec((tk, tm), lambda i, j, kk: (kk, i)) if mode == "tn" else pl.BlockSpec((tm, tk), lambda i, j, kk: (i, kk))
    b_spec = pl.BlockSpec((tn, tk), lambda i, j, kk: (j, kk)) if mode == "nt" else pl.BlockSpec((tk, tn), lambda i, j, kk: (kk, j))
    return pl.pallas_call(
        body, name=name, grid=(m // tm, n // tn, nk),
        in_specs=[a_spec, b_spec], out_specs=pl.BlockSpec((tm, tn), lambda i, j, kk: (i, j)),
        out_shape=jax.ShapeDtypeStruct((m, n), out_dtype),
        scratch_shapes=[pltpu.VMEM((tm, tn), F32)],
        compiler_params=_params(("parallel", "parallel", "arbitrary")),
    )(a, b)


def _rows(body, t, tr, ins, outs, name):
    in_specs = []
    for arr, sp in ins:
        if sp[0] == "t":
            in_specs.append(pl.BlockSpec((tr, sp[1]), lambda i, cb=sp[2]: (i, cb)))
        else:
            in_specs.append(pl.BlockSpec(arr.shape, lambda i, nd=arr.ndim: (0,) * nd))
    out_specs, out_shape = [], []
    for shape, dt, kind in outs:
        out_specs.append(pl.BlockSpec((tr, shape[1]), lambda i: (i, 0)) if kind == "t" else pl.BlockSpec(shape, lambda i: (0, 0)))
        out_shape.append(jax.ShapeDtypeStruct(shape, dt))
    return pl.pallas_call(
        body, name=name, grid=(t // tr,), in_specs=in_specs, out_specs=out_specs, out_shape=out_shape,
        compiler_params=_params(("arbitrary",)),
    )(*[arr for arr, _ in ins])


def _rinv(v):
    return lax.rsqrt(jnp.mean(v * v, axis=-1, keepdims=True) + EPS)


def _sig(v):
    return 1.0 / (1.0 + jnp.exp(-v))


def _acc(ref, val):
    @pl.when(pl.program_id(0) == 0)
    def _():
        ref[...] = jnp.zeros_like(ref)

    ref[...] += jnp.sum(val, axis=0, keepdims=True)


def _rms_fwd(x, g, name):
    t, d = x.shape

    def body(x_ref, g_ref, h_ref):
        xv = x_ref[...]
        h_ref[...] = (xv * _rinv(xv) * g_ref[...]).astype(BF16)

    return _rows(body, t, 256, [(x, ("t", d, 0)), (g, ("b",))], [((t, d), BF16, "t")], name)[0]


def _post_pre(x, m, g_post, g_pre, name):
    t, d = x.shape

    def body(x_ref, m_ref, gp_ref, gn_ref, xo_ref, h_ref):
        mv = m_ref[...]
        xn = x_ref[...] + mv * _rinv(mv) * gp_ref[...]
        xo_ref[...] = xn
        h_ref[...] = (xn * _rinv(xn) * gn_ref[...]).astype(BF16)

    return _rows(body, t, 128, [(x, ("t", d, 0)), (m, ("t", d, 0)), (g_post, ("b",)), (g_pre, ("b",))],
                 [((t, d), F32, "t"), ((t, d), BF16, "t")], name)


def _mix_fwd(proj, ya, yb, name):
    t, d = ya.shape

    def body(ga_ref, gb_ref, ya_ref, yb_ref, o_ref):
        o_ref[...] = (_sig(ga_ref[...].astype(F32)) * ya_ref[...].astype(F32)
                      + _sig(gb_ref[...].astype(F32)) * yb_ref[...].astype(F32)).astype(BF16)

    return _rows(body, t, 256, [(proj, ("t", d, C_GA // d)), (proj, ("t", d, C_GB // d)), (ya, ("t", d, 0)), (yb, ("t", d, 0))],
                 [((t, d), BF16, "t")], name)[0]


def _mix_bwd(dmix, proj, ya, yb, name):
    t, d = ya.shape

    def body(dm_ref, ga_ref, gb_ref, ya_ref, yb_ref, dg_ref, dya_ref, dyb_ref):
        dm = dm_ref[...]
        sa, sb = _sig(ga_ref[...].astype(F32)), _sig(gb_ref[...].astype(F32))
        dg_ref[:, :d] = (dm * ya_ref[...].astype(F32) * sa * (1.0 - sa)).astype(BF16)
        dg_ref[:, d:] = (dm * yb_ref[...].astype(F32) * sb * (1.0 - sb)).astype(BF16)
        dya_ref[...] = (dm * sa).astype(BF16)
        dyb_ref[...] = (dm * sb).astype(BF16)

    return _rows(body, t, 128,
                 [(dmix, ("t", d, 0)), (proj, ("t", d, C_GA // d)), (proj, ("t", d, C_GB // d)), (ya, ("t", d, 0)), (yb, ("t", d, 0))],
                 [((t, 2 * d), BF16, "t"), ((t, d), BF16, "t"), ((t, d), BF16, "t")], name)


def _swiglu_fwd(fgu, name):
    t, f2 = fgu.shape
    f = f2 // 2
    tc = _pick(f, (1408, 512))
    nc = f // tc

    def body(g_ref, u_ref, s_ref):
        gv = g_ref[...].astype(F32)
        s_ref[...] = (gv * _sig(gv) * u_ref[...].astype(F32)).astype(BF16)

    return pl.pallas_call(
        body, name=name, grid=(t // 512, nc),
        in_specs=[pl.BlockSpec((512, tc), lambda i, j: (i, j)), pl.BlockSpec((512, tc), lambda i, j: (i, j + nc))],
        out_specs=pl.BlockSpec((512, tc), lambda i, j: (i, j)), out_shape=jax.ShapeDtypeStruct((t, f), BF16),
        compiler_params=_params(("parallel", "parallel")),
    )(fgu, fgu)


def _swiglu_bwd(ds, fgu, name):
    t, f2 = fgu.shape
    f = f2 // 2
    tc = _pick(f, (1408, 512))
    nc = f // tc

    def body(ds_ref, g_ref, u_ref, o_ref):
        dsv, gv, uv = ds_ref[...].astype(F32), g_ref[...].astype(F32), u_ref[...].astype(F32)
        sg = _sig(gv)
        dfg = dsv * uv * sg * (1.0 + gv * (1.0 - sg))
        dfu = dsv * gv * sg
        o_ref[...] = jnp.where(pl.program_id(1) < nc, dfg, dfu).astype(BF16)

    return pl.pallas_call(
        body, name=name, grid=(t // 512, 2 * nc),
        in_specs=[pl.BlockSpec((512, tc), lambda i, j: (i, j % nc)), pl.BlockSpec((512, tc), lambda i, j: (i, j % nc)),
                  pl.BlockSpec((512, tc), lambda i, j: (i, j % nc + nc))],
        out_specs=pl.BlockSpec((512, tc), lambda i, j: (i, j)), out_shape=jax.ShapeDtypeStruct((t, f2), BF16),
        compiler_params=_params(("parallel", "parallel")),
    )(ds, fgu, fgu)


def _ple_final(x2, pg, pp, tgt, g_post, name):
    t, d = x2.shape

    def body(x_ref, pg_ref, pp_ref, t_ref, g_ref, loss_ref, d3_ref, dpg_ref, dpp_ref, dg_ref):
        sg, ppv, g = _sig(pg_ref[...]), pp_ref[...], g_ref[...]
        e = sg * ppv
        r = _rinv(e)
        eh = e * r
        diff = x_ref[...] + eh * g - t_ref[...]
        loss_ref[...] = 0.5 * jnp.mean(diff * diff, axis=-1, keepdims=True)
        d3 = diff * (1.0 / d)
        d3_ref[...] = d3
        gd = d3 * g
        de = r * (gd - eh * jnp.mean(gd * eh, axis=-1, keepdims=True))
        dpg_ref[...] = (de * ppv * sg * (1.0 - sg)).astype(BF16)
        dpp_ref[...] = (de * sg).astype(BF16)
        _acc(dg_ref, d3 * eh)

    return _rows(body, t, 128, [(x2, ("t", d, 0)), (pg, ("t", d, 0)), (pp, ("t", d, 0)), (tgt, ("t", d, 0)), (g_post, ("b",))],
                 [((t, 1), F32, "t"), ((t, d), F32, "t"), ((t, d), BF16, "t"), ((t, d), BF16, "t"), ((1, d), F32, "a")], name)


def _norm_bwd(dn, dh, x, g_pre, fm, g_post, name):
    t, d = x.shape
    two = fm is not None

    def body(*refs):
        if two:
            dn_ref, dh_ref, x_ref, gp_ref, f_ref, gq_ref, dx_ref, df_ref, dgp_ref, dgq_ref = refs
        else:
            dn_ref, dh_ref, x_ref, gp_ref, dx_ref, dgp_ref = refs
        xv, dhv = x_ref[...], dh_ref[...]
        r = _rinv(xv)
        xh = xv * r
        gd = dhv * gp_ref[...]
        dx = dn_ref[...] + r * (gd - xh * jnp.mean(gd * xh, axis=-1, keepdims=True))
        dx_ref[...] = dx
        _acc(dgp_ref, dhv * xh)
        if two:
            fv = f_ref[...]
            rf = _rinv(fv)
            fh = fv * rf
            gd2 = dx * gq_ref[...]
            df_ref[...] = (rf * (gd2 - fh * jnp.mean(gd2 * fh, axis=-1, keepdims=True))).astype(BF16)
            _acc(dgq_ref, dx * fh)

    ins = [(dn, ("t", d, 0)), (dh, ("t", d, 0)), (x, ("t", d, 0)), (g_pre, ("b",))]
    outs = [((t, d), F32, "t")]
    if two:
        ins += [(fm, ("t", d, 0)), (g_post, ("b",))]
        outs += [((t, d), BF16, "t"), ((1, d), F32, "a"), ((1, d), F32, "a")]
    else:
        outs += [((1, d), F32, "a")]
    return _rows(body, t, 128, ins, outs, name)


CONV_TC = 256


def _shift_down(v, s):
    rows = lax.broadcasted_iota(jnp.int32, v.shape, 0)
    return jnp.where(rows >= s, pltpu.roll(v, s, 0), 0.0)


def _shift_up(v, s):
    n = v.shape[0]
    rows = lax.broadcasted_iota(jnp.int32, v.shape, 0)
    return jnp.where(rows < n - s, pltpu.roll(v, n - s, 0), 0.0)


def _conv_specs(t):
    nb = 1024 // CONV_TC
    seg = lambda c0: pl.BlockSpec((t, CONV_TC), lambda j, cb=c0 // CONV_TC: (0, cb + j))
    own = pl.BlockSpec((t, CONV_TC), lambda j: (0, j))
    wspec = pl.BlockSpec((3, CONV_TC), lambda j: (0, j))
    return nb, seg, own, wspec


def _conv_fwd(proj, conv_w, name):
    t = proj.shape[0]
    nb, seg, own, wspec = _conv_specs(t)

    def body(ax_ref, ab_ref, ac_ref, w_ref, za_ref):
        u = ac_ref[...].astype(F32) * ax_ref[...].astype(F32)
        w = w_ref[...]
        yc = w[0:1] * _shift_down(u, 2) + w[1:2] * _shift_down(u, 1) + w[2:3] * u
        za_ref[...] = (ab_ref[...].astype(F32) * yc).astype(BF16)

    return pl.pallas_call(
        body, name=name, grid=(nb,), in_specs=[seg(C_AX), seg(C_AB), seg(C_AC), wspec], out_specs=own,
        out_shape=jax.ShapeDtypeStruct((t, 1024), BF16), compiler_params=_params(("parallel",)),
    )(proj, proj, proj, conv_w)


def _conv_bwd(dza, proj, conv_w, name):
    t = proj.shape[0]
    nb, seg, own, wspec = _conv_specs(t)

    def body(dz_ref, ax_ref, ab_ref, ac_ref, w_ref, dax_ref, dab_ref, dac_ref, dw_ref):
        ax, ab, ac, dz = ax_ref[...].astype(F32), ab_ref[...].astype(F32), ac_ref[...].astype(F32), dz_ref[...].astype(F32)
        w = w_ref[...]
        u = ac * ax
        u1, u2 = _shift_down(u, 1), _shift_down(u, 2)
        yc = w[0:1] * u2 + w[1:2] * u1 + w[2:3] * u
        dab_ref[...] = (dz * yc).astype(BF16)
        dyc = dz * ab
        du = w[2:3] * dyc + w[1:2] * _shift_up(dyc, 1) + w[0:1] * _shift_up(dyc, 2)
        dax_ref[...] = (du * ac).astype(BF16)
        dac_ref[...] = (du * ax).astype(BF16)
        dw_ref[0:1, :] = jnp.sum(dyc * u2, axis=0, keepdims=True)
        dw_ref[1:2, :] = jnp.sum(dyc * u1, axis=0, keepdims=True)
        dw_ref[2:3, :] = jnp.sum(dyc * u, axis=0, keepdims=True)

    act = jax.ShapeDtypeStruct((t, 1024), BF16)
    return pl.pallas_call(
        body, name=name, grid=(nb,), in_specs=[own, seg(C_AX), seg(C_AB), seg(C_AC), wspec], out_specs=[own, own, own, wspec],
        out_shape=[act, act, act, jax.ShapeDtypeStruct((3, 1024), F32)], compiler_params=_params(("parallel",)),
    )(dza, proj, proj, proj, conv_w)


def _dot(a, b, dims, precision=None):
    return lax.dot_general(a, b, (dims, ((), ())), precision=precision, preferred_element_type=F32)


def _gla_chunk(q, k, v, og, alr, s_in, wa, ba, gain):
    c = q.shape[0]
    hi = lax.Precision.HIGHEST
    z = _dot(alr, wa, ((1,), (0,))) + ba
    la = (jnp.minimum(z, 0.0) - jnp.log(1.0 + jnp.exp(-jnp.abs(z)))) * (1.0 / TAU)
    row = lax.broadcasted_iota(jnp.int32, (c, c), 0)
    col = lax.broadcasted_iota(jnp.int32, (c, c), 1)
    lower = row >= col
    b = _dot(lower.astype(F32), la, ((1,), (0,)), hi)
    trow = lax.broadcasted_iota(jnp.int32, la.shape, 0)
    mid = jnp.sum(jnp.where(trow <= c // 2, la, 0.0), axis=0, keepdims=True)
    blast = jnp.sum(la, axis=0, keepdims=True)
    qs = q * (DK ** -0.5)
    e_up, e_dn = jnp.exp(b - mid), jnp.exp(mid - b)
    a_fwd = _dot(qs * e_up, k * e_dn, ((1,), (1,)))
    a_rev = _dot(qs * e_dn, k * e_up, ((1,), (1,)))
    att = jnp.where(lower, a_fwd, a_rev)
    o = _dot(att, v, ((1,), (0,))) + _dot(qs * jnp.exp(b), s_in, ((1,), (0,)))
    upd = _dot(k * jnp.exp(blast - b), v, ((0,), (0,)))
    blast_col = _dot(la, jnp.ones((c, DV), F32), ((0,), (0,)), hi)
    s_out = jnp.exp(blast_col) * s_in + upd
    on = o * _rinv(o) * gain
    return on * og * _sig(og), s_out


def _gla_specs(t, rev):
    n = t // CHUNK
    ch = (lambda i: n - 1 - i) if rev else (lambda i: i)
    col = lambda w, c0: pl.BlockSpec((CHUNK, w), lambda i, h, cb=c0 // w: (ch(i), cb + h))
    specs = dict(
        q=col(DK, C_Q), k=col(DK, C_K), v=col(DV, C_V), og=col(DV, C_OG),
        alr=pl.BlockSpec((CHUNK, 128), lambda i, h: (ch(i), C_ALR // 128)),
        wa=pl.BlockSpec((128, DK), lambda i, h: (0, h)), ba=pl.BlockSpec((1, DK), lambda i, h: (0, h)),
        gain=pl.BlockSpec((1, DV), lambda i, h: (0, 0)),
        state=pl.BlockSpec((None, None, DK, DV), lambda i, h: (ch(i), h, 0, 0)),
        odk=pl.BlockSpec((CHUNK, DK), lambda i, h: (ch(i), h)), odv=pl.BlockSpec((CHUNK, DV), lambda i, h: (ch(i), h)),
        oalr=pl.BlockSpec((CHUNK, 128), lambda i, h: (ch(i), 0)),
    )
    return n, specs


def _gla_fwd(proj, wa, ba, gain, name):
    t = proj.shape[0]
    n, sp = _gla_specs(t, False)

    def body(q_ref, k_ref, v_ref, og_ref, alr_ref, wa_ref, ba_ref, g_ref, zb_ref, st_ref, s_scr):
        h = pl.program_id(1)

        @pl.when(pl.program_id(0) == 0)
        def _():
            s_scr[h] = jnp.zeros((DK, DV), F32)

        s_in = s_scr[h]
        st_ref[...] = s_in
        zb, s_out = _gla_chunk(q_ref[...].astype(F32), k_ref[...].astype(F32), v_ref[...].astype(F32), og_ref[...].astype(F32),
                               alr_ref[...].astype(F32), s_in, wa_ref[...].astype(F32), ba_ref[...], g_ref[...])
        zb_ref[...] = zb.astype(BF16)
        s_scr[h] = s_out

    return pl.pallas_call(
        body, name=name, grid=(n, HEADS),
        in_specs=[sp["q"], sp["k"], sp["v"], sp["og"], sp["alr"], sp["wa"], sp["ba"], sp["gain"]],
        out_specs=[sp["odv"], sp["state"]],
        out_shape=[jax.ShapeDtypeStruct((t, HEADS * DV), BF16), jax.ShapeDtypeStruct((n, HEADS, DK, DV), F32)],
        scratch_shapes=[pltpu.VMEM((HEADS, DK, DV), F32)],
        compiler_params=_params(("arbitrary", "arbitrary")),
    )(proj, proj, proj, proj, proj, wa, ba, gain)


def _gla_bwd(dzb, proj, states, wa, ba, gain, name):
    t = proj.shape[0]
    n, sp = _gla_specs(t, True)

    def body(dz_ref, q_ref, k_ref, v_ref, og_ref, alr_ref, st_ref, wa_ref, ba_ref, g_ref,
             dq_ref, dk_ref, dv_ref, dog_ref, dalr_ref, dwa_ref, dba_ref, dg_ref, ds_scr):
        i, h = pl.program_id(0), pl.program_id(1)

        @pl.when(i == 0)
        def _():
            ds_scr[h] = jnp.zeros((DK, DV), F32)
            dwa_ref[h] = jnp.zeros((128, DK), F32)
            dba_ref[h] = jnp.zeros((1, DK), F32)

        @pl.when((i == 0) & (h == 0))
        def _():
            dg_ref[...] = jnp.zeros_like(dg_ref)

        args = (q_ref[...].astype(F32), k_ref[...].astype(F32), v_ref[...].astype(F32), og_ref[...].astype(F32),
                alr_ref[...].astype(F32), st_ref[...], wa_ref[...].astype(F32), ba_ref[...], g_ref[...])
        _, vjp = jax.vjp(_gla_chunk, *args)
        dq, dk, dv, dog, dalr, ds_in, dwa, dba, dgain = vjp((dz_ref[...].astype(F32), ds_scr[h]))
        dq_ref[...] = dq.astype(BF16)
        dk_ref[...] = dk.astype(BF16)
        dv_ref[...] = dv.astype(BF16)
        dog_ref[...] = dog.astype(BF16)

        @pl.when(h == 0)
        def _():
            dalr_ref[...] = dalr

        @pl.when(h != 0)
        def _():
            dalr_ref[...] += dalr

        ds_scr[h] = ds_in
        dwa_ref[h] += dwa
        dba_ref[h] += dba
        dg_ref[...] += dgain

    whole = lambda shape: pl.BlockSpec(shape, lambda i, h, nd=len(shape): (0,) * nd)
    return pl.pallas_call(
        body, name=name, grid=(n, HEADS),
        in_specs=[sp["odv"], sp["q"], sp["k"], sp["v"], sp["og"], sp["alr"], sp["state"], sp["wa"], sp["ba"], sp["gain"]],
        out_specs=[sp["odk"], sp["odk"], sp["odv"], sp["odv"], sp["oalr"], whole((HEADS, 128, DK)), whole((HEADS, 1, DK)), whole((1, DV))],
        out_shape=[jax.ShapeDtypeStruct((t, HEADS * DK), BF16), jax.ShapeDtypeStruct((t, HEADS * DK), BF16),
                   jax.ShapeDtypeStruct((t, HEADS * DV), BF16), jax.ShapeDtypeStruct((t, HEADS * DV), BF16),
                   jax.ShapeDtypeStruct((t, 128), F32), jax.ShapeDtypeStruct((HEADS, 128, DK), F32),
                   jax.ShapeDtypeStruct((HEADS, 1, DK), F32), jax.ShapeDtypeStruct((1, DV), F32)],
        scratch_shapes=[pltpu.VMEM((HEADS, DK, DV), F32)],
        compiler_params=_params(("arbitrary", "arbitrary")),
    )(dzb, proj, proj, proj, proj, proj, states, wa, ba, gain)


def _local_step(x, p, tgt, w, small):
    conv_w, w_alpha, b_alpha, gain = small["conv_w"], small["w_alpha_up"], small["b_alpha_up"], small["gla_head_gain"]
    wa_p = jnp.zeros((128, HEADS * DK), BF16).at[:GATE_RANK].set(w_alpha.astype(BF16))

    h1 = _rms_fwd(x, small["g_pre_mix"], "rms_pre_mix")
    proj = _mm(h1, w["in_p"], "nn", BF16, "mm_proj")
    za = _conv_fwd(proj, conv_w, "conv_fwd")
    zb, states = _gla_fwd(proj, wa_p, b_alpha, gain, "gla_fwd")
    ya = _mm(za, w["a_out"], "nn", BF16, "mm_ya")
    yb = _mm(zb, w["b_out"], "nn", BF16, "mm_yb")
    mix = _mix_fwd(proj, ya, yb, "mix_fwd")
    m2 = _mm(mix, w["mix"], "nn", F32, "mm_mix")
    x1, h2 = _post_pre(x, m2, small["g_post_mix"], small["g_pre_ffn"], "norm_mix_ffn")
    fgu = _mm(h2, w["gu"], "nn", BF16, "mm_gu")
    s = _swiglu_fwd(fgu, "swiglu_fwd")
    f = _mm(s, w["down"], "nn", F32, "mm_down")
    x2, h3 = _post_pre(x1, f, small["g_post_ffn"], small["g_pre_ple"], "norm_ffn_ple")
    pg = _mm(h3, w["pg"], "nn", F32, "mm_pg")
    p_bf = p.astype(BF16)
    pp = _mm(p_bf, w["pp"], "nn", F32, "mm_pp")
    loss_rows, d3, dpg, dpp, dg_post_ple = _ple_final(x2, pg, pp, tgt, small["g_post_ple"], "ple_final")

    gw = {}
    gw["pp"] = _mm(p_bf, dpp, "tn", BF16, "mm_dw_pp")
    gw["pg"] = _mm(h3, dpg, "tn", BF16, "mm_dw_pg")
    dh3 = _mm(dpg, w["pg"], "nt", F32, "mm_dh3")
    d2, df, dg_pre_ple, dg_post_ffn = _norm_bwd(d3, dh3, x2, small["g_pre_ple"], f, small["g_post_ffn"], "norm_bwd_ple_ffn")
    ds = _mm(df, w["down"], "nt", BF16, "mm_ds")
    gw["down"] = _mm(s, df, "tn", BF16, "mm_dw_down")
    dfgu = _swiglu_bwd(ds, fgu, "swiglu_bwd")
    gw["gu"] = _mm(h2, dfgu, "tn", BF16, "mm_dw_gu")
    dh2 = _mm(dfgu, w["gu"], "nt", F32, "mm_dh2")
    d1, dm2, dg_pre_ffn, dg_post_mix = _norm_bwd(d2, dh2, x1, small["g_pre_ffn"], m2, small["g_post_mix"], "norm_bwd_ffn_mix")
    dmix = _mm(dm2, w["mix"], "nt", F32, "mm_dmix")
    gw["mix"] = _mm(mix, dm2, "tn", BF16, "mm_dw_mix")
    dgab, dya, dyb = _mix_bwd(dmix, proj, ya, yb, "mix_bwd")
    dza = _mm(dya, w["a_out"], "nt", BF16, "mm_dza")
    gw["a_out"] = _mm(za, dya, "tn", BF16, "mm_dw_a_out")
    dzb = _mm(dyb, w["b_out"], "nt", BF16, "mm_dzb")
    gw["b_out"] = _mm(zb, dyb, "tn", BF16, "mm_dw_b_out")
    dax, dab, dac, dconv = _conv_bwd(dza, proj, conv_w, "conv_bwd")
    dq, dk, dv, dog, dalr, dwa, dba, dgain = _gla_bwd(dzb, proj, states, wa_p, b_alpha, gain, "gla_bwd")
    dproj = jnp.concatenate([dgab, dax, dab, dac, dq, dk, dv, dog, dalr.astype(BF16)], axis=1)
    gw["in_p"] = _mm(h1, dproj, "tn", BF16, "mm_dw_in")
    dh1 = _mm(dproj, w["in_p"], "nt", F32, "mm_dh1")
    grad_x, dg_pre_mix = _norm_bwd(d1, dh1, x, small["g_pre_mix"], None, None, "norm_bwd_mix")

    gs = dict(
        conv_w=dconv,
        w_alpha_up=jnp.transpose(dwa[:, :GATE_RANK, :], (1, 0, 2)).reshape(GATE_RANK, HEADS * DK),
        b_alpha_up=dba.reshape(1, HEADS * DK), gla_head_gain=dgain,
        g_pre_mix=dg_pre_mix, g_post_mix=dg_post_mix, g_pre_ffn=dg_pre_ffn, g_post_ffn=dg_post_ffn,
        g_pre_ple=dg_pre_ple, g_post_ple=dg_post_ple,
    )
    return loss_rows, grad_x, gw, gs


ANY = pl.BlockSpec(memory_space=pl.ANY)


def _place():
    x, y, c = lax.axis_index("x"), lax.axis_index("y"), lax.axis_index("c")
    return x, y, c, [(1 - x, y), (x, 1 - y), (1 - x, 1 - y)]


def _all_gather(shards, name):
    n = len(shards)

    def body(*refs):
        ins, outs = refs[:n], refs[n:2 * n]
        send_sems, recv_sems, local_sems = refs[2 * n:]
        x, y, c, chips = _place()
        me, sibling = (x, y, c), (x, y, 1 - c)

        def slot(px, py, pc):
            return 4 * px + 2 * py + pc

        def copy(a, k, block, to, src=None):
            dst = outs[a].at[slot(*block)]
            return pltpu.make_async_remote_copy(src_ref=dst if src is None else src, dst_ref=dst, send_sem=send_sems.at[a, k],
                                                recv_sem=recv_sems.at[a, k], device_id=to, device_id_type=MESH)

        mine = [pltpu.make_async_copy(ins[a], outs[a].at[slot(*me)], local_sems.at[a]) for a in range(n)]
        for cp in mine:
            cp.start()
        first = []
        for j, chip in enumerate(chips):
            first += [copy(a, 1 + j, me, (*chip, c), src=ins[a]) for a in range(n)]
        first += [copy(a, 0, me, sibling, src=ins[a]) for a in range(n)]
        for cp in first:
            cp.start()
        passed = []
        for j, chip in enumerate(chips):
            for a in range(n):
                copy(a, 1 + j, (*chip, c), me).wait_recv()
                cp = copy(a, 4 + j, (*chip, c), sibling)
                cp.start()
                passed.append(cp)
        for a in range(n):
            copy(a, 0, sibling, me).wait_recv()
        for j, chip in enumerate(chips):
            for a in range(n):
                copy(a, 4 + j, (*chip, 1 - c), me).wait_recv()
        for cp in first + passed:
            cp.wait_send()
        for cp in mine:
            cp.wait()

    return pl.pallas_call(
        body, name=name, in_specs=[ANY] * n, out_specs=[ANY] * n,
        out_shape=[jax.ShapeDtypeStruct((N_DEV,) + s.shape, s.dtype) for s in shards],
        scratch_shapes=[pltpu.SemaphoreType.DMA((n, 7)), pltpu.SemaphoreType.DMA((n, 7)), pltpu.SemaphoreType.DMA((n,))],
    )(*shards)


def _sibling_exchange(parts, name):
    n = len(parts)

    def body(*refs):
        ins, outs = refs[:n], refs[n:2 * n]
        send_sems, recv_sems = refs[2 * n:]
        x, y, c, _ = _place()

        def copy(a, ch):
            return pltpu.make_async_remote_copy(src_ref=ins[a].at[2 * ch + 1 - c], dst_ref=outs[a].at[ch], send_sem=send_sems.at[a, ch],
                                                recv_sem=recv_sems.at[a, ch], device_id=(x, y, 1 - c), device_id_type=MESH)

        cps = [copy(a, ch) for ch in range(4) for a in range(n)]
        for cp in cps:
            cp.start()
        for cp in cps:
            cp.wait_recv()
        for cp in cps:
            cp.wait_send()

    return pl.pallas_call(
        body, name=name, in_specs=[ANY] * n, out_specs=[ANY] * n,
        out_shape=[jax.ShapeDtypeStruct((4,) + s.shape[1:], s.dtype) for s in parts],
        scratch_shapes=[pltpu.SemaphoreType.DMA((n, 4)), pltpu.SemaphoreType.DMA((n, 4))],
    )(*parts)


def _chip_exchange(parts, name):
    n = len(parts)

    def body(*refs):
        ins, outs = refs[:n], refs[n:2 * n]
        send_sems, recv_sems, local_sems = refs[2 * n:]
        x, y, c, chips = _place()
        my_chip = 2 * x + y

        def copy(a, j):
            px, py = chips[j]
            return pltpu.make_async_remote_copy(src_ref=ins[a].at[2 * px + py], dst_ref=outs[a].at[my_chip], send_sem=send_sems.at[a, j],
                                                recv_sem=recv_sems.at[a, j], device_id=(px, py, c), device_id_type=MESH)

        def landing(a, j):
            px, py = chips[j]
            return pltpu.make_async_remote_copy(src_ref=ins[a].at[my_chip], dst_ref=outs[a].at[2 * px + py], send_sem=send_sems.at[a, j],
                                                recv_sem=recv_sems.at[a, j], device_id=(px, py, c), device_id_type=MESH)

        mine = [pltpu.make_async_copy(ins[a].at[my_chip], outs[a].at[my_chip], local_sems.at[a]) for a in range(n)]
        for cp in mine:
            cp.start()
        cps = [copy(a, j) for j in range(3) for a in range(n)]
        for cp in cps:
            cp.start()
        for j in range(3):
            for a in range(n):
                landing(a, j).wait_recv()
        for cp in cps:
            cp.wait_send()
        for cp in mine:
            cp.wait()

    return pl.pallas_call(
        body, name=name, in_specs=[ANY] * n, out_specs=[ANY] * n,
        out_shape=[jax.ShapeDtypeStruct(s.shape, s.dtype) for s in parts],
        scratch_shapes=[pltpu.SemaphoreType.DMA((n, 3)), pltpu.SemaphoreType.DMA((n, 3)), pltpu.SemaphoreType.DMA((n,))],
    )(*parts)


def _pair_add(mine8, got4, name):
    _, r, cols = mine8.shape
    tr = _pick(r, (256, 64, 16))
    core = lax.axis_index("c").astype(jnp.int32).reshape(1)

    def body(c_ref, a_ref, b_ref, o_ref):
        o_ref[...] = (a_ref[...].astype(F32) + b_ref[...].astype(F32)).astype(BF16)

    return pl.pallas_call(
        body, name=name,
        grid_spec=pltpu.PrefetchScalarGridSpec(
            num_scalar_prefetch=1, grid=(4, r // tr),
            in_specs=[pl.BlockSpec((None, tr, cols), lambda ch, i, c_ref: (2 * ch + c_ref[0], i, 0)),
                      pl.BlockSpec((None, tr, cols), lambda ch, i, c_ref: (ch, i, 0))],
            out_specs=pl.BlockSpec((None, tr, cols), lambda ch, i, c_ref: (ch, i, 0))),
        out_shape=jax.ShapeDtypeStruct((4, r, cols), BF16),
        compiler_params=_params(("parallel", "parallel")),
    )(core, mine8, got4)


def _sum_everywhere(v, name):
    rows = v.shape[0]

    def body(v_ref, o_ref, buf, send_sems, recv_sems):
        x, y, c, _ = _place()
        me = 4 * x + 2 * y + c
        buf[me] = v_ref[...]
        cps = []
        for k in range(1, N_DEV):
            fx, fy, fc = (k >> 2) & 1, (k >> 1) & 1, k & 1
            to = (1 - x if fx else x, 1 - y if fy else y, 1 - c if fc else c)
            cps.append(pltpu.make_async_remote_copy(src_ref=buf.at[me], dst_ref=buf.at[me], send_sem=send_sems.at[k - 1],
                                                    recv_sem=recv_sems.at[k - 1], device_id=to, device_id_type=MESH))
        for cp in cps:
            cp.start()
        for cp in cps:
            cp.wait_recv()
        for cp in cps:
            cp.wait_send()
        acc = buf[0]
        for d in range(1, N_DEV):
            acc = acc + buf[d]
        o_ref[...] = acc

    vm = pl.BlockSpec(memory_space=pltpu.VMEM)
    return pl.pallas_call(
        body, name=name, in_specs=[vm], out_specs=vm, out_shape=jax.ShapeDtypeStruct(v.shape, F32),
        scratch_shapes=[pltpu.VMEM((N_DEV, rows, 128), F32), pltpu.SemaphoreType.DMA((N_DEV - 1,)), pltpu.SemaphoreType.DMA((N_DEV - 1,))],
    )(v)


def _adamw(w, g, m, v, name):
    r, c = w.shape
    parts = g.ndim == 3
    tr = r if r <= 256 else _pick(r, (128, 64))

    def body(w_ref, g_ref, m_ref, v_ref, go_ref, d_ref, mo_ref, vo_ref):
        if parts:
            gv = g_ref[0].astype(F32)
            for ch in range(1, 4):
                gv = gv + g_ref[ch].astype(F32)
        else:
            gv = g_ref[...]
        mn = B1 * m_ref[...] + (1.0 - B1) * gv
        vn = B2 * v_ref[...] + (1.0 - B2) * (gv * gv)
        m_hat = mn / (1.0 - B1 ** STEP)
        v_hat = vn / (1.0 - B2 ** STEP)
        go_ref[...] = gv
        d_ref[...] = -LR * (m_hat / (jnp.sqrt(v_hat) + ADAM_EPS) + WD * w_ref[...])
        mo_ref[...] = mn
        vo_ref[...] = vn

    tile = pl.BlockSpec((tr, c), lambda i: (i, 0))
    g_spec = pl.BlockSpec((4, tr, c), lambda i: (0, i, 0)) if parts else tile
    out = jax.ShapeDtypeStruct((r, c), F32)
    return pl.pallas_call(
        body, name=name, grid=(r // tr,), in_specs=[tile, g_spec, tile, tile], out_specs=[tile] * 4, out_shape=[out] * 4,
        compiler_params=_params(("parallel",)),
    )(w, g, m, v)


BIG = ["w_in", "w_a_out", "w_b_out", "w_mix_out", "w_ff_gate", "w_ff_up", "w_ff_down", "w_ple_gate", "w_ple_proj"]
COL_SHARDED = ["w_in", "w_a_out", "w_b_out", "w_ff_gate", "w_ff_up", "w_ple_proj"]
SMALL = ["conv_w", "w_alpha_up", "b_alpha_up", "gla_head_gain", "g_pre_mix", "g_post_mix", "g_pre_ffn", "g_post_ffn", "g_pre_ple", "g_post_ple"]
WEIGHTS = ["w_in", "conv_w", "w_a_out", "w_alpha_up", "b_alpha_up", "gla_head_gain", "w_b_out", "w_mix_out", "g_pre_mix", "g_post_mix",
           "g_pre_ffn", "g_post_ffn", "w_ff_gate", "w_ff_up", "w_ff_down", "g_pre_ple", "g_post_ple", "w_ple_gate", "w_ple_proj"]


def _cols_to_full(g8):
    n, r, c = g8.shape
    return jnp.transpose(g8, (1, 0, 2)).reshape(r, n * c)


def _full_to_cols(a):
    r, c = a.shape
    return jnp.transpose(a.reshape(r, N_DEV, c // N_DEV), (1, 0, 2))


def _pack(arrs, rows):
    flat = jnp.concatenate([a.reshape(-1) for a in arrs])
    return jnp.pad(flat, (0, rows * 128 - flat.shape[0])).reshape(rows, 128)


def _unpack(packed, shapes):
    flat, out, o = packed.reshape(-1), [], 0
    for s in shapes:
        size = 1
        for d in s:
            size *= d
        out.append(flat[o:o + size].reshape(s))
        o += size
    return out


def kernel(x, p, w_in, conv_w, w_a_out, w_alpha_up, b_alpha_up, gla_head_gain, w_b_out, w_mix_out, g_pre_mix, g_post_mix, g_pre_ffn, g_post_ffn, w_ff_gate, w_ff_up, w_ff_down, g_pre_ple, g_post_ple, w_ple_gate, w_ple_proj, loss_target, m_w_in, m_conv_w, m_w_a_out, m_w_alpha_up, m_b_alpha_up, m_gla_head_gain, m_w_b_out, m_w_mix_out, m_g_pre_mix, m_g_post_mix, m_g_pre_ffn, m_g_post_ffn, m_w_ff_gate, m_w_ff_up, m_w_ff_down, m_g_pre_ple, m_g_post_ple, m_w_ple_gate, m_w_ple_proj, v_w_in, v_conv_w, v_w_a_out, v_w_alpha_up, v_b_alpha_up, v_gla_head_gain, v_w_b_out, v_w_mix_out, v_g_pre_mix, v_g_post_mix, v_g_pre_ffn, v_g_post_ffn, v_w_ff_gate, v_w_ff_up, v_w_ff_down, v_g_pre_ple, v_g_post_ple, v_w_ple_gate, v_w_ple_proj):
    args = dict(locals())
    wts = {n: args[n][0] for n in WEIGHTS}
    mom = {n: args["m_" + n][0] for n in WEIGHTS}
    var = {n: args["v_" + n][0] for n in WEIGHTS}
    me = 4 * lax.axis_index("x") + 2 * lax.axis_index("y") + lax.axis_index("c")

    gathered = _all_gather([wts[n].astype(BF16) for n in BIG] + [wts["conv_w"], wts["w_alpha_up"]], "all_gather_weights")
    g8 = dict(zip(BIG + ["conv_w", "w_alpha_up"], gathered))
    w_in_full = _cols_to_full(g8["w_in"])
    full = dict(
        in_p=jnp.concatenate([w_in_full[:, R_GA:R_END], w_in_full[:, :R_ALR], w_in_full[:, R_ALR:R_GA],
                              jnp.zeros((w_in_full.shape[0], 128 - GATE_RANK), BF16)], axis=1),
        a_out=_cols_to_full(g8["w_a_out"]), b_out=_cols_to_full(g8["w_b_out"]),
        mix=g8["w_mix_out"].reshape(-1, g8["w_mix_out"].shape[-1]),
        gu=jnp.concatenate([_cols_to_full(g8["w_ff_gate"]), _cols_to_full(g8["w_ff_up"])], axis=1),
        down=g8["w_ff_down"].reshape(-1, g8["w_ff_down"].shape[-1]),
        pg=g8["w_ple_gate"].reshape(-1, g8["w_ple_gate"].shape[-1]), pp=_cols_to_full(g8["w_ple_proj"]),
    )
    small = dict(conv_w=_cols_to_full(g8["conv_w"]), w_alpha_up=_cols_to_full(g8["w_alpha_up"]))
    for n in SMALL[2:]:
        small[n] = wts[n].reshape(1, -1)

    loss_rows, grad_x, gw, gs = _local_step(x[0], p[0, 0], loss_target[0], full, small)
    loss = lax.psum(jnp.sum(loss_rows), ("x", "y", "c"))

    g_in = gw["in_p"]
    f_gate = gw["gu"].shape[1] // 2
    gfull = dict(
        w_in=jnp.concatenate([g_in[:, C_AX:C_ALR], g_in[:, C_ALR:C_ALR + GATE_RANK], g_in[:, :C_AX]], axis=1),
        w_a_out=gw["a_out"], w_b_out=gw["b_out"], w_mix_out=gw["mix"], w_ff_gate=gw["gu"][:, :f_gate], w_ff_up=gw["gu"][:, f_gate:],
        w_ff_down=gw["down"], w_ple_gate=gw["pg"], w_ple_proj=gw["pp"],
    )
    parts8 = [_full_to_cols(gfull[n]) if n in COL_SHARDED else gfull[n].reshape(N_DEV, -1, gfull[n].shape[-1]) for n in BIG]
    from_sibling = _sibling_exchange(parts8, "reduce_scatter_sibling")
    chip_sums = [_pair_add(a, b, "reduce_scatter_add_" + n) for n, a, b in zip(BIG, parts8, from_sibling)]
    parts4 = dict(zip(BIG, _chip_exchange(chip_sums, "reduce_scatter_chips")))

    small_shapes = [gs[n].shape for n in SMALL]
    gsum = dict(zip(SMALL, _unpack(_sum_everywhere(_pack([gs[n] for n in SMALL], 192), "all_reduce_small"), small_shapes)))
    gsum["conv_w"] = lax.dynamic_index_in_dim(gsum["conv_w"].reshape(3, N_DEV, -1), me, axis=1, keepdims=False)
    gsum["w_alpha_up"] = lax.dynamic_index_in_dim(gsum["w_alpha_up"].reshape(GATE_RANK, N_DEV, -1), me, axis=1, keepdims=False)

    res = {n: _adamw(wts[n], parts4[n], mom[n], var[n], "adamw_" + n) for n in BIG}
    shard_shapes = [wts[n].shape for n in SMALL]
    packed = [_pack([d[n] for n in SMALL], 120) for d in (wts, gsum, mom, var)]
    outs = [_unpack(o, shard_shapes) for o in _adamw(*packed, "adamw_small")]
    for i, n in enumerate(SMALL):
        res[n] = [o[i] for o in outs]

    lead = lambda a: a[None]
    return (loss, grad_x[None], *[lead(res[n][0]) for n in WEIGHTS], *[lead(res[n][1]) for n in WEIGHTS],
            *[lead(res[n][2]) for n in WEIGHTS], *[lead(res[n][3]) for n in WEIGHTS])
```

```python
import jax
import jax.numpy as jnp
from jax import lax
from jax.experimental import pallas as pl
from jax.experimental.pallas import tpu as pltpu
from jax.experimental.pallas import tpu_sc as plsc

F32, BF16 = jnp.float32, jnp.bfloat16
EPS = 1e-6
CHUNK = 64
HEADS, DK, DV = 4, 128, 256
GATE_RANK = 16
TAU = 16.0
LR, B1, B2, ADAM_EPS, WD, STEP = 0.001, 0.9, 0.999, 1e-08, 0.01, 10
N_DEV = 8
MESH = pl.DeviceIdType.MESH
VMEM_LIMIT = 56 * 1024 * 1024
ANY = pl.BlockSpec(memory_space=pl.ANY)

C_GA, C_GB, C_AX, C_AB, C_AC, C_Q, C_K, C_V, C_OG, C_ALR = 0, 2048, 4096, 5120, 6144, 7168, 7680, 8192, 9216, 10240
IN_PAD = 10368
R_AX, R_AB, R_AC, R_Q, R_K, R_V, R_OG, R_ALR, R_GA, R_GB, R_END = 0, 1024, 2048, 3072, 3584, 4096, 5120, 6144, 6160, 8208, 10256


def _params(sem):
    return pltpu.CompilerParams(dimension_semantics=sem, vmem_limit_bytes=VMEM_LIMIT)


def _pick(n, cands):
    for c in cands:
        if n % c == 0:
            return c
    return n


def _mm(a, b, mode, out_dtype, name, after=()):
    if mode == "nn":
        (m, k), (k2, n) = a.shape, b.shape
    elif mode == "nt":
        (m, k), (n, k2) = a.shape, b.shape
    else:
        (k, m), (k2, n) = a.shape, b.shape
    assert k == k2 and a.dtype == BF16 and b.dtype == BF16, (name, a.shape, b.shape, a.dtype, b.dtype)
    tm = _pick(m, (1024, 512, 256))
    tn = _pick(n, (1152, 1024, 1408, 512, 256))
    tk = _pick(k, (512, 1152, 256))
    nk = k // tk
    dims = {"nn": (((1,), (0,)), ((), ())), "nt": (((1,), (1,)), ((), ())), "tn": (((0,), (0,)), ((), ()))}[mode]

    def body(a_ref, b_ref, *rest):
        o_ref, acc_ref = rest[len(after):]
        kk = pl.program_id(2)

        @pl.when(kk == 0)
        def _():
            acc_ref[...] = jnp.zeros_like(acc_ref)

        acc_ref[...] += lax.dot_general(a_ref[...], b_ref[...], dims, preferred_element_type=F32)

        @pl.when(kk == nk - 1)
        def _():
            o_ref[...] = acc_ref[...].astype(o_ref.dtype)

    a_spec = pl.BlockSpec((tk, tm), lambda i, j, kk: (kk, i)) if mode == "tn" else pl.BlockSpec((tm, tk), lambda i, j, kk: (i, kk))
    b_spec = pl.BlockSpec((tn, tk), lambda i, j, kk: (j, kk)) if mode == "nt" else pl.BlockSpec((tk, tn), lambda i, j, kk: (kk, j))
    return pl.pallas_call(
        body, name=name, grid=(m // tm, n // tn, nk),
        in_specs=[a_spec, b_spec] + [ANY] * len(after), out_specs=pl.BlockSpec((tm, tn), lambda i, j, kk: (i, j)),
        out_shape=jax.ShapeDtypeStruct((m, n), out_dtype),
        scratch_shapes=[pltpu.VMEM((tm, tn), F32)],
        compiler_params=_params(("parallel", "parallel", "arbitrary")),
    )(a, b, *after)


def _rows(body, t, tr, ins, outs, name):
    in_specs = []
    for arr, sp in ins:
        if sp[0] == "t":
            in_specs.append(pl.BlockSpec((tr, sp[1]), lambda i, cb=sp[2]: (i, cb)))
        else:
            in_specs.append(pl.BlockSpec(arr.shape, lambda i, nd=arr.ndim: (0,) * nd))
    out_specs, out_shape = [], []
    for shape, dt, kind in outs:
        out_specs.append(pl.BlockSpec((tr, shape[1]), lambda i: (i, 0)) if kind == "t" else pl.BlockSpec(shape, lambda i: (0, 0)))
        out_shape.append(jax.ShapeDtypeStruct(shape, dt))
    return pl.pallas_call(
        body, name=name, grid=(t // tr,), in_specs=in_specs, out_specs=out_specs, out_shape=out_shape,
        compiler_params=_params(("arbitrary",)),
    )(*[arr for arr, _ in ins])


def _rinv(v):
    return lax.rsqrt(jnp.mean(v * v, axis=-1, keepdims=True) + EPS)


def _sig(v):
    return 1.0 / (1.0 + jnp.exp(-v))


def _acc(ref, val):
    @pl.when(pl.program_id(0) == 0)
    def _():
        ref[...] = jnp.zeros_like(ref)

    ref[...] += jnp.sum(val, axis=0, keepdims=True)


def _rms_fwd(x, g, name):
    t, d = x.shape

    def body(x_ref, g_ref, h_ref):
        xv = x_ref[...]
        h_ref[...] = (xv * _rinv(xv) * g_ref[...]).astype(BF16)

    return _rows(body, t, 256, [(x, ("t", d, 0)), (g, ("b",))], [((t, d), BF16, "t")], name)[0]


def _post_pre(x, m, g_post, g_pre, name):
    t, d = x.shape

    def body(x_ref, m_ref, gp_ref, gn_ref, xo_ref, h_ref):
        mv = m_ref[...]
        xn = x_ref[...] + mv * _rinv(mv) * gp_ref[...]
        xo_ref[...] = xn
        h_ref[...] = (xn * _rinv(xn) * gn_ref[...]).astype(BF16)

    return _rows(body, t, 128, [(x, ("t", d, 0)), (m, ("t", d, 0)), (g_post, ("b",)), (g_pre, ("b",))],
                 [((t, d), F32, "t"), ((t, d), BF16, "t")], name)


def _mix_fwd(proj, ya, yb, name):
    t, d = ya.shape

    def body(ga_ref, gb_ref, ya_ref, yb_ref, o_ref):
        o_ref[...] = (_sig(ga_ref[...].astype(F32)) * ya_ref[...].astype(F32)
                      + _sig(gb_ref[...].astype(F32)) * yb_ref[...].astype(F32)).astype(BF16)

    return _rows(body, t, 256, [(proj, ("t", d, C_GA // d)), (proj, ("t", d, C_GB // d)), (ya, ("t", d, 0)), (yb, ("t", d, 0))],
                 [((t, d), BF16, "t")], name)[0]


def _mix_bwd(dmix, proj, ya, yb, name):
    t, d = ya.shape

    def body(dm_ref, ga_ref, gb_ref, ya_ref, yb_ref, dg_ref, dya_ref, dyb_ref):
        dm = dm_ref[...]
        sa, sb = _sig(ga_ref[...].astype(F32)), _sig(gb_ref[...].astype(F32))
        dg_ref[:, :d] = (dm * ya_ref[...].astype(F32) * sa * (1.0 - sa)).astype(BF16)
        dg_ref[:, d:] = (dm * yb_ref[...].astype(F32) * sb * (1.0 - sb)).astype(BF16)
        dya_ref[...] = (dm * sa).astype(BF16)
        dyb_ref[...] = (dm * sb).astype(BF16)

    return _rows(body, t, 128,
                 [(dmix, ("t", d, 0)), (proj, ("t", d, C_GA // d)), (proj, ("t", d, C_GB // d)), (ya, ("t", d, 0)), (yb, ("t", d, 0))],
                 [((t, 2 * d), BF16, "t"), ((t, d), BF16, "t"), ((t, d), BF16, "t")], name)


def _swiglu_fwd(fgu, name):
    t, f2 = fgu.shape
    f = f2 // 2
    tc = _pick(f, (1408, 512))
    nc = f // tc

    def body(g_ref, u_ref, s_ref):
        gv = g_ref[...].astype(F32)
        s_ref[...] = (gv * _sig(gv) * u_ref[...].astype(F32)).astype(BF16)

    return pl.pallas_call(
        body, name=name, grid=(t // 512, nc),
        in_specs=[pl.BlockSpec((512, tc), lambda i, j: (i, j)), pl.BlockSpec((512, tc), lambda i, j: (i, j + nc))],
        out_specs=pl.BlockSpec((512, tc), lambda i, j: (i, j)), out_shape=jax.ShapeDtypeStruct((t, f), BF16),
        compiler_params=_params(("parallel", "parallel")),
    )(fgu, fgu)


def _swiglu_bwd(ds, fgu, name):
    t, f2 = fgu.shape
    f = f2 // 2
    tc = _pick(f, (1408, 512))
    nc = f // tc

    def body(ds_ref, g_ref, u_ref, o_ref):
        dsv, gv, uv = ds_ref[...].astype(F32), g_ref[...].astype(F32), u_ref[...].astype(F32)
        sg = _sig(gv)
        dfg = dsv * uv * sg * (1.0 + gv * (1.0 - sg))
        dfu = dsv * gv * sg
        o_ref[...] = jnp.where(pl.program_id(1) < nc, dfg, dfu).astype(BF16)

    return pl.pallas_call(
        body, name=name, grid=(t // 512, 2 * nc),
        in_specs=[pl.BlockSpec((512, tc), lambda i, j: (i, j % nc)), pl.BlockSpec((512, tc), lambda i, j: (i, j % nc)),
                  pl.BlockSpec((512, tc), lambda i, j: (i, j % nc + nc))],
        out_specs=pl.BlockSpec((512, tc), lambda i, j: (i, j)), out_shape=jax.ShapeDtypeStruct((t, f2), BF16),
        compiler_params=_params(("parallel", "parallel")),
    )(ds, fgu, fgu)


def _ple_final(x2, pg, pp, tgt, g_post, name):
    t, d = x2.shape

    def body(x_ref, pg_ref, pp_ref, t_ref, g_ref, loss_ref, d3_ref, dpg_ref, dpp_ref, dg_ref):
        sg, ppv, g = _sig(pg_ref[...]), pp_ref[...], g_ref[...]
        e = sg * ppv
        r = _rinv(e)
        eh = e * r
        diff = x_ref[...] + eh * g - t_ref[...]
        loss_ref[...] = 0.5 * jnp.mean(diff * diff, axis=-1, keepdims=True)
        d3 = diff * (1.0 / d)
        d3_ref[...] = d3
        gd = d3 * g
        de = r * (gd - eh * jnp.mean(gd * eh, axis=-1, keepdims=True))
        dpg_ref[...] = (de * ppv * sg * (1.0 - sg)).astype(BF16)
        dpp_ref[...] = (de * sg).astype(BF16)
        _acc(dg_ref, d3 * eh)

    return _rows(body, t, 128, [(x2, ("t", d, 0)), (pg, ("t", d, 0)), (pp, ("t", d, 0)), (tgt, ("t", d, 0)), (g_post, ("b",))],
                 [((t, 1), F32, "t"), ((t, d), F32, "t"), ((t, d), BF16, "t"), ((t, d), BF16, "t"), ((1, d), F32, "a")], name)


def _norm_bwd(dn, dh, x, g_pre, fm, g_post, name):
    t, d = x.shape
    two = fm is not None

    def body(*refs):
        if two:
            dn_ref, dh_ref, x_ref, gp_ref, f_ref, gq_ref, dx_ref, df_ref, dgp_ref, dgq_ref = refs
        else:
            dn_ref, dh_ref, x_ref, gp_ref, dx_ref, dgp_ref = refs
        xv, dhv = x_ref[...], dh_ref[...]
        r = _rinv(xv)
        xh = xv * r
        gd = dhv * gp_ref[...]
        dx = dn_ref[...] + r * (gd - xh * jnp.mean(gd * xh, axis=-1, keepdims=True))
        dx_ref[...] = dx
        _acc(dgp_ref, dhv * xh)
        if two:
            fv = f_ref[...]
            rf = _rinv(fv)
            fh = fv * rf
            gd2 = dx * gq_ref[...]
            df_ref[...] = (rf * (gd2 - fh * jnp.mean(gd2 * fh, axis=-1, keepdims=True))).astype(BF16)
            _acc(dgq_ref, dx * fh)

    ins = [(dn, ("t", d, 0)), (dh, ("t", d, 0)), (x, ("t", d, 0)), (g_pre, ("b",))]
    outs = [((t, d), F32, "t")]
    if two:
        ins += [(fm, ("t", d, 0)), (g_post, ("b",))]
        outs += [((t, d), BF16, "t"), ((1, d), F32, "a"), ((1, d), F32, "a")]
    else:
        outs += [((1, d), F32, "a")]
    return _rows(body, t, 128, ins, outs, name)


CONV_TC = 256


def _shift_down(v, s):
    rows = lax.broadcasted_iota(jnp.int32, v.shape, 0)
    return jnp.where(rows >= s, pltpu.roll(v, s, 0), 0.0)


def _shift_up(v, s):
    n = v.shape[0]
    rows = lax.broadcasted_iota(jnp.int32, v.shape, 0)
    return jnp.where(rows < n - s, pltpu.roll(v, n - s, 0), 0.0)


def _conv_specs(t):
    nb = 1024 // CONV_TC
    seg = lambda c0: pl.BlockSpec((t, CONV_TC), lambda j, cb=c0 // CONV_TC: (0, cb + j))
    own = pl.BlockSpec((t, CONV_TC), lambda j: (0, j))
    wspec = pl.BlockSpec((3, CONV_TC), lambda j: (0, j))
    return nb, seg, own, wspec


def _conv_fwd(proj, conv_w, name, after=()):
    t = proj.shape[0]
    nb, seg, own, wspec = _conv_specs(t)

    def body(ax_ref, ab_ref, ac_ref, w_ref, *rest):
        za_ref = rest[len(after)]
        u = ac_ref[...].astype(F32) * ax_ref[...].astype(F32)
        w = w_ref[...]
        yc = w[0:1] * _shift_down(u, 2) + w[1:2] * _shift_down(u, 1) + w[2:3] * u
        za_ref[...] = (ab_ref[...].astype(F32) * yc).astype(BF16)

    return pl.pallas_call(
        body, name=name, grid=(nb,), in_specs=[seg(C_AX), seg(C_AB), seg(C_AC), wspec] + [ANY] * len(after), out_specs=own,
        out_shape=jax.ShapeDtypeStruct((t, 1024), BF16), compiler_params=_params(("parallel",)),
    )(proj, proj, proj, conv_w, *after)


def _conv_bwd(dza, proj, conv_w, name):
    t = proj.shape[0]
    nb, seg, own, wspec = _conv_specs(t)

    def body(dz_ref, ax_ref, ab_ref, ac_ref, w_ref, dax_ref, dab_ref, dac_ref, dw_ref):
        ax, ab, ac, dz = ax_ref[...].astype(F32), ab_ref[...].astype(F32), ac_ref[...].astype(F32), dz_ref[...].astype(F32)
        w = w_ref[...]
        u = ac * ax
        u1, u2 = _shift_down(u, 1), _shift_down(u, 2)
        yc = w[0:1] * u2 + w[1:2] * u1 + w[2:3] * u
        dab_ref[...] = (dz * yc).astype(BF16)
        dyc = dz * ab
        du = w[2:3] * dyc + w[1:2] * _shift_up(dyc, 1) + w[0:1] * _shift_up(dyc, 2)
        dax_ref[...] = (du * ac).astype(BF16)
        dac_ref[...] = (du * ax).astype(BF16)
        dw_ref[0:1, :] = jnp.sum(dyc * u2, axis=0, keepdims=True)
        dw_ref[1:2, :] = jnp.sum(dyc * u1, axis=0, keepdims=True)
        dw_ref[2:3, :] = jnp.sum(dyc * u, axis=0, keepdims=True)

    act = jax.ShapeDtypeStruct((t, 1024), BF16)
    return pl.pallas_call(
        body, name=name, grid=(nb,), in_specs=[own, seg(C_AX), seg(C_AB), seg(C_AC), wspec], out_specs=[own, own, own, wspec],
        out_shape=[act, act, act, jax.ShapeDtypeStruct((3, 1024), F32)], compiler_params=_params(("parallel",)),
    )(dza, proj, proj, proj, conv_w)


def _dot(a, b, dims, precision=None):
    return lax.dot_general(a, b, (dims, ((), ())), precision=precision, preferred_element_type=F32)


def _gla_chunk(q, k, v, og, alr, s_in, wa, ba, gain):
    c = q.shape[0]
    hi = lax.Precision.HIGHEST
    z = _dot(alr, wa, ((1,), (0,))) + ba
    la = (jnp.minimum(z, 0.0) - jnp.log(1.0 + jnp.exp(-jnp.abs(z)))) * (1.0 / TAU)
    row = lax.broadcasted_iota(jnp.int32, (c, c), 0)
    col = lax.broadcasted_iota(jnp.int32, (c, c), 1)
    lower = row >= col
    b = _dot(lower.astype(F32), la, ((1,), (0,)), hi)
    trow = lax.broadcasted_iota(jnp.int32, la.shape, 0)
    mid = jnp.sum(jnp.where(trow <= c // 2, la, 0.0), axis=0, keepdims=True)
    blast = jnp.sum(la, axis=0, keepdims=True)
    qs = q * (DK ** -0.5)
    e_up, e_dn = jnp.exp(b - mid), jnp.exp(mid - b)
    a_fwd = _dot(qs * e_up, k * e_dn, ((1,), (1,)))
    a_rev = _dot(qs * e_dn, k * e_up, ((1,), (1,)))
    att = jnp.where(lower, a_fwd, a_rev)
    o = _dot(att, v, ((1,), (0,))) + _dot(qs * jnp.exp(b), s_in, ((1,), (0,)))
    upd = _dot(k * jnp.exp(blast - b), v, ((0,), (0,)))
    blast_col = _dot(la, jnp.ones((c, DV), F32), ((0,), (0,)), hi)
    s_out = jnp.exp(blast_col) * s_in + upd
    on = o * _rinv(o) * gain
    return on * og * _sig(og), s_out


def _gla_specs(t, rev):
    n = t // CHUNK
    ch = (lambda i: n - 1 - i) if rev else (lambda i: i)
    col = lambda w, c0: pl.BlockSpec((CHUNK, w), lambda i, h, cb=c0 // w: (ch(i), cb + h))
    specs = dict(
        q=col(DK, C_Q), k=col(DK, C_K), v=col(DV, C_V), og=col(DV, C_OG),
        alr=pl.BlockSpec((CHUNK, 128), lambda i, h: (ch(i), C_ALR // 128)),
        wa=pl.BlockSpec((128, DK), lambda i, h: (0, h)), ba=pl.BlockSpec((1, DK), lambda i, h: (0, h)),
        gain=pl.BlockSpec((1, DV), lambda i, h: (0, 0)),
        state=pl.BlockSpec((None, None, DK, DV), lambda i, h: (ch(i), h, 0, 0)),
        odk=pl.BlockSpec((CHUNK, DK), lambda i, h: (ch(i), h)), odv=pl.BlockSpec((CHUNK, DV), lambda i, h: (ch(i), h)),
        oalr=pl.BlockSpec((CHUNK, 128), lambda i, h: (ch(i), 0)),
    )
    return n, specs


def _gla_fwd(proj, wa, ba, gain, name):
    t = proj.shape[0]
    n, sp = _gla_specs(t, False)

    def body(q_ref, k_ref, v_ref, og_ref, alr_ref, wa_ref, ba_ref, g_ref, zb_ref, st_ref, s_scr):
        h = pl.program_id(1)

        @pl.when(pl.program_id(0) == 0)
        def _():
            s_scr[h] = jnp.zeros((DK, DV), F32)

        s_in = s_scr[h]
        st_ref[...] = s_in
        zb, s_out = _gla_chunk(q_ref[...].astype(F32), k_ref[...].astype(F32), v_ref[...].astype(F32), og_ref[...].astype(F32),
                               alr_ref[...].astype(F32), s_in, wa_ref[...].astype(F32), ba_ref[...], g_ref[...])
        zb_ref[...] = zb.astype(BF16)
        s_scr[h] = s_out

    return pl.pallas_call(
        body, name=name, grid=(n, HEADS),
        in_specs=[sp["q"], sp["k"], sp["v"], sp["og"], sp["alr"], sp["wa"], sp["ba"], sp["gain"]],
        out_specs=[sp["odv"], sp["state"]],
        out_shape=[jax.ShapeDtypeStruct((t, HEADS * DV), BF16), jax.ShapeDtypeStruct((n, HEADS, DK, DV), F32)],
        scratch_shapes=[pltpu.VMEM((HEADS, DK, DV), F32)],
        compiler_params=_params(("arbitrary", "arbitrary")),
    )(proj, proj, proj, proj, proj, wa, ba, gain)


def _gla_bwd(dzb, proj, states, wa, ba, gain, name):
    t = proj.shape[0]
    n, sp = _gla_specs(t, True)

    def body(dz_ref, q_ref, k_ref, v_ref, og_ref, alr_ref, st_ref, wa_ref, ba_ref, g_ref,
             dq_ref, dk_ref, dv_ref, dog_ref, dalr_ref, dwa_ref, dba_ref, dg_ref, ds_scr):
        i, h = pl.program_id(0), pl.program_id(1)

        @pl.when(i == 0)
        def _():
            ds_scr[h] = jnp.zeros((DK, DV), F32)
            dwa_ref[h] = jnp.zeros((128, DK), F32)
            dba_ref[h] = jnp.zeros((1, DK), F32)

        @pl.when((i == 0) & (h == 0))
        def _():
            dg_ref[...] = jnp.zeros_like(dg_ref)

        args = (q_ref[...].astype(F32), k_ref[...].astype(F32), v_ref[...].astype(F32), og_ref[...].astype(F32),
                alr_ref[...].astype(F32), st_ref[...], wa_ref[...].astype(F32), ba_ref[...], g_ref[...])
        _, vjp = jax.vjp(_gla_chunk, *args)
        dq, dk, dv, dog, dalr, ds_in, dwa, dba, dgain = vjp((dz_ref[...].astype(F32), ds_scr[h]))
        dq_ref[...] = dq.astype(BF16)
        dk_ref[...] = dk.astype(BF16)
        dv_ref[...] = dv.astype(BF16)
        dog_ref[...] = dog.astype(BF16)

        @pl.when(h == 0)
        def _():
            dalr_ref[...] = dalr

        @pl.when(h != 0)
        def _():
            dalr_ref[...] += dalr

        ds_scr[h] = ds_in
        dwa_ref[h] += dwa
        dba_ref[h] += dba
        dg_ref[...] += dgain

    whole = lambda shape: pl.BlockSpec(shape, lambda i, h, nd=len(shape): (0,) * nd)
    return pl.pallas_call(
        body, name=name, grid=(n, HEADS),
        in_specs=[sp["odv"], sp["q"], sp["k"], sp["v"], sp["og"], sp["alr"], sp["state"], sp["wa"], sp["ba"], sp["gain"]],
        out_specs=[sp["odk"], sp["odk"], sp["odv"], sp["odv"], sp["oalr"], whole((HEADS, 128, DK)), whole((HEADS, 1, DK)), whole((1, DV))],
        out_shape=[jax.ShapeDtypeStruct((t, HEADS * DK), BF16), jax.ShapeDtypeStruct((t, HEADS * DK), BF16),
                   jax.ShapeDtypeStruct((t, HEADS * DV), BF16), jax.ShapeDtypeStruct((t, HEADS * DV), BF16),
                   jax.ShapeDtypeStruct((t, 128), F32), jax.ShapeDtypeStruct((HEADS, 128, DK), F32),
                   jax.ShapeDtypeStruct((HEADS, 1, DK), F32), jax.ShapeDtypeStruct((1, DV), F32)],
        scratch_shapes=[pltpu.VMEM((HEADS, DK, DV), F32)],
        compiler_params=_params(("arbitrary", "arbitrary")),
    )(dzb, proj, proj, proj, proj, proj, states, wa, ba, gain)


def _local_step(x, p, tgt, gather_start, gather_finish, scatter_start, small):
    b_alpha, gain = small["b_alpha_up"], small["gla_head_gain"]
    gather_start(0, ())
    w = dict(gather_finish(0, ()))
    conv_w, w_alpha = w["conv_w"], w["w_alpha_up"]
    wa_p = jnp.zeros((128, HEADS * DK), BF16).at[:GATE_RANK].set(w_alpha.astype(BF16))

    t1 = gather_start(1, (w["in_p"],))
    h1 = _rms_fwd(x, small["g_pre_mix"], "rms_pre_mix")
    proj = _mm(h1, w["in_p"], "nn", BF16, "mm_proj", after=(t1,))
    t2 = gather_start(2, (proj,))
    za = _conv_fwd(proj, conv_w, "conv_fwd", after=(t2,))
    zb, states = _gla_fwd(proj, wa_p, b_alpha, gain, "gla_fwd")
    t3 = gather_start(3, (zb, za))
    w.update(gather_finish(1, (t3,)))
    ya = _mm(za, w["a_out"], "nn", BF16, "mm_ya")
    yb = _mm(zb, w["b_out"], "nn", BF16, "mm_yb")
    mix = _mix_fwd(proj, ya, yb, "mix_fwd")
    m2 = _mm(mix, w["mix"], "nn", F32, "mm_mix")
    t4 = gather_start(4, (m2,))
    x1, h2 = _post_pre(x, m2, small["g_post_mix"], small["g_pre_ffn"], "norm_mix_ffn")
    w.update(gather_finish(2, (h2, t4)))
    fgu = _mm(h2, w["gu"], "nn", BF16, "mm_gu")
    s = _swiglu_fwd(fgu, "swiglu_fwd")
    w.update(gather_finish(3, (s,)))
    f = _mm(s, w["down"], "nn", F32, "mm_down")
    x2, h3 = _post_pre(x1, f, small["g_post_ffn"], small["g_pre_ple"], "norm_ffn_ple")
    w.update(gather_finish(4, (h3,)))
    pg = _mm(h3, w["pg"], "nn", F32, "mm_pg")
    p_bf = p.astype(BF16)
    pp = _mm(p_bf, w["pp"], "nn", F32, "mm_pp")
    loss_rows, d3, dpg, dpp, dg_post_ple = _ple_final(x2, pg, pp, tgt, small["g_post_ple"], "ple_final")

    gw = {}
    gw["pp"] = _mm(p_bf, dpp, "tn", BF16, "mm_dw_pp")
    gw["pg"] = _mm(h3, dpg, "tn", BF16, "mm_dw_pg")
    dh3 = _mm(dpg, w["pg"], "nt", F32, "mm_dh3", after=(scatter_start(0, gw),))
    d2, df, dg_pre_ple, dg_post_ffn = _norm_bwd(d3, dh3, x2, small["g_pre_ple"], f, small["g_post_ffn"], "norm_bwd_ple_ffn")
    ds = _mm(df, w["down"], "nt", BF16, "mm_ds")
    gw["down"] = _mm(s, df, "tn", BF16, "mm_dw_down")
    dfgu = _swiglu_bwd(ds, fgu, "swiglu_bwd")
    gw["gu"] = _mm(h2, dfgu, "tn", BF16, "mm_dw_gu", after=(gw["down"],))
    dh2 = _mm(dfgu, w["gu"], "nt", F32, "mm_dh2", after=(scatter_start(1, gw),))
    d1, dm2, dg_pre_ffn, dg_post_mix = _norm_bwd(d2, dh2, x1, small["g_pre_ffn"], m2, small["g_post_mix"], "norm_bwd_ffn_mix")
    dmix = _mm(dm2, w["mix"], "nt", F32, "mm_dmix")
    gw["mix"] = _mm(mix, dm2, "tn", BF16, "mm_dw_mix")
    dgab, dya, dyb = _mix_bwd(dmix, proj, ya, yb, "mix_bwd")
    dza = _mm(dya, w["a_out"], "nt", BF16, "mm_dza", after=(gw["mix"],))
    gw["a_out"] = _mm(za, dya, "tn", BF16, "mm_dw_a_out")
    gw["b_out"] = _mm(zb, dyb, "tn", BF16, "mm_dw_b_out", after=(gw["a_out"],))
    dzb = _mm(dyb, w["b_out"], "nt", BF16, "mm_dzb", after=(scatter_start(2, gw),))
    dax, dab, dac, dconv = _conv_bwd(dza, proj, conv_w, "conv_bwd")
    dq, dk, dv, dog, dalr, dwa, dba, dgain = _gla_bwd(dzb, proj, states, wa_p, b_alpha, gain, "gla_bwd")
    dproj = jnp.concatenate([dgab, dax, dab, dac, dq, dk, dv, dog, dalr.astype(BF16)], axis=1)
    gw["in_p"] = _mm(h1, dproj, "tn", BF16, "mm_dw_in")
    dh1 = _mm(dproj, w["in_p"], "nt", F32, "mm_dh1", after=(scatter_start(3, gw),))
    grad_x, dg_pre_mix = _norm_bwd(d1, dh1, x, small["g_pre_mix"], None, None, "norm_bwd_mix")

    gs = dict(
        conv_w=dconv,
        w_alpha_up=jnp.transpose(dwa[:, :GATE_RANK, :], (1, 0, 2)).reshape(GATE_RANK, HEADS * DK),
        b_alpha_up=dba.reshape(1, HEADS * DK), gla_head_gain=dgain,
        g_pre_mix=dg_pre_mix, g_post_mix=dg_post_mix, g_pre_ffn=dg_pre_ffn, g_post_ffn=dg_post_ffn,
        g_pre_ple=dg_pre_ple, g_post_ple=dg_post_ple,
    )
    return loss_rows, grad_x, gs


def _place():
    x, y, c = lax.axis_index("x"), lax.axis_index("y"), lax.axis_index("c")
    return x, y, c, [(1 - x, y), (x, 1 - y), (1 - x, 1 - y)]


def _all_gather(shards, name, cid=None):
    n = len(shards)

    def body(*refs):
        ins, outs = refs[:n], refs[n:2 * n]
        send_sems, recv_sems, local_sems = refs[2 * n:]
        x, y, c, chips = _place()
        me, sibling = (x, y, c), (x, y, 1 - c)

        def slot(px, py, pc):
            return 4 * px + 2 * py + pc

        def copy(a, k, block, to, src=None):
            dst = outs[a].at[slot(*block)]
            return pltpu.make_async_remote_copy(src_ref=dst if src is None else src, dst_ref=dst, send_sem=send_sems.at[a, k],
                                                recv_sem=recv_sems.at[a, k], device_id=to, device_id_type=MESH)

        mine = [pltpu.make_async_copy(ins[a], outs[a].at[slot(*me)], local_sems.at[a]) for a in range(n)]
        for cp in mine:
            cp.start()
        first = []
        for j, chip in enumerate(chips):
            first += [copy(a, 1 + j, me, (*chip, c), src=ins[a]) for a in range(n)]
        first += [copy(a, 0, me, sibling, src=ins[a]) for a in range(n)]
        for cp in first:
            cp.start()
        passed = []
        for j, chip in enumerate(chips):
            for a in range(n):
                copy(a, 1 + j, (*chip, c), me).wait_recv()
                cp = copy(a, 4 + j, (*chip, c), sibling)
                cp.start()
                passed.append(cp)
        for a in range(n):
            copy(a, 0, sibling, me).wait_recv()
        for j, chip in enumerate(chips):
            for a in range(n):
                copy(a, 4 + j, (*chip, 1 - c), me).wait_recv()
        for cp in first + passed:
            cp.wait_send()
        for cp in mine:
            cp.wait()

    if cid is None:
        return pl.pallas_call(
            body, name=name, in_specs=[ANY] * n, out_specs=[ANY] * n,
            out_shape=[jax.ShapeDtypeStruct((N_DEV,) + s.shape, s.dtype) for s in shards],
            scratch_shapes=[pltpu.SemaphoreType.DMA((n, 7)), pltpu.SemaphoreType.DMA((n, 7)), pltpu.SemaphoreType.DMA((n,))],
        )(*shards)

    src = [jax.new_ref(s, memory_space=pltpu.MemorySpace.HBM) for s in shards]
    dst = [jax.empty_ref(jax.ShapeDtypeStruct((N_DEV,) + s.shape, s.dtype), memory_space=pltpu.MemorySpace.HBM) for s in shards]

    @pl.kernel(mesh=plsc.ScalarSubcoreMesh(axis_name="seq", num_cores=1), name=name,
               scratch_types=(pltpu.SemaphoreType.DMA((n, 7)), pltpu.SemaphoreType.DMA((n, 7)), pltpu.SemaphoreType.DMA((n,))),
               compiler_params=pltpu.CompilerParams(collective_id=cid))
    def launch(send_sems, recv_sems, local_sems):
        x, y, c, chips = _place()
        barrier = pltpu.get_barrier_semaphore()
        for peer in [(x, y, 1 - c)] + [(*chip, c) for chip in chips]:
            pl.semaphore_signal(barrier, inc=1, device_id=peer, device_id_type=MESH)
        pl.semaphore_wait(barrier, 4)
        body(*src, *dst, send_sems, recv_sems, local_sems)

    launch()
    return [r[...] for r in dst]


def _reduce_scatter(parts, name, cid):
    n = len(parts)
    src = [jax.new_ref(s, memory_space=pltpu.MemorySpace.HBM) for s in parts]
    dst = [jax.empty_ref(jax.ShapeDtypeStruct(s.shape, s.dtype), memory_space=pltpu.MemorySpace.HBM) for s in parts]

    @pl.kernel(mesh=plsc.ScalarSubcoreMesh(axis_name="seq", num_cores=1), name=name,
               scratch_types=(pltpu.SemaphoreType.DMA((n, N_DEV - 1)), pltpu.SemaphoreType.DMA((n, N_DEV - 1)), pltpu.SemaphoreType.DMA((n,))),
               compiler_params=pltpu.CompilerParams(collective_id=cid))
    def launch(send_sems, recv_sems, local_sems):
        x, y, c, _ = _place()
        me = 4 * x + 2 * y + c
        peers = [(1 - x if k & 4 else x, 1 - y if k & 2 else y, 1 - c if k & 1 else c) for k in range(1, N_DEV)]
        barrier = pltpu.get_barrier_semaphore()
        for peer in peers:
            pl.semaphore_signal(barrier, inc=1, device_id=peer, device_id_type=MESH)
        pl.semaphore_wait(barrier, N_DEV - 1)
        mine = [pltpu.make_async_copy(src[a].at[me], dst[a].at[me], local_sems.at[a]) for a in range(n)]
        for cp in mine:
            cp.start()
        cps = []
        for a in range(n):
            for k, (px, py, pc) in enumerate(peers):
                cps.append(pltpu.make_async_remote_copy(src_ref=src[a].at[4 * px + 2 * py + pc], dst_ref=dst[a].at[me], send_sem=send_sems.at[a, k],
                                                        recv_sem=recv_sems.at[a, k], device_id=(px, py, pc), device_id_type=MESH))
        for cp in cps:
            cp.start()
        for cp in cps:
            cp.wait_recv()
        for cp in cps:
            cp.wait_send()
        for cp in mine:
            cp.wait()

    launch()
    return [r[...] for r in dst]


def _sibling_exchange(parts, name):
    n = len(parts)

    def body(*refs):
        ins, outs = refs[:n], refs[n:2 * n]
        send_sems, recv_sems = refs[2 * n:]
        x, y, c, _ = _place()

        def copy(a, ch):
            return pltpu.make_async_remote_copy(src_ref=ins[a].at[2 * ch + 1 - c], dst_ref=outs[a].at[ch], send_sem=send_sems.at[a, ch],
                                                recv_sem=recv_sems.at[a, ch], device_id=(x, y, 1 - c), device_id_type=MESH)

        cps = [copy(a, ch) for ch in range(4) for a in range(n)]
        for cp in cps:
            cp.start()
        for cp in cps:
            cp.wait_recv()
        for cp in cps:
            cp.wait_send()

    return pl.pallas_call(
        body, name=name, in_specs=[ANY] * n, out_specs=[ANY] * n,
        out_shape=[jax.ShapeDtypeStruct((4,) + s.shape[1:], s.dtype) for s in parts],
        scratch_shapes=[pltpu.SemaphoreType.DMA((n, 4)), pltpu.SemaphoreType.DMA((n, 4))],
    )(*parts)


def _chip_exchange(parts, name):
    n = len(parts)

    def body(*refs):
        ins, outs = refs[:n], refs[n:2 * n]
        send_sems, recv_sems, local_sems = refs[2 * n:]
        x, y, c, chips = _place()
        my_chip = 2 * x + y

        def copy(a, j):
            px, py = chips[j]
            return pltpu.make_async_remote_copy(src_ref=ins[a].at[2 * px + py], dst_ref=outs[a].at[my_chip], send_sem=send_sems.at[a, j],
                                                recv_sem=recv_sems.at[a, j], device_id=(px, py, c), device_id_type=MESH)

        def landing(a, j):
            px, py = chips[j]
            return pltpu.make_async_remote_copy(src_ref=ins[a].at[my_chip], dst_ref=outs[a].at[2 * px + py], send_sem=send_sems.at[a, j],
                                                recv_sem=recv_sems.at[a, j], device_id=(px, py, c), device_id_type=MESH)

        mine = [pltpu.make_async_copy(ins[a].at[my_chip], outs[a].at[my_chip], local_sems.at[a]) for a in range(n)]
        for cp in mine:
            cp.start()
        cps = [copy(a, j) for j in range(3) for a in range(n)]
        for cp in cps:
            cp.start()
        for j in range(3):
            for a in range(n):
                landing(a, j).wait_recv()
        for cp in cps:
            cp.wait_send()
        for cp in mine:
            cp.wait()

    return pl.pallas_call(
        body, name=name, in_specs=[ANY] * n, out_specs=[ANY] * n,
        out_shape=[jax.ShapeDtypeStruct(s.shape, s.dtype) for s in parts],
        scratch_shapes=[pltpu.SemaphoreType.DMA((n, 3)), pltpu.SemaphoreType.DMA((n, 3)), pltpu.SemaphoreType.DMA((n,))],
    )(*parts)


def _pair_add(mine8, got4, name):
    _, r, cols = mine8.shape
    tr = _pick(r, (256, 64, 16))
    core = lax.axis_index("c").astype(jnp.int32).reshape(1)

    def body(c_ref, a_ref, b_ref, o_ref):
        o_ref[...] = (a_ref[...].astype(F32) + b_ref[...].astype(F32)).astype(BF16)

    return pl.pallas_call(
        body, name=name,
        grid_spec=pltpu.PrefetchScalarGridSpec(
            num_scalar_prefetch=1, grid=(4, r // tr),
            in_specs=[pl.BlockSpec((None, tr, cols), lambda ch, i, c_ref: (2 * ch + c_ref[0], i, 0)),
                      pl.BlockSpec((None, tr, cols), lambda ch, i, c_ref: (ch, i, 0))],
            out_specs=pl.BlockSpec((None, tr, cols), lambda ch, i, c_ref: (ch, i, 0))),
        out_shape=jax.ShapeDtypeStruct((4, r, cols), BF16),
        compiler_params=_params(("parallel", "parallel")),
    )(core, mine8, got4)


HBM = pl.BlockSpec(memory_space=pltpu.HBM)
SEM = pl.BlockSpec(memory_space=pltpu.SEMAPHORE)
EFFECT = pltpu.SideEffectType.DATAFLOW_SIDE_EFFECTING


def _in_hbm(a):
    return pltpu.with_memory_space_constraint(a, pltpu.HBM)


def _remote_copies(plan, srcs, lands, send_sems, recv_sems):
    return [pltpu.make_async_remote_copy(src_ref=s, dst_ref=d, send_sem=send_sems.at[i], recv_sem=recv_sems.at[i], device_id=peer,
                                         device_id_type=MESH) for i, (s, d, peer) in enumerate(plan(srcs, lands))]


def _copies_start(plan, n_copies, srcs, land_shapes, name, after=()):
    ns, nl = len(srcs), len(land_shapes)

    def body(*refs):
        send_sems, recv_sems = refs[ns + nl + len(after):ns + nl + len(after) + 2]
        for cp in _remote_copies(plan, refs[:ns], refs[ns:ns + nl], send_sems, recv_sems):
            cp.start()
        refs[-1][...] = jnp.zeros((8, 128), F32)

    sems = pltpu.SemaphoreType.DMA((n_copies,))
    return pl.pallas_call(
        body, name=name,
        out_shape=(sems, sems, *[pltpu.HBM(s.shape, s.dtype) for s in srcs], *[pltpu.HBM(s.shape, s.dtype) for s in land_shapes],
                   jax.ShapeDtypeStruct((8, 128), F32)),
        in_specs=[HBM] * (ns + nl) + [ANY] * len(after),
        out_specs=(SEM, SEM, *[HBM] * (ns + nl), pl.BlockSpec(memory_space=pltpu.VMEM)),
        input_output_aliases={i: 2 + i for i in range(ns + nl)},
        compiler_params=pltpu.CompilerParams(has_side_effects=EFFECT),
    )(*[_in_hbm(s) for s in srcs], *[_in_hbm(lax.empty(s.shape, s.dtype)) for s in land_shapes], *after)


def _copies_wait(plan, state, ns, name, after=()):
    send_sems, recv_sems, *arrs = state[:-1]
    n = len(arrs)

    def body(*refs):
        cps = _remote_copies(plan, refs[:ns], refs[ns:n], refs[n], refs[n + 1])
        for cp in cps:
            cp.wait_send()
        for cp in cps:
            cp.wait_recv()

    out = pl.pallas_call(
        body, name=name, out_shape=tuple(pltpu.HBM(a.shape, a.dtype) for a in arrs),
        in_specs=[HBM] * n + [SEM, SEM] + [ANY] * len(after), out_specs=tuple([HBM] * n),
        input_output_aliases={i: i for i in range(n)},
        compiler_params=pltpu.CompilerParams(has_side_effects=EFFECT),
    )(*arrs, send_sems, recv_sems, *after)
    return list(out[:ns]), list(out[ns:])


def _gather_plan(srcs, lands):
    x, y, c, chips = _place()
    peers = [(x, y, 1 - c)] + [(*chip, c) for chip in chips]
    return [(s, l.at[4 * x + 2 * y + c], peer) for s, l in zip(srcs, lands) for peer in peers]


def _scatter_plan(srcs, lands):
    x, y, c, _ = _place()
    peers = [(1 - x if k & 4 else x, 1 - y if k & 2 else y, 1 - c if k & 1 else c) for k in range(1, N_DEV)]
    return [(s.at[4 * px + 2 * py + pc], l.at[4 * x + 2 * y + c], (px, py, pc)) for s, l in zip(srcs, lands) for px, py, pc in peers]


def _gather_finish(shards, lands, name):
    n = len(shards)

    def body(*refs):
        ins, zones, outs = refs[:n], refs[n:2 * n], refs[2 * n:3 * n]
        send_sems, recv_sems, local_sems = refs[3 * n:]
        x, y, c, chips = _place()
        mine = [pltpu.make_async_copy(ins[a], outs[a].at[4 * x + 2 * y + c], local_sems.at[a]) for a in range(n)]
        for cp in mine:
            cp.start()
        cps = []
        for j, (px, py) in enumerate(chips):
            for a in range(n):
                cps.append(pltpu.make_async_remote_copy(
                    src_ref=zones[a].at[4 * px + 2 * py + c], dst_ref=outs[a].at[4 * px + 2 * py + c], send_sem=send_sems.at[a, j],
                    recv_sem=recv_sems.at[a, j], device_id=(x, y, 1 - c), device_id_type=MESH))
        for cp in cps:
            cp.start()
        for cp in cps:
            cp.wait_recv()
        for cp in cps:
            cp.wait_send()
        for cp in mine:
            cp.wait()

    return pl.pallas_call(
        body, name=name, in_specs=[ANY] * (2 * n), out_specs=[ANY] * n,
        out_shape=[jax.ShapeDtypeStruct(l.shape, l.dtype) for l in lands],
        input_output_aliases={n + a: a for a in range(n)},
        scratch_shapes=[pltpu.SemaphoreType.DMA((n, 3)), pltpu.SemaphoreType.DMA((n, 3)), pltpu.SemaphoreType.DMA((n,))],
    )(*shards, *lands)


def _sum_everywhere(v, name):
    rows = v.shape[0]

    def body(v_ref, o_ref, buf, send_sems, recv_sems):
        x, y, c, _ = _place()
        me = 4 * x + 2 * y + c
        buf[me] = v_ref[...]
        cps = []
        for k in range(1, N_DEV):
            fx, fy, fc = (k >> 2) & 1, (k >> 1) & 1, k & 1
            to = (1 - x if fx else x, 1 - y if fy else y, 1 - c if fc else c)
            cps.append(pltpu.make_async_remote_copy(src_ref=buf.at[me], dst_ref=buf.at[me], send_sem=send_sems.at[k - 1],
                                                    recv_sem=recv_sems.at[k - 1], device_id=to, device_id_type=MESH))
        for cp in cps:
            cp.start()
        for cp in cps:
            cp.wait_recv()
        for cp in cps:
            cp.wait_send()
        acc = buf[0]
        for d in range(1, N_DEV):
            acc = acc + buf[d]
        o_ref[...] = acc

    vm = pl.BlockSpec(memory_space=pltpu.VMEM)
    return pl.pallas_call(
        body, name=name, in_specs=[vm], out_specs=vm, out_shape=jax.ShapeDtypeStruct(v.shape, F32),
        scratch_shapes=[pltpu.VMEM((N_DEV, rows, 128), F32), pltpu.SemaphoreType.DMA((N_DEV - 1,)), pltpu.SemaphoreType.DMA((N_DEV - 1,))],
    )(v)


def _adamw_parts(w, got, mine, me, m, v, name, after=()):
    r, c = w.shape
    tr = _pick(r, (128, 64))

    def body(me_ref, w_ref, got_ref, own_ref, m_ref, v_ref, *rest):
        go_ref, d_ref, mo_ref, vo_ref = rest[len(after):]
        own = own_ref[...].astype(F32)
        gv = jnp.where(me_ref[0] == 0, own, got_ref[0].astype(F32))
        for d in range(1, N_DEV):
            gv = gv + jnp.where(me_ref[0] == d, own, got_ref[d].astype(F32))
        _adamw_math(gv, w_ref, m_ref, v_ref, go_ref, d_ref, mo_ref, vo_ref)

    tile = pl.BlockSpec((tr, c), lambda i, me_ref: (i, 0))
    out = jax.ShapeDtypeStruct((r, c), F32)
    return pl.pallas_call(
        body, name=name,
        grid_spec=pltpu.PrefetchScalarGridSpec(
            num_scalar_prefetch=1, grid=(r // tr,),
            in_specs=[tile, pl.BlockSpec((N_DEV, tr, c), lambda i, me_ref: (0, i, 0)),
                      pl.BlockSpec((None, tr, c), lambda i, me_ref: (me_ref[0], i, 0)), tile, tile] + [ANY] * len(after),
            out_specs=[tile] * 4),
        out_shape=[out] * 4, compiler_params=_params(("parallel",)),
    )(me.astype(jnp.int32).reshape(1), w, got, mine, m, v, *after)


def _adamw_math(gv, w_ref, m_ref, v_ref, go_ref, d_ref, mo_ref, vo_ref):
    mn = B1 * m_ref[...] + (1.0 - B1) * gv
    vn = B2 * v_ref[...] + (1.0 - B2) * (gv * gv)
    m_hat = mn / (1.0 - B1 ** STEP)
    v_hat = vn / (1.0 - B2 ** STEP)
    go_ref[...] = gv
    d_ref[...] = -LR * (m_hat / (jnp.sqrt(v_hat) + ADAM_EPS) + WD * w_ref[...])
    mo_ref[...] = mn
    vo_ref[...] = vn


def _adamw(w, g, m, v, name):
    r, c = w.shape
    parts = g.ndim == 3
    tr = r if r <= 128 else _pick(r, (128, 64))

    def body(w_ref, g_ref, m_ref, v_ref, go_ref, d_ref, mo_ref, vo_ref):
        if parts:
            gv = g_ref[0].astype(F32)
            for d in range(1, g.shape[0]):
                gv = gv + g_ref[d].astype(F32)
        else:
            gv = g_ref[...]
        mn = B1 * m_ref[...] + (1.0 - B1) * gv
        vn = B2 * v_ref[...] + (1.0 - B2) * (gv * gv)
        m_hat = mn / (1.0 - B1 ** STEP)
        v_hat = vn / (1.0 - B2 ** STEP)
        go_ref[...] = gv
        d_ref[...] = -LR * (m_hat / (jnp.sqrt(v_hat) + ADAM_EPS) + WD * w_ref[...])
        mo_ref[...] = mn
        vo_ref[...] = vn

    tile = pl.BlockSpec((tr, c), lambda i: (i, 0))
    g_spec = pl.BlockSpec((g.shape[0], tr, c), lambda i: (0, i, 0)) if parts else tile
    out = jax.ShapeDtypeStruct((r, c), F32)
    return pl.pallas_call(
        body, name=name, grid=(r // tr,), in_specs=[tile, g_spec, tile, tile], out_specs=[tile] * 4, out_shape=[out] * 4,
        compiler_params=_params(("parallel",)),
    )(w, g, m, v)


BIG = ["w_in", "w_a_out", "w_b_out", "w_mix_out", "w_ff_gate", "w_ff_up", "w_ff_down", "w_ple_gate", "w_ple_proj"]
COL_SHARDED = ["w_in", "w_a_out", "w_b_out", "w_ff_gate", "w_ff_up", "w_ple_proj"]
SMALL = ["conv_w", "w_alpha_up", "b_alpha_up", "gla_head_gain", "g_pre_mix", "g_post_mix", "g_pre_ffn", "g_post_ffn", "g_pre_ple", "g_post_ple"]
WEIGHTS = ["w_in", "conv_w", "w_a_out", "w_alpha_up", "b_alpha_up", "gla_head_gain", "w_b_out", "w_mix_out", "g_pre_mix", "g_post_mix",
           "g_pre_ffn", "g_post_ffn", "w_ff_gate", "w_ff_up", "w_ff_down", "g_pre_ple", "g_post_ple", "w_ple_gate", "w_ple_proj"]


def _cols_to_full(g8):
    n, r, c = g8.shape
    return jnp.transpose(g8, (1, 0, 2)).reshape(r, n * c)


def _full_to_cols(a):
    r, c = a.shape
    return jnp.transpose(a.reshape(r, N_DEV, c // N_DEV), (1, 0, 2))


def _pack(arrs, rows):
    flat = jnp.concatenate([a.reshape(-1) for a in arrs])
    return jnp.pad(flat, (0, rows * 128 - flat.shape[0])).reshape(rows, 128)


def _unpack(packed, shapes):
    flat, out, o = packed.reshape(-1), [], 0
    for s in shapes:
        size = 1
        for d in s:
            size *= d
        out.append(flat[o:o + size].reshape(s))
        o += size
    return out


def kernel(x, p, w_in, conv_w, w_a_out, w_alpha_up, b_alpha_up, gla_head_gain, w_b_out, w_mix_out, g_pre_mix, g_post_mix, g_pre_ffn, g_post_ffn, w_ff_gate, w_ff_up, w_ff_down, g_pre_ple, g_post_ple, w_ple_gate, w_ple_proj, loss_target, m_w_in, m_conv_w, m_w_a_out, m_w_alpha_up, m_b_alpha_up, m_gla_head_gain, m_w_b_out, m_w_mix_out, m_g_pre_mix, m_g_post_mix, m_g_pre_ffn, m_g_post_ffn, m_w_ff_gate, m_w_ff_up, m_w_ff_down, m_g_pre_ple, m_g_post_ple, m_w_ple_gate, m_w_ple_proj, v_w_in, v_conv_w, v_w_a_out, v_w_alpha_up, v_b_alpha_up, v_gla_head_gain, v_w_b_out, v_w_mix_out, v_g_pre_mix, v_g_post_mix, v_g_pre_ffn, v_g_post_ffn, v_w_ff_gate, v_w_ff_up, v_w_ff_down, v_g_pre_ple, v_g_post_ple, v_w_ple_gate, v_w_ple_proj):
    args = dict(locals())
    wts = {n: args[n][0] for n in WEIGHTS}
    mom = {n: args["m_" + n][0] for n in WEIGHTS}
    var = {n: args["v_" + n][0] for n in WEIGHTS}
    me = 4 * lax.axis_index("x") + 2 * lax.axis_index("y") + lax.axis_index("c")

    groups = [["w_in", "conv_w", "w_alpha_up"], ["w_a_out", "w_b_out", "w_mix_out"], ["w_ff_gate", "w_ff_up"], ["w_ff_down"],
              ["w_ple_gate", "w_ple_proj"]]
    grad_groups = [["w_ple_proj", "w_ple_gate"], ["w_ff_down", "w_ff_gate", "w_ff_up"], ["w_mix_out", "w_a_out", "w_b_out"], ["w_in"]]
    rows_full = lambda g: g.reshape(-1, g.shape[-1])
    gathers, scatters = {}, {}

    def gather_start(gi, after):
        shards = [wts[n].astype(BF16) if n in BIG else wts[n] for n in groups[gi]]
        zones = [jax.ShapeDtypeStruct((N_DEV,) + s.shape, s.dtype) for s in shards]
        gathers[gi] = (shards, _copies_start(_gather_plan, 4 * len(shards), shards, zones, "gather_start_%d" % gi, after))
        return gathers[gi][1][-1]

    def gather_finish(gi, after):
        shards, state = gathers[gi]
        shards, zones = _copies_wait(_gather_plan, state, len(shards), "gather_wait_%d" % gi, after)
        g8 = dict(zip(groups[gi], _gather_finish(shards, zones, "gather_finish_%d" % gi)))
        if gi == 0:
            w_in_full = _cols_to_full(g8["w_in"])
            return dict(in_p=jnp.concatenate([w_in_full[:, R_GA:R_END], w_in_full[:, :R_ALR], w_in_full[:, R_ALR:R_GA],
                                              jnp.zeros((w_in_full.shape[0], 128 - GATE_RANK), BF16)], axis=1),
                        conv_w=_cols_to_full(g8["conv_w"]), w_alpha_up=_cols_to_full(g8["w_alpha_up"]))
        if gi == 1:
            return dict(a_out=_cols_to_full(g8["w_a_out"]), b_out=_cols_to_full(g8["w_b_out"]), mix=rows_full(g8["w_mix_out"]))
        if gi == 2:
            return dict(gu=jnp.concatenate([_cols_to_full(g8["w_ff_gate"]), _cols_to_full(g8["w_ff_up"])], axis=1))
        if gi == 3:
            return dict(down=rows_full(g8["w_ff_down"]))
        return dict(pg=rows_full(g8["w_ple_gate"]), pp=_cols_to_full(g8["w_ple_proj"]))

    def scatter_start(gi, gw):
        if gi == 3:
            g_in = gw["in_p"]
            gfull = dict(w_in=jnp.concatenate([g_in[:, C_AX:C_ALR], g_in[:, C_ALR:C_ALR + GATE_RANK], g_in[:, :C_AX]], axis=1))
        elif gi == 1:
            f_gate = gw["gu"].shape[1] // 2
            gfull = dict(w_ff_down=gw["down"], w_ff_gate=gw["gu"][:, :f_gate], w_ff_up=gw["gu"][:, f_gate:])
        elif gi == 2:
            gfull = dict(w_mix_out=gw["mix"], w_a_out=gw["a_out"], w_b_out=gw["b_out"])
        else:
            gfull = dict(w_ple_proj=gw["pp"], w_ple_gate=gw["pg"])
        parts = [_full_to_cols(gfull[n]) if n in COL_SHARDED else gfull[n].reshape(N_DEV, -1, gfull[n].shape[-1]) for n in grad_groups[gi]]
        scatters[gi] = _copies_start(_scatter_plan, (N_DEV - 1) * len(parts), parts, parts, "scatter_start_%d" % gi)
        return scatters[gi][-1]

    small = {n: wts[n].reshape(1, -1) for n in SMALL[2:]}

    loss_rows, grad_x, gs = _local_step(x[0], p[0, 0], loss_target[0], gather_start, gather_finish, scatter_start, small)
    loss = lax.psum(jnp.sum(loss_rows), ("x", "y", "c"))

    res, done = {}, (grad_x,)
    for gi, names in enumerate(grad_groups):
        mine, got = _copies_wait(_scatter_plan, scatters[gi], len(names), "scatter_wait_%d" % gi, done)
        for n, g, own in zip(names, got, mine):
            res[n] = _adamw_parts(wts[n], g, own, me, mom[n], var[n], "adamw_" + n)
        done = tuple(res[n][1] for n in names)

    small_shapes = [gs[n].shape for n in SMALL]
    gsum = dict(zip(SMALL, _unpack(_sum_everywhere(_pack([gs[n] for n in SMALL], 192), "all_reduce_small"), small_shapes)))
    gsum["conv_w"] = lax.dynamic_index_in_dim(gsum["conv_w"].reshape(3, N_DEV, -1), me, axis=1, keepdims=False)
    gsum["w_alpha_up"] = lax.dynamic_index_in_dim(gsum["w_alpha_up"].reshape(GATE_RANK, N_DEV, -1), me, axis=1, keepdims=False)

    shard_shapes = [wts[n].shape for n in SMALL]
    packed = [_pack([d[n] for n in SMALL], 120) for d in (wts, gsum, mom, var)]
    outs = [_unpack(o, shard_shapes) for o in _adamw(*packed, "adamw_small")]
    for i, n in enumerate(SMALL):
        res[n] = [o[i] for o in outs]

    lead = lambda a: a[None]
    return (loss, grad_x[None], *[lead(res[n][0]) for n in WEIGHTS], *[lead(res[n][1]) for n in WEIGHTS],
            *[lead(res[n][2]) for n in WEIGHTS], *[lead(res[n][3]) for n in WEIGHTS])
```

```python
import jax
import jax.numpy as jnp
from jax import lax
from jax.experimental import pallas as pl
from jax.experimental.pallas import tpu as pltpu
from jax.experimental.pallas import tpu_sc as plsc

F32, BF16 = jnp.float32, jnp.bfloat16
EPS = 1e-6
CHUNK = 64
HEADS, DK, DV = 4, 128, 256
GATE_RANK = 16
TAU = 16.0
LR, B1, B2, ADAM_EPS, WD, STEP = 0.001, 0.9, 0.999, 1e-08, 0.01, 10
N_DEV = 8
MESH = pl.DeviceIdType.MESH
VMEM_LIMIT = 56 * 1024 * 1024
ANY = pl.BlockSpec(memory_space=pl.ANY)

C_GA, C_GB, C_AX, C_AB, C_AC, C_Q, C_K, C_V, C_OG, C_ALR = 0, 2048, 4096, 5120, 6144, 7168, 7680, 8192, 9216, 10240
IN_PAD = 10368
R_AX, R_AB, R_AC, R_Q, R_K, R_V, R_OG, R_ALR, R_GA, R_GB, R_END = 0, 1024, 2048, 3072, 3584, 4096, 5120, 6144, 6160, 8208, 10256


def _params(sem):
    return pltpu.CompilerParams(dimension_semantics=sem, vmem_limit_bytes=VMEM_LIMIT)


def _pick(n, cands):
    for c in cands:
        if n % c == 0:
            return c
    return n


def _mm(a, b, mode, out_dtype, name, after=()):
    if mode == "nn":
        (m, k), (k2, n) = a.shape, b.shape
    elif mode == "nt":
        (m, k), (n, k2) = a.shape, b.shape
    else:
        (k, m), (k2, n) = a.shape, b.shape
    assert k == k2 and a.dtype == BF16 and b.dtype == BF16, (name, a.shape, b.shape, a.dtype, b.dtype)
    tm = _pick(m, (1024, 512, 256))
    tn = _pick(n, (1152, 1024, 1408, 512, 256))
    tk = _pick(k, (512, 1152, 256))
    nk = k // tk
    dims = {"nn": (((1,), (0,)), ((), ())), "nt": (((1,), (1,)), ((), ())), "tn": (((0,), (0,)), ((), ()))}[mode]

    def body(a_ref, b_ref, *rest):
        o_ref, acc_ref = rest[len(after):]
        kk = pl.program_id(2)

        @pl.when(kk == 0)
        def _():
            acc_ref[...] = jnp.zeros_like(acc_ref)

        acc_ref[...] += lax.dot_general(a_ref[...], b_ref[...], dims, preferred_element_type=F32)

        @pl.when(kk == nk - 1)
        def _():
            o_ref[...] = acc_ref[...].astype(o_ref.dtype)

    a_spec = pl.BlockSpec((tk, tm), lambda i, j, kk: (kk, i)) if mode == "tn" else pl.BlockSpec((tm, tk), lambda i, j, kk: (i, kk))
    b_spec = pl.BlockSpec((tn, tk), lambda i, j, kk: (j, kk)) if mode == "nt" else pl.BlockSpec((tk, tn), lambda i, j, kk: (kk, j))
    return pl.pallas_call(
        body, name=name, grid=(m // tm, n // tn, nk),
        in_specs=[a_spec, b_spec] + [ANY] * len(after), out_specs=pl.BlockSpec((tm, tn), lambda i, j, kk: (i, j)),
        out_shape=jax.ShapeDtypeStruct((m, n), out_dtype),
        scratch_shapes=[pltpu.VMEM((tm, tn), F32)],
        compiler_params=_params(("parallel", "parallel", "arbitrary")),
    )(a, b, *after)


def _rows(body, t, tr, ins, outs, name):
    in_specs = []
    for arr, sp in ins:
        if sp[0] == "t":
            in_specs.append(pl.BlockSpec((tr, sp[1]), lambda i, cb=sp[2]: (i, cb)))
        else:
            in_specs.append(pl.BlockSpec(arr.shape, lambda i, nd=arr.ndim: (0,) * nd))
    out_specs, out_shape = [], []
    for shape, dt, kind in outs:
        out_specs.append(pl.BlockSpec((tr, shape[1]), lambda i: (i, 0)) if kind == "t" else pl.BlockSpec(shape, lambda i: (0, 0)))
        out_shape.append(jax.ShapeDtypeStruct(shape, dt))
    return pl.pallas_call(
        body, name=name, grid=(t // tr,), in_specs=in_specs, out_specs=out_specs, out_shape=out_shape,
        compiler_params=_params(("arbitrary",)),
    )(*[arr for arr, _ in ins])


def _rinv(v):
    return lax.rsqrt(jnp.mean(v * v, axis=-1, keepdims=True) + EPS)


def _sig(v):
    return 1.0 / (1.0 + jnp.exp(-v))


def _acc(ref, val):
    @pl.when(pl.program_id(0) == 0)
    def _():
        ref[...] = jnp.zeros_like(ref)

    ref[...] += jnp.sum(val, axis=0, keepdims=True)


def _rms_fwd(x, g, name):
    t, d = x.shape

    def body(x_ref, g_ref, h_ref):
        xv = x_ref[...]
        h_ref[...] = (xv * _rinv(xv) * g_ref[...]).astype(BF16)

    return _rows(body, t, 256, [(x, ("t", d, 0)), (g, ("b",))], [((t, d), BF16, "t")], name)[0]


def _post_pre(x, m, g_post, g_pre, name):
    t, d = x.shape

    def body(x_ref, m_ref, gp_ref, gn_ref, xo_ref, h_ref):
        mv = m_ref[...]
        xn = x_ref[...] + mv * _rinv(mv) * gp_ref[...]
        xo_ref[...] = xn
        h_ref[...] = (xn * _rinv(xn) * gn_ref[...]).astype(BF16)

    return _rows(body, t, 128, [(x, ("t", d, 0)), (m, ("t", d, 0)), (g_post, ("b",)), (g_pre, ("b",))],
                 [((t, d), F32, "t"), ((t, d), BF16, "t")], name)


def _mix_fwd(proj, ya, yb, name):
    t, d = ya.shape

    def body(ga_ref, gb_ref, ya_ref, yb_ref, o_ref):
        o_ref[...] = (_sig(ga_ref[...].astype(F32)) * ya_ref[...].astype(F32)
                      + _sig(gb_ref[...].astype(F32)) * yb_ref[...].astype(F32)).astype(BF16)

    return _rows(body, t, 256, [(proj, ("t", d, C_GA // d)), (proj, ("t", d, C_GB // d)), (ya, ("t", d, 0)), (yb, ("t", d, 0))],
                 [((t, d), BF16, "t")], name)[0]


def _mix_bwd(dmix, proj, ya, yb, name):
    t, d = ya.shape

    def body(dm_ref, ga_ref, gb_ref, ya_ref, yb_ref, dg_ref, dya_ref, dyb_ref):
        dm = dm_ref[...]
        sa, sb = _sig(ga_ref[...].astype(F32)), _sig(gb_ref[...].astype(F32))
        dg_ref[:, :d] = (dm * ya_ref[...].astype(F32) * sa * (1.0 - sa)).astype(BF16)
        dg_ref[:, d:] = (dm * yb_ref[...].astype(F32) * sb * (1.0 - sb)).astype(BF16)
        dya_ref[...] = (dm * sa).astype(BF16)
        dyb_ref[...] = (dm * sb).astype(BF16)

    return _rows(body, t, 128,
                 [(dmix, ("t", d, 0)), (proj, ("t", d, C_GA // d)), (proj, ("t", d, C_GB // d)), (ya, ("t", d, 0)), (yb, ("t", d, 0))],
                 [((t, 2 * d), BF16, "t"), ((t, d), BF16, "t"), ((t, d), BF16, "t")], name)


def _swiglu_fwd(fgu, name):
    t, f2 = fgu.shape
    f = f2 // 2
    tc = _pick(f, (1408, 512))
    nc = f // tc

    def body(g_ref, u_ref, s_ref):
        gv = g_ref[...].astype(F32)
        s_ref[...] = (gv * _sig(gv) * u_ref[...].astype(F32)).astype(BF16)

    return pl.pallas_call(
        body, name=name, grid=(t // 512, nc),
        in_specs=[pl.BlockSpec((512, tc), lambda i, j: (i, j)), pl.BlockSpec((512, tc), lambda i, j: (i, j + nc))],
        out_specs=pl.BlockSpec((512, tc), lambda i, j: (i, j)), out_shape=jax.ShapeDtypeStruct((t, f), BF16),
        compiler_params=_params(("parallel", "parallel")),
    )(fgu, fgu)


def _swiglu_bwd(ds, fgu, name):
    t, f2 = fgu.shape
    f = f2 // 2
    tc = _pick(f, (1408, 512))
    nc = f // tc

    def body(ds_ref, g_ref, u_ref, o_ref):
        dsv, gv, uv = ds_ref[...].astype(F32), g_ref[...].astype(F32), u_ref[...].astype(F32)
        sg = _sig(gv)
        dfg = dsv * uv * sg * (1.0 + gv * (1.0 - sg))
        dfu = dsv * gv * sg
        o_ref[...] = jnp.where(pl.program_id(1) < nc, dfg, dfu).astype(BF16)

    return pl.pallas_call(
        body, name=name, grid=(t // 512, 2 * nc),
        in_specs=[pl.BlockSpec((512, tc), lambda i, j: (i, j % nc)), pl.BlockSpec((512, tc), lambda i, j: (i, j % nc)),
                  pl.BlockSpec((512, tc), lambda i, j: (i, j % nc + nc))],
        out_specs=pl.BlockSpec((512, tc), lambda i, j: (i, j)), out_shape=jax.ShapeDtypeStruct((t, f2), BF16),
        compiler_params=_params(("parallel", "parallel")),
    )(ds, fgu, fgu)


def _ple_final(x2, pg, pp, tgt, g_post, name):
    t, d = x2.shape

    def body(x_ref, pg_ref, pp_ref, t_ref, g_ref, loss_ref, d3_ref, dpg_ref, dpp_ref, dg_ref):
        sg, ppv, g = _sig(pg_ref[...]), pp_ref[...], g_ref[...]
        e = sg * ppv
        r = _rinv(e)
        eh = e * r
        diff = x_ref[...] + eh * g - t_ref[...]
        loss_ref[...] = 0.5 * jnp.mean(diff * diff, axis=-1, keepdims=True)
        d3 = diff * (1.0 / d)
        d3_ref[...] = d3
        gd = d3 * g
        de = r * (gd - eh * jnp.mean(gd * eh, axis=-1, keepdims=True))
        dpg_ref[...] = (de * ppv * sg * (1.0 - sg)).astype(BF16)
        dpp_ref[...] = (de * sg).astype(BF16)
        _acc(dg_ref, d3 * eh)

    return _rows(body, t, 128, [(x2, ("t", d, 0)), (pg, ("t", d, 0)), (pp, ("t", d, 0)), (tgt, ("t", d, 0)), (g_post, ("b",))],
                 [((t, 1), F32, "t"), ((t, d), F32, "t"), ((t, d), BF16, "t"), ((t, d), BF16, "t"), ((1, d), F32, "a")], name)


def _norm_bwd(dn, dh, x, g_pre, fm, g_post, name):
    t, d = x.shape
    two = fm is not None

    def body(*refs):
        if two:
            dn_ref, dh_ref, x_ref, gp_ref, f_ref, gq_ref, dx_ref, df_ref, dgp_ref, dgq_ref = refs
        else:
            dn_ref, dh_ref, x_ref, gp_ref, dx_ref, dgp_ref = refs
        xv, dhv = x_ref[...], dh_ref[...]
        r = _rinv(xv)
        xh = xv * r
        gd = dhv * gp_ref[...]
        dx = dn_ref[...] + r * (gd - xh * jnp.mean(gd * xh, axis=-1, keepdims=True))
        dx_ref[...] = dx
        _acc(dgp_ref, dhv * xh)
        if two:
            fv = f_ref[...]
            rf = _rinv(fv)
            fh = fv * rf
            gd2 = dx * gq_ref[...]
            df_ref[...] = (rf * (gd2 - fh * jnp.mean(gd2 * fh, axis=-1, keepdims=True))).astype(BF16)
            _acc(dgq_ref, dx * fh)

    ins = [(dn, ("t", d, 0)), (dh, ("t", d, 0)), (x, ("t", d, 0)), (g_pre, ("b",))]
    outs = [((t, d), F32, "t")]
    if two:
        ins += [(fm, ("t", d, 0)), (g_post, ("b",))]
        outs += [((t, d), BF16, "t"), ((1, d), F32, "a"), ((1, d), F32, "a")]
    else:
        outs += [((1, d), F32, "a")]
    return _rows(body, t, 128, ins, outs, name)


CONV_TC = 256


def _shift_down(v, s):
    rows = lax.broadcasted_iota(jnp.int32, v.shape, 0)
    return jnp.where(rows >= s, pltpu.roll(v, s, 0), 0.0)


def _shift_up(v, s):
    n = v.shape[0]
    rows = lax.broadcasted_iota(jnp.int32, v.shape, 0)
    return jnp.where(rows < n - s, pltpu.roll(v, n - s, 0), 0.0)


def _conv_specs(t):
    nb = 1024 // CONV_TC
    seg = lambda c0: pl.BlockSpec((t, CONV_TC), lambda j, cb=c0 // CONV_TC: (0, cb + j))
    own = pl.BlockSpec((t, CONV_TC), lambda j: (0, j))
    wspec = pl.BlockSpec((3, CONV_TC), lambda j: (0, j))
    return nb, seg, own, wspec


def _conv_fwd(proj, conv_w, name, after=()):
    t = proj.shape[0]
    nb, seg, own, wspec = _conv_specs(t)

    def body(ax_ref, ab_ref, ac_ref, w_ref, *rest):
        za_ref = rest[len(after)]
        u = ac_ref[...].astype(F32) * ax_ref[...].astype(F32)
        w = w_ref[...]
        yc = w[0:1] * _shift_down(u, 2) + w[1:2] * _shift_down(u, 1) + w[2:3] * u
        za_ref[...] = (ab_ref[...].astype(F32) * yc).astype(BF16)

    return pl.pallas_call(
        body, name=name, grid=(nb,), in_specs=[seg(C_AX), seg(C_AB), seg(C_AC), wspec] + [ANY] * len(after), out_specs=own,
        out_shape=jax.ShapeDtypeStruct((t, 1024), BF16), compiler_params=_params(("parallel",)),
    )(proj, proj, proj, conv_w, *after)


def _conv_bwd(dza, proj, conv_w, name):
    t = proj.shape[0]
    nb, seg, own, wspec = _conv_specs(t)

    def body(dz_ref, ax_ref, ab_ref, ac_ref, w_ref, dax_ref, dab_ref, dac_ref, dw_ref):
        ax, ab, ac, dz = ax_ref[...].astype(F32), ab_ref[...].astype(F32), ac_ref[...].astype(F32), dz_ref[...].astype(F32)
        w = w_ref[...]
        u = ac * ax
        u1, u2 = _shift_down(u, 1), _shift_down(u, 2)
        yc = w[0:1] * u2 + w[1:2] * u1 + w[2:3] * u
        dab_ref[...] = (dz * yc).astype(BF16)
        dyc = dz * ab
        du = w[2:3] * dyc + w[1:2] * _shift_up(dyc, 1) + w[0:1] * _shift_up(dyc, 2)
        dax_ref[...] = (du * ac).astype(BF16)
        dac_ref[...] = (du * ax).astype(BF16)
        dw_ref[0:1, :] = jnp.sum(dyc * u2, axis=0, keepdims=True)
        dw_ref[1:2, :] = jnp.sum(dyc * u1, axis=0, keepdims=True)
        dw_ref[2:3, :] = jnp.sum(dyc * u, axis=0, keepdims=True)

    act = jax.ShapeDtypeStruct((t, 1024), BF16)
    return pl.pallas_call(
        body, name=name, grid=(nb,), in_specs=[own, seg(C_AX), seg(C_AB), seg(C_AC), wspec], out_specs=[own, own, own, wspec],
        out_shape=[act, act, act, jax.ShapeDtypeStruct((3, 1024), F32)], compiler_params=_params(("parallel",)),
    )(dza, proj, proj, proj, conv_w)


def _dot(a, b, dims, precision=None):
    return lax.dot_general(a, b, (dims, ((), ())), precision=precision, preferred_element_type=F32)


def _gla_chunk(q, k, v, og, alr, s_in, wa, ba, gain):
    c = q.shape[0]
    hi = lax.Precision.HIGHEST
    z = _dot(alr, wa, ((1,), (0,))) + ba
    la = (jnp.minimum(z, 0.0) - jnp.log(1.0 + jnp.exp(-jnp.abs(z)))) * (1.0 / TAU)
    row = lax.broadcasted_iota(jnp.int32, (c, c), 0)
    col = lax.broadcasted_iota(jnp.int32, (c, c), 1)
    lower = row >= col
    b = _dot(lower.astype(F32), la, ((1,), (0,)), hi)
    trow = lax.broadcasted_iota(jnp.int32, la.shape, 0)
    mid = jnp.sum(jnp.where(trow <= c // 2, la, 0.0), axis=0, keepdims=True)
    blast = jnp.sum(la, axis=0, keepdims=True)
    qs = q * (DK ** -0.5)
    e_up, e_dn = jnp.exp(b - mid), jnp.exp(mid - b)
    a_fwd = _dot(qs * e_up, k * e_dn, ((1,), (1,)))
    a_rev = _dot(qs * e_dn, k * e_up, ((1,), (1,)))
    att = jnp.where(lower, a_fwd, a_rev)
    o = _dot(att, v, ((1,), (0,))) + _dot(qs * jnp.exp(b), s_in, ((1,), (0,)))
    upd = _dot(k * jnp.exp(blast - b), v, ((0,), (0,)))
    blast_col = _dot(la, jnp.ones((c, DV), F32), ((0,), (0,)), hi)
    s_out = jnp.exp(blast_col) * s_in + upd
    on = o * _rinv(o) * gain
    return on * og * _sig(og), s_out


def _gla_specs(t, rev):
    n = t // CHUNK
    ch = (lambda i: n - 1 - i) if rev else (lambda i: i)
    col = lambda w, c0: pl.BlockSpec((CHUNK, w), lambda i, h, cb=c0 // w: (ch(i), cb + h))
    specs = dict(
        q=col(DK, C_Q), k=col(DK, C_K), v=col(DV, C_V), og=col(DV, C_OG),
        alr=pl.BlockSpec((CHUNK, 128), lambda i, h: (ch(i), C_ALR // 128)),
        wa=pl.BlockSpec((128, DK), lambda i, h: (0, h)), ba=pl.BlockSpec((1, DK), lambda i, h: (0, h)),
        gain=pl.BlockSpec((1, DV), lambda i, h: (0, 0)),
        state=pl.BlockSpec((None, None, DK, DV), lambda i, h: (ch(i), h, 0, 0)),
        odk=pl.BlockSpec((CHUNK, DK), lambda i, h: (ch(i), h)), odv=pl.BlockSpec((CHUNK, DV), lambda i, h: (ch(i), h)),
        oalr=pl.BlockSpec((CHUNK, 128), lambda i, h: (ch(i), 0)),
    )
    return n, specs


def _gla_fwd(proj, wa, ba, gain, name):
    t = proj.shape[0]
    n, sp = _gla_specs(t, False)

    def body(q_ref, k_ref, v_ref, og_ref, alr_ref, wa_ref, ba_ref, g_ref, zb_ref, st_ref, s_scr):
        h = pl.program_id(1)

        @pl.when(pl.program_id(0) == 0)
        def _():
            s_scr[h] = jnp.zeros((DK, DV), F32)

        s_in = s_scr[h]
        st_ref[...] = s_in
        zb, s_out = _gla_chunk(q_ref[...].astype(F32), k_ref[...].astype(F32), v_ref[...].astype(F32), og_ref[...].astype(F32),
                               alr_ref[...].astype(F32), s_in, wa_ref[...].astype(F32), ba_ref[...], g_ref[...])
        zb_ref[...] = zb.astype(BF16)
        s_scr[h] = s_out

    return pl.pallas_call(
        body, name=name, grid=(n, HEADS),
        in_specs=[sp["q"], sp["k"], sp["v"], sp["og"], sp["alr"], sp["wa"], sp["ba"], sp["gain"]],
        out_specs=[sp["odv"], sp["state"]],
        out_shape=[jax.ShapeDtypeStruct((t, HEADS * DV), BF16), jax.ShapeDtypeStruct((n, HEADS, DK, DV), F32)],
        scratch_shapes=[pltpu.VMEM((HEADS, DK, DV), F32)],
        compiler_params=_params(("arbitrary", "arbitrary")),
    )(proj, proj, proj, proj, proj, wa, ba, gain)


def _gla_bwd(dzb, proj, states, wa, ba, gain, name):
    t = proj.shape[0]
    n, sp = _gla_specs(t, True)

    def body(dz_ref, q_ref, k_ref, v_ref, og_ref, alr_ref, st_ref, wa_ref, ba_ref, g_ref,
             dq_ref, dk_ref, dv_ref, dog_ref, dalr_ref, dwa_ref, dba_ref, dg_ref, ds_scr):
        i, h = pl.program_id(0), pl.program_id(1)

        @pl.when(i == 0)
        def _():
            ds_scr[h] = jnp.zeros((DK, DV), F32)
            dwa_ref[h] = jnp.zeros((128, DK), F32)
            dba_ref[h] = jnp.zeros((1, DK), F32)

        @pl.when((i == 0) & (h == 0))
        def _():
            dg_ref[...] = jnp.zeros_like(dg_ref)

        args = (q_ref[...].astype(F32), k_ref[...].astype(F32), v_ref[...].astype(F32), og_ref[...].astype(F32),
                alr_ref[...].astype(F32), st_ref[...], wa_ref[...].astype(F32), ba_ref[...], g_ref[...])
        _, vjp = jax.vjp(_gla_chunk, *args)
        dq, dk, dv, dog, dalr, ds_in, dwa, dba, dgain = vjp((dz_ref[...].astype(F32), ds_scr[h]))
        dq_ref[...] = dq.astype(BF16)
        dk_ref[...] = dk.astype(BF16)
        dv_ref[...] = dv.astype(BF16)
        dog_ref[...] = dog.astype(BF16)

        @pl.when(h == 0)
        def _():
            dalr_ref[...] = dalr

        @pl.when(h != 0)
        def _():
            dalr_ref[...] += dalr

        ds_scr[h] = ds_in
        dwa_ref[h] += dwa
        dba_ref[h] += dba
        dg_ref[...] += dgain

    whole = lambda shape: pl.BlockSpec(shape, lambda i, h, nd=len(shape): (0,) * nd)
    return pl.pallas_call(
        body, name=name, grid=(n, HEADS),
        in_specs=[sp["odv"], sp["q"], sp["k"], sp["v"], sp["og"], sp["alr"], sp["state"], sp["wa"], sp["ba"], sp["gain"]],
        out_specs=[sp["odk"], sp["odk"], sp["odv"], sp["odv"], sp["oalr"], whole((HEADS, 128, DK)), whole((HEADS, 1, DK)), whole((1, DV))],
        out_shape=[jax.ShapeDtypeStruct((t, HEADS * DK), BF16), jax.ShapeDtypeStruct((t, HEADS * DK), BF16),
                   jax.ShapeDtypeStruct((t, HEADS * DV), BF16), jax.ShapeDtypeStruct((t, HEADS * DV), BF16),
                   jax.ShapeDtypeStruct((t, 128), F32), jax.ShapeDtypeStruct((HEADS, 128, DK), F32),
                   jax.ShapeDtypeStruct((HEADS, 1, DK), F32), jax.ShapeDtypeStruct((1, DV), F32)],
        scratch_shapes=[pltpu.VMEM((HEADS, DK, DV), F32)],
        compiler_params=_params(("arbitrary", "arbitrary")),
    )(dzb, proj, proj, proj, proj, proj, states, wa, ba, gain)


def _local_step(x, p, tgt, gather_start, gather_finish, scatter_start, small):
    b_alpha, gain = small["b_alpha_up"], small["gla_head_gain"]
    gather_start(0, ())
    w = dict(gather_finish(0, ()))
    conv_w, w_alpha = w["conv_w"], w["w_alpha_up"]
    wa_p = jnp.zeros((128, HEADS * DK), BF16).at[:GATE_RANK].set(w_alpha.astype(BF16))

    t1 = gather_start(1, (w["in_p"],))
    h1 = _rms_fwd(x, small["g_pre_mix"], "rms_pre_mix")
    proj = _mm(h1, w["in_p"], "nn", BF16, "mm_proj", after=(t1,))
    t2 = gather_start(2, (proj,))
    za = _conv_fwd(proj, conv_w, "conv_fwd", after=(t2,))
    zb, states = _gla_fwd(proj, wa_p, b_alpha, gain, "gla_fwd")
    t3 = gather_start(3, (zb, za))
    w.update(gather_finish(1, (t3,)))
    ya = _mm(za, w["a_out"], "nn", BF16, "mm_ya")
    yb = _mm(zb, w["b_out"], "nn", BF16, "mm_yb")
    mix = _mix_fwd(proj, ya, yb, "mix_fwd")
    m2 = _mm(mix, w["mix"], "nn", F32, "mm_mix")
    t4 = gather_start(4, (m2,))
    x1, h2 = _post_pre(x, m2, small["g_post_mix"], small["g_pre_ffn"], "norm_mix_ffn")
    w.update(gather_finish(2, (h2, t4)))
    fgu = _mm(h2, w["gu"], "nn", BF16, "mm_gu")
    s = _swiglu_fwd(fgu, "swiglu_fwd")
    w.update(gather_finish(3, (s,)))
    f = _mm(s, w["down"], "nn", F32, "mm_down")
    x2, h3 = _post_pre(x1, f, small["g_post_ffn"], small["g_pre_ple"], "norm_ffn_ple")
    w.update(gather_finish(4, (h3,)))
    pg = _mm(h3, w["pg"], "nn", F32, "mm_pg")
    p_bf = p.astype(BF16)
    pp = _mm(p_bf, w["pp"], "nn", F32, "mm_pp")
    loss_rows, d3, dpg, dpp, dg_post_ple = _ple_final(x2, pg, pp, tgt, small["g_post_ple"], "ple_final")

    gw = {}
    gw["pp"] = _mm(p_bf, dpp, "tn", BF16, "mm_dw_pp")
    gw["pg"] = _mm(h3, dpg, "tn", BF16, "mm_dw_pg")
    dh3 = _mm(dpg, w["pg"], "nt", F32, "mm_dh3", after=(scatter_start(0, gw),))
    d2, df, dg_pre_ple, dg_post_ffn = _norm_bwd(d3, dh3, x2, small["g_pre_ple"], f, small["g_post_ffn"], "norm_bwd_ple_ffn")
    ds = _mm(df, w["down"], "nt", BF16, "mm_ds")
    gw["down"] = _mm(s, df, "tn", BF16, "mm_dw_down")
    dfgu = _swiglu_bwd(ds, fgu, "swiglu_bwd")
    gw["gu"] = _mm(h2, dfgu, "tn", BF16, "mm_dw_gu", after=(gw["down"],))
    dh2 = _mm(dfgu, w["gu"], "nt", F32, "mm_dh2", after=(scatter_start(1, gw),))
    d1, dm2, dg_pre_ffn, dg_post_mix = _norm_bwd(d2, dh2, x1, small["g_pre_ffn"], m2, small["g_post_mix"], "norm_bwd_ffn_mix")
    dmix = _mm(dm2, w["mix"], "nt", F32, "mm_dmix")
    gw["mix"] = _mm(mix, dm2, "tn", BF16, "mm_dw_mix")
    dgab, dya, dyb = _mix_bwd(dmix, proj, ya, yb, "mix_bwd")
    dza = _mm(dya, w["a_out"], "nt", BF16, "mm_dza", after=(gw["mix"],))
    gw["a_out"] = _mm(za, dya, "tn", BF16, "mm_dw_a_out")
    gw["b_out"] = _mm(zb, dyb, "tn", BF16, "mm_dw_b_out", after=(gw["a_out"],))
    dzb = _mm(dyb, w["b_out"], "nt", BF16, "mm_dzb", after=(scatter_start(2, gw),))
    dax, dab, dac, dconv = _conv_bwd(dza, proj, conv_w, "conv_bwd")
    dq, dk, dv, dog, dalr, dwa, dba, dgain = _gla_bwd(dzb, proj, states, wa_p, b_alpha, gain, "gla_bwd")
    dproj = jnp.concatenate([dgab, dax, dab, dac, dq, dk, dv, dog, dalr.astype(BF16)], axis=1)
    gw["in_p"] = _mm(h1, dproj, "tn", BF16, "mm_dw_in")
    dh1 = _mm(dproj, w["in_p"], "nt", F32, "mm_dh1", after=(scatter_start(3, gw),))
    grad_x, dg_pre_mix = _norm_bwd(d1, dh1, x, small["g_pre_mix"], None, None, "norm_bwd_mix")

    gs = dict(
        conv_w=dconv,
        w_alpha_up=jnp.transpose(dwa[:, :GATE_RANK, :], (1, 0, 2)).reshape(GATE_RANK, HEADS * DK),
        b_alpha_up=dba.reshape(1, HEADS * DK), gla_head_gain=dgain,
        g_pre_mix=dg_pre_mix, g_post_mix=dg_post_mix, g_pre_ffn=dg_pre_ffn, g_post_ffn=dg_post_ffn,
        g_pre_ple=dg_pre_ple, g_post_ple=dg_post_ple,
    )
    return loss_rows, grad_x, gs


def _place():
    x, y, c = lax.axis_index("x"), lax.axis_index("y"), lax.axis_index("c")
    return x, y, c, [(1 - x, y), (x, 1 - y), (1 - x, 1 - y)]


def _all_gather(shards, name, cid=None):
    n = len(shards)

    def body(*refs):
        ins, outs = refs[:n], refs[n:2 * n]
        send_sems, recv_sems, local_sems = refs[2 * n:]
        x, y, c, chips = _place()
        me, sibling = (x, y, c), (x, y, 1 - c)

        def slot(px, py, pc):
            return 4 * px + 2 * py + pc

        def copy(a, k, block, to, src=None):
            dst = outs[a].at[slot(*block)]
            return pltpu.make_async_remote_copy(src_ref=dst if src is None else src, dst_ref=dst, send_sem=send_sems.at[a, k],
                                                recv_sem=recv_sems.at[a, k], device_id=to, device_id_type=MESH)

        mine = [pltpu.make_async_copy(ins[a], outs[a].at[slot(*me)], local_sems.at[a]) for a in range(n)]
        for cp in mine:
            cp.start()
        first = []
        for j, chip in enumerate(chips):
            first += [copy(a, 1 + j, me, (*chip, c), src=ins[a]) for a in range(n)]
        first += [copy(a, 0, me, sibling, src=ins[a]) for a in range(n)]
        for cp in first:
            cp.start()
        passed = []
        for j, chip in enumerate(chips):
            for a in range(n):
                copy(a, 1 + j, (*chip, c), me).wait_recv()
                cp = copy(a, 4 + j, (*chip, c), sibling)
                cp.start()
                passed.append(cp)
        for a in range(n):
            copy(a, 0, sibling, me).wait_recv()
        for j, chip in enumerate(chips):
            for a in range(n):
                copy(a, 4 + j, (*chip, 1 - c), me).wait_recv()
        for cp in first + passed:
            cp.wait_send()
        for cp in mine:
            cp.wait()

    if cid is None:
        return pl.pallas_call(
            body, name=name, in_specs=[ANY] * n, out_specs=[ANY] * n,
            out_shape=[jax.ShapeDtypeStruct((N_DEV,) + s.shape, s.dtype) for s in shards],
            scratch_shapes=[pltpu.SemaphoreType.DMA((n, 7)), pltpu.SemaphoreType.DMA((n, 7)), pltpu.SemaphoreType.DMA((n,))],
        )(*shards)

    src = [jax.new_ref(s, memory_space=pltpu.MemorySpace.HBM) for s in shards]
    dst = [jax.empty_ref(jax.ShapeDtypeStruct((N_DEV,) + s.shape, s.dtype), memory_space=pltpu.MemorySpace.HBM) for s in shards]

    @pl.kernel(mesh=plsc.ScalarSubcoreMesh(axis_name="seq", num_cores=1), name=name,
               scratch_types=(pltpu.SemaphoreType.DMA((n, 7)), pltpu.SemaphoreType.DMA((n, 7)), pltpu.SemaphoreType.DMA((n,))),
               compiler_params=pltpu.CompilerParams(collective_id=cid))
    def launch(send_sems, recv_sems, local_sems):
        x, y, c, chips = _place()
        barrier = pltpu.get_barrier_semaphore()
        for peer in [(x, y, 1 - c)] + [(*chip, c) for chip in chips]:
            pl.semaphore_signal(barrier, inc=1, device_id=peer, device_id_type=MESH)
        pl.semaphore_wait(barrier, 4)
        body(*src, *dst, send_sems, recv_sems, local_sems)

    launch()
    return [r[...] for r in dst]


def _reduce_scatter(parts, name, cid):
    n = len(parts)
    src = [jax.new_ref(s, memory_space=pltpu.MemorySpace.HBM) for s in parts]
    dst = [jax.empty_ref(jax.ShapeDtypeStruct(s.shape, s.dtype), memory_space=pltpu.MemorySpace.HBM) for s in parts]

    @pl.kernel(mesh=plsc.ScalarSubcoreMesh(axis_name="seq", num_cores=1), name=name,
               scratch_types=(pltpu.SemaphoreType.DMA((n, N_DEV - 1)), pltpu.SemaphoreType.DMA((n, N_DEV - 1)), pltpu.SemaphoreType.DMA((n,))),
               compiler_params=pltpu.CompilerParams(collective_id=cid))
    def launch(send_sems, recv_sems, local_sems):
        x, y, c, _ = _place()
        me = 4 * x + 2 * y + c
        peers = [(1 - x if k & 4 else x, 1 - y if k & 2 else y, 1 - c if k & 1 else c) for k in range(1, N_DEV)]
        barrier = pltpu.get_barrier_semaphore()
        for peer in peers:
            pl.semaphore_signal(barrier, inc=1, device_id=peer, device_id_type=MESH)
        pl.semaphore_wait(barrier, N_DEV - 1)
        mine = [pltpu.make_async_copy(src[a].at[me], dst[a].at[me], local_sems.at[a]) for a in range(n)]
        for cp in mine:
            cp.start()
        cps = []
        for a in range(n):
            for k, (px, py, pc) in enumerate(peers):
                cps.append(pltpu.make_async_remote_copy(src_ref=src[a].at[4 * px + 2 * py + pc], dst_ref=dst[a].at[me], send_sem=send_sems.at[a, k],
                                                        recv_sem=recv_sems.at[a, k], device_id=(px, py, pc), device_id_type=MESH))
        for cp in cps:
            cp.start()
        for cp in cps:
            cp.wait_recv()
        for cp in cps:
            cp.wait_send()
        for cp in mine:
            cp.wait()

    launch()
    return [r[...] for r in dst]


def _sibling_exchange(parts, name):
    n = len(parts)
    pieces = [_row_pieces(s.shape[1]) for s in parts]

    def body(*refs):
        ins, outs = refs[:n], refs[n:2 * n]
        send_sems, recv_sems = refs[2 * n:]
        x, y, c, _ = _place()

        def copy(a, ch, q, rows):
            return pltpu.make_async_remote_copy(src_ref=ins[a].at[2 * ch + 1 - c, rows], dst_ref=outs[a].at[ch, rows], send_sem=send_sems.at[a, ch, q],
                                                recv_sem=recv_sems.at[a, ch, q], device_id=(x, y, 1 - c), device_id_type=MESH)

        cps = [copy(a, ch, q, rows) for ch in range(4) for a in range(n) for q, rows in enumerate(pieces[a])]
        for cp in cps:
            cp.start()
        for cp in cps:
            cp.wait_recv()
        for cp in cps:
            cp.wait_send()

    return pl.pallas_call(
        body, name=name, in_specs=[ANY] * n, out_specs=[ANY] * n,
        out_shape=[jax.ShapeDtypeStruct((4,) + s.shape[1:], s.dtype) for s in parts],
        scratch_shapes=[pltpu.SemaphoreType.DMA((n, 4, PIECES)), pltpu.SemaphoreType.DMA((n, 4, PIECES))],
    )(*parts)


def _chip_exchange(parts, name):
    n = len(parts)

    def body(*refs):
        ins, outs = refs[:n], refs[n:2 * n]
        send_sems, recv_sems, local_sems = refs[2 * n:]
        x, y, c, chips = _place()
        my_chip = 2 * x + y

        def copy(a, j):
            px, py = chips[j]
            return pltpu.make_async_remote_copy(src_ref=ins[a].at[2 * px + py], dst_ref=outs[a].at[my_chip], send_sem=send_sems.at[a, j],
                                                recv_sem=recv_sems.at[a, j], device_id=(px, py, c), device_id_type=MESH)

        def landing(a, j):
            px, py = chips[j]
            return pltpu.make_async_remote_copy(src_ref=ins[a].at[my_chip], dst_ref=outs[a].at[2 * px + py], send_sem=send_sems.at[a, j],
                                                recv_sem=recv_sems.at[a, j], device_id=(px, py, c), device_id_type=MESH)

        mine = [pltpu.make_async_copy(ins[a].at[my_chip], outs[a].at[my_chip], local_sems.at[a]) for a in range(n)]
        for cp in mine:
            cp.start()
        cps = [copy(a, j) for j in range(3) for a in range(n)]
        for cp in cps:
            cp.start()
        for j in range(3):
            for a in range(n):
                landing(a, j).wait_recv()
        for cp in cps:
            cp.wait_send()
        for cp in mine:
            cp.wait()

    return pl.pallas_call(
        body, name=name, in_specs=[ANY] * n, out_specs=[ANY] * n,
        out_shape=[jax.ShapeDtypeStruct(s.shape, s.dtype) for s in parts],
        scratch_shapes=[pltpu.SemaphoreType.DMA((n, 3)), pltpu.SemaphoreType.DMA((n, 3)), pltpu.SemaphoreType.DMA((n,))],
    )(*parts)


def _pair_add(mine8, got4, name):
    _, r, cols = mine8.shape
    tr = _pick(r, (256, 64, 16))
    core = lax.axis_index("c").astype(jnp.int32).reshape(1)

    def body(c_ref, a_ref, b_ref, o_ref):
        o_ref[...] = (a_ref[...].astype(F32) + b_ref[...].astype(F32)).astype(BF16)

    return pl.pallas_call(
        body, name=name,
        grid_spec=pltpu.PrefetchScalarGridSpec(
            num_scalar_prefetch=1, grid=(4, r // tr),
            in_specs=[pl.BlockSpec((None, tr, cols), lambda ch, i, c_ref: (2 * ch + c_ref[0], i, 0)),
                      pl.BlockSpec((None, tr, cols), lambda ch, i, c_ref: (ch, i, 0))],
            out_specs=pl.BlockSpec((None, tr, cols), lambda ch, i, c_ref: (ch, i, 0))),
        out_shape=jax.ShapeDtypeStruct((4, r, cols), BF16),
        compiler_params=_params(("parallel", "parallel")),
    )(core, mine8, got4)


HBM = pl.BlockSpec(memory_space=pltpu.HBM)
SEM = pl.BlockSpec(memory_space=pltpu.SEMAPHORE)
EFFECT = pltpu.SideEffectType.DATAFLOW_SIDE_EFFECTING


def _in_hbm(a):
    return pltpu.with_memory_space_constraint(a, pltpu.HBM)


def _remote_copies(plan, srcs, lands, send_sems, recv_sems):
    return [pltpu.make_async_remote_copy(src_ref=s, dst_ref=d, send_sem=send_sems.at[i], recv_sem=recv_sems.at[i], device_id=peer,
                                         device_id_type=MESH) for i, (s, d, peer) in enumerate(plan(srcs, lands))]


def _copies_start(plan, n_copies, srcs, land_shapes, name, after=()):
    ns, nl = len(srcs), len(land_shapes)

    def body(*refs):
        send_sems, recv_sems = refs[ns + nl + len(after):ns + nl + len(after) + 2]
        for cp in _remote_copies(plan, refs[:ns], refs[ns:ns + nl], send_sems, recv_sems):
            cp.start()
        refs[-1][...] = jnp.zeros((8, 128), F32)

    sems = pltpu.SemaphoreType.DMA((n_copies,))
    return pl.pallas_call(
        body, name=name,
        out_shape=(sems, sems, *[pltpu.HBM(s.shape, s.dtype) for s in srcs], *[pltpu.HBM(s.shape, s.dtype) for s in land_shapes],
                   jax.ShapeDtypeStruct((8, 128), F32)),
        in_specs=[HBM] * (ns + nl) + [ANY] * len(after),
        out_specs=(SEM, SEM, *[HBM] * (ns + nl), pl.BlockSpec(memory_space=pltpu.VMEM)),
        input_output_aliases={i: 2 + i for i in range(ns + nl)},
        compiler_params=pltpu.CompilerParams(has_side_effects=EFFECT),
    )(*[_in_hbm(s) for s in srcs], *[_in_hbm(lax.empty(s.shape, s.dtype)) for s in land_shapes], *after)


def _copies_wait(plan, state, ns, name, after=()):
    send_sems, recv_sems, *arrs = state[:-1]
    n = len(arrs)

    def body(*refs):
        cps = _remote_copies(plan, refs[:ns], refs[ns:n], refs[n], refs[n + 1])
        for cp in cps:
            cp.wait_send()
        for cp in cps:
            cp.wait_recv()

    out = pl.pallas_call(
        body, name=name, out_shape=tuple(pltpu.HBM(a.shape, a.dtype) for a in arrs),
        in_specs=[HBM] * n + [SEM, SEM] + [ANY] * len(after), out_specs=tuple([HBM] * n),
        input_output_aliases={i: i for i in range(n)},
        compiler_params=pltpu.CompilerParams(has_side_effects=EFFECT),
    )(*arrs, send_sems, recv_sems, *after)
    return list(out[:ns]), list(out[ns:])


def _gather_plan(srcs, lands):
    x, y, c, chips = _place()
    peers = [(x, y, 1 - c)] + [(*chip, c) for chip in chips]
    return [(s, l.at[4 * x + 2 * y + c], peer) for s, l in zip(srcs, lands) for peer in peers]


def _scatter_plan(srcs, lands):
    x, y, c, _ = _place()
    peers = [(1 - x if k & 4 else x, 1 - y if k & 2 else y, 1 - c if k & 1 else c) for k in range(1, N_DEV)]
    return [(s.at[4 * px + 2 * py + pc], l.at[4 * x + 2 * y + c], (px, py, pc)) for s, l in zip(srcs, lands) for px, py, pc in peers]


def _everyone_plan(srcs, lands):
    x, y, c, _ = _place()
    peers = [(1 - x if k & 4 else x, 1 - y if k & 2 else y, 1 - c if k & 1 else c) for k in range(1, N_DEV)]
    return [(s, l.at[4 * x + 2 * y + c], peer) for s, l in zip(srcs, lands) for peer in peers]


def _sum_parts(got, own, me, name):
    def body(me_ref, got_ref, own_ref, o_ref):
        acc = jnp.where(me_ref[0] == 0, own_ref[...], got_ref[0])
        for d in range(1, N_DEV):
            acc = acc + jnp.where(me_ref[0] == d, own_ref[...], got_ref[d])
        o_ref[...] = acc

    return pl.pallas_call(
        body, name=name,
        grid_spec=pltpu.PrefetchScalarGridSpec(
            num_scalar_prefetch=1, grid=(1,),
            in_specs=[pl.BlockSpec(got.shape, lambda i, me_ref: (0, 0, 0)), pl.BlockSpec(own.shape, lambda i, me_ref: (0, 0))],
            out_specs=pl.BlockSpec(own.shape, lambda i, me_ref: (0, 0))),
        out_shape=jax.ShapeDtypeStruct(own.shape, F32),
    )(me.astype(jnp.int32).reshape(1), got, own)


def _chip_plan(srcs, lands):
    x, y, c, chips = _place()
    return [(s.at[2 * px + py], l.at[2 * x + y], (px, py, c)) for s, l in zip(srcs, lands) for px, py in chips]


PIECES = 8


def _row_pieces(rows):
    for k in (PIECES, 4, 2):
        if rows % (16 * k) == 0:
            return [pl.ds(q * (rows // k), rows // k) for q in range(k)]
    return [pl.ds(0, rows)]


def _gather_finish(shards, lands, name):
    n = len(shards)
    pieces = [_row_pieces(s.shape[0]) for s in shards]

    def body(*refs):
        ins, zones, outs = refs[:n], refs[n:2 * n], refs[2 * n:3 * n]
        send_sems, recv_sems, local_sems = refs[3 * n:]
        x, y, c, chips = _place()
        mine, cps = [], []
        for a in range(n):
            for q, rows in enumerate(pieces[a]):
                mine.append(pltpu.make_async_copy(ins[a].at[rows], outs[a].at[4 * x + 2 * y + c, rows], local_sems.at[a, q]))
        for cp in mine:
            cp.start()
        for j, (px, py) in enumerate(chips):
            for a in range(n):
                for q, rows in enumerate(pieces[a]):
                    cps.append(pltpu.make_async_remote_copy(
                        src_ref=zones[a].at[4 * px + 2 * py + c, rows], dst_ref=outs[a].at[4 * px + 2 * py + c, rows],
                        send_sem=send_sems.at[a, j, q], recv_sem=recv_sems.at[a, j, q], device_id=(x, y, 1 - c), device_id_type=MESH))
        for cp in cps:
            cp.start()
        for cp in cps:
            cp.wait_recv()
        for cp in cps:
            cp.wait_send()
        for cp in mine:
            cp.wait()

    return pl.pallas_call(
        body, name=name, in_specs=[ANY] * (2 * n), out_specs=[ANY] * n,
        out_shape=[jax.ShapeDtypeStruct(l.shape, l.dtype) for l in lands],
        input_output_aliases={n + a: a for a in range(n)},
        scratch_shapes=[pltpu.SemaphoreType.DMA((n, 3, PIECES)), pltpu.SemaphoreType.DMA((n, 3, PIECES)), pltpu.SemaphoreType.DMA((n, PIECES))],
    )(*shards, *lands)


def _sum_everywhere(v, name):
    rows = v.shape[0]

    def body(v_ref, o_ref, buf, send_sems, recv_sems):
        x, y, c, _ = _place()
        me = 4 * x + 2 * y + c
        buf[me] = v_ref[...]
        cps = []
        for k in range(1, N_DEV):
            fx, fy, fc = (k >> 2) & 1, (k >> 1) & 1, k & 1
            to = (1 - x if fx else x, 1 - y if fy else y, 1 - c if fc else c)
            cps.append(pltpu.make_async_remote_copy(src_ref=buf.at[me], dst_ref=buf.at[me], send_sem=send_sems.at[k - 1],
                                                    recv_sem=recv_sems.at[k - 1], device_id=to, device_id_type=MESH))
        for cp in cps:
            cp.start()
        for cp in cps:
            cp.wait_recv()
        for cp in cps:
            cp.wait_send()
        acc = buf[0]
        for d in range(1, N_DEV):
            acc = acc + buf[d]
        o_ref[...] = acc

    vm = pl.BlockSpec(memory_space=pltpu.VMEM)
    return pl.pallas_call(
        body, name=name, in_specs=[vm], out_specs=vm, out_shape=jax.ShapeDtypeStruct(v.shape, F32),
        scratch_shapes=[pltpu.VMEM((N_DEV, rows, 128), F32), pltpu.SemaphoreType.DMA((N_DEV - 1,)), pltpu.SemaphoreType.DMA((N_DEV - 1,))],
    )(v)


def _adamw_parts(w, got, mine, me, m, v, name, after=()):
    r, c = w.shape
    tr = _pick(r, (128, 64))
    n_parts = got.shape[0]

    def body(me_ref, w_ref, got_ref, own_ref, m_ref, v_ref, *rest):
        go_ref, d_ref, mo_ref, vo_ref = rest[len(after):]
        own = own_ref[...].astype(F32)
        gv = jnp.where(me_ref[0] == 0, own, got_ref[0].astype(F32))
        for d in range(1, n_parts):
            gv = gv + jnp.where(me_ref[0] == d, own, got_ref[d].astype(F32))
        _adamw_math(gv, w_ref, m_ref, v_ref, go_ref, d_ref, mo_ref, vo_ref)

    tile = pl.BlockSpec((tr, c), lambda i, me_ref: (i, 0))
    out = jax.ShapeDtypeStruct((r, c), F32)
    return pl.pallas_call(
        body, name=name,
        grid_spec=pltpu.PrefetchScalarGridSpec(
            num_scalar_prefetch=1, grid=(r // tr,),
            in_specs=[tile, pl.BlockSpec((n_parts, tr, c), lambda i, me_ref: (0, i, 0)),
                      pl.BlockSpec((None, tr, c), lambda i, me_ref: (me_ref[0], i, 0)), tile, tile] + [ANY] * len(after),
            out_specs=[tile] * 4),
        out_shape=[out] * 4, compiler_params=_params(("parallel",)),
    )(me.astype(jnp.int32).reshape(1), w, got, mine, m, v, *after)


def _adamw_math(gv, w_ref, m_ref, v_ref, go_ref, d_ref, mo_ref, vo_ref):
    mn = B1 * m_ref[...] + (1.0 - B1) * gv
    vn = B2 * v_ref[...] + (1.0 - B2) * (gv * gv)
    m_hat = mn / (1.0 - B1 ** STEP)
    v_hat = vn / (1.0 - B2 ** STEP)
    go_ref[...] = gv
    d_ref[...] = -LR * (m_hat / (jnp.sqrt(v_hat) + ADAM_EPS) + WD * w_ref[...])
    mo_ref[...] = mn
    vo_ref[...] = vn


def _adamw(w, g, m, v, name):
    r, c = w.shape
    parts = g.ndim == 3
    tr = r if r <= 128 else _pick(r, (128, 64))

    def body(w_ref, g_ref, m_ref, v_ref, go_ref, d_ref, mo_ref, vo_ref):
        if parts:
            gv = g_ref[0].astype(F32)
            for d in range(1, g.shape[0]):
                gv = gv + g_ref[d].astype(F32)
        else:
            gv = g_ref[...]
        mn = B1 * m_ref[...] + (1.0 - B1) * gv
        vn = B2 * v_ref[...] + (1.0 - B2) * (gv * gv)
        m_hat = mn / (1.0 - B1 ** STEP)
        v_hat = vn / (1.0 - B2 ** STEP)
        go_ref[...] = gv
        d_ref[...] = -LR * (m_hat / (jnp.sqrt(v_hat) + ADAM_EPS) + WD * w_ref[...])
        mo_ref[...] = mn
        vo_ref[...] = vn

    tile = pl.BlockSpec((tr, c), lambda i: (i, 0))
    g_spec = pl.BlockSpec((g.shape[0], tr, c), lambda i: (0, i, 0)) if parts else tile
    out = jax.ShapeDtypeStruct((r, c), F32)
    return pl.pallas_call(
        body, name=name, grid=(r // tr,), in_specs=[tile, g_spec, tile, tile], out_specs=[tile] * 4, out_shape=[out] * 4,
        compiler_params=_params(("parallel",)),
    )(w, g, m, v)


BIG = ["w_in", "w_a_out", "w_b_out", "w_mix_out", "w_ff_gate", "w_ff_up", "w_ff_down", "w_ple_gate", "w_ple_proj"]
COL_SHARDED = ["w_in", "w_a_out", "w_b_out", "w_ff_gate", "w_ff_up", "w_ple_proj"]
SMALL = ["conv_w", "w_alpha_up", "b_alpha_up", "gla_head_gain", "g_pre_mix", "g_post_mix", "g_pre_ffn", "g_post_ffn", "g_pre_ple", "g_post_ple"]
WEIGHTS = ["w_in", "conv_w", "w_a_out", "w_alpha_up", "b_alpha_up", "gla_head_gain", "w_b_out", "w_mix_out", "g_pre_mix", "g_post_mix",
           "g_pre_ffn", "g_post_ffn", "w_ff_gate", "w_ff_up", "w_ff_down", "g_pre_ple", "g_post_ple", "w_ple_gate", "w_ple_proj"]


def _cols_to_full(g8):
    n, r, c = g8.shape
    return jnp.transpose(g8, (1, 0, 2)).reshape(r, n * c)


def _full_to_cols(a):
    r, c = a.shape
    return jnp.transpose(a.reshape(r, N_DEV, c // N_DEV), (1, 0, 2))


def _pack(arrs, rows):
    flat = jnp.concatenate([a.reshape(-1) for a in arrs])
    return jnp.pad(flat, (0, rows * 128 - flat.shape[0])).reshape(rows, 128)


def _unpack(packed, shapes):
    flat, out, o = packed.reshape(-1), [], 0
    for s in shapes:
        size = 1
        for d in s:
            size *= d
        out.append(flat[o:o + size].reshape(s))
        o += size
    return out


def kernel(x, p, w_in, conv_w, w_a_out, w_alpha_up, b_alpha_up, gla_head_gain, w_b_out, w_mix_out, g_pre_mix, g_post_mix, g_pre_ffn, g_post_ffn, w_ff_gate, w_ff_up, w_ff_down, g_pre_ple, g_post_ple, w_ple_gate, w_ple_proj, loss_target, m_w_in, m_conv_w, m_w_a_out, m_w_alpha_up, m_b_alpha_up, m_gla_head_gain, m_w_b_out, m_w_mix_out, m_g_pre_mix, m_g_post_mix, m_g_pre_ffn, m_g_post_ffn, m_w_ff_gate, m_w_ff_up, m_w_ff_down, m_g_pre_ple, m_g_post_ple, m_w_ple_gate, m_w_ple_proj, v_w_in, v_conv_w, v_w_a_out, v_w_alpha_up, v_b_alpha_up, v_gla_head_gain, v_w_b_out, v_w_mix_out, v_g_pre_mix, v_g_post_mix, v_g_pre_ffn, v_g_post_ffn, v_w_ff_gate, v_w_ff_up, v_w_ff_down, v_g_pre_ple, v_g_post_ple, v_w_ple_gate, v_w_ple_proj):
    args = dict(locals())
    wts = {n: args[n][0] for n in WEIGHTS}
    mom = {n: args["m_" + n][0] for n in WEIGHTS}
    var = {n: args["v_" + n][0] for n in WEIGHTS}
    me = 4 * lax.axis_index("x") + 2 * lax.axis_index("y") + lax.axis_index("c")

    groups = [["w_in", "conv_w", "w_alpha_up"], ["w_a_out", "w_b_out", "w_mix_out"], ["w_ff_gate", "w_ff_up"], ["w_ff_down"],
              ["w_ple_gate", "w_ple_proj"]]
    grad_groups = [["w_ple_proj", "w_ple_gate"], ["w_ff_down", "w_ff_gate", "w_ff_up"], ["w_mix_out", "w_a_out", "w_b_out"], ["w_in"]]
    rows_full = lambda g: g.reshape(-1, g.shape[-1])
    gathers, scatters = {}, {}

    def gather_start(gi, after):
        shards = [wts[n].astype(BF16) if n in BIG else wts[n] for n in groups[gi]]
        zones = [jax.ShapeDtypeStruct((N_DEV,) + s.shape, s.dtype) for s in shards]
        gathers[gi] = (shards, _copies_start(_gather_plan, 4 * len(shards), shards, zones, "gather_start_%d" % gi, after))
        return gathers[gi][1][-1]

    def gather_finish(gi, after):
        shards, state = gathers[gi]
        shards, zones = _copies_wait(_gather_plan, state, len(shards), "gather_wait_%d" % gi, after)
        g8 = dict(zip(groups[gi], _gather_finish(shards, zones, "gather_finish_%d" % gi)))
        if gi == 0:
            w_in_full = _cols_to_full(g8["w_in"])
            return dict(in_p=jnp.concatenate([w_in_full[:, R_GA:R_END], w_in_full[:, :R_ALR], w_in_full[:, R_ALR:R_GA],
                                              jnp.zeros((w_in_full.shape[0], 128 - GATE_RANK), BF16)], axis=1),
                        conv_w=_cols_to_full(g8["conv_w"]), w_alpha_up=_cols_to_full(g8["w_alpha_up"]))
        if gi == 1:
            return dict(a_out=_cols_to_full(g8["w_a_out"]), b_out=_cols_to_full(g8["w_b_out"]), mix=rows_full(g8["w_mix_out"]))
        if gi == 2:
            return dict(gu=jnp.concatenate([_cols_to_full(g8["w_ff_gate"]), _cols_to_full(g8["w_ff_up"])], axis=1))
        if gi == 3:
            return dict(down=rows_full(g8["w_ff_down"]))
        return dict(pg=rows_full(g8["w_ple_gate"]), pp=_cols_to_full(g8["w_ple_proj"]))

    def scatter_start(gi, gw):
        if gi == 3:
            g_in = gw["in_p"]
            gfull = dict(w_in=jnp.concatenate([g_in[:, C_AX:C_ALR], g_in[:, C_ALR:C_ALR + GATE_RANK], g_in[:, :C_AX]], axis=1))
        elif gi == 1:
            f_gate = gw["gu"].shape[1] // 2
            gfull = dict(w_ff_down=gw["down"], w_ff_gate=gw["gu"][:, :f_gate], w_ff_up=gw["gu"][:, f_gate:])
        elif gi == 2:
            gfull = dict(w_mix_out=gw["mix"], w_a_out=gw["a_out"], w_b_out=gw["b_out"])
        else:
            gfull = dict(w_ple_proj=gw["pp"], w_ple_gate=gw["pg"])
        parts = [_full_to_cols(gfull[n]) if n in COL_SHARDED else gfull[n].reshape(N_DEV, -1, gfull[n].shape[-1]) for n in grad_groups[gi]]
        if gi == 3:
            from_sibling = _sibling_exchange(parts, "scatter_sibling_%d" % gi)
            parts = [_pair_add(a, b, "scatter_add_%d_%s" % (gi, n)) for n, a, b in zip(grad_groups[gi], parts, from_sibling)]
            scatters[gi] = _copies_start(_chip_plan, 3 * len(parts), parts, parts, "scatter_start_%d" % gi)
        else:
            scatters[gi] = _copies_start(_scatter_plan, (N_DEV - 1) * len(parts), parts, parts, "scatter_start_%d" % gi)
        return scatters[gi][-1]

    small = {n: wts[n].reshape(1, -1) for n in SMALL[2:]}

    loss_rows, grad_x, gs = _local_step(x[0], p[0, 0], loss_target[0], gather_start, gather_finish, scatter_start, small)
    loss = lax.psum(jnp.sum(loss_rows), ("x", "y", "c"))

    small_shapes = [gs[n].shape for n in SMALL]
    gs_packed = _pack([gs[n] for n in SMALL], 192)
    small_state = _copies_start(_everyone_plan, N_DEV - 1, [gs_packed], [jax.ShapeDtypeStruct((N_DEV,) + gs_packed.shape, F32)],
                                "small_start", (grad_x,))

    res, done = {}, (small_state[-1],)
    for gi, names in enumerate(grad_groups):
        plan, slot = (_chip_plan, me // 2) if gi == 3 else (_scatter_plan, me)
        mine, got = _copies_wait(plan, scatters[gi], len(names), "scatter_wait_%d" % gi, done)
        for n, g, own in zip(names, got, mine):
            res[n] = _adamw_parts(wts[n], g, own, slot, mom[n], var[n], "adamw_" + n)
        done = tuple(res[n][1] for n in names)

    (gs_own,), (gs_got,) = _copies_wait(_everyone_plan, small_state, 1, "small_wait", done)
    gsum = dict(zip(SMALL, _unpack(_sum_parts(gs_got, gs_own, me, "small_sum"), small_shapes)))
    gsum["conv_w"] = lax.dynamic_index_in_dim(gsum["conv_w"].reshape(3, N_DEV, -1), me, axis=1, keepdims=False)
    gsum["w_alpha_up"] = lax.dynamic_index_in_dim(gsum["w_alpha_up"].reshape(GATE_RANK, N_DEV, -1), me, axis=1, keepdims=False)

    shard_shapes = [wts[n].shape for n in SMALL]
    packed = [_pack([d[n] for n in SMALL], 120) for d in (wts, gsum, mom, var)]
    outs = [_unpack(o, shard_shapes) for o in _adamw(*packed, "adamw_small")]
    for i, n in enumerate(SMALL):
        res[n] = [o[i] for o in outs]

    lead = lambda a: a[None]
    return (loss, grad_x[None], *[lead(res[n][0]) for n in WEIGHTS], *[lead(res[n][1]) for n in WEIGHTS],
            *[lead(res[n][2]) for n in WEIGHTS], *[lead(res[n][3]) for n in WEIGHTS])
```

```python
import jax
import jax.numpy as jnp
from jax import lax
from jax.experimental import pallas as pl
from jax.experimental.pallas import tpu as pltpu
from jax.experimental.pallas import tpu_sc as plsc

F32, BF16 = jnp.float32, jnp.bfloat16
EPS = 1e-6
CHUNK = 64
HEADS, DK, DV = 4, 128, 256
GATE_RANK = 16
TAU = 16.0
LR, B1, B2, ADAM_EPS, WD, STEP = 0.001, 0.9, 0.999, 1e-08, 0.01, 10
N_DEV = 8
MESH = pl.DeviceIdType.MESH
VMEM_LIMIT = 56 * 1024 * 1024
ANY = pl.BlockSpec(memory_space=pl.ANY)

C_GA, C_GB, C_AX, C_AB, C_AC, C_Q, C_K, C_V, C_OG, C_ALR = 0, 2048, 4096, 5120, 6144, 7168, 7680, 8192, 9216, 10240
IN_PAD = 10368
R_AX, R_AB, R_AC, R_Q, R_K, R_V, R_OG, R_ALR, R_GA, R_GB, R_END = 0, 1024, 2048, 3072, 3584, 4096, 5120, 6144, 6160, 8208, 10256


def _params(sem):
    return pltpu.CompilerParams(dimension_semantics=sem, vmem_limit_bytes=VMEM_LIMIT)


def _pick(n, cands):
    for c in cands:
        if n % c == 0:
            return c
    return n


def _mm(a, b, mode, out_dtype, name, after=()):
    if mode == "nn":
        (m, k), (k2, n) = a.shape, b.shape
    elif mode == "nt":
        (m, k), (n, k2) = a.shape, b.shape
    else:
        (k, m), (k2, n) = a.shape, b.shape
    assert k == k2 and a.dtype == BF16 and b.dtype == BF16, (name, a.shape, b.shape, a.dtype, b.dtype)
    tm = _pick(m, (1024, 512, 256))
    tn = _pick(n, (1152, 1024, 1408, 512, 256))
    tk = _pick(k, (512, 1152, 256))
    nk = k // tk
    dims = {"nn": (((1,), (0,)), ((), ())), "nt": (((1,), (1,)), ((), ())), "tn": (((0,), (0,)), ((), ()))}[mode]

    def body(a_ref, b_ref, *rest):
        o_ref, acc_ref = rest[len(after):]
        kk = pl.program_id(2)

        @pl.when(kk == 0)
        def _():
            acc_ref[...] = jnp.zeros_like(acc_ref)

        acc_ref[...] += lax.dot_general(a_ref[...], b_ref[...], dims, preferred_element_type=F32)

        @pl.when(kk == nk - 1)
        def _():
            o_ref[...] = acc_ref[...].astype(o_ref.dtype)

    a_spec = pl.BlockSpec((tk, tm), lambda i, j, kk: (kk, i)) if mode == "tn" else pl.BlockSpec((tm, tk), lambda i, j, kk: (i, kk))
    b_spec = pl.BlockSpec((tn, tk), lambda i, j, kk: (j, kk)) if mode == "nt" else pl.BlockSpec((tk, tn), lambda i, j, kk: (kk, j))
    return pl.pallas_call(
        body, name=name, grid=(m // tm, n // tn, nk),
        in_specs=[a_spec, b_spec] + [ANY] * len(after), out_specs=pl.BlockSpec((tm, tn), lambda i, j, kk: (i, j)),
        out_shape=jax.ShapeDtypeStruct((m, n), out_dtype),
        scratch_shapes=[pltpu.VMEM((tm, tn), F32)],
        compiler_params=_params(("parallel", "parallel", "arbitrary")),
    )(a, b, *after)


def _rows(body, t, tr, ins, outs, name):
    in_specs = []
    for arr, sp in ins:
        if sp[0] == "t":
            in_specs.append(pl.BlockSpec((tr, sp[1]), lambda i, cb=sp[2]: (i, cb)))
        else:
            in_specs.append(pl.BlockSpec(arr.shape, lambda i, nd=arr.ndim: (0,) * nd))
    out_specs, out_shape = [], []
    for shape, dt, kind in outs:
        out_specs.append(pl.BlockSpec((tr, shape[1]), lambda i: (i, 0)) if kind == "t" else pl.BlockSpec(shape, lambda i: (0, 0)))
        out_shape.append(jax.ShapeDtypeStruct(shape, dt))
    return pl.pallas_call(
        body, name=name, grid=(t // tr,), in_specs=in_specs, out_specs=out_specs, out_shape=out_shape,
        compiler_params=_params(("arbitrary",)),
    )(*[arr for arr, _ in ins])


def _rinv(v):
    return lax.rsqrt(jnp.mean(v * v, axis=-1, keepdims=True) + EPS)


def _sig(v):
    return 1.0 / (1.0 + jnp.exp(-v))


def _acc(ref, val):
    @pl.when(pl.program_id(0) == 0)
    def _():
        ref[...] = jnp.zeros_like(ref)

    ref[...] += jnp.sum(val, axis=0, keepdims=True)


def _rms_fwd(x, g, name):
    t, d = x.shape

    def body(x_ref, g_ref, h_ref):
        xv = x_ref[...]
        h_ref[...] = (xv * _rinv(xv) * g_ref[...]).astype(BF16)

    return _rows(body, t, 256, [(x, ("t", d, 0)), (g, ("b",))], [((t, d), BF16, "t")], name)[0]


def _post_pre(x, m, g_post, g_pre, name):
    t, d = x.shape

    def body(x_ref, m_ref, gp_ref, gn_ref, xo_ref, h_ref):
        mv = m_ref[...]
        xn = x_ref[...] + mv * _rinv(mv) * gp_ref[...]
        xo_ref[...] = xn
        h_ref[...] = (xn * _rinv(xn) * gn_ref[...]).astype(BF16)

    return _rows(body, t, 128, [(x, ("t", d, 0)), (m, ("t", d, 0)), (g_post, ("b",)), (g_pre, ("b",))],
                 [((t, d), F32, "t"), ((t, d), BF16, "t")], name)


def _mix_fwd(proj, ya, yb, name):
    t, d = ya.shape

    def body(ga_ref, gb_ref, ya_ref, yb_ref, o_ref):
        o_ref[...] = (_sig(ga_ref[...].astype(F32)) * ya_ref[...].astype(F32)
                      + _sig(gb_ref[...].astype(F32)) * yb_ref[...].astype(F32)).astype(BF16)

    return _rows(body, t, 256, [(proj, ("t", d, C_GA // d)), (proj, ("t", d, C_GB // d)), (ya, ("t", d, 0)), (yb, ("t", d, 0))],
                 [((t, d), BF16, "t")], name)[0]


def _mix_bwd(dmix, proj, ya, yb, name):
    t, d = ya.shape

    def body(dm_ref, ga_ref, gb_ref, ya_ref, yb_ref, dg_ref, dya_ref, dyb_ref):
        dm = dm_ref[...]
        sa, sb = _sig(ga_ref[...].astype(F32)), _sig(gb_ref[...].astype(F32))
        dg_ref[:, :d] = (dm * ya_ref[...].astype(F32) * sa * (1.0 - sa)).astype(BF16)
        dg_ref[:, d:] = (dm * yb_ref[...].astype(F32) * sb * (1.0 - sb)).astype(BF16)
        dya_ref[...] = (dm * sa).astype(BF16)
        dyb_ref[...] = (dm * sb).astype(BF16)

    return _rows(body, t, 128,
                 [(dmix, ("t", d, 0)), (proj, ("t", d, C_GA // d)), (proj, ("t", d, C_GB // d)), (ya, ("t", d, 0)), (yb, ("t", d, 0))],
                 [((t, 2 * d), BF16, "t"), ((t, d), BF16, "t"), ((t, d), BF16, "t")], name)


def _swiglu_fwd(fgu, name):
    t, f2 = fgu.shape
    f = f2 // 2
    tc = _pick(f, (1408, 512))
    nc = f // tc

    def body(g_ref, u_ref, s_ref):
        gv = g_ref[...].astype(F32)
        s_ref[...] = (gv * _sig(gv) * u_ref[...].astype(F32)).astype(BF16)

    return pl.pallas_call(
        body, name=name, grid=(t // 512, nc),
        in_specs=[pl.BlockSpec((512, tc), lambda i, j: (i, j)), pl.BlockSpec((512, tc), lambda i, j: (i, j + nc))],
        out_specs=pl.BlockSpec((512, tc), lambda i, j: (i, j)), out_shape=jax.ShapeDtypeStruct((t, f), BF16),
        compiler_params=_params(("parallel", "parallel")),
    )(fgu, fgu)


def _swiglu_bwd(ds, fgu, name):
    t, f2 = fgu.shape
    f = f2 // 2
    tc = _pick(f, (1408, 512))
    nc = f // tc

    def body(ds_ref, g_ref, u_ref, o_ref):
        dsv, gv, uv = ds_ref[...].astype(F32), g_ref[...].astype(F32), u_ref[...].astype(F32)
        sg = _sig(gv)
        dfg = dsv * uv * sg * (1.0 + gv * (1.0 - sg))
        dfu = dsv * gv * sg
        o_ref[...] = jnp.where(pl.program_id(1) < nc, dfg, dfu).astype(BF16)

    return pl.pallas_call(
        body, name=name, grid=(t // 512, 2 * nc),
        in_specs=[pl.BlockSpec((512, tc), lambda i, j: (i, j % nc)), pl.BlockSpec((512, tc), lambda i, j: (i, j % nc)),
                  pl.BlockSpec((512, tc), lambda i, j: (i, j % nc + nc))],
        out_specs=pl.BlockSpec((512, tc), lambda i, j: (i, j)), out_shape=jax.ShapeDtypeStruct((t, f2), BF16),
        compiler_params=_params(("parallel", "parallel")),
    )(ds, fgu, fgu)


def _ple_final(x2, pg, pp, tgt, g_post, name):
    t, d = x2.shape

    def body(x_ref, pg_ref, pp_ref, t_ref, g_ref, loss_ref, d3_ref, dpg_ref, dpp_ref, dg_ref):
        sg, ppv, g = _sig(pg_ref[...]), pp_ref[...], g_ref[...]
        e = sg * ppv
        r = _rinv(e)
        eh = e * r
        diff = x_ref[...] + eh * g - t_ref[...]
        loss_ref[...] = 0.5 * jnp.mean(diff * diff, axis=-1, keepdims=True)
        d3 = diff * (1.0 / d)
        d3_ref[...] = d3
        gd = d3 * g
        de = r * (gd - eh * jnp.mean(gd * eh, axis=-1, keepdims=True))
        dpg_ref[...] = (de * ppv * sg * (1.0 - sg)).astype(BF16)
        dpp_ref[...] = (de * sg).astype(BF16)
        _acc(dg_ref, d3 * eh)

    return _rows(body, t, 128, [(x2, ("t", d, 0)), (pg, ("t", d, 0)), (pp, ("t", d, 0)), (tgt, ("t", d, 0)), (g_post, ("b",))],
                 [((t, 1), F32, "t"), ((t, d), F32, "t"), ((t, d), BF16, "t"), ((t, d), BF16, "t"), ((1, d), F32, "a")], name)


def _norm_bwd(dn, dh, x, g_pre, fm, g_post, name):
    t, d = x.shape
    two = fm is not None

    def body(*refs):
        if two:
            dn_ref, dh_ref, x_ref, gp_ref, f_ref, gq_ref, dx_ref, df_ref, dgp_ref, dgq_ref = refs
        else:
            dn_ref, dh_ref, x_ref, gp_ref, dx_ref, dgp_ref = refs
        xv, dhv = x_ref[...], dh_ref[...]
        r = _rinv(xv)
        xh = xv * r
        gd = dhv * gp_ref[...]
        dx = dn_ref[...] + r * (gd - xh * jnp.mean(gd * xh, axis=-1, keepdims=True))
        dx_ref[...] = dx
        _acc(dgp_ref, dhv * xh)
        if two:
            fv = f_ref[...]
            rf = _rinv(fv)
            fh = fv * rf
            gd2 = dx * gq_ref[...]
            df_ref[...] = (rf * (gd2 - fh * jnp.mean(gd2 * fh, axis=-1, keepdims=True))).astype(BF16)
            _acc(dgq_ref, dx * fh)

    ins = [(dn, ("t", d, 0)), (dh, ("t", d, 0)), (x, ("t", d, 0)), (g_pre, ("b",))]
    outs = [((t, d), F32, "t")]
    if two:
        ins += [(fm, ("t", d, 0)), (g_post, ("b",))]
        outs += [((t, d), BF16, "t"), ((1, d), F32, "a"), ((1, d), F32, "a")]
    else:
        outs += [((1, d), F32, "a")]
    return _rows(body, t, 128, ins, outs, name)


CONV_TC = 256


def _shift_down(v, s):
    rows = lax.broadcasted_iota(jnp.int32, v.shape, 0)
    return jnp.where(rows >= s, pltpu.roll(v, s, 0), 0.0)


def _shift_up(v, s):
    n = v.shape[0]
    rows = lax.broadcasted_iota(jnp.int32, v.shape, 0)
    return jnp.where(rows < n - s, pltpu.roll(v, n - s, 0), 0.0)


def _conv_specs(t):
    nb = 1024 // CONV_TC
    seg = lambda c0: pl.BlockSpec((t, CONV_TC), lambda j, cb=c0 // CONV_TC: (0, cb + j))
    own = pl.BlockSpec((t, CONV_TC), lambda j: (0, j))
    wspec = pl.BlockSpec((3, CONV_TC), lambda j: (0, j))
    return nb, seg, own, wspec


def _conv_fwd(proj, conv_w, name, after=()):
    t = proj.shape[0]
    nb, seg, own, wspec = _conv_specs(t)

    def body(ax_ref, ab_ref, ac_ref, w_ref, *rest):
        za_ref = rest[len(after)]
        u = ac_ref[...].astype(F32) * ax_ref[...].astype(F32)
        w = w_ref[...]
        yc = w[0:1] * _shift_down(u, 2) + w[1:2] * _shift_down(u, 1) + w[2:3] * u
        za_ref[...] = (ab_ref[...].astype(F32) * yc).astype(BF16)

    return pl.pallas_call(
        body, name=name, grid=(nb,), in_specs=[seg(C_AX), seg(C_AB), seg(C_AC), wspec] + [ANY] * len(after), out_specs=own,
        out_shape=jax.ShapeDtypeStruct((t, 1024), BF16), compiler_params=_params(("parallel",)),
    )(proj, proj, proj, conv_w, *after)


def _conv_bwd(dza, proj, conv_w, name):
    t = proj.shape[0]
    nb, seg, own, wspec = _conv_specs(t)

    def body(dz_ref, ax_ref, ab_ref, ac_ref, w_ref, dax_ref, dab_ref, dac_ref, dw_ref):
        ax, ab, ac, dz = ax_ref[...].astype(F32), ab_ref[...].astype(F32), ac_ref[...].astype(F32), dz_ref[...].astype(F32)
        w = w_ref[...]
        u = ac * ax
        u1, u2 = _shift_down(u, 1), _shift_down(u, 2)
        yc = w[0:1] * u2 + w[1:2] * u1 + w[2:3] * u
        dab_ref[...] = (dz * yc).astype(BF16)
        dyc = dz * ab
        du = w[2:3] * dyc + w[1:2] * _shift_up(dyc, 1) + w[0:1] * _shift_up(dyc, 2)
        dax_ref[...] = (du * ac).astype(BF16)
        dac_ref[...] = (du * ax).astype(BF16)
        dw_ref[0:1, :] = jnp.sum(dyc * u2, axis=0, keepdims=True)
        dw_ref[1:2, :] = jnp.sum(dyc * u1, axis=0, keepdims=True)
        dw_ref[2:3, :] = jnp.sum(dyc * u, axis=0, keepdims=True)

    act = jax.ShapeDtypeStruct((t, 1024), BF16)
    return pl.pallas_call(
        body, name=name, grid=(nb,), in_specs=[own, seg(C_AX), seg(C_AB), seg(C_AC), wspec], out_specs=[own, own, own, wspec],
        out_shape=[act, act, act, jax.ShapeDtypeStruct((3, 1024), F32)], compiler_params=_params(("parallel",)),
    )(dza, proj, proj, proj, conv_w)


def _dot(a, b, dims, precision=None):
    return lax.dot_general(a, b, (dims, ((), ())), precision=precision, preferred_element_type=F32)


def _gla_chunk(q, k, v, og, alr, s_in, wa, ba, gain):
    c = q.shape[0]
    hi = lax.Precision.HIGHEST
    z = _dot(alr, wa, ((1,), (0,))) + ba
    la = (jnp.minimum(z, 0.0) - jnp.log(1.0 + jnp.exp(-jnp.abs(z)))) * (1.0 / TAU)
    row = lax.broadcasted_iota(jnp.int32, (c, c), 0)
    col = lax.broadcasted_iota(jnp.int32, (c, c), 1)
    lower = row >= col
    b = _dot(lower.astype(F32), la, ((1,), (0,)), hi)
    trow = lax.broadcasted_iota(jnp.int32, la.shape, 0)
    mid = jnp.sum(jnp.where(trow <= c // 2, la, 0.0), axis=0, keepdims=True)
    blast = jnp.sum(la, axis=0, keepdims=True)
    qs = q * (DK ** -0.5)
    e_up, e_dn = jnp.exp(b - mid), jnp.exp(mid - b)
    a_fwd = _dot(qs * e_up, k * e_dn, ((1,), (1,)))
    a_rev = _dot(qs * e_dn, k * e_up, ((1,), (1,)))
    att = jnp.where(lower, a_fwd, a_rev)
    o = _dot(att, v, ((1,), (0,))) + _dot(qs * jnp.exp(b), s_in, ((1,), (0,)))
    upd = _dot(k * jnp.exp(blast - b), v, ((0,), (0,)))
    blast_col = _dot(la, jnp.ones((c, DV), F32), ((0,), (0,)), hi)
    s_out = jnp.exp(blast_col) * s_in + upd
    on = o * _rinv(o) * gain
    return on * og * _sig(og), s_out


def _gla_specs(t, rev):
    n = t // CHUNK
    ch = (lambda i: n - 1 - i) if rev else (lambda i: i)
    col = lambda w, c0: pl.BlockSpec((CHUNK, w), lambda i, h, cb=c0 // w: (ch(i), cb + h))
    specs = dict(
        q=col(DK, C_Q), k=col(DK, C_K), v=col(DV, C_V), og=col(DV, C_OG),
        alr=pl.BlockSpec((CHUNK, 128), lambda i, h: (ch(i), C_ALR // 128)),
        wa=pl.BlockSpec((128, DK), lambda i, h: (0, h)), ba=pl.BlockSpec((1, DK), lambda i, h: (0, h)),
        gain=pl.BlockSpec((1, DV), lambda i, h: (0, 0)),
        state=pl.BlockSpec((None, None, DK, DV), lambda i, h: (ch(i), h, 0, 0)),
        odk=pl.BlockSpec((CHUNK, DK), lambda i, h: (ch(i), h)), odv=pl.BlockSpec((CHUNK, DV), lambda i, h: (ch(i), h)),
        oalr=pl.BlockSpec((CHUNK, 128), lambda i, h: (ch(i), 0)),
    )
    return n, specs


def _gla_fwd(proj, wa, ba, gain, name):
    t = proj.shape[0]
    n, sp = _gla_specs(t, False)

    def body(q_ref, k_ref, v_ref, og_ref, alr_ref, wa_ref, ba_ref, g_ref, zb_ref, st_ref, s_scr):
        h = pl.program_id(1)

        @pl.when(pl.program_id(0) == 0)
        def _():
            s_scr[h] = jnp.zeros((DK, DV), F32)

        s_in = s_scr[h]
        st_ref[...] = s_in
        zb, s_out = _gla_chunk(q_ref[...].astype(F32), k_ref[...].astype(F32), v_ref[...].astype(F32), og_ref[...].astype(F32),
                               alr_ref[...].astype(F32), s_in, wa_ref[...].astype(F32), ba_ref[...], g_ref[...])
        zb_ref[...] = zb.astype(BF16)
        s_scr[h] = s_out

    return pl.pallas_call(
        body, name=name, grid=(n, HEADS),
        in_specs=[sp["q"], sp["k"], sp["v"], sp["og"], sp["alr"], sp["wa"], sp["ba"], sp["gain"]],
        out_specs=[sp["odv"], sp["state"]],
        out_shape=[jax.ShapeDtypeStruct((t, HEADS * DV), BF16), jax.ShapeDtypeStruct((n, HEADS, DK, DV), F32)],
        scratch_shapes=[pltpu.VMEM((HEADS, DK, DV), F32)],
        compiler_params=_params(("arbitrary", "arbitrary")),
    )(proj, proj, proj, proj, proj, wa, ba, gain)


def _gla_bwd(dzb, proj, states, wa, ba, gain, name):
    t = proj.shape[0]
    n, sp = _gla_specs(t, True)

    def body(dz_ref, q_ref, k_ref, v_ref, og_ref, alr_ref, st_ref, wa_ref, ba_ref, g_ref,
             dq_ref, dk_ref, dv_ref, dog_ref, dalr_ref, dwa_ref, dba_ref, dg_ref, ds_scr):
        i, h = pl.program_id(0), pl.program_id(1)

        @pl.when(i == 0)
        def _():
            ds_scr[h] = jnp.zeros((DK, DV), F32)
            dwa_ref[h] = jnp.zeros((128, DK), F32)
            dba_ref[h] = jnp.zeros((1, DK), F32)

        @pl.when((i == 0) & (h == 0))
        def _():
            dg_ref[...] = jnp.zeros_like(dg_ref)

        args = (q_ref[...].astype(F32), k_ref[...].astype(F32), v_ref[...].astype(F32), og_ref[...].astype(F32),
                alr_ref[...].astype(F32), st_ref[...], wa_ref[...].astype(F32), ba_ref[...], g_ref[...])
        _, vjp = jax.vjp(_gla_chunk, *args)
        dq, dk, dv, dog, dalr, ds_in, dwa, dba, dgain = vjp((dz_ref[...].astype(F32), ds_scr[h]))
        dq_ref[...] = dq.astype(BF16)
        dk_ref[...] = dk.astype(BF16)
        dv_ref[...] = dv.astype(BF16)
        dog_ref[...] = dog.astype(BF16)

        @pl.when(h == 0)
        def _():
            dalr_ref[...] = dalr

        @pl.when(h != 0)
        def _():
            dalr_ref[...] += dalr

        ds_scr[h] = ds_in
        dwa_ref[h] += dwa
        dba_ref[h] += dba
        dg_ref[...] += dgain

    whole = lambda shape: pl.BlockSpec(shape, lambda i, h, nd=len(shape): (0,) * nd)
    return pl.pallas_call(
        body, name=name, grid=(n, HEADS),
        in_specs=[sp["odv"], sp["q"], sp["k"], sp["v"], sp["og"], sp["alr"], sp["state"], sp["wa"], sp["ba"], sp["gain"]],
        out_specs=[sp["odk"], sp["odk"], sp["odv"], sp["odv"], sp["oalr"], whole((HEADS, 128, DK)), whole((HEADS, 1, DK)), whole((1, DV))],
        out_shape=[jax.ShapeDtypeStruct((t, HEADS * DK), BF16), jax.ShapeDtypeStruct((t, HEADS * DK), BF16),
                   jax.ShapeDtypeStruct((t, HEADS * DV), BF16), jax.ShapeDtypeStruct((t, HEADS * DV), BF16),
                   jax.ShapeDtypeStruct((t, 128), F32), jax.ShapeDtypeStruct((HEADS, 128, DK), F32),
                   jax.ShapeDtypeStruct((HEADS, 1, DK), F32), jax.ShapeDtypeStruct((1, DV), F32)],
        scratch_shapes=[pltpu.VMEM((HEADS, DK, DV), F32)],
        compiler_params=_params(("arbitrary", "arbitrary")),
    )(dzb, proj, proj, proj, proj, proj, states, wa, ba, gain)


def _local_step(x, p, tgt, gather_start, gather_finish, scatter_start, small):
    b_alpha, gain = small["b_alpha_up"], small["gla_head_gain"]
    gather_start(0, ())
    w = dict(gather_finish(0, ()))
    conv_w, w_alpha = w["conv_w"], w["w_alpha_up"]
    wa_p = jnp.zeros((128, HEADS * DK), BF16).at[:GATE_RANK].set(w_alpha.astype(BF16))

    t1 = gather_start(1, (w["in_p"],))
    h1 = _rms_fwd(x, small["g_pre_mix"], "rms_pre_mix")
    proj = _mm(h1, w["in_p"], "nn", BF16, "mm_proj", after=(t1,))
    t2 = gather_start(2, (proj,))
    za = _conv_fwd(proj, conv_w, "conv_fwd", after=(t2,))
    zb, states = _gla_fwd(proj, wa_p, b_alpha, gain, "gla_fwd")
    t3 = gather_start(3, (zb, za))
    w.update(gather_finish(1, (t3,)))
    ya = _mm(za, w["a_out"], "nn", BF16, "mm_ya")
    yb = _mm(zb, w["b_out"], "nn", BF16, "mm_yb")
    mix = _mix_fwd(proj, ya, yb, "mix_fwd")
    m2 = _mm(mix, w["mix"], "nn", F32, "mm_mix")
    t4 = gather_start(4, (m2,))
    x1, h2 = _post_pre(x, m2, small["g_post_mix"], small["g_pre_ffn"], "norm_mix_ffn")
    w.update(gather_finish(2, (h2, t4)))
    fgu = _mm(h2, w["gu"], "nn", BF16, "mm_gu")
    s = _swiglu_fwd(fgu, "swiglu_fwd")
    w.update(gather_finish(3, (s,)))
    f = _mm(s, w["down"], "nn", F32, "mm_down")
    x2, h3 = _post_pre(x1, f, small["g_post_ffn"], small["g_pre_ple"], "norm_ffn_ple")
    w.update(gather_finish(4, (h3,)))
    pg = _mm(h3, w["pg"], "nn", F32, "mm_pg")
    p_bf = p.astype(BF16)
    pp = _mm(p_bf, w["pp"], "nn", F32, "mm_pp")
    loss_rows, d3, dpg, dpp, dg_post_ple = _ple_final(x2, pg, pp, tgt, small["g_post_ple"], "ple_final")

    gw = {}
    gw["pp"] = _mm(p_bf, dpp, "tn", BF16, "mm_dw_pp")
    gw["pg"] = _mm(h3, dpg, "tn", BF16, "mm_dw_pg")
    dh3 = _mm(dpg, w["pg"], "nt", F32, "mm_dh3", after=(scatter_start(0, gw),))
    d2, df, dg_pre_ple, dg_post_ffn = _norm_bwd(d3, dh3, x2, small["g_pre_ple"], f, small["g_post_ffn"], "norm_bwd_ple_ffn")
    ds = _mm(df, w["down"], "nt", BF16, "mm_ds")
    gw["down"] = _mm(s, df, "tn", BF16, "mm_dw_down")
    dfgu = _swiglu_bwd(ds, fgu, "swiglu_bwd")
    gw["gu"] = _mm(h2, dfgu, "tn", BF16, "mm_dw_gu", after=(gw["down"],))
    dh2 = _mm(dfgu, w["gu"], "nt", F32, "mm_dh2", after=(scatter_start(1, gw),))
    d1, dm2, dg_pre_ffn, dg_post_mix = _norm_bwd(d2, dh2, x1, small["g_pre_ffn"], m2, small["g_post_mix"], "norm_bwd_ffn_mix")
    dmix = _mm(dm2, w["mix"], "nt", F32, "mm_dmix")
    gw["mix"] = _mm(mix, dm2, "tn", BF16, "mm_dw_mix")
    dgab, dya, dyb = _mix_bwd(dmix, proj, ya, yb, "mix_bwd")
    dza = _mm(dya, w["a_out"], "nt", BF16, "mm_dza", after=(gw["mix"],))
    gw["a_out"] = _mm(za, dya, "tn", BF16, "mm_dw_a_out")
    gw["b_out"] = _mm(zb, dyb, "tn", BF16, "mm_dw_b_out", after=(gw["a_out"],))
    dzb = _mm(dyb, w["b_out"], "nt", BF16, "mm_dzb", after=(scatter_start(2, gw),))
    dax, dab, dac, dconv = _conv_bwd(dza, proj, conv_w, "conv_bwd")
    dq, dk, dv, dog, dalr, dwa, dba, dgain = _gla_bwd(dzb, proj, states, wa_p, b_alpha, gain, "gla_bwd")
    dproj = jnp.concatenate([dgab, dax, dab, dac, dq, dk, dv, dog, dalr.astype(BF16)], axis=1)
    gw["in_p"] = _mm(h1, dproj, "tn", BF16, "mm_dw_in")
    dh1 = _mm(dproj, w["in_p"], "nt", F32, "mm_dh1", after=(scatter_start(3, gw),))
    grad_x, dg_pre_mix = _norm_bwd(d1, dh1, x, small["g_pre_mix"], None, None, "norm_bwd_mix")

    gs = dict(
        conv_w=dconv,
        w_alpha_up=jnp.transpose(dwa[:, :GATE_RANK, :], (1, 0, 2)).reshape(GATE_RANK, HEADS * DK),
        b_alpha_up=dba.reshape(1, HEADS * DK), gla_head_gain=dgain,
        g_pre_mix=dg_pre_mix, g_post_mix=dg_post_mix, g_pre_ffn=dg_pre_ffn, g_post_ffn=dg_post_ffn,
        g_pre_ple=dg_pre_ple, g_post_ple=dg_post_ple,
    )
    return loss_rows, grad_x, gs


def _place():
    x, y, c = lax.axis_index("x"), lax.axis_index("y"), lax.axis_index("c")
    return x, y, c, [(1 - x, y), (x, 1 - y), (1 - x, 1 - y)]


def _all_gather(shards, name, cid=None):
    n = len(shards)

    def body(*refs):
        ins, outs = refs[:n], refs[n:2 * n]
        send_sems, recv_sems, local_sems = refs[2 * n:]
        x, y, c, chips = _place()
        me, sibling = (x, y, c), (x, y, 1 - c)

        def slot(px, py, pc):
            return 4 * px + 2 * py + pc

        def copy(a, k, block, to, src=None):
            dst = outs[a].at[slot(*block)]
            return pltpu.make_async_remote_copy(src_ref=dst if src is None else src, dst_ref=dst, send_sem=send_sems.at[a, k],
                                                recv_sem=recv_sems.at[a, k], device_id=to, device_id_type=MESH)

        mine = [pltpu.make_async_copy(ins[a], outs[a].at[slot(*me)], local_sems.at[a]) for a in range(n)]
        for cp in mine:
            cp.start()
        first = []
        for j, chip in enumerate(chips):
            first += [copy(a, 1 + j, me, (*chip, c), src=ins[a]) for a in range(n)]
        first += [copy(a, 0, me, sibling, src=ins[a]) for a in range(n)]
        for cp in first:
            cp.start()
        passed = []
        for j, chip in enumerate(chips):
            for a in range(n):
                copy(a, 1 + j, (*chip, c), me).wait_recv()
                cp = copy(a, 4 + j, (*chip, c), sibling)
                cp.start()
                passed.append(cp)
        for a in range(n):
            copy(a, 0, sibling, me).wait_recv()
        for j, chip in enumerate(chips):
            for a in range(n):
                copy(a, 4 + j, (*chip, 1 - c), me).wait_recv()
        for cp in first + passed:
            cp.wait_send()
        for cp in mine:
            cp.wait()

    if cid is None:
        return pl.pallas_call(
            body, name=name, in_specs=[ANY] * n, out_specs=[ANY] * n,
            out_shape=[jax.ShapeDtypeStruct((N_DEV,) + s.shape, s.dtype) for s in shards],
            scratch_shapes=[pltpu.SemaphoreType.DMA((n, 7)), pltpu.SemaphoreType.DMA((n, 7)), pltpu.SemaphoreType.DMA((n,))],
        )(*shards)

    src = [jax.new_ref(s, memory_space=pltpu.MemorySpace.HBM) for s in shards]
    dst = [jax.empty_ref(jax.ShapeDtypeStruct((N_DEV,) + s.shape, s.dtype), memory_space=pltpu.MemorySpace.HBM) for s in shards]

    @pl.kernel(mesh=plsc.ScalarSubcoreMesh(axis_name="seq", num_cores=1), name=name,
               scratch_types=(pltpu.SemaphoreType.DMA((n, 7)), pltpu.SemaphoreType.DMA((n, 7)), pltpu.SemaphoreType.DMA((n,))),
               compiler_params=pltpu.CompilerParams(collective_id=cid))
    def launch(send_sems, recv_sems, local_sems):
        x, y, c, chips = _place()
        barrier = pltpu.get_barrier_semaphore()
        for peer in [(x, y, 1 - c)] + [(*chip, c) for chip in chips]:
            pl.semaphore_signal(barrier, inc=1, device_id=peer, device_id_type=MESH)
        pl.semaphore_wait(barrier, 4)
        body(*src, *dst, send_sems, recv_sems, local_sems)

    launch()
    return [r[...] for r in dst]


def _reduce_scatter(parts, name, cid):
    n = len(parts)
    src = [jax.new_ref(s, memory_space=pltpu.MemorySpace.HBM) for s in parts]
    dst = [jax.empty_ref(jax.ShapeDtypeStruct(s.shape, s.dtype), memory_space=pltpu.MemorySpace.HBM) for s in parts]

    @pl.kernel(mesh=plsc.ScalarSubcoreMesh(axis_name="seq", num_cores=1), name=name,
               scratch_types=(pltpu.SemaphoreType.DMA((n, N_DEV - 1)), pltpu.SemaphoreType.DMA((n, N_DEV - 1)), pltpu.SemaphoreType.DMA((n,))),
               compiler_params=pltpu.CompilerParams(collective_id=cid))
    def launch(send_sems, recv_sems, local_sems):
        x, y, c, _ = _place()
        me = 4 * x + 2 * y + c
        peers = [(1 - x if k & 4 else x, 1 - y if k & 2 else y, 1 - c if k & 1 else c) for k in range(1, N_DEV)]
        barrier = pltpu.get_barrier_semaphore()
        for peer in peers:
            pl.semaphore_signal(barrier, inc=1, device_id=peer, device_id_type=MESH)
        pl.semaphore_wait(barrier, N_DEV - 1)
        mine = [pltpu.make_async_copy(src[a].at[me], dst[a].at[me], local_sems.at[a]) for a in range(n)]
        for cp in mine:
            cp.start()
        cps = []
        for a in range(n):
            for k, (px, py, pc) in enumerate(peers):
                cps.append(pltpu.make_async_remote_copy(src_ref=src[a].at[4 * px + 2 * py + pc], dst_ref=dst[a].at[me], send_sem=send_sems.at[a, k],
                                                        recv_sem=recv_sems.at[a, k], device_id=(px, py, pc), device_id_type=MESH))
        for cp in cps:
            cp.start()
        for cp in cps:
            cp.wait_recv()
        for cp in cps:
            cp.wait_send()
        for cp in mine:
            cp.wait()

    launch()
    return [r[...] for r in dst]


def _sibling_exchange(parts, name):
    n = len(parts)
    pieces = [_row_pieces(s.shape[1]) for s in parts]

    def body(*refs):
        ins, outs = refs[:n], refs[n:2 * n]
        send_sems, recv_sems = refs[2 * n:]
        x, y, c, _ = _place()

        def copy(a, ch, q, rows):
            return pltpu.make_async_remote_copy(src_ref=ins[a].at[2 * ch + 1 - c, rows], dst_ref=outs[a].at[ch, rows], send_sem=send_sems.at[a, ch, q],
                                                recv_sem=recv_sems.at[a, ch, q], device_id=(x, y, 1 - c), device_id_type=MESH)

        cps = [copy(a, ch, q, rows) for ch in range(4) for a in range(n) for q, rows in enumerate(pieces[a])]
        for cp in cps:
            cp.start()
        for cp in cps:
            cp.wait_recv()
        for cp in cps:
            cp.wait_send()

    return pl.pallas_call(
        body, name=name, in_specs=[ANY] * n, out_specs=[ANY] * n,
        out_shape=[jax.ShapeDtypeStruct((4,) + s.shape[1:], s.dtype) for s in parts],
        scratch_shapes=[pltpu.SemaphoreType.DMA((n, 4, PIECES)), pltpu.SemaphoreType.DMA((n, 4, PIECES))],
    )(*parts)


def _chip_exchange(parts, name):
    n = len(parts)

    def body(*refs):
        ins, outs = refs[:n], refs[n:2 * n]
        send_sems, recv_sems, local_sems = refs[2 * n:]
        x, y, c, chips = _place()
        my_chip = 2 * x + y

        def copy(a, j):
            px, py = chips[j]
            return pltpu.make_async_remote_copy(src_ref=ins[a].at[2 * px + py], dst_ref=outs[a].at[my_chip], send_sem=send_sems.at[a, j],
                                                recv_sem=recv_sems.at[a, j], device_id=(px, py, c), device_id_type=MESH)

        def landing(a, j):
            px, py = chips[j]
            return pltpu.make_async_remote_copy(src_ref=ins[a].at[my_chip], dst_ref=outs[a].at[2 * px + py], send_sem=send_sems.at[a, j],
                                                recv_sem=recv_sems.at[a, j], device_id=(px, py, c), device_id_type=MESH)

        mine = [pltpu.make_async_copy(ins[a].at[my_chip], outs[a].at[my_chip], local_sems.at[a]) for a in range(n)]
        for cp in mine:
            cp.start()
        cps = [copy(a, j) for j in range(3) for a in range(n)]
        for cp in cps:
            cp.start()
        for j in range(3):
            for a in range(n):
                landing(a, j).wait_recv()
        for cp in cps:
            cp.wait_send()
        for cp in mine:
            cp.wait()

    return pl.pallas_call(
        body, name=name, in_specs=[ANY] * n, out_specs=[ANY] * n,
        out_shape=[jax.ShapeDtypeStruct(s.shape, s.dtype) for s in parts],
        scratch_shapes=[pltpu.SemaphoreType.DMA((n, 3)), pltpu.SemaphoreType.DMA((n, 3)), pltpu.SemaphoreType.DMA((n,))],
    )(*parts)


def _pair_add(mine8, got4, name):
    _, r, cols = mine8.shape
    tr = _pick(r, (256, 64, 16))
    core = lax.axis_index("c").astype(jnp.int32).reshape(1)

    def body(c_ref, a_ref, b_ref, o_ref):
        o_ref[...] = (a_ref[...].astype(F32) + b_ref[...].astype(F32)).astype(BF16)

    return pl.pallas_call(
        body, name=name,
        grid_spec=pltpu.PrefetchScalarGridSpec(
            num_scalar_prefetch=1, grid=(4, r // tr),
            in_specs=[pl.BlockSpec((None, tr, cols), lambda ch, i, c_ref: (2 * ch + c_ref[0], i, 0)),
                      pl.BlockSpec((None, tr, cols), lambda ch, i, c_ref: (ch, i, 0))],
            out_specs=pl.BlockSpec((None, tr, cols), lambda ch, i, c_ref: (ch, i, 0))),
        out_shape=jax.ShapeDtypeStruct((4, r, cols), BF16),
        compiler_params=_params(("parallel", "parallel")),
    )(core, mine8, got4)


HBM = pl.BlockSpec(memory_space=pltpu.HBM)
SEM = pl.BlockSpec(memory_space=pltpu.SEMAPHORE)
EFFECT = pltpu.SideEffectType.DATAFLOW_SIDE_EFFECTING


def _in_hbm(a):
    return pltpu.with_memory_space_constraint(a, pltpu.HBM)


def _remote_copies(plan, srcs, lands, send_sems, recv_sems):
    return [pltpu.make_async_remote_copy(src_ref=s, dst_ref=d, send_sem=send_sems.at[i], recv_sem=recv_sems.at[i], device_id=peer,
                                         device_id_type=MESH) for i, (s, d, peer) in enumerate(plan(srcs, lands))]


def _copies_start(plan, n_copies, srcs, land_shapes, name, after=()):
    ns, nl = len(srcs), len(land_shapes)

    def body(*refs):
        send_sems, recv_sems = refs[ns + nl + len(after):ns + nl + len(after) + 2]
        for cp in _remote_copies(plan, refs[:ns], refs[ns:ns + nl], send_sems, recv_sems):
            cp.start()
        refs[-1][...] = jnp.zeros((8, 128), F32)

    sems = pltpu.SemaphoreType.DMA((n_copies,))
    return pl.pallas_call(
        body, name=name,
        out_shape=(sems, sems, *[pltpu.HBM(s.shape, s.dtype) for s in srcs], *[pltpu.HBM(s.shape, s.dtype) for s in land_shapes],
                   jax.ShapeDtypeStruct((8, 128), F32)),
        in_specs=[HBM] * (ns + nl) + [ANY] * len(after),
        out_specs=(SEM, SEM, *[HBM] * (ns + nl), pl.BlockSpec(memory_space=pltpu.VMEM)),
        input_output_aliases={i: 2 + i for i in range(ns + nl)},
        compiler_params=pltpu.CompilerParams(has_side_effects=EFFECT),
    )(*[_in_hbm(s) for s in srcs], *[_in_hbm(lax.empty(s.shape, s.dtype)) for s in land_shapes], *after)


def _copies_wait(plan, state, ns, name, after=()):
    send_sems, recv_sems, *arrs = state[:-1]
    n = len(arrs)

    def body(*refs):
        cps = _remote_copies(plan, refs[:ns], refs[ns:n], refs[n], refs[n + 1])
        for cp in cps:
            cp.wait_send()
        for cp in cps:
            cp.wait_recv()

    out = pl.pallas_call(
        body, name=name, out_shape=tuple(pltpu.HBM(a.shape, a.dtype) for a in arrs),
        in_specs=[HBM] * n + [SEM, SEM] + [ANY] * len(after), out_specs=tuple([HBM] * n),
        input_output_aliases={i: i for i in range(n)},
        compiler_params=pltpu.CompilerParams(has_side_effects=EFFECT),
    )(*arrs, send_sems, recv_sems, *after)
    return list(out[:ns]), list(out[ns:])


def _gather_plan(srcs, lands):
    x, y, c, chips = _place()
    peers = [(x, y, 1 - c)] + [(*chip, c) for chip in chips]
    return [(s, l.at[4 * x + 2 * y + c], peer) for s, l in zip(srcs, lands) for peer in peers]


def _scatter_plan(srcs, lands):
    x, y, c, _ = _place()
    peers = [(1 - x if k & 4 else x, 1 - y if k & 2 else y, 1 - c if k & 1 else c) for k in range(1, N_DEV)]
    return [(s.at[4 * px + 2 * py + pc], l.at[4 * x + 2 * y + c], (px, py, pc)) for s, l in zip(srcs, lands) for px, py, pc in peers]


def _everyone_plan(srcs, lands):
    x, y, c, _ = _place()
    peers = [(1 - x if k & 4 else x, 1 - y if k & 2 else y, 1 - c if k & 1 else c) for k in range(1, N_DEV)]
    return [(s, l.at[4 * x + 2 * y + c], peer) for s, l in zip(srcs, lands) for peer in peers]


def _sum_parts(got, own, me, name):
    def body(me_ref, got_ref, own_ref, o_ref):
        acc = jnp.where(me_ref[0] == 0, own_ref[...], got_ref[0])
        for d in range(1, N_DEV):
            acc = acc + jnp.where(me_ref[0] == d, own_ref[...], got_ref[d])
        o_ref[...] = acc

    return pl.pallas_call(
        body, name=name,
        grid_spec=pltpu.PrefetchScalarGridSpec(
            num_scalar_prefetch=1, grid=(1,),
            in_specs=[pl.BlockSpec(got.shape, lambda i, me_ref: (0, 0, 0)), pl.BlockSpec(own.shape, lambda i, me_ref: (0, 0))],
            out_specs=pl.BlockSpec(own.shape, lambda i, me_ref: (0, 0))),
        out_shape=jax.ShapeDtypeStruct(own.shape, F32),
    )(me.astype(jnp.int32).reshape(1), got, own)


def _chip_plan(srcs, lands):
    x, y, c, chips = _place()
    return [(s.at[2 * px + py], l.at[2 * x + y], (px, py, c)) for s, l in zip(srcs, lands) for px, py in chips]


PIECES = 8


def _row_pieces(rows):
    for k in (PIECES, 4, 2):
        if rows % (16 * k) == 0:
            return [pl.ds(q * (rows // k), rows // k) for q in range(k)]
    return [pl.ds(0, rows)]


def _put_own(shard, zone, me, name):
    r, c = shard.shape
    tr = r if r <= 256 else _pick(r, (256, 64))

    def body(me_ref, s_ref, z_ref, o_ref):
        o_ref[...] = s_ref[...]

    return pl.pallas_call(
        body, name=name,
        grid_spec=pltpu.PrefetchScalarGridSpec(
            num_scalar_prefetch=1, grid=(r // tr,),
            in_specs=[pl.BlockSpec((tr, c), lambda i, me_ref: (i, 0)), ANY],
            out_specs=pl.BlockSpec((None, tr, c), lambda i, me_ref: (me_ref[0], i, 0))),
        out_shape=jax.ShapeDtypeStruct(zone.shape, zone.dtype), input_output_aliases={2: 0},
        compiler_params=_params(("arbitrary",)),
    )(me.astype(jnp.int32).reshape(1), shard, zone)


def _gather_finish(lands, name):
    n = len(lands)
    pieces = [_row_pieces(s.shape[1]) for s in lands]

    def body(*refs):
        zones, outs = refs[:n], refs[n:2 * n]
        send_sems, recv_sems = refs[2 * n:]
        x, y, c, chips = _place()
        cps = []
        for j, (px, py) in enumerate(chips):
            for a in range(n):
                for q, rows in enumerate(pieces[a]):
                    cps.append(pltpu.make_async_remote_copy(
                        src_ref=zones[a].at[4 * px + 2 * py + c, rows], dst_ref=outs[a].at[4 * px + 2 * py + c, rows],
                        send_sem=send_sems.at[a, j, q], recv_sem=recv_sems.at[a, j, q], device_id=(x, y, 1 - c), device_id_type=MESH))
        for cp in cps:
            cp.start()
        for cp in cps:
            cp.wait_recv()
        for cp in cps:
            cp.wait_send()

    return pl.pallas_call(
        body, name=name, in_specs=[ANY] * n, out_specs=[ANY] * n,
        out_shape=[jax.ShapeDtypeStruct(l.shape, l.dtype) for l in lands],
        input_output_aliases={a: a for a in range(n)},
        scratch_shapes=[pltpu.SemaphoreType.DMA((n, 3, PIECES)), pltpu.SemaphoreType.DMA((n, 3, PIECES))],
    )(*lands)


def _sum_everywhere(v, name):
    rows = v.shape[0]

    def body(v_ref, o_ref, buf, send_sems, recv_sems):
        x, y, c, _ = _place()
        me = 4 * x + 2 * y + c
        buf[me] = v_ref[...]
        cps = []
        for k in range(1, N_DEV):
            fx, fy, fc = (k >> 2) & 1, (k >> 1) & 1, k & 1
            to = (1 - x if fx else x, 1 - y if fy else y, 1 - c if fc else c)
            cps.append(pltpu.make_async_remote_copy(src_ref=buf.at[me], dst_ref=buf.at[me], send_sem=send_sems.at[k - 1],
                                                    recv_sem=recv_sems.at[k - 1], device_id=to, device_id_type=MESH))
        for cp in cps:
            cp.start()
        for cp in cps:
            cp.wait_recv()
        for cp in cps:
            cp.wait_send()
        acc = buf[0]
        for d in range(1, N_DEV):
            acc = acc + buf[d]
        o_ref[...] = acc

    vm = pl.BlockSpec(memory_space=pltpu.VMEM)
    return pl.pallas_call(
        body, name=name, in_specs=[vm], out_specs=vm, out_shape=jax.ShapeDtypeStruct(v.shape, F32),
        scratch_shapes=[pltpu.VMEM((N_DEV, rows, 128), F32), pltpu.SemaphoreType.DMA((N_DEV - 1,)), pltpu.SemaphoreType.DMA((N_DEV - 1,))],
    )(v)


def _adamw_parts(w, got, mine, me, m, v, name, after=()):
    r, c = w.shape
    tr = _pick(r, (128, 64))
    n_parts = got.shape[0]

    def body(me_ref, w_ref, got_ref, own_ref, m_ref, v_ref, *rest):
        go_ref, d_ref, mo_ref, vo_ref = rest[len(after):]
        own = own_ref[...].astype(F32)
        gv = jnp.where(me_ref[0] == 0, own, got_ref[0].astype(F32))
        for d in range(1, n_parts):
            gv = gv + jnp.where(me_ref[0] == d, own, got_ref[d].astype(F32))
        _adamw_math(gv, w_ref, m_ref, v_ref, go_ref, d_ref, mo_ref, vo_ref)

    tile = pl.BlockSpec((tr, c), lambda i, me_ref: (i, 0))
    out = jax.ShapeDtypeStruct((r, c), F32)
    return pl.pallas_call(
        body, name=name,
        grid_spec=pltpu.PrefetchScalarGridSpec(
            num_scalar_prefetch=1, grid=(r // tr,),
            in_specs=[tile, pl.BlockSpec((n_parts, tr, c), lambda i, me_ref: (0, i, 0)),
                      pl.BlockSpec((None, tr, c), lambda i, me_ref: (me_ref[0], i, 0)), tile, tile] + [ANY] * len(after),
            out_specs=[tile] * 4),
        out_shape=[out] * 4, compiler_params=_params(("parallel",)),
    )(me.astype(jnp.int32).reshape(1), w, got, mine, m, v, *after)


def _adamw_math(gv, w_ref, m_ref, v_ref, go_ref, d_ref, mo_ref, vo_ref):
    mn = B1 * m_ref[...] + (1.0 - B1) * gv
    vn = B2 * v_ref[...] + (1.0 - B2) * (gv * gv)
    m_hat = mn / (1.0 - B1 ** STEP)
    v_hat = vn / (1.0 - B2 ** STEP)
    go_ref[...] = gv
    d_ref[...] = -LR * (m_hat / (jnp.sqrt(v_hat) + ADAM_EPS) + WD * w_ref[...])
    mo_ref[...] = mn
    vo_ref[...] = vn


def _adamw(w, g, m, v, name):
    r, c = w.shape
    parts = g.ndim == 3
    tr = r if r <= 128 else _pick(r, (128, 64))

    def body(w_ref, g_ref, m_ref, v_ref, go_ref, d_ref, mo_ref, vo_ref):
        if parts:
            gv = g_ref[0].astype(F32)
            for d in range(1, g.shape[0]):
                gv = gv + g_ref[d].astype(F32)
        else:
            gv = g_ref[...]
        mn = B1 * m_ref[...] + (1.0 - B1) * gv
        vn = B2 * v_ref[...] + (1.0 - B2) * (gv * gv)
        m_hat = mn / (1.0 - B1 ** STEP)
        v_hat = vn / (1.0 - B2 ** STEP)
        go_ref[...] = gv
        d_ref[...] = -LR * (m_hat / (jnp.sqrt(v_hat) + ADAM_EPS) + WD * w_ref[...])
        mo_ref[...] = mn
        vo_ref[...] = vn

    tile = pl.BlockSpec((tr, c), lambda i: (i, 0))
    g_spec = pl.BlockSpec((g.shape[0], tr, c), lambda i: (0, i, 0)) if parts else tile
    out = jax.ShapeDtypeStruct((r, c), F32)
    return pl.pallas_call(
        body, name=name, grid=(r // tr,), in_specs=[tile, g_spec, tile, tile], out_specs=[tile] * 4, out_shape=[out] * 4,
        compiler_params=_params(("parallel",)),
    )(w, g, m, v)


BIG = ["w_in", "w_a_out", "w_b_out", "w_mix_out", "w_ff_gate", "w_ff_up", "w_ff_down", "w_ple_gate", "w_ple_proj"]
COL_SHARDED = ["w_in", "w_a_out", "w_b_out", "w_ff_gate", "w_ff_up", "w_ple_proj"]
SMALL = ["conv_w", "w_alpha_up", "b_alpha_up", "gla_head_gain", "g_pre_mix", "g_post_mix", "g_pre_ffn", "g_post_ffn", "g_pre_ple", "g_post_ple"]
WEIGHTS = ["w_in", "conv_w", "w_a_out", "w_alpha_up", "b_alpha_up", "gla_head_gain", "w_b_out", "w_mix_out", "g_pre_mix", "g_post_mix",
           "g_pre_ffn", "g_post_ffn", "w_ff_gate", "w_ff_up", "w_ff_down", "g_pre_ple", "g_post_ple", "w_ple_gate", "w_ple_proj"]


def _cols_to_full(g8):
    n, r, c = g8.shape
    return jnp.transpose(g8, (1, 0, 2)).reshape(r, n * c)


def _full_to_cols(a):
    r, c = a.shape
    return jnp.transpose(a.reshape(r, N_DEV, c // N_DEV), (1, 0, 2))


def _pack(arrs, rows):
    flat = jnp.concatenate([a.reshape(-1) for a in arrs])
    return jnp.pad(flat, (0, rows * 128 - flat.shape[0])).reshape(rows, 128)


def _unpack(packed, shapes):
    flat, out, o = packed.reshape(-1), [], 0
    for s in shapes:
        size = 1
        for d in s:
            size *= d
        out.append(flat[o:o + size].reshape(s))
        o += size
    return out


def kernel(x, p, w_in, conv_w, w_a_out, w_alpha_up, b_alpha_up, gla_head_gain, w_b_out, w_mix_out, g_pre_mix, g_post_mix, g_pre_ffn, g_post_ffn, w_ff_gate, w_ff_up, w_ff_down, g_pre_ple, g_post_ple, w_ple_gate, w_ple_proj, loss_target, m_w_in, m_conv_w, m_w_a_out, m_w_alpha_up, m_b_alpha_up, m_gla_head_gain, m_w_b_out, m_w_mix_out, m_g_pre_mix, m_g_post_mix, m_g_pre_ffn, m_g_post_ffn, m_w_ff_gate, m_w_ff_up, m_w_ff_down, m_g_pre_ple, m_g_post_ple, m_w_ple_gate, m_w_ple_proj, v_w_in, v_conv_w, v_w_a_out, v_w_alpha_up, v_b_alpha_up, v_gla_head_gain, v_w_b_out, v_w_mix_out, v_g_pre_mix, v_g_post_mix, v_g_pre_ffn, v_g_post_ffn, v_w_ff_gate, v_w_ff_up, v_w_ff_down, v_g_pre_ple, v_g_post_ple, v_w_ple_gate, v_w_ple_proj):
    args = dict(locals())
    wts = {n: args[n][0] for n in WEIGHTS}
    mom = {n: args["m_" + n][0] for n in WEIGHTS}
    var = {n: args["v_" + n][0] for n in WEIGHTS}
    me = 4 * lax.axis_index("x") + 2 * lax.axis_index("y") + lax.axis_index("c")

    groups = [["w_in", "conv_w", "w_alpha_up"], ["w_a_out", "w_b_out", "w_mix_out"], ["w_ff_gate", "w_ff_up"], ["w_ff_down"],
              ["w_ple_gate", "w_ple_proj"]]
    grad_groups = [["w_ple_proj", "w_ple_gate"], ["w_ff_down", "w_ff_gate", "w_ff_up"], ["w_mix_out", "w_a_out", "w_b_out"], ["w_in"]]
    rows_full = lambda g: g.reshape(-1, g.shape[-1])
    gathers, scatters = {}, {}

    def gather_start(gi, after):
        shards = [wts[n].astype(BF16) if n in BIG else wts[n] for n in groups[gi]]
        zones = [jax.ShapeDtypeStruct((N_DEV,) + s.shape, s.dtype) for s in shards]
        gathers[gi] = (shards, _copies_start(_gather_plan, 4 * len(shards), shards, zones, "gather_start_%d" % gi, after))
        return gathers[gi][1][-1]

    def gather_finish(gi, after):
        shards, state = gathers[gi]
        shards, zones = _copies_wait(_gather_plan, state, len(shards), "gather_wait_%d" % gi, after)
        zones = _gather_finish(zones, "gather_finish_%d" % gi)
        g8 = {n: _put_own(s, z, me, "gather_own_" + n) for n, s, z in zip(groups[gi], shards, zones)}
        if gi == 0:
            w_in_full = _cols_to_full(g8["w_in"])
            return dict(in_p=jnp.concatenate([w_in_full[:, R_GA:R_END], w_in_full[:, :R_ALR], w_in_full[:, R_ALR:R_GA],
                                              jnp.zeros((w_in_full.shape[0], 128 - GATE_RANK), BF16)], axis=1),
                        conv_w=_cols_to_full(g8["conv_w"]), w_alpha_up=_cols_to_full(g8["w_alpha_up"]))
        if gi == 1:
            return dict(a_out=_cols_to_full(g8["w_a_out"]), b_out=_cols_to_full(g8["w_b_out"]), mix=rows_full(g8["w_mix_out"]))
        if gi == 2:
            return dict(gu=jnp.concatenate([_cols_to_full(g8["w_ff_gate"]), _cols_to_full(g8["w_ff_up"])], axis=1))
        if gi == 3:
            return dict(down=rows_full(g8["w_ff_down"]))
        return dict(pg=rows_full(g8["w_ple_gate"]), pp=_cols_to_full(g8["w_ple_proj"]))

    def scatter_start(gi, gw):
        if gi == 3:
            g_in = gw["in_p"]
            gfull = dict(w_in=jnp.concatenate([g_in[:, C_AX:C_ALR], g_in[:, C_ALR:C_ALR + GATE_RANK], g_in[:, :C_AX]], axis=1))
        elif gi == 1:
            f_gate = gw["gu"].shape[1] // 2
            gfull = dict(w_ff_down=gw["down"], w_ff_gate=gw["gu"][:, :f_gate], w_ff_up=gw["gu"][:, f_gate:])
        elif gi == 2:
            gfull = dict(w_mix_out=gw["mix"], w_a_out=gw["a_out"], w_b_out=gw["b_out"])
        else:
            gfull = dict(w_ple_proj=gw["pp"], w_ple_gate=gw["pg"])
        parts = [_full_to_cols(gfull[n]) if n in COL_SHARDED else gfull[n].reshape(N_DEV, -1, gfull[n].shape[-1]) for n in grad_groups[gi]]
        if gi == 3:
            from_sibling = _sibling_exchange(parts, "scatter_sibling_%d" % gi)
            parts = [_pair_add(a, b, "scatter_add_%d_%s" % (gi, n)) for n, a, b in zip(grad_groups[gi], parts, from_sibling)]
            scatters[gi] = _copies_start(_chip_plan, 3 * len(parts), parts, parts, "scatter_start_%d" % gi)
        else:
            scatters[gi] = _copies_start(_scatter_plan, (N_DEV - 1) * len(parts), parts, parts, "scatter_start_%d" % gi)
        return scatters[gi][-1]

    small = {n: wts[n].reshape(1, -1) for n in SMALL[2:]}

    loss_rows, grad_x, gs = _local_step(x[0], p[0, 0], loss_target[0], gather_start, gather_finish, scatter_start, small)
    loss = lax.psum(jnp.sum(loss_rows), ("x", "y", "c"))

    small_shapes = [gs[n].shape for n in SMALL]
    gs_packed = _pack([gs[n] for n in SMALL], 192)
    small_state = _copies_start(_everyone_plan, N_DEV - 1, [gs_packed], [jax.ShapeDtypeStruct((N_DEV,) + gs_packed.shape, F32)],
                                "small_start", (grad_x,))

    res, done = {}, (small_state[-1],)
    for gi, names in enumerate(grad_groups):
        plan, slot = (_chip_plan, me // 2) if gi == 3 else (_scatter_plan, me)
        mine, got = _copies_wait(plan, scatters[gi], len(names), "scatter_wait_%d" % gi, done)
        for n, g, own in zip(names, got, mine):
            res[n] = _adamw_parts(wts[n], g, own, slot, mom[n], var[n], "adamw_" + n)
        done = tuple(res[n][1] for n in names)

    (gs_own,), (gs_got,) = _copies_wait(_everyone_plan, small_state, 1, "small_wait", done)
    gsum = dict(zip(SMALL, _unpack(_sum_parts(gs_got, gs_own, me, "small_sum"), small_shapes)))
    gsum["conv_w"] = lax.dynamic_index_in_dim(gsum["conv_w"].reshape(3, N_DEV, -1), me, axis=1, keepdims=False)
    gsum["w_alpha_up"] = lax.dynamic_index_in_dim(gsum["w_alpha_up"].reshape(GATE_RANK, N_DEV, -1), me, axis=1, keepdims=False)

    shard_shapes = [wts[n].shape for n in SMALL]
    packed = [_pack([d[n] for n in SMALL], 120) for d in (wts, gsum, mom, var)]
    outs = [_unpack(o, shard_shapes) for o in _adamw(*packed, "adamw_small")]
    for i, n in enumerate(SMALL):
        res[n] = [o[i] for o in outs]

    lead = lambda a: a[None]
    return (loss, grad_x[None], *[lead(res[n][0]) for n in WEIGHTS], *[lead(res[n][1]) for n in WEIGHTS],
            *[lead(res[n][2]) for n in WEIGHTS], *[lead(res[n][3]) for n in WEIGHTS])
```

```python
import jax
import jax.numpy as jnp
from jax import lax
from jax.experimental import pallas as pl
from jax.experimental.pallas import tpu as pltpu
from jax.experimental.pallas import tpu_sc as plsc

F32, BF16 = jnp.float32, jnp.bfloat16
EPS = 1e-6
CHUNK = 64
HEADS, DK, DV = 4, 128, 256
GATE_RANK = 16
TAU = 16.0
LR, B1, B2, ADAM_EPS, WD, STEP = 0.001, 0.9, 0.999, 1e-08, 0.01, 10
N_DEV = 8
MESH = pl.DeviceIdType.MESH
VMEM_LIMIT = 56 * 1024 * 1024
ANY = pl.BlockSpec(memory_space=pl.ANY)

C_GA, C_GB, C_AX, C_AB, C_AC, C_Q, C_K, C_V, C_OG, C_ALR = 0, 2048, 4096, 5120, 6144, 7168, 7680, 8192, 9216, 10240
IN_PAD = 10368
R_AX, R_AB, R_AC, R_Q, R_K, R_V, R_OG, R_ALR, R_GA, R_GB, R_END = 0, 1024, 2048, 3072, 3584, 4096, 5120, 6144, 6160, 8208, 10256


def _params(sem):
    return pltpu.CompilerParams(dimension_semantics=sem, vmem_limit_bytes=VMEM_LIMIT)


def _pick(n, cands):
    for c in cands:
        if n % c == 0:
            return c
    return n


def _tiles(r, c):
    for tr in (128, 64):
        if r % tr == 0:
            return r // tr, (tr, c), lambda i: (i, 0)
    tc = _pick(c, (256, 128))
    return c // tc, (r, tc), lambda i: (0, i)


def _mm(a, b, mode, out_dtype, name, after=(), add=None, b3=False, out3=False, tm=None, tk=None):
    bshape = (b.shape[1], N_DEV * b.shape[2]) if b3 else b.shape
    if mode == "nn":
        (m, k), (k2, n) = a.shape, bshape
    elif mode == "nt":
        (m, k), (n, k2) = a.shape, bshape
    else:
        (k, m), (k2, n) = a.shape, bshape
    assert k == k2 and a.dtype == BF16 and b.dtype == BF16, (name, a.shape, b.shape, a.dtype, b.dtype)
    tm = tm if tm and m % tm == 0 else _pick(m, (1024, 512, 256))
    tn = _pick(n, (1152, 1024, 1408, 512, 256))
    tk = tk if tk and k % tk == 0 else _pick(k, (512, 1152, 256))
    if out3 or (b3 and mode == "nn"):
        tn = n // N_DEV
    if b3 and mode == "nt":
        tk = k // N_DEV
    nk = k // tk
    dims = {"nn": (((1,), (0,)), ((), ())), "nt": (((1,), (1,)), ((), ())), "tn": (((0,), (0,)), ((), ()))}[mode]
    n_extra = len(after) + (add is not None)

    def body(a_ref, b_ref, *rest):
        o_ref, acc_ref = rest[n_extra:]
        kk = pl.program_id(2)

        @pl.when(kk == 0)
        def _():
            acc_ref[...] = jnp.zeros_like(acc_ref) if add is None else rest[0][...]

        acc_ref[...] += lax.dot_general(a_ref[...], b_ref[...], dims, preferred_element_type=F32)

        @pl.when(kk == nk - 1)
        def _():
            o_ref[...] = acc_ref[...].astype(o_ref.dtype)

    a_spec = pl.BlockSpec((tk, tm), lambda i, j, kk: (kk, i)) if mode == "tn" else pl.BlockSpec((tm, tk), lambda i, j, kk: (i, kk))
    if b3:
        b_spec = (pl.BlockSpec((None, tn, tk), lambda i, j, kk: (kk, j, 0)) if mode == "nt"
                  else pl.BlockSpec((None, tk, tn), lambda i, j, kk: (j, kk, 0)))
    else:
        b_spec = pl.BlockSpec((tn, tk), lambda i, j, kk: (j, kk)) if mode == "nt" else pl.BlockSpec((tk, tn), lambda i, j, kk: (kk, j))
    tile = pl.BlockSpec((tm, tn), lambda i, j, kk: (i, j))
    out_spec = pl.BlockSpec((None, tm, tn), lambda i, j, kk: (j, i, 0)) if out3 else tile
    return pl.pallas_call(
        body, name=name, grid=(m // tm, n // tn, nk),
        in_specs=[a_spec, b_spec] + ([tile] if add is not None else []) + [ANY] * len(after), out_specs=out_spec,
        out_shape=jax.ShapeDtypeStruct((N_DEV, m, tn) if out3 else (m, n), out_dtype),
        scratch_shapes=[pltpu.VMEM((tm, tn), F32)],
        compiler_params=_params(("parallel", "parallel", "arbitrary")),
    )(a, b, *([add] if add is not None else []), *after)


def _rows(body, t, tr, ins, outs, name):
    in_specs = []
    for arr, sp in ins:
        if sp[0] == "t":
            in_specs.append(pl.BlockSpec((tr, sp[1]), lambda i, cb=sp[2]: (i, cb)))
        else:
            in_specs.append(pl.BlockSpec(arr.shape, lambda i, nd=arr.ndim: (0,) * nd))
    out_specs, out_shape = [], []
    for shape, dt, kind in outs:
        out_specs.append(pl.BlockSpec((tr, shape[1]), lambda i: (i, 0)) if kind == "t" else pl.BlockSpec(shape, lambda i: (0, 0)))
        out_shape.append(jax.ShapeDtypeStruct(shape, dt))
    return pl.pallas_call(
        body, name=name, grid=(t // tr,), in_specs=in_specs, out_specs=out_specs, out_shape=out_shape,
        compiler_params=_params(("arbitrary",)),
    )(*[arr for arr, _ in ins])


def _rinv(v):
    return lax.rsqrt(jnp.mean(v * v, axis=-1, keepdims=True) + EPS)


def _sig(v):
    return 1.0 / (1.0 + jnp.exp(-v))


def _acc(ref, val):
    @pl.when(pl.program_id(0) == 0)
    def _():
        ref[...] = jnp.zeros_like(ref)

    ref[...] += jnp.sum(val, axis=0, keepdims=True)


def _rms_fwd(x, g, name):
    t, d = x.shape

    def body(x_ref, g_ref, h_ref):
        xv = x_ref[...]
        h_ref[...] = (xv * _rinv(xv) * g_ref[...]).astype(BF16)

    return _rows(body, t, 256, [(x, ("t", d, 0)), (g, ("b",))], [((t, d), BF16, "t")], name)[0]


def _post_pre(x, m, g_post, g_pre, name):
    t, d = x.shape

    def body(x_ref, m_ref, gp_ref, gn_ref, xo_ref, h_ref):
        mv = m_ref[...]
        xn = x_ref[...] + mv * _rinv(mv) * gp_ref[...]
        xo_ref[...] = xn
        h_ref[...] = (xn * _rinv(xn) * gn_ref[...]).astype(BF16)

    return _rows(body, t, 128, [(x, ("t", d, 0)), (m, ("t", d, 0)), (g_post, ("b",)), (g_pre, ("b",))],
                 [((t, d), F32, "t"), ((t, d), BF16, "t")], name)


def _mix_fwd(proj, ya, yb, name):
    t, d = ya.shape

    def body(ga_ref, gb_ref, ya_ref, yb_ref, o_ref):
        o_ref[...] = (_sig(ga_ref[...].astype(F32)) * ya_ref[...].astype(F32)
                      + _sig(gb_ref[...].astype(F32)) * yb_ref[...].astype(F32)).astype(BF16)

    return _rows(body, t, 256, [(proj, ("t", d, C_GA // d)), (proj, ("t", d, C_GB // d)), (ya, ("t", d, 0)), (yb, ("t", d, 0))],
                 [((t, d), BF16, "t")], name)[0]


def _mix_bwd(dmix, proj, ya, yb, name):
    t, d = ya.shape

    def body(dm_ref, ga_ref, gb_ref, ya_ref, yb_ref, dg_ref, dya_ref, dyb_ref):
        dm = dm_ref[...]
        sa, sb = _sig(ga_ref[...].astype(F32)), _sig(gb_ref[...].astype(F32))
        dg_ref[:, :d] = (dm * ya_ref[...].astype(F32) * sa * (1.0 - sa)).astype(BF16)
        dg_ref[:, d:] = (dm * yb_ref[...].astype(F32) * sb * (1.0 - sb)).astype(BF16)
        dya_ref[...] = (dm * sa).astype(BF16)
        dyb_ref[...] = (dm * sb).astype(BF16)

    return _rows(body, t, 128,
                 [(dmix, ("t", d, 0)), (proj, ("t", d, C_GA // d)), (proj, ("t", d, C_GB // d)), (ya, ("t", d, 0)), (yb, ("t", d, 0))],
                 [((t, 2 * d), BF16, "t"), ((t, d), BF16, "t"), ((t, d), BF16, "t")], name)


def _swiglu_call(body, ins, n_out, name):
    t, f = ins[0].shape
    tc = _pick(f, (1408, 512))
    tile = pl.BlockSpec((512, tc), lambda i, j: (i, j))
    return pl.pallas_call(
        body, name=name, grid=(t // 512, f // tc), in_specs=[tile] * len(ins), out_specs=[tile] * n_out,
        out_shape=[jax.ShapeDtypeStruct((t, f), BF16)] * n_out, compiler_params=_params(("parallel", "parallel")),
    )(*ins)


def _swiglu_fwd(fg, fu, name):
    def body(g_ref, u_ref, s_ref):
        gv = g_ref[...].astype(F32)
        s_ref[...] = (gv * _sig(gv) * u_ref[...].astype(F32)).astype(BF16)

    return _swiglu_call(body, [fg, fu], 1, name)[0]


def _swiglu_bwd(ds, fg, fu, name):
    def body(ds_ref, g_ref, u_ref, dg_ref, du_ref):
        dsv, gv, uv = ds_ref[...].astype(F32), g_ref[...].astype(F32), u_ref[...].astype(F32)
        sg = _sig(gv)
        dg_ref[...] = (dsv * uv * sg * (1.0 + gv * (1.0 - sg))).astype(BF16)
        du_ref[...] = (dsv * gv * sg).astype(BF16)

    return _swiglu_call(body, [ds, fg, fu], 2, name)


def _ple_final(x2, pg, pp, tgt, g_post, name):
    t, d = x2.shape

    def body(x_ref, pg_ref, pp_ref, t_ref, g_ref, loss_ref, d3_ref, dpg_ref, dpp_ref, dg_ref):
        sg, ppv, g = _sig(pg_ref[...]), pp_ref[...], g_ref[...]
        e = sg * ppv
        r = _rinv(e)
        eh = e * r
        diff = x_ref[...] + eh * g - t_ref[...]
        loss_ref[...] = 0.5 * jnp.mean(diff * diff, axis=-1, keepdims=True)
        d3 = diff * (1.0 / d)
        d3_ref[...] = d3
        gd = d3 * g
        de = r * (gd - eh * jnp.mean(gd * eh, axis=-1, keepdims=True))
        dpg_ref[...] = (de * ppv * sg * (1.0 - sg)).astype(BF16)
        dpp_ref[...] = (de * sg).astype(BF16)
        _acc(dg_ref, d3 * eh)

    return _rows(body, t, 128, [(x2, ("t", d, 0)), (pg, ("t", d, 0)), (pp, ("t", d, 0)), (tgt, ("t", d, 0)), (g_post, ("b",))],
                 [((t, 1), F32, "t"), ((t, d), F32, "t"), ((t, d), BF16, "t"), ((t, d), BF16, "t"), ((1, d), F32, "a")], name)


def _norm_bwd(dn, dh, x, g_pre, fm, g_post, name):
    t, d = x.shape
    two = fm is not None

    def body(*refs):
        if two:
            dn_ref, dh_ref, x_ref, gp_ref, f_ref, gq_ref, dx_ref, df_ref, dgp_ref, dgq_ref = refs
        else:
            dn_ref, dh_ref, x_ref, gp_ref, dx_ref, dgp_ref = refs
        xv, dhv = x_ref[...], dh_ref[...]
        r = _rinv(xv)
        xh = xv * r
        gd = dhv * gp_ref[...]
        dx = dn_ref[...] + r * (gd - xh * jnp.mean(gd * xh, axis=-1, keepdims=True))
        dx_ref[...] = dx
        _acc(dgp_ref, dhv * xh)
        if two:
            fv = f_ref[...]
            rf = _rinv(fv)
            fh = fv * rf
            gd2 = dx * gq_ref[...]
            df_ref[...] = (rf * (gd2 - fh * jnp.mean(gd2 * fh, axis=-1, keepdims=True))).astype(BF16)
            _acc(dgq_ref, dx * fh)

    ins = [(dn, ("t", d, 0)), (dh, ("t", d, 0)), (x, ("t", d, 0)), (g_pre, ("b",))]
    outs = [((t, d), F32, "t")]
    if two:
        ins += [(fm, ("t", d, 0)), (g_post, ("b",))]
        outs += [((t, d), BF16, "t"), ((1, d), F32, "a"), ((1, d), F32, "a")]
    else:
        outs += [((1, d), F32, "a")]
    return _rows(body, t, 128, ins, outs, name)


CONV_TC = 256


def _shift_down(v, s):
    rows = lax.broadcasted_iota(jnp.int32, v.shape, 0)
    return jnp.where(rows >= s, pltpu.roll(v, s, 0), 0.0)


def _shift_up(v, s):
    n = v.shape[0]
    rows = lax.broadcasted_iota(jnp.int32, v.shape, 0)
    return jnp.where(rows < n - s, pltpu.roll(v, n - s, 0), 0.0)


def _conv_specs(t):
    nb = 1024 // CONV_TC
    seg = lambda c0: pl.BlockSpec((t, CONV_TC), lambda j, cb=c0 // CONV_TC: (0, cb + j))
    own = pl.BlockSpec((t, CONV_TC), lambda j: (0, j))
    wspec = pl.BlockSpec((3, CONV_TC), lambda j: (0, j))
    return nb, seg, own, wspec


def _conv_fwd(proj, conv_w, name, after=()):
    t = proj.shape[0]
    nb, seg, own, wspec = _conv_specs(t)

    def body(ax_ref, ab_ref, ac_ref, w_ref, *rest):
        za_ref = rest[len(after)]
        u = ac_ref[...].astype(F32) * ax_ref[...].astype(F32)
        w = w_ref[...]
        yc = w[0:1] * _shift_down(u, 2) + w[1:2] * _shift_down(u, 1) + w[2:3] * u
        za_ref[...] = (ab_ref[...].astype(F32) * yc).astype(BF16)

    return pl.pallas_call(
        body, name=name, grid=(nb,), in_specs=[seg(C_AX), seg(C_AB), seg(C_AC), wspec] + [ANY] * len(after), out_specs=own,
        out_shape=jax.ShapeDtypeStruct((t, 1024), BF16), compiler_params=_params(("parallel",)),
    )(proj, proj, proj, conv_w, *after)


def _conv_bwd(dza, proj, conv_w, name):
    t = proj.shape[0]
    nb, seg, own, wspec = _conv_specs(t)

    def body(dz_ref, ax_ref, ab_ref, ac_ref, w_ref, dax_ref, dab_ref, dac_ref, dw_ref):
        ax, ab, ac, dz = ax_ref[...].astype(F32), ab_ref[...].astype(F32), ac_ref[...].astype(F32), dz_ref[...].astype(F32)
        w = w_ref[...]
        u = ac * ax
        u1, u2 = _shift_down(u, 1), _shift_down(u, 2)
        yc = w[0:1] * u2 + w[1:2] * u1 + w[2:3] * u
        dab_ref[...] = (dz * yc).astype(BF16)
        dyc = dz * ab
        du = w[2:3] * dyc + w[1:2] * _shift_up(dyc, 1) + w[0:1] * _shift_up(dyc, 2)
        dax_ref[...] = (du * ac).astype(BF16)
        dac_ref[...] = (du * ax).astype(BF16)
        dw_ref[0:1, :] = jnp.sum(dyc * u2, axis=0, keepdims=True)
        dw_ref[1:2, :] = jnp.sum(dyc * u1, axis=0, keepdims=True)
        dw_ref[2:3, :] = jnp.sum(dyc * u, axis=0, keepdims=True)

    act = jax.ShapeDtypeStruct((t, 1024), BF16)
    return pl.pallas_call(
        body, name=name, grid=(nb,), in_specs=[own, seg(C_AX), seg(C_AB), seg(C_AC), wspec], out_specs=[own, own, own, wspec],
        out_shape=[act, act, act, jax.ShapeDtypeStruct((3, 1024), F32)], compiler_params=_params(("parallel",)),
    )(dza, proj, proj, proj, conv_w)


def _dot(a, b, dims, precision=None):
    return lax.dot_general(a, b, (dims, ((), ())), precision=precision, preferred_element_type=F32)


def _gla_chunk(q, k, v, og, alr, s_in, wa, ba, gain):
    c = q.shape[0]
    hi = lax.Precision.HIGHEST
    z = _dot(alr, wa, ((1,), (0,))) + ba
    la = (jnp.minimum(z, 0.0) - jnp.log(1.0 + jnp.exp(-jnp.abs(z)))) * (1.0 / TAU)
    row = lax.broadcasted_iota(jnp.int32, (c, c), 0)
    col = lax.broadcasted_iota(jnp.int32, (c, c), 1)
    lower = row >= col
    b = _dot(lower.astype(F32), la, ((1,), (0,)), hi)
    trow = lax.broadcasted_iota(jnp.int32, la.shape, 0)
    mid = jnp.sum(jnp.where(trow <= c // 2, la, 0.0), axis=0, keepdims=True)
    blast = jnp.sum(la, axis=0, keepdims=True)
    qs = q * (DK ** -0.5)
    e_up, e_dn = jnp.exp(b - mid), jnp.exp(mid - b)
    a_fwd = _dot(qs * e_up, k * e_dn, ((1,), (1,)))
    a_rev = _dot(qs * e_dn, k * e_up, ((1,), (1,)))
    att = jnp.where(lower, a_fwd, a_rev)
    o = _dot(att, v, ((1,), (0,))) + _dot(qs * jnp.exp(b), s_in, ((1,), (0,)))
    upd = _dot(k * jnp.exp(blast - b), v, ((0,), (0,)))
    blast_col = _dot(la, jnp.ones((c, DV), F32), ((0,), (0,)), hi)
    s_out = jnp.exp(blast_col) * s_in + upd
    on = o * _rinv(o) * gain
    return on * og * _sig(og), s_out


def _gla_specs(t, rev):
    n = t // CHUNK
    ch = (lambda i: n - 1 - i) if rev else (lambda i: i)
    col = lambda w, c0: pl.BlockSpec((CHUNK, w), lambda i, h, cb=c0 // w: (ch(i), cb + h))
    specs = dict(
        q=col(DK, C_Q), k=col(DK, C_K), v=col(DV, C_V), og=col(DV, C_OG),
        alr=pl.BlockSpec((CHUNK, 128), lambda i, h: (ch(i), C_ALR // 128)),
        wa=pl.BlockSpec((128, DK), lambda i, h: (0, h)), ba=pl.BlockSpec((1, DK), lambda i, h: (0, h)),
        gain=pl.BlockSpec((1, DV), lambda i, h: (0, 0)),
        state=pl.BlockSpec((None, None, DK, DV), lambda i, h: (ch(i), h, 0, 0)),
        odk=pl.BlockSpec((CHUNK, DK), lambda i, h: (ch(i), h)), odv=pl.BlockSpec((CHUNK, DV), lambda i, h: (ch(i), h)),
        oalr=pl.BlockSpec((CHUNK, 128), lambda i, h: (ch(i), 0)),
    )
    return n, specs


def _gla_fwd(proj, wa, ba, gain, name):
    t = proj.shape[0]
    n, sp = _gla_specs(t, False)

    def body(q_ref, k_ref, v_ref, og_ref, alr_ref, wa_ref, ba_ref, g_ref, zb_ref, st_ref, s_scr):
        h = pl.program_id(1)

        @pl.when(pl.program_id(0) == 0)
        def _():
            s_scr[h] = jnp.zeros((DK, DV), F32)

        s_in = s_scr[h]
        st_ref[...] = s_in
        zb, s_out = _gla_chunk(q_ref[...].astype(F32), k_ref[...].astype(F32), v_ref[...].astype(F32), og_ref[...].astype(F32),
                               alr_ref[...].astype(F32), s_in, wa_ref[...].astype(F32), ba_ref[...], g_ref[...])
        zb_ref[...] = zb.astype(BF16)
        s_scr[h] = s_out

    return pl.pallas_call(
        body, name=name, grid=(n, HEADS),
        in_specs=[sp["q"], sp["k"], sp["v"], sp["og"], sp["alr"], sp["wa"], sp["ba"], sp["gain"]],
        out_specs=[sp["odv"], sp["state"]],
        out_shape=[jax.ShapeDtypeStruct((t, HEADS * DV), BF16), jax.ShapeDtypeStruct((n, HEADS, DK, DV), F32)],
        scratch_shapes=[pltpu.VMEM((HEADS, DK, DV), F32)],
        compiler_params=_params(("arbitrary", "arbitrary")),
    )(proj, proj, proj, proj, proj, wa, ba, gain)


def _gla_bwd(dzb, proj, states, wa, ba, gain, name):
    t = proj.shape[0]
    n, sp = _gla_specs(t, True)

    def body(dz_ref, q_ref, k_ref, v_ref, og_ref, alr_ref, st_ref, wa_ref, ba_ref, g_ref,
             dq_ref, dk_ref, dv_ref, dog_ref, dalr_ref, dwa_ref, dba_ref, dg_ref, ds_scr):
        i, h = pl.program_id(0), pl.program_id(1)

        @pl.when(i == 0)
        def _():
            ds_scr[h] = jnp.zeros((DK, DV), F32)
            dwa_ref[h] = jnp.zeros((128, DK), F32)
            dba_ref[h] = jnp.zeros((1, DK), F32)

        @pl.when((i == 0) & (h == 0))
        def _():
            dg_ref[...] = jnp.zeros_like(dg_ref)

        args = (q_ref[...].astype(F32), k_ref[...].astype(F32), v_ref[...].astype(F32), og_ref[...].astype(F32),
                alr_ref[...].astype(F32), st_ref[...], wa_ref[...].astype(F32), ba_ref[...], g_ref[...])
        _, vjp = jax.vjp(_gla_chunk, *args)
        dq, dk, dv, dog, dalr, ds_in, dwa, dba, dgain = vjp((dz_ref[...].astype(F32), ds_scr[h]))
        dq_ref[...] = dq.astype(BF16)
        dk_ref[...] = dk.astype(BF16)
        dv_ref[...] = dv.astype(BF16)
        dog_ref[...] = dog.astype(BF16)

        @pl.when(h == 0)
        def _():
            dalr_ref[...] = dalr

        @pl.when(h != 0)
        def _():
            dalr_ref[...] += dalr

        ds_scr[h] = ds_in
        dwa_ref[h] += dwa
        dba_ref[h] += dba
        dg_ref[...] += dgain

    whole = lambda shape: pl.BlockSpec(shape, lambda i, h, nd=len(shape): (0,) * nd)
    return pl.pallas_call(
        body, name=name, grid=(n, HEADS),
        in_specs=[sp["odv"], sp["q"], sp["k"], sp["v"], sp["og"], sp["alr"], sp["state"], sp["wa"], sp["ba"], sp["gain"]],
        out_specs=[sp["odk"], sp["odk"], sp["odv"], sp["odv"], sp["oalr"], whole((HEADS, 128, DK)), whole((HEADS, 1, DK)), whole((1, DV))],
        out_shape=[jax.ShapeDtypeStruct((t, HEADS * DK), BF16), jax.ShapeDtypeStruct((t, HEADS * DK), BF16),
                   jax.ShapeDtypeStruct((t, HEADS * DV), BF16), jax.ShapeDtypeStruct((t, HEADS * DV), BF16),
                   jax.ShapeDtypeStruct((t, 128), F32), jax.ShapeDtypeStruct((HEADS, 128, DK), F32),
                   jax.ShapeDtypeStruct((HEADS, 1, DK), F32), jax.ShapeDtypeStruct((1, DV), F32)],
        scratch_shapes=[pltpu.VMEM((HEADS, DK, DV), F32)],
        compiler_params=_params(("arbitrary", "arbitrary")),
    )(dzb, proj, proj, proj, proj, proj, states, wa, ba, gain)


def _local_step(x, p, tgt, gather_start, gather_finish, scatter_start, small):
    b_alpha, gain = small["b_alpha_up"], small["gla_head_gain"]
    gather_start(0, ())
    w = dict(gather_finish(0, ()))
    conv_w, w_alpha = w["conv_w"], w["w_alpha_up"]
    wa_p = jnp.zeros((128, HEADS * DK), BF16).at[:GATE_RANK].set(w_alpha.astype(BF16))

    t1 = gather_start(1, (w["in_t"],))
    h1 = _rms_fwd(x, small["g_pre_mix"], "rms_pre_mix")
    proj = _mm(h1, w["in_t"], "nt", BF16, "mm_proj", after=(t1,))
    t2 = gather_start(2, (proj,))
    za = _conv_fwd(proj, conv_w, "conv_fwd", after=(t2,))
    zb, states = _gla_fwd(proj, wa_p, b_alpha, gain, "gla_fwd")
    t3 = gather_start(3, (zb, za))
    w.update(gather_finish(1, (t3,)))
    ya = _mm(za, w["a_out"], "nn", BF16, "mm_ya", b3=True, tm=2048, tk=1024)
    yb = _mm(zb, w["b_out"], "nn", BF16, "mm_yb", b3=True, tm=2048, tk=1024)
    mix = _mix_fwd(proj, ya, yb, "mix_fwd")
    m2 = _mm(mix, w["mix"], "nn", F32, "mm_mix")
    t4 = gather_start(4, (m2,))
    x1, h2 = _post_pre(x, m2, small["g_post_mix"], small["g_pre_ffn"], "norm_mix_ffn")
    w.update(gather_finish(2, (h2, t4)))
    fg = _mm(h2, w["gate_t"], "nt", BF16, "mm_gate")
    fu = _mm(h2, w["up_t"], "nt", BF16, "mm_up")
    s = _swiglu_fwd(fg, fu, "swiglu_fwd")
    w.update(gather_finish(3, (s,)))
    f = _mm(s, w["down"], "nn", F32, "mm_down")
    x2, h3 = _post_pre(x1, f, small["g_post_ffn"], small["g_pre_ple"], "norm_ffn_ple")
    w.update(gather_finish(4, (h3,)))
    pg = _mm(h3, w["pg"], "nn", F32, "mm_pg")
    p_bf = p.astype(BF16)
    pp = _mm(p_bf, w["pp"], "nn", F32, "mm_pp", b3=True, tm=2048)
    loss_rows, d3, dpg, dpp, dg_post_ple = _ple_final(x2, pg, pp, tgt, small["g_post_ple"], "ple_final")

    gw = {}
    gw["pp"] = _mm(p_bf, dpp, "tn", BF16, "mm_dw_pp", out3=True)
    gw["pg"] = _mm(h3, dpg, "tn", BF16, "mm_dw_pg")
    dh3 = _mm(dpg, w["pg"], "nt", F32, "mm_dh3", after=(scatter_start(0, gw),))
    d2, df, dg_pre_ple, dg_post_ffn = _norm_bwd(d3, dh3, x2, small["g_pre_ple"], f, small["g_post_ffn"], "norm_bwd_ple_ffn")
    ds = _mm(df, w["down"], "nt", BF16, "mm_ds")
    gw["down"] = _mm(s, df, "tn", BF16, "mm_dw_down", tm=1408)
    dfg, dfu = _swiglu_bwd(ds, fg, fu, "swiglu_bwd")
    gw["gate_t"] = _mm(dfg, h2, "tn", BF16, "mm_dw_gate", after=(gw["down"],), tm=1408)
    gw["up_t"] = _mm(dfu, h2, "tn", BF16, "mm_dw_up", after=(gw["gate_t"],), tm=1408)
    dh2 = _mm(dfg, w["gate_t"], "nn", F32, "mm_dh2_gate", after=(scatter_start(1, gw),))
    dh2 = _mm(dfu, w["up_t"], "nn", F32, "mm_dh2_up", add=dh2)
    d1, dm2, dg_pre_ffn, dg_post_mix = _norm_bwd(d2, dh2, x1, small["g_pre_ffn"], m2, small["g_post_mix"], "norm_bwd_ffn_mix")
    dmix = _mm(dm2, w["mix"], "nt", F32, "mm_dmix")
    gw["mix"] = _mm(mix, dm2, "tn", BF16, "mm_dw_mix")
    dgab, dya, dyb = _mix_bwd(dmix, proj, ya, yb, "mix_bwd")
    dza = _mm(dya, w["a_out"], "nt", BF16, "mm_dza", after=(gw["mix"],), b3=True, tm=2048)
    gw["a_out"] = _mm(za, dya, "tn", BF16, "mm_dw_a_out", out3=True, tk=1024)
    gw["b_out"] = _mm(zb, dyb, "tn", BF16, "mm_dw_b_out", after=(gw["a_out"],), out3=True, tk=1024)
    dzb = _mm(dyb, w["b_out"], "nt", BF16, "mm_dzb", after=(scatter_start(2, gw),), b3=True, tm=2048)
    dax, dab, dac, dconv = _conv_bwd(dza, proj, conv_w, "conv_bwd")
    dq, dk, dv, dog, dalr, dwa, dba, dgain = _gla_bwd(dzb, proj, states, wa_p, b_alpha, gain, "gla_bwd")
    dproj = jnp.concatenate([dgab, dax, dab, dac, dq, dk, dv, dog, dalr.astype(BF16)], axis=1)
    gw["in_t"] = _mm(dproj, h1, "tn", BF16, "mm_dw_in", tm=1152)
    dh1 = _mm(dproj, w["in_t"], "nn", F32, "mm_dh1", after=(scatter_start(3, gw),))
    grad_x, dg_pre_mix = _norm_bwd(d1, dh1, x, small["g_pre_mix"], None, None, "norm_bwd_mix")

    gs = dict(
        conv_w=dconv,
        w_alpha_up=jnp.transpose(dwa[:, :GATE_RANK, :], (1, 0, 2)).reshape(GATE_RANK, HEADS * DK),
        b_alpha_up=dba.reshape(1, HEADS * DK), gla_head_gain=dgain,
        g_pre_mix=dg_pre_mix, g_post_mix=dg_post_mix, g_pre_ffn=dg_pre_ffn, g_post_ffn=dg_post_ffn,
        g_pre_ple=dg_pre_ple, g_post_ple=dg_post_ple,
    )
    return loss_rows, grad_x, gs


def _place():
    x, y, c = lax.axis_index("x"), lax.axis_index("y"), lax.axis_index("c")
    return x, y, c, [(1 - x, y), (x, 1 - y), (1 - x, 1 - y)]


def _all_gather(shards, name, cid=None):
    n = len(shards)

    def body(*refs):
        ins, outs = refs[:n], refs[n:2 * n]
        send_sems, recv_sems, local_sems = refs[2 * n:]
        x, y, c, chips = _place()
        me, sibling = (x, y, c), (x, y, 1 - c)

        def slot(px, py, pc):
            return 4 * px + 2 * py + pc

        def copy(a, k, block, to, src=None):
            dst = outs[a].at[slot(*block)]
            return pltpu.make_async_remote_copy(src_ref=dst if src is None else src, dst_ref=dst, send_sem=send_sems.at[a, k],
                                                recv_sem=recv_sems.at[a, k], device_id=to, device_id_type=MESH)

        mine = [pltpu.make_async_copy(ins[a], outs[a].at[slot(*me)], local_sems.at[a]) for a in range(n)]
        for cp in mine:
            cp.start()
        first = []
        for j, chip in enumerate(chips):
            first += [copy(a, 1 + j, me, (*chip, c), src=ins[a]) for a in range(n)]
        first += [copy(a, 0, me, sibling, src=ins[a]) for a in range(n)]
        for cp in first:
            cp.start()
        passed = []
        for j, chip in enumerate(chips):
            for a in range(n):
                copy(a, 1 + j, (*chip, c), me).wait_recv()
                cp = copy(a, 4 + j, (*chip, c), sibling)
                cp.start()
                passed.append(cp)
        for a in range(n):
            copy(a, 0, sibling, me).wait_recv()
        for j, chip in enumerate(chips):
            for a in range(n):
                copy(a, 4 + j, (*chip, 1 - c), me).wait_recv()
        for cp in first + passed:
            cp.wait_send()
        for cp in mine:
            cp.wait()

    if cid is None:
        return pl.pallas_call(
            body, name=name, in_specs=[ANY] * n, out_specs=[ANY] * n,
            out_shape=[jax.ShapeDtypeStruct((N_DEV,) + s.shape, s.dtype) for s in shards],
            scratch_shapes=[pltpu.SemaphoreType.DMA((n, 7)), pltpu.SemaphoreType.DMA((n, 7)), pltpu.SemaphoreType.DMA((n,))],
        )(*shards)

    src = [jax.new_ref(s, memory_space=pltpu.MemorySpace.HBM) for s in shards]
    dst = [jax.empty_ref(jax.ShapeDtypeStruct((N_DEV,) + s.shape, s.dtype), memory_space=pltpu.MemorySpace.HBM) for s in shards]

    @pl.kernel(mesh=plsc.ScalarSubcoreMesh(axis_name="seq", num_cores=1), name=name,
               scratch_types=(pltpu.SemaphoreType.DMA((n, 7)), pltpu.SemaphoreType.DMA((n, 7)), pltpu.SemaphoreType.DMA((n,))),
               compiler_params=pltpu.CompilerParams(collective_id=cid))
    def launch(send_sems, recv_sems, local_sems):
        x, y, c, chips = _place()
        barrier = pltpu.get_barrier_semaphore()
        for peer in [(x, y, 1 - c)] + [(*chip, c) for chip in chips]:
            pl.semaphore_signal(barrier, inc=1, device_id=peer, device_id_type=MESH)
        pl.semaphore_wait(barrier, 4)
        body(*src, *dst, send_sems, recv_sems, local_sems)

    launch()
    return [r[...] for r in dst]


def _reduce_scatter(parts, name, cid):
    n = len(parts)
    src = [jax.new_ref(s, memory_space=pltpu.MemorySpace.HBM) for s in parts]
    dst = [jax.empty_ref(jax.ShapeDtypeStruct(s.shape, s.dtype), memory_space=pltpu.MemorySpace.HBM) for s in parts]

    @pl.kernel(mesh=plsc.ScalarSubcoreMesh(axis_name="seq", num_cores=1), name=name,
               scratch_types=(pltpu.SemaphoreType.DMA((n, N_DEV - 1)), pltpu.SemaphoreType.DMA((n, N_DEV - 1)), pltpu.SemaphoreType.DMA((n,))),
               compiler_params=pltpu.CompilerParams(collective_id=cid))
    def launch(send_sems, recv_sems, local_sems):
        x, y, c, _ = _place()
        me = 4 * x + 2 * y + c
        peers = [(1 - x if k & 4 else x, 1 - y if k & 2 else y, 1 - c if k & 1 else c) for k in range(1, N_DEV)]
        barrier = pltpu.get_barrier_semaphore()
        for peer in peers:
            pl.semaphore_signal(barrier, inc=1, device_id=peer, device_id_type=MESH)
        pl.semaphore_wait(barrier, N_DEV - 1)
        mine = [pltpu.make_async_copy(src[a].at[me], dst[a].at[me], local_sems.at[a]) for a in range(n)]
        for cp in mine:
            cp.start()
        cps = []
        for a in range(n):
            for k, (px, py, pc) in enumerate(peers):
                cps.append(pltpu.make_async_remote_copy(src_ref=src[a].at[4 * px + 2 * py + pc], dst_ref=dst[a].at[me], send_sem=send_sems.at[a, k],
                                                        recv_sem=recv_sems.at[a, k], device_id=(px, py, pc), device_id_type=MESH))
        for cp in cps:
            cp.start()
        for cp in cps:
            cp.wait_recv()
        for cp in cps:
            cp.wait_send()
        for cp in mine:
            cp.wait()

    launch()
    return [r[...] for r in dst]


def _sibling_exchange(parts, name):
    n = len(parts)
    pieces = [_row_pieces(s.shape[1]) for s in parts]

    def body(*refs):
        ins, outs = refs[:n], refs[n:2 * n]
        send_sems, recv_sems = refs[2 * n:]
        x, y, c, _ = _place()

        def copy(a, ch, q, rows):
            return pltpu.make_async_remote_copy(src_ref=ins[a].at[2 * ch + 1 - c, rows], dst_ref=outs[a].at[ch, rows], send_sem=send_sems.at[a, ch, q],
                                                recv_sem=recv_sems.at[a, ch, q], device_id=(x, y, 1 - c), device_id_type=MESH)

        cps = [copy(a, ch, q, rows) for ch in range(4) for a in range(n) for q, rows in enumerate(pieces[a])]
        for cp in cps:
            cp.start()
        for cp in cps:
            cp.wait_recv()
        for cp in cps:
            cp.wait_send()

    return pl.pallas_call(
        body, name=name, in_specs=[ANY] * n, out_specs=[ANY] * n,
        out_shape=[jax.ShapeDtypeStruct((4,) + s.shape[1:], s.dtype) for s in parts],
        scratch_shapes=[pltpu.SemaphoreType.DMA((n, 4, PIECES)), pltpu.SemaphoreType.DMA((n, 4, PIECES))],
    )(*parts)


def _chip_exchange(parts, name):
    n = len(parts)

    def body(*refs):
        ins, outs = refs[:n], refs[n:2 * n]
        send_sems, recv_sems, local_sems = refs[2 * n:]
        x, y, c, chips = _place()
        my_chip = 2 * x + y

        def copy(a, j):
            px, py = chips[j]
            return pltpu.make_async_remote_copy(src_ref=ins[a].at[2 * px + py], dst_ref=outs[a].at[my_chip], send_sem=send_sems.at[a, j],
                                                recv_sem=recv_sems.at[a, j], device_id=(px, py, c), device_id_type=MESH)

        def landing(a, j):
            px, py = chips[j]
            return pltpu.make_async_remote_copy(src_ref=ins[a].at[my_chip], dst_ref=outs[a].at[2 * px + py], send_sem=send_sems.at[a, j],
                                                recv_sem=recv_sems.at[a, j], device_id=(px, py, c), device_id_type=MESH)

        mine = [pltpu.make_async_copy(ins[a].at[my_chip], outs[a].at[my_chip], local_sems.at[a]) for a in range(n)]
        for cp in mine:
            cp.start()
        cps = [copy(a, j) for j in range(3) for a in range(n)]
        for cp in cps:
            cp.start()
        for j in range(3):
            for a in range(n):
                landing(a, j).wait_recv()
        for cp in cps:
            cp.wait_send()
        for cp in mine:
            cp.wait()

    return pl.pallas_call(
        body, name=name, in_specs=[ANY] * n, out_specs=[ANY] * n,
        out_shape=[jax.ShapeDtypeStruct(s.shape, s.dtype) for s in parts],
        scratch_shapes=[pltpu.SemaphoreType.DMA((n, 3)), pltpu.SemaphoreType.DMA((n, 3)), pltpu.SemaphoreType.DMA((n,))],
    )(*parts)


def _pair_add(mine8, got4, name):
    _, r, cols = mine8.shape
    steps, blk, at = _tiles(r, cols)
    core = lax.axis_index("c").astype(jnp.int32).reshape(1)

    def body(c_ref, a_ref, b_ref, o_ref):
        o_ref[...] = (a_ref[...].astype(F32) + b_ref[...].astype(F32)).astype(BF16)

    return pl.pallas_call(
        body, name=name,
        grid_spec=pltpu.PrefetchScalarGridSpec(
            num_scalar_prefetch=1, grid=(4, steps),
            in_specs=[pl.BlockSpec((None,) + blk, lambda ch, i, c_ref: (2 * ch + c_ref[0],) + at(i)),
                      pl.BlockSpec((None,) + blk, lambda ch, i, c_ref: (ch,) + at(i))],
            out_specs=pl.BlockSpec((None,) + blk, lambda ch, i, c_ref: (ch,) + at(i))),
        out_shape=jax.ShapeDtypeStruct((4, r, cols), BF16),
        compiler_params=_params(("parallel", "parallel")),
    )(core, mine8, got4)


HBM = pl.BlockSpec(memory_space=pltpu.HBM)
SEM = pl.BlockSpec(memory_space=pltpu.SEMAPHORE)
EFFECT = pltpu.SideEffectType.DATAFLOW_SIDE_EFFECTING


def _in_hbm(a):
    return pltpu.with_memory_space_constraint(a, pltpu.HBM)


def _remote_copies(plan, srcs, lands, send_sems, recv_sems):
    return [pltpu.make_async_remote_copy(src_ref=s, dst_ref=d, send_sem=send_sems.at[i], recv_sem=recv_sems.at[i], device_id=peer,
                                         device_id_type=MESH) for i, (s, d, peer) in enumerate(plan(srcs, lands))]


def _copies_start(plan, n_copies, srcs, land_shapes, name, after=()):
    ns, nl = len(srcs), len(land_shapes)

    def body(*refs):
        send_sems, recv_sems = refs[ns + nl + len(after):ns + nl + len(after) + 2]
        for cp in _remote_copies(plan, refs[:ns], refs[ns:ns + nl], send_sems, recv_sems):
            cp.start()
        refs[-1][...] = jnp.zeros((8, 128), F32)

    sems = pltpu.SemaphoreType.DMA((n_copies,))
    return pl.pallas_call(
        body, name=name,
        out_shape=(sems, sems, *[pltpu.HBM(s.shape, s.dtype) for s in srcs], *[pltpu.HBM(s.shape, s.dtype) for s in land_shapes],
                   jax.ShapeDtypeStruct((8, 128), F32)),
        in_specs=[HBM] * (ns + nl) + [ANY] * len(after),
        out_specs=(SEM, SEM, *[HBM] * (ns + nl), pl.BlockSpec(memory_space=pltpu.VMEM)),
        input_output_aliases={i: 2 + i for i in range(ns + nl)},
        compiler_params=pltpu.CompilerParams(has_side_effects=EFFECT),
    )(*[_in_hbm(s) for s in srcs], *[_in_hbm(lax.empty(s.shape, s.dtype)) for s in land_shapes], *after)


def _copies_wait(plan, state, ns, name, after=()):
    send_sems, recv_sems, *arrs = state[:-1]
    n = len(arrs)

    def body(*refs):
        cps = _remote_copies(plan, refs[:ns], refs[ns:n], refs[n], refs[n + 1])
        for cp in cps:
            cp.wait_send()
        for cp in cps:
            cp.wait_recv()

    out = pl.pallas_call(
        body, name=name, out_shape=tuple(pltpu.HBM(a.shape, a.dtype) for a in arrs),
        in_specs=[HBM] * n + [SEM, SEM] + [ANY] * len(after), out_specs=tuple([HBM] * n),
        input_output_aliases={i: i for i in range(n)},
        compiler_params=pltpu.CompilerParams(has_side_effects=EFFECT),
    )(*arrs, send_sems, recv_sems, *after)
    return list(out[:ns]), list(out[ns:])


def _gather_plan(srcs, lands):
    x, y, c, chips = _place()
    peers = [(x, y, 1 - c)] + [(*chip, c) for chip in chips]
    return [(s, l.at[4 * x + 2 * y + c], peer) for s, l in zip(srcs, lands) for peer in peers]


def _scatter_plan(srcs, lands):
    x, y, c, _ = _place()
    peers = [(1 - x if k & 4 else x, 1 - y if k & 2 else y, 1 - c if k & 1 else c) for k in range(1, N_DEV)]
    return [(s.at[4 * px + 2 * py + pc], l.at[4 * x + 2 * y + c], (px, py, pc)) for s, l in zip(srcs, lands) for px, py, pc in peers]


def _everyone_plan(srcs, lands):
    x, y, c, _ = _place()
    peers = [(1 - x if k & 4 else x, 1 - y if k & 2 else y, 1 - c if k & 1 else c) for k in range(1, N_DEV)]
    return [(s, l.at[4 * x + 2 * y + c], peer) for s, l in zip(srcs, lands) for peer in peers]


def _sum_parts(got, own, me, name):
    def body(me_ref, got_ref, own_ref, o_ref):
        acc = jnp.where(me_ref[0] == 0, own_ref[...], got_ref[0])
        for d in range(1, N_DEV):
            acc = acc + jnp.where(me_ref[0] == d, own_ref[...], got_ref[d])
        o_ref[...] = acc

    return pl.pallas_call(
        body, name=name,
        grid_spec=pltpu.PrefetchScalarGridSpec(
            num_scalar_prefetch=1, grid=(1,),
            in_specs=[pl.BlockSpec(got.shape, lambda i, me_ref: (0, 0, 0)), pl.BlockSpec(own.shape, lambda i, me_ref: (0, 0))],
            out_specs=pl.BlockSpec(own.shape, lambda i, me_ref: (0, 0))),
        out_shape=jax.ShapeDtypeStruct(own.shape, F32),
    )(me.astype(jnp.int32).reshape(1), got, own)


def _chip_plan(srcs, lands):
    x, y, c, chips = _place()
    return [(s.at[2 * px + py], l.at[2 * x + y], (px, py, c)) for s, l in zip(srcs, lands) for px, py in chips]


PIECES = 8


def _row_pieces(rows):
    for k in (PIECES, 4, 2):
        if rows % (16 * k) == 0:
            return [pl.ds(q * (rows // k), rows // k) for q in range(k)]
    return [pl.ds(0, rows)]


def _put_own(shard, zone, me, name):
    r, c = shard.shape
    tr = r if r <= 256 else _pick(r, (256, 64))

    def body(me_ref, s_ref, z_ref, o_ref):
        o_ref[...] = s_ref[...]

    return pl.pallas_call(
        body, name=name,
        grid_spec=pltpu.PrefetchScalarGridSpec(
            num_scalar_prefetch=1, grid=(r // tr,),
            in_specs=[pl.BlockSpec((tr, c), lambda i, me_ref: (i, 0)), ANY],
            out_specs=pl.BlockSpec((None, tr, c), lambda i, me_ref: (me_ref[0], i, 0))),
        out_shape=jax.ShapeDtypeStruct(zone.shape, zone.dtype), input_output_aliases={2: 0},
        compiler_params=_params(("arbitrary",)),
    )(me.astype(jnp.int32).reshape(1), shard, zone)


def _gather_finish(lands, name):
    n = len(lands)
    pieces = [_row_pieces(s.shape[1]) for s in lands]

    def body(*refs):
        zones, outs = refs[:n], refs[n:2 * n]
        send_sems, recv_sems = refs[2 * n:]
        x, y, c, chips = _place()
        cps = []
        for j, (px, py) in enumerate(chips):
            for a in range(n):
                for q, rows in enumerate(pieces[a]):
                    cps.append(pltpu.make_async_remote_copy(
                        src_ref=zones[a].at[4 * px + 2 * py + c, rows], dst_ref=outs[a].at[4 * px + 2 * py + c, rows],
                        send_sem=send_sems.at[a, j, q], recv_sem=recv_sems.at[a, j, q], device_id=(x, y, 1 - c), device_id_type=MESH))
        for cp in cps:
            cp.start()
        for cp in cps:
            cp.wait_recv()
        for cp in cps:
            cp.wait_send()

    return pl.pallas_call(
        body, name=name, in_specs=[ANY] * n, out_specs=[ANY] * n,
        out_shape=[jax.ShapeDtypeStruct(l.shape, l.dtype) for l in lands],
        input_output_aliases={a: a for a in range(n)},
        scratch_shapes=[pltpu.SemaphoreType.DMA((n, 3, PIECES)), pltpu.SemaphoreType.DMA((n, 3, PIECES))],
    )(*lands)


def _sum_everywhere(v, name):
    rows = v.shape[0]

    def body(v_ref, o_ref, buf, send_sems, recv_sems):
        x, y, c, _ = _place()
        me = 4 * x + 2 * y + c
        buf[me] = v_ref[...]
        cps = []
        for k in range(1, N_DEV):
            fx, fy, fc = (k >> 2) & 1, (k >> 1) & 1, k & 1
            to = (1 - x if fx else x, 1 - y if fy else y, 1 - c if fc else c)
            cps.append(pltpu.make_async_remote_copy(src_ref=buf.at[me], dst_ref=buf.at[me], send_sem=send_sems.at[k - 1],
                                                    recv_sem=recv_sems.at[k - 1], device_id=to, device_id_type=MESH))
        for cp in cps:
            cp.start()
        for cp in cps:
            cp.wait_recv()
        for cp in cps:
            cp.wait_send()
        acc = buf[0]
        for d in range(1, N_DEV):
            acc = acc + buf[d]
        o_ref[...] = acc

    vm = pl.BlockSpec(memory_space=pltpu.VMEM)
    return pl.pallas_call(
        body, name=name, in_specs=[vm], out_specs=vm, out_shape=jax.ShapeDtypeStruct(v.shape, F32),
        scratch_shapes=[pltpu.VMEM((N_DEV, rows, 128), F32), pltpu.SemaphoreType.DMA((N_DEV - 1,)), pltpu.SemaphoreType.DMA((N_DEV - 1,))],
    )(v)


def _adamw_parts(w, got, mine, me, m, v, name, after=()):
    r, c = w.shape
    n_parts = got.shape[0]
    steps, blk, at = _tiles(r, c)

    def body(me_ref, w_ref, got_ref, own_ref, m_ref, v_ref, *rest):
        go_ref, d_ref, mo_ref, vo_ref = rest[len(after):]
        own = own_ref[...].astype(F32)
        gv = jnp.where(me_ref[0] == 0, own, got_ref[0].astype(F32))
        for d in range(1, n_parts):
            gv = gv + jnp.where(me_ref[0] == d, own, got_ref[d].astype(F32))
        _adamw_math(gv, w_ref, m_ref, v_ref, go_ref, d_ref, mo_ref, vo_ref)

    tile = pl.BlockSpec(blk, lambda i, me_ref: at(i))
    out = jax.ShapeDtypeStruct((r, c), F32)
    return pl.pallas_call(
        body, name=name,
        grid_spec=pltpu.PrefetchScalarGridSpec(
            num_scalar_prefetch=1, grid=(steps,),
            in_specs=[tile, pl.BlockSpec((n_parts,) + blk, lambda i, me_ref: (0,) + at(i)),
                      pl.BlockSpec((None,) + blk, lambda i, me_ref: (me_ref[0],) + at(i)), tile, tile] + [ANY] * len(after),
            out_specs=[tile] * 4),
        out_shape=[out] * 4, compiler_params=_params(("parallel",)),
    )(me.astype(jnp.int32).reshape(1), w, got, mine, m, v, *after)


def _adamw_math(gv, w_ref, m_ref, v_ref, go_ref, d_ref, mo_ref, vo_ref):
    mn = B1 * m_ref[...] + (1.0 - B1) * gv
    vn = B2 * v_ref[...] + (1.0 - B2) * (gv * gv)
    m_hat = mn / (1.0 - B1 ** STEP)
    v_hat = vn / (1.0 - B2 ** STEP)
    go_ref[...] = gv
    d_ref[...] = -LR * (m_hat / (jnp.sqrt(v_hat) + ADAM_EPS) + WD * w_ref[...])
    mo_ref[...] = mn
    vo_ref[...] = vn


def _adamw(w, g, m, v, name):
    r, c = w.shape
    parts = g.ndim == 3
    tr = r if r <= 128 else _pick(r, (128, 64))

    def body(w_ref, g_ref, m_ref, v_ref, go_ref, d_ref, mo_ref, vo_ref):
        if parts:
            gv = g_ref[0].astype(F32)
            for d in range(1, g.shape[0]):
                gv = gv + g_ref[d].astype(F32)
        else:
            gv = g_ref[...]
        mn = B1 * m_ref[...] + (1.0 - B1) * gv
        vn = B2 * v_ref[...] + (1.0 - B2) * (gv * gv)
        m_hat = mn / (1.0 - B1 ** STEP)
        v_hat = vn / (1.0 - B2 ** STEP)
        go_ref[...] = gv
        d_ref[...] = -LR * (m_hat / (jnp.sqrt(v_hat) + ADAM_EPS) + WD * w_ref[...])
        mo_ref[...] = mn
        vo_ref[...] = vn

    tile = pl.BlockSpec((tr, c), lambda i: (i, 0))
    g_spec = pl.BlockSpec((g.shape[0], tr, c), lambda i: (0, i, 0)) if parts else tile
    out = jax.ShapeDtypeStruct((r, c), F32)
    return pl.pallas_call(
        body, name=name, grid=(r // tr,), in_specs=[tile, g_spec, tile, tile], out_specs=[tile] * 4, out_shape=[out] * 4,
        compiler_params=_params(("parallel",)),
    )(w, g, m, v)


BIG = ["w_in", "w_a_out", "w_b_out", "w_mix_out", "w_ff_gate", "w_ff_up", "w_ff_down", "w_ple_gate", "w_ple_proj"]
TRANSPOSED = ["w_in", "w_ff_gate", "w_ff_up"]
SMALL = ["conv_w", "w_alpha_up", "b_alpha_up", "gla_head_gain", "g_pre_mix", "g_post_mix", "g_pre_ffn", "g_post_ffn", "g_pre_ple", "g_post_ple"]
WEIGHTS = ["w_in", "conv_w", "w_a_out", "w_alpha_up", "b_alpha_up", "gla_head_gain", "w_b_out", "w_mix_out", "g_pre_mix", "g_post_mix",
           "g_pre_ffn", "g_post_ffn", "w_ff_gate", "w_ff_up", "w_ff_down", "g_pre_ple", "g_post_ple", "w_ple_gate", "w_ple_proj"]


def _cols_to_full(g8):
    n, r, c = g8.shape
    return jnp.transpose(g8, (1, 0, 2)).reshape(r, n * c)


def _full_to_cols(a):
    r, c = a.shape
    return jnp.transpose(a.reshape(r, N_DEV, c // N_DEV), (1, 0, 2))


def _pack(arrs, rows):
    flat = jnp.concatenate([a.reshape(-1) for a in arrs])
    return jnp.pad(flat, (0, rows * 128 - flat.shape[0])).reshape(rows, 128)


def _unpack(packed, shapes):
    flat, out, o = packed.reshape(-1), [], 0
    for s in shapes:
        size = 1
        for d in s:
            size *= d
        out.append(flat[o:o + size].reshape(s))
        o += size
    return out


def kernel(x, p, w_in, conv_w, w_a_out, w_alpha_up, b_alpha_up, gla_head_gain, w_b_out, w_mix_out, g_pre_mix, g_post_mix, g_pre_ffn, g_post_ffn, w_ff_gate, w_ff_up, w_ff_down, g_pre_ple, g_post_ple, w_ple_gate, w_ple_proj, loss_target, m_w_in, m_conv_w, m_w_a_out, m_w_alpha_up, m_b_alpha_up, m_gla_head_gain, m_w_b_out, m_w_mix_out, m_g_pre_mix, m_g_post_mix, m_g_pre_ffn, m_g_post_ffn, m_w_ff_gate, m_w_ff_up, m_w_ff_down, m_g_pre_ple, m_g_post_ple, m_w_ple_gate, m_w_ple_proj, v_w_in, v_conv_w, v_w_a_out, v_w_alpha_up, v_b_alpha_up, v_gla_head_gain, v_w_b_out, v_w_mix_out, v_g_pre_mix, v_g_post_mix, v_g_pre_ffn, v_g_post_ffn, v_w_ff_gate, v_w_ff_up, v_w_ff_down, v_g_pre_ple, v_g_post_ple, v_w_ple_gate, v_w_ple_proj):
    args = dict(locals())
    shard = lambda n, a: jnp.transpose(a[0]) if n in TRANSPOSED else a[0]
    wts = {n: shard(n, args[n]) for n in WEIGHTS}
    mom = {n: shard(n, args["m_" + n]) for n in WEIGHTS}
    var = {n: shard(n, args["v_" + n]) for n in WEIGHTS}
    me =4 * lax.axis_index("x") + 2 * lax.axis_index("y") + lax.axis_index("c")

    groups = [["w_in", "conv_w", "w_alpha_up"], ["w_a_out", "w_b_out", "w_mix_out"], ["w_ff_gate", "w_ff_up"], ["w_ff_down"],
              ["w_ple_gate", "w_ple_proj"]]
    grad_groups = [["w_ple_proj", "w_ple_gate"], ["w_ff_down", "w_ff_gate", "w_ff_up"], ["w_mix_out", "w_a_out", "w_b_out"], ["w_in"]]
    rows_full = lambda g: g.reshape(-1, g.shape[-1])
    gathers, scatters = {}, {}

    def gather_start(gi, after):
        shards = [wts[n].astype(BF16) if n in BIG else wts[n] for n in groups[gi]]
        zones = [jax.ShapeDtypeStruct((N_DEV,) + s.shape, s.dtype) for s in shards]
        gathers[gi] = (shards, _copies_start(_gather_plan, 4 * len(shards), shards, zones, "gather_start_%d" % gi, after))
        return gathers[gi][1][-1]

    def gather_finish(gi, after):
        shards, state = gathers[gi]
        shards, zones = _copies_wait(_gather_plan, state, len(shards), "gather_wait_%d" % gi, after)
        zones = _gather_finish(zones, "gather_finish_%d" % gi)
        g8 = {n: _put_own(s, z, me, "gather_own_" + n) for n, s, z in zip(groups[gi], shards, zones)}
        if gi == 0:
            w_in_t = rows_full(g8["w_in"])
            return dict(in_t=jnp.concatenate([w_in_t[R_GA:R_END], w_in_t[:R_ALR], w_in_t[R_ALR:R_GA],
                                              jnp.zeros((128 - GATE_RANK, w_in_t.shape[1]), BF16)], axis=0),
                        conv_w=_cols_to_full(g8["conv_w"]), w_alpha_up=_cols_to_full(g8["w_alpha_up"]))
        if gi == 1:
            return dict(a_out=g8["w_a_out"], b_out=g8["w_b_out"], mix=rows_full(g8["w_mix_out"]))
        if gi == 2:
            return dict(gate_t=rows_full(g8["w_ff_gate"]), up_t=rows_full(g8["w_ff_up"]))
        if gi == 3:
            return dict(down=rows_full(g8["w_ff_down"]))
        return dict(pg=rows_full(g8["w_ple_gate"]), pp=g8["w_ple_proj"])

    def scatter_start(gi, gw):
        if gi == 3:
            g_in = gw["in_t"]
            full = dict(w_in=jnp.concatenate([g_in[C_AX:C_ALR], g_in[C_ALR:C_ALR + GATE_RANK], g_in[:C_AX]], axis=0))
        elif gi == 1:
            full = dict(w_ff_down=gw["down"], w_ff_gate=gw["gate_t"], w_ff_up=gw["up_t"])
        elif gi == 2:
            full = dict(w_mix_out=gw["mix"], w_a_out=gw["a_out"], w_b_out=gw["b_out"])
        else:
            full = dict(w_ple_proj=gw["pp"], w_ple_gate=gw["pg"])
        parts = [full[n] if full[n].ndim == 3 else full[n].reshape(N_DEV, -1, full[n].shape[-1]) for n in grad_groups[gi]]
        if gi == 3:
            from_sibling = _sibling_exchange(parts, "scatter_sibling_%d" % gi)
            parts = [_pair_add(a, b, "scatter_add_%d_%s" % (gi, n)) for n, a, b in zip(grad_groups[gi], parts, from_sibling)]
            scatters[gi] = _copies_start(_chip_plan, 3 * len(parts), parts, parts, "scatter_start_%d" % gi)
        else:
            scatters[gi] = _copies_start(_scatter_plan, (N_DEV - 1) * len(parts), parts, parts, "scatter_start_%d" % gi)
        return scatters[gi][-1]

    small = {n: wts[n].reshape(1, -1) for n in SMALL[2:]}

    loss_rows, grad_x, gs = _local_step(x[0], p[0, 0], loss_target[0], gather_start, gather_finish, scatter_start, small)
    loss = lax.psum(jnp.sum(loss_rows), ("x", "y", "c"))

    small_shapes = [gs[n].shape for n in SMALL]
    gs_packed = _pack([gs[n] for n in SMALL], 192)
    small_state = _copies_start(_everyone_plan, N_DEV - 1, [gs_packed], [jax.ShapeDtypeStruct((N_DEV,) + gs_packed.shape, F32)],
                                "small_start", (grad_x,))

    res, done = {}, (small_state[-1],)
    for gi, names in enumerate(grad_groups):
        plan, slot = (_chip_plan, me // 2) if gi == 3 else (_scatter_plan, me)
        mine, got = _copies_wait(plan, scatters[gi], len(names), "scatter_wait_%d" % gi, done)
        for n, g, own in zip(names, got, mine):
            res[n] = _adamw_parts(wts[n], g, own, slot, mom[n], var[n], "adamw_" + n)
        done = tuple(res[n][1] for n in names)

    (gs_own,), (gs_got,) = _copies_wait(_everyone_plan, small_state, 1, "small_wait", done)
    gsum = dict(zip(SMALL, _unpack(_sum_parts(gs_got, gs_own, me, "small_sum"), small_shapes)))
    gsum["conv_w"] = lax.dynamic_index_in_dim(gsum["conv_w"].reshape(3, N_DEV, -1), me, axis=1, keepdims=False)
    gsum["w_alpha_up"] = lax.dynamic_index_in_dim(gsum["w_alpha_up"].reshape(GATE_RANK, N_DEV, -1), me, axis=1, keepdims=False)

    shard_shapes = [wts[n].shape for n in SMALL]
    packed = [_pack([d[n] for n in SMALL], 120) for d in (wts, gsum, mom, var)]
    outs = [_unpack(o, shard_shapes) for o in _adamw(*packed, "adamw_small")]
    for i, n in enumerate(SMALL):
        res[n] = [o[i] for o in outs]

    back = lambda n, a: (jnp.transpose(a) if n in TRANSPOSED else a)[None]
    return (loss, grad_x[None], *[back(n, res[n][i]) for i in range(4) for n in WEIGHTS])
```

```python
import jax
import jax.numpy as jnp
from jax import lax
from jax.experimental import pallas as pl
from jax.experimental.pallas import tpu as pltpu
from jax.experimental.pallas import tpu_sc as plsc

F32, BF16 = jnp.float32, jnp.bfloat16
EPS = 1e-6
CHUNK = 64
HEADS, DK, DV = 4, 128, 256
GATE_RANK = 16
TAU = 16.0
LR, B1, B2, ADAM_EPS, WD, STEP = 0.001, 0.9, 0.999, 1e-08, 0.01, 10
N_DEV = 8
MESH = pl.DeviceIdType.MESH
VMEM_LIMIT = 56 * 1024 * 1024
ANY = pl.BlockSpec(memory_space=pl.ANY)

C_GA, C_GB, C_AX, C_AB, C_AC, C_Q, C_K, C_V, C_OG, C_ALR = 0, 2048, 4096, 5120, 6144, 7168, 7680, 8192, 9216, 10240
IN_PAD = 10368
R_AX, R_AB, R_AC, R_Q, R_K, R_V, R_OG, R_ALR, R_GA, R_GB, R_END = 0, 1024, 2048, 3072, 3584, 4096, 5120, 6144, 6160, 8208, 10256


def _params(sem):
    return pltpu.CompilerParams(dimension_semantics=sem, vmem_limit_bytes=VMEM_LIMIT)


def _pick(n, cands):
    for c in cands:
        if n % c == 0:
            return c
    return n


def _tiles(r, c):
    for tr in (128, 64):
        if r % tr == 0:
            return r // tr, (tr, c), lambda i: (i, 0)
    tc = _pick(c, (256, 128))
    return c // tc, (r, tc), lambda i: (0, i)


def _mm(a, b, mode, out_dtype, name, after=(), add=None, b3=False, out3=False, tm=None, tk=None):
    bshape = (b.shape[1], N_DEV * b.shape[2]) if b3 else b.shape
    if mode == "nn":
        (m, k), (k2, n) = a.shape, bshape
    elif mode == "nt":
        (m, k), (n, k2) = a.shape, bshape
    else:
        (k, m), (k2, n) = a.shape, bshape
    assert k == k2 and a.dtype == BF16 and b.dtype == BF16, (name, a.shape, b.shape, a.dtype, b.dtype)
    tm = tm if tm and m % tm == 0 else _pick(m, (2048, 1024, 512, 256))
    tn = _pick(n, (1152, 1024, 1408, 512, 256))
    tk = tk if tk and k % tk == 0 else _pick(k, (2048, 1408, 1152, 1024, 512, 256))
    if out3 or (b3 and mode == "nn"):
        tn = n // N_DEV
    if b3 and mode == "nt":
        tk = k // N_DEV
    nk = k // tk
    dims = {"nn": (((1,), (0,)), ((), ())), "nt": (((1,), (1,)), ((), ())), "tn": (((0,), (0,)), ((), ()))}[mode]
    n_extra = len(after) + (add is not None)

    def body(a_ref, b_ref, *rest):
        o_ref = rest[n_extra]
        prod = lax.dot_general(a_ref[...], b_ref[...], dims, preferred_element_type=F32)
        if nk == 1:
            o_ref[...] = (prod if add is None else prod + rest[0][...]).astype(o_ref.dtype)
            return
        acc_ref = rest[n_extra + 1]
        kk = pl.program_id(2)

        @pl.when(kk == 0)
        def _():
            acc_ref[...] = prod if add is None else prod + rest[0][...]

        @pl.when((kk > 0) & (kk < nk - 1))
        def _():
            acc_ref[...] += prod

        @pl.when(kk == nk - 1)
        def _():
            o_ref[...] = (acc_ref[...] + prod).astype(o_ref.dtype)

    a_spec = pl.BlockSpec((tk, tm), lambda i, j, kk: (kk, i)) if mode == "tn" else pl.BlockSpec((tm, tk), lambda i, j, kk: (i, kk))
    if b3:
        b_spec = (pl.BlockSpec((None, tn, tk), lambda i, j, kk: (kk, j, 0)) if mode == "nt"
                  else pl.BlockSpec((None, tk, tn), lambda i, j, kk: (j, kk, 0)))
    else:
        b_spec = pl.BlockSpec((tn, tk), lambda i, j, kk: (j, kk)) if mode == "nt" else pl.BlockSpec((tk, tn), lambda i, j, kk: (kk, j))
    tile = pl.BlockSpec((tm, tn), lambda i, j, kk: (i, j))
    out_spec = pl.BlockSpec((None, tm, tn), lambda i, j, kk: (j, i, 0)) if out3 else tile
    return pl.pallas_call(
        body, name=name, grid=(m // tm, n // tn, nk),
        in_specs=[a_spec, b_spec] + ([tile] if add is not None else []) + [ANY] * len(after), out_specs=out_spec,
        out_shape=jax.ShapeDtypeStruct((N_DEV, m, tn) if out3 else (m, n), out_dtype),
        scratch_shapes=[pltpu.VMEM((tm, tn), F32)] if nk > 1 else [],
        compiler_params=_params(("parallel", "parallel", "arbitrary")),
    )(a, b, *([add] if add is not None else []), *after)


def _rows(body, t, tr, ins, outs, name):
    in_specs = []
    for arr, sp in ins:
        if sp[0] == "t":
            in_specs.append(pl.BlockSpec((tr, sp[1]), lambda i, cb=sp[2]: (i, cb)))
        else:
            in_specs.append(pl.BlockSpec(arr.shape, lambda i, nd=arr.ndim: (0,) * nd))
    out_specs, out_shape = [], []
    for shape, dt, kind in outs:
        out_specs.append(pl.BlockSpec((tr, shape[1]), lambda i: (i, 0)) if kind == "t" else pl.BlockSpec(shape, lambda i: (0, 0)))
        out_shape.append(jax.ShapeDtypeStruct(shape, dt))
    return pl.pallas_call(
        body, name=name, grid=(t // tr,), in_specs=in_specs, out_specs=out_specs, out_shape=out_shape,
        compiler_params=_params(("arbitrary",)),
    )(*[arr for arr, _ in ins])


def _rinv(v):
    return lax.rsqrt(jnp.mean(v * v, axis=-1, keepdims=True) + EPS)


def _sig(v):
    return 1.0 / (1.0 + jnp.exp(-v))


def _acc(ref, val):
    @pl.when(pl.program_id(0) == 0)
    def _():
        ref[...] = jnp.zeros_like(ref)

    ref[...] += jnp.sum(val, axis=0, keepdims=True)


def _rms_fwd(x, g, name):
    t, d = x.shape

    def body(x_ref, g_ref, h_ref):
        xv = x_ref[...]
        h_ref[...] = (xv * _rinv(xv) * g_ref[...]).astype(BF16)

    return _rows(body, t, 256, [(x, ("t", d, 0)), (g, ("b",))], [((t, d), BF16, "t")], name)[0]


def _post_pre(x, m, g_post, g_pre, name):
    t, d = x.shape

    def body(x_ref, m_ref, gp_ref, gn_ref, xo_ref, h_ref):
        mv = m_ref[...]
        xn = x_ref[...] + mv * _rinv(mv) * gp_ref[...]
        xo_ref[...] = xn
        h_ref[...] = (xn * _rinv(xn) * gn_ref[...]).astype(BF16)

    return _rows(body, t, 128, [(x, ("t", d, 0)), (m, ("t", d, 0)), (g_post, ("b",)), (g_pre, ("b",))],
                 [((t, d), F32, "t"), ((t, d), BF16, "t")], name)


def _mix_fwd(proj, ya, yb, name):
    t, d = ya.shape

    def body(ga_ref, gb_ref, ya_ref, yb_ref, o_ref):
        o_ref[...] = (_sig(ga_ref[...].astype(F32)) * ya_ref[...].astype(F32)
                      + _sig(gb_ref[...].astype(F32)) * yb_ref[...].astype(F32)).astype(BF16)

    return _rows(body, t, 256, [(proj, ("t", d, C_GA // d)), (proj, ("t", d, C_GB // d)), (ya, ("t", d, 0)), (yb, ("t", d, 0))],
                 [((t, d), BF16, "t")], name)[0]


def _mix_bwd(dmix, proj, ya, yb, name):
    t, d = ya.shape

    def body(dm_ref, ga_ref, gb_ref, ya_ref, yb_ref, dg_ref, dya_ref, dyb_ref):
        dm = dm_ref[...]
        sa, sb = _sig(ga_ref[...].astype(F32)), _sig(gb_ref[...].astype(F32))
        dg_ref[:, :d] = (dm * ya_ref[...].astype(F32) * sa * (1.0 - sa)).astype(BF16)
        dg_ref[:, d:] = (dm * yb_ref[...].astype(F32) * sb * (1.0 - sb)).astype(BF16)
        dya_ref[...] = (dm * sa).astype(BF16)
        dyb_ref[...] = (dm * sb).astype(BF16)

    return _rows(body, t, 128,
                 [(dmix, ("t", d, 0)), (proj, ("t", d, C_GA // d)), (proj, ("t", d, C_GB // d)), (ya, ("t", d, 0)), (yb, ("t", d, 0))],
                 [((t, 2 * d), BF16, "t"), ((t, d), BF16, "t"), ((t, d), BF16, "t")], name)


def _swiglu_call(body, ins, n_out, name):
    t, f = ins[0].shape
    tc = _pick(f, (1408, 512))
    tile = pl.BlockSpec((512, tc), lambda i, j: (i, j))
    return pl.pallas_call(
        body, name=name, grid=(t // 512, f // tc), in_specs=[tile] * len(ins), out_specs=[tile] * n_out,
        out_shape=[jax.ShapeDtypeStruct((t, f), BF16)] * n_out, compiler_params=_params(("parallel", "parallel")),
    )(*ins)


def _swiglu_fwd(fg, fu, name):
    def body(g_ref, u_ref, s_ref):
        gv = g_ref[...].astype(F32)
        s_ref[...] = (gv * _sig(gv) * u_ref[...].astype(F32)).astype(BF16)

    return _swiglu_call(body, [fg, fu], 1, name)[0]


def _swiglu_bwd(ds, fg, fu, name):
    def body(ds_ref, g_ref, u_ref, dg_ref, du_ref):
        dsv, gv, uv = ds_ref[...].astype(F32), g_ref[...].astype(F32), u_ref[...].astype(F32)
        sg = _sig(gv)
        dg_ref[...] = (dsv * uv * sg * (1.0 + gv * (1.0 - sg))).astype(BF16)
        du_ref[...] = (dsv * gv * sg).astype(BF16)

    return _swiglu_call(body, [ds, fg, fu], 2, name)


def _ple_final(x2, pg, pp, tgt, g_post, name):
    t, d = x2.shape

    def body(x_ref, pg_ref, pp_ref, t_ref, g_ref, loss_ref, d3_ref, dpg_ref, dpp_ref, dg_ref):
        sg, ppv, g = _sig(pg_ref[...]), pp_ref[...], g_ref[...]
        e = sg * ppv
        r = _rinv(e)
        eh = e * r
        diff = x_ref[...] + eh * g - t_ref[...]
        loss_ref[...] = 0.5 * jnp.mean(diff * diff, axis=-1, keepdims=True)
        d3 = diff * (1.0 / d)
        d3_ref[...] = d3
        gd = d3 * g
        de = r * (gd - eh * jnp.mean(gd * eh, axis=-1, keepdims=True))
        dpg_ref[...] = (de * ppv * sg * (1.0 - sg)).astype(BF16)
        dpp_ref[...] = (de * sg).astype(BF16)
        _acc(dg_ref, d3 * eh)

    return _rows(body, t, 128, [(x2, ("t", d, 0)), (pg, ("t", d, 0)), (pp, ("t", d, 0)), (tgt, ("t", d, 0)), (g_post, ("b",))],
                 [((t, 1), F32, "t"), ((t, d), F32, "t"), ((t, d), BF16, "t"), ((t, d), BF16, "t"), ((1, d), F32, "a")], name)


def _norm_bwd(dn, dh, x, g_pre, fm, g_post, name):
    t, d = x.shape
    two = fm is not None

    def body(*refs):
        if two:
            dn_ref, dh_ref, x_ref, gp_ref, f_ref, gq_ref, dx_ref, df_ref, dgp_ref, dgq_ref = refs
        else:
            dn_ref, dh_ref, x_ref, gp_ref, dx_ref, dgp_ref = refs
        xv, dhv = x_ref[...], dh_ref[...]
        r = _rinv(xv)
        xh = xv * r
        gd = dhv * gp_ref[...]
        dx = dn_ref[...] + r * (gd - xh * jnp.mean(gd * xh, axis=-1, keepdims=True))
        dx_ref[...] = dx
        _acc(dgp_ref, dhv * xh)
        if two:
            fv = f_ref[...]
            rf = _rinv(fv)
            fh = fv * rf
            gd2 = dx * gq_ref[...]
            df_ref[...] = (rf * (gd2 - fh * jnp.mean(gd2 * fh, axis=-1, keepdims=True))).astype(BF16)
            _acc(dgq_ref, dx * fh)

    ins = [(dn, ("t", d, 0)), (dh, ("t", d, 0)), (x, ("t", d, 0)), (g_pre, ("b",))]
    outs = [((t, d), F32, "t")]
    if two:
        ins += [(fm, ("t", d, 0)), (g_post, ("b",))]
        outs += [((t, d), BF16, "t"), ((1, d), F32, "a"), ((1, d), F32, "a")]
    else:
        outs += [((1, d), F32, "a")]
    return _rows(body, t, 128, ins, outs, name)


CONV_TC = 256


def _shift_down(v, s):
    rows = lax.broadcasted_iota(jnp.int32, v.shape, 0)
    return jnp.where(rows >= s, pltpu.roll(v, s, 0), 0.0)


def _shift_up(v, s):
    n = v.shape[0]
    rows = lax.broadcasted_iota(jnp.int32, v.shape, 0)
    return jnp.where(rows < n - s, pltpu.roll(v, n - s, 0), 0.0)


def _conv_specs(t):
    nb = 1024 // CONV_TC
    seg = lambda c0: pl.BlockSpec((t, CONV_TC), lambda j, cb=c0 // CONV_TC: (0, cb + j))
    own = pl.BlockSpec((t, CONV_TC), lambda j: (0, j))
    wspec = pl.BlockSpec((3, CONV_TC), lambda j: (0, j))
    return nb, seg, own, wspec


def _conv_fwd(proj, conv_w, name, after=()):
    t = proj.shape[0]
    nb, seg, own, wspec = _conv_specs(t)

    def body(ax_ref, ab_ref, ac_ref, w_ref, *rest):
        za_ref = rest[len(after)]
        u = ac_ref[...].astype(F32) * ax_ref[...].astype(F32)
        w = w_ref[...]
        yc = w[0:1] * _shift_down(u, 2) + w[1:2] * _shift_down(u, 1) + w[2:3] * u
        za_ref[...] = (ab_ref[...].astype(F32) * yc).astype(BF16)

    return pl.pallas_call(
        body, name=name, grid=(nb,), in_specs=[seg(C_AX), seg(C_AB), seg(C_AC), wspec] + [ANY] * len(after), out_specs=own,
        out_shape=jax.ShapeDtypeStruct((t, 1024), BF16), compiler_params=_params(("parallel",)),
    )(proj, proj, proj, conv_w, *after)


def _conv_bwd(dza, proj, conv_w, name):
    t = proj.shape[0]
    nb, seg, own, wspec = _conv_specs(t)

    def body(dz_ref, ax_ref, ab_ref, ac_ref, w_ref, dax_ref, dab_ref, dac_ref, dw_ref):
        ax, ab, ac, dz = ax_ref[...].astype(F32), ab_ref[...].astype(F32), ac_ref[...].astype(F32), dz_ref[...].astype(F32)
        w = w_ref[...]
        u = ac * ax
        u1, u2 = _shift_down(u, 1), _shift_down(u, 2)
        yc = w[0:1] * u2 + w[1:2] * u1 + w[2:3] * u
        dab_ref[...] = (dz * yc).astype(BF16)
        dyc = dz * ab
        du = w[2:3] * dyc + w[1:2] * _shift_up(dyc, 1) + w[0:1] * _shift_up(dyc, 2)
        dax_ref[...] = (du * ac).astype(BF16)
        dac_ref[...] = (du * ax).astype(BF16)
        dw_ref[0:1, :] = jnp.sum(dyc * u2, axis=0, keepdims=True)
        dw_ref[1:2, :] = jnp.sum(dyc * u1, axis=0, keepdims=True)
        dw_ref[2:3, :] = jnp.sum(dyc * u, axis=0, keepdims=True)

    act = jax.ShapeDtypeStruct((t, 1024), BF16)
    return pl.pallas_call(
        body, name=name, grid=(nb,), in_specs=[own, seg(C_AX), seg(C_AB), seg(C_AC), wspec], out_specs=[own, own, own, wspec],
        out_shape=[act, act, act, jax.ShapeDtypeStruct((3, 1024), F32)], compiler_params=_params(("parallel",)),
    )(dza, proj, proj, proj, conv_w)


def _dot(a, b, dims, precision=None):
    return lax.dot_general(a, b, (dims, ((), ())), precision=precision, preferred_element_type=F32)


def _gla_chunk(q, k, v, og, alr, s_in, wa, ba, gain):
    c = q.shape[0]
    hi = lax.Precision.HIGHEST
    z = _dot(alr, wa, ((1,), (0,))) + ba
    la = (jnp.minimum(z, 0.0) - jnp.log(1.0 + jnp.exp(-jnp.abs(z)))) * (1.0 / TAU)
    row = lax.broadcasted_iota(jnp.int32, (c, c), 0)
    col = lax.broadcasted_iota(jnp.int32, (c, c), 1)
    lower = row >= col
    b = _dot(lower.astype(F32), la, ((1,), (0,)), hi)
    trow = lax.broadcasted_iota(jnp.int32, la.shape, 0)
    mid = jnp.sum(jnp.where(trow <= c // 2, la, 0.0), axis=0, keepdims=True)
    blast = jnp.sum(la, axis=0, keepdims=True)
    qs = q * (DK ** -0.5)
    e_up, e_dn = jnp.exp(b - mid), jnp.exp(mid - b)
    a_fwd = _dot(qs * e_up, k * e_dn, ((1,), (1,)))
    a_rev = _dot(qs * e_dn, k * e_up, ((1,), (1,)))
    att = jnp.where(lower, a_fwd, a_rev)
    o = _dot(att, v, ((1,), (0,))) + _dot(qs * jnp.exp(b), s_in, ((1,), (0,)))
    upd = _dot(k * jnp.exp(blast - b), v, ((0,), (0,)))
    blast_col = _dot(la, jnp.ones((c, DV), F32), ((0,), (0,)), hi)
    s_out = jnp.exp(blast_col) * s_in + upd
    on = o * _rinv(o) * gain
    return on * og * _sig(og), s_out


def _gla_specs(t, rev):
    n = t // CHUNK
    ch = (lambda i: n - 1 - i) if rev else (lambda i: i)
    col = lambda w, c0: pl.BlockSpec((CHUNK, w), lambda i, h, cb=c0 // w: (ch(i), cb + h))
    specs = dict(
        q=col(DK, C_Q), k=col(DK, C_K), v=col(DV, C_V), og=col(DV, C_OG),
        alr=pl.BlockSpec((CHUNK, 128), lambda i, h: (ch(i), C_ALR // 128)),
        wa=pl.BlockSpec((128, DK), lambda i, h: (0, h)), ba=pl.BlockSpec((1, DK), lambda i, h: (0, h)),
        gain=pl.BlockSpec((1, DV), lambda i, h: (0, 0)),
        state=pl.BlockSpec((None, None, DK, DV), lambda i, h: (ch(i), h, 0, 0)),
        odk=pl.BlockSpec((CHUNK, DK), lambda i, h: (ch(i), h)), odv=pl.BlockSpec((CHUNK, DV), lambda i, h: (ch(i), h)),
        oalr=pl.BlockSpec((CHUNK, 128), lambda i, h: (ch(i), 0)),
    )
    return n, specs


def _gla_fwd(proj, wa, ba, gain, name):
    t = proj.shape[0]
    n, sp = _gla_specs(t, False)

    def body(q_ref, k_ref, v_ref, og_ref, alr_ref, wa_ref, ba_ref, g_ref, zb_ref, st_ref, s_scr):
        h = pl.program_id(1)

        @pl.when(pl.program_id(0) == 0)
        def _():
            s_scr[h] = jnp.zeros((DK, DV), F32)

        s_in = s_scr[h]
        st_ref[...] = s_in
        zb, s_out = _gla_chunk(q_ref[...].astype(F32), k_ref[...].astype(F32), v_ref[...].astype(F32), og_ref[...].astype(F32),
                               alr_ref[...].astype(F32), s_in, wa_ref[...].astype(F32), ba_ref[...], g_ref[...])
        zb_ref[...] = zb.astype(BF16)
        s_scr[h] = s_out

    return pl.pallas_call(
        body, name=name, grid=(n, HEADS),
        in_specs=[sp["q"], sp["k"], sp["v"], sp["og"], sp["alr"], sp["wa"], sp["ba"], sp["gain"]],
        out_specs=[sp["odv"], sp["state"]],
        out_shape=[jax.ShapeDtypeStruct((t, HEADS * DV), BF16), jax.ShapeDtypeStruct((n, HEADS, DK, DV), F32)],
        scratch_shapes=[pltpu.VMEM((HEADS, DK, DV), F32)],
        compiler_params=_params(("arbitrary", "arbitrary")),
    )(proj, proj, proj, proj, proj, wa, ba, gain)


def _gla_bwd(dzb, proj, states, wa, ba, gain, name):
    t = proj.shape[0]
    n, sp = _gla_specs(t, True)

    def body(dz_ref, q_ref, k_ref, v_ref, og_ref, alr_ref, st_ref, wa_ref, ba_ref, g_ref,
             dq_ref, dk_ref, dv_ref, dog_ref, dalr_ref, dwa_ref, dba_ref, dg_ref, ds_scr):
        i, h = pl.program_id(0), pl.program_id(1)

        @pl.when(i == 0)
        def _():
            ds_scr[h] = jnp.zeros((DK, DV), F32)
            dwa_ref[h] = jnp.zeros((128, DK), F32)
            dba_ref[h] = jnp.zeros((1, DK), F32)

        @pl.when((i == 0) & (h == 0))
        def _():
            dg_ref[...] = jnp.zeros_like(dg_ref)

        args = (q_ref[...].astype(F32), k_ref[...].astype(F32), v_ref[...].astype(F32), og_ref[...].astype(F32),
                alr_ref[...].astype(F32), st_ref[...], wa_ref[...].astype(F32), ba_ref[...], g_ref[...])
        _, vjp = jax.vjp(_gla_chunk, *args)
        dq, dk, dv, dog, dalr, ds_in, dwa, dba, dgain = vjp((dz_ref[...].astype(F32), ds_scr[h]))
        dq_ref[...] = dq.astype(BF16)
        dk_ref[...] = dk.astype(BF16)
        dv_ref[...] = dv.astype(BF16)
        dog_ref[...] = dog.astype(BF16)

        @pl.when(h == 0)
        def _():
            dalr_ref[...] = dalr

        @pl.when(h != 0)
        def _():
            dalr_ref[...] += dalr

        ds_scr[h] = ds_in
        dwa_ref[h] += dwa
        dba_ref[h] += dba
        dg_ref[...] += dgain

    whole = lambda shape: pl.BlockSpec(shape, lambda i, h, nd=len(shape): (0,) * nd)
    return pl.pallas_call(
        body, name=name, grid=(n, HEADS),
        in_specs=[sp["odv"], sp["q"], sp["k"], sp["v"], sp["og"], sp["alr"], sp["state"], sp["wa"], sp["ba"], sp["gain"]],
        out_specs=[sp["odk"], sp["odk"], sp["odv"], sp["odv"], sp["oalr"], whole((HEADS, 128, DK)), whole((HEADS, 1, DK)), whole((1, DV))],
        out_shape=[jax.ShapeDtypeStruct((t, HEADS * DK), BF16), jax.ShapeDtypeStruct((t, HEADS * DK), BF16),
                   jax.ShapeDtypeStruct((t, HEADS * DV), BF16), jax.ShapeDtypeStruct((t, HEADS * DV), BF16),
                   jax.ShapeDtypeStruct((t, 128), F32), jax.ShapeDtypeStruct((HEADS, 128, DK), F32),
                   jax.ShapeDtypeStruct((HEADS, 1, DK), F32), jax.ShapeDtypeStruct((1, DV), F32)],
        scratch_shapes=[pltpu.VMEM((HEADS, DK, DV), F32)],
        compiler_params=_params(("arbitrary", "arbitrary")),
    )(dzb, proj, proj, proj, proj, proj, states, wa, ba, gain)


def _local_step(x, p, tgt, gather_start, gather_finish, scatter_start, small):
    b_alpha, gain = small["b_alpha_up"], small["gla_head_gain"]
    gather_start(0, ())
    w = dict(gather_finish(0, ()))
    conv_w, w_alpha = w["conv_w"], w["w_alpha_up"]
    wa_p = jnp.zeros((128, HEADS * DK), BF16).at[:GATE_RANK].set(w_alpha.astype(BF16))

    t1 = gather_start(1, (w["in_t"],))
    h1 = _rms_fwd(x, small["g_pre_mix"], "rms_pre_mix")
    proj = _mm(h1, w["in_t"], "nt", BF16, "mm_proj", after=(t1,))
    t2 = gather_start(2, (proj,))
    za = _conv_fwd(proj, conv_w, "conv_fwd", after=(t2,))
    zb, states = _gla_fwd(proj, wa_p, b_alpha, gain, "gla_fwd")
    t3 = gather_start(3, (zb, za))
    w.update(gather_finish(1, (t3,)))
    ya = _mm(za, w["a_out"], "nn", BF16, "mm_ya", b3=True, tm=2048, tk=1024)
    yb = _mm(zb, w["b_out"], "nn", BF16, "mm_yb", b3=True, tm=2048, tk=1024)
    mix = _mix_fwd(proj, ya, yb, "mix_fwd")
    m2 = _mm(mix, w["mix"], "nn", F32, "mm_mix")
    t4 = gather_start(4, (m2,))
    x1, h2 = _post_pre(x, m2, small["g_post_mix"], small["g_pre_ffn"], "norm_mix_ffn")
    w.update(gather_finish(2, (h2, t4)))
    fg = _mm(h2, w["gate_t"], "nt", BF16, "mm_gate")
    fu = _mm(h2, w["up_t"], "nt", BF16, "mm_up")
    s = _swiglu_fwd(fg, fu, "swiglu_fwd")
    w.update(gather_finish(3, (s,)))
    f = _mm(s, w["down"], "nn", F32, "mm_down")
    x2, h3 = _post_pre(x1, f, small["g_post_ffn"], small["g_pre_ple"], "norm_ffn_ple")
    w.update(gather_finish(4, (h3,)))
    pg = _mm(h3, w["pg"], "nn", F32, "mm_pg")
    p_bf = p.astype(BF16)
    pp = _mm(p_bf, w["pp"], "nn", F32, "mm_pp", b3=True, tm=2048)
    loss_rows, d3, dpg, dpp, dg_post_ple = _ple_final(x2, pg, pp, tgt, small["g_post_ple"], "ple_final")

    gw = {}
    gw["pp"] = _mm(p_bf, dpp, "tn", BF16, "mm_dw_pp", out3=True)
    gw["pg"] = _mm(h3, dpg, "tn", BF16, "mm_dw_pg")
    dh3 = _mm(dpg, w["pg"], "nt", F32, "mm_dh3", after=(scatter_start(0, gw),))
    d2, df, dg_pre_ple, dg_post_ffn = _norm_bwd(d3, dh3, x2, small["g_pre_ple"], f, small["g_post_ffn"], "norm_bwd_ple_ffn")
    ds = _mm(df, w["down"], "nt", BF16, "mm_ds")
    gw["down"] = _mm(s, df, "tn", BF16, "mm_dw_down", tm=1408)
    dfg, dfu = _swiglu_bwd(ds, fg, fu, "swiglu_bwd")
    gw["gate_t"] = _mm(dfg, h2, "tn", BF16, "mm_dw_gate", after=(gw["down"],), tm=1408)
    gw["up_t"] = _mm(dfu, h2, "tn", BF16, "mm_dw_up", after=(gw["gate_t"],), tm=1408)
    dh2 = _mm(dfg, w["gate_t"], "nn", F32, "mm_dh2_gate", after=(scatter_start(1, gw),))
    dh2 = _mm(dfu, w["up_t"], "nn", F32, "mm_dh2_up", add=dh2, tm=1024)
    d1, dm2, dg_pre_ffn, dg_post_mix = _norm_bwd(d2, dh2, x1, small["g_pre_ffn"], m2, small["g_post_mix"], "norm_bwd_ffn_mix")
    dmix = _mm(dm2, w["mix"], "nt", F32, "mm_dmix")
    gw["mix"] = _mm(mix, dm2, "tn", BF16, "mm_dw_mix")
    dgab, dya, dyb = _mix_bwd(dmix, proj, ya, yb, "mix_bwd")
    dza = _mm(dya, w["a_out"], "nt", BF16, "mm_dza", after=(gw["mix"],), b3=True, tm=2048)
    gw["a_out"] = _mm(za, dya, "tn", BF16, "mm_dw_a_out", out3=True, tk=1024)
    gw["b_out"] = _mm(zb, dyb, "tn", BF16, "mm_dw_b_out", after=(gw["a_out"],), out3=True, tk=1024)
    dzb = _mm(dyb, w["b_out"], "nt", BF16, "mm_dzb", after=(scatter_start(2, gw),), b3=True, tm=2048)
    dax, dab, dac, dconv = _conv_bwd(dza, proj, conv_w, "conv_bwd")
    dq, dk, dv, dog, dalr, dwa, dba, dgain = _gla_bwd(dzb, proj, states, wa_p, b_alpha, gain, "gla_bwd")
    dproj = jnp.concatenate([dgab, dax, dab, dac, dq, dk, dv, dog, dalr.astype(BF16)], axis=1)
    gw["in_t"] = _mm(dproj, h1, "tn", BF16, "mm_dw_in", tm=1152)
    dh1 = _mm(dproj, w["in_t"], "nn", F32, "mm_dh1", after=(scatter_start(3, gw),))
    grad_x, dg_pre_mix = _norm_bwd(d1, dh1, x, small["g_pre_mix"], None, None, "norm_bwd_mix")

    gs = dict(
        conv_w=dconv,
        w_alpha_up=jnp.transpose(dwa[:, :GATE_RANK, :], (1, 0, 2)).reshape(GATE_RANK, HEADS * DK),
        b_alpha_up=dba.reshape(1, HEADS * DK), gla_head_gain=dgain,
        g_pre_mix=dg_pre_mix, g_post_mix=dg_post_mix, g_pre_ffn=dg_pre_ffn, g_post_ffn=dg_post_ffn,
        g_pre_ple=dg_pre_ple, g_post_ple=dg_post_ple,
    )
    return loss_rows, grad_x, gs


def _place():
    x, y, c = lax.axis_index("x"), lax.axis_index("y"), lax.axis_index("c")
    return x, y, c, [(1 - x, y), (x, 1 - y), (1 - x, 1 - y)]


def _all_gather(shards, name, cid=None):
    n = len(shards)

    def body(*refs):
        ins, outs = refs[:n], refs[n:2 * n]
        send_sems, recv_sems, local_sems = refs[2 * n:]
        x, y, c, chips = _place()
        me, sibling = (x, y, c), (x, y, 1 - c)

        def slot(px, py, pc):
            return 4 * px + 2 * py + pc

        def copy(a, k, block, to, src=None):
            dst = outs[a].at[slot(*block)]
            return pltpu.make_async_remote_copy(src_ref=dst if src is None else src, dst_ref=dst, send_sem=send_sems.at[a, k],
                                                recv_sem=recv_sems.at[a, k], device_id=to, device_id_type=MESH)

        mine = [pltpu.make_async_copy(ins[a], outs[a].at[slot(*me)], local_sems.at[a]) for a in range(n)]
        for cp in mine:
            cp.start()
        first = []
        for j, chip in enumerate(chips):
            first += [copy(a, 1 + j, me, (*chip, c), src=ins[a]) for a in range(n)]
        first += [copy(a, 0, me, sibling, src=ins[a]) for a in range(n)]
        for cp in first:
            cp.start()
        passed = []
        for j, chip in enumerate(chips):
            for a in range(n):
                copy(a, 1 + j, (*chip, c), me).wait_recv()
                cp = copy(a, 4 + j, (*chip, c), sibling)
                cp.start()
                passed.append(cp)
        for a in range(n):
            copy(a, 0, sibling, me).wait_recv()
        for j, chip in enumerate(chips):
            for a in range(n):
                copy(a, 4 + j, (*chip, 1 - c), me).wait_recv()
        for cp in first + passed:
            cp.wait_send()
        for cp in mine:
            cp.wait()

    if cid is None:
        return pl.pallas_call(
            body, name=name, in_specs=[ANY] * n, out_specs=[ANY] * n,
            out_shape=[jax.ShapeDtypeStruct((N_DEV,) + s.shape, s.dtype) for s in shards],
            scratch_shapes=[pltpu.SemaphoreType.DMA((n, 7)), pltpu.SemaphoreType.DMA((n, 7)), pltpu.SemaphoreType.DMA((n,))],
        )(*shards)

    src = [jax.new_ref(s, memory_space=pltpu.MemorySpace.HBM) for s in shards]
    dst = [jax.empty_ref(jax.ShapeDtypeStruct((N_DEV,) + s.shape, s.dtype), memory_space=pltpu.MemorySpace.HBM) for s in shards]

    @pl.kernel(mesh=plsc.ScalarSubcoreMesh(axis_name="seq", num_cores=1), name=name,
               scratch_types=(pltpu.SemaphoreType.DMA((n, 7)), pltpu.SemaphoreType.DMA((n, 7)), pltpu.SemaphoreType.DMA((n,))),
               compiler_params=pltpu.CompilerParams(collective_id=cid))
    def launch(send_sems, recv_sems, local_sems):
        x, y, c, chips = _place()
        barrier = pltpu.get_barrier_semaphore()
        for peer in [(x, y, 1 - c)] + [(*chip, c) for chip in chips]:
            pl.semaphore_signal(barrier, inc=1, device_id=peer, device_id_type=MESH)
        pl.semaphore_wait(barrier, 4)
        body(*src, *dst, send_sems, recv_sems, local_sems)

    launch()
    return [r[...] for r in dst]


def _reduce_scatter(parts, name, cid):
    n = len(parts)
    src = [jax.new_ref(s, memory_space=pltpu.MemorySpace.HBM) for s in parts]
    dst = [jax.empty_ref(jax.ShapeDtypeStruct(s.shape, s.dtype), memory_space=pltpu.MemorySpace.HBM) for s in parts]

    @pl.kernel(mesh=plsc.ScalarSubcoreMesh(axis_name="seq", num_cores=1), name=name,
               scratch_types=(pltpu.SemaphoreType.DMA((n, N_DEV - 1)), pltpu.SemaphoreType.DMA((n, N_DEV - 1)), pltpu.SemaphoreType.DMA((n,))),
               compiler_params=pltpu.CompilerParams(collective_id=cid))
    def launch(send_sems, recv_sems, local_sems):
        x, y, c, _ = _place()
        me = 4 * x + 2 * y + c
        peers = [(1 - x if k & 4 else x, 1 - y if k & 2 else y, 1 - c if k & 1 else c) for k in range(1, N_DEV)]
        barrier = pltpu.get_barrier_semaphore()
        for peer in peers:
            pl.semaphore_signal(barrier, inc=1, device_id=peer, device_id_type=MESH)
        pl.semaphore_wait(barrier, N_DEV - 1)
        mine = [pltpu.make_async_copy(src[a].at[me], dst[a].at[me], local_sems.at[a]) for a in range(n)]
        for cp in mine:
            cp.start()
        cps = []
        for a in range(n):
            for k, (px, py, pc) in enumerate(peers):
                cps.append(pltpu.make_async_remote_copy(src_ref=src[a].at[4 * px + 2 * py + pc], dst_ref=dst[a].at[me], send_sem=send_sems.at[a, k],
                                                        recv_sem=recv_sems.at[a, k], device_id=(px, py, pc), device_id_type=MESH))
        for cp in cps:
            cp.start()
        for cp in cps:
            cp.wait_recv()
        for cp in cps:
            cp.wait_send()
        for cp in mine:
            cp.wait()

    launch()
    return [r[...] for r in dst]


def _sibling_exchange(parts, name):
    n = len(parts)
    pieces = [_row_pieces(s.shape[1]) for s in parts]

    def body(*refs):
        ins, outs = refs[:n], refs[n:2 * n]
        send_sems, recv_sems = refs[2 * n:]
        x, y, c, _ = _place()

        def copy(a, ch, q, rows):
            return pltpu.make_async_remote_copy(src_ref=ins[a].at[2 * ch + 1 - c, rows], dst_ref=outs[a].at[ch, rows], send_sem=send_sems.at[a, ch, q],
                                                recv_sem=recv_sems.at[a, ch, q], device_id=(x, y, 1 - c), device_id_type=MESH)

        cps = [copy(a, ch, q, rows) for ch in range(4) for a in range(n) for q, rows in enumerate(pieces[a])]
        for cp in cps:
            cp.start()
        for cp in cps:
            cp.wait_recv()
        for cp in cps:
            cp.wait_send()

    return pl.pallas_call(
        body, name=name, in_specs=[ANY] * n, out_specs=[ANY] * n,
        out_shape=[jax.ShapeDtypeStruct((4,) + s.shape[1:], s.dtype) for s in parts],
        scratch_shapes=[pltpu.SemaphoreType.DMA((n, 4, PIECES)), pltpu.SemaphoreType.DMA((n, 4, PIECES))],
    )(*parts)


def _chip_exchange(parts, name):
    n = len(parts)

    def body(*refs):
        ins, outs = refs[:n], refs[n:2 * n]
        send_sems, recv_sems, local_sems = refs[2 * n:]
        x, y, c, chips = _place()
        my_chip = 2 * x + y

        def copy(a, j):
            px, py = chips[j]
            return pltpu.make_async_remote_copy(src_ref=ins[a].at[2 * px + py], dst_ref=outs[a].at[my_chip], send_sem=send_sems.at[a, j],
                                                recv_sem=recv_sems.at[a, j], device_id=(px, py, c), device_id_type=MESH)

        def landing(a, j):
            px, py = chips[j]
            return pltpu.make_async_remote_copy(src_ref=ins[a].at[my_chip], dst_ref=outs[a].at[2 * px + py], send_sem=send_sems.at[a, j],
                                                recv_sem=recv_sems.at[a, j], device_id=(px, py, c), device_id_type=MESH)

        mine = [pltpu.make_async_copy(ins[a].at[my_chip], outs[a].at[my_chip], local_sems.at[a]) for a in range(n)]
        for cp in mine:
            cp.start()
        cps = [copy(a, j) for j in range(3) for a in range(n)]
        for cp in cps:
            cp.start()
        for j in range(3):
            for a in range(n):
                landing(a, j).wait_recv()
        for cp in cps:
            cp.wait_send()
        for cp in mine:
            cp.wait()

    return pl.pallas_call(
        body, name=name, in_specs=[ANY] * n, out_specs=[ANY] * n,
        out_shape=[jax.ShapeDtypeStruct(s.shape, s.dtype) for s in parts],
        scratch_shapes=[pltpu.SemaphoreType.DMA((n, 3)), pltpu.SemaphoreType.DMA((n, 3)), pltpu.SemaphoreType.DMA((n,))],
    )(*parts)


def _pair_add(mine8, got4, name):
    _, r, cols = mine8.shape
    steps, blk, at = _tiles(r, cols)
    core = lax.axis_index("c").astype(jnp.int32).reshape(1)

    def body(c_ref, a_ref, b_ref, o_ref):
        o_ref[...] = (a_ref[...].astype(F32) + b_ref[...].astype(F32)).astype(BF16)

    return pl.pallas_call(
        body, name=name,
        grid_spec=pltpu.PrefetchScalarGridSpec(
            num_scalar_prefetch=1, grid=(4, steps),
            in_specs=[pl.BlockSpec((None,) + blk, lambda ch, i, c_ref: (2 * ch + c_ref[0],) + at(i)),
                      pl.BlockSpec((None,) + blk, lambda ch, i, c_ref: (ch,) + at(i))],
            out_specs=pl.BlockSpec((None,) + blk, lambda ch, i, c_ref: (ch,) + at(i))),
        out_shape=jax.ShapeDtypeStruct((4, r, cols), BF16),
        compiler_params=_params(("parallel", "parallel")),
    )(core, mine8, got4)


HBM = pl.BlockSpec(memory_space=pltpu.HBM)
SEM = pl.BlockSpec(memory_space=pltpu.SEMAPHORE)
EFFECT = pltpu.SideEffectType.DATAFLOW_SIDE_EFFECTING


def _in_hbm(a):
    return pltpu.with_memory_space_constraint(a, pltpu.HBM)


def _remote_copies(plan, srcs, lands, send_sems, recv_sems):
    return [pltpu.make_async_remote_copy(src_ref=s, dst_ref=d, send_sem=send_sems.at[i], recv_sem=recv_sems.at[i], device_id=peer,
                                         device_id_type=MESH) for i, (s, d, peer) in enumerate(plan(srcs, lands))]


def _copies_start(plan, n_copies, srcs, land_shapes, name, after=()):
    ns, nl = len(srcs), len(land_shapes)

    def body(*refs):
        send_sems, recv_sems = refs[ns + nl + len(after):ns + nl + len(after) + 2]
        for cp in _remote_copies(plan, refs[:ns], refs[ns:ns + nl], send_sems, recv_sems):
            cp.start()
        refs[-1][...] = jnp.zeros((8, 128), F32)

    sems = pltpu.SemaphoreType.DMA((n_copies,))
    return pl.pallas_call(
        body, name=name,
        out_shape=(sems, sems, *[pltpu.HBM(s.shape, s.dtype) for s in srcs], *[pltpu.HBM(s.shape, s.dtype) for s in land_shapes],
                   jax.ShapeDtypeStruct((8, 128), F32)),
        in_specs=[HBM] * (ns + nl) + [ANY] * len(after),
        out_specs=(SEM, SEM, *[HBM] * (ns + nl), pl.BlockSpec(memory_space=pltpu.VMEM)),
        input_output_aliases={i: 2 + i for i in range(ns + nl)},
        compiler_params=pltpu.CompilerParams(has_side_effects=EFFECT),
    )(*[_in_hbm(s) for s in srcs], *[_in_hbm(lax.empty(s.shape, s.dtype)) for s in land_shapes], *after)


def _copies_wait(plan, state, ns, name, after=()):
    send_sems, recv_sems, *arrs = state[:-1]
    n = len(arrs)

    def body(*refs):
        cps = _remote_copies(plan, refs[:ns], refs[ns:n], refs[n], refs[n + 1])
        for cp in cps:
            cp.wait_send()
        for cp in cps:
            cp.wait_recv()

    out = pl.pallas_call(
        body, name=name, out_shape=tuple(pltpu.HBM(a.shape, a.dtype) for a in arrs),
        in_specs=[HBM] * n + [SEM, SEM] + [ANY] * len(after), out_specs=tuple([HBM] * n),
        input_output_aliases={i: i for i in range(n)},
        compiler_params=pltpu.CompilerParams(has_side_effects=EFFECT),
    )(*arrs, send_sems, recv_sems, *after)
    return list(out[:ns]), list(out[ns:])


def _gather_plan(srcs, lands):
    x, y, c, chips = _place()
    peers = [(x, y, 1 - c)] + [(*chip, c) for chip in chips]
    return [(s, l.at[4 * x + 2 * y + c], peer) for s, l in zip(srcs, lands) for peer in peers]


def _scatter_plan(srcs, lands):
    x, y, c, _ = _place()
    peers = [(1 - x if k & 4 else x, 1 - y if k & 2 else y, 1 - c if k & 1 else c) for k in range(1, N_DEV)]
    return [(s.at[4 * px + 2 * py + pc], l.at[4 * x + 2 * y + c], (px, py, pc)) for s, l in zip(srcs, lands) for px, py, pc in peers]


def _everyone_plan(srcs, lands):
    x, y, c, _ = _place()
    peers = [(1 - x if k & 4 else x, 1 - y if k & 2 else y, 1 - c if k & 1 else c) for k in range(1, N_DEV)]
    return [(s, l.at[4 * x + 2 * y + c], peer) for s, l in zip(srcs, lands) for peer in peers]


def _sum_parts(got, own, me, name):
    def body(me_ref, got_ref, own_ref, o_ref):
        acc = jnp.where(me_ref[0] == 0, own_ref[...], got_ref[0])
        for d in range(1, N_DEV):
            acc = acc + jnp.where(me_ref[0] == d, own_ref[...], got_ref[d])
        o_ref[...] = acc

    return pl.pallas_call(
        body, name=name,
        grid_spec=pltpu.PrefetchScalarGridSpec(
            num_scalar_prefetch=1, grid=(1,),
            in_specs=[pl.BlockSpec(got.shape, lambda i, me_ref: (0, 0, 0)), pl.BlockSpec(own.shape, lambda i, me_ref: (0, 0))],
            out_specs=pl.BlockSpec(own.shape, lambda i, me_ref: (0, 0))),
        out_shape=jax.ShapeDtypeStruct(own.shape, F32),
    )(me.astype(jnp.int32).reshape(1), got, own)


def _chip_plan(srcs, lands):
    x, y, c, chips = _place()
    return [(s.at[2 * px + py], l.at[2 * x + y], (px, py, c)) for s, l in zip(srcs, lands) for px, py in chips]


PIECES = 8


def _row_pieces(rows):
    for k in (PIECES, 4, 2):
        if rows % (16 * k) == 0:
            return [pl.ds(q * (rows // k), rows // k) for q in range(k)]
    return [pl.ds(0, rows)]


def _put_own(shard, zone, me, name):
    r, c = shard.shape
    tr = r if r <= 256 else _pick(r, (256, 64))

    def body(me_ref, s_ref, z_ref, o_ref):
        o_ref[...] = s_ref[...]

    return pl.pallas_call(
        body, name=name,
        grid_spec=pltpu.PrefetchScalarGridSpec(
            num_scalar_prefetch=1, grid=(r // tr,),
            in_specs=[pl.BlockSpec((tr, c), lambda i, me_ref: (i, 0)), ANY],
            out_specs=pl.BlockSpec((None, tr, c), lambda i, me_ref: (me_ref[0], i, 0))),
        out_shape=jax.ShapeDtypeStruct(zone.shape, zone.dtype), input_output_aliases={2: 0},
        compiler_params=_params(("arbitrary",)),
    )(me.astype(jnp.int32).reshape(1), shard, zone)


def _gather_finish(lands, name):
    n = len(lands)
    pieces = [_row_pieces(s.shape[1]) for s in lands]

    def body(*refs):
        zones, outs = refs[:n], refs[n:2 * n]
        send_sems, recv_sems = refs[2 * n:]
        x, y, c, chips = _place()
        cps = []
        for j, (px, py) in enumerate(chips):
            for a in range(n):
                for q, rows in enumerate(pieces[a]):
                    cps.append(pltpu.make_async_remote_copy(
                        src_ref=zones[a].at[4 * px + 2 * py + c, rows], dst_ref=outs[a].at[4 * px + 2 * py + c, rows],
                        send_sem=send_sems.at[a, j, q], recv_sem=recv_sems.at[a, j, q], device_id=(x, y, 1 - c), device_id_type=MESH))
        for cp in cps:
            cp.start()
        for cp in cps:
            cp.wait_recv()
        for cp in cps:
            cp.wait_send()

    return pl.pallas_call(
        body, name=name, in_specs=[ANY] * n, out_specs=[ANY] * n,
        out_shape=[jax.ShapeDtypeStruct(l.shape, l.dtype) for l in lands],
        input_output_aliases={a: a for a in range(n)},
        scratch_shapes=[pltpu.SemaphoreType.DMA((n, 3, PIECES)), pltpu.SemaphoreType.DMA((n, 3, PIECES))],
    )(*lands)


def _sum_everywhere(v, name):
    rows = v.shape[0]

    def body(v_ref, o_ref, buf, send_sems, recv_sems):
        x, y, c, _ = _place()
        me = 4 * x + 2 * y + c
        buf[me] = v_ref[...]
        cps = []
        for k in range(1, N_DEV):
            fx, fy, fc = (k >> 2) & 1, (k >> 1) & 1, k & 1
            to = (1 - x if fx else x, 1 - y if fy else y, 1 - c if fc else c)
            cps.append(pltpu.make_async_remote_copy(src_ref=buf.at[me], dst_ref=buf.at[me], send_sem=send_sems.at[k - 1],
                                                    recv_sem=recv_sems.at[k - 1], device_id=to, device_id_type=MESH))
        for cp in cps:
            cp.start()
        for cp in cps:
            cp.wait_recv()
        for cp in cps:
            cp.wait_send()
        acc = buf[0]
        for d in range(1, N_DEV):
            acc = acc + buf[d]
        o_ref[...] = acc

    vm = pl.BlockSpec(memory_space=pltpu.VMEM)
    return pl.pallas_call(
        body, name=name, in_specs=[vm], out_specs=vm, out_shape=jax.ShapeDtypeStruct(v.shape, F32),
        scratch_shapes=[pltpu.VMEM((N_DEV, rows, 128), F32), pltpu.SemaphoreType.DMA((N_DEV - 1,)), pltpu.SemaphoreType.DMA((N_DEV - 1,))],
    )(v)


def _adamw_parts(w, got, mine, me, m, v, name, after=()):
    r, c = w.shape
    n_parts = got.shape[0]
    steps, blk, at = _tiles(r, c)

    def body(me_ref, w_ref, got_ref, own_ref, m_ref, v_ref, *rest):
        go_ref, d_ref, mo_ref, vo_ref = rest[len(after):]
        own = own_ref[...].astype(F32)
        gv = jnp.where(me_ref[0] == 0, own, got_ref[0].astype(F32))
        for d in range(1, n_parts):
            gv = gv + jnp.where(me_ref[0] == d, own, got_ref[d].astype(F32))
        _adamw_math(gv, w_ref, m_ref, v_ref, go_ref, d_ref, mo_ref, vo_ref)

    tile = pl.BlockSpec(blk, lambda i, me_ref: at(i))
    out = jax.ShapeDtypeStruct((r, c), F32)
    return pl.pallas_call(
        body, name=name,
        grid_spec=pltpu.PrefetchScalarGridSpec(
            num_scalar_prefetch=1, grid=(steps,),
            in_specs=[tile, pl.BlockSpec((n_parts,) + blk, lambda i, me_ref: (0,) + at(i)),
                      pl.BlockSpec((None,) + blk, lambda i, me_ref: (me_ref[0],) + at(i)), tile, tile] + [ANY] * len(after),
            out_specs=[tile] * 4),
        out_shape=[out] * 4, compiler_params=_params(("parallel",)),
    )(me.astype(jnp.int32).reshape(1), w, got, mine, m, v, *after)


def _adamw_math(gv, w_ref, m_ref, v_ref, go_ref, d_ref, mo_ref, vo_ref):
    mn = B1 * m_ref[...] + (1.0 - B1) * gv
    vn = B2 * v_ref[...] + (1.0 - B2) * (gv * gv)
    m_hat = mn / (1.0 - B1 ** STEP)
    v_hat = vn / (1.0 - B2 ** STEP)
    go_ref[...] = gv
    d_ref[...] = -LR * (m_hat / (jnp.sqrt(v_hat) + ADAM_EPS) + WD * w_ref[...])
    mo_ref[...] = mn
    vo_ref[...] = vn


def _adamw(w, g, m, v, name):
    r, c = w.shape
    parts = g.ndim == 3
    tr = r if r <= 128 else _pick(r, (128, 64))

    def body(w_ref, g_ref, m_ref, v_ref, go_ref, d_ref, mo_ref, vo_ref):
        if parts:
            gv = g_ref[0].astype(F32)
            for d in range(1, g.shape[0]):
                gv = gv + g_ref[d].astype(F32)
        else:
            gv = g_ref[...]
        mn = B1 * m_ref[...] + (1.0 - B1) * gv
        vn = B2 * v_ref[...] + (1.0 - B2) * (gv * gv)
        m_hat = mn / (1.0 - B1 ** STEP)
        v_hat = vn / (1.0 - B2 ** STEP)
        go_ref[...] = gv
        d_ref[...] = -LR * (m_hat / (jnp.sqrt(v_hat) + ADAM_EPS) + WD * w_ref[...])
        mo_ref[...] = mn
        vo_ref[...] = vn

    tile = pl.BlockSpec((tr, c), lambda i: (i, 0))
    g_spec = pl.BlockSpec((g.shape[0], tr, c), lambda i: (0, i, 0)) if parts else tile
    out = jax.ShapeDtypeStruct((r, c), F32)
    return pl.pallas_call(
        body, name=name, grid=(r // tr,), in_specs=[tile, g_spec, tile, tile], out_specs=[tile] * 4, out_shape=[out] * 4,
        compiler_params=_params(("parallel",)),
    )(w, g, m, v)


BIG = ["w_in", "w_a_out", "w_b_out", "w_mix_out", "w_ff_gate", "w_ff_up", "w_ff_down", "w_ple_gate", "w_ple_proj"]
TRANSPOSED = ["w_in", "w_ff_gate", "w_ff_up"]
SMALL = ["conv_w", "w_alpha_up", "b_alpha_up", "gla_head_gain", "g_pre_mix", "g_post_mix", "g_pre_ffn", "g_post_ffn", "g_pre_ple", "g_post_ple"]
WEIGHTS = ["w_in", "conv_w", "w_a_out", "w_alpha_up", "b_alpha_up", "gla_head_gain", "w_b_out", "w_mix_out", "g_pre_mix", "g_post_mix",
           "g_pre_ffn", "g_post_ffn", "w_ff_gate", "w_ff_up", "w_ff_down", "g_pre_ple", "g_post_ple", "w_ple_gate", "w_ple_proj"]


def _natural_rows(lo, hi, blocks):
    per, out = R_END // N_DEV, []
    for b in range(N_DEV):
        a, e = max(lo, b * per), min(hi, (b + 1) * per)
        if a < e:
            out.append(blocks[b] if e - a == per else blocks[b, a - b * per:e - b * per])
    return out


def _in_t_from_blocks(z):
    pad = jnp.zeros((128 - GATE_RANK, z.shape[-1]), z.dtype)
    return jnp.concatenate(_natural_rows(R_GA, R_END, z) + _natural_rows(0, R_ALR, z) + _natural_rows(R_ALR, R_GA, z) + [pad], axis=0)


def _blocks_from_in_t(g):
    per = R_END // N_DEV

    def rows(lo, hi):
        out = []
        for n0, n1, p0 in ((0, R_ALR, C_AX), (R_ALR, R_GA, C_ALR), (R_GA, R_END, 0)):
            a, e = max(lo, n0), min(hi, n1)
            if a < e:
                out.append(g[p0 + a - n0:p0 + e - n0])
        return out

    return jnp.stack([jnp.concatenate(rows(b * per, (b + 1) * per), axis=0) for b in range(N_DEV)])


def _cols_to_full(g8):
    n, r, c = g8.shape
    return jnp.transpose(g8, (1, 0, 2)).reshape(r, n * c)


def _full_to_cols(a):
    r, c = a.shape
    return jnp.transpose(a.reshape(r, N_DEV, c // N_DEV), (1, 0, 2))


def _pack(arrs, rows):
    flat = jnp.concatenate([a.reshape(-1) for a in arrs])
    return jnp.pad(flat, (0, rows * 128 - flat.shape[0])).reshape(rows, 128)


def _unpack(packed, shapes):
    flat, out, o = packed.reshape(-1), [], 0
    for s in shapes:
        size = 1
        for d in s:
            size *= d
        out.append(flat[o:o + size].reshape(s))
        o += size
    return out


def kernel(x, p, w_in, conv_w, w_a_out, w_alpha_up, b_alpha_up, gla_head_gain, w_b_out, w_mix_out, g_pre_mix, g_post_mix, g_pre_ffn, g_post_ffn, w_ff_gate, w_ff_up, w_ff_down, g_pre_ple, g_post_ple, w_ple_gate, w_ple_proj, loss_target, m_w_in, m_conv_w, m_w_a_out, m_w_alpha_up, m_b_alpha_up, m_gla_head_gain, m_w_b_out, m_w_mix_out, m_g_pre_mix, m_g_post_mix, m_g_pre_ffn, m_g_post_ffn, m_w_ff_gate, m_w_ff_up, m_w_ff_down, m_g_pre_ple, m_g_post_ple, m_w_ple_gate, m_w_ple_proj, v_w_in, v_conv_w, v_w_a_out, v_w_alpha_up, v_b_alpha_up, v_gla_head_gain, v_w_b_out, v_w_mix_out, v_g_pre_mix, v_g_post_mix, v_g_pre_ffn, v_g_post_ffn, v_w_ff_gate, v_w_ff_up, v_w_ff_down, v_g_pre_ple, v_g_post_ple, v_w_ple_gate, v_w_ple_proj):
    args = dict(locals())
    shard = lambda n, a: jnp.transpose(a[0]) if n in TRANSPOSED else a[0]
    wts = {n: shard(n, args[n]) for n in WEIGHTS}
    mom = {n: shard(n, args["m_" + n]) for n in WEIGHTS}
    var = {n: shard(n, args["v_" + n]) for n in WEIGHTS}
    me =4 * lax.axis_index("x") + 2 * lax.axis_index("y") + lax.axis_index("c")

    groups = [["w_in", "conv_w", "w_alpha_up"], ["w_a_out", "w_b_out", "w_mix_out"], ["w_ff_gate", "w_ff_up"], ["w_ff_down"],
              ["w_ple_gate", "w_ple_proj"]]
    grad_groups = [["w_ple_proj", "w_ple_gate"], ["w_ff_down", "w_ff_gate", "w_ff_up"], ["w_mix_out", "w_a_out", "w_b_out"], ["w_in"]]
    rows_full = lambda g: g.reshape(-1, g.shape[-1])
    gathers, scatters = {}, {}

    def gather_start(gi, after):
        shards = [wts[n].astype(BF16) if n in BIG else wts[n] for n in groups[gi]]
        zones = [jax.ShapeDtypeStruct((N_DEV,) + s.shape, s.dtype) for s in shards]
        gathers[gi] = (shards, _copies_start(_gather_plan, 4 * len(shards), shards, zones, "gather_start_%d" % gi, after))
        return gathers[gi][1][-1]

    def gather_finish(gi, after):
        shards, state = gathers[gi]
        shards, zones = _copies_wait(_gather_plan, state, len(shards), "gather_wait_%d" % gi, after)
        zones = _gather_finish(zones, "gather_finish_%d" % gi)
        g8 = {n: _put_own(s, z, me, "gather_own_" + n) for n, s, z in zip(groups[gi], shards, zones)}
        if gi == 0:
            return dict(in_t=_in_t_from_blocks(g8["w_in"]),
                        conv_w=_cols_to_full(g8["conv_w"]), w_alpha_up=_cols_to_full(g8["w_alpha_up"]))
        if gi == 1:
            return dict(a_out=g8["w_a_out"], b_out=g8["w_b_out"], mix=rows_full(g8["w_mix_out"]))
        if gi == 2:
            return dict(gate_t=rows_full(g8["w_ff_gate"]), up_t=rows_full(g8["w_ff_up"]))
        if gi == 3:
            return dict(down=rows_full(g8["w_ff_down"]))
        return dict(pg=rows_full(g8["w_ple_gate"]), pp=g8["w_ple_proj"])

    def scatter_start(gi, gw):
        if gi == 3:
            full = dict(w_in=_blocks_from_in_t(gw["in_t"]))
        elif gi == 1:
            full = dict(w_ff_down=gw["down"], w_ff_gate=gw["gate_t"], w_ff_up=gw["up_t"])
        elif gi == 2:
            full = dict(w_mix_out=gw["mix"], w_a_out=gw["a_out"], w_b_out=gw["b_out"])
        else:
            full = dict(w_ple_proj=gw["pp"], w_ple_gate=gw["pg"])
        parts = [full[n] if full[n].ndim == 3 else full[n].reshape(N_DEV, -1, full[n].shape[-1]) for n in grad_groups[gi]]
        if gi == 3:
            from_sibling = _sibling_exchange(parts, "scatter_sibling_%d" % gi)
            parts = [_pair_add(a, b, "scatter_add_%d_%s" % (gi, n)) for n, a, b in zip(grad_groups[gi], parts, from_sibling)]
            scatters[gi] = _copies_start(_chip_plan, 3 * len(parts), parts, parts, "scatter_start_%d" % gi)
        else:
            scatters[gi] = _copies_start(_scatter_plan, (N_DEV - 1) * len(parts), parts, parts, "scatter_start_%d" % gi)
        return scatters[gi][-1]

    small = {n: wts[n].reshape(1, -1) for n in SMALL[2:]}

    loss_rows, grad_x, gs = _local_step(x[0], p[0, 0], loss_target[0], gather_start, gather_finish, scatter_start, small)
    gs["loss"] = jnp.sum(loss_rows).reshape(1, 1)

    small_shapes = [gs[n].shape for n in SMALL]
    gs_packed = _pack([gs[n] for n in SMALL + ["loss"]], 192)
    small_state = _copies_start(_everyone_plan, N_DEV - 1, [gs_packed], [jax.ShapeDtypeStruct((N_DEV,) + gs_packed.shape, F32)],
                                "small_start", (grad_x,))

    res, done = {}, (small_state[-1],)
    for gi, names in enumerate(grad_groups):
        plan, slot = (_chip_plan, me // 2) if gi == 3 else (_scatter_plan, me)
        mine, got = _copies_wait(plan, scatters[gi], len(names), "scatter_wait_%d" % gi, done)
        for n, g, own in zip(names, got, mine):
            res[n] = _adamw_parts(wts[n], g, own, slot, mom[n], var[n], "adamw_" + n)
        done = tuple(res[n][1] for n in names)

    (gs_own,), (gs_got,) = _copies_wait(_everyone_plan, small_state, 1, "small_wait", done)
    gsum = dict(zip(SMALL + ["loss"], _unpack(_sum_parts(gs_got, gs_own, me, "small_sum"), small_shapes + [(1, 1)])))
    loss = gsum["loss"].reshape(())
    gsum["conv_w"] = lax.dynamic_index_in_dim(gsum["conv_w"].reshape(3, N_DEV, -1), me, axis=1, keepdims=False)
    gsum["w_alpha_up"] = lax.dynamic_index_in_dim(gsum["w_alpha_up"].reshape(GATE_RANK, N_DEV, -1), me, axis=1, keepdims=False)

    shard_shapes = [wts[n].shape for n in SMALL]
    packed = [_pack([d[n] for n in SMALL], 120) for d in (wts, gsum, mom, var)]
    outs = [_unpack(o, shard_shapes) for o in _adamw(*packed, "adamw_small")]
    for i, n in enumerate(SMALL):
        res[n] = [o[i] for o in outs]

    back = lambda n, a: (jnp.transpose(a) if n in TRANSPOSED else a)[None]
    return (loss, grad_x[None], *[back(n, res[n][i]) for i in range(4) for n in WEIGHTS])
```

```python
import functools

import jax
import jax.numpy as jnp
from jax import lax
from jax.experimental import pallas as pl
from jax.experimental.pallas import tpu as pltpu
from jax.experimental.pallas import tpu_sc as plsc

F32, BF16 = jnp.float32, jnp.bfloat16
EPS = 1e-6
CHUNK = 64
HEADS, DK, DV = 4, 128, 256
GATE_RANK = 16
TAU = 16.0
LR, B1, B2, ADAM_EPS, WD, STEP = 0.001, 0.9, 0.999, 1e-08, 0.01, 10
N_DEV = 8
MESH = pl.DeviceIdType.MESH
VMEM_LIMIT = 56 * 1024 * 1024
ANY = pl.BlockSpec(memory_space=pl.ANY)

C_GA, C_GB, C_AX, C_AB, C_AC, C_Q, C_K, C_V, C_OG, C_ALR = 0, 2048, 4096, 5120, 6144, 7168, 7680, 8192, 9216, 10240
IN_PAD = 10368
R_AX, R_AB, R_AC, R_Q, R_K, R_V, R_OG, R_ALR, R_GA, R_GB, R_END = 0, 1024, 2048, 3072, 3584, 4096, 5120, 6144, 6160, 8208, 10256


def _params(sem):
    return pltpu.CompilerParams(dimension_semantics=sem, vmem_limit_bytes=VMEM_LIMIT)


def _pick(n, cands):
    for c in cands:
        if n % c == 0:
            return c
    return n


def _tiles(r, c):
    for tr in (128, 64):
        if r % tr == 0:
            return r // tr, (tr, c), lambda i: (i, 0)
    tc = _pick(c, (256, 128))
    return c // tc, (r, tc), lambda i: (0, i)


def _mm(a, b, mode, out_dtype, name, after=(), add=None, b3=False, out3=False, tm=None, tk=None):
    bshape = (b.shape[1], N_DEV * b.shape[2]) if b3 else b.shape
    if mode == "nn":
        (m, k), (k2, n) = a.shape, bshape
    elif mode == "nt":
        (m, k), (n, k2) = a.shape, bshape
    else:
        (k, m), (k2, n) = a.shape, bshape
    assert k == k2 and a.dtype == BF16 and b.dtype == BF16, (name, a.shape, b.shape, a.dtype, b.dtype)
    tm = tm if tm and m % tm == 0 else _pick(m, (2048, 1024, 512, 256))
    tn = _pick(n, (1152, 1024, 1408, 512, 256))
    tk = tk if tk and k % tk == 0 else _pick(k, (2048, 1408, 1152, 1024, 512, 256))
    if out3 or (b3 and mode == "nn"):
        tn = n // N_DEV
    if b3 and mode == "nt":
        tk = k // N_DEV
    nk = k // tk
    dims = {"nn": (((1,), (0,)), ((), ())), "nt": (((1,), (1,)), ((), ())), "tn": (((0,), (0,)), ((), ()))}[mode]
    n_extra = len(after) + (add is not None)

    def body(a_ref, b_ref, *rest):
        o_ref = rest[n_extra]
        prod = lax.dot_general(a_ref[...], b_ref[...], dims, preferred_element_type=F32)
        if nk == 1:
            o_ref[...] = (prod if add is None else prod + rest[0][...]).astype(o_ref.dtype)
            return
        acc_ref = rest[n_extra + 1]
        kk = pl.program_id(2)

        @pl.when(kk == 0)
        def _():
            acc_ref[...] = prod if add is None else prod + rest[0][...]

        @pl.when((kk > 0) & (kk < nk - 1))
        def _():
            acc_ref[...] += prod

        @pl.when(kk == nk - 1)
        def _():
            o_ref[...] = (acc_ref[...] + prod).astype(o_ref.dtype)

    a_spec = pl.BlockSpec((tk, tm), lambda i, j, kk: (kk, i)) if mode == "tn" else pl.BlockSpec((tm, tk), lambda i, j, kk: (i, kk))
    if b3:
        b_spec = (pl.BlockSpec((None, tn, tk), lambda i, j, kk: (kk, j, 0)) if mode == "nt"
                  else pl.BlockSpec((None, tk, tn), lambda i, j, kk: (j, kk, 0)))
    else:
        b_spec = pl.BlockSpec((tn, tk), lambda i, j, kk: (j, kk)) if mode == "nt" else pl.BlockSpec((tk, tn), lambda i, j, kk: (kk, j))
    tile = pl.BlockSpec((tm, tn), lambda i, j, kk: (i, j))
    out_spec = pl.BlockSpec((None, tm, tn), lambda i, j, kk: (j, i, 0)) if out3 else tile
    return pl.pallas_call(
        body, name=name, grid=(m // tm, n // tn, nk),
        in_specs=[a_spec, b_spec] + ([tile] if add is not None else []) + [ANY] * len(after), out_specs=out_spec,
        out_shape=jax.ShapeDtypeStruct((N_DEV, m, tn) if out3 else (m, n), out_dtype),
        scratch_shapes=[pltpu.VMEM((tm, tn), F32)] if nk > 1 else [],
        compiler_params=_params(("parallel", "parallel", "arbitrary")),
    )(a, b, *([add] if add is not None else []), *after)


def _rows(body, t, tr, ins, outs, name):
    in_specs = []
    for arr, sp in ins:
        if sp[0] == "t":
            in_specs.append(pl.BlockSpec((tr, sp[1]), lambda i, cb=sp[2]: (i, cb)))
        else:
            in_specs.append(pl.BlockSpec(arr.shape, lambda i, nd=arr.ndim: (0,) * nd))
    out_specs, out_shape = [], []
    for shape, dt, kind in outs:
        out_specs.append(pl.BlockSpec((tr, shape[1]), lambda i: (i, 0)) if kind == "t" else pl.BlockSpec(shape, lambda i: (0, 0)))
        out_shape.append(jax.ShapeDtypeStruct(shape, dt))
    return pl.pallas_call(
        body, name=name, grid=(t // tr,), in_specs=in_specs, out_specs=out_specs, out_shape=out_shape,
        compiler_params=_params(("arbitrary",)),
    )(*[arr for arr, _ in ins])


def _rinv(v):
    return lax.rsqrt(jnp.mean(v * v, axis=-1, keepdims=True) + EPS)


def _sig(v):
    return 1.0 / (1.0 + jnp.exp(-v))


def _acc(ref, val):
    @pl.when(pl.program_id(0) == 0)
    def _():
        ref[...] = jnp.zeros_like(ref)

    ref[...] += jnp.sum(val, axis=0, keepdims=True)


def _rms_fwd(x, g, name):
    t, d = x.shape

    def body(x_ref, g_ref, h_ref):
        xv = x_ref[...]
        h_ref[...] = (xv * _rinv(xv) * g_ref[...]).astype(BF16)

    return _rows(body, t, 256, [(x, ("t", d, 0)), (g, ("b",))], [((t, d), BF16, "t")], name)[0]


def _post_pre(x, m, g_post, g_pre, name):
    t, d = x.shape

    def body(x_ref, m_ref, gp_ref, gn_ref, xo_ref, h_ref):
        mv = m_ref[...]
        xn = x_ref[...] + mv * _rinv(mv) * gp_ref[...]
        xo_ref[...] = xn
        h_ref[...] = (xn * _rinv(xn) * gn_ref[...]).astype(BF16)

    return _rows(body, t, 128, [(x, ("t", d, 0)), (m, ("t", d, 0)), (g_post, ("b",)), (g_pre, ("b",))],
                 [((t, d), F32, "t"), ((t, d), BF16, "t")], name)


def _mix_fwd(proj, ya, yb, name):
    t, d = ya.shape

    def body(ga_ref, gb_ref, ya_ref, yb_ref, o_ref):
        o_ref[...] = (_sig(ga_ref[...].astype(F32)) * ya_ref[...].astype(F32)
                      + _sig(gb_ref[...].astype(F32)) * yb_ref[...].astype(F32)).astype(BF16)

    return _rows(body, t, 256, [(proj, ("t", d, C_GA // d)), (proj, ("t", d, C_GB // d)), (ya, ("t", d, 0)), (yb, ("t", d, 0))],
                 [((t, d), BF16, "t")], name)[0]


def _mix_bwd(dmix, proj, ya, yb, name):
    t, d = ya.shape

    def body(dm_ref, ga_ref, gb_ref, ya_ref, yb_ref, dg_ref, dya_ref, dyb_ref):
        dm = dm_ref[...]
        sa, sb = _sig(ga_ref[...].astype(F32)), _sig(gb_ref[...].astype(F32))
        dg_ref[:, :d] = (dm * ya_ref[...].astype(F32) * sa * (1.0 - sa)).astype(BF16)
        dg_ref[:, d:] = (dm * yb_ref[...].astype(F32) * sb * (1.0 - sb)).astype(BF16)
        dya_ref[...] = (dm * sa).astype(BF16)
        dyb_ref[...] = (dm * sb).astype(BF16)

    return _rows(body, t, 128,
                 [(dmix, ("t", d, 0)), (proj, ("t", d, C_GA // d)), (proj, ("t", d, C_GB // d)), (ya, ("t", d, 0)), (yb, ("t", d, 0))],
                 [((t, 2 * d), BF16, "t"), ((t, d), BF16, "t"), ((t, d), BF16, "t")], name)


def _swiglu_call(body, ins, n_out, name):
    t, f = ins[0].shape
    tc = _pick(f, (1408, 512))
    tile = pl.BlockSpec((512, tc), lambda i, j: (i, j))
    return pl.pallas_call(
        body, name=name, grid=(t // 512, f // tc), in_specs=[tile] * len(ins), out_specs=[tile] * n_out,
        out_shape=[jax.ShapeDtypeStruct((t, f), BF16)] * n_out, compiler_params=_params(("parallel", "parallel")),
    )(*ins)


def _swiglu_fwd(fg, fu, name):
    def body(g_ref, u_ref, s_ref):
        gv = g_ref[...].astype(F32)
        s_ref[...] = (gv * _sig(gv) * u_ref[...].astype(F32)).astype(BF16)

    return _swiglu_call(body, [fg, fu], 1, name)[0]


def _swiglu_bwd(ds, fg, fu, name):
    def body(ds_ref, g_ref, u_ref, dg_ref, du_ref):
        dsv, gv, uv = ds_ref[...].astype(F32), g_ref[...].astype(F32), u_ref[...].astype(F32)
        sg = _sig(gv)
        dg_ref[...] = (dsv * uv * sg * (1.0 + gv * (1.0 - sg))).astype(BF16)
        du_ref[...] = (dsv * gv * sg).astype(BF16)

    return _swiglu_call(body, [ds, fg, fu], 2, name)


def _ple_final(x2, pg, pp, tgt, g_post, name):
    t, d = x2.shape

    def body(x_ref, pg_ref, pp_ref, t_ref, g_ref, loss_ref, d3_ref, dpg_ref, dpp_ref, dg_ref):
        sg, ppv, g = _sig(pg_ref[...]), pp_ref[...], g_ref[...]
        e = sg * ppv
        r = _rinv(e)
        eh = e * r
        diff = x_ref[...] + eh * g - t_ref[...]
        loss_ref[...] = 0.5 * jnp.mean(diff * diff, axis=-1, keepdims=True)
        d3 = diff * (1.0 / d)
        d3_ref[...] = d3
        gd = d3 * g
        de = r * (gd - eh * jnp.mean(gd * eh, axis=-1, keepdims=True))
        dpg_ref[...] = (de * ppv * sg * (1.0 - sg)).astype(BF16)
        dpp_ref[...] = (de * sg).astype(BF16)
        _acc(dg_ref, d3 * eh)

    return _rows(body, t, 128, [(x2, ("t", d, 0)), (pg, ("t", d, 0)), (pp, ("t", d, 0)), (tgt, ("t", d, 0)), (g_post, ("b",))],
                 [((t, 1), F32, "t"), ((t, d), F32, "t"), ((t, d), BF16, "t"), ((t, d), BF16, "t"), ((1, d), F32, "a")], name)


def _norm_bwd(dn, dh, x, g_pre, fm, g_post, name):
    t, d = x.shape
    two = fm is not None

    def body(*refs):
        if two:
            dn_ref, dh_ref, x_ref, gp_ref, f_ref, gq_ref, dx_ref, df_ref, dgp_ref, dgq_ref = refs
        else:
            dn_ref, dh_ref, x_ref, gp_ref, dx_ref, dgp_ref = refs
        xv, dhv = x_ref[...], dh_ref[...]
        r = _rinv(xv)
        xh = xv * r
        gd = dhv * gp_ref[...]
        dx = dn_ref[...] + r * (gd - xh * jnp.mean(gd * xh, axis=-1, keepdims=True))
        dx_ref[...] = dx
        _acc(dgp_ref, dhv * xh)
        if two:
            fv = f_ref[...]
            rf = _rinv(fv)
            fh = fv * rf
            gd2 = dx * gq_ref[...]
            df_ref[...] = (rf * (gd2 - fh * jnp.mean(gd2 * fh, axis=-1, keepdims=True))).astype(BF16)
            _acc(dgq_ref, dx * fh)

    ins = [(dn, ("t", d, 0)), (dh, ("t", d, 0)), (x, ("t", d, 0)), (g_pre, ("b",))]
    outs = [((t, d), F32, "t")]
    if two:
        ins += [(fm, ("t", d, 0)), (g_post, ("b",))]
        outs += [((t, d), BF16, "t"), ((1, d), F32, "a"), ((1, d), F32, "a")]
    else:
        outs += [((1, d), F32, "a")]
    return _rows(body, t, 128, ins, outs, name)


CONV_TC = 256


def _shift_down(v, s):
    rows = lax.broadcasted_iota(jnp.int32, v.shape, 0)
    return jnp.where(rows >= s, pltpu.roll(v, s, 0), 0.0)


def _shift_up(v, s):
    n = v.shape[0]
    rows = lax.broadcasted_iota(jnp.int32, v.shape, 0)
    return jnp.where(rows < n - s, pltpu.roll(v, n - s, 0), 0.0)


def _conv_specs(t):
    nb = 1024 // CONV_TC
    seg = lambda c0: pl.BlockSpec((t, CONV_TC), lambda j, cb=c0 // CONV_TC: (0, cb + j))
    own = pl.BlockSpec((t, CONV_TC), lambda j: (0, j))
    wspec = pl.BlockSpec((3, CONV_TC), lambda j: (0, j))
    return nb, seg, own, wspec


def _conv_fwd(proj, conv_w, name, after=()):
    t = proj.shape[0]
    nb, seg, own, wspec = _conv_specs(t)

    def body(ax_ref, ab_ref, ac_ref, w_ref, *rest):
        za_ref = rest[len(after)]
        u = ac_ref[...].astype(F32) * ax_ref[...].astype(F32)
        w = w_ref[...]
        yc = w[0:1] * _shift_down(u, 2) + w[1:2] * _shift_down(u, 1) + w[2:3] * u
        za_ref[...] = (ab_ref[...].astype(F32) * yc).astype(BF16)

    return pl.pallas_call(
        body, name=name, grid=(nb,), in_specs=[seg(C_AX), seg(C_AB), seg(C_AC), wspec] + [ANY] * len(after), out_specs=own,
        out_shape=jax.ShapeDtypeStruct((t, 1024), BF16), compiler_params=_params(("parallel",)),
    )(proj, proj, proj, conv_w, *after)


def _conv_bwd(dza, proj, conv_w, name):
    t = proj.shape[0]
    nb, seg, own, wspec = _conv_specs(t)

    def body(dz_ref, ax_ref, ab_ref, ac_ref, w_ref, dax_ref, dab_ref, dac_ref, dw_ref):
        ax, ab, ac, dz = ax_ref[...].astype(F32), ab_ref[...].astype(F32), ac_ref[...].astype(F32), dz_ref[...].astype(F32)
        w = w_ref[...]
        u = ac * ax
        u1, u2 = _shift_down(u, 1), _shift_down(u, 2)
        yc = w[0:1] * u2 + w[1:2] * u1 + w[2:3] * u
        dab_ref[...] = (dz * yc).astype(BF16)
        dyc = dz * ab
        du = w[2:3] * dyc + w[1:2] * _shift_up(dyc, 1) + w[0:1] * _shift_up(dyc, 2)
        dax_ref[...] = (du * ac).astype(BF16)
        dac_ref[...] = (du * ax).astype(BF16)
        dw_ref[0:1, :] = jnp.sum(dyc * u2, axis=0, keepdims=True)
        dw_ref[1:2, :] = jnp.sum(dyc * u1, axis=0, keepdims=True)
        dw_ref[2:3, :] = jnp.sum(dyc * u, axis=0, keepdims=True)

    act = jax.ShapeDtypeStruct((t, 1024), BF16)
    return pl.pallas_call(
        body, name=name, grid=(nb,), in_specs=[own, seg(C_AX), seg(C_AB), seg(C_AC), wspec], out_specs=[own, own, own, wspec],
        out_shape=[act, act, act, jax.ShapeDtypeStruct((3, 1024), F32)], compiler_params=_params(("parallel",)),
    )(dza, proj, proj, proj, conv_w)


def _dot(a, b, dims, precision=None):
    return lax.dot_general(a, b, (dims, ((), ())), precision=precision, preferred_element_type=F32)


_CONTRACT = {"nn": ((1,), (0,)), "nt": ((1,), (1,)), "tn": ((0,), (0,))}


def _bdot_raw(a, b, mode):
    return _dot(a.astype(BF16), b.astype(BF16), _CONTRACT[mode])


@functools.partial(jax.custom_vjp, nondiff_argnums=(2,))
def _bdot(a, b, mode):
    return _bdot_raw(a, b, mode)


def _bdot_fwd(a, b, mode):
    return _bdot_raw(a, b, mode), (a, b)


def _bdot_bwd(mode, res, ct):
    a, b = res
    if mode == "nn":
        return _bdot_raw(ct, b, "nt"), _bdot_raw(a, ct, "tn")
    if mode == "nt":
        return _bdot_raw(ct, b, "nn"), _bdot_raw(ct, a, "tn")
    return _bdot_raw(b, ct, "nt"), _bdot_raw(a, ct, "nn")


_bdot.defvjp(_bdot_fwd, _bdot_bwd)


def _gla_chunk(q, k, v, og, alr, s_in, wa, ba, gain):
    c = q.shape[0]
    hi = lax.Precision.HIGHEST
    z = _bdot(alr, wa, "nn") + ba
    la = (jnp.minimum(z, 0.0) - jnp.log(1.0 + jnp.exp(-jnp.abs(z)))) * (1.0 / TAU)
    row = lax.broadcasted_iota(jnp.int32, (c, c), 0)
    col = lax.broadcasted_iota(jnp.int32, (c, c), 1)
    lower = row >= col
    b = _dot(lower.astype(F32), la, ((1,), (0,)), hi)
    trow = lax.broadcasted_iota(jnp.int32, la.shape, 0)
    mid = jnp.sum(jnp.where(trow <= c // 2, la, 0.0), axis=0, keepdims=True)
    blast = jnp.sum(la, axis=0, keepdims=True)
    qs = q * (DK ** -0.5)
    e_up, e_dn = jnp.exp(b - mid), jnp.exp(mid - b)
    a_fwd = _bdot(qs * e_up, k * e_dn, "nt")
    a_rev = _bdot(qs * e_dn, k * e_up, "nt")
    att = jnp.where(lower, a_fwd, a_rev)
    o = _bdot(att, v, "nn") + _bdot(qs * jnp.exp(b), s_in, "nn")
    upd = _bdot(k * jnp.exp(blast - b), v, "tn")
    blast_col = _dot(la, jnp.ones((c, DV), F32), ((0,), (0,)), hi)
    s_out = jnp.exp(blast_col) * s_in + upd
    on = o * _rinv(o) * gain
    return on * og * _sig(og), s_out


def _gla_specs(t, rev):
    n = t // CHUNK
    ch = (lambda i: n - 1 - i) if rev else (lambda i: i)
    col = lambda w, c0: pl.BlockSpec((CHUNK, HEADS * w), lambda i, cb=c0 // (HEADS * w): (ch(i), cb))
    whole = lambda shape: pl.BlockSpec(shape, lambda i, nd=len(shape): (0,) * nd)
    specs = dict(
        q=col(DK, C_Q), k=col(DK, C_K), v=col(DV, C_V), og=col(DV, C_OG),
        alr=pl.BlockSpec((CHUNK, 128), lambda i: (ch(i), C_ALR // 128)),
        wa=whole((128, HEADS * DK)), ba=whole((1, HEADS * DK)), gain=whole((1, DV)),
        state=pl.BlockSpec((None, HEADS, DK, DV), lambda i: (ch(i), 0, 0, 0)),
        odk=pl.BlockSpec((CHUNK, HEADS * DK), lambda i: (ch(i), 0)), odv=pl.BlockSpec((CHUNK, HEADS * DV), lambda i: (ch(i), 0)),
        oalr=pl.BlockSpec((CHUNK, 128), lambda i: (ch(i), 0)), whole=whole,
    )
    return n, specs


def _head_cols(h):
    return slice(h * DK, (h + 1) * DK), slice(h * DV, (h + 1) * DV)


def _gla_fwd(proj, wa, ba, gain, name):
    t = proj.shape[0]
    n, sp = _gla_specs(t, False)

    def body(q_ref, k_ref, v_ref, og_ref, alr_ref, wa_ref, ba_ref, g_ref, zb_ref, st_ref, s_scr):
        @pl.when(pl.program_id(0) == 0)
        def _():
            s_scr[...] = jnp.zeros_like(s_scr)

        alr = alr_ref[...].astype(F32)
        for h in range(HEADS):
            kc, vc = _head_cols(h)
            s_in = s_scr[h]
            st_ref[h] = s_in
            zb, s_out = _gla_chunk(q_ref[:, kc].astype(F32), k_ref[:, kc].astype(F32), v_ref[:, vc].astype(F32), og_ref[:, vc].astype(F32),
                                   alr, s_in, wa_ref[:, kc].astype(F32), ba_ref[:, kc], g_ref[...])
            zb_ref[:, vc] = zb.astype(BF16)
            s_scr[h] = s_out

    return pl.pallas_call(
        body, name=name, grid=(n,),
        in_specs=[sp["q"], sp["k"], sp["v"], sp["og"], sp["alr"], sp["wa"], sp["ba"], sp["gain"]],
        out_specs=[sp["odv"], sp["state"]],
        out_shape=[jax.ShapeDtypeStruct((t, HEADS * DV), BF16), jax.ShapeDtypeStruct((n, HEADS, DK, DV), F32)],
        scratch_shapes=[pltpu.VMEM((HEADS, DK, DV), F32)],
        compiler_params=_params(("arbitrary",)),
    )(proj, proj, proj, proj, proj, wa, ba, gain)


def _gla_bwd(dzb, proj, states, wa, ba, gain, name):
    t = proj.shape[0]
    n, sp = _gla_specs(t, True)

    def body(dz_ref, q_ref, k_ref, v_ref, og_ref, alr_ref, st_ref, wa_ref, ba_ref, g_ref,
             dq_ref, dk_ref, dv_ref, dog_ref, dalr_ref, dwa_ref, dba_ref, dg_ref, ds_scr):
        @pl.when(pl.program_id(0) == 0)
        def _():
            ds_scr[...] = jnp.zeros_like(ds_scr)
            dwa_ref[...] = jnp.zeros_like(dwa_ref)
            dba_ref[...] = jnp.zeros_like(dba_ref)
            dg_ref[...] = jnp.zeros_like(dg_ref)

        alr = alr_ref[...].astype(F32)
        dalr_sum, dgain_sum = None, None
        for h in range(HEADS):
            kc, vc = _head_cols(h)
            args = (q_ref[:, kc].astype(F32), k_ref[:, kc].astype(F32), v_ref[:, vc].astype(F32), og_ref[:, vc].astype(F32),
                    alr, st_ref[h], wa_ref[:, kc].astype(F32), ba_ref[:, kc], g_ref[...])
            _, vjp = jax.vjp(_gla_chunk, *args)
            dq, dk, dv, dog, dalr, ds_in, dwa, dba, dgain = vjp((dz_ref[:, vc].astype(F32), ds_scr[h]))
            dq_ref[:, kc] = dq.astype(BF16)
            dk_ref[:, kc] = dk.astype(BF16)
            dv_ref[:, vc] = dv.astype(BF16)
            dog_ref[:, vc] = dog.astype(BF16)
            ds_scr[h] = ds_in
            dwa_ref[h] += dwa
            dba_ref[h] += dba
            dalr_sum = dalr if h == 0 else dalr_sum + dalr
            dgain_sum = dgain if h == 0 else dgain_sum + dgain
        dalr_ref[...] = dalr_sum
        dg_ref[...] += dgain_sum

    whole = sp["whole"]
    return pl.pallas_call(
        body, name=name, grid=(n,),
        in_specs=[sp["odv"], sp["q"], sp["k"], sp["v"], sp["og"], sp["alr"], sp["state"], sp["wa"], sp["ba"], sp["gain"]],
        out_specs=[sp["odk"], sp["odk"], sp["odv"], sp["odv"], sp["oalr"], whole((HEADS, 128, DK)), whole((HEADS, 1, DK)), whole((1, DV))],
        out_shape=[jax.ShapeDtypeStruct((t, HEADS * DK), BF16), jax.ShapeDtypeStruct((t, HEADS * DK), BF16),
                   jax.ShapeDtypeStruct((t, HEADS * DV), BF16), jax.ShapeDtypeStruct((t, HEADS * DV), BF16),
                   jax.ShapeDtypeStruct((t, 128), F32), jax.ShapeDtypeStruct((HEADS, 128, DK), F32),
                   jax.ShapeDtypeStruct((HEADS, 1, DK), F32), jax.ShapeDtypeStruct((1, DV), F32)],
        scratch_shapes=[pltpu.VMEM((HEADS, DK, DV), F32)],
        compiler_params=_params(("arbitrary",)),
    )(dzb, proj, proj, proj, proj, proj, states, wa, ba, gain)


def _local_step(x, p, tgt, gather_start, gather_finish, scatter_start, small):
    b_alpha, gain = small["b_alpha_up"], small["gla_head_gain"]
    gather_start(0, ())
    w = dict(gather_finish(0, ()))
    conv_w, w_alpha = w["conv_w"], w["w_alpha_up"]
    wa_p = jnp.zeros((128, HEADS * DK), BF16).at[:GATE_RANK].set(w_alpha.astype(BF16))

    t1 = gather_start(1, (w["in_t"],))
    h1 = _rms_fwd(x, small["g_pre_mix"], "rms_pre_mix")
    proj = _mm(h1, w["in_t"], "nt", BF16, "mm_proj", after=(t1,))
    t2 = gather_start(2, (proj,))
    za = _conv_fwd(proj, conv_w, "conv_fwd", after=(t2,))
    zb, states = _gla_fwd(proj, wa_p, b_alpha, gain, "gla_fwd")
    t3 = gather_start(3, (zb, za))
    w.update(gather_finish(1, (t3,)))
    ya = _mm(za, w["a_out"], "nn", BF16, "mm_ya", b3=True, tm=2048, tk=1024)
    yb = _mm(zb, w["b_out"], "nn", BF16, "mm_yb", b3=True, tm=2048, tk=1024)
    mix = _mix_fwd(proj, ya, yb, "mix_fwd")
    m2 = _mm(mix, w["mix"], "nn", F32, "mm_mix")
    t4 = gather_start(4, (m2,))
    x1, h2 = _post_pre(x, m2, small["g_post_mix"], small["g_pre_ffn"], "norm_mix_ffn")
    w.update(gather_finish(2, (h2, t4)))
    fg = _mm(h2, w["gate_t"], "nt", BF16, "mm_gate")
    fu = _mm(h2, w["up_t"], "nt", BF16, "mm_up")
    s = _swiglu_fwd(fg, fu, "swiglu_fwd")
    w.update(gather_finish(3, (s,)))
    f = _mm(s, w["down"], "nn", F32, "mm_down")
    x2, h3 = _post_pre(x1, f, small["g_post_ffn"], small["g_pre_ple"], "norm_ffn_ple")
    w.update(gather_finish(4, (h3,)))
    pg = _mm(h3, w["pg"], "nn", F32, "mm_pg")
    p_bf = p.astype(BF16)
    pp = _mm(p_bf, w["pp"], "nn", F32, "mm_pp", b3=True, tm=2048)
    loss_rows, d3, dpg, dpp, dg_post_ple = _ple_final(x2, pg, pp, tgt, small["g_post_ple"], "ple_final")

    gw = {}
    gw["pp"] = _mm(p_bf, dpp, "tn", BF16, "mm_dw_pp", out3=True)
    gw["pg"] = _mm(h3, dpg, "tn", BF16, "mm_dw_pg")
    dh3 = _mm(dpg, w["pg"], "nt", F32, "mm_dh3", after=(scatter_start(0, gw),))
    d2, df, dg_pre_ple, dg_post_ffn = _norm_bwd(d3, dh3, x2, small["g_pre_ple"], f, small["g_post_ffn"], "norm_bwd_ple_ffn")
    ds = _mm(df, w["down"], "nt", BF16, "mm_ds")
    gw["down"] = _mm(s, df, "tn", BF16, "mm_dw_down", tm=1408)
    dfg, dfu = _swiglu_bwd(ds, fg, fu, "swiglu_bwd")
    gw["gate_t"] = _mm(dfg, h2, "tn", BF16, "mm_dw_gate", after=(gw["down"],), tm=1408)
    gw["up_t"] = _mm(dfu, h2, "tn", BF16, "mm_dw_up", after=(gw["gate_t"],), tm=1408)
    dh2 = _mm(dfg, w["gate_t"], "nn", F32, "mm_dh2_gate", after=(scatter_start(1, gw),))
    dh2 = _mm(dfu, w["up_t"], "nn", F32, "mm_dh2_up", add=dh2, tm=1024)
    d1, dm2, dg_pre_ffn, dg_post_mix = _norm_bwd(d2, dh2, x1, small["g_pre_ffn"], m2, small["g_post_mix"], "norm_bwd_ffn_mix")
    dmix = _mm(dm2, w["mix"], "nt", F32, "mm_dmix")
    gw["mix"] = _mm(mix, dm2, "tn", BF16, "mm_dw_mix")
    dgab, dya, dyb = _mix_bwd(dmix, proj, ya, yb, "mix_bwd")
    dza = _mm(dya, w["a_out"], "nt", BF16, "mm_dza", after=(gw["mix"],), b3=True, tm=2048)
    gw["a_out"] = _mm(za, dya, "tn", BF16, "mm_dw_a_out", out3=True, tk=1024)
    gw["b_out"] = _mm(zb, dyb, "tn", BF16, "mm_dw_b_out", after=(gw["a_out"],), out3=True, tk=1024)
    dzb = _mm(dyb, w["b_out"], "nt", BF16, "mm_dzb", after=(scatter_start(2, gw),), b3=True, tm=2048)
    dax, dab, dac, dconv = _conv_bwd(dza, proj, conv_w, "conv_bwd")
    dq, dk, dv, dog, dalr, dwa, dba, dgain = _gla_bwd(dzb, proj, states, wa_p, b_alpha, gain, "gla_bwd")
    dproj = jnp.concatenate([dgab, dax, dab, dac, dq, dk, dv, dog, dalr.astype(BF16)], axis=1)
    gw["in_t"] = _mm(dproj, h1, "tn", BF16, "mm_dw_in", tm=1152)
    dh1 = _mm(dproj, w["in_t"], "nn", F32, "mm_dh1", after=(scatter_start(3, gw),))
    grad_x, dg_pre_mix = _norm_bwd(d1, dh1, x, small["g_pre_mix"], None, None, "norm_bwd_mix")

    gs = dict(
        conv_w=dconv,
        w_alpha_up=jnp.transpose(dwa[:, :GATE_RANK, :], (1, 0, 2)).reshape(GATE_RANK, HEADS * DK),
        b_alpha_up=dba.reshape(1, HEADS * DK), gla_head_gain=dgain,
        g_pre_mix=dg_pre_mix, g_post_mix=dg_post_mix, g_pre_ffn=dg_pre_ffn, g_post_ffn=dg_post_ffn,
        g_pre_ple=dg_pre_ple, g_post_ple=dg_post_ple,
    )
    return loss_rows, grad_x, gs


def _place():
    x, y, c = lax.axis_index("x"), lax.axis_index("y"), lax.axis_index("c")
    return x, y, c, [(1 - x, y), (x, 1 - y), (1 - x, 1 - y)]


def _all_gather(shards, name, cid=None):
    n = len(shards)

    def body(*refs):
        ins, outs = refs[:n], refs[n:2 * n]
        send_sems, recv_sems, local_sems = refs[2 * n:]
        x, y, c, chips = _place()
        me, sibling = (x, y, c), (x, y, 1 - c)

        def slot(px, py, pc):
            return 4 * px + 2 * py + pc

        def copy(a, k, block, to, src=None):
            dst = outs[a].at[slot(*block)]
            return pltpu.make_async_remote_copy(src_ref=dst if src is None else src, dst_ref=dst, send_sem=send_sems.at[a, k],
                                                recv_sem=recv_sems.at[a, k], device_id=to, device_id_type=MESH)

        mine = [pltpu.make_async_copy(ins[a], outs[a].at[slot(*me)], local_sems.at[a]) for a in range(n)]
        for cp in mine:
            cp.start()
        first = []
        for j, chip in enumerate(chips):
            first += [copy(a, 1 + j, me, (*chip, c), src=ins[a]) for a in range(n)]
        first += [copy(a, 0, me, sibling, src=ins[a]) for a in range(n)]
        for cp in first:
            cp.start()
        passed = []
        for j, chip in enumerate(chips):
            for a in range(n):
                copy(a, 1 + j, (*chip, c), me).wait_recv()
                cp = copy(a, 4 + j, (*chip, c), sibling)
                cp.start()
                passed.append(cp)
        for a in range(n):
            copy(a, 0, sibling, me).wait_recv()
        for j, chip in enumerate(chips):
            for a in range(n):
                copy(a, 4 + j, (*chip, 1 - c), me).wait_recv()
        for cp in first + passed:
            cp.wait_send()
        for cp in mine:
            cp.wait()

    if cid is None:
        return pl.pallas_call(
            body, name=name, in_specs=[ANY] * n, out_specs=[ANY] * n,
            out_shape=[jax.ShapeDtypeStruct((N_DEV,) + s.shape, s.dtype) for s in shards],
            scratch_shapes=[pltpu.SemaphoreType.DMA((n, 7)), pltpu.SemaphoreType.DMA((n, 7)), pltpu.SemaphoreType.DMA((n,))],
        )(*shards)

    src = [jax.new_ref(s, memory_space=pltpu.MemorySpace.HBM) for s in shards]
    dst = [jax.empty_ref(jax.ShapeDtypeStruct((N_DEV,) + s.shape, s.dtype), memory_space=pltpu.MemorySpace.HBM) for s in shards]

    @pl.kernel(mesh=plsc.ScalarSubcoreMesh(axis_name="seq", num_cores=1), name=name,
               scratch_types=(pltpu.SemaphoreType.DMA((n, 7)), pltpu.SemaphoreType.DMA((n, 7)), pltpu.SemaphoreType.DMA((n,))),
               compiler_params=pltpu.CompilerParams(collective_id=cid))
    def launch(send_sems, recv_sems, local_sems):
        x, y, c, chips = _place()
        barrier = pltpu.get_barrier_semaphore()
        for peer in [(x, y, 1 - c)] + [(*chip, c) for chip in chips]:
            pl.semaphore_signal(barrier, inc=1, device_id=peer, device_id_type=MESH)
        pl.semaphore_wait(barrier, 4)
        body(*src, *dst, send_sems, recv_sems, local_sems)

    launch()
    return [r[...] for r in dst]


def _reduce_scatter(parts, name, cid):
    n = len(parts)
    src = [jax.new_ref(s, memory_space=pltpu.MemorySpace.HBM) for s in parts]
    dst = [jax.empty_ref(jax.ShapeDtypeStruct(s.shape, s.dtype), memory_space=pltpu.MemorySpace.HBM) for s in parts]

    @pl.kernel(mesh=plsc.ScalarSubcoreMesh(axis_name="seq", num_cores=1), name=name,
               scratch_types=(pltpu.SemaphoreType.DMA((n, N_DEV - 1)), pltpu.SemaphoreType.DMA((n, N_DEV - 1)), pltpu.SemaphoreType.DMA((n,))),
               compiler_params=pltpu.CompilerParams(collective_id=cid))
    def launch(send_sems, recv_sems, local_sems):
        x, y, c, _ = _place()
        me = 4 * x + 2 * y + c
        peers = [(1 - x if k & 4 else x, 1 - y if k & 2 else y, 1 - c if k & 1 else c) for k in range(1, N_DEV)]
        barrier = pltpu.get_barrier_semaphore()
        for peer in peers:
            pl.semaphore_signal(barrier, inc=1, device_id=peer, device_id_type=MESH)
        pl.semaphore_wait(barrier, N_DEV - 1)
        mine = [pltpu.make_async_copy(src[a].at[me], dst[a].at[me], local_sems.at[a]) for a in range(n)]
        for cp in mine:
            cp.start()
        cps = []
        for a in range(n):
            for k, (px, py, pc) in enumerate(peers):
                cps.append(pltpu.make_async_remote_copy(src_ref=src[a].at[4 * px + 2 * py + pc], dst_ref=dst[a].at[me], send_sem=send_sems.at[a, k],
                                                        recv_sem=recv_sems.at[a, k], device_id=(px, py, pc), device_id_type=MESH))
        for cp in cps:
            cp.start()
        for cp in cps:
            cp.wait_recv()
        for cp in cps:
            cp.wait_send()
        for cp in mine:
            cp.wait()

    launch()
    return [r[...] for r in dst]


def _sibling_exchange(parts, name):
    n = len(parts)
    pieces = [_row_pieces(s.shape[1]) for s in parts]

    def body(*refs):
        ins, outs = refs[:n], refs[n:2 * n]
        send_sems, recv_sems = refs[2 * n:]
        x, y, c, _ = _place()

        def copy(a, ch, q, rows):
            return pltpu.make_async_remote_copy(src_ref=ins[a].at[2 * ch + 1 - c, rows], dst_ref=outs[a].at[ch, rows], send_sem=send_sems.at[a, ch, q],
                                                recv_sem=recv_sems.at[a, ch, q], device_id=(x, y, 1 - c), device_id_type=MESH)

        cps = [copy(a, ch, q, rows) for ch in range(4) for a in range(n) for q, rows in enumerate(pieces[a])]
        for cp in cps:
            cp.start()
        for cp in cps:
            cp.wait_recv()
        for cp in cps:
            cp.wait_send()

    return pl.pallas_call(
        body, name=name, in_specs=[ANY] * n, out_specs=[ANY] * n,
        out_shape=[jax.ShapeDtypeStruct((4,) + s.shape[1:], s.dtype) for s in parts],
        scratch_shapes=[pltpu.SemaphoreType.DMA((n, 4, PIECES)), pltpu.SemaphoreType.DMA((n, 4, PIECES))],
    )(*parts)


def _chip_exchange(parts, name):
    n = len(parts)

    def body(*refs):
        ins, outs = refs[:n], refs[n:2 * n]
        send_sems, recv_sems, local_sems = refs[2 * n:]
        x, y, c, chips = _place()
        my_chip = 2 * x + y

        def copy(a, j):
            px, py = chips[j]
            return pltpu.make_async_remote_copy(src_ref=ins[a].at[2 * px + py], dst_ref=outs[a].at[my_chip], send_sem=send_sems.at[a, j],
                                                recv_sem=recv_sems.at[a, j], device_id=(px, py, c), device_id_type=MESH)

        def landing(a, j):
            px, py = chips[j]
            return pltpu.make_async_remote_copy(src_ref=ins[a].at[my_chip], dst_ref=outs[a].at[2 * px + py], send_sem=send_sems.at[a, j],
                                                recv_sem=recv_sems.at[a, j], device_id=(px, py, c), device_id_type=MESH)

        mine = [pltpu.make_async_copy(ins[a].at[my_chip], outs[a].at[my_chip], local_sems.at[a]) for a in range(n)]
        for cp in mine:
            cp.start()
        cps = [copy(a, j) for j in range(3) for a in range(n)]
        for cp in cps:
            cp.start()
        for j in range(3):
            for a in range(n):
                landing(a, j).wait_recv()
        for cp in cps:
            cp.wait_send()
        for cp in mine:
            cp.wait()

    return pl.pallas_call(
        body, name=name, in_specs=[ANY] * n, out_specs=[ANY] * n,
        out_shape=[jax.ShapeDtypeStruct(s.shape, s.dtype) for s in parts],
        scratch_shapes=[pltpu.SemaphoreType.DMA((n, 3)), pltpu.SemaphoreType.DMA((n, 3)), pltpu.SemaphoreType.DMA((n,))],
    )(*parts)


def _pair_add(mine8, got4, name):
    _, r, cols = mine8.shape
    steps, blk, at = _tiles(r, cols)
    core = lax.axis_index("c").astype(jnp.int32).reshape(1)

    def body(c_ref, a_ref, b_ref, o_ref):
        o_ref[...] = (a_ref[...].astype(F32) + b_ref[...].astype(F32)).astype(BF16)

    return pl.pallas_call(
        body, name=name,
        grid_spec=pltpu.PrefetchScalarGridSpec(
            num_scalar_prefetch=1, grid=(4, steps),
            in_specs=[pl.BlockSpec((None,) + blk, lambda ch, i, c_ref: (2 * ch + c_ref[0],) + at(i)),
                      pl.BlockSpec((None,) + blk, lambda ch, i, c_ref: (ch,) + at(i))],
            out_specs=pl.BlockSpec((None,) + blk, lambda ch, i, c_ref: (ch,) + at(i))),
        out_shape=jax.ShapeDtypeStruct((4, r, cols), BF16),
        compiler_params=_params(("parallel", "parallel")),
    )(core, mine8, got4)


HBM = pl.BlockSpec(memory_space=pltpu.HBM)
SEM = pl.BlockSpec(memory_space=pltpu.SEMAPHORE)
EFFECT = pltpu.SideEffectType.DATAFLOW_SIDE_EFFECTING


def _in_hbm(a):
    return pltpu.with_memory_space_constraint(a, pltpu.HBM)


def _remote_copies(plan, srcs, lands, send_sems, recv_sems):
    return [pltpu.make_async_remote_copy(src_ref=s, dst_ref=d, send_sem=send_sems.at[i], recv_sem=recv_sems.at[i], device_id=peer,
                                         device_id_type=MESH) for i, (s, d, peer) in enumerate(plan(srcs, lands))]


def _copies_start(plan, n_copies, srcs, land_shapes, name, after=()):
    ns, nl = len(srcs), len(land_shapes)

    def body(*refs):
        send_sems, recv_sems = refs[ns + nl + len(after):ns + nl + len(after) + 2]
        for cp in _remote_copies(plan, refs[:ns], refs[ns:ns + nl], send_sems, recv_sems):
            cp.start()
        refs[-1][...] = jnp.zeros((8, 128), F32)

    sems = pltpu.SemaphoreType.DMA((n_copies,))
    return pl.pallas_call(
        body, name=name,
        out_shape=(sems, sems, *[pltpu.HBM(s.shape, s.dtype) for s in srcs], *[pltpu.HBM(s.shape, s.dtype) for s in land_shapes],
                   jax.ShapeDtypeStruct((8, 128), F32)),
        in_specs=[HBM] * (ns + nl) + [ANY] * len(after),
        out_specs=(SEM, SEM, *[HBM] * (ns + nl), pl.BlockSpec(memory_space=pltpu.VMEM)),
        input_output_aliases={i: 2 + i for i in range(ns + nl)},
        compiler_params=pltpu.CompilerParams(has_side_effects=EFFECT),
    )(*[_in_hbm(s) for s in srcs], *[_in_hbm(lax.empty(s.shape, s.dtype)) for s in land_shapes], *after)


def _copies_wait(plan, state, ns, name, after=()):
    send_sems, recv_sems, *arrs = state[:-1]
    n = len(arrs)

    def body(*refs):
        cps = _remote_copies(plan, refs[:ns], refs[ns:n], refs[n], refs[n + 1])
        for cp in cps:
            cp.wait_send()
        for cp in cps:
            cp.wait_recv()

    out = pl.pallas_call(
        body, name=name, out_shape=tuple(pltpu.HBM(a.shape, a.dtype) for a in arrs),
        in_specs=[HBM] * n + [SEM, SEM] + [ANY] * len(after), out_specs=tuple([HBM] * n),
        input_output_aliases={i: i for i in range(n)},
        compiler_params=pltpu.CompilerParams(has_side_effects=EFFECT),
    )(*arrs, send_sems, recv_sems, *after)
    return list(out[:ns]), list(out[ns:])


def _gather_plan(srcs, lands):
    x, y, c, chips = _place()
    peers = [(x, y, 1 - c)] + [(*chip, c) for chip in chips]
    return [(s, l.at[4 * x + 2 * y + c], peer) for s, l in zip(srcs, lands) for peer in peers]


def _scatter_plan(srcs, lands):
    x, y, c, _ = _place()
    peers = [(1 - x if k & 4 else x, 1 - y if k & 2 else y, 1 - c if k & 1 else c) for k in range(1, N_DEV)]
    return [(s.at[4 * px + 2 * py + pc], l.at[4 * x + 2 * y + c], (px, py, pc)) for s, l in zip(srcs, lands) for px, py, pc in peers]


def _everyone_plan(srcs, lands):
    x, y, c, _ = _place()
    peers = [(1 - x if k & 4 else x, 1 - y if k & 2 else y, 1 - c if k & 1 else c) for k in range(1, N_DEV)]
    return [(s, l.at[4 * x + 2 * y + c], peer) for s, l in zip(srcs, lands) for peer in peers]


def _sum_parts(got, own, me, name):
    def body(me_ref, got_ref, own_ref, o_ref):
        acc = jnp.where(me_ref[0] == 0, own_ref[...], got_ref[0])
        for d in range(1, N_DEV):
            acc = acc + jnp.where(me_ref[0] == d, own_ref[...], got_ref[d])
        o_ref[...] = acc

    return pl.pallas_call(
        body, name=name,
        grid_spec=pltpu.PrefetchScalarGridSpec(
            num_scalar_prefetch=1, grid=(1,),
            in_specs=[pl.BlockSpec(got.shape, lambda i, me_ref: (0, 0, 0)), pl.BlockSpec(own.shape, lambda i, me_ref: (0, 0))],
            out_specs=pl.BlockSpec(own.shape, lambda i, me_ref: (0, 0))),
        out_shape=jax.ShapeDtypeStruct(own.shape, F32),
    )(me.astype(jnp.int32).reshape(1), got, own)


def _chip_plan(srcs, lands):
    x, y, c, chips = _place()
    return [(s.at[2 * px + py], l.at[2 * x + y], (px, py, c)) for s, l in zip(srcs, lands) for px, py in chips]


PIECES = 8


def _row_pieces(rows):
    for k in (PIECES, 4, 2):
        if rows % (16 * k) == 0:
            return [pl.ds(q * (rows // k), rows // k) for q in range(k)]
    return [pl.ds(0, rows)]


def _put_own(shard, zone, me, name):
    r, c = shard.shape
    tr = r if r <= 256 else _pick(r, (256, 64))

    def body(me_ref, s_ref, z_ref, o_ref):
        o_ref[...] = s_ref[...]

    return pl.pallas_call(
        body, name=name,
        grid_spec=pltpu.PrefetchScalarGridSpec(
            num_scalar_prefetch=1, grid=(r // tr,),
            in_specs=[pl.BlockSpec((tr, c), lambda i, me_ref: (i, 0)), ANY],
            out_specs=pl.BlockSpec((None, tr, c), lambda i, me_ref: (me_ref[0], i, 0))),
        out_shape=jax.ShapeDtypeStruct(zone.shape, zone.dtype), input_output_aliases={2: 0},
        compiler_params=_params(("arbitrary",)),
    )(me.astype(jnp.int32).reshape(1), shard, zone)


def _gather_finish(lands, name):
    n = len(lands)
    pieces = [_row_pieces(s.shape[1]) for s in lands]

    def body(*refs):
        zones, outs = refs[:n], refs[n:2 * n]
        send_sems, recv_sems = refs[2 * n:]
        x, y, c, chips = _place()
        cps = []
        for j, (px, py) in enumerate(chips):
            for a in range(n):
                for q, rows in enumerate(pieces[a]):
                    cps.append(pltpu.make_async_remote_copy(
                        src_ref=zones[a].at[4 * px + 2 * py + c, rows], dst_ref=outs[a].at[4 * px + 2 * py + c, rows],
                        send_sem=send_sems.at[a, j, q], recv_sem=recv_sems.at[a, j, q], device_id=(x, y, 1 - c), device_id_type=MESH))
        for cp in cps:
            cp.start()
        for cp in cps:
            cp.wait_recv()
        for cp in cps:
            cp.wait_send()

    return pl.pallas_call(
        body, name=name, in_specs=[ANY] * n, out_specs=[ANY] * n,
        out_shape=[jax.ShapeDtypeStruct(l.shape, l.dtype) for l in lands],
        input_output_aliases={a: a for a in range(n)},
        scratch_shapes=[pltpu.SemaphoreType.DMA((n, 3, PIECES)), pltpu.SemaphoreType.DMA((n, 3, PIECES))],
    )(*lands)


def _sum_everywhere(v, name):
    rows = v.shape[0]

    def body(v_ref, o_ref, buf, send_sems, recv_sems):
        x, y, c, _ = _place()
        me = 4 * x + 2 * y + c
        buf[me] = v_ref[...]
        cps = []
        for k in range(1, N_DEV):
            fx, fy, fc = (k >> 2) & 1, (k >> 1) & 1, k & 1
            to = (1 - x if fx else x, 1 - y if fy else y, 1 - c if fc else c)
            cps.append(pltpu.make_async_remote_copy(src_ref=buf.at[me], dst_ref=buf.at[me], send_sem=send_sems.at[k - 1],
                                                    recv_sem=recv_sems.at[k - 1], device_id=to, device_id_type=MESH))
        for cp in cps:
            cp.start()
        for cp in cps:
            cp.wait_recv()
        for cp in cps:
            cp.wait_send()
        acc = buf[0]
        for d in range(1, N_DEV):
            acc = acc + buf[d]
        o_ref[...] = acc

    vm = pl.BlockSpec(memory_space=pltpu.VMEM)
    return pl.pallas_call(
        body, name=name, in_specs=[vm], out_specs=vm, out_shape=jax.ShapeDtypeStruct(v.shape, F32),
        scratch_shapes=[pltpu.VMEM((N_DEV, rows, 128), F32), pltpu.SemaphoreType.DMA((N_DEV - 1,)), pltpu.SemaphoreType.DMA((N_DEV - 1,))],
    )(v)


def _adamw_parts(w, got, mine, me, m, v, name, after=()):
    r, c = w.shape
    n_parts = got.shape[0]
    steps, blk, at = _tiles(r, c)

    def body(me_ref, w_ref, got_ref, own_ref, m_ref, v_ref, *rest):
        go_ref, d_ref, mo_ref, vo_ref = rest[len(after):]
        own = own_ref[...].astype(F32)
        gv = jnp.where(me_ref[0] == 0, own, got_ref[0].astype(F32))
        for d in range(1, n_parts):
            gv = gv + jnp.where(me_ref[0] == d, own, got_ref[d].astype(F32))
        _adamw_math(gv, w_ref, m_ref, v_ref, go_ref, d_ref, mo_ref, vo_ref)

    tile = pl.BlockSpec(blk, lambda i, me_ref: at(i))
    out = jax.ShapeDtypeStruct((r, c), F32)
    return pl.pallas_call(
        body, name=name,
        grid_spec=pltpu.PrefetchScalarGridSpec(
            num_scalar_prefetch=1, grid=(steps,),
            in_specs=[tile, pl.BlockSpec((n_parts,) + blk, lambda i, me_ref: (0,) + at(i)),
                      pl.BlockSpec((None,) + blk, lambda i, me_ref: (me_ref[0],) + at(i)), tile, tile] + [ANY] * len(after),
            out_specs=[tile] * 4),
        out_shape=[out] * 4, compiler_params=_params(("parallel",)),
    )(me.astype(jnp.int32).reshape(1), w, got, mine, m, v, *after)


def _adamw_math(gv, w_ref, m_ref, v_ref, go_ref, d_ref, mo_ref, vo_ref):
    mn = B1 * m_ref[...] + (1.0 - B1) * gv
    vn = B2 * v_ref[...] + (1.0 - B2) * (gv * gv)
    m_hat = mn / (1.0 - B1 ** STEP)
    v_hat = vn / (1.0 - B2 ** STEP)
    go_ref[...] = gv
    d_ref[...] = -LR * (m_hat / (jnp.sqrt(v_hat) + ADAM_EPS) + WD * w_ref[...])
    mo_ref[...] = mn
    vo_ref[...] = vn


def _adamw(w, g, m, v, name):
    r, c = w.shape
    parts = g.ndim == 3
    tr = r if r <= 128 else _pick(r, (128, 64))

    def body(w_ref, g_ref, m_ref, v_ref, go_ref, d_ref, mo_ref, vo_ref):
        if parts:
            gv = g_ref[0].astype(F32)
            for d in range(1, g.shape[0]):
                gv = gv + g_ref[d].astype(F32)
        else:
            gv = g_ref[...]
        mn = B1 * m_ref[...] + (1.0 - B1) * gv
        vn = B2 * v_ref[...] + (1.0 - B2) * (gv * gv)
        m_hat = mn / (1.0 - B1 ** STEP)
        v_hat = vn / (1.0 - B2 ** STEP)
        go_ref[...] = gv
        d_ref[...] = -LR * (m_hat / (jnp.sqrt(v_hat) + ADAM_EPS) + WD * w_ref[...])
        mo_ref[...] = mn
        vo_ref[...] = vn

    tile = pl.BlockSpec((tr, c), lambda i: (i, 0))
    g_spec = pl.BlockSpec((g.shape[0], tr, c), lambda i: (0, i, 0)) if parts else tile
    out = jax.ShapeDtypeStruct((r, c), F32)
    return pl.pallas_call(
        body, name=name, grid=(r // tr,), in_specs=[tile, g_spec, tile, tile], out_specs=[tile] * 4, out_shape=[out] * 4,
        compiler_params=_params(("parallel",)),
    )(w, g, m, v)


BIG = ["w_in", "w_a_out", "w_b_out", "w_mix_out", "w_ff_gate", "w_ff_up", "w_ff_down", "w_ple_gate", "w_ple_proj"]
TRANSPOSED = ["w_in", "w_ff_gate", "w_ff_up"]
SMALL = ["conv_w", "w_alpha_up", "b_alpha_up", "gla_head_gain", "g_pre_mix", "g_post_mix", "g_pre_ffn", "g_post_ffn", "g_pre_ple", "g_post_ple"]
WEIGHTS = ["w_in", "conv_w", "w_a_out", "w_alpha_up", "b_alpha_up", "gla_head_gain", "w_b_out", "w_mix_out", "g_pre_mix", "g_post_mix",
           "g_pre_ffn", "g_post_ffn", "w_ff_gate", "w_ff_up", "w_ff_down", "g_pre_ple", "g_post_ple", "w_ple_gate", "w_ple_proj"]


def _in_t_from_blocks(z):
    w = z.reshape(-1, z.shape[-1])
    return jnp.concatenate([w[R_GA:R_END], w[:R_ALR], w[R_ALR:R_GA], jnp.zeros((128 - GATE_RANK, w.shape[1]), w.dtype)], axis=0)


def _blocks_from_in_t(g):
    per = R_END // N_DEV

    def rows(lo, hi):
        out = []
        for n0, n1, p0 in ((0, R_ALR, C_AX), (R_ALR, R_GA, C_ALR), (R_GA, R_END, 0)):
            a, e = max(lo, n0), min(hi, n1)
            if a < e:
                out.append(g[p0 + a - n0:p0 + e - n0])
        return out

    return jnp.stack([jnp.concatenate(rows(b * per, (b + 1) * per), axis=0) for b in range(N_DEV)])


def _cols_to_full(g8):
    n, r, c = g8.shape
    return jnp.transpose(g8, (1, 0, 2)).reshape(r, n * c)


def _full_to_cols(a):
    r, c = a.shape
    return jnp.transpose(a.reshape(r, N_DEV, c // N_DEV), (1, 0, 2))


def _pack(arrs, rows):
    flat = jnp.concatenate([a.reshape(-1) for a in arrs])
    return jnp.pad(flat, (0, rows * 128 - flat.shape[0])).reshape(rows, 128)


def _unpack(packed, shapes):
    flat, out, o = packed.reshape(-1), [], 0
    for s in shapes:
        size = 1
        for d in s:
            size *= d
        out.append(flat[o:o + size].reshape(s))
        o += size
    return out


def kernel(x, p, w_in, conv_w, w_a_out, w_alpha_up, b_alpha_up, gla_head_gain, w_b_out, w_mix_out, g_pre_mix, g_post_mix, g_pre_ffn, g_post_ffn, w_ff_gate, w_ff_up, w_ff_down, g_pre_ple, g_post_ple, w_ple_gate, w_ple_proj, loss_target, m_w_in, m_conv_w, m_w_a_out, m_w_alpha_up, m_b_alpha_up, m_gla_head_gain, m_w_b_out, m_w_mix_out, m_g_pre_mix, m_g_post_mix, m_g_pre_ffn, m_g_post_ffn, m_w_ff_gate, m_w_ff_up, m_w_ff_down, m_g_pre_ple, m_g_post_ple, m_w_ple_gate, m_w_ple_proj, v_w_in, v_conv_w, v_w_a_out, v_w_alpha_up, v_b_alpha_up, v_gla_head_gain, v_w_b_out, v_w_mix_out, v_g_pre_mix, v_g_post_mix, v_g_pre_ffn, v_g_post_ffn, v_w_ff_gate, v_w_ff_up, v_w_ff_down, v_g_pre_ple, v_g_post_ple, v_w_ple_gate, v_w_ple_proj):
    args = dict(locals())
    shard = lambda n, a: jnp.transpose(a[0]) if n in TRANSPOSED else a[0]
    wts = {n: shard(n, args[n]) for n in WEIGHTS}
    mom = {n: shard(n, args["m_" + n]) for n in WEIGHTS}
    var = {n: shard(n, args["v_" + n]) for n in WEIGHTS}
    me =4 * lax.axis_index("x") + 2 * lax.axis_index("y") + lax.axis_index("c")

    groups = [["w_in", "conv_w", "w_alpha_up"], ["w_a_out", "w_b_out", "w_mix_out"], ["w_ff_gate", "w_ff_up"], ["w_ff_down"],
              ["w_ple_gate", "w_ple_proj"]]
    grad_groups = [["w_ple_proj", "w_ple_gate"], ["w_ff_down", "w_ff_gate", "w_ff_up"], ["w_mix_out", "w_a_out", "w_b_out"], ["w_in"]]
    rows_full = lambda g: g.reshape(-1, g.shape[-1])
    gathers, scatters = {}, {}

    def gather_start(gi, after):
        shards = [wts[n].astype(BF16) if n in BIG else wts[n] for n in groups[gi]]
        zones = [jax.ShapeDtypeStruct((N_DEV,) + s.shape, s.dtype) for s in shards]
        gathers[gi] = (shards, _copies_start(_gather_plan, 4 * len(shards), shards, zones, "gather_start_%d" % gi, after))
        return gathers[gi][1][-1]

    def gather_finish(gi, after):
        shards, state = gathers[gi]
        shards, zones = _copies_wait(_gather_plan, state, len(shards), "gather_wait_%d" % gi, after)
        zones = _gather_finish(zones, "gather_finish_%d" % gi)
        g8 = {n: _put_own(s, z, me, "gather_own_" + n) for n, s, z in zip(groups[gi], shards, zones)}
        if gi == 0:
            return dict(in_t=_in_t_from_blocks(g8["w_in"]),
                        conv_w=_cols_to_full(g8["conv_w"]), w_alpha_up=_cols_to_full(g8["w_alpha_up"]))
        if gi == 1:
            return dict(a_out=g8["w_a_out"], b_out=g8["w_b_out"], mix=rows_full(g8["w_mix_out"]))
        if gi == 2:
            return dict(gate_t=rows_full(g8["w_ff_gate"]), up_t=rows_full(g8["w_ff_up"]))
        if gi == 3:
            return dict(down=rows_full(g8["w_ff_down"]))
        return dict(pg=rows_full(g8["w_ple_gate"]), pp=g8["w_ple_proj"])

    def scatter_start(gi, gw):
        if gi == 3:
            full = dict(w_in=_blocks_from_in_t(gw["in_t"]))
        elif gi == 1:
            full = dict(w_ff_down=gw["down"], w_ff_gate=gw["gate_t"], w_ff_up=gw["up_t"])
        elif gi == 2:
            full = dict(w_mix_out=gw["mix"], w_a_out=gw["a_out"], w_b_out=gw["b_out"])
        else:
            full = dict(w_ple_proj=gw["pp"], w_ple_gate=gw["pg"])
        parts = [full[n] if full[n].ndim == 3 else full[n].reshape(N_DEV, -1, full[n].shape[-1]) for n in grad_groups[gi]]
        if gi == 3:
            from_sibling = _sibling_exchange(parts, "scatter_sibling_%d" % gi)
            parts = [_pair_add(a, b, "scatter_add_%d_%s" % (gi, n)) for n, a, b in zip(grad_groups[gi], parts, from_sibling)]
            scatters[gi] = _copies_start(_chip_plan, 3 * len(parts), parts, parts, "scatter_start_%d" % gi)
        else:
            scatters[gi] = _copies_start(_scatter_plan, (N_DEV - 1) * len(parts), parts, parts, "scatter_start_%d" % gi)
        return scatters[gi][-1]

    small = {n: wts[n].reshape(1, -1) for n in SMALL[2:]}

    loss_rows, grad_x, gs = _local_step(x[0], p[0, 0], loss_target[0], gather_start, gather_finish, scatter_start, small)
    gs["loss"] = jnp.sum(loss_rows).reshape(1, 1)

    small_shapes = [gs[n].shape for n in SMALL]
    gs_packed = _pack([gs[n] for n in SMALL + ["loss"]], 192)
    small_state = _copies_start(_everyone_plan, N_DEV - 1, [gs_packed], [jax.ShapeDtypeStruct((N_DEV,) + gs_packed.shape, F32)],
                                "small_start", (grad_x,))

    res, done = {}, (small_state[-1],)
    for gi, names in enumerate(grad_groups):
        plan, slot = (_chip_plan, me // 2) if gi == 3 else (_scatter_plan, me)
        mine, got = _copies_wait(plan, scatters[gi], len(names), "scatter_wait_%d" % gi, done)
        for n, g, own in zip(names, got, mine):
            res[n] = _adamw_parts(wts[n], g, own, slot, mom[n], var[n], "adamw_" + n)
        done = tuple(res[n][1] for n in names)

    (gs_own,), (gs_got,) = _copies_wait(_everyone_plan, small_state, 1, "small_wait", done)
    gsum = dict(zip(SMALL + ["loss"], _unpack(_sum_parts(gs_got, gs_own, me, "small_sum"), small_shapes + [(1, 1)])))
    loss = gsum["loss"].reshape(())
    gsum["conv_w"] = lax.dynamic_index_in_dim(gsum["conv_w"].reshape(3, N_DEV, -1), me, axis=1, keepdims=False)
    gsum["w_alpha_up"] = lax.dynamic_index_in_dim(gsum["w_alpha_up"].reshape(GATE_RANK, N_DEV, -1), me, axis=1, keepdims=False)

    shard_shapes = [wts[n].shape for n in SMALL]
    packed = [_pack([d[n] for n in SMALL], 120) for d in (wts, gsum, mom, var)]
    outs = [_unpack(o, shard_shapes) for o in _adamw(*packed, "adamw_small")]
    for i, n in enumerate(SMALL):
        res[n] = [o[i] for o in outs]

    back = lambda n, a: (jnp.transpose(a) if n in TRANSPOSED else a)[None]
    return (loss, grad_x[None], *[back(n, res[n][i]) for i in range(4) for n in WEIGHTS])
```

```python
import functools

import jax
import jax.numpy as jnp
from jax import lax
from jax.experimental import pallas as pl
from jax.experimental.pallas import tpu as pltpu
from jax.experimental.pallas import tpu_sc as plsc

F32, BF16 = jnp.float32, jnp.bfloat16
EPS = 1e-6
CHUNK = 64
STEP_CHUNKS = 2
STEP_ROWS = STEP_CHUNKS * CHUNK
HEADS, DK, DV = 4, 128, 256
GATE_RANK = 16
TAU = 16.0
LR, B1, B2, ADAM_EPS, WD, STEP = 0.001, 0.9, 0.999, 1e-08, 0.01, 10
N_DEV = 8
MESH = pl.DeviceIdType.MESH
VMEM_LIMIT = 56 * 1024 * 1024
ANY = pl.BlockSpec(memory_space=pl.ANY)

C_GA, C_GB, C_AX, C_AB, C_AC, C_Q, C_K, C_V, C_OG, C_ALR = 0, 2048, 4096, 5120, 6144, 7168, 7680, 8192, 9216, 10240
IN_PAD = 10368
R_AX, R_AB, R_AC, R_Q, R_K, R_V, R_OG, R_ALR, R_GA, R_GB, R_END = 0, 1024, 2048, 3072, 3584, 4096, 5120, 6144, 6160, 8208, 10256


def _params(sem):
    return pltpu.CompilerParams(dimension_semantics=sem, vmem_limit_bytes=VMEM_LIMIT)


def _pick(n, cands):
    for c in cands:
        if n % c == 0:
            return c
    return n


def _tiles(r, c):
    for tr in (128, 64):
        if r % tr == 0:
            return r // tr, (tr, c), lambda i: (i, 0)
    tc = _pick(c, (256, 128))
    return c // tc, (r, tc), lambda i: (0, i)


def _mm(a, b, mode, out_dtype, name, after=(), add=None, b3=False, out3=False, tm=None, tk=None):
    bshape = (b.shape[1], N_DEV * b.shape[2]) if b3 else b.shape
    if mode == "nn":
        (m, k), (k2, n) = a.shape, bshape
    elif mode == "nt":
        (m, k), (n, k2) = a.shape, bshape
    else:
        (k, m), (k2, n) = a.shape, bshape
    assert k == k2 and a.dtype == BF16 and b.dtype == BF16, (name, a.shape, b.shape, a.dtype, b.dtype)
    tm = tm if tm and m % tm == 0 else _pick(m, (2048, 1024, 512, 256))
    tn = _pick(n, (1152, 1024, 1408, 512, 256))
    tk = tk if tk and k % tk == 0 else _pick(k, (2048, 1408, 1152, 1024, 512, 256))
    if out3 or (b3 and mode == "nn"):
        tn = n // N_DEV
    if b3 and mode == "nt":
        tk = k // N_DEV
    nk = k // tk
    dims = {"nn": (((1,), (0,)), ((), ())), "nt": (((1,), (1,)), ((), ())), "tn": (((0,), (0,)), ((), ()))}[mode]
    n_extra = len(after) + (add is not None)

    def body(a_ref, b_ref, *rest):
        o_ref = rest[n_extra]
        prod = lax.dot_general(a_ref[...], b_ref[...], dims, preferred_element_type=F32)
        if nk == 1:
            o_ref[...] = (prod if add is None else prod + rest[0][...]).astype(o_ref.dtype)
            return
        acc_ref = rest[n_extra + 1]
        kk = pl.program_id(2)

        @pl.when(kk == 0)
        def _():
            acc_ref[...] = prod if add is None else prod + rest[0][...]

        @pl.when((kk > 0) & (kk < nk - 1))
        def _():
            acc_ref[...] += prod

        @pl.when(kk == nk - 1)
        def _():
            o_ref[...] = (acc_ref[...] + prod).astype(o_ref.dtype)

    a_spec = pl.BlockSpec((tk, tm), lambda i, j, kk: (kk, i)) if mode == "tn" else pl.BlockSpec((tm, tk), lambda i, j, kk: (i, kk))
    if b3:
        b_spec = (pl.BlockSpec((None, tn, tk), lambda i, j, kk: (kk, j, 0)) if mode == "nt"
                  else pl.BlockSpec((None, tk, tn), lambda i, j, kk: (j, kk, 0)))
    else:
        b_spec = pl.BlockSpec((tn, tk), lambda i, j, kk: (j, kk)) if mode == "nt" else pl.BlockSpec((tk, tn), lambda i, j, kk: (kk, j))
    tile = pl.BlockSpec((tm, tn), lambda i, j, kk: (i, j))
    out_spec = pl.BlockSpec((None, tm, tn), lambda i, j, kk: (j, i, 0)) if out3 else tile
    return pl.pallas_call(
        body, name=name, grid=(m // tm, n // tn, nk),
        in_specs=[a_spec, b_spec] + ([tile] if add is not None else []) + [ANY] * len(after), out_specs=out_spec,
        out_shape=jax.ShapeDtypeStruct((N_DEV, m, tn) if out3 else (m, n), out_dtype),
        scratch_shapes=[pltpu.VMEM((tm, tn), F32)] if nk > 1 else [],
        compiler_params=_params(("parallel", "parallel", "arbitrary")),
    )(a, b, *([add] if add is not None else []), *after)


def _rows(body, t, tr, ins, outs, name):
    in_specs = []
    for arr, sp in ins:
        if sp[0] == "t":
            in_specs.append(pl.BlockSpec((tr, sp[1]), lambda i, cb=sp[2]: (i, cb)))
        else:
            in_specs.append(pl.BlockSpec(arr.shape, lambda i, nd=arr.ndim: (0,) * nd))
    out_specs, out_shape = [], []
    for shape, dt, kind in outs:
        out_specs.append(pl.BlockSpec((tr, shape[1]), lambda i: (i, 0)) if kind == "t" else pl.BlockSpec(shape, lambda i: (0, 0)))
        out_shape.append(jax.ShapeDtypeStruct(shape, dt))
    return pl.pallas_call(
        body, name=name, grid=(t // tr,), in_specs=in_specs, out_specs=out_specs, out_shape=out_shape,
        compiler_params=_params(("arbitrary",)),
    )(*[arr for arr, _ in ins])


def _rinv(v):
    return lax.rsqrt(jnp.mean(v * v, axis=-1, keepdims=True) + EPS)


def _sig(v):
    return 1.0 / (1.0 + jnp.exp(-v))


def _acc(ref, val):
    @pl.when(pl.program_id(0) == 0)
    def _():
        ref[...] = jnp.zeros_like(ref)

    ref[...] += jnp.sum(val, axis=0, keepdims=True)


def _rms_fwd(x, g, name):
    t, d = x.shape

    def body(x_ref, g_ref, h_ref):
        xv = x_ref[...]
        h_ref[...] = (xv * _rinv(xv) * g_ref[...]).astype(BF16)

    return _rows(body, t, 256, [(x, ("t", d, 0)), (g, ("b",))], [((t, d), BF16, "t")], name)[0]


def _post_pre(x, m, g_post, g_pre, name):
    t, d = x.shape

    def body(x_ref, m_ref, gp_ref, gn_ref, xo_ref, h_ref):
        mv = m_ref[...]
        xn = x_ref[...] + mv * _rinv(mv) * gp_ref[...]
        xo_ref[...] = xn
        h_ref[...] = (xn * _rinv(xn) * gn_ref[...]).astype(BF16)

    return _rows(body, t, 128, [(x, ("t", d, 0)), (m, ("t", d, 0)), (g_post, ("b",)), (g_pre, ("b",))],
                 [((t, d), F32, "t"), ((t, d), BF16, "t")], name)


def _mix_fwd(proj, ya, yb, name):
    t, d = ya.shape

    def body(ga_ref, gb_ref, ya_ref, yb_ref, o_ref):
        o_ref[...] = (_sig(ga_ref[...].astype(F32)) * ya_ref[...].astype(F32)
                      + _sig(gb_ref[...].astype(F32)) * yb_ref[...].astype(F32)).astype(BF16)

    return _rows(body, t, 256, [(proj, ("t", d, C_GA // d)), (proj, ("t", d, C_GB // d)), (ya, ("t", d, 0)), (yb, ("t", d, 0))],
                 [((t, d), BF16, "t")], name)[0]


def _mix_bwd(dmix, proj, ya, yb, name):
    t, d = ya.shape

    def body(dm_ref, ga_ref, gb_ref, ya_ref, yb_ref, dg_ref, dya_ref, dyb_ref):
        dm = dm_ref[...]
        sa, sb = _sig(ga_ref[...].astype(F32)), _sig(gb_ref[...].astype(F32))
        dg_ref[:, :d] = (dm * ya_ref[...].astype(F32) * sa * (1.0 - sa)).astype(BF16)
        dg_ref[:, d:] = (dm * yb_ref[...].astype(F32) * sb * (1.0 - sb)).astype(BF16)
        dya_ref[...] = (dm * sa).astype(BF16)
        dyb_ref[...] = (dm * sb).astype(BF16)

    return _rows(body, t, 128,
                 [(dmix, ("t", d, 0)), (proj, ("t", d, C_GA // d)), (proj, ("t", d, C_GB // d)), (ya, ("t", d, 0)), (yb, ("t", d, 0))],
                 [((t, 2 * d), BF16, "t"), ((t, d), BF16, "t"), ((t, d), BF16, "t")], name)


def _swiglu_call(body, ins, n_out, name):
    t, f = ins[0].shape
    tc = _pick(f, (1408, 512))
    tile = pl.BlockSpec((512, tc), lambda i, j: (i, j))
    return pl.pallas_call(
        body, name=name, grid=(t // 512, f // tc), in_specs=[tile] * len(ins), out_specs=[tile] * n_out,
        out_shape=[jax.ShapeDtypeStruct((t, f), BF16)] * n_out, compiler_params=_params(("parallel", "parallel")),
    )(*ins)


def _swiglu_fwd(fg, fu, name):
    def body(g_ref, u_ref, s_ref):
        gv = g_ref[...].astype(F32)
        s_ref[...] = (gv * _sig(gv) * u_ref[...].astype(F32)).astype(BF16)

    return _swiglu_call(body, [fg, fu], 1, name)[0]


def _swiglu_bwd(ds, fg, fu, name):
    def body(ds_ref, g_ref, u_ref, dg_ref, du_ref):
        dsv, gv, uv = ds_ref[...].astype(F32), g_ref[...].astype(F32), u_ref[...].astype(F32)
        sg = _sig(gv)
        dg_ref[...] = (dsv * uv * sg * (1.0 + gv * (1.0 - sg))).astype(BF16)
        du_ref[...] = (dsv * gv * sg).astype(BF16)

    return _swiglu_call(body, [ds, fg, fu], 2, name)


def _ple_final(x2, pg, pp, tgt, g_post, name):
    t, d = x2.shape

    def body(x_ref, pg_ref, pp_ref, t_ref, g_ref, loss_ref, d3_ref, dpg_ref, dpp_ref, dg_ref):
        sg, ppv, g = _sig(pg_ref[...]), pp_ref[...], g_ref[...]
        e = sg * ppv
        r = _rinv(e)
        eh = e * r
        diff = x_ref[...] + eh * g - t_ref[...]
        loss_ref[...] = 0.5 * jnp.mean(diff * diff, axis=-1, keepdims=True)
        d3 = diff * (1.0 / d)
        d3_ref[...] = d3
        gd = d3 * g
        de = r * (gd - eh * jnp.mean(gd * eh, axis=-1, keepdims=True))
        dpg_ref[...] = (de * ppv * sg * (1.0 - sg)).astype(BF16)
        dpp_ref[...] = (de * sg).astype(BF16)
        _acc(dg_ref, d3 * eh)

    return _rows(body, t, 128, [(x2, ("t", d, 0)), (pg, ("t", d, 0)), (pp, ("t", d, 0)), (tgt, ("t", d, 0)), (g_post, ("b",))],
                 [((t, 1), F32, "t"), ((t, d), F32, "t"), ((t, d), BF16, "t"), ((t, d), BF16, "t"), ((1, d), F32, "a")], name)


def _norm_bwd(dn, dh, x, g_pre, fm, g_post, name):
    t, d = x.shape
    two = fm is not None

    def body(*refs):
        if two:
            dn_ref, dh_ref, x_ref, gp_ref, f_ref, gq_ref, dx_ref, df_ref, dgp_ref, dgq_ref = refs
        else:
            dn_ref, dh_ref, x_ref, gp_ref, dx_ref, dgp_ref = refs
        xv, dhv = x_ref[...], dh_ref[...]
        r = _rinv(xv)
        xh = xv * r
        gd = dhv * gp_ref[...]
        dx = dn_ref[...] + r * (gd - xh * jnp.mean(gd * xh, axis=-1, keepdims=True))
        dx_ref[...] = dx
        _acc(dgp_ref, dhv * xh)
        if two:
            fv = f_ref[...]
            rf = _rinv(fv)
            fh = fv * rf
            gd2 = dx * gq_ref[...]
            df_ref[...] = (rf * (gd2 - fh * jnp.mean(gd2 * fh, axis=-1, keepdims=True))).astype(BF16)
            _acc(dgq_ref, dx * fh)

    ins = [(dn, ("t", d, 0)), (dh, ("t", d, 0)), (x, ("t", d, 0)), (g_pre, ("b",))]
    outs = [((t, d), F32, "t")]
    if two:
        ins += [(fm, ("t", d, 0)), (g_post, ("b",))]
        outs += [((t, d), BF16, "t"), ((1, d), F32, "a"), ((1, d), F32, "a")]
    else:
        outs += [((1, d), F32, "a")]
    return _rows(body, t, 128, ins, outs, name)


CONV_TC = 256


def _shift_down(v, s):
    rows = lax.broadcasted_iota(jnp.int32, v.shape, 0)
    return jnp.where(rows >= s, pltpu.roll(v, s, 0), 0.0)


def _shift_up(v, s):
    n = v.shape[0]
    rows = lax.broadcasted_iota(jnp.int32, v.shape, 0)
    return jnp.where(rows < n - s, pltpu.roll(v, n - s, 0), 0.0)


def _conv_specs(t):
    nb = 1024 // CONV_TC
    seg = lambda c0: pl.BlockSpec((t, CONV_TC), lambda j, cb=c0 // CONV_TC: (0, cb + j))
    own = pl.BlockSpec((t, CONV_TC), lambda j: (0, j))
    wspec = pl.BlockSpec((3, CONV_TC), lambda j: (0, j))
    return nb, seg, own, wspec


def _conv_fwd(proj, conv_w, name, after=()):
    t = proj.shape[0]
    nb, seg, own, wspec = _conv_specs(t)

    def body(ax_ref, ab_ref, ac_ref, w_ref, *rest):
        za_ref = rest[len(after)]
        u = ac_ref[...].astype(F32) * ax_ref[...].astype(F32)
        w = w_ref[...]
        yc = w[0:1] * _shift_down(u, 2) + w[1:2] * _shift_down(u, 1) + w[2:3] * u
        za_ref[...] = (ab_ref[...].astype(F32) * yc).astype(BF16)

    return pl.pallas_call(
        body, name=name, grid=(nb,), in_specs=[seg(C_AX), seg(C_AB), seg(C_AC), wspec] + [ANY] * len(after), out_specs=own,
        out_shape=jax.ShapeDtypeStruct((t, 1024), BF16), compiler_params=_params(("parallel",)),
    )(proj, proj, proj, conv_w, *after)


def _conv_bwd(dza, proj, conv_w, name):
    t = proj.shape[0]
    nb, seg, own, wspec = _conv_specs(t)

    def body(dz_ref, ax_ref, ab_ref, ac_ref, w_ref, dax_ref, dab_ref, dac_ref, dw_ref):
        ax, ab, ac, dz = ax_ref[...].astype(F32), ab_ref[...].astype(F32), ac_ref[...].astype(F32), dz_ref[...].astype(F32)
        w = w_ref[...]
        u = ac * ax
        u1, u2 = _shift_down(u, 1), _shift_down(u, 2)
        yc = w[0:1] * u2 + w[1:2] * u1 + w[2:3] * u
        dab_ref[...] = (dz * yc).astype(BF16)
        dyc = dz * ab
        du = w[2:3] * dyc + w[1:2] * _shift_up(dyc, 1) + w[0:1] * _shift_up(dyc, 2)
        dax_ref[...] = (du * ac).astype(BF16)
        dac_ref[...] = (du * ax).astype(BF16)
        dw_ref[0:1, :] = jnp.sum(dyc * u2, axis=0, keepdims=True)
        dw_ref[1:2, :] = jnp.sum(dyc * u1, axis=0, keepdims=True)
        dw_ref[2:3, :] = jnp.sum(dyc * u, axis=0, keepdims=True)

    act = jax.ShapeDtypeStruct((t, 1024), BF16)
    return pl.pallas_call(
        body, name=name, grid=(nb,), in_specs=[own, seg(C_AX), seg(C_AB), seg(C_AC), wspec], out_specs=[own, own, own, wspec],
        out_shape=[act, act, act, jax.ShapeDtypeStruct((3, 1024), F32)], compiler_params=_params(("parallel",)),
    )(dza, proj, proj, proj, conv_w)


def _dot(a, b, dims, precision=None):
    return lax.dot_general(a, b, (dims, ((), ())), precision=precision, preferred_element_type=F32)


_CONTRACT = {"nn": ((1,), (0,)), "nt": ((1,), (1,)), "tn": ((0,), (0,))}


def _bdot_raw(a, b, mode):
    return _dot(a.astype(BF16), b.astype(BF16), _CONTRACT[mode])


@functools.partial(jax.custom_vjp, nondiff_argnums=(2,))
def _bdot(a, b, mode):
    return _bdot_raw(a, b, mode)


def _bdot_fwd(a, b, mode):
    return _bdot_raw(a, b, mode), (a, b)


def _bdot_bwd(mode, res, ct):
    a, b = res
    if mode == "nn":
        return _bdot_raw(ct, b, "nt"), _bdot_raw(a, ct, "tn")
    if mode == "nt":
        return _bdot_raw(ct, b, "nn"), _bdot_raw(ct, a, "tn")
    return _bdot_raw(b, ct, "nt"), _bdot_raw(a, ct, "nn")


_bdot.defvjp(_bdot_fwd, _bdot_bwd)


@functools.partial(jax.custom_vjp, nondiff_argnums=(2,))
def _sum_dot(ones, x, mode):
    head = x.astype(BF16)
    tail = x - head.astype(F32)
    if mode == "nn":
        return _bdot_raw(ones, head, "nn") + _bdot_raw(ones, tail, "nn")
    return _bdot_raw(head, ones, "tn") + _bdot_raw(tail, ones, "tn")


def _sum_dot_fwd(ones, x, mode):
    return _sum_dot(ones, x, mode), ones


def _sum_dot_bwd(mode, ones, ct):
    return jnp.zeros_like(ones), (_bdot_raw(ones, ct, "tn") if mode == "nn" else _bdot_raw(ones, ct, "nt"))


_sum_dot.defvjp(_sum_dot_fwd, _sum_dot_bwd)


def _gla_chunk(q, k, v, og, alr, s_in, wa, ba, gain):
    c = q.shape[0]
    z =_bdot(alr, wa, "nn") + ba
    la = (jnp.minimum(z, 0.0) - jnp.log(1.0 + jnp.exp(-jnp.abs(z)))) * (1.0 / TAU)
    row = lax.broadcasted_iota(jnp.int32, (c, c), 0)
    col = lax.broadcasted_iota(jnp.int32, (c, c), 1)
    lower = row >= col
    b = _sum_dot(lower.astype(F32), la, "nn")
    trow = lax.broadcasted_iota(jnp.int32, la.shape, 0)
    mid = jnp.sum(jnp.where(trow <= c // 2, la, 0.0), axis=0, keepdims=True)
    blast = jnp.sum(la, axis=0, keepdims=True)
    qs = q * (DK ** -0.5)
    e_up, e_dn = jnp.exp(b - mid), jnp.exp(mid - b)
    a_fwd = _bdot(qs * e_up, k * e_dn, "nt")
    a_rev = _bdot(qs * e_dn, k * e_up, "nt")
    att = jnp.where(lower, a_fwd, a_rev)
    o = _bdot(att, v, "nn") + _bdot(qs * jnp.exp(b), s_in, "nn")
    upd = _bdot(k * jnp.exp(blast - b), v, "tn")
    blast_col = _sum_dot(jnp.ones((c, DV), F32), la, "tn")
    s_out = jnp.exp(blast_col) * s_in + upd
    on = o * _rinv(o) * gain
    return on * og * _sig(og), s_out


def _gla_specs(t, rev):
    n = t // STEP_ROWS
    ch = (lambda i: n - 1 - i) if rev else (lambda i: i)
    col = lambda w, c0: pl.BlockSpec((STEP_ROWS, HEADS * w), lambda i, cb=c0 // (HEADS * w): (ch(i), cb))
    whole = lambda shape: pl.BlockSpec(shape, lambda i, nd=len(shape): (0,) * nd)
    specs = dict(
        q=col(DK, C_Q), k=col(DK, C_K), v=col(DV, C_V), og=col(DV, C_OG),
        alr=pl.BlockSpec((STEP_ROWS, 128), lambda i: (ch(i), C_ALR // 128)),
        wa=whole((128, HEADS * DK)), ba=whole((1, HEADS * DK)), gain=whole((1, DV)),
        state=pl.BlockSpec((STEP_CHUNKS, HEADS, DK, DV), lambda i: (ch(i), 0, 0, 0)),
        odk=pl.BlockSpec((STEP_ROWS, HEADS * DK), lambda i: (ch(i), 0)), odv=pl.BlockSpec((STEP_ROWS, HEADS * DV), lambda i: (ch(i), 0)),
        oalr=pl.BlockSpec((STEP_ROWS, 128), lambda i: (ch(i), 0)), whole=whole,
    )
    return n, specs


def _head_cols(h):
    return slice(h * DK, (h + 1) * DK), slice(h * DV, (h + 1) * DV)


def _gla_fwd(proj, wa, ba, gain, name):
    t = proj.shape[0]
    n, sp = _gla_specs(t, False)

    def body(q_ref, k_ref, v_ref, og_ref, alr_ref, wa_ref, ba_ref, g_ref, zb_ref, st_ref, s_scr):
        @pl.when(pl.program_id(0) == 0)
        def _():
            s_scr[...] = jnp.zeros_like(s_scr)

        state = [s_scr[h] for h in range(HEADS)]
        for c in range(STEP_CHUNKS):
            rows = slice(c * CHUNK, (c + 1) * CHUNK)
            alr = alr_ref[rows, :].astype(F32)
            for h in range(HEADS):
                kc, vc = _head_cols(h)
                st_ref[c, h] = state[h]
                zb, state[h] = _gla_chunk(q_ref[rows, kc].astype(F32), k_ref[rows, kc].astype(F32), v_ref[rows, vc].astype(F32),
                                          og_ref[rows, vc].astype(F32), alr, state[h], wa_ref[:, kc].astype(F32), ba_ref[:, kc], g_ref[...])
                zb_ref[rows, vc] = zb.astype(BF16)
        for h in range(HEADS):
            s_scr[h] = state[h]

    return pl.pallas_call(
        body, name=name, grid=(n,),
        in_specs=[sp["q"], sp["k"], sp["v"], sp["og"], sp["alr"], sp["wa"], sp["ba"], sp["gain"]],
        out_specs=[sp["odv"], sp["state"]],
        out_shape=[jax.ShapeDtypeStruct((t, HEADS * DV), BF16), jax.ShapeDtypeStruct((t // CHUNK, HEADS, DK, DV), F32)],
        scratch_shapes=[pltpu.VMEM((HEADS, DK, DV), F32)],
        compiler_params=_params(("arbitrary",)),
    )(proj, proj, proj, proj, proj, wa, ba, gain)


def _gla_bwd(dzb, proj, states, wa, ba, gain, name):
    t = proj.shape[0]
    n, sp = _gla_specs(t, True)

    def body(dz_ref, q_ref, k_ref, v_ref, og_ref, alr_ref, st_ref, wa_ref, ba_ref, g_ref,
             dq_ref, dk_ref, dv_ref, dog_ref, dalr_ref, dwa_ref, dba_ref, dg_ref, ds_scr):
        @pl.when(pl.program_id(0) == 0)
        def _():
            ds_scr[...] = jnp.zeros_like(ds_scr)
            dwa_ref[...] = jnp.zeros_like(dwa_ref)
            dba_ref[...] = jnp.zeros_like(dba_ref)
            dg_ref[...] = jnp.zeros_like(dg_ref)

        dstate = [ds_scr[h] for h in range(HEADS)]
        dwa_sum, dba_sum, dgain_sum = [None] * HEADS, [None] * HEADS, None
        for c in reversed(range(STEP_CHUNKS)):
            rows = slice(c * CHUNK, (c + 1) * CHUNK)
            alr = alr_ref[rows, :].astype(F32)
            dalr_sum = None
            for h in range(HEADS):
                kc, vc = _head_cols(h)
                args = (q_ref[rows, kc].astype(F32), k_ref[rows, kc].astype(F32), v_ref[rows, vc].astype(F32), og_ref[rows, vc].astype(F32),
                        alr, st_ref[c, h], wa_ref[:, kc].astype(F32), ba_ref[:, kc], g_ref[...])
                _, vjp = jax.vjp(_gla_chunk, *args)
                dq, dk, dv, dog, dalr, dstate[h], dwa, dba, dgain = vjp((dz_ref[rows, vc].astype(F32), dstate[h]))
                dq_ref[rows, kc] = dq.astype(BF16)
                dk_ref[rows, kc] = dk.astype(BF16)
                dv_ref[rows, vc] = dv.astype(BF16)
                dog_ref[rows, vc] = dog.astype(BF16)
                dwa_sum[h] = dwa if dwa_sum[h] is None else dwa_sum[h] + dwa
                dba_sum[h] = dba if dba_sum[h] is None else dba_sum[h] + dba
                dalr_sum = dalr if dalr_sum is None else dalr_sum + dalr
                dgain_sum = dgain if dgain_sum is None else dgain_sum + dgain
            dalr_ref[rows, :] = dalr_sum
        for h in range(HEADS):
            ds_scr[h] = dstate[h]
            dwa_ref[h] += dwa_sum[h]
            dba_ref[h] += dba_sum[h]
        dg_ref[...] += dgain_sum

    whole = sp["whole"]
    return pl.pallas_call(
        body, name=name, grid=(n,),
        in_specs=[sp["odv"], sp["q"], sp["k"], sp["v"], sp["og"], sp["alr"], sp["state"], sp["wa"], sp["ba"], sp["gain"]],
        out_specs=[sp["odk"], sp["odk"], sp["odv"], sp["odv"], sp["oalr"], whole((HEADS, 128, DK)), whole((HEADS, 1, DK)), whole((1, DV))],
        out_shape=[jax.ShapeDtypeStruct((t, HEADS * DK), BF16), jax.ShapeDtypeStruct((t, HEADS * DK), BF16),
                   jax.ShapeDtypeStruct((t, HEADS * DV), BF16), jax.ShapeDtypeStruct((t, HEADS * DV), BF16),
                   jax.ShapeDtypeStruct((t, 128), F32), jax.ShapeDtypeStruct((HEADS, 128, DK), F32),
                   jax.ShapeDtypeStruct((HEADS, 1, DK), F32), jax.ShapeDtypeStruct((1, DV), F32)],
        scratch_shapes=[pltpu.VMEM((HEADS, DK, DV), F32)],
        compiler_params=_params(("arbitrary",)),
    )(dzb, proj, proj, proj, proj, proj, states, wa, ba, gain)


def _local_step(x, p, tgt, gather_start, gather_finish, scatter_start, small):
    b_alpha, gain = small["b_alpha_up"], small["gla_head_gain"]
    gather_start(0, ())
    w = dict(gather_finish(0, ()))
    conv_w, w_alpha = w["conv_w"], w["w_alpha_up"]
    wa_p = jnp.zeros((128, HEADS * DK), BF16).at[:GATE_RANK].set(w_alpha.astype(BF16))

    t1 = gather_start(1, (w["in_t"],))
    h1 = _rms_fwd(x, small["g_pre_mix"], "rms_pre_mix")
    proj = _mm(h1, w["in_t"], "nt", BF16, "mm_proj", after=(t1,))
    t2 = gather_start(2, (proj,))
    za = _conv_fwd(proj, conv_w, "conv_fwd", after=(t2,))
    zb, states = _gla_fwd(proj, wa_p, b_alpha, gain, "gla_fwd")
    t3 = gather_start(3, (zb, za))
    w.update(gather_finish(1, (t3,)))
    ya = _mm(za, w["a_out"], "nn", BF16, "mm_ya", b3=True, tm=2048, tk=1024)
    yb = _mm(zb, w["b_out"], "nn", BF16, "mm_yb", b3=True, tm=2048, tk=1024)
    mix = _mix_fwd(proj, ya, yb, "mix_fwd")
    m2 = _mm(mix, w["mix"], "nn", F32, "mm_mix")
    t4 = gather_start(4, (m2,))
    x1, h2 = _post_pre(x, m2, small["g_post_mix"], small["g_pre_ffn"], "norm_mix_ffn")
    w.update(gather_finish(2, (h2, t4)))
    fg = _mm(h2, w["gate_t"], "nt", BF16, "mm_gate")
    fu = _mm(h2, w["up_t"], "nt", BF16, "mm_up")
    s = _swiglu_fwd(fg, fu, "swiglu_fwd")
    w.update(gather_finish(3, (s,)))
    f = _mm(s, w["down"], "nn", F32, "mm_down")
    x2, h3 = _post_pre(x1, f, small["g_post_ffn"], small["g_pre_ple"], "norm_ffn_ple")
    w.update(gather_finish(4, (h3,)))
    pg = _mm(h3, w["pg"], "nn", F32, "mm_pg")
    p_bf = p.astype(BF16)
    pp = _mm(p_bf, w["pp"], "nn", F32, "mm_pp", b3=True, tm=2048)
    loss_rows, d3, dpg, dpp, dg_post_ple = _ple_final(x2, pg, pp, tgt, small["g_post_ple"], "ple_final")

    gw = {}
    gw["pp"] = _mm(p_bf, dpp, "tn", BF16, "mm_dw_pp", out3=True)
    gw["pg"] = _mm(h3, dpg, "tn", BF16, "mm_dw_pg")
    dh3 = _mm(dpg, w["pg"], "nt", F32, "mm_dh3", after=(scatter_start(0, gw),))
    d2, df, dg_pre_ple, dg_post_ffn = _norm_bwd(d3, dh3, x2, small["g_pre_ple"], f, small["g_post_ffn"], "norm_bwd_ple_ffn")
    ds = _mm(df, w["down"], "nt", BF16, "mm_ds")
    gw["down"] = _mm(s, df, "tn", BF16, "mm_dw_down", tm=1408)
    dfg, dfu = _swiglu_bwd(ds, fg, fu, "swiglu_bwd")
    gw["gate_t"] = _mm(dfg, h2, "tn", BF16, "mm_dw_gate", after=(gw["down"],), tm=1408)
    gw["up_t"] = _mm(dfu, h2, "tn", BF16, "mm_dw_up", after=(gw["gate_t"],), tm=1408)
    dh2 = _mm(dfg, w["gate_t"], "nn", F32, "mm_dh2_gate", after=(scatter_start(1, gw),))
    dh2 = _mm(dfu, w["up_t"], "nn", F32, "mm_dh2_up", add=dh2, tm=1024)
    d1, dm2, dg_pre_ffn, dg_post_mix = _norm_bwd(d2, dh2, x1, small["g_pre_ffn"], m2, small["g_post_mix"], "norm_bwd_ffn_mix")
    dmix = _mm(dm2, w["mix"], "nt", F32, "mm_dmix")
    gw["mix"] = _mm(mix, dm2, "tn", BF16, "mm_dw_mix")
    dgab, dya, dyb = _mix_bwd(dmix, proj, ya, yb, "mix_bwd")
    dza = _mm(dya, w["a_out"], "nt", BF16, "mm_dza", after=(gw["mix"],), b3=True, tm=2048)
    gw["a_out"] = _mm(za, dya, "tn", BF16, "mm_dw_a_out", out3=True, tk=1024)
    gw["b_out"] = _mm(zb, dyb, "tn", BF16, "mm_dw_b_out", after=(gw["a_out"],), out3=True, tk=1024)
    dzb = _mm(dyb, w["b_out"], "nt", BF16, "mm_dzb", after=(scatter_start(2, gw),), b3=True, tm=2048)
    dax, dab, dac, dconv = _conv_bwd(dza, proj, conv_w, "conv_bwd")
    dq, dk, dv, dog, dalr, dwa, dba, dgain = _gla_bwd(dzb, proj, states, wa_p, b_alpha, gain, "gla_bwd")
    dproj = jnp.concatenate([dgab, dax, dab, dac, dq, dk, dv, dog, dalr.astype(BF16)], axis=1)
    gw["in_t"] = _mm(dproj, h1, "tn", BF16, "mm_dw_in", tm=1152)
    dh1 = _mm(dproj, w["in_t"], "nn", F32, "mm_dh1", after=(scatter_start(3, gw),))
    grad_x, dg_pre_mix = _norm_bwd(d1, dh1, x, small["g_pre_mix"], None, None, "norm_bwd_mix")

    gs = dict(
        conv_w=dconv,
        w_alpha_up=jnp.transpose(dwa[:, :GATE_RANK, :], (1, 0, 2)).reshape(GATE_RANK, HEADS * DK),
        b_alpha_up=dba.reshape(1, HEADS * DK), gla_head_gain=dgain,
        g_pre_mix=dg_pre_mix, g_post_mix=dg_post_mix, g_pre_ffn=dg_pre_ffn, g_post_ffn=dg_post_ffn,
        g_pre_ple=dg_pre_ple, g_post_ple=dg_post_ple,
    )
    return loss_rows, grad_x, gs


def _place():
    x, y, c = lax.axis_index("x"), lax.axis_index("y"), lax.axis_index("c")
    return x, y, c, [(1 - x, y), (x, 1 - y), (1 - x, 1 - y)]


def _all_gather(shards, name, cid=None):
    n = len(shards)

    def body(*refs):
        ins, outs = refs[:n], refs[n:2 * n]
        send_sems, recv_sems, local_sems = refs[2 * n:]
        x, y, c, chips = _place()
        me, sibling = (x, y, c), (x, y, 1 - c)

        def slot(px, py, pc):
            return 4 * px + 2 * py + pc

        def copy(a, k, block, to, src=None):
            dst = outs[a].at[slot(*block)]
            return pltpu.make_async_remote_copy(src_ref=dst if src is None else src, dst_ref=dst, send_sem=send_sems.at[a, k],
                                                recv_sem=recv_sems.at[a, k], device_id=to, device_id_type=MESH)

        mine = [pltpu.make_async_copy(ins[a], outs[a].at[slot(*me)], local_sems.at[a]) for a in range(n)]
        for cp in mine:
            cp.start()
        first = []
        for j, chip in enumerate(chips):
            first += [copy(a, 1 + j, me, (*chip, c), src=ins[a]) for a in range(n)]
        first += [copy(a, 0, me, sibling, src=ins[a]) for a in range(n)]
        for cp in first:
            cp.start()
        passed = []
        for j, chip in enumerate(chips):
            for a in range(n):
                copy(a, 1 + j, (*chip, c), me).wait_recv()
                cp = copy(a, 4 + j, (*chip, c), sibling)
                cp.start()
                passed.append(cp)
        for a in range(n):
            copy(a, 0, sibling, me).wait_recv()
        for j, chip in enumerate(chips):
            for a in range(n):
                copy(a, 4 + j, (*chip, 1 - c), me).wait_recv()
        for cp in first + passed:
            cp.wait_send()
        for cp in mine:
            cp.wait()

    if cid is None:
        return pl.pallas_call(
            body, name=name, in_specs=[ANY] * n, out_specs=[ANY] * n,
            out_shape=[jax.ShapeDtypeStruct((N_DEV,) + s.shape, s.dtype) for s in shards],
            scratch_shapes=[pltpu.SemaphoreType.DMA((n, 7)), pltpu.SemaphoreType.DMA((n, 7)), pltpu.SemaphoreType.DMA((n,))],
        )(*shards)

    src = [jax.new_ref(s, memory_space=pltpu.MemorySpace.HBM) for s in shards]
    dst = [jax.empty_ref(jax.ShapeDtypeStruct((N_DEV,) + s.shape, s.dtype), memory_space=pltpu.MemorySpace.HBM) for s in shards]

    @pl.kernel(mesh=plsc.ScalarSubcoreMesh(axis_name="seq", num_cores=1), name=name,
               scratch_types=(pltpu.SemaphoreType.DMA((n, 7)), pltpu.SemaphoreType.DMA((n, 7)), pltpu.SemaphoreType.DMA((n,))),
               compiler_params=pltpu.CompilerParams(collective_id=cid))
    def launch(send_sems, recv_sems, local_sems):
        x, y, c, chips = _place()
        barrier = pltpu.get_barrier_semaphore()
        for peer in [(x, y, 1 - c)] + [(*chip, c) for chip in chips]:
            pl.semaphore_signal(barrier, inc=1, device_id=peer, device_id_type=MESH)
        pl.semaphore_wait(barrier, 4)
        body(*src, *dst, send_sems, recv_sems, local_sems)

    launch()
    return [r[...] for r in dst]


def _reduce_scatter(parts, name, cid):
    n = len(parts)
    src = [jax.new_ref(s, memory_space=pltpu.MemorySpace.HBM) for s in parts]
    dst = [jax.empty_ref(jax.ShapeDtypeStruct(s.shape, s.dtype), memory_space=pltpu.MemorySpace.HBM) for s in parts]

    @pl.kernel(mesh=plsc.ScalarSubcoreMesh(axis_name="seq", num_cores=1), name=name,
               scratch_types=(pltpu.SemaphoreType.DMA((n, N_DEV - 1)), pltpu.SemaphoreType.DMA((n, N_DEV - 1)), pltpu.SemaphoreType.DMA((n,))),
               compiler_params=pltpu.CompilerParams(collective_id=cid))
    def launch(send_sems, recv_sems, local_sems):
        x, y, c, _ = _place()
        me = 4 * x + 2 * y + c
        peers = [(1 - x if k & 4 else x, 1 - y if k & 2 else y, 1 - c if k & 1 else c) for k in range(1, N_DEV)]
        barrier = pltpu.get_barrier_semaphore()
        for peer in peers:
            pl.semaphore_signal(barrier, inc=1, device_id=peer, device_id_type=MESH)
        pl.semaphore_wait(barrier, N_DEV - 1)
        mine = [pltpu.make_async_copy(src[a].at[me], dst[a].at[me], local_sems.at[a]) for a in range(n)]
        for cp in mine:
            cp.start()
        cps = []
        for a in range(n):
            for k, (px, py, pc) in enumerate(peers):
                cps.append(pltpu.make_async_remote_copy(src_ref=src[a].at[4 * px + 2 * py + pc], dst_ref=dst[a].at[me], send_sem=send_sems.at[a, k],
                                                        recv_sem=recv_sems.at[a, k], device_id=(px, py, pc), device_id_type=MESH))
        for cp in cps:
            cp.start()
        for cp in cps:
            cp.wait_recv()
        for cp in cps:
            cp.wait_send()
        for cp in mine:
            cp.wait()

    launch()
    return [r[...] for r in dst]


def _sibling_exchange(parts, name):
    n = len(parts)
    pieces = [_row_pieces(s.shape[1]) for s in parts]

    def body(*refs):
        ins, outs = refs[:n], refs[n:2 * n]
        send_sems, recv_sems = refs[2 * n:]
        x, y, c, _ = _place()

        def copy(a, ch, q, rows):
            return pltpu.make_async_remote_copy(src_ref=ins[a].at[2 * ch + 1 - c, rows], dst_ref=outs[a].at[ch, rows], send_sem=send_sems.at[a, ch, q],
                                                recv_sem=recv_sems.at[a, ch, q], device_id=(x, y, 1 - c), device_id_type=MESH)

        cps = [copy(a, ch, q, rows) for ch in range(4) for a in range(n) for q, rows in enumerate(pieces[a])]
        for cp in cps:
            cp.start()
        for cp in cps:
            cp.wait_recv()
        for cp in cps:
            cp.wait_send()

    return pl.pallas_call(
        body, name=name, in_specs=[ANY] * n, out_specs=[ANY] * n,
        out_shape=[jax.ShapeDtypeStruct((4,) + s.shape[1:], s.dtype) for s in parts],
        scratch_shapes=[pltpu.SemaphoreType.DMA((n, 4, PIECES)), pltpu.SemaphoreType.DMA((n, 4, PIECES))],
    )(*parts)


def _chip_exchange(parts, name):
    n = len(parts)

    def body(*refs):
        ins, outs = refs[:n], refs[n:2 * n]
        send_sems, recv_sems, local_sems = refs[2 * n:]
        x, y, c, chips = _place()
        my_chip = 2 * x + y

        def copy(a, j):
            px, py = chips[j]
            return pltpu.make_async_remote_copy(src_ref=ins[a].at[2 * px + py], dst_ref=outs[a].at[my_chip], send_sem=send_sems.at[a, j],
                                                recv_sem=recv_sems.at[a, j], device_id=(px, py, c), device_id_type=MESH)

        def landing(a, j):
            px, py = chips[j]
            return pltpu.make_async_remote_copy(src_ref=ins[a].at[my_chip], dst_ref=outs[a].at[2 * px + py], send_sem=send_sems.at[a, j],
                                                recv_sem=recv_sems.at[a, j], device_id=(px, py, c), device_id_type=MESH)

        mine = [pltpu.make_async_copy(ins[a].at[my_chip], outs[a].at[my_chip], local_sems.at[a]) for a in range(n)]
        for cp in mine:
            cp.start()
        cps = [copy(a, j) for j in range(3) for a in range(n)]
        for cp in cps:
            cp.start()
        for j in range(3):
            for a in range(n):
                landing(a, j).wait_recv()
        for cp in cps:
            cp.wait_send()
        for cp in mine:
            cp.wait()

    return pl.pallas_call(
        body, name=name, in_specs=[ANY] * n, out_specs=[ANY] * n,
        out_shape=[jax.ShapeDtypeStruct(s.shape, s.dtype) for s in parts],
        scratch_shapes=[pltpu.SemaphoreType.DMA((n, 3)), pltpu.SemaphoreType.DMA((n, 3)), pltpu.SemaphoreType.DMA((n,))],
    )(*parts)


def _pair_add(mine8, got4, name):
    _, r, cols = mine8.shape
    steps, blk, at = _tiles(r, cols)
    core = lax.axis_index("c").astype(jnp.int32).reshape(1)

    def body(c_ref, a_ref, b_ref, o_ref):
        o_ref[...] = (a_ref[...].astype(F32) + b_ref[...].astype(F32)).astype(BF16)

    return pl.pallas_call(
        body, name=name,
        grid_spec=pltpu.PrefetchScalarGridSpec(
            num_scalar_prefetch=1, grid=(4, steps),
            in_specs=[pl.BlockSpec((None,) + blk, lambda ch, i, c_ref: (2 * ch + c_ref[0],) + at(i)),
                      pl.BlockSpec((None,) + blk, lambda ch, i, c_ref: (ch,) + at(i))],
            out_specs=pl.BlockSpec((None,) + blk, lambda ch, i, c_ref: (ch,) + at(i))),
        out_shape=jax.ShapeDtypeStruct((4, r, cols), BF16),
        compiler_params=_params(("parallel", "parallel")),
    )(core, mine8, got4)


HBM = pl.BlockSpec(memory_space=pltpu.HBM)
SEM = pl.BlockSpec(memory_space=pltpu.SEMAPHORE)
EFFECT = pltpu.SideEffectType.DATAFLOW_SIDE_EFFECTING


def _in_hbm(a):
    return pltpu.with_memory_space_constraint(a, pltpu.HBM)


def _remote_copies(plan, srcs, lands, send_sems, recv_sems):
    return [pltpu.make_async_remote_copy(src_ref=s, dst_ref=d, send_sem=send_sems.at[i], recv_sem=recv_sems.at[i], device_id=peer,
                                         device_id_type=MESH) for i, (s, d, peer) in enumerate(plan(srcs, lands))]


def _copies_start(plan, n_copies, srcs, land_shapes, name, after=()):
    ns, nl = len(srcs), len(land_shapes)

    def body(*refs):
        send_sems, recv_sems = refs[ns + nl + len(after):ns + nl + len(after) + 2]
        for cp in _remote_copies(plan, refs[:ns], refs[ns:ns + nl], send_sems, recv_sems):
            cp.start()
        refs[-1][...] = jnp.zeros((8, 128), F32)

    sems = pltpu.SemaphoreType.DMA((n_copies,))
    return pl.pallas_call(
        body, name=name,
        out_shape=(sems, sems, *[pltpu.HBM(s.shape, s.dtype) for s in srcs], *[pltpu.HBM(s.shape, s.dtype) for s in land_shapes],
                   jax.ShapeDtypeStruct((8, 128), F32)),
        in_specs=[HBM] * (ns + nl) + [ANY] * len(after),
        out_specs=(SEM, SEM, *[HBM] * (ns + nl), pl.BlockSpec(memory_space=pltpu.VMEM)),
        input_output_aliases={i: 2 + i for i in range(ns + nl)},
        compiler_params=pltpu.CompilerParams(has_side_effects=EFFECT),
    )(*[_in_hbm(s) for s in srcs], *[_in_hbm(lax.empty(s.shape, s.dtype)) for s in land_shapes], *after)


def _copies_wait(plan, state, ns, name, after=()):
    send_sems, recv_sems, *arrs = state[:-1]
    n = len(arrs)

    def body(*refs):
        cps = _remote_copies(plan, refs[:ns], refs[ns:n], refs[n], refs[n + 1])
        for cp in cps:
            cp.wait_send()
        for cp in cps:
            cp.wait_recv()

    out = pl.pallas_call(
        body, name=name, out_shape=tuple(pltpu.HBM(a.shape, a.dtype) for a in arrs),
        in_specs=[HBM] * n + [SEM, SEM] + [ANY] * len(after), out_specs=tuple([HBM] * n),
        input_output_aliases={i: i for i in range(n)},
        compiler_params=pltpu.CompilerParams(has_side_effects=EFFECT),
    )(*arrs, send_sems, recv_sems, *after)
    return list(out[:ns]), list(out[ns:])


def _gather_plan(srcs, lands):
    x, y, c, chips = _place()
    peers = [(x, y, 1 - c)] + [(*chip, c) for chip in chips]
    return [(s, l.at[4 * x + 2 * y + c], peer) for s, l in zip(srcs, lands) for peer in peers]


def _scatter_plan(srcs, lands):
    x, y, c, _ = _place()
    peers = [(1 - x if k & 4 else x, 1 - y if k & 2 else y, 1 - c if k & 1 else c) for k in range(1, N_DEV)]
    return [(s.at[4 * px + 2 * py + pc], l.at[4 * x + 2 * y + c], (px, py, pc)) for s, l in zip(srcs, lands) for px, py, pc in peers]


def _everyone_plan(srcs, lands):
    x, y, c, _ = _place()
    peers = [(1 - x if k & 4 else x, 1 - y if k & 2 else y, 1 - c if k & 1 else c) for k in range(1, N_DEV)]
    return [(s, l.at[4 * x + 2 * y + c], peer) for s, l in zip(srcs, lands) for peer in peers]


def _sum_parts(got, own, me, name):
    def body(me_ref, got_ref, own_ref, o_ref):
        acc = jnp.where(me_ref[0] == 0, own_ref[...], got_ref[0])
        for d in range(1, N_DEV):
            acc = acc + jnp.where(me_ref[0] == d, own_ref[...], got_ref[d])
        o_ref[...] = acc

    return pl.pallas_call(
        body, name=name,
        grid_spec=pltpu.PrefetchScalarGridSpec(
            num_scalar_prefetch=1, grid=(1,),
            in_specs=[pl.BlockSpec(got.shape, lambda i, me_ref: (0, 0, 0)), pl.BlockSpec(own.shape, lambda i, me_ref: (0, 0))],
            out_specs=pl.BlockSpec(own.shape, lambda i, me_ref: (0, 0))),
        out_shape=jax.ShapeDtypeStruct(own.shape, F32),
    )(me.astype(jnp.int32).reshape(1), got, own)


def _chip_plan(srcs, lands):
    x, y, c, chips = _place()
    return [(s.at[2 * px + py], l.at[2 * x + y], (px, py, c)) for s, l in zip(srcs, lands) for px, py in chips]


PIECES = 8


def _row_pieces(rows):
    for k in (PIECES, 4, 2):
        if rows % (16 * k) == 0:
            return [pl.ds(q * (rows // k), rows // k) for q in range(k)]
    return [pl.ds(0, rows)]


def _put_own(shard, zone, me, name):
    r, c = shard.shape
    tr = r if r <= 256 else _pick(r, (256, 64))

    def body(me_ref, s_ref, z_ref, o_ref):
        o_ref[...] = s_ref[...]

    return pl.pallas_call(
        body, name=name,
        grid_spec=pltpu.PrefetchScalarGridSpec(
            num_scalar_prefetch=1, grid=(r // tr,),
            in_specs=[pl.BlockSpec((tr, c), lambda i, me_ref: (i, 0)), ANY],
            out_specs=pl.BlockSpec((None, tr, c), lambda i, me_ref: (me_ref[0], i, 0))),
        out_shape=jax.ShapeDtypeStruct(zone.shape, zone.dtype), input_output_aliases={2: 0},
        compiler_params=_params(("arbitrary",)),
    )(me.astype(jnp.int32).reshape(1), shard, zone)


def _gather_finish(lands, name):
    n = len(lands)
    pieces = [_row_pieces(s.shape[1]) for s in lands]

    def body(*refs):
        zones, outs = refs[:n], refs[n:2 * n]
        send_sems, recv_sems = refs[2 * n:]
        x, y, c, chips = _place()
        cps = []
        for j, (px, py) in enumerate(chips):
            for a in range(n):
                for q, rows in enumerate(pieces[a]):
                    cps.append(pltpu.make_async_remote_copy(
                        src_ref=zones[a].at[4 * px + 2 * py + c, rows], dst_ref=outs[a].at[4 * px + 2 * py + c, rows],
                        send_sem=send_sems.at[a, j, q], recv_sem=recv_sems.at[a, j, q], device_id=(x, y, 1 - c), device_id_type=MESH))
        for cp in cps:
            cp.start()
        for cp in cps:
            cp.wait_recv()
        for cp in cps:
            cp.wait_send()

    return pl.pallas_call(
        body, name=name, in_specs=[ANY] * n, out_specs=[ANY] * n,
        out_shape=[jax.ShapeDtypeStruct(l.shape, l.dtype) for l in lands],
        input_output_aliases={a: a for a in range(n)},
        scratch_shapes=[pltpu.SemaphoreType.DMA((n, 3, PIECES)), pltpu.SemaphoreType.DMA((n, 3, PIECES))],
    )(*lands)


def _sum_everywhere(v, name):
    rows = v.shape[0]

    def body(v_ref, o_ref, buf, send_sems, recv_sems):
        x, y, c, _ = _place()
        me = 4 * x + 2 * y + c
        buf[me] = v_ref[...]
        cps = []
        for k in range(1, N_DEV):
            fx, fy, fc = (k >> 2) & 1, (k >> 1) & 1, k & 1
            to = (1 - x if fx else x, 1 - y if fy else y, 1 - c if fc else c)
            cps.append(pltpu.make_async_remote_copy(src_ref=buf.at[me], dst_ref=buf.at[me], send_sem=send_sems.at[k - 1],
                                                    recv_sem=recv_sems.at[k - 1], device_id=to, device_id_type=MESH))
        for cp in cps:
            cp.start()
        for cp in cps:
            cp.wait_recv()
        for cp in cps:
            cp.wait_send()
        acc = buf[0]
        for d in range(1, N_DEV):
            acc = acc + buf[d]
        o_ref[...] = acc

    vm = pl.BlockSpec(memory_space=pltpu.VMEM)
    return pl.pallas_call(
        body, name=name, in_specs=[vm], out_specs=vm, out_shape=jax.ShapeDtypeStruct(v.shape, F32),
        scratch_shapes=[pltpu.VMEM((N_DEV, rows, 128), F32), pltpu.SemaphoreType.DMA((N_DEV - 1,)), pltpu.SemaphoreType.DMA((N_DEV - 1,))],
    )(v)


def _adamw_parts(w, got, mine, me, m, v, name, after=()):
    r, c = w.shape
    n_parts = got.shape[0]
    steps, blk, at = _tiles(r, c)

    def body(me_ref, w_ref, got_ref, own_ref, m_ref, v_ref, *rest):
        go_ref, d_ref, mo_ref, vo_ref = rest[len(after):]
        own = own_ref[...].astype(F32)
        gv = jnp.where(me_ref[0] == 0, own, got_ref[0].astype(F32))
        for d in range(1, n_parts):
            gv = gv + jnp.where(me_ref[0] == d, own, got_ref[d].astype(F32))
        _adamw_math(gv, w_ref, m_ref, v_ref, go_ref, d_ref, mo_ref, vo_ref)

    tile = pl.BlockSpec(blk, lambda i, me_ref: at(i))
    out = jax.ShapeDtypeStruct((r, c), F32)
    return pl.pallas_call(
        body, name=name,
        grid_spec=pltpu.PrefetchScalarGridSpec(
            num_scalar_prefetch=1, grid=(steps,),
            in_specs=[tile, pl.BlockSpec((n_parts,) + blk, lambda i, me_ref: (0,) + at(i)),
                      pl.BlockSpec((None,) + blk, lambda i, me_ref: (me_ref[0],) + at(i)), tile, tile] + [ANY] * len(after),
            out_specs=[tile] * 4),
        out_shape=[out] * 4, compiler_params=_params(("parallel",)),
    )(me.astype(jnp.int32).reshape(1), w, got, mine, m, v, *after)


def _adamw_math(gv, w_ref, m_ref, v_ref, go_ref, d_ref, mo_ref, vo_ref):
    mn = B1 * m_ref[...] + (1.0 - B1) * gv
    vn = B2 * v_ref[...] + (1.0 - B2) * (gv * gv)
    m_hat = mn / (1.0 - B1 ** STEP)
    v_hat = vn / (1.0 - B2 ** STEP)
    go_ref[...] = gv
    d_ref[...] = -LR * (m_hat / (jnp.sqrt(v_hat) + ADAM_EPS) + WD * w_ref[...])
    mo_ref[...] = mn
    vo_ref[...] = vn


def _adamw(w, g, m, v, name):
    r, c = w.shape
    parts = g.ndim == 3
    tr = r if r <= 128 else _pick(r, (128, 64))

    def body(w_ref, g_ref, m_ref, v_ref, go_ref, d_ref, mo_ref, vo_ref):
        if parts:
            gv = g_ref[0].astype(F32)
            for d in range(1, g.shape[0]):
                gv = gv + g_ref[d].astype(F32)
        else:
            gv = g_ref[...]
        mn = B1 * m_ref[...] + (1.0 - B1) * gv
        vn = B2 * v_ref[...] + (1.0 - B2) * (gv * gv)
        m_hat = mn / (1.0 - B1 ** STEP)
        v_hat = vn / (1.0 - B2 ** STEP)
        go_ref[...] = gv
        d_ref[...] = -LR * (m_hat / (jnp.sqrt(v_hat) + ADAM_EPS) + WD * w_ref[...])
        mo_ref[...] = mn
        vo_ref[...] = vn

    tile = pl.BlockSpec((tr, c), lambda i: (i, 0))
    g_spec = pl.BlockSpec((g.shape[0], tr, c), lambda i: (0, i, 0)) if parts else tile
    out = jax.ShapeDtypeStruct((r, c), F32)
    return pl.pallas_call(
        body, name=name, grid=(r // tr,), in_specs=[tile, g_spec, tile, tile], out_specs=[tile] * 4, out_shape=[out] * 4,
        compiler_params=_params(("parallel",)),
    )(w, g, m, v)


BIG = ["w_in", "w_a_out", "w_b_out", "w_mix_out", "w_ff_gate", "w_ff_up", "w_ff_down", "w_ple_gate", "w_ple_proj"]
TRANSPOSED = ["w_in", "w_ff_gate", "w_ff_up"]
SMALL = ["conv_w", "w_alpha_up", "b_alpha_up", "gla_head_gain", "g_pre_mix", "g_post_mix", "g_pre_ffn", "g_post_ffn", "g_pre_ple", "g_post_ple"]
WEIGHTS = ["w_in", "conv_w", "w_a_out", "w_alpha_up", "b_alpha_up", "gla_head_gain", "w_b_out", "w_mix_out", "g_pre_mix", "g_post_mix",
           "g_pre_ffn", "g_post_ffn", "w_ff_gate", "w_ff_up", "w_ff_down", "g_pre_ple", "g_post_ple", "w_ple_gate", "w_ple_proj"]


def _in_t_from_blocks(z):
    w = z.reshape(-1, z.shape[-1])
    return jnp.concatenate([w[R_GA:R_END], w[:R_ALR], w[R_ALR:R_GA], jnp.zeros((128 - GATE_RANK, w.shape[1]), w.dtype)], axis=0)


def _blocks_from_in_t(g):
    per = R_END // N_DEV

    def rows(lo, hi):
        out = []
        for n0, n1, p0 in ((0, R_ALR, C_AX), (R_ALR, R_GA, C_ALR), (R_GA, R_END, 0)):
            a, e = max(lo, n0), min(hi, n1)
            if a < e:
                out.append(g[p0 + a - n0:p0 + e - n0])
        return out

    return jnp.stack([jnp.concatenate(rows(b * per, (b + 1) * per), axis=0) for b in range(N_DEV)])


def _cols_to_full(g8):
    n, r, c = g8.shape
    return jnp.transpose(g8, (1, 0, 2)).reshape(r, n * c)


def _full_to_cols(a):
    r, c = a.shape
    return jnp.transpose(a.reshape(r, N_DEV, c // N_DEV), (1, 0, 2))


def _pack(arrs, rows):
    flat = jnp.concatenate([a.reshape(-1) for a in arrs])
    return jnp.pad(flat, (0, rows * 128 - flat.shape[0])).reshape(rows, 128)


def _unpack(packed, shapes):
    flat, out, o = packed.reshape(-1), [], 0
    for s in shapes:
        size = 1
        for d in s:
            size *= d
        out.append(flat[o:o + size].reshape(s))
        o += size
    return out


def kernel(x, p, w_in, conv_w, w_a_out, w_alpha_up, b_alpha_up, gla_head_gain, w_b_out, w_mix_out, g_pre_mix, g_post_mix, g_pre_ffn, g_post_ffn, w_ff_gate, w_ff_up, w_ff_down, g_pre_ple, g_post_ple, w_ple_gate, w_ple_proj, loss_target, m_w_in, m_conv_w, m_w_a_out, m_w_alpha_up, m_b_alpha_up, m_gla_head_gain, m_w_b_out, m_w_mix_out, m_g_pre_mix, m_g_post_mix, m_g_pre_ffn, m_g_post_ffn, m_w_ff_gate, m_w_ff_up, m_w_ff_down, m_g_pre_ple, m_g_post_ple, m_w_ple_gate, m_w_ple_proj, v_w_in, v_conv_w, v_w_a_out, v_w_alpha_up, v_b_alpha_up, v_gla_head_gain, v_w_b_out, v_w_mix_out, v_g_pre_mix, v_g_post_mix, v_g_pre_ffn, v_g_post_ffn, v_w_ff_gate, v_w_ff_up, v_w_ff_down, v_g_pre_ple, v_g_post_ple, v_w_ple_gate, v_w_ple_proj):
    args = dict(locals())
    shard = lambda n, a: jnp.transpose(a[0]) if n in TRANSPOSED else a[0]
    wts = {n: shard(n, args[n]) for n in WEIGHTS}
    mom = {n: shard(n, args["m_" + n]) for n in WEIGHTS}
    var = {n: shard(n, args["v_" + n]) for n in WEIGHTS}
    me =4 * lax.axis_index("x") + 2 * lax.axis_index("y") + lax.axis_index("c")

    groups = [["w_in", "conv_w", "w_alpha_up"], ["w_a_out", "w_b_out", "w_mix_out"], ["w_ff_gate", "w_ff_up"], ["w_ff_down"],
              ["w_ple_gate", "w_ple_proj"]]
    grad_groups = [["w_ple_proj", "w_ple_gate"], ["w_ff_down", "w_ff_gate", "w_ff_up"], ["w_mix_out", "w_a_out", "w_b_out"], ["w_in"]]
    rows_full = lambda g: g.reshape(-1, g.shape[-1])
    gathers, scatters = {}, {}

    def gather_start(gi, after):
        shards = [wts[n].astype(BF16) if n in BIG else wts[n] for n in groups[gi]]
        zones = [jax.ShapeDtypeStruct((N_DEV,) + s.shape, s.dtype) for s in shards]
        gathers[gi] = (shards, _copies_start(_gather_plan, 4 * len(shards), shards, zones, "gather_start_%d" % gi, after))
        return gathers[gi][1][-1]

    def gather_finish(gi, after):
        shards, state = gathers[gi]
        shards, zones = _copies_wait(_gather_plan, state, len(shards), "gather_wait_%d" % gi, after)
        zones = _gather_finish(zones, "gather_finish_%d" % gi)
        g8 = {n: _put_own(s, z, me, "gather_own_" + n) for n, s, z in zip(groups[gi], shards, zones)}
        if gi == 0:
            return dict(in_t=_in_t_from_blocks(g8["w_in"]),
                        conv_w=_cols_to_full(g8["conv_w"]), w_alpha_up=_cols_to_full(g8["w_alpha_up"]))
        if gi == 1:
            return dict(a_out=g8["w_a_out"], b_out=g8["w_b_out"], mix=rows_full(g8["w_mix_out"]))
        if gi == 2:
            return dict(gate_t=rows_full(g8["w_ff_gate"]), up_t=rows_full(g8["w_ff_up"]))
        if gi == 3:
            return dict(down=rows_full(g8["w_ff_down"]))
        return dict(pg=rows_full(g8["w_ple_gate"]), pp=g8["w_ple_proj"])

    def scatter_start(gi, gw):
        if gi == 3:
            full = dict(w_in=_blocks_from_in_t(gw["in_t"]))
        elif gi == 1:
            full = dict(w_ff_down=gw["down"], w_ff_gate=gw["gate_t"], w_ff_up=gw["up_t"])
        elif gi == 2:
            full = dict(w_mix_out=gw["mix"], w_a_out=gw["a_out"], w_b_out=gw["b_out"])
        else:
            full = dict(w_ple_proj=gw["pp"], w_ple_gate=gw["pg"])
        parts = [full[n] if full[n].ndim == 3 else full[n].reshape(N_DEV, -1, full[n].shape[-1]) for n in grad_groups[gi]]
        if gi == 3:
            from_sibling = _sibling_exchange(parts, "scatter_sibling_%d" % gi)
            parts = [_pair_add(a, b, "scatter_add_%d_%s" % (gi, n)) for n, a, b in zip(grad_groups[gi], parts, from_sibling)]
            scatters[gi] = _copies_start(_chip_plan, 3 * len(parts), parts, parts, "scatter_start_%d" % gi)
        else:
            scatters[gi] = _copies_start(_scatter_plan, (N_DEV - 1) * len(parts), parts, parts, "scatter_start_%d" % gi)
        return scatters[gi][-1]

    small = {n: wts[n].reshape(1, -1) for n in SMALL[2:]}

    loss_rows, grad_x, gs = _local_step(x[0], p[0, 0], loss_target[0], gather_start, gather_finish, scatter_start, small)
    gs["loss"] = jnp.sum(loss_rows).reshape(1, 1)

    small_shapes = [gs[n].shape for n in SMALL]
    gs_packed = _pack([gs[n] for n in SMALL + ["loss"]], 192)
    small_state = _copies_start(_everyone_plan, N_DEV - 1, [gs_packed], [jax.ShapeDtypeStruct((N_DEV,) + gs_packed.shape, F32)],
                                "small_start", (grad_x,))

    res, done = {}, (small_state[-1],)
    for gi, names in enumerate(grad_groups):
        plan, slot = (_chip_plan, me // 2) if gi == 3 else (_scatter_plan, me)
        mine, got = _copies_wait(plan, scatters[gi], len(names), "scatter_wait_%d" % gi, done)
        for n, g, own in zip(names, got, mine):
            res[n] = _adamw_parts(wts[n], g, own, slot, mom[n], var[n], "adamw_" + n)
        done = tuple(res[n][1] for n in names)

    (gs_own,), (gs_got,) = _copies_wait(_everyone_plan, small_state, 1, "small_wait", done)
    gsum = dict(zip(SMALL + ["loss"], _unpack(_sum_parts(gs_got, gs_own, me, "small_sum"), small_shapes + [(1, 1)])))
    loss = gsum["loss"].reshape(())
    gsum["conv_w"] = lax.dynamic_index_in_dim(gsum["conv_w"].reshape(3, N_DEV, -1), me, axis=1, keepdims=False)
    gsum["w_alpha_up"] = lax.dynamic_index_in_dim(gsum["w_alpha_up"].reshape(GATE_RANK, N_DEV, -1), me, axis=1, keepdims=False)

    shard_shapes = [wts[n].shape for n in SMALL]
    packed = [_pack([d[n] for n in SMALL], 120) for d in (wts, gsum, mom, var)]
    outs = [_unpack(o, shard_shapes) for o in _adamw(*packed, "adamw_small")]
    for i, n in enumerate(SMALL):
        res[n] = [o[i] for o in outs]

    back = lambda n, a: (jnp.transpose(a) if n in TRANSPOSED else a)[None]
    return (loss, grad_x[None], *[back(n, res[n][i]) for i in range(4) for n in WEIGHTS])
```

```python
import functools

import jax
import jax.numpy as jnp
from jax import lax
from jax.experimental import pallas as pl
from jax.experimental.pallas import tpu as pltpu
from jax.experimental.pallas import tpu_sc as plsc

F32, BF16 = jnp.float32, jnp.bfloat16
EPS = 1e-6
CHUNK = 64
STEP_CHUNKS = 2
STEP_ROWS = STEP_CHUNKS * CHUNK
HEADS, DK, DV = 4, 128, 256
GATE_RANK = 16
TAU = 16.0
LR, B1, B2, ADAM_EPS, WD, STEP = 0.001, 0.9, 0.999, 1e-08, 0.01, 10
N_DEV = 8
MESH = pl.DeviceIdType.MESH
VMEM_LIMIT = 56 * 1024 * 1024
ANY = pl.BlockSpec(memory_space=pl.ANY)

C_GA, C_GB, C_AX, C_AB, C_AC, C_Q, C_K, C_V, C_OG, C_ALR = 0, 2048, 4096, 5120, 6144, 7168, 7680, 8192, 9216, 10240
IN_PAD = 10368
R_AX, R_AB, R_AC, R_Q, R_K, R_V, R_OG, R_ALR, R_GA, R_GB, R_END = 0, 1024, 2048, 3072, 3584, 4096, 5120, 6144, 6160, 8208, 10256


def _params(sem):
    return pltpu.CompilerParams(dimension_semantics=sem, vmem_limit_bytes=VMEM_LIMIT)


def _pick(n, cands):
    for c in cands:
        if n % c == 0:
            return c
    return n


def _tiles(r, c):
    for tr in (128, 64):
        if r % tr == 0:
            return r // tr, (tr, c), lambda i: (i, 0)
    tc = _pick(c, (256, 128))
    return c // tc, (r, tc), lambda i: (0, i)


def _mm(a, b, mode, out_dtype, name, after=(), add=None, b3=False, out3=False, tm=None, tk=None):
    bshape = (b.shape[1], N_DEV * b.shape[2]) if b3 else b.shape
    if mode == "nn":
        (m, k), (k2, n) = a.shape, bshape
    elif mode == "nt":
        (m, k), (n, k2) = a.shape, bshape
    else:
        (k, m), (k2, n) = a.shape, bshape
    assert k == k2 and a.dtype == BF16 and b.dtype == BF16, (name, a.shape, b.shape, a.dtype, b.dtype)
    tm = tm if tm and m % tm == 0 else _pick(m, (2048, 1024, 512, 256))
    tn = _pick(n, (1152, 1024, 1408, 512, 256))
    tk = tk if tk and k % tk == 0 else _pick(k, (2048, 1408, 1152, 1024, 512, 256))
    if out3 or (b3 and mode == "nn"):
        tn = n // N_DEV
    if b3 and mode == "nt":
        tk = k // N_DEV
    nk = k // tk
    dims = {"nn": (((1,), (0,)), ((), ())), "nt": (((1,), (1,)), ((), ())), "tn": (((0,), (0,)), ((), ()))}[mode]
    n_extra = len(after) + (add is not None)

    def body(a_ref, b_ref, *rest):
        o_ref = rest[n_extra]
        prod = lax.dot_general(a_ref[...], b_ref[...], dims, preferred_element_type=F32)
        if nk == 1:
            o_ref[...] = (prod if add is None else prod + rest[0][...]).astype(o_ref.dtype)
            return
        acc_ref = rest[n_extra + 1]
        kk = pl.program_id(2)

        @pl.when(kk == 0)
        def _():
            acc_ref[...] = prod if add is None else prod + rest[0][...]

        @pl.when((kk > 0) & (kk < nk - 1))
        def _():
            acc_ref[...] += prod

        @pl.when(kk == nk - 1)
        def _():
            o_ref[...] = (acc_ref[...] + prod).astype(o_ref.dtype)

    a_spec = pl.BlockSpec((tk, tm), lambda i, j, kk: (kk, i)) if mode == "tn" else pl.BlockSpec((tm, tk), lambda i, j, kk: (i, kk))
    if b3:
        b_spec = (pl.BlockSpec((None, tn, tk), lambda i, j, kk: (kk, j, 0)) if mode == "nt"
                  else pl.BlockSpec((None, tk, tn), lambda i, j, kk: (j, kk, 0)))
    else:
        b_spec = pl.BlockSpec((tn, tk), lambda i, j, kk: (j, kk)) if mode == "nt" else pl.BlockSpec((tk, tn), lambda i, j, kk: (kk, j))
    tile = pl.BlockSpec((tm, tn), lambda i, j, kk: (i, j))
    out_spec = pl.BlockSpec((None, tm, tn), lambda i, j, kk: (j, i, 0)) if out3 else tile
    return pl.pallas_call(
        body, name=name, grid=(m // tm, n // tn, nk),
        in_specs=[a_spec, b_spec] + ([tile] if add is not None else []) + [ANY] * len(after), out_specs=out_spec,
        out_shape=jax.ShapeDtypeStruct((N_DEV, m, tn) if out3 else (m, n), out_dtype),
        scratch_shapes=[pltpu.VMEM((tm, tn), F32)] if nk > 1 else [],
        compiler_params=_params(("parallel", "parallel", "arbitrary")),
    )(a, b, *([add] if add is not None else []), *after)


def _rows(body, t, tr, ins, outs, name):
    in_specs = []
    for arr, sp in ins:
        if sp[0] == "t":
            in_specs.append(pl.BlockSpec((tr, sp[1]), lambda i, cb=sp[2]: (i, cb)))
        else:
            in_specs.append(pl.BlockSpec(arr.shape, lambda i, nd=arr.ndim: (0,) * nd))
    out_specs, out_shape = [], []
    for shape, dt, kind in outs:
        out_specs.append(pl.BlockSpec((tr, shape[1]), lambda i: (i, 0)) if kind == "t" else pl.BlockSpec(shape, lambda i: (0, 0)))
        out_shape.append(jax.ShapeDtypeStruct(shape, dt))
    return pl.pallas_call(
        body, name=name, grid=(t // tr,), in_specs=in_specs, out_specs=out_specs, out_shape=out_shape,
        compiler_params=_params(("arbitrary",)),
    )(*[arr for arr, _ in ins])


def _rinv(v):
    return lax.rsqrt(jnp.mean(v * v, axis=-1, keepdims=True) + EPS)


def _sig(v):
    return 1.0 / (1.0 + jnp.exp(-v))


def _acc(ref, val):
    @pl.when(pl.program_id(0) == 0)
    def _():
        ref[...] = jnp.zeros_like(ref)

    ref[...] += jnp.sum(val, axis=0, keepdims=True)


def _rms_fwd(x, g, name):
    t, d = x.shape

    def body(x_ref, g_ref, h_ref):
        xv = x_ref[...]
        h_ref[...] = (xv * _rinv(xv) * g_ref[...]).astype(BF16)

    return _rows(body, t, 256, [(x, ("t", d, 0)), (g, ("b",))], [((t, d), BF16, "t")], name)[0]


def _post_pre(x, m, g_post, g_pre, name):
    t, d = x.shape

    def body(x_ref, m_ref, gp_ref, gn_ref, xo_ref, h_ref):
        mv = m_ref[...]
        xn = x_ref[...] + mv * _rinv(mv) * gp_ref[...]
        xo_ref[...] = xn
        h_ref[...] = (xn * _rinv(xn) * gn_ref[...]).astype(BF16)

    return _rows(body, t, 128, [(x, ("t", d, 0)), (m, ("t", d, 0)), (g_post, ("b",)), (g_pre, ("b",))],
                 [((t, d), F32, "t"), ((t, d), BF16, "t")], name)


def _mix_fwd(proj, ya, yb, name):
    t, d = ya.shape

    def body(ga_ref, gb_ref, ya_ref, yb_ref, o_ref):
        o_ref[...] = (_sig(ga_ref[...].astype(F32)) * ya_ref[...].astype(F32)
                      + _sig(gb_ref[...].astype(F32)) * yb_ref[...].astype(F32)).astype(BF16)

    return _rows(body, t, 256, [(proj, ("t", d, C_GA // d)), (proj, ("t", d, C_GB // d)), (ya, ("t", d, 0)), (yb, ("t", d, 0))],
                 [((t, d), BF16, "t")], name)[0]


def _mix_bwd(dmix, proj, ya, yb, name):
    t, d = ya.shape

    def body(dm_ref, ga_ref, gb_ref, ya_ref, yb_ref, dg_ref, dya_ref, dyb_ref):
        dm = dm_ref[...]
        sa, sb = _sig(ga_ref[...].astype(F32)), _sig(gb_ref[...].astype(F32))
        dg_ref[:, :d] = (dm * ya_ref[...].astype(F32) * sa * (1.0 - sa)).astype(BF16)
        dg_ref[:, d:] = (dm * yb_ref[...].astype(F32) * sb * (1.0 - sb)).astype(BF16)
        dya_ref[...] = (dm * sa).astype(BF16)
        dyb_ref[...] = (dm * sb).astype(BF16)

    return _rows(body, t, 128,
                 [(dmix, ("t", d, 0)), (proj, ("t", d, C_GA // d)), (proj, ("t", d, C_GB // d)), (ya, ("t", d, 0)), (yb, ("t", d, 0))],
                 [((t, 2 * d), BF16, "t"), ((t, d), BF16, "t"), ((t, d), BF16, "t")], name)


def _swiglu_call(body, ins, n_out, name):
    t, f = ins[0].shape
    tc = _pick(f, (1408, 512))
    tile = pl.BlockSpec((512, tc), lambda i, j: (i, j))
    return pl.pallas_call(
        body, name=name, grid=(t // 512, f // tc), in_specs=[tile] * len(ins), out_specs=[tile] * n_out,
        out_shape=[jax.ShapeDtypeStruct((t, f), BF16)] * n_out, compiler_params=_params(("parallel", "parallel")),
    )(*ins)


def _swiglu_fwd(fg, fu, name):
    def body(g_ref, u_ref, s_ref):
        gv = g_ref[...].astype(F32)
        s_ref[...] = (gv * _sig(gv) * u_ref[...].astype(F32)).astype(BF16)

    return _swiglu_call(body, [fg, fu], 1, name)[0]


def _swiglu_bwd(ds, fg, fu, name):
    def body(ds_ref, g_ref, u_ref, dg_ref, du_ref):
        dsv, gv, uv = ds_ref[...].astype(F32), g_ref[...].astype(F32), u_ref[...].astype(F32)
        sg = _sig(gv)
        dg_ref[...] = (dsv * uv * sg * (1.0 + gv * (1.0 - sg))).astype(BF16)
        du_ref[...] = (dsv * gv * sg).astype(BF16)

    return _swiglu_call(body, [ds, fg, fu], 2, name)


def _ple_final(x2, pg, pp, tgt, g_post, name):
    t, d = x2.shape

    def body(x_ref, pg_ref, pp_ref, t_ref, g_ref, loss_ref, d3_ref, dpg_ref, dpp_ref, dg_ref):
        sg, ppv, g = _sig(pg_ref[...]), pp_ref[...], g_ref[...]
        e = sg * ppv
        r = _rinv(e)
        eh = e * r
        diff = x_ref[...] + eh * g - t_ref[...]
        loss_ref[...] = 0.5 * jnp.mean(diff * diff, axis=-1, keepdims=True)
        d3 = diff * (1.0 / d)
        d3_ref[...] = d3
        gd = d3 * g
        de = r * (gd - eh * jnp.mean(gd * eh, axis=-1, keepdims=True))
        dpg_ref[...] = (de * ppv * sg * (1.0 - sg)).astype(BF16)
        dpp_ref[...] = (de * sg).astype(BF16)
        _acc(dg_ref, d3 * eh)

    return _rows(body, t, 128, [(x2, ("t", d, 0)), (pg, ("t", d, 0)), (pp, ("t", d, 0)), (tgt, ("t", d, 0)), (g_post, ("b",))],
                 [((t, 1), F32, "t"), ((t, d), F32, "t"), ((t, d), BF16, "t"), ((t, d), BF16, "t"), ((1, d), F32, "a")], name)


def _norm_bwd(dn, dh, x, g_pre, fm, g_post, name):
    t, d = x.shape
    two = fm is not None

    def body(*refs):
        if two:
            dn_ref, dh_ref, x_ref, gp_ref, f_ref, gq_ref, dx_ref, df_ref, dgp_ref, dgq_ref = refs
        else:
            dn_ref, dh_ref, x_ref, gp_ref, dx_ref, dgp_ref = refs
        xv, dhv = x_ref[...], dh_ref[...]
        r = _rinv(xv)
        xh = xv * r
        gd = dhv * gp_ref[...]
        dx = dn_ref[...] + r * (gd - xh * jnp.mean(gd * xh, axis=-1, keepdims=True))
        dx_ref[...] = dx
        _acc(dgp_ref, dhv * xh)
        if two:
            fv = f_ref[...]
            rf = _rinv(fv)
            fh = fv * rf
            gd2 = dx * gq_ref[...]
            df_ref[...] = (rf * (gd2 - fh * jnp.mean(gd2 * fh, axis=-1, keepdims=True))).astype(BF16)
            _acc(dgq_ref, dx * fh)

    ins = [(dn, ("t", d, 0)), (dh, ("t", d, 0)), (x, ("t", d, 0)), (g_pre, ("b",))]
    outs = [((t, d), F32, "t")]
    if two:
        ins += [(fm, ("t", d, 0)), (g_post, ("b",))]
        outs += [((t, d), BF16, "t"), ((1, d), F32, "a"), ((1, d), F32, "a")]
    else:
        outs += [((1, d), F32, "a")]
    return _rows(body, t, 128, ins, outs, name)


CONV_TC = 256


def _shift_down(v, s):
    rows = lax.broadcasted_iota(jnp.int32, v.shape, 0)
    return jnp.where(rows >= s, pltpu.roll(v, s, 0), 0.0)


def _shift_up(v, s):
    n = v.shape[0]
    rows = lax.broadcasted_iota(jnp.int32, v.shape, 0)
    return jnp.where(rows < n - s, pltpu.roll(v, n - s, 0), 0.0)


def _conv_specs(t):
    nb = 1024 // CONV_TC
    seg = lambda c0: pl.BlockSpec((t, CONV_TC), lambda j, cb=c0 // CONV_TC: (0, cb + j))
    own = pl.BlockSpec((t, CONV_TC), lambda j: (0, j))
    wspec = pl.BlockSpec((3, CONV_TC), lambda j: (0, j))
    return nb, seg, own, wspec


def _conv_fwd(proj, conv_w, name, after=()):
    t = proj.shape[0]
    nb, seg, own, wspec = _conv_specs(t)

    def body(ax_ref, ab_ref, ac_ref, w_ref, *rest):
        za_ref = rest[len(after)]
        u = ac_ref[...].astype(F32) * ax_ref[...].astype(F32)
        w = w_ref[...]
        yc = w[0:1] * _shift_down(u, 2) + w[1:2] * _shift_down(u, 1) + w[2:3] * u
        za_ref[...] = (ab_ref[...].astype(F32) * yc).astype(BF16)

    return pl.pallas_call(
        body, name=name, grid=(nb,), in_specs=[seg(C_AX), seg(C_AB), seg(C_AC), wspec] + [ANY] * len(after), out_specs=own,
        out_shape=jax.ShapeDtypeStruct((t, 1024), BF16), compiler_params=_params(("parallel",)),
    )(proj, proj, proj, conv_w, *after)


def _conv_bwd(dza, proj, conv_w, name):
    t = proj.shape[0]
    nb, seg, own, wspec = _conv_specs(t)

    def body(dz_ref, ax_ref, ab_ref, ac_ref, w_ref, dax_ref, dab_ref, dac_ref, dw_ref):
        ax, ab, ac, dz = ax_ref[...].astype(F32), ab_ref[...].astype(F32), ac_ref[...].astype(F32), dz_ref[...].astype(F32)
        w = w_ref[...]
        u = ac * ax
        u1, u2 = _shift_down(u, 1), _shift_down(u, 2)
        yc = w[0:1] * u2 + w[1:2] * u1 + w[2:3] * u
        dab_ref[...] = (dz * yc).astype(BF16)
        dyc = dz * ab
        du = w[2:3] * dyc + w[1:2] * _shift_up(dyc, 1) + w[0:1] * _shift_up(dyc, 2)
        dax_ref[...] = (du * ac).astype(BF16)
        dac_ref[...] = (du * ax).astype(BF16)
        dw_ref[0:1, :] = jnp.sum(dyc * u2, axis=0, keepdims=True)
        dw_ref[1:2, :] = jnp.sum(dyc * u1, axis=0, keepdims=True)
        dw_ref[2:3, :] = jnp.sum(dyc * u, axis=0, keepdims=True)

    act = jax.ShapeDtypeStruct((t, 1024), BF16)
    return pl.pallas_call(
        body, name=name, grid=(nb,), in_specs=[own, seg(C_AX), seg(C_AB), seg(C_AC), wspec], out_specs=[own, own, own, wspec],
        out_shape=[act, act, act, jax.ShapeDtypeStruct((3, 1024), F32)], compiler_params=_params(("parallel",)),
    )(dza, proj, proj, proj, conv_w)


def _dot(a, b, dims, precision=None):
    return lax.dot_general(a, b, (dims, ((), ())), precision=precision, preferred_element_type=F32)


_CONTRACT = {"nn": ((1,), (0,)), "nt": ((1,), (1,)), "tn": ((0,), (0,))}


def _bdot_raw(a, b, mode):
    return _dot(a.astype(BF16), b.astype(BF16), _CONTRACT[mode])


@functools.partial(jax.custom_vjp, nondiff_argnums=(2,))
def _bdot(a, b, mode):
    return _bdot_raw(a, b, mode)


def _bdot_fwd(a, b, mode):
    return _bdot_raw(a, b, mode), (a, b)


def _bdot_bwd(mode, res, ct):
    a, b = res
    if mode == "nn":
        return _bdot_raw(ct, b, "nt"), _bdot_raw(a, ct, "tn")
    if mode == "nt":
        return _bdot_raw(ct, b, "nn"), _bdot_raw(ct, a, "tn")
    return _bdot_raw(b, ct, "nt"), _bdot_raw(a, ct, "nn")


_bdot.defvjp(_bdot_fwd, _bdot_bwd)


@functools.partial(jax.custom_vjp, nondiff_argnums=(2,))
def _sum_dot(ones, x, mode):
    head = x.astype(BF16)
    tail = x - head.astype(F32)
    if mode == "nn":
        return _bdot_raw(ones, head, "nn") + _bdot_raw(ones, tail, "nn")
    return _bdot_raw(head, ones, "tn") + _bdot_raw(tail, ones, "tn")


def _sum_dot_fwd(ones, x, mode):
    return _sum_dot(ones, x, mode), ones


def _sum_dot_bwd(mode, ones, ct):
    return jnp.zeros_like(ones), (_bdot_raw(ones, ct, "tn") if mode == "nn" else _bdot_raw(ones, ct, "nt"))


_sum_dot.defvjp(_sum_dot_fwd, _sum_dot_bwd)


def _gla_chunk(q, k, v, og, alr, s_in, wa, ba, gain):
    c = q.shape[0]
    z =_bdot(alr, wa, "nn") + ba
    la = (jnp.minimum(z, 0.0) - jnp.log(1.0 + jnp.exp(-jnp.abs(z)))) * (1.0 / TAU)
    row = lax.broadcasted_iota(jnp.int32, (c, c), 0)
    col = lax.broadcasted_iota(jnp.int32, (c, c), 1)
    lower = row >= col
    b = _sum_dot(lower.astype(F32), la, "nn")
    trow = lax.broadcasted_iota(jnp.int32, la.shape, 0)
    mid = jnp.sum(jnp.where(trow <= c // 2, la, 0.0), axis=0, keepdims=True)
    blast = jnp.sum(la, axis=0, keepdims=True)
    qs = q * (DK ** -0.5)
    e_up, e_dn = jnp.exp(b - mid), jnp.exp(mid - b)
    a_fwd = _bdot(qs * e_up, k * e_dn, "nt")
    a_rev = _bdot(qs * e_dn, k * e_up, "nt")
    att = jnp.where(lower, a_fwd, a_rev)
    o = _bdot(att, v, "nn") + _bdot(qs * jnp.exp(b), s_in, "nn")
    upd = _bdot(k * jnp.exp(blast - b), v, "tn")
    blast_col = _sum_dot(jnp.ones((c, DV), F32), la, "tn")
    s_out = jnp.exp(blast_col) * s_in + upd
    on = o * _rinv(o) * gain
    return on * og * _sig(og), s_out


def _gla_specs(t, rev):
    n = t // STEP_ROWS
    ch = (lambda i: n - 1 - i) if rev else (lambda i: i)
    col = lambda w, c0: pl.BlockSpec((STEP_ROWS, HEADS * w), lambda i, cb=c0 // (HEADS * w): (ch(i), cb))
    whole = lambda shape: pl.BlockSpec(shape, lambda i, nd=len(shape): (0,) * nd)
    specs = dict(
        q=col(DK, C_Q), k=col(DK, C_K), v=col(DV, C_V), og=col(DV, C_OG),
        alr=pl.BlockSpec((STEP_ROWS, 128), lambda i: (ch(i), C_ALR // 128)),
        wa=whole((128, HEADS * DK)), ba=whole((1, HEADS * DK)), gain=whole((1, DV)),
        state=pl.BlockSpec((STEP_CHUNKS, HEADS, DK, DV), lambda i: (ch(i), 0, 0, 0)),
        odk=pl.BlockSpec((STEP_ROWS, HEADS * DK), lambda i: (ch(i), 0)), odv=pl.BlockSpec((STEP_ROWS, HEADS * DV), lambda i: (ch(i), 0)),
        oalr=pl.BlockSpec((STEP_ROWS, 128), lambda i: (ch(i), 0)), whole=whole,
    )
    return n, specs


def _head_cols(h):
    return slice(h * DK, (h + 1) * DK), slice(h * DV, (h + 1) * DV)


def _gla_fwd(proj, wa, ba, gain, name):
    t = proj.shape[0]
    n, sp = _gla_specs(t, False)

    def body(q_ref, k_ref, v_ref, og_ref, alr_ref, wa_ref, ba_ref, g_ref, zb_ref, st_ref, s_scr):
        @pl.when(pl.program_id(0) == 0)
        def _():
            s_scr[...] = jnp.zeros_like(s_scr)

        state = [s_scr[h] for h in range(HEADS)]
        for c in range(STEP_CHUNKS):
            rows = slice(c * CHUNK, (c + 1) * CHUNK)
            alr = alr_ref[rows, :].astype(F32)
            for h in range(HEADS):
                kc, vc = _head_cols(h)
                st_ref[c, h] = state[h]
                zb, state[h] = _gla_chunk(q_ref[rows, kc].astype(F32), k_ref[rows, kc].astype(F32), v_ref[rows, vc].astype(F32),
                                          og_ref[rows, vc].astype(F32), alr, state[h], wa_ref[:, kc].astype(F32), ba_ref[:, kc], g_ref[...])
                zb_ref[rows, vc] = zb.astype(BF16)
        for h in range(HEADS):
            s_scr[h] = state[h]

    return pl.pallas_call(
        body, name=name, grid=(n,),
        in_specs=[sp["q"], sp["k"], sp["v"], sp["og"], sp["alr"], sp["wa"], sp["ba"], sp["gain"]],
        out_specs=[sp["odv"], sp["state"]],
        out_shape=[jax.ShapeDtypeStruct((t, HEADS * DV), BF16), jax.ShapeDtypeStruct((t // CHUNK, HEADS, DK, DV), F32)],
        scratch_shapes=[pltpu.VMEM((HEADS, DK, DV), F32)],
        compiler_params=_params(("arbitrary",)),
    )(proj, proj, proj, proj, proj, wa, ba, gain)


def _gla_bwd(dzb, proj, states, wa, ba, gain, name):
    t = proj.shape[0]
    n, sp = _gla_specs(t, True)

    def body(dz_ref, q_ref, k_ref, v_ref, og_ref, alr_ref, st_ref, wa_ref, ba_ref, g_ref,
             dq_ref, dk_ref, dv_ref, dog_ref, dalr_ref, dwa_ref, dba_ref, dg_ref, ds_scr):
        @pl.when(pl.program_id(0) == 0)
        def _():
            ds_scr[...] = jnp.zeros_like(ds_scr)
            dwa_ref[...] = jnp.zeros_like(dwa_ref)
            dba_ref[...] = jnp.zeros_like(dba_ref)
            dg_ref[...] = jnp.zeros_like(dg_ref)

        dstate = [ds_scr[h] for h in range(HEADS)]
        dwa_sum, dba_sum, dgain_sum = [None] * HEADS, [None] * HEADS, None
        for c in reversed(range(STEP_CHUNKS)):
            rows = slice(c * CHUNK, (c + 1) * CHUNK)
            alr = alr_ref[rows, :].astype(F32)
            dalr_sum = None
            for h in range(HEADS):
                kc, vc = _head_cols(h)
                args = (q_ref[rows, kc].astype(F32), k_ref[rows, kc].astype(F32), v_ref[rows, vc].astype(F32), og_ref[rows, vc].astype(F32),
                        alr, st_ref[c, h], wa_ref[:, kc].astype(F32), ba_ref[:, kc], g_ref[...])
                _, vjp = jax.vjp(_gla_chunk, *args)
                dq, dk, dv, dog, dalr, dstate[h], dwa, dba, dgain = vjp((dz_ref[rows, vc].astype(F32), dstate[h]))
                dq_ref[rows, kc] = dq.astype(BF16)
                dk_ref[rows, kc] = dk.astype(BF16)
                dv_ref[rows, vc] = dv.astype(BF16)
                dog_ref[rows, vc] = dog.astype(BF16)
                dwa_sum[h] = dwa if dwa_sum[h] is None else dwa_sum[h] + dwa
                dba_sum[h] = dba if dba_sum[h] is None else dba_sum[h] + dba
                dalr_sum = dalr if dalr_sum is None else dalr_sum + dalr
                dgain_sum = dgain if dgain_sum is None else dgain_sum + dgain
            dalr_ref[rows, :] = dalr_sum
        for h in range(HEADS):
            ds_scr[h] = dstate[h]
            dwa_ref[h] += dwa_sum[h]
            dba_ref[h] += dba_sum[h]
        dg_ref[...] += dgain_sum

    whole = sp["whole"]
    return pl.pallas_call(
        body, name=name, grid=(n,),
        in_specs=[sp["odv"], sp["q"], sp["k"], sp["v"], sp["og"], sp["alr"], sp["state"], sp["wa"], sp["ba"], sp["gain"]],
        out_specs=[sp["odk"], sp["odk"], sp["odv"], sp["odv"], sp["oalr"], whole((HEADS, 128, DK)), whole((HEADS, 1, DK)), whole((1, DV))],
        out_shape=[jax.ShapeDtypeStruct((t, HEADS * DK), BF16), jax.ShapeDtypeStruct((t, HEADS * DK), BF16),
                   jax.ShapeDtypeStruct((t, HEADS * DV), BF16), jax.ShapeDtypeStruct((t, HEADS * DV), BF16),
                   jax.ShapeDtypeStruct((t, 128), F32), jax.ShapeDtypeStruct((HEADS, 128, DK), F32),
                   jax.ShapeDtypeStruct((HEADS, 1, DK), F32), jax.ShapeDtypeStruct((1, DV), F32)],
        scratch_shapes=[pltpu.VMEM((HEADS, DK, DV), F32)],
        compiler_params=_params(("arbitrary",)),
    )(dzb, proj, proj, proj, proj, proj, states, wa, ba, gain)


def _local_step(x, p, tgt, gather_start, gather_finish, scatter_start, small):
    b_alpha, gain = small["b_alpha_up"], small["gla_head_gain"]
    gather_start(0, ())
    w = dict(gather_finish(0, ()))
    conv_w, w_alpha = w["conv_w"], w["w_alpha_up"]
    wa_p = jnp.zeros((128, HEADS * DK), BF16).at[:GATE_RANK].set(w_alpha.astype(BF16))

    t2 = gather_start(2, (w["in_t"], gather_start(1, ())))
    h1 = _rms_fwd(x, small["g_pre_mix"], "rms_pre_mix")
    proj = _mm(h1, w["in_t"], "nt", BF16, "mm_proj", after=(t2,))
    za = _conv_fwd(proj, conv_w, "conv_fwd")
    zb, states = _gla_fwd(proj, wa_p, b_alpha, gain, "gla_fwd")
    t3 = gather_start(3, (zb, za))
    w.update(gather_finish(1, (t3,)))
    ya = _mm(za, w["a_out"], "nn", BF16, "mm_ya", b3=True, tm=2048, tk=1024)
    yb = _mm(zb, w["b_out"], "nn", BF16, "mm_yb", b3=True, tm=2048, tk=1024)
    mix = _mix_fwd(proj, ya, yb, "mix_fwd")
    m2 = _mm(mix, w["mix"], "nn", F32, "mm_mix")
    t4 = gather_start(4, (m2,))
    x1, h2 = _post_pre(x, m2, small["g_post_mix"], small["g_pre_ffn"], "norm_mix_ffn")
    w.update(gather_finish(2, (h2, t4)))
    fg = _mm(h2, w["gate_t"], "nt", BF16, "mm_gate")
    fu = _mm(h2, w["up_t"], "nt", BF16, "mm_up")
    s = _swiglu_fwd(fg, fu, "swiglu_fwd")
    w.update(gather_finish(3, (s,)))
    f = _mm(s, w["down"], "nn", F32, "mm_down")
    x2, h3 = _post_pre(x1, f, small["g_post_ffn"], small["g_pre_ple"], "norm_ffn_ple")
    w.update(gather_finish(4, (h3,)))
    pg = _mm(h3, w["pg"], "nn", F32, "mm_pg")
    p_bf = p.astype(BF16)
    pp = _mm(p_bf, w["pp"], "nn", F32, "mm_pp", b3=True, tm=2048)
    loss_rows, d3, dpg, dpp, dg_post_ple = _ple_final(x2, pg, pp, tgt, small["g_post_ple"], "ple_final")

    gw = {}
    gw["pp"] = _mm(p_bf, dpp, "tn", BF16, "mm_dw_pp", out3=True)
    gw["pg"] = _mm(h3, dpg, "tn", BF16, "mm_dw_pg")
    dh3 = _mm(dpg, w["pg"], "nt", F32, "mm_dh3", after=(scatter_start(0, gw),))
    d2, df, dg_pre_ple, dg_post_ffn = _norm_bwd(d3, dh3, x2, small["g_pre_ple"], f, small["g_post_ffn"], "norm_bwd_ple_ffn")
    ds = _mm(df, w["down"], "nt", BF16, "mm_ds")
    gw["down"] = _mm(s, df, "tn", BF16, "mm_dw_down", tm=1408)
    dfg, dfu = _swiglu_bwd(ds, fg, fu, "swiglu_bwd")
    gw["gate_t"] = _mm(dfg, h2, "tn", BF16, "mm_dw_gate", after=(gw["down"],), tm=1408)
    gw["up_t"] = _mm(dfu, h2, "tn", BF16, "mm_dw_up", after=(gw["gate_t"],), tm=1408)
    dh2 = _mm(dfg, w["gate_t"], "nn", F32, "mm_dh2_gate", after=(scatter_start(1, gw),))
    dh2 = _mm(dfu, w["up_t"], "nn", F32, "mm_dh2_up", add=dh2, tm=1024)
    d1, dm2, dg_pre_ffn, dg_post_mix = _norm_bwd(d2, dh2, x1, small["g_pre_ffn"], m2, small["g_post_mix"], "norm_bwd_ffn_mix")
    dmix = _mm(dm2, w["mix"], "nt", F32, "mm_dmix")
    gw["mix"] = _mm(mix, dm2, "tn", BF16, "mm_dw_mix")
    dgab, dya, dyb = _mix_bwd(dmix, proj, ya, yb, "mix_bwd")
    dza = _mm(dya, w["a_out"], "nt", BF16, "mm_dza", after=(gw["mix"],), b3=True, tm=2048)
    gw["a_out"] = _mm(za, dya, "tn", BF16, "mm_dw_a_out", out3=True, tk=1024)
    gw["b_out"] = _mm(zb, dyb, "tn", BF16, "mm_dw_b_out", after=(gw["a_out"],), out3=True, tk=1024)
    dzb = _mm(dyb, w["b_out"], "nt", BF16, "mm_dzb", after=(scatter_start(2, gw),), b3=True, tm=2048)
    dax, dab, dac, dconv = _conv_bwd(dza, proj, conv_w, "conv_bwd")
    dq, dk, dv, dog, dalr, dwa, dba, dgain = _gla_bwd(dzb, proj, states, wa_p, b_alpha, gain, "gla_bwd")
    dproj = jnp.concatenate([dgab, dax, dab, dac, dq, dk, dv, dog, dalr.astype(BF16)], axis=1)
    gw["in_t"] = _mm(dproj, h1, "tn", BF16, "mm_dw_in", tm=1152)
    dh1 = _mm(dproj, w["in_t"], "nn", F32, "mm_dh1", after=(scatter_start(3, gw),))
    grad_x, dg_pre_mix = _norm_bwd(d1, dh1, x, small["g_pre_mix"], None, None, "norm_bwd_mix")

    gs = dict(
        conv_w=dconv,
        w_alpha_up=jnp.transpose(dwa[:, :GATE_RANK, :], (1, 0, 2)).reshape(GATE_RANK, HEADS * DK),
        b_alpha_up=dba.reshape(1, HEADS * DK), gla_head_gain=dgain,
        g_pre_mix=dg_pre_mix, g_post_mix=dg_post_mix, g_pre_ffn=dg_pre_ffn, g_post_ffn=dg_post_ffn,
        g_pre_ple=dg_pre_ple, g_post_ple=dg_post_ple,
    )
    return loss_rows, grad_x, gs


def _place():
    x, y, c = lax.axis_index("x"), lax.axis_index("y"), lax.axis_index("c")
    return x, y, c, [(1 - x, y), (x, 1 - y), (1 - x, 1 - y)]


def _all_gather(shards, name, cid=None):
    n = len(shards)

    def body(*refs):
        ins, outs = refs[:n], refs[n:2 * n]
        send_sems, recv_sems, local_sems = refs[2 * n:]
        x, y, c, chips = _place()
        me, sibling = (x, y, c), (x, y, 1 - c)

        def slot(px, py, pc):
            return 4 * px + 2 * py + pc

        def copy(a, k, block, to, src=None):
            dst = outs[a].at[slot(*block)]
            return pltpu.make_async_remote_copy(src_ref=dst if src is None else src, dst_ref=dst, send_sem=send_sems.at[a, k],
                                                recv_sem=recv_sems.at[a, k], device_id=to, device_id_type=MESH)

        mine = [pltpu.make_async_copy(ins[a], outs[a].at[slot(*me)], local_sems.at[a]) for a in range(n)]
        for cp in mine:
            cp.start()
        first = []
        for j, chip in enumerate(chips):
            first += [copy(a, 1 + j, me, (*chip, c), src=ins[a]) for a in range(n)]
        first += [copy(a, 0, me, sibling, src=ins[a]) for a in range(n)]
        for cp in first:
            cp.start()
        passed = []
        for j, chip in enumerate(chips):
            for a in range(n):
                copy(a, 1 + j, (*chip, c), me).wait_recv()
                cp = copy(a, 4 + j, (*chip, c), sibling)
                cp.start()
                passed.append(cp)
        for a in range(n):
            copy(a, 0, sibling, me).wait_recv()
        for j, chip in enumerate(chips):
            for a in range(n):
                copy(a, 4 + j, (*chip, 1 - c), me).wait_recv()
        for cp in first + passed:
            cp.wait_send()
        for cp in mine:
            cp.wait()

    if cid is None:
        return pl.pallas_call(
            body, name=name, in_specs=[ANY] * n, out_specs=[ANY] * n,
            out_shape=[jax.ShapeDtypeStruct((N_DEV,) + s.shape, s.dtype) for s in shards],
            scratch_shapes=[pltpu.SemaphoreType.DMA((n, 7)), pltpu.SemaphoreType.DMA((n, 7)), pltpu.SemaphoreType.DMA((n,))],
        )(*shards)

    src = [jax.new_ref(s, memory_space=pltpu.MemorySpace.HBM) for s in shards]
    dst = [jax.empty_ref(jax.ShapeDtypeStruct((N_DEV,) + s.shape, s.dtype), memory_space=pltpu.MemorySpace.HBM) for s in shards]

    @pl.kernel(mesh=plsc.ScalarSubcoreMesh(axis_name="seq", num_cores=1), name=name,
               scratch_types=(pltpu.SemaphoreType.DMA((n, 7)), pltpu.SemaphoreType.DMA((n, 7)), pltpu.SemaphoreType.DMA((n,))),
               compiler_params=pltpu.CompilerParams(collective_id=cid))
    def launch(send_sems, recv_sems, local_sems):
        x, y, c, chips = _place()
        barrier = pltpu.get_barrier_semaphore()
        for peer in [(x, y, 1 - c)] + [(*chip, c) for chip in chips]:
            pl.semaphore_signal(barrier, inc=1, device_id=peer, device_id_type=MESH)
        pl.semaphore_wait(barrier, 4)
        body(*src, *dst, send_sems, recv_sems, local_sems)

    launch()
    return [r[...] for r in dst]


def _reduce_scatter(parts, name, cid):
    n = len(parts)
    src = [jax.new_ref(s, memory_space=pltpu.MemorySpace.HBM) for s in parts]
    dst = [jax.empty_ref(jax.ShapeDtypeStruct(s.shape, s.dtype), memory_space=pltpu.MemorySpace.HBM) for s in parts]

    @pl.kernel(mesh=plsc.ScalarSubcoreMesh(axis_name="seq", num_cores=1), name=name,
               scratch_types=(pltpu.SemaphoreType.DMA((n, N_DEV - 1)), pltpu.SemaphoreType.DMA((n, N_DEV - 1)), pltpu.SemaphoreType.DMA((n,))),
               compiler_params=pltpu.CompilerParams(collective_id=cid))
    def launch(send_sems, recv_sems, local_sems):
        x, y, c, _ = _place()
        me = 4 * x + 2 * y + c
        peers = [(1 - x if k & 4 else x, 1 - y if k & 2 else y, 1 - c if k & 1 else c) for k in range(1, N_DEV)]
        barrier = pltpu.get_barrier_semaphore()
        for peer in peers:
            pl.semaphore_signal(barrier, inc=1, device_id=peer, device_id_type=MESH)
        pl.semaphore_wait(barrier, N_DEV - 1)
        mine = [pltpu.make_async_copy(src[a].at[me], dst[a].at[me], local_sems.at[a]) for a in range(n)]
        for cp in mine:
            cp.start()
        cps = []
        for a in range(n):
            for k, (px, py, pc) in enumerate(peers):
                cps.append(pltpu.make_async_remote_copy(src_ref=src[a].at[4 * px + 2 * py + pc], dst_ref=dst[a].at[me], send_sem=send_sems.at[a, k],
                                                        recv_sem=recv_sems.at[a, k], device_id=(px, py, pc), device_id_type=MESH))
        for cp in cps:
            cp.start()
        for cp in cps:
            cp.wait_recv()
        for cp in cps:
            cp.wait_send()
        for cp in mine:
            cp.wait()

    launch()
    return [r[...] for r in dst]


def _sibling_exchange(parts, name):
    n = len(parts)
    pieces = [_row_pieces(s.shape[1]) for s in parts]

    def body(*refs):
        ins, outs = refs[:n], refs[n:2 * n]
        send_sems, recv_sems = refs[2 * n:]
        x, y, c, _ = _place()

        def copy(a, ch, q, rows):
            return pltpu.make_async_remote_copy(src_ref=ins[a].at[2 * ch + 1 - c, rows], dst_ref=outs[a].at[ch, rows], send_sem=send_sems.at[a, ch, q],
                                                recv_sem=recv_sems.at[a, ch, q], device_id=(x, y, 1 - c), device_id_type=MESH)

        cps = [copy(a, ch, q, rows) for ch in range(4) for a in range(n) for q, rows in enumerate(pieces[a])]
        for cp in cps:
            cp.start()
        for cp in cps:
            cp.wait_recv()
        for cp in cps:
            cp.wait_send()

    return pl.pallas_call(
        body, name=name, in_specs=[ANY] * n, out_specs=[ANY] * n,
        out_shape=[jax.ShapeDtypeStruct((4,) + s.shape[1:], s.dtype) for s in parts],
        scratch_shapes=[pltpu.SemaphoreType.DMA((n, 4, PIECES)), pltpu.SemaphoreType.DMA((n, 4, PIECES))],
    )(*parts)


def _chip_exchange(parts, name):
    n = len(parts)

    def body(*refs):
        ins, outs = refs[:n], refs[n:2 * n]
        send_sems, recv_sems, local_sems = refs[2 * n:]
        x, y, c, chips = _place()
        my_chip = 2 * x + y

        def copy(a, j):
            px, py = chips[j]
            return pltpu.make_async_remote_copy(src_ref=ins[a].at[2 * px + py], dst_ref=outs[a].at[my_chip], send_sem=send_sems.at[a, j],
                                                recv_sem=recv_sems.at[a, j], device_id=(px, py, c), device_id_type=MESH)

        def landing(a, j):
            px, py = chips[j]
            return pltpu.make_async_remote_copy(src_ref=ins[a].at[my_chip], dst_ref=outs[a].at[2 * px + py], send_sem=send_sems.at[a, j],
                                                recv_sem=recv_sems.at[a, j], device_id=(px, py, c), device_id_type=MESH)

        mine = [pltpu.make_async_copy(ins[a].at[my_chip], outs[a].at[my_chip], local_sems.at[a]) for a in range(n)]
        for cp in mine:
            cp.start()
        cps = [copy(a, j) for j in range(3) for a in range(n)]
        for cp in cps:
            cp.start()
        for j in range(3):
            for a in range(n):
                landing(a, j).wait_recv()
        for cp in cps:
            cp.wait_send()
        for cp in mine:
            cp.wait()

    return pl.pallas_call(
        body, name=name, in_specs=[ANY] * n, out_specs=[ANY] * n,
        out_shape=[jax.ShapeDtypeStruct(s.shape, s.dtype) for s in parts],
        scratch_shapes=[pltpu.SemaphoreType.DMA((n, 3)), pltpu.SemaphoreType.DMA((n, 3)), pltpu.SemaphoreType.DMA((n,))],
    )(*parts)


def _pair_add(mine8, got4, name):
    _, r, cols = mine8.shape
    steps, blk, at = _tiles(r, cols)
    core = lax.axis_index("c").astype(jnp.int32).reshape(1)

    def body(c_ref, a_ref, b_ref, o_ref):
        o_ref[...] = (a_ref[...].astype(F32) + b_ref[...].astype(F32)).astype(BF16)

    return pl.pallas_call(
        body, name=name,
        grid_spec=pltpu.PrefetchScalarGridSpec(
            num_scalar_prefetch=1, grid=(4, steps),
            in_specs=[pl.BlockSpec((None,) + blk, lambda ch, i, c_ref: (2 * ch + c_ref[0],) + at(i)),
                      pl.BlockSpec((None,) + blk, lambda ch, i, c_ref: (ch,) + at(i))],
            out_specs=pl.BlockSpec((None,) + blk, lambda ch, i, c_ref: (ch,) + at(i))),
        out_shape=jax.ShapeDtypeStruct((4, r, cols), BF16),
        compiler_params=_params(("parallel", "parallel")),
    )(core, mine8, got4)


HBM = pl.BlockSpec(memory_space=pltpu.HBM)
SEM = pl.BlockSpec(memory_space=pltpu.SEMAPHORE)
EFFECT = pltpu.SideEffectType.DATAFLOW_SIDE_EFFECTING


def _in_hbm(a):
    return pltpu.with_memory_space_constraint(a, pltpu.HBM)


def _remote_copies(plan, srcs, lands, send_sems, recv_sems):
    return [pltpu.make_async_remote_copy(src_ref=s, dst_ref=d, send_sem=send_sems.at[i], recv_sem=recv_sems.at[i], device_id=peer,
                                         device_id_type=MESH) for i, (s, d, peer) in enumerate(plan(srcs, lands))]


def _copies_start(plan, n_copies, srcs, land_shapes, name, after=()):
    ns, nl = len(srcs), len(land_shapes)

    def body(*refs):
        send_sems, recv_sems = refs[ns + nl + len(after):ns + nl + len(after) + 2]
        for cp in _remote_copies(plan, refs[:ns], refs[ns:ns + nl], send_sems, recv_sems):
            cp.start()
        refs[-1][...] = jnp.zeros((8, 128), F32)

    sems = pltpu.SemaphoreType.DMA((n_copies,))
    return pl.pallas_call(
        body, name=name,
        out_shape=(sems, sems, *[pltpu.HBM(s.shape, s.dtype) for s in srcs], *[pltpu.HBM(s.shape, s.dtype) for s in land_shapes],
                   jax.ShapeDtypeStruct((8, 128), F32)),
        in_specs=[HBM] * (ns + nl) + [ANY] * len(after),
        out_specs=(SEM, SEM, *[HBM] * (ns + nl), pl.BlockSpec(memory_space=pltpu.VMEM)),
        input_output_aliases={i: 2 + i for i in range(ns + nl)},
        compiler_params=pltpu.CompilerParams(has_side_effects=EFFECT),
    )(*[_in_hbm(s) for s in srcs], *[_in_hbm(lax.empty(s.shape, s.dtype)) for s in land_shapes], *after)


def _copies_wait(plan, state, ns, name, after=()):
    send_sems, recv_sems, *arrs = state[:-1]
    n = len(arrs)

    def body(*refs):
        cps = _remote_copies(plan, refs[:ns], refs[ns:n], refs[n], refs[n + 1])
        for cp in cps:
            cp.wait_send()
        for cp in cps:
            cp.wait_recv()

    out = pl.pallas_call(
        body, name=name, out_shape=tuple(pltpu.HBM(a.shape, a.dtype) for a in arrs),
        in_specs=[HBM] * n + [SEM, SEM] + [ANY] * len(after), out_specs=tuple([HBM] * n),
        input_output_aliases={i: i for i in range(n)},
        compiler_params=pltpu.CompilerParams(has_side_effects=EFFECT),
    )(*arrs, send_sems, recv_sems, *after)
    return list(out[:ns]), list(out[ns:])


def _gather_plan(srcs, lands):
    x, y, c, chips = _place()
    peers = [(x, y, 1 - c)] + [(*chip, c) for chip in chips]
    return [(s, l.at[4 * x + 2 * y + c], peer) for s, l in zip(srcs, lands) for peer in peers]


def _scatter_plan(srcs, lands):
    x, y, c, _ = _place()
    peers = [(1 - x if k & 4 else x, 1 - y if k & 2 else y, 1 - c if k & 1 else c) for k in range(1, N_DEV)]
    return [(s.at[4 * px + 2 * py + pc], l.at[4 * x + 2 * y + c], (px, py, pc)) for s, l in zip(srcs, lands) for px, py, pc in peers]


def _everyone_plan(srcs, lands):
    x, y, c, _ = _place()
    peers = [(1 - x if k & 4 else x, 1 - y if k & 2 else y, 1 - c if k & 1 else c) for k in range(1, N_DEV)]
    return [(s, l.at[4 * x + 2 * y + c], peer) for s, l in zip(srcs, lands) for peer in peers]


def _sum_parts(got, own, me, name):
    def body(me_ref, got_ref, own_ref, o_ref):
        acc = jnp.where(me_ref[0] == 0, own_ref[...], got_ref[0])
        for d in range(1, N_DEV):
            acc = acc + jnp.where(me_ref[0] == d, own_ref[...], got_ref[d])
        o_ref[...] = acc

    return pl.pallas_call(
        body, name=name,
        grid_spec=pltpu.PrefetchScalarGridSpec(
            num_scalar_prefetch=1, grid=(1,),
            in_specs=[pl.BlockSpec(got.shape, lambda i, me_ref: (0, 0, 0)), pl.BlockSpec(own.shape, lambda i, me_ref: (0, 0))],
            out_specs=pl.BlockSpec(own.shape, lambda i, me_ref: (0, 0))),
        out_shape=jax.ShapeDtypeStruct(own.shape, F32),
    )(me.astype(jnp.int32).reshape(1), got, own)


def _chip_plan(srcs, lands):
    x, y, c, chips = _place()
    return [(s.at[2 * px + py], l.at[2 * x + y], (px, py, c)) for s, l in zip(srcs, lands) for px, py in chips]


PIECES = 8


def _row_pieces(rows):
    for k in (PIECES, 4, 2):
        if rows % (16 * k) == 0:
            return [pl.ds(q * (rows // k), rows // k) for q in range(k)]
    return [pl.ds(0, rows)]


def _put_own(shard, zone, me, name):
    r, c = shard.shape
    tr = r if r <= 256 else _pick(r, (256, 64))

    def body(me_ref, s_ref, z_ref, o_ref):
        o_ref[...] = s_ref[...]

    return pl.pallas_call(
        body, name=name,
        grid_spec=pltpu.PrefetchScalarGridSpec(
            num_scalar_prefetch=1, grid=(r // tr,),
            in_specs=[pl.BlockSpec((tr, c), lambda i, me_ref: (i, 0)), ANY],
            out_specs=pl.BlockSpec((None, tr, c), lambda i, me_ref: (me_ref[0], i, 0))),
        out_shape=jax.ShapeDtypeStruct(zone.shape, zone.dtype), input_output_aliases={2: 0},
        compiler_params=_params(("arbitrary",)),
    )(me.astype(jnp.int32).reshape(1), shard, zone)


def _gather_finish(lands, name):
    n = len(lands)
    pieces = [_row_pieces(s.shape[1]) for s in lands]

    def body(*refs):
        zones, outs = refs[:n], refs[n:2 * n]
        send_sems, recv_sems = refs[2 * n:]
        x, y, c, chips = _place()
        cps = []
        for j, (px, py) in enumerate(chips):
            for a in range(n):
                for q, rows in enumerate(pieces[a]):
                    cps.append(pltpu.make_async_remote_copy(
                        src_ref=zones[a].at[4 * px + 2 * py + c, rows], dst_ref=outs[a].at[4 * px + 2 * py + c, rows],
                        send_sem=send_sems.at[a, j, q], recv_sem=recv_sems.at[a, j, q], device_id=(x, y, 1 - c), device_id_type=MESH))
        for cp in cps:
            cp.start()
        for cp in cps:
            cp.wait_recv()
        for cp in cps:
            cp.wait_send()

    return pl.pallas_call(
        body, name=name, in_specs=[ANY] * n, out_specs=[ANY] * n,
        out_shape=[jax.ShapeDtypeStruct(l.shape, l.dtype) for l in lands],
        input_output_aliases={a: a for a in range(n)},
        scratch_shapes=[pltpu.SemaphoreType.DMA((n, 3, PIECES)), pltpu.SemaphoreType.DMA((n, 3, PIECES))],
    )(*lands)


def _sum_everywhere(v, name):
    rows = v.shape[0]

    def body(v_ref, o_ref, buf, send_sems, recv_sems):
        x, y, c, _ = _place()
        me = 4 * x + 2 * y + c
        buf[me] = v_ref[...]
        cps = []
        for k in range(1, N_DEV):
            fx, fy, fc = (k >> 2) & 1, (k >> 1) & 1, k & 1
            to = (1 - x if fx else x, 1 - y if fy else y, 1 - c if fc else c)
            cps.append(pltpu.make_async_remote_copy(src_ref=buf.at[me], dst_ref=buf.at[me], send_sem=send_sems.at[k - 1],
                                                    recv_sem=recv_sems.at[k - 1], device_id=to, device_id_type=MESH))
        for cp in cps:
            cp.start()
        for cp in cps:
            cp.wait_recv()
        for cp in cps:
            cp.wait_send()
        acc = buf[0]
        for d in range(1, N_DEV):
            acc = acc + buf[d]
        o_ref[...] = acc

    vm = pl.BlockSpec(memory_space=pltpu.VMEM)
    return pl.pallas_call(
        body, name=name, in_specs=[vm], out_specs=vm, out_shape=jax.ShapeDtypeStruct(v.shape, F32),
        scratch_shapes=[pltpu.VMEM((N_DEV, rows, 128), F32), pltpu.SemaphoreType.DMA((N_DEV - 1,)), pltpu.SemaphoreType.DMA((N_DEV - 1,))],
    )(v)


def _adamw_parts(w, got, mine, me, m, v, name, after=()):
    r, c = w.shape
    n_parts = got.shape[0]
    steps, blk, at = _tiles(r, c)

    def body(me_ref, w_ref, got_ref, own_ref, m_ref, v_ref, *rest):
        go_ref, d_ref, mo_ref, vo_ref = rest[len(after):]
        own = own_ref[...].astype(F32)
        gv = jnp.where(me_ref[0] == 0, own, got_ref[0].astype(F32))
        for d in range(1, n_parts):
            gv = gv + jnp.where(me_ref[0] == d, own, got_ref[d].astype(F32))
        _adamw_math(gv, w_ref, m_ref, v_ref, go_ref, d_ref, mo_ref, vo_ref)

    tile = pl.BlockSpec(blk, lambda i, me_ref: at(i))
    out = jax.ShapeDtypeStruct((r, c), F32)
    return pl.pallas_call(
        body, name=name,
        grid_spec=pltpu.PrefetchScalarGridSpec(
            num_scalar_prefetch=1, grid=(steps,),
            in_specs=[tile, pl.BlockSpec((n_parts,) + blk, lambda i, me_ref: (0,) + at(i)),
                      pl.BlockSpec((None,) + blk, lambda i, me_ref: (me_ref[0],) + at(i)), tile, tile] + [ANY] * len(after),
            out_specs=[tile] * 4),
        out_shape=[out] * 4, compiler_params=_params(("parallel",)),
    )(me.astype(jnp.int32).reshape(1), w, got, mine, m, v, *after)


def _adamw_math(gv, w_ref, m_ref, v_ref, go_ref, d_ref, mo_ref, vo_ref):
    mn = B1 * m_ref[...] + (1.0 - B1) * gv
    vn = B2 * v_ref[...] + (1.0 - B2) * (gv * gv)
    m_hat = mn / (1.0 - B1 ** STEP)
    v_hat = vn / (1.0 - B2 ** STEP)
    go_ref[...] = gv
    d_ref[...] = -LR * (m_hat / (jnp.sqrt(v_hat) + ADAM_EPS) + WD * w_ref[...])
    mo_ref[...] = mn
    vo_ref[...] = vn


def _adamw(w, g, m, v, name):
    r, c = w.shape
    parts = g.ndim == 3
    tr = r if r <= 128 else _pick(r, (128, 64))

    def body(w_ref, g_ref, m_ref, v_ref, go_ref, d_ref, mo_ref, vo_ref):
        if parts:
            gv = g_ref[0].astype(F32)
            for d in range(1, g.shape[0]):
                gv = gv + g_ref[d].astype(F32)
        else:
            gv = g_ref[...]
        mn = B1 * m_ref[...] + (1.0 - B1) * gv
        vn = B2 * v_ref[...] + (1.0 - B2) * (gv * gv)
        m_hat = mn / (1.0 - B1 ** STEP)
        v_hat = vn / (1.0 - B2 ** STEP)
        go_ref[...] = gv
        d_ref[...] = -LR * (m_hat / (jnp.sqrt(v_hat) + ADAM_EPS) + WD * w_ref[...])
        mo_ref[...] = mn
        vo_ref[...] = vn

    tile = pl.BlockSpec((tr, c), lambda i: (i, 0))
    g_spec = pl.BlockSpec((g.shape[0], tr, c), lambda i: (0, i, 0)) if parts else tile
    out = jax.ShapeDtypeStruct((r, c), F32)
    return pl.pallas_call(
        body, name=name, grid=(r // tr,), in_specs=[tile, g_spec, tile, tile], out_specs=[tile] * 4, out_shape=[out] * 4,
        compiler_params=_params(("parallel",)),
    )(w, g, m, v)


BIG = ["w_in", "w_a_out", "w_b_out", "w_mix_out", "w_ff_gate", "w_ff_up", "w_ff_down", "w_ple_gate", "w_ple_proj"]
TRANSPOSED = ["w_in", "w_ff_gate", "w_ff_up"]
SMALL = ["conv_w", "w_alpha_up", "b_alpha_up", "gla_head_gain", "g_pre_mix", "g_post_mix", "g_pre_ffn", "g_post_ffn", "g_pre_ple", "g_post_ple"]
WEIGHTS = ["w_in", "conv_w", "w_a_out", "w_alpha_up", "b_alpha_up", "gla_head_gain", "w_b_out", "w_mix_out", "g_pre_mix", "g_post_mix",
           "g_pre_ffn", "g_post_ffn", "w_ff_gate", "w_ff_up", "w_ff_down", "g_pre_ple", "g_post_ple", "w_ple_gate", "w_ple_proj"]


def _in_t_from_blocks(z):
    w = z.reshape(-1, z.shape[-1])
    return jnp.concatenate([w[R_GA:R_END], w[:R_ALR], w[R_ALR:R_GA], jnp.zeros((128 - GATE_RANK, w.shape[1]), w.dtype)], axis=0)


def _blocks_from_in_t(g):
    per = R_END // N_DEV

    def rows(lo, hi):
        out = []
        for n0, n1, p0 in ((0, R_ALR, C_AX), (R_ALR, R_GA, C_ALR), (R_GA, R_END, 0)):
            a, e = max(lo, n0), min(hi, n1)
            if a < e:
                out.append(g[p0 + a - n0:p0 + e - n0])
        return out

    return jnp.stack([jnp.concatenate(rows(b * per, (b + 1) * per), axis=0) for b in range(N_DEV)])


def _cols_to_full(g8):
    n, r, c = g8.shape
    return jnp.transpose(g8, (1, 0, 2)).reshape(r, n * c)


def _full_to_cols(a):
    r, c = a.shape
    return jnp.transpose(a.reshape(r, N_DEV, c // N_DEV), (1, 0, 2))


def _pack(arrs, rows):
    flat = jnp.concatenate([a.reshape(-1) for a in arrs])
    return jnp.pad(flat, (0, rows * 128 - flat.shape[0])).reshape(rows, 128)


def _unpack(packed, shapes):
    flat, out, o = packed.reshape(-1), [], 0
    for s in shapes:
        size = 1
        for d in s:
            size *= d
        out.append(flat[o:o + size].reshape(s))
        o += size
    return out


def kernel(x, p, w_in, conv_w, w_a_out, w_alpha_up, b_alpha_up, gla_head_gain, w_b_out, w_mix_out, g_pre_mix, g_post_mix, g_pre_ffn, g_post_ffn, w_ff_gate, w_ff_up, w_ff_down, g_pre_ple, g_post_ple, w_ple_gate, w_ple_proj, loss_target, m_w_in, m_conv_w, m_w_a_out, m_w_alpha_up, m_b_alpha_up, m_gla_head_gain, m_w_b_out, m_w_mix_out, m_g_pre_mix, m_g_post_mix, m_g_pre_ffn, m_g_post_ffn, m_w_ff_gate, m_w_ff_up, m_w_ff_down, m_g_pre_ple, m_g_post_ple, m_w_ple_gate, m_w_ple_proj, v_w_in, v_conv_w, v_w_a_out, v_w_alpha_up, v_b_alpha_up, v_gla_head_gain, v_w_b_out, v_w_mix_out, v_g_pre_mix, v_g_post_mix, v_g_pre_ffn, v_g_post_ffn, v_w_ff_gate, v_w_ff_up, v_w_ff_down, v_g_pre_ple, v_g_post_ple, v_w_ple_gate, v_w_ple_proj):
    args = dict(locals())
    shard = lambda n, a: jnp.transpose(a[0]) if n in TRANSPOSED else a[0]
    wts = {n: shard(n, args[n]) for n in WEIGHTS}
    mom = {n: shard(n, args["m_" + n]) for n in WEIGHTS}
    var = {n: shard(n, args["v_" + n]) for n in WEIGHTS}
    me =4 * lax.axis_index("x") + 2 * lax.axis_index("y") + lax.axis_index("c")

    groups = [["w_in", "conv_w", "w_alpha_up"], ["w_a_out", "w_b_out", "w_mix_out"], ["w_ff_gate", "w_ff_up"], ["w_ff_down"],
              ["w_ple_gate", "w_ple_proj"]]
    grad_groups = [["w_ple_proj", "w_ple_gate"], ["w_ff_down", "w_ff_gate", "w_ff_up"], ["w_mix_out", "w_a_out", "w_b_out"], ["w_in"]]
    rows_full = lambda g: g.reshape(-1, g.shape[-1])
    gathers, scatters = {}, {}

    def gather_start(gi, after):
        if gi not in gathers:
            shards = [wts[n].astype(BF16) if n in BIG else wts[n] for n in groups[gi]]
            zones = [jax.ShapeDtypeStruct((N_DEV,) + s.shape, s.dtype) for s in shards]
            gathers[gi] = (shards, _copies_start(_gather_plan, 4 * len(shards), shards, zones, "gather_start_%d" % gi, after))
        return gathers[gi][1][-1]

    def gather_finish(gi, after):
        shards, state = gathers[gi]
        shards, zones = _copies_wait(_gather_plan, state, len(shards), "gather_wait_%d" % gi, after)
        if gi == 0:
            gather_start(1, (zones[0],))
        zones = _gather_finish(zones, "gather_finish_%d" % gi)
        g8 = {n: _put_own(s, z, me, "gather_own_" + n) for n, s, z in zip(groups[gi], shards, zones)}
        if gi == 0:
            return dict(in_t=_in_t_from_blocks(g8["w_in"]),
                        conv_w=_cols_to_full(g8["conv_w"]), w_alpha_up=_cols_to_full(g8["w_alpha_up"]))
        if gi == 1:
            return dict(a_out=g8["w_a_out"], b_out=g8["w_b_out"], mix=rows_full(g8["w_mix_out"]))
        if gi == 2:
            return dict(gate_t=rows_full(g8["w_ff_gate"]), up_t=rows_full(g8["w_ff_up"]))
        if gi == 3:
            return dict(down=rows_full(g8["w_ff_down"]))
        return dict(pg=rows_full(g8["w_ple_gate"]), pp=g8["w_ple_proj"])

    def scatter_start(gi, gw):
        if gi == 3:
            full = dict(w_in=_blocks_from_in_t(gw["in_t"]))
        elif gi == 1:
            full = dict(w_ff_down=gw["down"], w_ff_gate=gw["gate_t"], w_ff_up=gw["up_t"])
        elif gi == 2:
            full = dict(w_mix_out=gw["mix"], w_a_out=gw["a_out"], w_b_out=gw["b_out"])
        else:
            full = dict(w_ple_proj=gw["pp"], w_ple_gate=gw["pg"])
        parts = [full[n] if full[n].ndim == 3 else full[n].reshape(N_DEV, -1, full[n].shape[-1]) for n in grad_groups[gi]]
        if gi == 3:
            from_sibling = _sibling_exchange(parts, "scatter_sibling_%d" % gi)
            parts = [_pair_add(a, b, "scatter_add_%d_%s" % (gi, n)) for n, a, b in zip(grad_groups[gi], parts, from_sibling)]
            scatters[gi] = _copies_start(_chip_plan, 3 * len(parts), parts, parts, "scatter_start_%d" % gi)
        else:
            scatters[gi] = _copies_start(_scatter_plan, (N_DEV - 1) * len(parts), parts, parts, "scatter_start_%d" % gi)
        return scatters[gi][-1]

    small = {n: wts[n].reshape(1, -1) for n in SMALL[2:]}

    loss_rows, grad_x, gs = _local_step(x[0], p[0, 0], loss_target[0], gather_start, gather_finish, scatter_start, small)
    gs["loss"] = jnp.sum(loss_rows).reshape(1, 1)

    small_shapes = [gs[n].shape for n in SMALL]
    gs_packed = _pack([gs[n] for n in SMALL + ["loss"]], 192)
    small_state = _copies_start(_everyone_plan, N_DEV - 1, [gs_packed], [jax.ShapeDtypeStruct((N_DEV,) + gs_packed.shape, F32)],
                                "small_start", (grad_x,))

    res, done = {}, (small_state[-1],)
    for gi, names in enumerate(grad_groups):
        plan, slot = (_chip_plan, me // 2) if gi == 3 else (_scatter_plan, me)
        mine, got = _copies_wait(plan, scatters[gi], len(names), "scatter_wait_%d" % gi, done)
        for n, g, own in zip(names, got, mine):
            res[n] = _adamw_parts(wts[n], g, own, slot, mom[n], var[n], "adamw_" + n)
        done = tuple(res[n][1] for n in names)

    (gs_own,), (gs_got,) = _copies_wait(_everyone_plan, small_state, 1, "small_wait", done)
    gsum = dict(zip(SMALL + ["loss"], _unpack(_sum_parts(gs_got, gs_own, me, "small_sum"), small_shapes + [(1, 1)])))
    loss = gsum["loss"].reshape(())
    gsum["conv_w"] = lax.dynamic_index_in_dim(gsum["conv_w"].reshape(3, N_DEV, -1), me, axis=1, keepdims=False)
    gsum["w_alpha_up"] = lax.dynamic_index_in_dim(gsum["w_alpha_up"].reshape(GATE_RANK, N_DEV, -1), me, axis=1, keepdims=False)

    shard_shapes = [wts[n].shape for n in SMALL]
    packed = [_pack([d[n] for n in SMALL], 120) for d in (wts, gsum, mom, var)]
    outs = [_unpack(o, shard_shapes) for o in _adamw(*packed, "adamw_small")]
    for i, n in enumerate(SMALL):
        res[n] = [o[i] for o in outs]

    back = lambda n, a: (jnp.transpose(a) if n in TRANSPOSED else a)[None]
    return (loss, grad_x[None], *[back(n, res[n][i]) for i in range(4) for n in WEIGHTS])
```

```python
import functools

import jax
import jax.numpy as jnp
from jax import lax
from jax.experimental import pallas as pl
from jax.experimental.pallas import tpu as pltpu
from jax.experimental.pallas import tpu_sc as plsc

F32, BF16 = jnp.float32, jnp.bfloat16
EPS = 1e-6
CHUNK = 64
STEP_CHUNKS = 2
STEP_ROWS = STEP_CHUNKS * CHUNK
HEADS, DK, DV = 4, 128, 256
GATE_RANK = 16
TAU = 16.0
LR, B1, B2, ADAM_EPS, WD, STEP = 0.001, 0.9, 0.999, 1e-08, 0.01, 10
N_DEV = 8
MESH = pl.DeviceIdType.MESH
VMEM_LIMIT = 56 * 1024 * 1024
ANY = pl.BlockSpec(memory_space=pl.ANY)

C_GA, C_GB, C_AX, C_AB, C_AC, C_Q, C_K, C_V, C_OG, C_ALR = 0, 2048, 4096, 5120, 6144, 7168, 7680, 8192, 9216, 10240
IN_PAD = 10368
R_AX, R_AB, R_AC, R_Q, R_K, R_V, R_OG, R_ALR, R_GA, R_GB, R_END = 0, 1024, 2048, 3072, 3584, 4096, 5120, 6144, 6160, 8208, 10256


def _params(sem):
    return pltpu.CompilerParams(dimension_semantics=sem, vmem_limit_bytes=VMEM_LIMIT)


def _pick(n, cands):
    for c in cands:
        if n % c == 0:
            return c
    return n


def _tiles(r, c):
    for tr in (128, 64):
        if r % tr == 0:
            return r // tr, (tr, c), lambda i: (i, 0)
    tc = _pick(c, (256, 128))
    return c // tc, (r, tc), lambda i: (0, i)


def _mm(a, b, mode, out_dtype, name, after=(), add=None, b3=False, out3=False, tm=None, tk=None):
    bshape = (b.shape[1], N_DEV * b.shape[2]) if b3 else b.shape
    if mode == "nn":
        (m, k), (k2, n) = a.shape, bshape
    elif mode == "nt":
        (m, k), (n, k2) = a.shape, bshape
    else:
        (k, m), (k2, n) = a.shape, bshape
    assert k == k2 and a.dtype == BF16 and b.dtype == BF16, (name, a.shape, b.shape, a.dtype, b.dtype)
    tm = tm if tm and m % tm == 0 else _pick(m, (2048, 1024, 512, 256))
    tn = _pick(n, (1152, 1024, 1408, 512, 256))
    tk = tk if tk and k % tk == 0 else _pick(k, (2048, 1408, 1152, 1024, 512, 256))
    if out3 or (b3 and mode == "nn"):
        tn = n // N_DEV
    if b3 and mode == "nt":
        tk = k // N_DEV
    nk = k // tk
    dims = {"nn": (((1,), (0,)), ((), ())), "nt": (((1,), (1,)), ((), ())), "tn": (((0,), (0,)), ((), ()))}[mode]
    n_extra = len(after) + (add is not None)

    def body(a_ref, b_ref, *rest):
        o_ref = rest[n_extra]
        prod = lax.dot_general(a_ref[...], b_ref[...], dims, preferred_element_type=F32)
        if nk == 1:
            o_ref[...] = (prod if add is None else prod + rest[0][...]).astype(o_ref.dtype)
            return
        acc_ref = rest[n_extra + 1]
        kk = pl.program_id(2)

        @pl.when(kk == 0)
        def _():
            acc_ref[...] = prod if add is None else prod + rest[0][...]

        @pl.when((kk > 0) & (kk < nk - 1))
        def _():
            acc_ref[...] += prod

        @pl.when(kk == nk - 1)
        def _():
            o_ref[...] = (acc_ref[...] + prod).astype(o_ref.dtype)

    a_spec = pl.BlockSpec((tk, tm), lambda i, j, kk: (kk, i)) if mode == "tn" else pl.BlockSpec((tm, tk), lambda i, j, kk: (i, kk))
    if b3:
        b_spec = (pl.BlockSpec((None, tn, tk), lambda i, j, kk: (kk, j, 0)) if mode == "nt"
                  else pl.BlockSpec((None, tk, tn), lambda i, j, kk: (j, kk, 0)))
    else:
        b_spec = pl.BlockSpec((tn, tk), lambda i, j, kk: (j, kk)) if mode == "nt" else pl.BlockSpec((tk, tn), lambda i, j, kk: (kk, j))
    tile = pl.BlockSpec((tm, tn), lambda i, j, kk: (i, j))
    out_spec = pl.BlockSpec((None, tm, tn), lambda i, j, kk: (j, i, 0)) if out3 else tile
    return pl.pallas_call(
        body, name=name, grid=(m // tm, n // tn, nk),
        in_specs=[a_spec, b_spec] + ([tile] if add is not None else []) + [ANY] * len(after), out_specs=out_spec,
        out_shape=jax.ShapeDtypeStruct((N_DEV, m, tn) if out3 else (m, n), out_dtype),
        scratch_shapes=[pltpu.VMEM((tm, tn), F32)] if nk > 1 else [],
        compiler_params=_params(("parallel", "parallel", "arbitrary")),
    )(a, b, *([add] if add is not None else []), *after)


def _rows(body, t, tr, ins, outs, name):
    in_specs = []
    for arr, sp in ins:
        if sp[0] == "t":
            in_specs.append(pl.BlockSpec((tr, sp[1]), lambda i, cb=sp[2]: (i, cb)))
        else:
            in_specs.append(pl.BlockSpec(arr.shape, lambda i, nd=arr.ndim: (0,) * nd))
    out_specs, out_shape = [], []
    for shape, dt, kind in outs:
        out_specs.append(pl.BlockSpec((tr, shape[1]), lambda i: (i, 0)) if kind == "t" else pl.BlockSpec(shape, lambda i: (0, 0)))
        out_shape.append(jax.ShapeDtypeStruct(shape, dt))
    return pl.pallas_call(
        body, name=name, grid=(t // tr,), in_specs=in_specs, out_specs=out_specs, out_shape=out_shape,
        compiler_params=_params(("arbitrary",)),
    )(*[arr for arr, _ in ins])


def _rinv(v):
    return lax.rsqrt(jnp.mean(v * v, axis=-1, keepdims=True) + EPS)


def _sig(v):
    return 1.0 / (1.0 + jnp.exp(-v))


def _acc(ref, val):
    @pl.when(pl.program_id(0) == 0)
    def _():
        ref[...] = jnp.zeros_like(ref)

    ref[...] += jnp.sum(val, axis=0, keepdims=True)


def _rms_fwd(x, g, name):
    t, d = x.shape

    def body(x_ref, g_ref, h_ref):
        xv = x_ref[...]
        h_ref[...] = (xv * _rinv(xv) * g_ref[...]).astype(BF16)

    return _rows(body, t, 256, [(x, ("t", d, 0)), (g, ("b",))], [((t, d), BF16, "t")], name)[0]


def _post_pre(x, m, g_post, g_pre, name):
    t, d = x.shape

    def body(x_ref, m_ref, gp_ref, gn_ref, xo_ref, h_ref):
        mv = m_ref[...]
        xn = x_ref[...] + mv * _rinv(mv) * gp_ref[...]
        xo_ref[...] = xn
        h_ref[...] = (xn * _rinv(xn) * gn_ref[...]).astype(BF16)

    return _rows(body, t, 128, [(x, ("t", d, 0)), (m, ("t", d, 0)), (g_post, ("b",)), (g_pre, ("b",))],
                 [((t, d), F32, "t"), ((t, d), BF16, "t")], name)


def _mix_fwd(proj, ya, yb, name):
    t, d = ya.shape

    def body(ga_ref, gb_ref, ya_ref, yb_ref, o_ref):
        o_ref[...] = (_sig(ga_ref[...].astype(F32)) * ya_ref[...].astype(F32)
                      + _sig(gb_ref[...].astype(F32)) * yb_ref[...].astype(F32)).astype(BF16)

    return _rows(body, t, 256, [(proj, ("t", d, C_GA // d)), (proj, ("t", d, C_GB // d)), (ya, ("t", d, 0)), (yb, ("t", d, 0))],
                 [((t, d), BF16, "t")], name)[0]


def _mix_bwd(dmix, proj, ya, yb, name):
    t, d = ya.shape

    def body(dm_ref, ga_ref, gb_ref, ya_ref, yb_ref, dg_ref, dya_ref, dyb_ref):
        dm = dm_ref[...]
        sa, sb = _sig(ga_ref[...].astype(F32)), _sig(gb_ref[...].astype(F32))
        dg_ref[:, :d] = (dm * ya_ref[...].astype(F32) * sa * (1.0 - sa)).astype(BF16)
        dg_ref[:, d:] = (dm * yb_ref[...].astype(F32) * sb * (1.0 - sb)).astype(BF16)
        dya_ref[...] = (dm * sa).astype(BF16)
        dyb_ref[...] = (dm * sb).astype(BF16)

    return _rows(body, t, 128,
                 [(dmix, ("t", d, 0)), (proj, ("t", d, C_GA // d)), (proj, ("t", d, C_GB // d)), (ya, ("t", d, 0)), (yb, ("t", d, 0))],
                 [((t, 2 * d), BF16, "t"), ((t, d), BF16, "t"), ((t, d), BF16, "t")], name)


def _swiglu_call(body, ins, n_out, name):
    t, f = ins[0].shape
    tc = _pick(f, (1408, 512))
    tile = pl.BlockSpec((512, tc), lambda i, j: (i, j))
    return pl.pallas_call(
        body, name=name, grid=(t // 512, f // tc), in_specs=[tile] * len(ins), out_specs=[tile] * n_out,
        out_shape=[jax.ShapeDtypeStruct((t, f), BF16)] * n_out, compiler_params=_params(("parallel", "parallel")),
    )(*ins)


def _swiglu_fwd(fg, fu, name):
    def body(g_ref, u_ref, s_ref):
        gv = g_ref[...].astype(F32)
        s_ref[...] = (gv * _sig(gv) * u_ref[...].astype(F32)).astype(BF16)

    return _swiglu_call(body, [fg, fu], 1, name)[0]


def _swiglu_bwd(ds, fg, fu, name):
    def body(ds_ref, g_ref, u_ref, dg_ref, du_ref):
        dsv, gv, uv = ds_ref[...].astype(F32), g_ref[...].astype(F32), u_ref[...].astype(F32)
        sg = _sig(gv)
        dg_ref[...] = (dsv * uv * sg * (1.0 + gv * (1.0 - sg))).astype(BF16)
        du_ref[...] = (dsv * gv * sg).astype(BF16)

    return _swiglu_call(body, [ds, fg, fu], 2, name)


def _ple_final(x2, pg, pp, tgt, g_post, name):
    t, d = x2.shape

    def body(x_ref, pg_ref, pp_ref, t_ref, g_ref, loss_ref, d3_ref, dpg_ref, dpp_ref, dg_ref):
        sg, ppv, g = _sig(pg_ref[...]), pp_ref[...], g_ref[...]
        e = sg * ppv
        r = _rinv(e)
        eh = e * r
        diff = x_ref[...] + eh * g - t_ref[...]
        loss_ref[...] = 0.5 * jnp.mean(diff * diff, axis=-1, keepdims=True)
        d3 = diff * (1.0 / d)
        d3_ref[...] = d3
        gd = d3 * g
        de = r * (gd - eh * jnp.mean(gd * eh, axis=-1, keepdims=True))
        dpg_ref[...] = (de * ppv * sg * (1.0 - sg)).astype(BF16)
        dpp_ref[...] = (de * sg).astype(BF16)
        _acc(dg_ref, d3 * eh)

    return _rows(body, t, 128, [(x2, ("t", d, 0)), (pg, ("t", d, 0)), (pp, ("t", d, 0)), (tgt, ("t", d, 0)), (g_post, ("b",))],
                 [((t, 1), F32, "t"), ((t, d), F32, "t"), ((t, d), BF16, "t"), ((t, d), BF16, "t"), ((1, d), F32, "a")], name)


def _norm_bwd(dn, dh, x, g_pre, fm, g_post, name):
    t, d = x.shape
    two = fm is not None

    def body(*refs):
        if two:
            dn_ref, dh_ref, x_ref, gp_ref, f_ref, gq_ref, dx_ref, df_ref, dgp_ref, dgq_ref = refs
        else:
            dn_ref, dh_ref, x_ref, gp_ref, dx_ref, dgp_ref = refs
        xv, dhv = x_ref[...], dh_ref[...]
        r = _rinv(xv)
        xh = xv * r
        gd = dhv * gp_ref[...]
        dx = dn_ref[...] + r * (gd - xh * jnp.mean(gd * xh, axis=-1, keepdims=True))
        dx_ref[...] = dx
        _acc(dgp_ref, dhv * xh)
        if two:
            fv = f_ref[...]
            rf = _rinv(fv)
            fh = fv * rf
            gd2 = dx * gq_ref[...]
            df_ref[...] = (rf * (gd2 - fh * jnp.mean(gd2 * fh, axis=-1, keepdims=True))).astype(BF16)
            _acc(dgq_ref, dx * fh)

    ins = [(dn, ("t", d, 0)), (dh, ("t", d, 0)), (x, ("t", d, 0)), (g_pre, ("b",))]
    outs = [((t, d), F32, "t")]
    if two:
        ins += [(fm, ("t", d, 0)), (g_post, ("b",))]
        outs += [((t, d), BF16, "t"), ((1, d), F32, "a"), ((1, d), F32, "a")]
    else:
        outs += [((1, d), F32, "a")]
    return _rows(body, t, 128, ins, outs, name)


CONV_TC = 256


def _shift_down(v, s):
    rows = lax.broadcasted_iota(jnp.int32, v.shape, 0)
    return jnp.where(rows >= s, pltpu.roll(v, s, 0), 0.0)


def _shift_up(v, s):
    n = v.shape[0]
    rows = lax.broadcasted_iota(jnp.int32, v.shape, 0)
    return jnp.where(rows < n - s, pltpu.roll(v, n - s, 0), 0.0)


def _conv_specs(t):
    nb = 1024 // CONV_TC
    seg = lambda c0: pl.BlockSpec((t, CONV_TC), lambda j, cb=c0 // CONV_TC: (0, cb + j))
    own = pl.BlockSpec((t, CONV_TC), lambda j: (0, j))
    wspec = pl.BlockSpec((3, CONV_TC), lambda j: (0, j))
    return nb, seg, own, wspec


def _conv_fwd(proj, conv_w, name, after=()):
    t = proj.shape[0]
    nb, seg, own, wspec = _conv_specs(t)

    def body(ax_ref, ab_ref, ac_ref, w_ref, *rest):
        za_ref = rest[len(after)]
        u = ac_ref[...].astype(F32) * ax_ref[...].astype(F32)
        w = w_ref[...]
        yc = w[0:1] * _shift_down(u, 2) + w[1:2] * _shift_down(u, 1) + w[2:3] * u
        za_ref[...] = (ab_ref[...].astype(F32) * yc).astype(BF16)

    return pl.pallas_call(
        body, name=name, grid=(nb,), in_specs=[seg(C_AX), seg(C_AB), seg(C_AC), wspec] + [ANY] * len(after), out_specs=own,
        out_shape=jax.ShapeDtypeStruct((t, 1024), BF16), compiler_params=_params(("parallel",)),
    )(proj, proj, proj, conv_w, *after)


def _conv_bwd(dza, proj, conv_w, name):
    t = proj.shape[0]
    nb, seg, own, wspec = _conv_specs(t)

    def body(dz_ref, ax_ref, ab_ref, ac_ref, w_ref, dax_ref, dab_ref, dac_ref, dw_ref):
        ax, ab, ac, dz = ax_ref[...].astype(F32), ab_ref[...].astype(F32), ac_ref[...].astype(F32), dz_ref[...].astype(F32)
        w = w_ref[...]
        u = ac * ax
        u1, u2 = _shift_down(u, 1), _shift_down(u, 2)
        yc = w[0:1] * u2 + w[1:2] * u1 + w[2:3] * u
        dab_ref[...] = (dz * yc).astype(BF16)
        dyc = dz * ab
        du = w[2:3] * dyc + w[1:2] * _shift_up(dyc, 1) + w[0:1] * _shift_up(dyc, 2)
        dax_ref[...] = (du * ac).astype(BF16)
        dac_ref[...] = (du * ax).astype(BF16)
        dw_ref[0:1, :] = jnp.sum(dyc * u2, axis=0, keepdims=True)
        dw_ref[1:2, :] = jnp.sum(dyc * u1, axis=0, keepdims=True)
        dw_ref[2:3, :] = jnp.sum(dyc * u, axis=0, keepdims=True)

    act = jax.ShapeDtypeStruct((t, 1024), BF16)
    return pl.pallas_call(
        body, name=name, grid=(nb,), in_specs=[own, seg(C_AX), seg(C_AB), seg(C_AC), wspec], out_specs=[own, own, own, wspec],
        out_shape=[act, act, act, jax.ShapeDtypeStruct((3, 1024), F32)], compiler_params=_params(("parallel",)),
    )(dza, proj, proj, proj, conv_w)


def _dot(a, b, dims, precision=None):
    return lax.dot_general(a, b, (dims, ((), ())), precision=precision, preferred_element_type=F32)


_CONTRACT = {"nn": ((1,), (0,)), "nt": ((1,), (1,)), "tn": ((0,), (0,))}


def _bdot_raw(a, b, mode):
    return _dot(a.astype(BF16), b.astype(BF16), _CONTRACT[mode])


@functools.partial(jax.custom_vjp, nondiff_argnums=(2,))
def _bdot(a, b, mode):
    return _bdot_raw(a, b, mode)


def _bdot_fwd(a, b, mode):
    return _bdot_raw(a, b, mode), (a, b)


def _bdot_bwd(mode, res, ct):
    a, b = res
    if mode == "nn":
        return _bdot_raw(ct, b, "nt"), _bdot_raw(a, ct, "tn")
    if mode == "nt":
        return _bdot_raw(ct, b, "nn"), _bdot_raw(ct, a, "tn")
    return _bdot_raw(b, ct, "nt"), _bdot_raw(a, ct, "nn")


_bdot.defvjp(_bdot_fwd, _bdot_bwd)


@functools.partial(jax.custom_vjp, nondiff_argnums=(2,))
def _sum_dot(ones, x, mode):
    head = x.astype(BF16)
    tail = x - head.astype(F32)
    if mode == "nn":
        return _bdot_raw(ones, head, "nn") + _bdot_raw(ones, tail, "nn")
    return _bdot_raw(head, ones, "tn") + _bdot_raw(tail, ones, "tn")


def _sum_dot_fwd(ones, x, mode):
    return _sum_dot(ones, x, mode), ones


def _sum_dot_bwd(mode, ones, ct):
    return jnp.zeros_like(ones), (_bdot_raw(ones, ct, "tn") if mode == "nn" else _bdot_raw(ones, ct, "nt"))


_sum_dot.defvjp(_sum_dot_fwd, _sum_dot_bwd)


def _gla_chunk(q, k, v, og, alr, s_in, wa, ba, gain):
    c = q.shape[0]
    z =_bdot(alr, wa, "nn") + ba
    la = (jnp.minimum(z, 0.0) - jnp.log(1.0 + jnp.exp(-jnp.abs(z)))) * (1.0 / TAU)
    row = lax.broadcasted_iota(jnp.int32, (c, c), 0)
    col = lax.broadcasted_iota(jnp.int32, (c, c), 1)
    lower = row >= col
    b = _sum_dot(lower.astype(F32), la, "nn")
    trow = lax.broadcasted_iota(jnp.int32, la.shape, 0)
    mid = jnp.sum(jnp.where(trow <= c // 2, la, 0.0), axis=0, keepdims=True)
    blast = jnp.sum(la, axis=0, keepdims=True)
    qs = q * (DK ** -0.5)
    e_up, e_dn = jnp.exp(b - mid), jnp.exp(mid - b)
    a_fwd = _bdot(qs * e_up, k * e_dn, "nt")
    a_rev = _bdot(qs * e_dn, k * e_up, "nt")
    att = jnp.where(lower, a_fwd, a_rev)
    o = _bdot(att, v, "nn") + _bdot(qs * jnp.exp(b), s_in, "nn")
    upd = _bdot(k * jnp.exp(blast - b), v, "tn")
    blast_col = _sum_dot(jnp.ones((c, DV), F32), la, "tn")
    s_out = jnp.exp(blast_col) * s_in + upd
    on = o * _rinv(o) * gain
    return on * og * _sig(og), s_out


def _gla_specs(t, rev):
    n = t // STEP_ROWS
    ch = (lambda i: n - 1 - i) if rev else (lambda i: i)
    col = lambda w, c0: pl.BlockSpec((STEP_ROWS, HEADS * w), lambda i, cb=c0 // (HEADS * w): (ch(i), cb))
    whole = lambda shape: pl.BlockSpec(shape, lambda i, nd=len(shape): (0,) * nd)
    specs = dict(
        q=col(DK, C_Q), k=col(DK, C_K), v=col(DV, C_V), og=col(DV, C_OG),
        alr=pl.BlockSpec((STEP_ROWS, 128), lambda i: (ch(i), C_ALR // 128)),
        wa=whole((128, HEADS * DK)), ba=whole((1, HEADS * DK)), gain=whole((1, DV)),
        state=pl.BlockSpec((STEP_CHUNKS, HEADS, DK, DV), lambda i: (ch(i), 0, 0, 0)),
        odk=pl.BlockSpec((STEP_ROWS, HEADS * DK), lambda i: (ch(i), 0)), odv=pl.BlockSpec((STEP_ROWS, HEADS * DV), lambda i: (ch(i), 0)),
        oalr=pl.BlockSpec((STEP_ROWS, 128), lambda i: (ch(i), 0)), whole=whole,
    )
    return n, specs


def _head_cols(h):
    return slice(h * DK, (h + 1) * DK), slice(h * DV, (h + 1) * DV)


def _gla_fwd(proj, wa, ba, gain, name):
    t = proj.shape[0]
    n, sp = _gla_specs(t, False)

    def body(q_ref, k_ref, v_ref, og_ref, alr_ref, wa_ref, ba_ref, g_ref, zb_ref, st_ref, s_scr):
        @pl.when(pl.program_id(0) == 0)
        def _():
            s_scr[...] = jnp.zeros_like(s_scr)

        state = [s_scr[h] for h in range(HEADS)]
        for c in range(STEP_CHUNKS):
            rows = slice(c * CHUNK, (c + 1) * CHUNK)
            alr = alr_ref[rows, :].astype(F32)
            for h in range(HEADS):
                kc, vc = _head_cols(h)
                st_ref[c, h] = state[h]
                zb, state[h] = _gla_chunk(q_ref[rows, kc].astype(F32), k_ref[rows, kc].astype(F32), v_ref[rows, vc].astype(F32),
                                          og_ref[rows, vc].astype(F32), alr, state[h], wa_ref[:, kc].astype(F32), ba_ref[:, kc], g_ref[...])
                zb_ref[rows, vc] = zb.astype(BF16)
        for h in range(HEADS):
            s_scr[h] = state[h]

    return pl.pallas_call(
        body, name=name, grid=(n,),
        in_specs=[sp["q"], sp["k"], sp["v"], sp["og"], sp["alr"], sp["wa"], sp["ba"], sp["gain"]],
        out_specs=[sp["odv"], sp["state"]],
        out_shape=[jax.ShapeDtypeStruct((t, HEADS * DV), BF16), jax.ShapeDtypeStruct((t // CHUNK, HEADS, DK, DV), F32)],
        scratch_shapes=[pltpu.VMEM((HEADS, DK, DV), F32)],
        compiler_params=_params(("arbitrary",)),
    )(proj, proj, proj, proj, proj, wa, ba, gain)


def _gla_bwd(dzb, proj, states, wa, ba, gain, name):
    t = proj.shape[0]
    n, sp = _gla_specs(t, True)

    def body(dz_ref, q_ref, k_ref, v_ref, og_ref, alr_ref, st_ref, wa_ref, ba_ref, g_ref,
             dq_ref, dk_ref, dv_ref, dog_ref, dalr_ref, dwa_ref, dba_ref, dg_ref, ds_scr):
        @pl.when(pl.program_id(0) == 0)
        def _():
            ds_scr[...] = jnp.zeros_like(ds_scr)
            dwa_ref[...] = jnp.zeros_like(dwa_ref)
            dba_ref[...] = jnp.zeros_like(dba_ref)
            dg_ref[...] = jnp.zeros_like(dg_ref)

        dstate = [ds_scr[h] for h in range(HEADS)]
        dwa_sum, dba_sum, dgain_sum = [None] * HEADS, [None] * HEADS, None
        for c in reversed(range(STEP_CHUNKS)):
            rows = slice(c * CHUNK, (c + 1) * CHUNK)
            alr = alr_ref[rows, :].astype(F32)
            dalr_sum = None
            for h in range(HEADS):
                kc, vc = _head_cols(h)
                args = (q_ref[rows, kc].astype(F32), k_ref[rows, kc].astype(F32), v_ref[rows, vc].astype(F32), og_ref[rows, vc].astype(F32),
                        alr, st_ref[c, h], wa_ref[:, kc].astype(F32), ba_ref[:, kc], g_ref[...])
                _, vjp = jax.vjp(_gla_chunk, *args)
                dq, dk, dv, dog, dalr, dstate[h], dwa, dba, dgain = vjp((dz_ref[rows, vc].astype(F32), dstate[h]))
                dq_ref[rows, kc] = dq.astype(BF16)
                dk_ref[rows, kc] = dk.astype(BF16)
                dv_ref[rows, vc] = dv.astype(BF16)
                dog_ref[rows, vc] = dog.astype(BF16)
                dwa_sum[h] = dwa if dwa_sum[h] is None else dwa_sum[h] + dwa
                dba_sum[h] = dba if dba_sum[h] is None else dba_sum[h] + dba
                dalr_sum = dalr if dalr_sum is None else dalr_sum + dalr
                dgain_sum = dgain if dgain_sum is None else dgain_sum + dgain
            dalr_ref[rows, :] = dalr_sum
        for h in range(HEADS):
            ds_scr[h] = dstate[h]
            dwa_ref[h] += dwa_sum[h]
            dba_ref[h] += dba_sum[h]
        dg_ref[...] += dgain_sum

    whole = sp["whole"]
    return pl.pallas_call(
        body, name=name, grid=(n,),
        in_specs=[sp["odv"], sp["q"], sp["k"], sp["v"], sp["og"], sp["alr"], sp["state"], sp["wa"], sp["ba"], sp["gain"]],
        out_specs=[sp["odk"], sp["odk"], sp["odv"], sp["odv"], sp["oalr"], whole((HEADS, 128, DK)), whole((HEADS, 1, DK)), whole((1, DV))],
        out_shape=[jax.ShapeDtypeStruct((t, HEADS * DK), BF16), jax.ShapeDtypeStruct((t, HEADS * DK), BF16),
                   jax.ShapeDtypeStruct((t, HEADS * DV), BF16), jax.ShapeDtypeStruct((t, HEADS * DV), BF16),
                   jax.ShapeDtypeStruct((t, 128), F32), jax.ShapeDtypeStruct((HEADS, 128, DK), F32),
                   jax.ShapeDtypeStruct((HEADS, 1, DK), F32), jax.ShapeDtypeStruct((1, DV), F32)],
        scratch_shapes=[pltpu.VMEM((HEADS, DK, DV), F32)],
        compiler_params=_params(("arbitrary",)),
    )(dzb, proj, proj, proj, proj, proj, states, wa, ba, gain)


def _local_step(x, p, tgt, gather_start, gather_finish, scatter_start, small):
    b_alpha, gain = small["b_alpha_up"], small["gla_head_gain"]
    gather_start(0, ())
    w = dict(gather_finish(0, ()))
    conv_w, w_alpha = w["conv_w"], w["w_alpha_up"]
    wa_p = jnp.zeros((128, HEADS * DK), BF16).at[:GATE_RANK].set(w_alpha.astype(BF16))

    t2 = gather_start(2, (w["in_t"], gather_start(1, ())))
    h1 = _rms_fwd(x, small["g_pre_mix"], "rms_pre_mix")
    proj = _mm(h1, w["in_t"], "nt", BF16, "mm_proj", after=(t2,))
    za = _conv_fwd(proj, conv_w, "conv_fwd")
    zb, states = _gla_fwd(proj, wa_p, b_alpha, gain, "gla_fwd")
    t3 = gather_start(3, (zb, za))
    w.update(gather_finish(1, (t3,)))
    ya = _mm(za, w["a_out"], "nn", BF16, "mm_ya", b3=True, tm=2048, tk=1024)
    yb = _mm(zb, w["b_out"], "nn", BF16, "mm_yb", b3=True, tm=2048, tk=1024)
    mix = _mix_fwd(proj, ya, yb, "mix_fwd")
    m2 = _mm(mix, w["mix"], "nn", F32, "mm_mix")
    t4 = gather_start(4, (m2,))
    x1, h2 = _post_pre(x, m2, small["g_post_mix"], small["g_pre_ffn"], "norm_mix_ffn")
    w.update(gather_finish(2, (h2, t4)))
    fg = _mm(h2, w["gate_t"], "nt", BF16, "mm_gate")
    fu = _mm(h2, w["up_t"], "nt", BF16, "mm_up")
    s = _swiglu_fwd(fg, fu, "swiglu_fwd")
    w.update(gather_finish(3, (s,)))
    f = _mm(s, w["down"], "nn", F32, "mm_down")
    x2, h3 = _post_pre(x1, f, small["g_post_ffn"], small["g_pre_ple"], "norm_ffn_ple")
    w.update(gather_finish(4, (h3,)))
    pg = _mm(h3, w["pg"], "nn", F32, "mm_pg")
    p_bf = p.astype(BF16)
    pp = _mm(p_bf, w["pp"], "nn", F32, "mm_pp", b3=True, tm=2048)
    loss_rows, d3, dpg, dpp, dg_post_ple = _ple_final(x2, pg, pp, tgt, small["g_post_ple"], "ple_final")

    gw = {}
    gw["pp"] = _mm(p_bf, dpp, "tn", BF16, "mm_dw_pp", out3=True)
    gw["pg"] = _mm(h3, dpg, "tn", BF16, "mm_dw_pg")
    dh3 = _mm(dpg, w["pg"], "nt", F32, "mm_dh3", after=(scatter_start(0, gw),))
    d2, df, dg_pre_ple, dg_post_ffn = _norm_bwd(d3, dh3, x2, small["g_pre_ple"], f, small["g_post_ffn"], "norm_bwd_ple_ffn")
    ds = _mm(df, w["down"], "nt", BF16, "mm_ds")
    gw["down"] = _mm(s, df, "tn", BF16, "mm_dw_down", tm=1408)
    dfg, dfu = _swiglu_bwd(ds, fg, fu, "swiglu_bwd")
    gw["gate_t"] = _mm(dfg, h2, "tn", BF16, "mm_dw_gate", after=(gw["down"],), tm=1408)
    gw["up_t"] = _mm(dfu, h2, "tn", BF16, "mm_dw_up", after=(gw["gate_t"],), tm=1408)
    dh2 = _mm(dfg, w["gate_t"], "nn", F32, "mm_dh2_gate", after=(scatter_start(1, gw),))
    dh2 = _mm(dfu, w["up_t"], "nn", F32, "mm_dh2_up", add=dh2, tm=1024)
    d1, dm2, dg_pre_ffn, dg_post_mix = _norm_bwd(d2, dh2, x1, small["g_pre_ffn"], m2, small["g_post_mix"], "norm_bwd_ffn_mix")
    dmix = _mm(dm2, w["mix"], "nt", F32, "mm_dmix")
    gw["mix"] = _mm(mix, dm2, "tn", BF16, "mm_dw_mix")
    dgab, dya, dyb = _mix_bwd(dmix, proj, ya, yb, "mix_bwd")
    dza = _mm(dya, w["a_out"], "nt", BF16, "mm_dza", after=(gw["mix"],), b3=True, tm=2048)
    gw["a_out"] = _mm(za, dya, "tn", BF16, "mm_dw_a_out", out3=True, tk=1024)
    gw["b_out"] = _mm(zb, dyb, "tn", BF16, "mm_dw_b_out", after=(gw["a_out"],), out3=True, tk=1024)
    dzb = _mm(dyb, w["b_out"], "nt", BF16, "mm_dzb", after=(scatter_start(2, gw),), b3=True, tm=2048)
    dax, dab, dac, dconv = _conv_bwd(dza, proj, conv_w, "conv_bwd")
    dq, dk, dv, dog, dalr, dwa, dba, dgain = _gla_bwd(dzb, proj, states, wa_p, b_alpha, gain, "gla_bwd")
    dproj = jnp.concatenate([dgab, dax, dab, dac, dq, dk, dv, dog, dalr.astype(BF16)], axis=1)
    gw["in_t"] = _mm(dproj, h1, "tn", BF16, "mm_dw_in", tm=1152)
    dh1 = _mm(dproj, w["in_t"], "nn", F32, "mm_dh1", after=(scatter_start(3, gw),))
    grad_x, dg_pre_mix = _norm_bwd(d1, dh1, x, small["g_pre_mix"], None, None, "norm_bwd_mix")

    gs = dict(
        conv_w=dconv,
        w_alpha_up=jnp.transpose(dwa[:, :GATE_RANK, :], (1, 0, 2)).reshape(GATE_RANK, HEADS * DK),
        b_alpha_up=dba.reshape(1, HEADS * DK), gla_head_gain=dgain,
        g_pre_mix=dg_pre_mix, g_post_mix=dg_post_mix, g_pre_ffn=dg_pre_ffn, g_post_ffn=dg_post_ffn,
        g_pre_ple=dg_pre_ple, g_post_ple=dg_post_ple,
    )
    return loss_rows, grad_x, gs


def _place():
    x, y, c = lax.axis_index("x"), lax.axis_index("y"), lax.axis_index("c")
    return x, y, c, [(1 - x, y), (x, 1 - y), (1 - x, 1 - y)]


def _all_gather(shards, name, cid=None):
    n = len(shards)

    def body(*refs):
        ins, outs = refs[:n], refs[n:2 * n]
        send_sems, recv_sems, local_sems = refs[2 * n:]
        x, y, c, chips = _place()
        me, sibling = (x, y, c), (x, y, 1 - c)

        def slot(px, py, pc):
            return 4 * px + 2 * py + pc

        def copy(a, k, block, to, src=None):
            dst = outs[a].at[slot(*block)]
            return pltpu.make_async_remote_copy(src_ref=dst if src is None else src, dst_ref=dst, send_sem=send_sems.at[a, k],
                                                recv_sem=recv_sems.at[a, k], device_id=to, device_id_type=MESH)

        mine = [pltpu.make_async_copy(ins[a], outs[a].at[slot(*me)], local_sems.at[a]) for a in range(n)]
        for cp in mine:
            cp.start()
        first = []
        for j, chip in enumerate(chips):
            first += [copy(a, 1 + j, me, (*chip, c), src=ins[a]) for a in range(n)]
        first += [copy(a, 0, me, sibling, src=ins[a]) for a in range(n)]
        for cp in first:
            cp.start()
        passed = []
        for j, chip in enumerate(chips):
            for a in range(n):
                copy(a, 1 + j, (*chip, c), me).wait_recv()
                cp = copy(a, 4 + j, (*chip, c), sibling)
                cp.start()
                passed.append(cp)
        for a in range(n):
            copy(a, 0, sibling, me).wait_recv()
        for j, chip in enumerate(chips):
            for a in range(n):
                copy(a, 4 + j, (*chip, 1 - c), me).wait_recv()
        for cp in first + passed:
            cp.wait_send()
        for cp in mine:
            cp.wait()

    if cid is None:
        return pl.pallas_call(
            body, name=name, in_specs=[ANY] * n, out_specs=[ANY] * n,
            out_shape=[jax.ShapeDtypeStruct((N_DEV,) + s.shape, s.dtype) for s in shards],
            scratch_shapes=[pltpu.SemaphoreType.DMA((n, 7)), pltpu.SemaphoreType.DMA((n, 7)), pltpu.SemaphoreType.DMA((n,))],
        )(*shards)

    src = [jax.new_ref(s, memory_space=pltpu.MemorySpace.HBM) for s in shards]
    dst = [jax.empty_ref(jax.ShapeDtypeStruct((N_DEV,) + s.shape, s.dtype), memory_space=pltpu.MemorySpace.HBM) for s in shards]

    @pl.kernel(mesh=plsc.ScalarSubcoreMesh(axis_name="seq", num_cores=1), name=name,
               scratch_types=(pltpu.SemaphoreType.DMA((n, 7)), pltpu.SemaphoreType.DMA((n, 7)), pltpu.SemaphoreType.DMA((n,))),
               compiler_params=pltpu.CompilerParams(collective_id=cid))
    def launch(send_sems, recv_sems, local_sems):
        x, y, c, chips = _place()
        barrier = pltpu.get_barrier_semaphore()
        for peer in [(x, y, 1 - c)] + [(*chip, c) for chip in chips]:
            pl.semaphore_signal(barrier, inc=1, device_id=peer, device_id_type=MESH)
        pl.semaphore_wait(barrier, 4)
        body(*src, *dst, send_sems, recv_sems, local_sems)

    launch()
    return [r[...] for r in dst]


def _reduce_scatter(parts, name, cid):
    n = len(parts)
    src = [jax.new_ref(s, memory_space=pltpu.MemorySpace.HBM) for s in parts]
    dst = [jax.empty_ref(jax.ShapeDtypeStruct(s.shape, s.dtype), memory_space=pltpu.MemorySpace.HBM) for s in parts]

    @pl.kernel(mesh=plsc.ScalarSubcoreMesh(axis_name="seq", num_cores=1), name=name,
               scratch_types=(pltpu.SemaphoreType.DMA((n, N_DEV - 1)), pltpu.SemaphoreType.DMA((n, N_DEV - 1)), pltpu.SemaphoreType.DMA((n,))),
               compiler_params=pltpu.CompilerParams(collective_id=cid))
    def launch(send_sems, recv_sems, local_sems):
        x, y, c, _ = _place()
        me = 4 * x + 2 * y + c
        peers = [(1 - x if k & 4 else x, 1 - y if k & 2 else y, 1 - c if k & 1 else c) for k in range(1, N_DEV)]
        barrier = pltpu.get_barrier_semaphore()
        for peer in peers:
            pl.semaphore_signal(barrier, inc=1, device_id=peer, device_id_type=MESH)
        pl.semaphore_wait(barrier, N_DEV - 1)
        mine = [pltpu.make_async_copy(src[a].at[me], dst[a].at[me], local_sems.at[a]) for a in range(n)]
        for cp in mine:
            cp.start()
        cps = []
        for a in range(n):
            for k, (px, py, pc) in enumerate(peers):
                cps.append(pltpu.make_async_remote_copy(src_ref=src[a].at[4 * px + 2 * py + pc], dst_ref=dst[a].at[me], send_sem=send_sems.at[a, k],
                                                        recv_sem=recv_sems.at[a, k], device_id=(px, py, pc), device_id_type=MESH))
        for cp in cps:
            cp.start()
        for cp in cps:
            cp.wait_recv()
        for cp in cps:
            cp.wait_send()
        for cp in mine:
            cp.wait()

    launch()
    return [r[...] for r in dst]


def _sibling_exchange(parts, name):
    n = len(parts)
    pieces = [_row_pieces(s.shape[1]) for s in parts]

    def body(*refs):
        ins, outs = refs[:n], refs[n:2 * n]
        send_sems, recv_sems = refs[2 * n:]
        x, y, c, _ = _place()

        def copy(a, ch, q, rows):
            return pltpu.make_async_remote_copy(src_ref=ins[a].at[2 * ch + 1 - c, rows], dst_ref=outs[a].at[ch, rows], send_sem=send_sems.at[a, ch, q],
                                                recv_sem=recv_sems.at[a, ch, q], device_id=(x, y, 1 - c), device_id_type=MESH)

        cps = [copy(a, ch, q, rows) for ch in range(4) for a in range(n) for q, rows in enumerate(pieces[a])]
        for cp in cps:
            cp.start()
        for cp in cps:
            cp.wait_recv()
        for cp in cps:
            cp.wait_send()

    return pl.pallas_call(
        body, name=name, in_specs=[ANY] * n, out_specs=[ANY] * n,
        out_shape=[jax.ShapeDtypeStruct((4,) + s.shape[1:], s.dtype) for s in parts],
        scratch_shapes=[pltpu.SemaphoreType.DMA((n, 4, PIECES)), pltpu.SemaphoreType.DMA((n, 4, PIECES))],
    )(*parts)


def _chip_exchange(parts, name):
    n = len(parts)

    def body(*refs):
        ins, outs = refs[:n], refs[n:2 * n]
        send_sems, recv_sems, local_sems = refs[2 * n:]
        x, y, c, chips = _place()
        my_chip = 2 * x + y

        def copy(a, j):
            px, py = chips[j]
            return pltpu.make_async_remote_copy(src_ref=ins[a].at[2 * px + py], dst_ref=outs[a].at[my_chip], send_sem=send_sems.at[a, j],
                                                recv_sem=recv_sems.at[a, j], device_id=(px, py, c), device_id_type=MESH)

        def landing(a, j):
            px, py = chips[j]
            return pltpu.make_async_remote_copy(src_ref=ins[a].at[my_chip], dst_ref=outs[a].at[2 * px + py], send_sem=send_sems.at[a, j],
                                                recv_sem=recv_sems.at[a, j], device_id=(px, py, c), device_id_type=MESH)

        mine = [pltpu.make_async_copy(ins[a].at[my_chip], outs[a].at[my_chip], local_sems.at[a]) for a in range(n)]
        for cp in mine:
            cp.start()
        cps = [copy(a, j) for j in range(3) for a in range(n)]
        for cp in cps:
            cp.start()
        for j in range(3):
            for a in range(n):
                landing(a, j).wait_recv()
        for cp in cps:
            cp.wait_send()
        for cp in mine:
            cp.wait()

    return pl.pallas_call(
        body, name=name, in_specs=[ANY] * n, out_specs=[ANY] * n,
        out_shape=[jax.ShapeDtypeStruct(s.shape, s.dtype) for s in parts],
        scratch_shapes=[pltpu.SemaphoreType.DMA((n, 3)), pltpu.SemaphoreType.DMA((n, 3)), pltpu.SemaphoreType.DMA((n,))],
    )(*parts)


def _pair_add(mine8, got4, name):
    _, r, cols = mine8.shape
    steps, blk, at = _tiles(r, cols)
    core = lax.axis_index("c").astype(jnp.int32).reshape(1)

    def body(c_ref, a_ref, b_ref, o_ref):
        o_ref[...] = (a_ref[...].astype(F32) + b_ref[...].astype(F32)).astype(BF16)

    return pl.pallas_call(
        body, name=name,
        grid_spec=pltpu.PrefetchScalarGridSpec(
            num_scalar_prefetch=1, grid=(4, steps),
            in_specs=[pl.BlockSpec((None,) + blk, lambda ch, i, c_ref: (2 * ch + c_ref[0],) + at(i)),
                      pl.BlockSpec((None,) + blk, lambda ch, i, c_ref: (ch,) + at(i))],
            out_specs=pl.BlockSpec((None,) + blk, lambda ch, i, c_ref: (ch,) + at(i))),
        out_shape=jax.ShapeDtypeStruct((4, r, cols), BF16),
        compiler_params=_params(("parallel", "parallel")),
    )(core, mine8, got4)


HBM = pl.BlockSpec(memory_space=pltpu.HBM)
SEM = pl.BlockSpec(memory_space=pltpu.SEMAPHORE)
EFFECT = pltpu.SideEffectType.DATAFLOW_SIDE_EFFECTING


def _in_hbm(a):
    return pltpu.with_memory_space_constraint(a, pltpu.HBM)


def _remote_copies(plan, srcs, lands, send_sems, recv_sems):
    return [pltpu.make_async_remote_copy(src_ref=s, dst_ref=d, send_sem=send_sems.at[i], recv_sem=recv_sems.at[i], device_id=peer,
                                         device_id_type=MESH) for i, (s, d, peer) in enumerate(plan(srcs, lands))]


def _copies_start(plan, n_copies, srcs, land_shapes, name, after=()):
    ns, nl = len(srcs), len(land_shapes)

    def body(*refs):
        send_sems, recv_sems = refs[ns + nl + len(after):ns + nl + len(after) + 2]
        for cp in _remote_copies(plan, refs[:ns], refs[ns:ns + nl], send_sems, recv_sems):
            cp.start()
        refs[-1][...] = jnp.zeros((8, 128), F32)

    sems = pltpu.SemaphoreType.DMA((n_copies,))
    return pl.pallas_call(
        body, name=name,
        out_shape=(sems, sems, *[pltpu.HBM(s.shape, s.dtype) for s in srcs], *[pltpu.HBM(s.shape, s.dtype) for s in land_shapes],
                   jax.ShapeDtypeStruct((8, 128), F32)),
        in_specs=[HBM] * (ns + nl) + [ANY] * len(after),
        out_specs=(SEM, SEM, *[HBM] * (ns + nl), pl.BlockSpec(memory_space=pltpu.VMEM)),
        input_output_aliases={i: 2 + i for i in range(ns + nl)},
        compiler_params=pltpu.CompilerParams(has_side_effects=EFFECT),
    )(*[_in_hbm(s) for s in srcs], *[_in_hbm(lax.empty(s.shape, s.dtype)) for s in land_shapes], *after)


def _copies_wait(plan, state, ns, name, after=()):
    send_sems, recv_sems, *arrs = state[:-1]
    n = len(arrs)

    def body(*refs):
        cps = _remote_copies(plan, refs[:ns], refs[ns:n], refs[n], refs[n + 1])
        for cp in cps:
            cp.wait_send()
        for cp in cps:
            cp.wait_recv()

    out = pl.pallas_call(
        body, name=name, out_shape=tuple(pltpu.HBM(a.shape, a.dtype) for a in arrs),
        in_specs=[HBM] * n + [SEM, SEM] + [ANY] * len(after), out_specs=tuple([HBM] * n),
        input_output_aliases={i: i for i in range(n)},
        compiler_params=pltpu.CompilerParams(has_side_effects=EFFECT),
    )(*arrs, send_sems, recv_sems, *after)
    return list(out[:ns]), list(out[ns:])


def _gather_plan(srcs, lands):
    x, y, c, chips = _place()
    peers = [(x, y, 1 - c)] + [(*chip, c) for chip in chips]
    return [(s, l.at[4 * x + 2 * y + c], peer) for s, l in zip(srcs, lands) for peer in peers]


def _gather_plan_near(srcs, lands):
    x, y, c, _ = _place()
    peers = [(x, y, 1 - c), (1 - x, y, c), (x, 1 - y, c)]
    return [(s, l.at[4 * x + 2 * y + c], peer) for s, l in zip(srcs, lands) for peer in peers]


def _scatter_plan(srcs, lands):
    x, y, c, _ = _place()
    peers = [(1 - x if k & 4 else x, 1 - y if k & 2 else y, 1 - c if k & 1 else c) for k in range(1, N_DEV)]
    return [(s.at[4 * px + 2 * py + pc], l.at[4 * x + 2 * y + c], (px, py, pc)) for s, l in zip(srcs, lands) for px, py, pc in peers]


def _everyone_plan(srcs, lands):
    x, y, c, _ = _place()
    peers = [(1 - x if k & 4 else x, 1 - y if k & 2 else y, 1 - c if k & 1 else c) for k in range(1, N_DEV)]
    return [(s, l.at[4 * x + 2 * y + c], peer) for s, l in zip(srcs, lands) for peer in peers]


def _sum_parts(got, own, me, name):
    def body(me_ref, got_ref, own_ref, o_ref):
        acc = jnp.where(me_ref[0] == 0, own_ref[...], got_ref[0])
        for d in range(1, N_DEV):
            acc = acc + jnp.where(me_ref[0] == d, own_ref[...], got_ref[d])
        o_ref[...] = acc

    return pl.pallas_call(
        body, name=name,
        grid_spec=pltpu.PrefetchScalarGridSpec(
            num_scalar_prefetch=1, grid=(1,),
            in_specs=[pl.BlockSpec(got.shape, lambda i, me_ref: (0, 0, 0)), pl.BlockSpec(own.shape, lambda i, me_ref: (0, 0))],
            out_specs=pl.BlockSpec(own.shape, lambda i, me_ref: (0, 0))),
        out_shape=jax.ShapeDtypeStruct(own.shape, F32),
    )(me.astype(jnp.int32).reshape(1), got, own)


def _chip_plan(srcs, lands):
    x, y, c, chips = _place()
    return [(s.at[2 * px + py], l.at[2 * x + y], (px, py, c)) for s, l in zip(srcs, lands) for px, py in chips]


PIECES = 8


def _row_pieces(rows):
    for k in (PIECES, 4, 2):
        if rows % (16 * k) == 0:
            return [pl.ds(q * (rows // k), rows // k) for q in range(k)]
    return [pl.ds(0, rows)]


def _put_own(shard, zone, me, name):
    r, c = shard.shape
    tr = r if r <= 256 else _pick(r, (256, 64))

    def body(me_ref, s_ref, z_ref, o_ref):
        o_ref[...] = s_ref[...]

    return pl.pallas_call(
        body, name=name,
        grid_spec=pltpu.PrefetchScalarGridSpec(
            num_scalar_prefetch=1, grid=(r // tr,),
            in_specs=[pl.BlockSpec((tr, c), lambda i, me_ref: (i, 0)), ANY],
            out_specs=pl.BlockSpec((None, tr, c), lambda i, me_ref: (me_ref[0], i, 0))),
        out_shape=jax.ShapeDtypeStruct(zone.shape, zone.dtype), input_output_aliases={2: 0},
        compiler_params=_params(("arbitrary",)),
    )(me.astype(jnp.int32).reshape(1), shard, zone)


def _gather_finish(lands, name):
    n = len(lands)

    def body(*refs):
        zones, outs = refs[:n], refs[n:2 * n]
        send_sems, recv_sems = refs[2 * n:]
        x, y, c, chips = _place()
        cps = [pltpu.make_async_remote_copy(
            src_ref=zones[a].at[4 * px + 2 * py + c], dst_ref=outs[a].at[4 * px + 2 * py + c], send_sem=send_sems.at[a, j],
            recv_sem=recv_sems.at[a, j], device_id=(x, y, 1 - c), device_id_type=MESH) for j, (px, py) in enumerate(chips) for a in range(n)]
        for cp in cps:
            cp.start()
        for cp in cps:
            cp.wait_recv()
        for cp in cps:
            cp.wait_send()

    return pl.pallas_call(
        body, name=name, in_specs=[ANY] * n, out_specs=[ANY] * n,
        out_shape=[jax.ShapeDtypeStruct(l.shape, l.dtype) for l in lands],
        input_output_aliases={a: a for a in range(n)},
        scratch_shapes=[pltpu.SemaphoreType.DMA((n, 3)), pltpu.SemaphoreType.DMA((n, 3))],
    )(*lands)


def _gather_relay(lands, name):
    n = len(lands)

    def body(*refs):
        zones, outs = refs[:n], refs[n:2 * n]
        send_sems, recv_sems = refs[2 * n:]
        x, y, c, _ = _place()
        south = c == 0
        near_x, near_y, across = 4 * (1 - x) + 2 * y + c, 4 * x + 2 * (1 - y) + c, 4 * (1 - x) + 2 * (1 - y) + c
        passed = jnp.where(south, near_y, near_x)
        onward = (jnp.where(south, 1 - x, x), jnp.where(south, y, 1 - y), c)

        def copy(a, k, slot, to):
            return pltpu.make_async_remote_copy(src_ref=zones[a].at[slot], dst_ref=outs[a].at[slot], send_sem=send_sems.at[a, k],
                                                recv_sem=recv_sems.at[a, k], device_id=to, device_id_type=MESH)

        first = [copy(a, 0, passed, onward) for a in range(n)]
        first += [copy(a, 1 + j, slot, (x, y, 1 - c)) for j, slot in enumerate((near_x, near_y)) for a in range(n)]
        for cp in first:
            cp.start()
        last = []
        for a in range(n):
            copy(a, 0, across, onward).wait_recv()
            last.append(copy(a, 3, across, (x, y, 1 - c)))
            last[-1].start()
        for cp in first[n:] + last:
            cp.wait_recv()
        for cp in first + last:
            cp.wait_send()

    return pl.pallas_call(
        body, name=name, in_specs=[ANY] * n, out_specs=[ANY] * n,
        out_shape=[jax.ShapeDtypeStruct(l.shape, l.dtype) for l in lands],
        input_output_aliases={a: a for a in range(n)},
        scratch_shapes=[pltpu.SemaphoreType.DMA((n, 4)), pltpu.SemaphoreType.DMA((n, 4))],
    )(*lands)


def _sum_everywhere(v, name):
    rows = v.shape[0]

    def body(v_ref, o_ref, buf, send_sems, recv_sems):
        x, y, c, _ = _place()
        me = 4 * x + 2 * y + c
        buf[me] = v_ref[...]
        cps = []
        for k in range(1, N_DEV):
            fx, fy, fc = (k >> 2) & 1, (k >> 1) & 1, k & 1
            to = (1 - x if fx else x, 1 - y if fy else y, 1 - c if fc else c)
            cps.append(pltpu.make_async_remote_copy(src_ref=buf.at[me], dst_ref=buf.at[me], send_sem=send_sems.at[k - 1],
                                                    recv_sem=recv_sems.at[k - 1], device_id=to, device_id_type=MESH))
        for cp in cps:
            cp.start()
        for cp in cps:
            cp.wait_recv()
        for cp in cps:
            cp.wait_send()
        acc = buf[0]
        for d in range(1, N_DEV):
            acc = acc + buf[d]
        o_ref[...] = acc

    vm = pl.BlockSpec(memory_space=pltpu.VMEM)
    return pl.pallas_call(
        body, name=name, in_specs=[vm], out_specs=vm, out_shape=jax.ShapeDtypeStruct(v.shape, F32),
        scratch_shapes=[pltpu.VMEM((N_DEV, rows, 128), F32), pltpu.SemaphoreType.DMA((N_DEV - 1,)), pltpu.SemaphoreType.DMA((N_DEV - 1,))],
    )(v)


def _adamw_parts(w, got, mine, me, m, v, name, after=()):
    r, c = w.shape
    n_parts = got.shape[0]
    steps, blk, at = _tiles(r, c)

    def body(me_ref, w_ref, got_ref, own_ref, m_ref, v_ref, *rest):
        go_ref, d_ref, mo_ref, vo_ref = rest[len(after):]
        own = own_ref[...].astype(F32)
        gv = jnp.where(me_ref[0] == 0, own, got_ref[0].astype(F32))
        for d in range(1, n_parts):
            gv = gv + jnp.where(me_ref[0] == d, own, got_ref[d].astype(F32))
        _adamw_math(gv, w_ref, m_ref, v_ref, go_ref, d_ref, mo_ref, vo_ref)

    tile = pl.BlockSpec(blk, lambda i, me_ref: at(i))
    out = jax.ShapeDtypeStruct((r, c), F32)
    return pl.pallas_call(
        body, name=name,
        grid_spec=pltpu.PrefetchScalarGridSpec(
            num_scalar_prefetch=1, grid=(steps,),
            in_specs=[tile, pl.BlockSpec((n_parts,) + blk, lambda i, me_ref: (0,) + at(i)),
                      pl.BlockSpec((None,) + blk, lambda i, me_ref: (me_ref[0],) + at(i)), tile, tile] + [ANY] * len(after),
            out_specs=[tile] * 4),
        out_shape=[out] * 4, compiler_params=_params(("parallel",)),
    )(me.astype(jnp.int32).reshape(1), w, got, mine, m, v, *after)


def _adamw_math(gv, w_ref, m_ref, v_ref, go_ref, d_ref, mo_ref, vo_ref):
    mn = B1 * m_ref[...] + (1.0 - B1) * gv
    vn = B2 * v_ref[...] + (1.0 - B2) * (gv * gv)
    m_hat = mn / (1.0 - B1 ** STEP)
    v_hat = vn / (1.0 - B2 ** STEP)
    go_ref[...] = gv
    d_ref[...] = -LR * (m_hat / (jnp.sqrt(v_hat) + ADAM_EPS) + WD * w_ref[...])
    mo_ref[...] = mn
    vo_ref[...] = vn


def _adamw(w, g, m, v, name):
    r, c = w.shape
    parts = g.ndim == 3
    tr = r if r <= 128 else _pick(r, (128, 64))

    def body(w_ref, g_ref, m_ref, v_ref, go_ref, d_ref, mo_ref, vo_ref):
        if parts:
            gv = g_ref[0].astype(F32)
            for d in range(1, g.shape[0]):
                gv = gv + g_ref[d].astype(F32)
        else:
            gv = g_ref[...]
        mn = B1 * m_ref[...] + (1.0 - B1) * gv
        vn = B2 * v_ref[...] + (1.0 - B2) * (gv * gv)
        m_hat = mn / (1.0 - B1 ** STEP)
        v_hat = vn / (1.0 - B2 ** STEP)
        go_ref[...] = gv
        d_ref[...] = -LR * (m_hat / (jnp.sqrt(v_hat) + ADAM_EPS) + WD * w_ref[...])
        mo_ref[...] = mn
        vo_ref[...] = vn

    tile = pl.BlockSpec((tr, c), lambda i: (i, 0))
    g_spec = pl.BlockSpec((g.shape[0], tr, c), lambda i: (0, i, 0)) if parts else tile
    out = jax.ShapeDtypeStruct((r, c), F32)
    return pl.pallas_call(
        body, name=name, grid=(r // tr,), in_specs=[tile, g_spec, tile, tile], out_specs=[tile] * 4, out_shape=[out] * 4,
        compiler_params=_params(("parallel",)),
    )(w, g, m, v)


BIG = ["w_in", "w_a_out", "w_b_out", "w_mix_out", "w_ff_gate", "w_ff_up", "w_ff_down", "w_ple_gate", "w_ple_proj"]
TRANSPOSED = ["w_in", "w_ff_gate", "w_ff_up"]
SMALL = ["conv_w", "w_alpha_up", "b_alpha_up", "gla_head_gain", "g_pre_mix", "g_post_mix", "g_pre_ffn", "g_post_ffn", "g_pre_ple", "g_post_ple"]
WEIGHTS = ["w_in", "conv_w", "w_a_out", "w_alpha_up", "b_alpha_up", "gla_head_gain", "w_b_out", "w_mix_out", "g_pre_mix", "g_post_mix",
           "g_pre_ffn", "g_post_ffn", "w_ff_gate", "w_ff_up", "w_ff_down", "g_pre_ple", "g_post_ple", "w_ple_gate", "w_ple_proj"]


def _in_t_from_blocks(z):
    w = z.reshape(-1, z.shape[-1])
    return jnp.concatenate([w[R_GA:R_END], w[:R_ALR], w[R_ALR:R_GA], jnp.zeros((128 - GATE_RANK, w.shape[1]), w.dtype)], axis=0)


def _blocks_from_in_t(g):
    per = R_END // N_DEV

    def rows(lo, hi):
        out = []
        for n0, n1, p0 in ((0, R_ALR, C_AX), (R_ALR, R_GA, C_ALR), (R_GA, R_END, 0)):
            a, e = max(lo, n0), min(hi, n1)
            if a < e:
                out.append(g[p0 + a - n0:p0 + e - n0])
        return out

    return jnp.stack([jnp.concatenate(rows(b * per, (b + 1) * per), axis=0) for b in range(N_DEV)])


def _cols_to_full(g8):
    n, r, c = g8.shape
    return jnp.transpose(g8, (1, 0, 2)).reshape(r, n * c)


def _full_to_cols(a):
    r, c = a.shape
    return jnp.transpose(a.reshape(r, N_DEV, c // N_DEV), (1, 0, 2))


def _pack(arrs, rows):
    flat = jnp.concatenate([a.reshape(-1) for a in arrs])
    return jnp.pad(flat, (0, rows * 128 - flat.shape[0])).reshape(rows, 128)


def _unpack(packed, shapes):
    flat, out, o = packed.reshape(-1), [], 0
    for s in shapes:
        size = 1
        for d in s:
            size *= d
        out.append(flat[o:o + size].reshape(s))
        o += size
    return out


def kernel(x, p, w_in, conv_w, w_a_out, w_alpha_up, b_alpha_up, gla_head_gain, w_b_out, w_mix_out, g_pre_mix, g_post_mix, g_pre_ffn, g_post_ffn, w_ff_gate, w_ff_up, w_ff_down, g_pre_ple, g_post_ple, w_ple_gate, w_ple_proj, loss_target, m_w_in, m_conv_w, m_w_a_out, m_w_alpha_up, m_b_alpha_up, m_gla_head_gain, m_w_b_out, m_w_mix_out, m_g_pre_mix, m_g_post_mix, m_g_pre_ffn, m_g_post_ffn, m_w_ff_gate, m_w_ff_up, m_w_ff_down, m_g_pre_ple, m_g_post_ple, m_w_ple_gate, m_w_ple_proj, v_w_in, v_conv_w, v_w_a_out, v_w_alpha_up, v_b_alpha_up, v_gla_head_gain, v_w_b_out, v_w_mix_out, v_g_pre_mix, v_g_post_mix, v_g_pre_ffn, v_g_post_ffn, v_w_ff_gate, v_w_ff_up, v_w_ff_down, v_g_pre_ple, v_g_post_ple, v_w_ple_gate, v_w_ple_proj):
    args = dict(locals())
    shard = lambda n, a: jnp.transpose(a[0]) if n in TRANSPOSED else a[0]
    wts = {n: shard(n, args[n]) for n in WEIGHTS}
    mom = {n: shard(n, args["m_" + n]) for n in WEIGHTS}
    var = {n: shard(n, args["v_" + n]) for n in WEIGHTS}
    me =4 * lax.axis_index("x") + 2 * lax.axis_index("y") + lax.axis_index("c")

    groups = [["w_in", "conv_w", "w_alpha_up"], ["w_a_out", "w_b_out", "w_mix_out"], ["w_ff_gate", "w_ff_up"], ["w_ff_down"],
              ["w_ple_gate", "w_ple_proj"]]
    grad_groups = [["w_ple_proj", "w_ple_gate"], ["w_ff_down", "w_ff_gate", "w_ff_up"], ["w_mix_out", "w_a_out", "w_b_out"], ["w_in"]]
    rows_full = lambda g: g.reshape(-1, g.shape[-1])
    gathers, scatters = {}, {}

    def gather_start(gi, after):
        if gi not in gathers:
            shards = [wts[n].astype(BF16) if n in BIG else wts[n] for n in groups[gi]]
            zones = [jax.ShapeDtypeStruct((N_DEV,) + s.shape, s.dtype) for s in shards]
            plan, peers = (_gather_plan_near, 3) if gi == 0 else (_gather_plan, 4)
            gathers[gi] = (shards, _copies_start(plan, peers * len(shards), shards, zones, "gather_start_%d" % gi, after))
        return gathers[gi][1][-1]

    def gather_finish(gi, after):
        shards, state = gathers[gi]
        shards, zones = _copies_wait(_gather_plan_near if gi == 0 else _gather_plan, state, len(shards), "gather_wait_%d" % gi, after)
        if gi == 0:
            zones = _gather_relay(zones, "gather_relay_%d" % gi)
            gather_start(1, (zones[0],))
        else:
            zones = _gather_finish(zones, "gather_finish_%d" % gi)
        g8 = {n: _put_own(s, z, me, "gather_own_" + n) for n, s, z in zip(groups[gi], shards, zones)}
        if gi == 0:
            return dict(in_t=_in_t_from_blocks(g8["w_in"]),
                        conv_w=_cols_to_full(g8["conv_w"]), w_alpha_up=_cols_to_full(g8["w_alpha_up"]))
        if gi == 1:
            return dict(a_out=g8["w_a_out"], b_out=g8["w_b_out"], mix=rows_full(g8["w_mix_out"]))
        if gi == 2:
            return dict(gate_t=rows_full(g8["w_ff_gate"]), up_t=rows_full(g8["w_ff_up"]))
        if gi == 3:
            return dict(down=rows_full(g8["w_ff_down"]))
        return dict(pg=rows_full(g8["w_ple_gate"]), pp=g8["w_ple_proj"])

    def scatter_start(gi, gw):
        if gi == 3:
            full = dict(w_in=_blocks_from_in_t(gw["in_t"]))
        elif gi == 1:
            full = dict(w_ff_down=gw["down"], w_ff_gate=gw["gate_t"], w_ff_up=gw["up_t"])
        elif gi == 2:
            full = dict(w_mix_out=gw["mix"], w_a_out=gw["a_out"], w_b_out=gw["b_out"])
        else:
            full = dict(w_ple_proj=gw["pp"], w_ple_gate=gw["pg"])
        parts = [full[n] if full[n].ndim == 3 else full[n].reshape(N_DEV, -1, full[n].shape[-1]) for n in grad_groups[gi]]
        if gi == 3:
            from_sibling = _sibling_exchange(parts, "scatter_sibling_%d" % gi)
            parts = [_pair_add(a, b, "scatter_add_%d_%s" % (gi, n)) for n, a, b in zip(grad_groups[gi], parts, from_sibling)]
            scatters[gi] = _copies_start(_chip_plan, 3 * len(parts), parts, parts, "scatter_start_%d" % gi)
        else:
            scatters[gi] = _copies_start(_scatter_plan, (N_DEV - 1) * len(parts), parts, parts, "scatter_start_%d" % gi)
        return scatters[gi][-1]

    small = {n: wts[n].reshape(1, -1) for n in SMALL[2:]}

    loss_rows, grad_x, gs = _local_step(x[0], p[0, 0], loss_target[0], gather_start, gather_finish, scatter_start, small)
    gs["loss"] = jnp.sum(loss_rows).reshape(1, 1)

    small_shapes = [gs[n].shape for n in SMALL]
    gs_packed = _pack([gs[n] for n in SMALL + ["loss"]], 192)
    small_state = _copies_start(_everyone_plan, N_DEV - 1, [gs_packed], [jax.ShapeDtypeStruct((N_DEV,) + gs_packed.shape, F32)],
                                "small_start", (grad_x,))

    res, done = {}, (small_state[-1],)
    for gi, names in enumerate(grad_groups):
        plan, slot = (_chip_plan, me // 2) if gi == 3 else (_scatter_plan, me)
        mine, got = _copies_wait(plan, scatters[gi], len(names), "scatter_wait_%d" % gi, done)
        for n, g, own in zip(names, got, mine):
            res[n] = _adamw_parts(wts[n], g, own, slot, mom[n], var[n], "adamw_" + n)
        done = tuple(res[n][1] for n in names)

    (gs_own,), (gs_got,) = _copies_wait(_everyone_plan, small_state, 1, "small_wait", done)
    gsum = dict(zip(SMALL + ["loss"], _unpack(_sum_parts(gs_got, gs_own, me, "small_sum"), small_shapes + [(1, 1)])))
    loss = gsum["loss"].reshape(())
    gsum["conv_w"] = lax.dynamic_index_in_dim(gsum["conv_w"].reshape(3, N_DEV, -1), me, axis=1, keepdims=False)
    gsum["w_alpha_up"] = lax.dynamic_index_in_dim(gsum["w_alpha_up"].reshape(GATE_RANK, N_DEV, -1), me, axis=1, keepdims=False)

    shard_shapes = [wts[n].shape for n in SMALL]
    packed = [_pack([d[n] for n in SMALL], 120) for d in (wts, gsum, mom, var)]
    outs = [_unpack(o, shard_shapes) for o in _adamw(*packed, "adamw_small")]
    for i, n in enumerate(SMALL):
        res[n] = [o[i] for o in outs]

    back = lambda n, a: (jnp.transpose(a) if n in TRANSPOSED else a)[None]
    return (loss, grad_x[None], *[back(n, res[n][i]) for i in range(4) for n in WEIGHTS])
```

```python
import functools

import jax
import jax.numpy as jnp
from jax import lax
from jax.experimental import pallas as pl
from jax.experimental.pallas import tpu as pltpu
from jax.experimental.pallas import tpu_sc as plsc

F32, BF16 = jnp.float32, jnp.bfloat16
EPS = 1e-6
CHUNK = 64
STEP_CHUNKS = 2
STEP_ROWS = STEP_CHUNKS * CHUNK
HEADS, DK, DV = 4, 128, 256
GATE_RANK = 16
TAU = 16.0
LR, B1, B2, ADAM_EPS, WD, STEP = 0.001, 0.9, 0.999, 1e-08, 0.01, 10
N_DEV = 8
MESH = pl.DeviceIdType.MESH
VMEM_LIMIT = 56 * 1024 * 1024
ANY = pl.BlockSpec(memory_space=pl.ANY)

C_GA, C_GB, C_AX, C_AB, C_AC, C_Q, C_K, C_V, C_OG, C_ALR = 0, 2048, 4096, 5120, 6144, 7168, 7680, 8192, 9216, 10240
IN_PAD = 10368
R_AX, R_AB, R_AC, R_Q, R_K, R_V, R_OG, R_ALR, R_GA, R_GB, R_END = 0, 1024, 2048, 3072, 3584, 4096, 5120, 6144, 6160, 8208, 10256


def _params(sem):
    return pltpu.CompilerParams(dimension_semantics=sem, vmem_limit_bytes=VMEM_LIMIT)


def _pick(n, cands):
    for c in cands:
        if n % c == 0:
            return c
    return n


def _tiles(r, c):
    for tr in (128, 64):
        if r % tr == 0:
            return r // tr, (tr, c), lambda i: (i, 0)
    tc = _pick(c, (256, 128))
    return c // tc, (r, tc), lambda i: (0, i)


def _mm(a, b, mode, out_dtype, name, after=(), add=None, b3=False, out3=False, tm=None, tk=None):
    bshape = (b.shape[1], N_DEV * b.shape[2]) if b3 else b.shape
    if mode == "nn":
        (m, k), (k2, n) = a.shape, bshape
    elif mode == "nt":
        (m, k), (n, k2) = a.shape, bshape
    else:
        (k, m), (k2, n) = a.shape, bshape
    assert k == k2 and a.dtype == BF16 and b.dtype == BF16, (name, a.shape, b.shape, a.dtype, b.dtype)
    tm = tm if tm and m % tm == 0 else _pick(m, (2048, 1024, 512, 256))
    tn = _pick(n, (1152, 1024, 1408, 512, 256))
    tk = tk if tk and k % tk == 0 else _pick(k, (2048, 1408, 1152, 1024, 512, 256))
    if out3 or (b3 and mode == "nn"):
        tn = n // N_DEV
    if b3 and mode == "nt":
        tk = k // N_DEV
    nk = k // tk
    dims = {"nn": (((1,), (0,)), ((), ())), "nt": (((1,), (1,)), ((), ())), "tn": (((0,), (0,)), ((), ()))}[mode]
    n_extra = len(after) + (add is not None)

    def body(a_ref, b_ref, *rest):
        o_ref = rest[n_extra]
        prod = lax.dot_general(a_ref[...], b_ref[...], dims, preferred_element_type=F32)
        if nk == 1:
            o_ref[...] = (prod if add is None else prod + rest[0][...]).astype(o_ref.dtype)
            return
        acc_ref = rest[n_extra + 1]
        kk = pl.program_id(2)

        @pl.when(kk == 0)
        def _():
            acc_ref[...] = prod if add is None else prod + rest[0][...]

        @pl.when((kk > 0) & (kk < nk - 1))
        def _():
            acc_ref[...] += prod

        @pl.when(kk == nk - 1)
        def _():
            o_ref[...] = (acc_ref[...] + prod).astype(o_ref.dtype)

    a_spec = pl.BlockSpec((tk, tm), lambda i, j, kk: (kk, i)) if mode == "tn" else pl.BlockSpec((tm, tk), lambda i, j, kk: (i, kk))
    if b3:
        b_spec = (pl.BlockSpec((None, tn, tk), lambda i, j, kk: (kk, j, 0)) if mode == "nt"
                  else pl.BlockSpec((None, tk, tn), lambda i, j, kk: (j, kk, 0)))
    else:
        b_spec = pl.BlockSpec((tn, tk), lambda i, j, kk: (j, kk)) if mode == "nt" else pl.BlockSpec((tk, tn), lambda i, j, kk: (kk, j))
    tile = pl.BlockSpec((tm, tn), lambda i, j, kk: (i, j))
    out_spec = pl.BlockSpec((None, tm, tn), lambda i, j, kk: (j, i, 0)) if out3 else tile
    return pl.pallas_call(
        body, name=name, grid=(m // tm, n // tn, nk),
        in_specs=[a_spec, b_spec] + ([tile] if add is not None else []) + [ANY] * len(after), out_specs=out_spec,
        out_shape=jax.ShapeDtypeStruct((N_DEV, m, tn) if out3 else (m, n), out_dtype),
        scratch_shapes=[pltpu.VMEM((tm, tn), F32)] if nk > 1 else [],
        compiler_params=_params(("parallel", "parallel", "arbitrary")),
    )(a, b, *([add] if add is not None else []), *after)


def _rows(body, t, tr, ins, outs, name):
    in_specs = []
    for arr, sp in ins:
        if sp[0] == "t":
            in_specs.append(pl.BlockSpec((tr, sp[1]), lambda i, cb=sp[2]: (i, cb)))
        else:
            in_specs.append(pl.BlockSpec(arr.shape, lambda i, nd=arr.ndim: (0,) * nd))
    out_specs, out_shape = [], []
    for shape, dt, kind in outs:
        out_specs.append(pl.BlockSpec((tr, shape[1]), lambda i: (i, 0)) if kind == "t" else pl.BlockSpec(shape, lambda i: (0, 0)))
        out_shape.append(jax.ShapeDtypeStruct(shape, dt))
    return pl.pallas_call(
        body, name=name, grid=(t // tr,), in_specs=in_specs, out_specs=out_specs, out_shape=out_shape,
        compiler_params=_params(("arbitrary",)),
    )(*[arr for arr, _ in ins])


def _rinv(v):
    return lax.rsqrt(jnp.mean(v * v, axis=-1, keepdims=True) + EPS)


def _sig(v):
    return 1.0 / (1.0 + jnp.exp(-v))


def _acc(ref, val):
    @pl.when(pl.program_id(0) == 0)
    def _():
        ref[...] = jnp.zeros_like(ref)

    ref[...] += jnp.sum(val, axis=0, keepdims=True)


def _rms_fwd(x, g, name):
    t, d = x.shape

    def body(x_ref, g_ref, h_ref):
        xv = x_ref[...]
        h_ref[...] = (xv * _rinv(xv) * g_ref[...]).astype(BF16)

    return _rows(body, t, 256, [(x, ("t", d, 0)), (g, ("b",))], [((t, d), BF16, "t")], name)[0]


def _post_pre(x, m, g_post, g_pre, name):
    t, d = x.shape

    def body(x_ref, m_ref, gp_ref, gn_ref, xo_ref, h_ref):
        mv = m_ref[...]
        xn = x_ref[...] + mv * _rinv(mv) * gp_ref[...]
        xo_ref[...] = xn
        h_ref[...] = (xn * _rinv(xn) * gn_ref[...]).astype(BF16)

    return _rows(body, t, 128, [(x, ("t", d, 0)), (m, ("t", d, 0)), (g_post, ("b",)), (g_pre, ("b",))],
                 [((t, d), F32, "t"), ((t, d), BF16, "t")], name)


def _mix_fwd(proj, ya, yb, name):
    t, d = ya.shape

    def body(ga_ref, gb_ref, ya_ref, yb_ref, o_ref):
        o_ref[...] = (_sig(ga_ref[...].astype(F32)) * ya_ref[...].astype(F32)
                      + _sig(gb_ref[...].astype(F32)) * yb_ref[...].astype(F32)).astype(BF16)

    return _rows(body, t, 256, [(proj, ("t", d, C_GA // d)), (proj, ("t", d, C_GB // d)), (ya, ("t", d, 0)), (yb, ("t", d, 0))],
                 [((t, d), BF16, "t")], name)[0]


def _mix_bwd(dmix, proj, ya, yb, name):
    t, d = ya.shape

    def body(dm_ref, ga_ref, gb_ref, ya_ref, yb_ref, dg_ref, dya_ref, dyb_ref):
        dm = dm_ref[...]
        sa, sb = _sig(ga_ref[...].astype(F32)), _sig(gb_ref[...].astype(F32))
        dg_ref[:, :d] = (dm * ya_ref[...].astype(F32) * sa * (1.0 - sa)).astype(BF16)
        dg_ref[:, d:] = (dm * yb_ref[...].astype(F32) * sb * (1.0 - sb)).astype(BF16)
        dya_ref[...] = (dm * sa).astype(BF16)
        dyb_ref[...] = (dm * sb).astype(BF16)

    return _rows(body, t, 128,
                 [(dmix, ("t", d, 0)), (proj, ("t", d, C_GA // d)), (proj, ("t", d, C_GB // d)), (ya, ("t", d, 0)), (yb, ("t", d, 0))],
                 [((t, 2 * d), BF16, "t"), ((t, d), BF16, "t"), ((t, d), BF16, "t")], name)


def _swiglu_call(body, ins, n_out, name):
    t, f = ins[0].shape
    tc = _pick(f, (1408, 512))
    tile = pl.BlockSpec((512, tc), lambda i, j: (i, j))
    return pl.pallas_call(
        body, name=name, grid=(t // 512, f // tc), in_specs=[tile] * len(ins), out_specs=[tile] * n_out,
        out_shape=[jax.ShapeDtypeStruct((t, f), BF16)] * n_out, compiler_params=_params(("parallel", "parallel")),
    )(*ins)


def _swiglu_fwd(fg, fu, name):
    def body(g_ref, u_ref, s_ref):
        gv = g_ref[...].astype(F32)
        s_ref[...] = (gv * _sig(gv) * u_ref[...].astype(F32)).astype(BF16)

    return _swiglu_call(body, [fg, fu], 1, name)[0]


def _swiglu_bwd(ds, fg, fu, name):
    def body(ds_ref, g_ref, u_ref, dg_ref, du_ref):
        dsv, gv, uv = ds_ref[...].astype(F32), g_ref[...].astype(F32), u_ref[...].astype(F32)
        sg = _sig(gv)
        dg_ref[...] = (dsv * uv * sg * (1.0 + gv * (1.0 - sg))).astype(BF16)
        du_ref[...] = (dsv * gv * sg).astype(BF16)

    return _swiglu_call(body, [ds, fg, fu], 2, name)


def _ple_final(x2, pg, pp, tgt, g_post, name):
    t, d = x2.shape

    def body(x_ref, pg_ref, pp_ref, t_ref, g_ref, loss_ref, d3_ref, dpg_ref, dpp_ref, dg_ref):
        sg, ppv, g = _sig(pg_ref[...]), pp_ref[...], g_ref[...]
        e = sg * ppv
        r = _rinv(e)
        eh = e * r
        diff = x_ref[...] + eh * g - t_ref[...]
        loss_ref[...] = 0.5 * jnp.mean(diff * diff, axis=-1, keepdims=True)
        d3 = diff * (1.0 / d)
        d3_ref[...] = d3
        gd = d3 * g
        de = r * (gd - eh * jnp.mean(gd * eh, axis=-1, keepdims=True))
        dpg_ref[...] = (de * ppv * sg * (1.0 - sg)).astype(BF16)
        dpp_ref[...] = (de * sg).astype(BF16)
        _acc(dg_ref, d3 * eh)

    return _rows(body, t, 128, [(x2, ("t", d, 0)), (pg, ("t", d, 0)), (pp, ("t", d, 0)), (tgt, ("t", d, 0)), (g_post, ("b",))],
                 [((t, 1), F32, "t"), ((t, d), F32, "t"), ((t, d), BF16, "t"), ((t, d), BF16, "t"), ((1, d), F32, "a")], name)


def _norm_bwd(dn, dh, x, g_pre, fm, g_post, name):
    t, d = x.shape
    two = fm is not None

    def body(*refs):
        if two:
            dn_ref, dh_ref, x_ref, gp_ref, f_ref, gq_ref, dx_ref, df_ref, dgp_ref, dgq_ref = refs
        else:
            dn_ref, dh_ref, x_ref, gp_ref, dx_ref, dgp_ref = refs
        xv, dhv = x_ref[...], dh_ref[...]
        r = _rinv(xv)
        xh = xv * r
        gd = dhv * gp_ref[...]
        dx = dn_ref[...] + r * (gd - xh * jnp.mean(gd * xh, axis=-1, keepdims=True))
        dx_ref[...] = dx
        _acc(dgp_ref, dhv * xh)
        if two:
            fv = f_ref[...]
            rf = _rinv(fv)
            fh = fv * rf
            gd2 = dx * gq_ref[...]
            df_ref[...] = (rf * (gd2 - fh * jnp.mean(gd2 * fh, axis=-1, keepdims=True))).astype(BF16)
            _acc(dgq_ref, dx * fh)

    ins = [(dn, ("t", d, 0)), (dh, ("t", d, 0)), (x, ("t", d, 0)), (g_pre, ("b",))]
    outs = [((t, d), F32, "t")]
    if two:
        ins += [(fm, ("t", d, 0)), (g_post, ("b",))]
        outs += [((t, d), BF16, "t"), ((1, d), F32, "a"), ((1, d), F32, "a")]
    else:
        outs += [((1, d), F32, "a")]
    return _rows(body, t, 128, ins, outs, name)


CONV_TC = 256


def _shift_down(v, s):
    rows = lax.broadcasted_iota(jnp.int32, v.shape, 0)
    return jnp.where(rows >= s, pltpu.roll(v, s, 0), 0.0)


def _shift_up(v, s):
    n = v.shape[0]
    rows = lax.broadcasted_iota(jnp.int32, v.shape, 0)
    return jnp.where(rows < n - s, pltpu.roll(v, n - s, 0), 0.0)


def _conv_specs(t):
    nb = 1024 // CONV_TC
    seg = lambda c0: pl.BlockSpec((t, CONV_TC), lambda j, cb=c0 // CONV_TC: (0, cb + j))
    own = pl.BlockSpec((t, CONV_TC), lambda j: (0, j))
    wspec = pl.BlockSpec((3, CONV_TC), lambda j: (0, j))
    return nb, seg, own, wspec


def _conv_fwd(proj, conv_w, name, after=()):
    t = proj.shape[0]
    nb, seg, own, wspec = _conv_specs(t)

    def body(ax_ref, ab_ref, ac_ref, w_ref, *rest):
        za_ref = rest[len(after)]
        u = ac_ref[...].astype(F32) * ax_ref[...].astype(F32)
        w = w_ref[...]
        yc = w[0:1] * _shift_down(u, 2) + w[1:2] * _shift_down(u, 1) + w[2:3] * u
        za_ref[...] = (ab_ref[...].astype(F32) * yc).astype(BF16)

    return pl.pallas_call(
        body, name=name, grid=(nb,), in_specs=[seg(C_AX), seg(C_AB), seg(C_AC), wspec] + [ANY] * len(after), out_specs=own,
        out_shape=jax.ShapeDtypeStruct((t, 1024), BF16), compiler_params=_params(("parallel",)),
    )(proj, proj, proj, conv_w, *after)


def _conv_bwd(dza, proj, conv_w, name):
    t = proj.shape[0]
    nb, seg, own, wspec = _conv_specs(t)

    def body(dz_ref, ax_ref, ab_ref, ac_ref, w_ref, dax_ref, dab_ref, dac_ref, dw_ref):
        ax, ab, ac, dz = ax_ref[...].astype(F32), ab_ref[...].astype(F32), ac_ref[...].astype(F32), dz_ref[...].astype(F32)
        w = w_ref[...]
        u = ac * ax
        u1, u2 = _shift_down(u, 1), _shift_down(u, 2)
        yc = w[0:1] * u2 + w[1:2] * u1 + w[2:3] * u
        dab_ref[...] = (dz * yc).astype(BF16)
        dyc = dz * ab
        du = w[2:3] * dyc + w[1:2] * _shift_up(dyc, 1) + w[0:1] * _shift_up(dyc, 2)
        dax_ref[...] = (du * ac).astype(BF16)
        dac_ref[...] = (du * ax).astype(BF16)
        dw_ref[0:1, :] = jnp.sum(dyc * u2, axis=0, keepdims=True)
        dw_ref[1:2, :] = jnp.sum(dyc * u1, axis=0, keepdims=True)
        dw_ref[2:3, :] = jnp.sum(dyc * u, axis=0, keepdims=True)

    act = jax.ShapeDtypeStruct((t, 1024), BF16)
    return pl.pallas_call(
        body, name=name, grid=(nb,), in_specs=[own, seg(C_AX), seg(C_AB), seg(C_AC), wspec], out_specs=[own, own, own, wspec],
        out_shape=[act, act, act, jax.ShapeDtypeStruct((3, 1024), F32)], compiler_params=_params(("parallel",)),
    )(dza, proj, proj, proj, conv_w)


def _dot(a, b, dims, precision=None):
    return lax.dot_general(a, b, (dims, ((), ())), precision=precision, preferred_element_type=F32)


_CONTRACT = {"nn": ((1,), (0,)), "nt": ((1,), (1,)), "tn": ((0,), (0,))}


def _bdot_raw(a, b, mode):
    return _dot(a.astype(BF16), b.astype(BF16), _CONTRACT[mode])


@functools.partial(jax.custom_vjp, nondiff_argnums=(2,))
def _bdot(a, b, mode):
    return _bdot_raw(a, b, mode)


def _bdot_fwd(a, b, mode):
    return _bdot_raw(a, b, mode), (a, b)


def _bdot_bwd(mode, res, ct):
    a, b = res
    if mode == "nn":
        return _bdot_raw(ct, b, "nt"), _bdot_raw(a, ct, "tn")
    if mode == "nt":
        return _bdot_raw(ct, b, "nn"), _bdot_raw(ct, a, "tn")
    return _bdot_raw(b, ct, "nt"), _bdot_raw(a, ct, "nn")


_bdot.defvjp(_bdot_fwd, _bdot_bwd)


@functools.partial(jax.custom_vjp, nondiff_argnums=(2,))
def _sum_dot(ones, x, mode):
    head = x.astype(BF16)
    tail = x - head.astype(F32)
    if mode == "nn":
        return _bdot_raw(ones, head, "nn") + _bdot_raw(ones, tail, "nn")
    return _bdot_raw(head, ones, "tn") + _bdot_raw(tail, ones, "tn")


def _sum_dot_fwd(ones, x, mode):
    return _sum_dot(ones, x, mode), ones


def _sum_dot_bwd(mode, ones, ct):
    return jnp.zeros_like(ones), (_bdot_raw(ones, ct, "tn") if mode == "nn" else _bdot_raw(ones, ct, "nt"))


_sum_dot.defvjp(_sum_dot_fwd, _sum_dot_bwd)


def _gla_chunk(q, k, v, og, alr, s_in, wa, ba, gain):
    c = q.shape[0]
    z =_bdot(alr, wa, "nn") + ba
    la = (jnp.minimum(z, 0.0) - jnp.log(1.0 + jnp.exp(-jnp.abs(z)))) * (1.0 / TAU)
    row = lax.broadcasted_iota(jnp.int32, (c, c), 0)
    col = lax.broadcasted_iota(jnp.int32, (c, c), 1)
    lower = row >= col
    b = _sum_dot(lower.astype(F32), la, "nn")
    trow = lax.broadcasted_iota(jnp.int32, la.shape, 0)
    mid = jnp.sum(jnp.where(trow <= c // 2, la, 0.0), axis=0, keepdims=True)
    blast = jnp.sum(la, axis=0, keepdims=True)
    qs = q * (DK ** -0.5)
    e_up, e_dn = jnp.exp(b - mid), jnp.exp(mid - b)
    a_fwd = _bdot(qs * e_up, k * e_dn, "nt")
    a_rev = _bdot(qs * e_dn, k * e_up, "nt")
    att = jnp.where(lower, a_fwd, a_rev)
    o = _bdot(att, v, "nn") + _bdot(qs * jnp.exp(b), s_in, "nn")
    upd = _bdot(k * jnp.exp(blast - b), v, "tn")
    blast_col = _sum_dot(jnp.ones((c, DV), F32), la, "tn")
    s_out = jnp.exp(blast_col) * s_in + upd
    on = o * _rinv(o) * gain
    return on * og * _sig(og), s_out


def _gla_specs(t, rev):
    n = t // STEP_ROWS
    ch = (lambda i: n - 1 - i) if rev else (lambda i: i)
    col = lambda w, c0: pl.BlockSpec((STEP_ROWS, HEADS * w), lambda i, cb=c0 // (HEADS * w): (ch(i), cb))
    whole = lambda shape: pl.BlockSpec(shape, lambda i, nd=len(shape): (0,) * nd)
    specs = dict(
        q=col(DK, C_Q), k=col(DK, C_K), v=col(DV, C_V), og=col(DV, C_OG),
        alr=pl.BlockSpec((STEP_ROWS, 128), lambda i: (ch(i), C_ALR // 128)),
        wa=whole((128, HEADS * DK)), ba=whole((1, HEADS * DK)), gain=whole((1, DV)),
        state=pl.BlockSpec((STEP_CHUNKS, HEADS, DK, DV), lambda i: (ch(i), 0, 0, 0)),
        odk=pl.BlockSpec((STEP_ROWS, HEADS * DK), lambda i: (ch(i), 0)), odv=pl.BlockSpec((STEP_ROWS, HEADS * DV), lambda i: (ch(i), 0)),
        oalr=pl.BlockSpec((STEP_ROWS, 128), lambda i: (ch(i), 0)), whole=whole,
    )
    return n, specs


def _head_cols(h):
    return slice(h * DK, (h + 1) * DK), slice(h * DV, (h + 1) * DV)


def _gla_fwd(proj, wa, ba, gain, name):
    t = proj.shape[0]
    n, sp = _gla_specs(t, False)

    def body(q_ref, k_ref, v_ref, og_ref, alr_ref, wa_ref, ba_ref, g_ref, zb_ref, st_ref, s_scr):
        @pl.when(pl.program_id(0) == 0)
        def _():
            s_scr[...] = jnp.zeros_like(s_scr)

        state = [s_scr[h] for h in range(HEADS)]
        for c in range(STEP_CHUNKS):
            rows = slice(c * CHUNK, (c + 1) * CHUNK)
            alr = alr_ref[rows, :].astype(F32)
            for h in range(HEADS):
                kc, vc = _head_cols(h)
                st_ref[c, h] = state[h]
                zb, state[h] = _gla_chunk(q_ref[rows, kc].astype(F32), k_ref[rows, kc].astype(F32), v_ref[rows, vc].astype(F32),
                                          og_ref[rows, vc].astype(F32), alr, state[h], wa_ref[:, kc].astype(F32), ba_ref[:, kc], g_ref[...])
                zb_ref[rows, vc] = zb.astype(BF16)
        for h in range(HEADS):
            s_scr[h] = state[h]

    return pl.pallas_call(
        body, name=name, grid=(n,),
        in_specs=[sp["q"], sp["k"], sp["v"], sp["og"], sp["alr"], sp["wa"], sp["ba"], sp["gain"]],
        out_specs=[sp["odv"], sp["state"]],
        out_shape=[jax.ShapeDtypeStruct((t, HEADS * DV), BF16), jax.ShapeDtypeStruct((t // CHUNK, HEADS, DK, DV), F32)],
        scratch_shapes=[pltpu.VMEM((HEADS, DK, DV), F32)],
        compiler_params=_params(("arbitrary",)),
    )(proj, proj, proj, proj, proj, wa, ba, gain)


def _gla_bwd(dzb, proj, states, wa, ba, gain, name):
    t = proj.shape[0]
    n, sp = _gla_specs(t, True)

    def body(dz_ref, q_ref, k_ref, v_ref, og_ref, alr_ref, st_ref, wa_ref, ba_ref, g_ref,
             dq_ref, dk_ref, dv_ref, dog_ref, dalr_ref, dwa_ref, dba_ref, dg_ref, ds_scr):
        @pl.when(pl.program_id(0) == 0)
        def _():
            ds_scr[...] = jnp.zeros_like(ds_scr)
            dwa_ref[...] = jnp.zeros_like(dwa_ref)
            dba_ref[...] = jnp.zeros_like(dba_ref)
            dg_ref[...] = jnp.zeros_like(dg_ref)

        dstate = [ds_scr[h] for h in range(HEADS)]
        dwa_sum, dba_sum, dgain_sum = [None] * HEADS, [None] * HEADS, None
        for c in reversed(range(STEP_CHUNKS)):
            rows = slice(c * CHUNK, (c + 1) * CHUNK)
            alr = alr_ref[rows, :].astype(F32)
            dalr_sum = None
            for h in range(HEADS):
                kc, vc = _head_cols(h)
                args = (q_ref[rows, kc].astype(F32), k_ref[rows, kc].astype(F32), v_ref[rows, vc].astype(F32), og_ref[rows, vc].astype(F32),
                        alr, st_ref[c, h], wa_ref[:, kc].astype(F32), ba_ref[:, kc], g_ref[...])
                _, vjp = jax.vjp(_gla_chunk, *args)
                dq, dk, dv, dog, dalr, dstate[h], dwa, dba, dgain = vjp((dz_ref[rows, vc].astype(F32), dstate[h]))
                dq_ref[rows, kc] = dq.astype(BF16)
                dk_ref[rows, kc] = dk.astype(BF16)
                dv_ref[rows, vc] = dv.astype(BF16)
                dog_ref[rows, vc] = dog.astype(BF16)
                dwa_sum[h] = dwa if dwa_sum[h] is None else dwa_sum[h] + dwa
                dba_sum[h] = dba if dba_sum[h] is None else dba_sum[h] + dba
                dalr_sum = dalr if dalr_sum is None else dalr_sum + dalr
                dgain_sum = dgain if dgain_sum is None else dgain_sum + dgain
            dalr_ref[rows, :] = dalr_sum
        for h in range(HEADS):
            ds_scr[h] = dstate[h]
            dwa_ref[h] += dwa_sum[h]
            dba_ref[h] += dba_sum[h]
        dg_ref[...] += dgain_sum

    whole = sp["whole"]
    return pl.pallas_call(
        body, name=name, grid=(n,),
        in_specs=[sp["odv"], sp["q"], sp["k"], sp["v"], sp["og"], sp["alr"], sp["state"], sp["wa"], sp["ba"], sp["gain"]],
        out_specs=[sp["odk"], sp["odk"], sp["odv"], sp["odv"], sp["oalr"], whole((HEADS, 128, DK)), whole((HEADS, 1, DK)), whole((1, DV))],
        out_shape=[jax.ShapeDtypeStruct((t, HEADS * DK), BF16), jax.ShapeDtypeStruct((t, HEADS * DK), BF16),
                   jax.ShapeDtypeStruct((t, HEADS * DV), BF16), jax.ShapeDtypeStruct((t, HEADS * DV), BF16),
                   jax.ShapeDtypeStruct((t, 128), F32), jax.ShapeDtypeStruct((HEADS, 128, DK), F32),
                   jax.ShapeDtypeStruct((HEADS, 1, DK), F32), jax.ShapeDtypeStruct((1, DV), F32)],
        scratch_shapes=[pltpu.VMEM((HEADS, DK, DV), F32)],
        compiler_params=_params(("arbitrary",)),
    )(dzb, proj, proj, proj, proj, proj, states, wa, ba, gain)


def _local_step(x, p, tgt, gather_start, gather_finish, scatter_start, small):
    b_alpha, gain = small["b_alpha_up"], small["gla_head_gain"]
    gather_start(0, ())
    w = dict(gather_finish(0, ()))
    conv_w, w_alpha = w["conv_w"], w["w_alpha_up"]
    wa_p = jnp.zeros((128, HEADS * DK), BF16).at[:GATE_RANK].set(w_alpha.astype(BF16))

    t2 = gather_start(2, (w["in_t"], gather_start(1, ())))
    h1 = _rms_fwd(x, small["g_pre_mix"], "rms_pre_mix")
    proj = _mm(h1, w["in_t"], "nt", BF16, "mm_proj", after=(t2,))
    za = _conv_fwd(proj, conv_w, "conv_fwd")
    zb, states = _gla_fwd(proj, wa_p, b_alpha, gain, "gla_fwd")
    t3 = gather_start(3, (zb, za))
    w.update(gather_finish(1, (t3,)))
    ya = _mm(za, w["a_out"], "nn", BF16, "mm_ya", b3=True, tm=2048, tk=1024)
    yb = _mm(zb, w["b_out"], "nn", BF16, "mm_yb", b3=True, tm=2048, tk=1024)
    mix = _mix_fwd(proj, ya, yb, "mix_fwd")
    m2 = _mm(mix, w["mix"], "nn", F32, "mm_mix")
    t4 = gather_start(4, (m2,))
    x1, h2 = _post_pre(x, m2, small["g_post_mix"], small["g_pre_ffn"], "norm_mix_ffn")
    w.update(gather_finish(2, (h2, t4)))
    fg = _mm(h2, w["gate_t"], "nt", BF16, "mm_gate")
    fu = _mm(h2, w["up_t"], "nt", BF16, "mm_up")
    s = _swiglu_fwd(fg, fu, "swiglu_fwd")
    w.update(gather_finish(3, (s,)))
    f = _mm(s, w["down"], "nn", F32, "mm_down")
    x2, h3 = _post_pre(x1, f, small["g_post_ffn"], small["g_pre_ple"], "norm_ffn_ple")
    w.update(gather_finish(4, (h3,)))
    pg = _mm(h3, w["pg"], "nn", F32, "mm_pg")
    p_bf = p.astype(BF16)
    pp = _mm(p_bf, w["pp"], "nn", F32, "mm_pp", b3=True, tm=2048)
    loss_rows, d3, dpg, dpp, dg_post_ple = _ple_final(x2, pg, pp, tgt, small["g_post_ple"], "ple_final")

    gw = {}
    gw["pp"] = _mm(p_bf, dpp, "tn", BF16, "mm_dw_pp", out3=True)
    gw["pg"] = _mm(h3, dpg, "tn", BF16, "mm_dw_pg")
    dh3 = _mm(dpg, w["pg"], "nt", F32, "mm_dh3", after=(scatter_start(0, gw),))
    d2, df, dg_pre_ple, dg_post_ffn = _norm_bwd(d3, dh3, x2, small["g_pre_ple"], f, small["g_post_ffn"], "norm_bwd_ple_ffn")
    ds = _mm(df, w["down"], "nt", BF16, "mm_ds")
    gw["down"] = _mm(s, df, "tn", BF16, "mm_dw_down", tm=1408)
    dfg, dfu = _swiglu_bwd(ds, fg, fu, "swiglu_bwd")
    gw["gate_t"] = _mm(dfg, h2, "tn", BF16, "mm_dw_gate", after=(gw["down"],), tm=1408)
    gw["up_t"] = _mm(dfu, h2, "tn", BF16, "mm_dw_up", after=(gw["gate_t"],), tm=1408)
    dh2 = _mm(dfg, w["gate_t"], "nn", F32, "mm_dh2_gate", after=(scatter_start(1, gw),))
    dh2 = _mm(dfu, w["up_t"], "nn", F32, "mm_dh2_up", add=dh2, tm=1024)
    d1, dm2, dg_pre_ffn, dg_post_mix = _norm_bwd(d2, dh2, x1, small["g_pre_ffn"], m2, small["g_post_mix"], "norm_bwd_ffn_mix")
    dmix = _mm(dm2, w["mix"], "nt", F32, "mm_dmix")
    gw["mix"] = _mm(mix, dm2, "tn", BF16, "mm_dw_mix")
    dgab, dya, dyb = _mix_bwd(dmix, proj, ya, yb, "mix_bwd")
    dza = _mm(dya, w["a_out"], "nt", BF16, "mm_dza", after=(gw["mix"],), b3=True, tm=2048)
    gw["a_out"] = _mm(za, dya, "tn", BF16, "mm_dw_a_out", out3=True, tk=1024)
    gw["b_out"] = _mm(zb, dyb, "tn", BF16, "mm_dw_b_out", after=(gw["a_out"],), out3=True, tk=1024)
    dzb = _mm(dyb, w["b_out"], "nt", BF16, "mm_dzb", after=(scatter_start(2, gw),), b3=True, tm=2048)
    dax, dab, dac, dconv = _conv_bwd(dza, proj, conv_w, "conv_bwd")
    dq, dk, dv, dog, dalr, dwa, dba, dgain = _gla_bwd(dzb, proj, states, wa_p, b_alpha, gain, "gla_bwd")
    dproj = jnp.concatenate([dgab, dax, dab, dac, dq, dk, dv, dog, dalr.astype(BF16)], axis=1)
    gw["in_t"] = _mm(dproj, h1, "tn", BF16, "mm_dw_in", tm=1152)
    dh1 = _mm(dproj, w["in_t"], "nn", F32, "mm_dh1", after=(scatter_start(3, gw),))
    grad_x, dg_pre_mix = _norm_bwd(d1, dh1, x, small["g_pre_mix"], None, None, "norm_bwd_mix")

    gs = dict(
        conv_w=dconv,
        w_alpha_up=jnp.transpose(dwa[:, :GATE_RANK, :], (1, 0, 2)).reshape(GATE_RANK, HEADS * DK),
        b_alpha_up=dba.reshape(1, HEADS * DK), gla_head_gain=dgain,
        g_pre_mix=dg_pre_mix, g_post_mix=dg_post_mix, g_pre_ffn=dg_pre_ffn, g_post_ffn=dg_post_ffn,
        g_pre_ple=dg_pre_ple, g_post_ple=dg_post_ple,
    )
    return loss_rows, grad_x, gs


def _place():
    x, y, c = lax.axis_index("x"), lax.axis_index("y"), lax.axis_index("c")
    return x, y, c, [(1 - x, y), (x, 1 - y), (1 - x, 1 - y)]


def _all_gather(shards, name, cid=None):
    n = len(shards)

    def body(*refs):
        ins, outs = refs[:n], refs[n:2 * n]
        send_sems, recv_sems, local_sems = refs[2 * n:]
        x, y, c, chips = _place()
        me, sibling = (x, y, c), (x, y, 1 - c)

        def slot(px, py, pc):
            return 4 * px + 2 * py + pc

        def copy(a, k, block, to, src=None):
            dst = outs[a].at[slot(*block)]
            return pltpu.make_async_remote_copy(src_ref=dst if src is None else src, dst_ref=dst, send_sem=send_sems.at[a, k],
                                                recv_sem=recv_sems.at[a, k], device_id=to, device_id_type=MESH)

        mine = [pltpu.make_async_copy(ins[a], outs[a].at[slot(*me)], local_sems.at[a]) for a in range(n)]
        for cp in mine:
            cp.start()
        first = []
        for j, chip in enumerate(chips):
            first += [copy(a, 1 + j, me, (*chip, c), src=ins[a]) for a in range(n)]
        first += [copy(a, 0, me, sibling, src=ins[a]) for a in range(n)]
        for cp in first:
            cp.start()
        passed = []
        for j, chip in enumerate(chips):
            for a in range(n):
                copy(a, 1 + j, (*chip, c), me).wait_recv()
                cp = copy(a, 4 + j, (*chip, c), sibling)
                cp.start()
                passed.append(cp)
        for a in range(n):
            copy(a, 0, sibling, me).wait_recv()
        for j, chip in enumerate(chips):
            for a in range(n):
                copy(a, 4 + j, (*chip, 1 - c), me).wait_recv()
        for cp in first + passed:
            cp.wait_send()
        for cp in mine:
            cp.wait()

    if cid is None:
        return pl.pallas_call(
            body, name=name, in_specs=[ANY] * n, out_specs=[ANY] * n,
            out_shape=[jax.ShapeDtypeStruct((N_DEV,) + s.shape, s.dtype) for s in shards],
            scratch_shapes=[pltpu.SemaphoreType.DMA((n, 7)), pltpu.SemaphoreType.DMA((n, 7)), pltpu.SemaphoreType.DMA((n,))],
        )(*shards)

    src = [jax.new_ref(s, memory_space=pltpu.MemorySpace.HBM) for s in shards]
    dst = [jax.empty_ref(jax.ShapeDtypeStruct((N_DEV,) + s.shape, s.dtype), memory_space=pltpu.MemorySpace.HBM) for s in shards]

    @pl.kernel(mesh=plsc.ScalarSubcoreMesh(axis_name="seq", num_cores=1), name=name,
               scratch_types=(pltpu.SemaphoreType.DMA((n, 7)), pltpu.SemaphoreType.DMA((n, 7)), pltpu.SemaphoreType.DMA((n,))),
               compiler_params=pltpu.CompilerParams(collective_id=cid))
    def launch(send_sems, recv_sems, local_sems):
        x, y, c, chips = _place()
        barrier = pltpu.get_barrier_semaphore()
        for peer in [(x, y, 1 - c)] + [(*chip, c) for chip in chips]:
            pl.semaphore_signal(barrier, inc=1, device_id=peer, device_id_type=MESH)
        pl.semaphore_wait(barrier, 4)
        body(*src, *dst, send_sems, recv_sems, local_sems)

    launch()
    return [r[...] for r in dst]


def _reduce_scatter(parts, name, cid):
    n = len(parts)
    src = [jax.new_ref(s, memory_space=pltpu.MemorySpace.HBM) for s in parts]
    dst = [jax.empty_ref(jax.ShapeDtypeStruct(s.shape, s.dtype), memory_space=pltpu.MemorySpace.HBM) for s in parts]

    @pl.kernel(mesh=plsc.ScalarSubcoreMesh(axis_name="seq", num_cores=1), name=name,
               scratch_types=(pltpu.SemaphoreType.DMA((n, N_DEV - 1)), pltpu.SemaphoreType.DMA((n, N_DEV - 1)), pltpu.SemaphoreType.DMA((n,))),
               compiler_params=pltpu.CompilerParams(collective_id=cid))
    def launch(send_sems, recv_sems, local_sems):
        x, y, c, _ = _place()
        me = 4 * x + 2 * y + c
        peers = [(1 - x if k & 4 else x, 1 - y if k & 2 else y, 1 - c if k & 1 else c) for k in range(1, N_DEV)]
        barrier = pltpu.get_barrier_semaphore()
        for peer in peers:
            pl.semaphore_signal(barrier, inc=1, device_id=peer, device_id_type=MESH)
        pl.semaphore_wait(barrier, N_DEV - 1)
        mine = [pltpu.make_async_copy(src[a].at[me], dst[a].at[me], local_sems.at[a]) for a in range(n)]
        for cp in mine:
            cp.start()
        cps = []
        for a in range(n):
            for k, (px, py, pc) in enumerate(peers):
                cps.append(pltpu.make_async_remote_copy(src_ref=src[a].at[4 * px + 2 * py + pc], dst_ref=dst[a].at[me], send_sem=send_sems.at[a, k],
                                                        recv_sem=recv_sems.at[a, k], device_id=(px, py, pc), device_id_type=MESH))
        for cp in cps:
            cp.start()
        for cp in cps:
            cp.wait_recv()
        for cp in cps:
            cp.wait_send()
        for cp in mine:
            cp.wait()

    launch()
    return [r[...] for r in dst]


def _sibling_exchange(parts, name):
    n = len(parts)
    pieces = [_row_pieces(s.shape[1]) for s in parts]

    def body(*refs):
        ins, outs = refs[:n], refs[n:2 * n]
        send_sems, recv_sems = refs[2 * n:]
        x, y, c, _ = _place()

        def copy(a, ch, q, rows):
            return pltpu.make_async_remote_copy(src_ref=ins[a].at[2 * ch + 1 - c, rows], dst_ref=outs[a].at[ch, rows], send_sem=send_sems.at[a, ch, q],
                                                recv_sem=recv_sems.at[a, ch, q], device_id=(x, y, 1 - c), device_id_type=MESH)

        cps = [copy(a, ch, q, rows) for ch in range(4) for a in range(n) for q, rows in enumerate(pieces[a])]
        for cp in cps:
            cp.start()
        for cp in cps:
            cp.wait_recv()
        for cp in cps:
            cp.wait_send()

    return pl.pallas_call(
        body, name=name, in_specs=[ANY] * n, out_specs=[ANY] * n,
        out_shape=[jax.ShapeDtypeStruct((4,) + s.shape[1:], s.dtype) for s in parts],
        scratch_shapes=[pltpu.SemaphoreType.DMA((n, 4, PIECES)), pltpu.SemaphoreType.DMA((n, 4, PIECES))],
    )(*parts)


def _chip_exchange(parts, name):
    n = len(parts)

    def body(*refs):
        ins, outs = refs[:n], refs[n:2 * n]
        send_sems, recv_sems, local_sems = refs[2 * n:]
        x, y, c, chips = _place()
        my_chip = 2 * x + y

        def copy(a, j):
            px, py = chips[j]
            return pltpu.make_async_remote_copy(src_ref=ins[a].at[2 * px + py], dst_ref=outs[a].at[my_chip], send_sem=send_sems.at[a, j],
                                                recv_sem=recv_sems.at[a, j], device_id=(px, py, c), device_id_type=MESH)

        def landing(a, j):
            px, py = chips[j]
            return pltpu.make_async_remote_copy(src_ref=ins[a].at[my_chip], dst_ref=outs[a].at[2 * px + py], send_sem=send_sems.at[a, j],
                                                recv_sem=recv_sems.at[a, j], device_id=(px, py, c), device_id_type=MESH)

        mine = [pltpu.make_async_copy(ins[a].at[my_chip], outs[a].at[my_chip], local_sems.at[a]) for a in range(n)]
        for cp in mine:
            cp.start()
        cps = [copy(a, j) for j in range(3) for a in range(n)]
        for cp in cps:
            cp.start()
        for j in range(3):
            for a in range(n):
                landing(a, j).wait_recv()
        for cp in cps:
            cp.wait_send()
        for cp in mine:
            cp.wait()

    return pl.pallas_call(
        body, name=name, in_specs=[ANY] * n, out_specs=[ANY] * n,
        out_shape=[jax.ShapeDtypeStruct(s.shape, s.dtype) for s in parts],
        scratch_shapes=[pltpu.SemaphoreType.DMA((n, 3)), pltpu.SemaphoreType.DMA((n, 3)), pltpu.SemaphoreType.DMA((n,))],
    )(*parts)


def _pair_add(mine8, got4, name):
    _, r, cols = mine8.shape
    steps, blk, at = _tiles(r, cols)
    core = lax.axis_index("c").astype(jnp.int32).reshape(1)

    def body(c_ref, a_ref, b_ref, o_ref):
        o_ref[...] = (a_ref[...].astype(F32) + b_ref[...].astype(F32)).astype(BF16)

    return pl.pallas_call(
        body, name=name,
        grid_spec=pltpu.PrefetchScalarGridSpec(
            num_scalar_prefetch=1, grid=(4, steps),
            in_specs=[pl.BlockSpec((None,) + blk, lambda ch, i, c_ref: (2 * ch + c_ref[0],) + at(i)),
                      pl.BlockSpec((None,) + blk, lambda ch, i, c_ref: (ch,) + at(i))],
            out_specs=pl.BlockSpec((None,) + blk, lambda ch, i, c_ref: (ch,) + at(i))),
        out_shape=jax.ShapeDtypeStruct((4, r, cols), BF16),
        compiler_params=_params(("parallel", "parallel")),
    )(core, mine8, got4)


HBM = pl.BlockSpec(memory_space=pltpu.HBM)
SEM = pl.BlockSpec(memory_space=pltpu.SEMAPHORE)
EFFECT = pltpu.SideEffectType.DATAFLOW_SIDE_EFFECTING


def _in_hbm(a):
    return pltpu.with_memory_space_constraint(a, pltpu.HBM)


def _remote_copies(plan, srcs, lands, send_sems, recv_sems):
    return [pltpu.make_async_remote_copy(src_ref=s, dst_ref=d, send_sem=send_sems.at[i], recv_sem=recv_sems.at[i], device_id=peer,
                                         device_id_type=MESH) for i, (s, d, peer) in enumerate(plan(srcs, lands))]


def _copies_start(plan, n_copies, srcs, land_shapes, name, after=()):
    ns, nl = len(srcs), len(land_shapes)

    def body(*refs):
        send_sems, recv_sems = refs[ns + nl + len(after):ns + nl + len(after) + 2]
        for cp in _remote_copies(plan, refs[:ns], refs[ns:ns + nl], send_sems, recv_sems):
            cp.start()
        refs[-1][...] = jnp.zeros((8, 128), F32)

    sems = pltpu.SemaphoreType.DMA((n_copies,))
    return pl.pallas_call(
        body, name=name,
        out_shape=(sems, sems, *[pltpu.HBM(s.shape, s.dtype) for s in srcs], *[pltpu.HBM(s.shape, s.dtype) for s in land_shapes],
                   jax.ShapeDtypeStruct((8, 128), F32)),
        in_specs=[HBM] * (ns + nl) + [ANY] * len(after),
        out_specs=(SEM, SEM, *[HBM] * (ns + nl), pl.BlockSpec(memory_space=pltpu.VMEM)),
        input_output_aliases={i: 2 + i for i in range(ns + nl)},
        compiler_params=pltpu.CompilerParams(has_side_effects=EFFECT),
    )(*[_in_hbm(s) for s in srcs], *[_in_hbm(lax.empty(s.shape, s.dtype)) for s in land_shapes], *after)


def _copies_wait(plan, state, ns, name, after=()):
    send_sems, recv_sems, *arrs = state[:-1]
    n = len(arrs)

    def body(*refs):
        cps = _remote_copies(plan, refs[:ns], refs[ns:n], refs[n], refs[n + 1])
        for cp in cps:
            cp.wait_send()
        for cp in cps:
            cp.wait_recv()

    out = pl.pallas_call(
        body, name=name, out_shape=tuple(pltpu.HBM(a.shape, a.dtype) for a in arrs),
        in_specs=[HBM] * n + [SEM, SEM] + [ANY] * len(after), out_specs=tuple([HBM] * n),
        input_output_aliases={i: i for i in range(n)},
        compiler_params=pltpu.CompilerParams(has_side_effects=EFFECT),
    )(*arrs, send_sems, recv_sems, *after)
    return list(out[:ns]), list(out[ns:])


def _gather_plan(srcs, lands):
    x, y, c, chips = _place()
    peers = [(x, y, c), (x, y, 1 - c)] + [(*chip, c) for chip in chips]
    return [(s, l.at[4 * x + 2 * y + c], peer) for s, l in zip(srcs, lands) for peer in peers]


def _gather_plan_near(srcs, lands):
    x, y, c, _ = _place()
    peers = [(x, y, c), (x, y, 1 - c), (1 - x, y, c), (x, 1 - y, c)]
    return [(s, l.at[4 * x + 2 * y + c], peer) for s, l in zip(srcs, lands) for peer in peers]


def _scatter_plan(srcs, lands):
    x, y, c, _ = _place()
    peers = [(1 - x if k & 4 else x, 1 - y if k & 2 else y, 1 - c if k & 1 else c) for k in range(1, N_DEV)]
    return [(s.at[4 * px + 2 * py + pc], l.at[4 * x + 2 * y + c], (px, py, pc)) for s, l in zip(srcs, lands) for px, py, pc in peers]


def _everyone_plan(srcs, lands):
    x, y, c, _ = _place()
    peers = [(1 - x if k & 4 else x, 1 - y if k & 2 else y, 1 - c if k & 1 else c) for k in range(1, N_DEV)]
    return [(s, l.at[4 * x + 2 * y + c], peer) for s, l in zip(srcs, lands) for peer in peers]


def _sum_parts(got, own, me, name):
    def body(me_ref, got_ref, own_ref, o_ref):
        acc = jnp.where(me_ref[0] == 0, own_ref[...], got_ref[0])
        for d in range(1, N_DEV):
            acc = acc + jnp.where(me_ref[0] == d, own_ref[...], got_ref[d])
        o_ref[...] = acc

    return pl.pallas_call(
        body, name=name,
        grid_spec=pltpu.PrefetchScalarGridSpec(
            num_scalar_prefetch=1, grid=(1,),
            in_specs=[pl.BlockSpec(got.shape, lambda i, me_ref: (0, 0, 0)), pl.BlockSpec(own.shape, lambda i, me_ref: (0, 0))],
            out_specs=pl.BlockSpec(own.shape, lambda i, me_ref: (0, 0))),
        out_shape=jax.ShapeDtypeStruct(own.shape, F32),
    )(me.astype(jnp.int32).reshape(1), got, own)


def _chip_plan(srcs, lands):
    x, y, c, chips = _place()
    return [(s.at[2 * px + py], l.at[2 * x + y], (px, py, c)) for s, l in zip(srcs, lands) for px, py in chips]


PIECES = 8


def _row_pieces(rows):
    for k in (PIECES, 4, 2):
        if rows % (16 * k) == 0:
            return [pl.ds(q * (rows // k), rows // k) for q in range(k)]
    return [pl.ds(0, rows)]


def _put_own(shard, zone, me, name):
    r, c = shard.shape
    tr = r if r <= 256 else _pick(r, (256, 64))

    def body(me_ref, s_ref, z_ref, o_ref):
        o_ref[...] = s_ref[...]

    return pl.pallas_call(
        body, name=name,
        grid_spec=pltpu.PrefetchScalarGridSpec(
            num_scalar_prefetch=1, grid=(r // tr,),
            in_specs=[pl.BlockSpec((tr, c), lambda i, me_ref: (i, 0)), ANY],
            out_specs=pl.BlockSpec((None, tr, c), lambda i, me_ref: (me_ref[0], i, 0))),
        out_shape=jax.ShapeDtypeStruct(zone.shape, zone.dtype), input_output_aliases={2: 0},
        compiler_params=_params(("arbitrary",)),
    )(me.astype(jnp.int32).reshape(1), shard, zone)


def _gather_finish(lands, name):
    n = len(lands)

    def body(*refs):
        zones, outs = refs[:n], refs[n:2 * n]
        send_sems, recv_sems = refs[2 * n:]
        x, y, c, chips = _place()
        cps = [pltpu.make_async_remote_copy(
            src_ref=zones[a].at[4 * px + 2 * py + c], dst_ref=outs[a].at[4 * px + 2 * py + c], send_sem=send_sems.at[a, j],
            recv_sem=recv_sems.at[a, j], device_id=(x, y, 1 - c), device_id_type=MESH) for j, (px, py) in enumerate(chips) for a in range(n)]
        for cp in cps:
            cp.start()
        for cp in cps:
            cp.wait_recv()
        for cp in cps:
            cp.wait_send()

    return pl.pallas_call(
        body, name=name, in_specs=[ANY] * n, out_specs=[ANY] * n,
        out_shape=[jax.ShapeDtypeStruct(l.shape, l.dtype) for l in lands],
        input_output_aliases={a: a for a in range(n)},
        scratch_shapes=[pltpu.SemaphoreType.DMA((n, 3)), pltpu.SemaphoreType.DMA((n, 3))],
    )(*lands)


def _gather_relay(lands, name):
    n = len(lands)

    def body(*refs):
        zones, outs = refs[:n], refs[n:2 * n]
        send_sems, recv_sems = refs[2 * n:]
        x, y, c, _ = _place()
        south = c == 0
        near_x, near_y, across = 4 * (1 - x) + 2 * y + c, 4 * x + 2 * (1 - y) + c, 4 * (1 - x) + 2 * (1 - y) + c
        passed = jnp.where(south, near_y, near_x)
        onward = (jnp.where(south, 1 - x, x), jnp.where(south, y, 1 - y), c)

        def copy(a, k, slot, to):
            return pltpu.make_async_remote_copy(src_ref=zones[a].at[slot], dst_ref=outs[a].at[slot], send_sem=send_sems.at[a, k],
                                                recv_sem=recv_sems.at[a, k], device_id=to, device_id_type=MESH)

        first = [copy(a, 0, passed, onward) for a in range(n)]
        first += [copy(a, 1 + j, slot, (x, y, 1 - c)) for j, slot in enumerate((near_x, near_y)) for a in range(n)]
        for cp in first:
            cp.start()
        last = []
        for a in range(n):
            copy(a, 0, across, onward).wait_recv()
            last.append(copy(a, 3, across, (x, y, 1 - c)))
            last[-1].start()
        for cp in first[n:] + last:
            cp.wait_recv()
        for cp in first + last:
            cp.wait_send()

    return pl.pallas_call(
        body, name=name, in_specs=[ANY] * n, out_specs=[ANY] * n,
        out_shape=[jax.ShapeDtypeStruct(l.shape, l.dtype) for l in lands],
        input_output_aliases={a: a for a in range(n)},
        scratch_shapes=[pltpu.SemaphoreType.DMA((n, 4)), pltpu.SemaphoreType.DMA((n, 4))],
    )(*lands)


def _sum_everywhere(v, name):
    rows = v.shape[0]

    def body(v_ref, o_ref, buf, send_sems, recv_sems):
        x, y, c, _ = _place()
        me = 4 * x + 2 * y + c
        buf[me] = v_ref[...]
        cps = []
        for k in range(1, N_DEV):
            fx, fy, fc = (k >> 2) & 1, (k >> 1) & 1, k & 1
            to = (1 - x if fx else x, 1 - y if fy else y, 1 - c if fc else c)
            cps.append(pltpu.make_async_remote_copy(src_ref=buf.at[me], dst_ref=buf.at[me], send_sem=send_sems.at[k - 1],
                                                    recv_sem=recv_sems.at[k - 1], device_id=to, device_id_type=MESH))
        for cp in cps:
            cp.start()
        for cp in cps:
            cp.wait_recv()
        for cp in cps:
            cp.wait_send()
        acc = buf[0]
        for d in range(1, N_DEV):
            acc = acc + buf[d]
        o_ref[...] = acc

    vm = pl.BlockSpec(memory_space=pltpu.VMEM)
    return pl.pallas_call(
        body, name=name, in_specs=[vm], out_specs=vm, out_shape=jax.ShapeDtypeStruct(v.shape, F32),
        scratch_shapes=[pltpu.VMEM((N_DEV, rows, 128), F32), pltpu.SemaphoreType.DMA((N_DEV - 1,)), pltpu.SemaphoreType.DMA((N_DEV - 1,))],
    )(v)


def _adamw_parts(w, got, mine, me, m, v, name, after=()):
    r, c = w.shape
    n_parts = got.shape[0]
    steps, blk, at = _tiles(r, c)

    def body(me_ref, w_ref, got_ref, own_ref, m_ref, v_ref, *rest):
        go_ref, d_ref, mo_ref, vo_ref = rest[len(after):]
        own = own_ref[...].astype(F32)
        gv = jnp.where(me_ref[0] == 0, own, got_ref[0].astype(F32))
        for d in range(1, n_parts):
            gv = gv + jnp.where(me_ref[0] == d, own, got_ref[d].astype(F32))
        _adamw_math(gv, w_ref, m_ref, v_ref, go_ref, d_ref, mo_ref, vo_ref)

    tile = pl.BlockSpec(blk, lambda i, me_ref: at(i))
    out = jax.ShapeDtypeStruct((r, c), F32)
    return pl.pallas_call(
        body, name=name,
        grid_spec=pltpu.PrefetchScalarGridSpec(
            num_scalar_prefetch=1, grid=(steps,),
            in_specs=[tile, pl.BlockSpec((n_parts,) + blk, lambda i, me_ref: (0,) + at(i)),
                      pl.BlockSpec((None,) + blk, lambda i, me_ref: (me_ref[0],) + at(i)), tile, tile] + [ANY] * len(after),
            out_specs=[tile] * 4),
        out_shape=[out] * 4, compiler_params=_params(("parallel",)),
    )(me.astype(jnp.int32).reshape(1), w, got, mine, m, v, *after)


def _adamw_math(gv, w_ref, m_ref, v_ref, go_ref, d_ref, mo_ref, vo_ref):
    mn = B1 * m_ref[...] + (1.0 - B1) * gv
    vn = B2 * v_ref[...] + (1.0 - B2) * (gv * gv)
    m_hat = mn / (1.0 - B1 ** STEP)
    v_hat = vn / (1.0 - B2 ** STEP)
    go_ref[...] = gv
    d_ref[...] = -LR * (m_hat / (jnp.sqrt(v_hat) + ADAM_EPS) + WD * w_ref[...])
    mo_ref[...] = mn
    vo_ref[...] = vn


def _adamw(w, g, m, v, name):
    r, c = w.shape
    parts = g.ndim == 3
    tr = r if r <= 128 else _pick(r, (128, 64))

    def body(w_ref, g_ref, m_ref, v_ref, go_ref, d_ref, mo_ref, vo_ref):
        if parts:
            gv = g_ref[0].astype(F32)
            for d in range(1, g.shape[0]):
                gv = gv + g_ref[d].astype(F32)
        else:
            gv = g_ref[...]
        mn = B1 * m_ref[...] + (1.0 - B1) * gv
        vn = B2 * v_ref[...] + (1.0 - B2) * (gv * gv)
        m_hat = mn / (1.0 - B1 ** STEP)
        v_hat = vn / (1.0 - B2 ** STEP)
        go_ref[...] = gv
        d_ref[...] = -LR * (m_hat / (jnp.sqrt(v_hat) + ADAM_EPS) + WD * w_ref[...])
        mo_ref[...] = mn
        vo_ref[...] = vn

    tile = pl.BlockSpec((tr, c), lambda i: (i, 0))
    g_spec = pl.BlockSpec((g.shape[0], tr, c), lambda i: (0, i, 0)) if parts else tile
    out = jax.ShapeDtypeStruct((r, c), F32)
    return pl.pallas_call(
        body, name=name, grid=(r // tr,), in_specs=[tile, g_spec, tile, tile], out_specs=[tile] * 4, out_shape=[out] * 4,
        compiler_params=_params(("parallel",)),
    )(w, g, m, v)


BIG = ["w_in", "w_a_out", "w_b_out", "w_mix_out", "w_ff_gate", "w_ff_up", "w_ff_down", "w_ple_gate", "w_ple_proj"]
TRANSPOSED = ["w_in", "w_ff_gate", "w_ff_up"]
SMALL = ["conv_w", "w_alpha_up", "b_alpha_up", "gla_head_gain", "g_pre_mix", "g_post_mix", "g_pre_ffn", "g_post_ffn", "g_pre_ple", "g_post_ple"]
WEIGHTS = ["w_in", "conv_w", "w_a_out", "w_alpha_up", "b_alpha_up", "gla_head_gain", "w_b_out", "w_mix_out", "g_pre_mix", "g_post_mix",
           "g_pre_ffn", "g_post_ffn", "w_ff_gate", "w_ff_up", "w_ff_down", "g_pre_ple", "g_post_ple", "w_ple_gate", "w_ple_proj"]


def _in_t_from_blocks(z):
    w = z.reshape(-1, z.shape[-1])
    return jnp.concatenate([w[R_GA:R_END], w[:R_ALR], w[R_ALR:R_GA], jnp.zeros((128 - GATE_RANK, w.shape[1]), w.dtype)], axis=0)


def _blocks_from_in_t(g):
    per = R_END // N_DEV

    def rows(lo, hi):
        out = []
        for n0, n1, p0 in ((0, R_ALR, C_AX), (R_ALR, R_GA, C_ALR), (R_GA, R_END, 0)):
            a, e = max(lo, n0), min(hi, n1)
            if a < e:
                out.append(g[p0 + a - n0:p0 + e - n0])
        return out

    return jnp.stack([jnp.concatenate(rows(b * per, (b + 1) * per), axis=0) for b in range(N_DEV)])


def _cols_to_full(g8):
    n, r, c = g8.shape
    return jnp.transpose(g8, (1, 0, 2)).reshape(r, n * c)


def _full_to_cols(a):
    r, c = a.shape
    return jnp.transpose(a.reshape(r, N_DEV, c // N_DEV), (1, 0, 2))


def _pack(arrs, rows):
    flat = jnp.concatenate([a.reshape(-1) for a in arrs])
    return jnp.pad(flat, (0, rows * 128 - flat.shape[0])).reshape(rows, 128)


def _unpack(packed, shapes):
    flat, out, o = packed.reshape(-1), [], 0
    for s in shapes:
        size = 1
        for d in s:
            size *= d
        out.append(flat[o:o + size].reshape(s))
        o += size
    return out


def kernel(x, p, w_in, conv_w, w_a_out, w_alpha_up, b_alpha_up, gla_head_gain, w_b_out, w_mix_out, g_pre_mix, g_post_mix, g_pre_ffn, g_post_ffn, w_ff_gate, w_ff_up, w_ff_down, g_pre_ple, g_post_ple, w_ple_gate, w_ple_proj, loss_target, m_w_in, m_conv_w, m_w_a_out, m_w_alpha_up, m_b_alpha_up, m_gla_head_gain, m_w_b_out, m_w_mix_out, m_g_pre_mix, m_g_post_mix, m_g_pre_ffn, m_g_post_ffn, m_w_ff_gate, m_w_ff_up, m_w_ff_down, m_g_pre_ple, m_g_post_ple, m_w_ple_gate, m_w_ple_proj, v_w_in, v_conv_w, v_w_a_out, v_w_alpha_up, v_b_alpha_up, v_gla_head_gain, v_w_b_out, v_w_mix_out, v_g_pre_mix, v_g_post_mix, v_g_pre_ffn, v_g_post_ffn, v_w_ff_gate, v_w_ff_up, v_w_ff_down, v_g_pre_ple, v_g_post_ple, v_w_ple_gate, v_w_ple_proj):
    args = dict(locals())
    shard = lambda n, a: jnp.transpose(a[0]) if n in TRANSPOSED else a[0]
    wts = {n: shard(n, args[n]) for n in WEIGHTS}
    mom = {n: shard(n, args["m_" + n]) for n in WEIGHTS}
    var = {n: shard(n, args["v_" + n]) for n in WEIGHTS}
    me =4 * lax.axis_index("x") + 2 * lax.axis_index("y") + lax.axis_index("c")

    groups = [["w_in", "conv_w", "w_alpha_up"], ["w_a_out", "w_b_out", "w_mix_out"], ["w_ff_gate", "w_ff_up"], ["w_ff_down"],
              ["w_ple_gate", "w_ple_proj"]]
    grad_groups = [["w_ple_proj", "w_ple_gate"], ["w_ff_down", "w_ff_gate", "w_ff_up"], ["w_mix_out", "w_a_out", "w_b_out"], ["w_in"]]
    rows_full = lambda g: g.reshape(-1, g.shape[-1])
    gathers, scatters = {}, {}

    def gather_start(gi, after):
        if gi not in gathers:
            shards = [wts[n].astype(BF16) if n in BIG else wts[n] for n in groups[gi]]
            zones = [jax.ShapeDtypeStruct((N_DEV,) + s.shape, s.dtype) for s in shards]
            plan, peers = (_gather_plan_near, 4) if gi == 0 else (_gather_plan, 5)
            gathers[gi] = (shards, _copies_start(plan, peers * len(shards), shards, zones, "gather_start_%d" % gi, after))
        return gathers[gi][1][-1]

    def gather_finish(gi, after):
        shards, state = gathers[gi]
        shards, zones = _copies_wait(_gather_plan_near if gi == 0 else _gather_plan, state, len(shards), "gather_wait_%d" % gi, after)
        if gi == 0:
            zones = _gather_relay(zones, "gather_relay_%d" % gi)
            gather_start(1, (zones[0],))
        else:
            zones = _gather_finish(zones, "gather_finish_%d" % gi)
        g8 = dict(zip(groups[gi], zones))
        if gi == 0:
            return dict(in_t=_in_t_from_blocks(g8["w_in"]),
                        conv_w=_cols_to_full(g8["conv_w"]), w_alpha_up=_cols_to_full(g8["w_alpha_up"]))
        if gi == 1:
            return dict(a_out=g8["w_a_out"], b_out=g8["w_b_out"], mix=rows_full(g8["w_mix_out"]))
        if gi == 2:
            return dict(gate_t=rows_full(g8["w_ff_gate"]), up_t=rows_full(g8["w_ff_up"]))
        if gi == 3:
            return dict(down=rows_full(g8["w_ff_down"]))
        return dict(pg=rows_full(g8["w_ple_gate"]), pp=g8["w_ple_proj"])

    def scatter_start(gi, gw):
        if gi == 3:
            full = dict(w_in=_blocks_from_in_t(gw["in_t"]))
        elif gi == 1:
            full = dict(w_ff_down=gw["down"], w_ff_gate=gw["gate_t"], w_ff_up=gw["up_t"])
        elif gi == 2:
            full = dict(w_mix_out=gw["mix"], w_a_out=gw["a_out"], w_b_out=gw["b_out"])
        else:
            full = dict(w_ple_proj=gw["pp"], w_ple_gate=gw["pg"])
        parts = [full[n] if full[n].ndim == 3 else full[n].reshape(N_DEV, -1, full[n].shape[-1]) for n in grad_groups[gi]]
        if gi == 3:
            from_sibling = _sibling_exchange(parts, "scatter_sibling_%d" % gi)
            parts = [_pair_add(a, b, "scatter_add_%d_%s" % (gi, n)) for n, a, b in zip(grad_groups[gi], parts, from_sibling)]
            scatters[gi] = _copies_start(_chip_plan, 3 * len(parts), parts, parts, "scatter_start_%d" % gi)
        else:
            scatters[gi] = _copies_start(_scatter_plan, (N_DEV - 1) * len(parts), parts, parts, "scatter_start_%d" % gi)
        return scatters[gi][-1]

    small = {n: wts[n].reshape(1, -1) for n in SMALL[2:]}

    loss_rows, grad_x, gs = _local_step(x[0], p[0, 0], loss_target[0], gather_start, gather_finish, scatter_start, small)
    gs["loss"] = jnp.sum(loss_rows).reshape(1, 1)

    small_shapes = [gs[n].shape for n in SMALL]
    gs_packed = _pack([gs[n] for n in SMALL + ["loss"]], 192)
    small_state = _copies_start(_everyone_plan, N_DEV - 1, [gs_packed], [jax.ShapeDtypeStruct((N_DEV,) + gs_packed.shape, F32)],
                                "small_start", (grad_x,))

    res, done = {}, (small_state[-1],)
    for gi, names in enumerate(grad_groups):
        plan, slot = (_chip_plan, me // 2) if gi == 3 else (_scatter_plan, me)
        mine, got = _copies_wait(plan, scatters[gi], len(names), "scatter_wait_%d" % gi, done)
        for n, g, own in zip(names, got, mine):
            res[n] = _adamw_parts(wts[n], g, own, slot, mom[n], var[n], "adamw_" + n)
        done = tuple(res[n][1] for n in names)

    (gs_own,), (gs_got,) = _copies_wait(_everyone_plan, small_state, 1, "small_wait", done)
    gsum = dict(zip(SMALL + ["loss"], _unpack(_sum_parts(gs_got, gs_own, me, "small_sum"), small_shapes + [(1, 1)])))
    loss = gsum["loss"].reshape(())
    gsum["conv_w"] = lax.dynamic_index_in_dim(gsum["conv_w"].reshape(3, N_DEV, -1), me, axis=1, keepdims=False)
    gsum["w_alpha_up"] = lax.dynamic_index_in_dim(gsum["w_alpha_up"].reshape(GATE_RANK, N_DEV, -1), me, axis=1, keepdims=False)

    shard_shapes = [wts[n].shape for n in SMALL]
    packed = [_pack([d[n] for n in SMALL], 120) for d in (wts, gsum, mom, var)]
    outs = [_unpack(o, shard_shapes) for o in _adamw(*packed, "adamw_small")]
    for i, n in enumerate(SMALL):
        res[n] = [o[i] for o in outs]

    back = lambda n, a: (jnp.transpose(a) if n in TRANSPOSED else a)[None]
    return (loss, grad_x[None], *[back(n, res[n][i]) for i in range(4) for n in WEIGHTS])
```

```python
import functools

import jax
import jax.numpy as jnp
from jax import lax
from jax.experimental import pallas as pl
from jax.experimental.pallas import tpu as pltpu
from jax.experimental.pallas import tpu_sc as plsc

F32, BF16 = jnp.float32, jnp.bfloat16
EPS = 1e-6
CHUNK = 64
STEP_CHUNKS = 2
STEP_ROWS = STEP_CHUNKS * CHUNK
HEADS, DK, DV = 4, 128, 256
GATE_RANK = 16
TAU = 16.0
LR, B1, B2, ADAM_EPS, WD, STEP = 0.001, 0.9, 0.999, 1e-08, 0.01, 10
N_DEV = 8
MESH = pl.DeviceIdType.MESH
VMEM_LIMIT = 56 * 1024 * 1024
ANY = pl.BlockSpec(memory_space=pl.ANY)

C_GA, C_GB, C_AX, C_AB, C_AC, C_Q, C_K, C_V, C_OG, C_ALR = 0, 2048, 4096, 5120, 6144, 7168, 7680, 8192, 9216, 10240
IN_PAD = 10368
R_AX, R_AB, R_AC, R_Q, R_K, R_V, R_OG, R_ALR, R_GA, R_GB, R_END = 0, 1024, 2048, 3072, 3584, 4096, 5120, 6144, 6160, 8208, 10256


def _params(sem):
    return pltpu.CompilerParams(dimension_semantics=sem, vmem_limit_bytes=VMEM_LIMIT)


def _pick(n, cands):
    for c in cands:
        if n % c == 0:
            return c
    return n


def _tiles(r, c):
    for tr in (128, 64):
        if r % tr == 0:
            return r // tr, (tr, c), lambda i: (i, 0)
    tc = _pick(c, (256, 128))
    return c // tc, (r, tc), lambda i: (0, i)


def _mm(a, b, mode, out_dtype, name, after=(), add=None, b3=False, out3=False, tm=None, tk=None):
    bshape = (b.shape[1], N_DEV * b.shape[2]) if b3 else b.shape
    if mode == "nn":
        (m, k), (k2, n) = a.shape, bshape
    elif mode == "nt":
        (m, k), (n, k2) = a.shape, bshape
    else:
        (k, m), (k2, n) = a.shape, bshape
    assert k == k2 and a.dtype == BF16 and b.dtype == BF16, (name, a.shape, b.shape, a.dtype, b.dtype)
    tm = tm if tm and m % tm == 0 else _pick(m, (2048, 1024, 512, 256))
    tn = _pick(n, (1152, 1024, 1408, 512, 256))
    tk = tk if tk and k % tk == 0 else _pick(k, (2048, 1408, 1152, 1024, 512, 256))
    if out3 or (b3 and mode == "nn"):
        tn = n // N_DEV
    if b3 and mode == "nt":
        tk = k // N_DEV
    nk = k // tk
    dims = {"nn": (((1,), (0,)), ((), ())), "nt": (((1,), (1,)), ((), ())), "tn": (((0,), (0,)), ((), ()))}[mode]
    n_extra = len(after) + (add is not None)

    def body(a_ref, b_ref, *rest):
        o_ref = rest[n_extra]
        prod = lax.dot_general(a_ref[...], b_ref[...], dims, preferred_element_type=F32)
        if nk == 1:
            o_ref[...] = (prod if add is None else prod + rest[0][...]).astype(o_ref.dtype)
            return
        acc_ref = rest[n_extra + 1]
        kk = pl.program_id(2)

        @pl.when(kk == 0)
        def _():
            acc_ref[...] = prod if add is None else prod + rest[0][...]

        @pl.when((kk > 0) & (kk < nk - 1))
        def _():
            acc_ref[...] += prod

        @pl.when(kk == nk - 1)
        def _():
            o_ref[...] = (acc_ref[...] + prod).astype(o_ref.dtype)

    a_spec = pl.BlockSpec((tk, tm), lambda i, j, kk: (kk, i)) if mode == "tn" else pl.BlockSpec((tm, tk), lambda i, j, kk: (i, kk))
    if b3:
        b_spec = (pl.BlockSpec((None, tn, tk), lambda i, j, kk: (kk, j, 0)) if mode == "nt"
                  else pl.BlockSpec((None, tk, tn), lambda i, j, kk: (j, kk, 0)))
    else:
        b_spec = pl.BlockSpec((tn, tk), lambda i, j, kk: (j, kk)) if mode == "nt" else pl.BlockSpec((tk, tn), lambda i, j, kk: (kk, j))
    tile = pl.BlockSpec((tm, tn), lambda i, j, kk: (i, j))
    out_spec = pl.BlockSpec((None, tm, tn), lambda i, j, kk: (j, i, 0)) if out3 else tile
    return pl.pallas_call(
        body, name=name, grid=(m // tm, n // tn, nk),
        in_specs=[a_spec, b_spec] + ([tile] if add is not None else []) + [ANY] * len(after), out_specs=out_spec,
        out_shape=jax.ShapeDtypeStruct((N_DEV, m, tn) if out3 else (m, n), out_dtype),
        scratch_shapes=[pltpu.VMEM((tm, tn), F32)] if nk > 1 else [],
        compiler_params=_params(("parallel", "parallel", "arbitrary")),
    )(a, b, *([add] if add is not None else []), *after)


def _rows(body, t, tr, ins, outs, name):
    in_specs = []
    for arr, sp in ins:
        if sp[0] == "t":
            in_specs.append(pl.BlockSpec((tr, sp[1]), lambda i, cb=sp[2]: (i, cb)))
        else:
            in_specs.append(pl.BlockSpec(arr.shape, lambda i, nd=arr.ndim: (0,) * nd))
    out_specs, out_shape = [], []
    for shape, dt, kind in outs:
        out_specs.append(pl.BlockSpec((tr, shape[1]), lambda i: (i, 0)) if kind == "t" else pl.BlockSpec(shape, lambda i: (0, 0)))
        out_shape.append(jax.ShapeDtypeStruct(shape, dt))
    return pl.pallas_call(
        body, name=name, grid=(t // tr,), in_specs=in_specs, out_specs=out_specs, out_shape=out_shape,
        compiler_params=_params(("arbitrary",)),
    )(*[arr for arr, _ in ins])


def _rinv(v):
    return lax.rsqrt(jnp.mean(v * v, axis=-1, keepdims=True) + EPS)


def _sig(v):
    return 1.0 / (1.0 + jnp.exp(-v))


def _acc(ref, val):
    @pl.when(pl.program_id(0) == 0)
    def _():
        ref[...] = jnp.zeros_like(ref)

    ref[...] += jnp.sum(val, axis=0, keepdims=True)


def _rms_fwd(x, g, name):
    t, d = x.shape

    def body(x_ref, g_ref, h_ref):
        xv = x_ref[...]
        h_ref[...] = (xv * _rinv(xv) * g_ref[...]).astype(BF16)

    return _rows(body, t, 256, [(x, ("t", d, 0)), (g, ("b",))], [((t, d), BF16, "t")], name)[0]


def _post_pre(x, m, g_post, g_pre, name):
    t, d = x.shape

    def body(x_ref, m_ref, gp_ref, gn_ref, xo_ref, h_ref):
        mv = m_ref[...]
        xn = x_ref[...] + mv * _rinv(mv) * gp_ref[...]
        xo_ref[...] = xn
        h_ref[...] = (xn * _rinv(xn) * gn_ref[...]).astype(BF16)

    return _rows(body, t, 128, [(x, ("t", d, 0)), (m, ("t", d, 0)), (g_post, ("b",)), (g_pre, ("b",))],
                 [((t, d), F32, "t"), ((t, d), BF16, "t")], name)


def _mix_fwd(proj, ya, yb, name):
    t, d = ya.shape

    def body(ga_ref, gb_ref, ya_ref, yb_ref, o_ref):
        o_ref[...] = (_sig(ga_ref[...].astype(F32)) * ya_ref[...].astype(F32)
                      + _sig(gb_ref[...].astype(F32)) * yb_ref[...].astype(F32)).astype(BF16)

    return _rows(body, t, 256, [(proj, ("t", d, C_GA // d)), (proj, ("t", d, C_GB // d)), (ya, ("t", d, 0)), (yb, ("t", d, 0))],
                 [((t, d), BF16, "t")], name)[0]


def _mix_bwd(dmix, proj, ya, yb, name):
    t, d = ya.shape

    def body(dm_ref, ga_ref, gb_ref, ya_ref, yb_ref, dg_ref, dya_ref, dyb_ref):
        dm = dm_ref[...]
        sa, sb = _sig(ga_ref[...].astype(F32)), _sig(gb_ref[...].astype(F32))
        dg_ref[:, :d] = (dm * ya_ref[...].astype(F32) * sa * (1.0 - sa)).astype(BF16)
        dg_ref[:, d:] = (dm * yb_ref[...].astype(F32) * sb * (1.0 - sb)).astype(BF16)
        dya_ref[...] = (dm * sa).astype(BF16)
        dyb_ref[...] = (dm * sb).astype(BF16)

    return _rows(body, t, 128,
                 [(dmix, ("t", d, 0)), (proj, ("t", d, C_GA // d)), (proj, ("t", d, C_GB // d)), (ya, ("t", d, 0)), (yb, ("t", d, 0))],
                 [((t, 2 * d), BF16, "t"), ((t, d), BF16, "t"), ((t, d), BF16, "t")], name)


def _swiglu_call(body, ins, n_out, name):
    t, f = ins[0].shape
    tc = _pick(f, (1408, 512))
    tile = pl.BlockSpec((512, tc), lambda i, j: (i, j))
    return pl.pallas_call(
        body, name=name, grid=(t // 512, f // tc), in_specs=[tile] * len(ins), out_specs=[tile] * n_out,
        out_shape=[jax.ShapeDtypeStruct((t, f), BF16)] * n_out, compiler_params=_params(("parallel", "parallel")),
    )(*ins)


def _swiglu_fwd(fg, fu, name):
    def body(g_ref, u_ref, s_ref):
        gv = g_ref[...].astype(F32)
        s_ref[...] = (gv * _sig(gv) * u_ref[...].astype(F32)).astype(BF16)

    return _swiglu_call(body, [fg, fu], 1, name)[0]


def _swiglu_bwd(ds, fg, fu, name):
    def body(ds_ref, g_ref, u_ref, dg_ref, du_ref):
        dsv, gv, uv = ds_ref[...].astype(F32), g_ref[...].astype(F32), u_ref[...].astype(F32)
        sg = _sig(gv)
        dg_ref[...] = (dsv * uv * sg * (1.0 + gv * (1.0 - sg))).astype(BF16)
        du_ref[...] = (dsv * gv * sg).astype(BF16)

    return _swiglu_call(body, [ds, fg, fu], 2, name)


def _ple_final(x2, pg, pp, tgt, g_post, name):
    t, d = x2.shape

    def body(x_ref, pg_ref, pp_ref, t_ref, g_ref, loss_ref, d3_ref, dpg_ref, dpp_ref, dg_ref):
        sg, ppv, g = _sig(pg_ref[...]), pp_ref[...], g_ref[...]
        e = sg * ppv
        r = _rinv(e)
        eh = e * r
        diff = x_ref[...] + eh * g - t_ref[...]
        loss_ref[...] = 0.5 * jnp.mean(diff * diff, axis=-1, keepdims=True)
        d3 = diff * (1.0 / d)
        d3_ref[...] = d3
        gd = d3 * g
        de = r * (gd - eh * jnp.mean(gd * eh, axis=-1, keepdims=True))
        dpg_ref[...] = (de * ppv * sg * (1.0 - sg)).astype(BF16)
        dpp_ref[...] = (de * sg).astype(BF16)
        _acc(dg_ref, d3 * eh)

    return _rows(body, t, 128, [(x2, ("t", d, 0)), (pg, ("t", d, 0)), (pp, ("t", d, 0)), (tgt, ("t", d, 0)), (g_post, ("b",))],
                 [((t, 1), F32, "t"), ((t, d), F32, "t"), ((t, d), BF16, "t"), ((t, d), BF16, "t"), ((1, d), F32, "a")], name)


def _norm_bwd(dn, dh, x, g_pre, fm, g_post, name):
    t, d = x.shape
    two = fm is not None

    def body(*refs):
        if two:
            dn_ref, dh_ref, x_ref, gp_ref, f_ref, gq_ref, dx_ref, df_ref, dgp_ref, dgq_ref = refs
        else:
            dn_ref, dh_ref, x_ref, gp_ref, dx_ref, dgp_ref = refs
        xv, dhv = x_ref[...], dh_ref[...]
        r = _rinv(xv)
        xh = xv * r
        gd = dhv * gp_ref[...]
        dx = dn_ref[...] + r * (gd - xh * jnp.mean(gd * xh, axis=-1, keepdims=True))
        dx_ref[...] = dx
        _acc(dgp_ref, dhv * xh)
        if two:
            fv = f_ref[...]
            rf = _rinv(fv)
            fh = fv * rf
            gd2 = dx * gq_ref[...]
            df_ref[...] = (rf * (gd2 - fh * jnp.mean(gd2 * fh, axis=-1, keepdims=True))).astype(BF16)
            _acc(dgq_ref, dx * fh)

    ins = [(dn, ("t", d, 0)), (dh, ("t", d, 0)), (x, ("t", d, 0)), (g_pre, ("b",))]
    outs = [((t, d), F32, "t")]
    if two:
        ins += [(fm, ("t", d, 0)), (g_post, ("b",))]
        outs += [((t, d), BF16, "t"), ((1, d), F32, "a"), ((1, d), F32, "a")]
    else:
        outs += [((1, d), F32, "a")]
    return _rows(body, t, 128, ins, outs, name)


CONV_TC = 256


def _shift_down(v, s):
    rows = lax.broadcasted_iota(jnp.int32, v.shape, 0)
    return jnp.where(rows >= s, pltpu.roll(v, s, 0), 0.0)


def _shift_up(v, s):
    n = v.shape[0]
    rows = lax.broadcasted_iota(jnp.int32, v.shape, 0)
    return jnp.where(rows < n - s, pltpu.roll(v, n - s, 0), 0.0)


def _conv_specs(t):
    nb = 1024 // CONV_TC
    seg = lambda c0: pl.BlockSpec((t, CONV_TC), lambda j, cb=c0 // CONV_TC: (0, cb + j))
    own = pl.BlockSpec((t, CONV_TC), lambda j: (0, j))
    wspec = pl.BlockSpec((3, CONV_TC), lambda j: (0, j))
    return nb, seg, own, wspec


def _conv_fwd(proj, conv_w, name, after=()):
    t = proj.shape[0]
    nb, seg, own, wspec = _conv_specs(t)

    def body(ax_ref, ab_ref, ac_ref, w_ref, *rest):
        za_ref = rest[len(after)]
        u = ac_ref[...].astype(F32) * ax_ref[...].astype(F32)
        w = w_ref[...]
        yc = w[0:1] * _shift_down(u, 2) + w[1:2] * _shift_down(u, 1) + w[2:3] * u
        za_ref[...] = (ab_ref[...].astype(F32) * yc).astype(BF16)

    return pl.pallas_call(
        body, name=name, grid=(nb,), in_specs=[seg(C_AX), seg(C_AB), seg(C_AC), wspec] + [ANY] * len(after), out_specs=own,
        out_shape=jax.ShapeDtypeStruct((t, 1024), BF16), compiler_params=_params(("parallel",)),
    )(proj, proj, proj, conv_w, *after)


def _conv_bwd(dza, proj, conv_w, name):
    t = proj.shape[0]
    nb, seg, own, wspec = _conv_specs(t)

    def body(dz_ref, ax_ref, ab_ref, ac_ref, w_ref, dax_ref, dab_ref, dac_ref, dw_ref):
        ax, ab, ac, dz = ax_ref[...].astype(F32), ab_ref[...].astype(F32), ac_ref[...].astype(F32), dz_ref[...].astype(F32)
        w = w_ref[...]
        u = ac * ax
        u1, u2 = _shift_down(u, 1), _shift_down(u, 2)
        yc = w[0:1] * u2 + w[1:2] * u1 + w[2:3] * u
        dab_ref[...] = (dz * yc).astype(BF16)
        dyc = dz * ab
        du = w[2:3] * dyc + w[1:2] * _shift_up(dyc, 1) + w[0:1] * _shift_up(dyc, 2)
        dax_ref[...] = (du * ac).astype(BF16)
        dac_ref[...] = (du * ax).astype(BF16)
        dw_ref[0:1, :] = jnp.sum(dyc * u2, axis=0, keepdims=True)
        dw_ref[1:2, :] = jnp.sum(dyc * u1, axis=0, keepdims=True)
        dw_ref[2:3, :] = jnp.sum(dyc * u, axis=0, keepdims=True)

    act = jax.ShapeDtypeStruct((t, 1024), BF16)
    return pl.pallas_call(
        body, name=name, grid=(nb,), in_specs=[own, seg(C_AX), seg(C_AB), seg(C_AC), wspec], out_specs=[own, own, own, wspec],
        out_shape=[act, act, act, jax.ShapeDtypeStruct((3, 1024), F32)], compiler_params=_params(("parallel",)),
    )(dza, proj, proj, proj, conv_w)


def _dot(a, b, dims, precision=None):
    return lax.dot_general(a, b, (dims, ((), ())), precision=precision, preferred_element_type=F32)


_CONTRACT = {"nn": ((1,), (0,)), "nt": ((1,), (1,)), "tn": ((0,), (0,))}


def _bdot_raw(a, b, mode):
    return _dot(a.astype(BF16), b.astype(BF16), _CONTRACT[mode])


@functools.partial(jax.custom_vjp, nondiff_argnums=(2,))
def _bdot(a, b, mode):
    return _bdot_raw(a, b, mode)


def _bdot_fwd(a, b, mode):
    return _bdot_raw(a, b, mode), (a, b)


def _bdot_bwd(mode, res, ct):
    a, b = res
    if mode == "nn":
        return _bdot_raw(ct, b, "nt"), _bdot_raw(a, ct, "tn")
    if mode == "nt":
        return _bdot_raw(ct, b, "nn"), _bdot_raw(ct, a, "tn")
    return _bdot_raw(b, ct, "nt"), _bdot_raw(a, ct, "nn")


_bdot.defvjp(_bdot_fwd, _bdot_bwd)


@functools.partial(jax.custom_vjp, nondiff_argnums=(2,))
def _sum_dot(ones, x, mode):
    head = x.astype(BF16)
    tail = x - head.astype(F32)
    if mode == "nn":
        return _bdot_raw(ones, head, "nn") + _bdot_raw(ones, tail, "nn")
    return _bdot_raw(head, ones, "tn") + _bdot_raw(tail, ones, "tn")


def _sum_dot_fwd(ones, x, mode):
    return _sum_dot(ones, x, mode), ones


def _sum_dot_bwd(mode, ones, ct):
    return jnp.zeros_like(ones), (_bdot_raw(ones, ct, "tn") if mode == "nn" else _bdot_raw(ones, ct, "nt"))


_sum_dot.defvjp(_sum_dot_fwd, _sum_dot_bwd)


def _gla_chunk(q, k, v, og, alr, s_in, wa, ba, gain):
    c = q.shape[0]
    z =_bdot(alr, wa, "nn") + ba
    la = (jnp.minimum(z, 0.0) - jnp.log(1.0 + jnp.exp(-jnp.abs(z)))) * (1.0 / TAU)
    row = lax.broadcasted_iota(jnp.int32, (c, c), 0)
    col = lax.broadcasted_iota(jnp.int32, (c, c), 1)
    lower = row >= col
    b = _sum_dot(lower.astype(F32), la, "nn")
    trow = lax.broadcasted_iota(jnp.int32, la.shape, 0)
    mid = jnp.sum(jnp.where(trow <= c // 2, la, 0.0), axis=0, keepdims=True)
    blast = jnp.sum(la, axis=0, keepdims=True)
    qs = q * (DK ** -0.5)
    e_up, e_dn = jnp.exp(b - mid), jnp.exp(mid - b)
    a_fwd = _bdot(qs * e_up, k * e_dn, "nt")
    a_rev = _bdot(qs * e_dn, k * e_up, "nt")
    att = jnp.where(lower, a_fwd, a_rev)
    o = _bdot(att, v, "nn") + _bdot(qs * jnp.exp(b), s_in, "nn")
    upd = _bdot(k * jnp.exp(blast - b), v, "tn")
    blast_col = _sum_dot(jnp.ones((c, DV), F32), la, "tn")
    s_out = jnp.exp(blast_col) * s_in + upd
    on = o * _rinv(o) * gain
    return on * og * _sig(og), s_out


def _gla_specs(t, rev):
    n = t // STEP_ROWS
    ch = (lambda i: n - 1 - i) if rev else (lambda i: i)
    col = lambda w, c0: pl.BlockSpec((STEP_ROWS, HEADS * w), lambda i, cb=c0 // (HEADS * w): (ch(i), cb))
    whole = lambda shape: pl.BlockSpec(shape, lambda i, nd=len(shape): (0,) * nd)
    specs = dict(
        q=col(DK, C_Q), k=col(DK, C_K), v=col(DV, C_V), og=col(DV, C_OG),
        alr=pl.BlockSpec((STEP_ROWS, 128), lambda i: (ch(i), C_ALR // 128)),
        wa=whole((128, HEADS * DK)), ba=whole((1, HEADS * DK)), gain=whole((1, DV)),
        state=pl.BlockSpec((STEP_CHUNKS, HEADS, DK, DV), lambda i: (ch(i), 0, 0, 0)),
        odk=pl.BlockSpec((STEP_ROWS, HEADS * DK), lambda i: (ch(i), 0)), odv=pl.BlockSpec((STEP_ROWS, HEADS * DV), lambda i: (ch(i), 0)),
        oalr=pl.BlockSpec((STEP_ROWS, 128), lambda i: (ch(i), 0)), whole=whole,
    )
    return n, specs


def _head_cols(h):
    return slice(h * DK, (h + 1) * DK), slice(h * DV, (h + 1) * DV)


def _gla_fwd(proj, wa, ba, gain, name):
    t = proj.shape[0]
    n, sp = _gla_specs(t, False)

    def body(q_ref, k_ref, v_ref, og_ref, alr_ref, wa_ref, ba_ref, g_ref, zb_ref, st_ref, s_scr):
        @pl.when(pl.program_id(0) == 0)
        def _():
            s_scr[...] = jnp.zeros_like(s_scr)

        state = [s_scr[h] for h in range(HEADS)]
        for c in range(STEP_CHUNKS):
            rows = slice(c * CHUNK, (c + 1) * CHUNK)
            alr = alr_ref[rows, :].astype(F32)
            for h in range(HEADS):
                kc, vc = _head_cols(h)
                st_ref[c, h] = state[h]
                zb, state[h] = _gla_chunk(q_ref[rows, kc].astype(F32), k_ref[rows, kc].astype(F32), v_ref[rows, vc].astype(F32),
                                          og_ref[rows, vc].astype(F32), alr, state[h], wa_ref[:, kc].astype(F32), ba_ref[:, kc], g_ref[...])
                zb_ref[rows, vc] = zb.astype(BF16)
        for h in range(HEADS):
            s_scr[h] = state[h]

    return pl.pallas_call(
        body, name=name, grid=(n,),
        in_specs=[sp["q"], sp["k"], sp["v"], sp["og"], sp["alr"], sp["wa"], sp["ba"], sp["gain"]],
        out_specs=[sp["odv"], sp["state"]],
        out_shape=[jax.ShapeDtypeStruct((t, HEADS * DV), BF16), jax.ShapeDtypeStruct((t // CHUNK, HEADS, DK, DV), F32)],
        scratch_shapes=[pltpu.VMEM((HEADS, DK, DV), F32)],
        compiler_params=_params(("arbitrary",)),
    )(proj, proj, proj, proj, proj, wa, ba, gain)


def _gla_bwd(dzb, proj, states, wa, ba, gain, name):
    t = proj.shape[0]
    n, sp = _gla_specs(t, True)

    def body(dz_ref, q_ref, k_ref, v_ref, og_ref, alr_ref, st_ref, wa_ref, ba_ref, g_ref,
             dq_ref, dk_ref, dv_ref, dog_ref, dalr_ref, dwa_ref, dba_ref, dg_ref, ds_scr):
        @pl.when(pl.program_id(0) == 0)
        def _():
            ds_scr[...] = jnp.zeros_like(ds_scr)
            dwa_ref[...] = jnp.zeros_like(dwa_ref)
            dba_ref[...] = jnp.zeros_like(dba_ref)
            dg_ref[...] = jnp.zeros_like(dg_ref)

        dstate = [ds_scr[h] for h in range(HEADS)]
        dwa_sum, dba_sum, dgain_sum = [None] * HEADS, [None] * HEADS, None
        for c in reversed(range(STEP_CHUNKS)):
            rows = slice(c * CHUNK, (c + 1) * CHUNK)
            alr = alr_ref[rows, :].astype(F32)
            dalr_sum = None
            for h in range(HEADS):
                kc, vc = _head_cols(h)
                args = (q_ref[rows, kc].astype(F32), k_ref[rows, kc].astype(F32), v_ref[rows, vc].astype(F32), og_ref[rows, vc].astype(F32),
                        alr, st_ref[c, h], wa_ref[:, kc].astype(F32), ba_ref[:, kc], g_ref[...])
                _, vjp = jax.vjp(_gla_chunk, *args)
                dq, dk, dv, dog, dalr, dstate[h], dwa, dba, dgain = vjp((dz_ref[rows, vc].astype(F32), dstate[h]))
                dq_ref[rows, kc] = dq.astype(BF16)
                dk_ref[rows, kc] = dk.astype(BF16)
                dv_ref[rows, vc] = dv.astype(BF16)
                dog_ref[rows, vc] = dog.astype(BF16)
                dwa_sum[h] = dwa if dwa_sum[h] is None else dwa_sum[h] + dwa
                dba_sum[h] = dba if dba_sum[h] is None else dba_sum[h] + dba
                dalr_sum = dalr if dalr_sum is None else dalr_sum + dalr
                dgain_sum = dgain if dgain_sum is None else dgain_sum + dgain
            dalr_ref[rows, :] = dalr_sum
        for h in range(HEADS):
            ds_scr[h] = dstate[h]
            dwa_ref[h] += dwa_sum[h]
            dba_ref[h] += dba_sum[h]
        dg_ref[...] += dgain_sum

    whole = sp["whole"]
    return pl.pallas_call(
        body, name=name, grid=(n,),
        in_specs=[sp["odv"], sp["q"], sp["k"], sp["v"], sp["og"], sp["alr"], sp["state"], sp["wa"], sp["ba"], sp["gain"]],
        out_specs=[sp["odk"], sp["odk"], sp["odv"], sp["odv"], sp["oalr"], whole((HEADS, 128, DK)), whole((HEADS, 1, DK)), whole((1, DV))],
        out_shape=[jax.ShapeDtypeStruct((t, HEADS * DK), BF16), jax.ShapeDtypeStruct((t, HEADS * DK), BF16),
                   jax.ShapeDtypeStruct((t, HEADS * DV), BF16), jax.ShapeDtypeStruct((t, HEADS * DV), BF16),
                   jax.ShapeDtypeStruct((t, 128), F32), jax.ShapeDtypeStruct((HEADS, 128, DK), F32),
                   jax.ShapeDtypeStruct((HEADS, 1, DK), F32), jax.ShapeDtypeStruct((1, DV), F32)],
        scratch_shapes=[pltpu.VMEM((HEADS, DK, DV), F32)],
        compiler_params=_params(("arbitrary",)),
    )(dzb, proj, proj, proj, proj, proj, states, wa, ba, gain)


def _local_step(x, p, tgt, gather_start, gather_finish, scatter_start, small):
    b_alpha, gain = small["b_alpha_up"], small["gla_head_gain"]
    gather_start(0, ())
    w = dict(gather_finish(0, ()))
    conv_w, w_alpha = w["conv_w"], w["w_alpha_up"]
    wa_p = jnp.zeros((128, HEADS * DK), BF16).at[:GATE_RANK].set(w_alpha.astype(BF16))

    t2 = gather_start(2, (w["in_t"], gather_start(1, ())))
    h1 = _rms_fwd(x, small["g_pre_mix"], "rms_pre_mix")
    proj = _mm(h1, w["in_t"], "nt", BF16, "mm_proj", after=(t2,))
    za = _conv_fwd(proj, conv_w, "conv_fwd")
    zb, states = _gla_fwd(proj, wa_p, b_alpha, gain, "gla_fwd")
    t3 = gather_start(3, (zb, za))
    w.update(gather_finish(1, (t3,)))
    ya = _mm(za, w["a_out"], "nn", BF16, "mm_ya", b3=True, tm=2048, tk=1024)
    yb = _mm(zb, w["b_out"], "nn", BF16, "mm_yb", b3=True, tm=2048, tk=1024)
    mix = _mix_fwd(proj, ya, yb, "mix_fwd")
    m2 = _mm(mix, w["mix"], "nn", F32, "mm_mix")
    t4 = gather_start(4, (m2,))
    x1, h2 = _post_pre(x, m2, small["g_post_mix"], small["g_pre_ffn"], "norm_mix_ffn")
    w.update(gather_finish(2, (h2, t4)))
    fg = _mm(h2, w["gate_t"], "nt", BF16, "mm_gate")
    fu = _mm(h2, w["up_t"], "nt", BF16, "mm_up")
    s = _swiglu_fwd(fg, fu, "swiglu_fwd")
    w.update(gather_finish(3, (s,)))
    f = _mm(s, w["down"], "nn", F32, "mm_down")
    x2, h3 = _post_pre(x1, f, small["g_post_ffn"], small["g_pre_ple"], "norm_ffn_ple")
    w.update(gather_finish(4, (h3,)))
    pg = _mm(h3, w["pg"], "nn", F32, "mm_pg")
    p_bf = p.astype(BF16)
    pp = _mm(p_bf, w["pp"], "nn", F32, "mm_pp", b3=True, tm=2048)
    loss_rows, d3, dpg, dpp, dg_post_ple = _ple_final(x2, pg, pp, tgt, small["g_post_ple"], "ple_final")

    gw = {}
    gw["pp"] = _mm(p_bf, dpp, "tn", BF16, "mm_dw_pp", out3=True)
    gw["pg"] = _mm(h3, dpg, "tn", BF16, "mm_dw_pg")
    dh3 = _mm(dpg, w["pg"], "nt", F32, "mm_dh3", after=(scatter_start(["w_ple_proj", "w_ple_gate"], gw),))
    d2, df, dg_pre_ple, dg_post_ffn = _norm_bwd(d3, dh3, x2, small["g_pre_ple"], f, small["g_post_ffn"], "norm_bwd_ple_ffn")
    gw["down"] = _mm(s, df, "tn", BF16, "mm_dw_down", tm=1408)
    ds = _mm(df, w["down"], "nt", BF16, "mm_ds", after=(scatter_start(["w_ff_down"], gw),))
    dfg, dfu = _swiglu_bwd(ds, fg, fu, "swiglu_bwd")
    gw["gate_t"] = _mm(dfg, h2, "tn", BF16, "mm_dw_gate", tm=1408)
    gw["up_t"] = _mm(dfu, h2, "tn", BF16, "mm_dw_up", after=(gw["gate_t"],), tm=1408)
    dh2 = _mm(dfg, w["gate_t"], "nn", F32, "mm_dh2_gate", after=(scatter_start(["w_ff_gate", "w_ff_up"], gw),))
    dh2 = _mm(dfu, w["up_t"], "nn", F32, "mm_dh2_up", add=dh2, tm=1024)
    d1, dm2, dg_pre_ffn, dg_post_mix = _norm_bwd(d2, dh2, x1, small["g_pre_ffn"], m2, small["g_post_mix"], "norm_bwd_ffn_mix")
    dmix = _mm(dm2, w["mix"], "nt", F32, "mm_dmix")
    gw["mix"] = _mm(mix, dm2, "tn", BF16, "mm_dw_mix")
    dgab, dya, dyb = _mix_bwd(dmix, proj, ya, yb, "mix_bwd")
    dza = _mm(dya, w["a_out"], "nt", BF16, "mm_dza", after=(scatter_start(["w_mix_out"], gw),), b3=True, tm=2048)
    gw["a_out"] = _mm(za, dya, "tn", BF16, "mm_dw_a_out", out3=True, tk=1024)
    gw["b_out"] = _mm(zb, dyb, "tn", BF16, "mm_dw_b_out", after=(gw["a_out"],), out3=True, tk=1024)
    dzb = _mm(dyb, w["b_out"], "nt", BF16, "mm_dzb", after=(scatter_start(["w_a_out", "w_b_out"], gw),), b3=True, tm=2048)
    dax, dab, dac, dconv = _conv_bwd(dza, proj, conv_w, "conv_bwd")
    dq, dk, dv, dog, dalr, dwa, dba, dgain = _gla_bwd(dzb, proj, states, wa_p, b_alpha, gain, "gla_bwd")
    dproj = jnp.concatenate([dgab, dax, dab, dac, dq, dk, dv, dog, dalr.astype(BF16)], axis=1)
    gw["in_t"] = _mm(dproj, h1, "tn", BF16, "mm_dw_in", tm=1152)
    dh1 = _mm(dproj, w["in_t"], "nn", F32, "mm_dh1", after=(scatter_start(["w_in"], gw),))
    grad_x, dg_pre_mix = _norm_bwd(d1, dh1, x, small["g_pre_mix"], None, None, "norm_bwd_mix")

    gs = dict(
        conv_w=dconv,
        w_alpha_up=jnp.transpose(dwa[:, :GATE_RANK, :], (1, 0, 2)).reshape(GATE_RANK, HEADS * DK),
        b_alpha_up=dba.reshape(1, HEADS * DK), gla_head_gain=dgain,
        g_pre_mix=dg_pre_mix, g_post_mix=dg_post_mix, g_pre_ffn=dg_pre_ffn, g_post_ffn=dg_post_ffn,
        g_pre_ple=dg_pre_ple, g_post_ple=dg_post_ple,
    )
    return loss_rows, grad_x, gs


def _place():
    x, y, c = lax.axis_index("x"), lax.axis_index("y"), lax.axis_index("c")
    return x, y, c, [(1 - x, y), (x, 1 - y), (1 - x, 1 - y)]


def _all_gather(shards, name, cid=None):
    n = len(shards)

    def body(*refs):
        ins, outs = refs[:n], refs[n:2 * n]
        send_sems, recv_sems, local_sems = refs[2 * n:]
        x, y, c, chips = _place()
        me, sibling = (x, y, c), (x, y, 1 - c)

        def slot(px, py, pc):
            return 4 * px + 2 * py + pc

        def copy(a, k, block, to, src=None):
            dst = outs[a].at[slot(*block)]
            return pltpu.make_async_remote_copy(src_ref=dst if src is None else src, dst_ref=dst, send_sem=send_sems.at[a, k],
                                                recv_sem=recv_sems.at[a, k], device_id=to, device_id_type=MESH)

        mine = [pltpu.make_async_copy(ins[a], outs[a].at[slot(*me)], local_sems.at[a]) for a in range(n)]
        for cp in mine:
            cp.start()
        first = []
        for j, chip in enumerate(chips):
            first += [copy(a, 1 + j, me, (*chip, c), src=ins[a]) for a in range(n)]
        first += [copy(a, 0, me, sibling, src=ins[a]) for a in range(n)]
        for cp in first:
            cp.start()
        passed = []
        for j, chip in enumerate(chips):
            for a in range(n):
                copy(a, 1 + j, (*chip, c), me).wait_recv()
                cp = copy(a, 4 + j, (*chip, c), sibling)
                cp.start()
                passed.append(cp)
        for a in range(n):
            copy(a, 0, sibling, me).wait_recv()
        for j, chip in enumerate(chips):
            for a in range(n):
                copy(a, 4 + j, (*chip, 1 - c), me).wait_recv()
        for cp in first + passed:
            cp.wait_send()
        for cp in mine:
            cp.wait()

    if cid is None:
        return pl.pallas_call(
            body, name=name, in_specs=[ANY] * n, out_specs=[ANY] * n,
            out_shape=[jax.ShapeDtypeStruct((N_DEV,) + s.shape, s.dtype) for s in shards],
            scratch_shapes=[pltpu.SemaphoreType.DMA((n, 7)), pltpu.SemaphoreType.DMA((n, 7)), pltpu.SemaphoreType.DMA((n,))],
        )(*shards)

    src = [jax.new_ref(s, memory_space=pltpu.MemorySpace.HBM) for s in shards]
    dst = [jax.empty_ref(jax.ShapeDtypeStruct((N_DEV,) + s.shape, s.dtype), memory_space=pltpu.MemorySpace.HBM) for s in shards]

    @pl.kernel(mesh=plsc.ScalarSubcoreMesh(axis_name="seq", num_cores=1), name=name,
               scratch_types=(pltpu.SemaphoreType.DMA((n, 7)), pltpu.SemaphoreType.DMA((n, 7)), pltpu.SemaphoreType.DMA((n,))),
               compiler_params=pltpu.CompilerParams(collective_id=cid))
    def launch(send_sems, recv_sems, local_sems):
        x, y, c, chips = _place()
        barrier = pltpu.get_barrier_semaphore()
        for peer in [(x, y, 1 - c)] + [(*chip, c) for chip in chips]:
            pl.semaphore_signal(barrier, inc=1, device_id=peer, device_id_type=MESH)
        pl.semaphore_wait(barrier, 4)
        body(*src, *dst, send_sems, recv_sems, local_sems)

    launch()
    return [r[...] for r in dst]


def _reduce_scatter(parts, name, cid):
    n = len(parts)
    src = [jax.new_ref(s, memory_space=pltpu.MemorySpace.HBM) for s in parts]
    dst = [jax.empty_ref(jax.ShapeDtypeStruct(s.shape, s.dtype), memory_space=pltpu.MemorySpace.HBM) for s in parts]

    @pl.kernel(mesh=plsc.ScalarSubcoreMesh(axis_name="seq", num_cores=1), name=name,
               scratch_types=(pltpu.SemaphoreType.DMA((n, N_DEV - 1)), pltpu.SemaphoreType.DMA((n, N_DEV - 1)), pltpu.SemaphoreType.DMA((n,))),
               compiler_params=pltpu.CompilerParams(collective_id=cid))
    def launch(send_sems, recv_sems, local_sems):
        x, y, c, _ = _place()
        me = 4 * x + 2 * y + c
        peers = [(1 - x if k & 4 else x, 1 - y if k & 2 else y, 1 - c if k & 1 else c) for k in range(1, N_DEV)]
        barrier = pltpu.get_barrier_semaphore()
        for peer in peers:
            pl.semaphore_signal(barrier, inc=1, device_id=peer, device_id_type=MESH)
        pl.semaphore_wait(barrier, N_DEV - 1)
        mine = [pltpu.make_async_copy(src[a].at[me], dst[a].at[me], local_sems.at[a]) for a in range(n)]
        for cp in mine:
            cp.start()
        cps = []
        for a in range(n):
            for k, (px, py, pc) in enumerate(peers):
                cps.append(pltpu.make_async_remote_copy(src_ref=src[a].at[4 * px + 2 * py + pc], dst_ref=dst[a].at[me], send_sem=send_sems.at[a, k],
                                                        recv_sem=recv_sems.at[a, k], device_id=(px, py, pc), device_id_type=MESH))
        for cp in cps:
            cp.start()
        for cp in cps:
            cp.wait_recv()
        for cp in cps:
            cp.wait_send()
        for cp in mine:
            cp.wait()

    launch()
    return [r[...] for r in dst]


def _sibling_exchange(parts, name):
    n = len(parts)
    pieces = [_row_pieces(s.shape[1]) for s in parts]

    def body(*refs):
        ins, outs = refs[:n], refs[n:2 * n]
        send_sems, recv_sems = refs[2 * n:]
        x, y, c, _ = _place()

        def copy(a, ch, q, rows):
            return pltpu.make_async_remote_copy(src_ref=ins[a].at[2 * ch + 1 - c, rows], dst_ref=outs[a].at[ch, rows], send_sem=send_sems.at[a, ch, q],
                                                recv_sem=recv_sems.at[a, ch, q], device_id=(x, y, 1 - c), device_id_type=MESH)

        cps = [copy(a, ch, q, rows) for ch in range(4) for a in range(n) for q, rows in enumerate(pieces[a])]
        for cp in cps:
            cp.start()
        for cp in cps:
            cp.wait_recv()
        for cp in cps:
            cp.wait_send()

    return pl.pallas_call(
        body, name=name, in_specs=[ANY] * n, out_specs=[ANY] * n,
        out_shape=[jax.ShapeDtypeStruct((4,) + s.shape[1:], s.dtype) for s in parts],
        scratch_shapes=[pltpu.SemaphoreType.DMA((n, 4, PIECES)), pltpu.SemaphoreType.DMA((n, 4, PIECES))],
    )(*parts)


def _chip_exchange(parts, name):
    n = len(parts)

    def body(*refs):
        ins, outs = refs[:n], refs[n:2 * n]
        send_sems, recv_sems, local_sems = refs[2 * n:]
        x, y, c, chips = _place()
        my_chip = 2 * x + y

        def copy(a, j):
            px, py = chips[j]
            return pltpu.make_async_remote_copy(src_ref=ins[a].at[2 * px + py], dst_ref=outs[a].at[my_chip], send_sem=send_sems.at[a, j],
                                                recv_sem=recv_sems.at[a, j], device_id=(px, py, c), device_id_type=MESH)

        def landing(a, j):
            px, py = chips[j]
            return pltpu.make_async_remote_copy(src_ref=ins[a].at[my_chip], dst_ref=outs[a].at[2 * px + py], send_sem=send_sems.at[a, j],
                                                recv_sem=recv_sems.at[a, j], device_id=(px, py, c), device_id_type=MESH)

        mine = [pltpu.make_async_copy(ins[a].at[my_chip], outs[a].at[my_chip], local_sems.at[a]) for a in range(n)]
        for cp in mine:
            cp.start()
        cps = [copy(a, j) for j in range(3) for a in range(n)]
        for cp in cps:
            cp.start()
        for j in range(3):
            for a in range(n):
                landing(a, j).wait_recv()
        for cp in cps:
            cp.wait_send()
        for cp in mine:
            cp.wait()

    return pl.pallas_call(
        body, name=name, in_specs=[ANY] * n, out_specs=[ANY] * n,
        out_shape=[jax.ShapeDtypeStruct(s.shape, s.dtype) for s in parts],
        scratch_shapes=[pltpu.SemaphoreType.DMA((n, 3)), pltpu.SemaphoreType.DMA((n, 3)), pltpu.SemaphoreType.DMA((n,))],
    )(*parts)


def _pair_add(mine8, got4, name):
    _, r, cols = mine8.shape
    steps, blk, at = _tiles(r, cols)
    core = lax.axis_index("c").astype(jnp.int32).reshape(1)

    def body(c_ref, a_ref, b_ref, o_ref):
        o_ref[...] = (a_ref[...].astype(F32) + b_ref[...].astype(F32)).astype(BF16)

    return pl.pallas_call(
        body, name=name,
        grid_spec=pltpu.PrefetchScalarGridSpec(
            num_scalar_prefetch=1, grid=(4, steps),
            in_specs=[pl.BlockSpec((None,) + blk, lambda ch, i, c_ref: (2 * ch + c_ref[0],) + at(i)),
                      pl.BlockSpec((None,) + blk, lambda ch, i, c_ref: (ch,) + at(i))],
            out_specs=pl.BlockSpec((None,) + blk, lambda ch, i, c_ref: (ch,) + at(i))),
        out_shape=jax.ShapeDtypeStruct((4, r, cols), BF16),
        compiler_params=_params(("parallel", "parallel")),
    )(core, mine8, got4)


HBM = pl.BlockSpec(memory_space=pltpu.HBM)
SEM = pl.BlockSpec(memory_space=pltpu.SEMAPHORE)
EFFECT = pltpu.SideEffectType.DATAFLOW_SIDE_EFFECTING


def _in_hbm(a):
    return pltpu.with_memory_space_constraint(a, pltpu.HBM)


def _remote_copies(plan, srcs, lands, send_sems, recv_sems):
    return [pltpu.make_async_remote_copy(src_ref=s, dst_ref=d, send_sem=send_sems.at[i], recv_sem=recv_sems.at[i], device_id=peer,
                                         device_id_type=MESH) for i, (s, d, peer) in enumerate(plan(srcs, lands))]


def _copies_start(plan, n_copies, srcs, land_shapes, name, after=()):
    ns, nl = len(srcs), len(land_shapes)

    def body(*refs):
        send_sems, recv_sems = refs[ns + nl + len(after):ns + nl + len(after) + 2]
        for cp in _remote_copies(plan, refs[:ns], refs[ns:ns + nl], send_sems, recv_sems):
            cp.start()
        refs[-1][...] = jnp.zeros((8, 128), F32)

    sems = pltpu.SemaphoreType.DMA((n_copies,))
    return pl.pallas_call(
        body, name=name,
        out_shape=(sems, sems, *[pltpu.HBM(s.shape, s.dtype) for s in srcs], *[pltpu.HBM(s.shape, s.dtype) for s in land_shapes],
                   jax.ShapeDtypeStruct((8, 128), F32)),
        in_specs=[HBM] * (ns + nl) + [ANY] * len(after),
        out_specs=(SEM, SEM, *[HBM] * (ns + nl), pl.BlockSpec(memory_space=pltpu.VMEM)),
        input_output_aliases={i: 2 + i for i in range(ns + nl)},
        compiler_params=pltpu.CompilerParams(has_side_effects=EFFECT),
    )(*[_in_hbm(s) for s in srcs], *[_in_hbm(lax.empty(s.shape, s.dtype)) for s in land_shapes], *after)


def _copies_wait(plan, state, ns, name, after=()):
    send_sems, recv_sems, *arrs = state[:-1]
    n = len(arrs)

    def body(*refs):
        cps = _remote_copies(plan, refs[:ns], refs[ns:n], refs[n], refs[n + 1])
        for cp in cps:
            cp.wait_send()
        for cp in cps:
            cp.wait_recv()

    out = pl.pallas_call(
        body, name=name, out_shape=tuple(pltpu.HBM(a.shape, a.dtype) for a in arrs),
        in_specs=[HBM] * n + [SEM, SEM] + [ANY] * len(after), out_specs=tuple([HBM] * n),
        input_output_aliases={i: i for i in range(n)},
        compiler_params=pltpu.CompilerParams(has_side_effects=EFFECT),
    )(*arrs, send_sems, recv_sems, *after)
    return list(out[:ns]), list(out[ns:])


def _gather_plan(srcs, lands):
    x, y, c, chips = _place()
    peers = [(x, y, 1 - c)] + [(*chip, c) for chip in chips]
    return [(s, l.at[4 * x + 2 * y + c], peer) for s, l in zip(srcs, lands) for peer in peers]


def _gather_plan_near(srcs, lands):
    x, y, c, _ = _place()
    peers = [(x, y, 1 - c), (1 - x, y, c), (x, 1 - y, c)]
    return [(s, l.at[4 * x + 2 * y + c], peer) for s, l in zip(srcs, lands) for peer in peers]


def _scatter_plan(srcs, lands):
    x, y, c, _ = _place()
    peers = [(1 - x if k & 4 else x, 1 - y if k & 2 else y, 1 - c if k & 1 else c) for k in range(1, N_DEV)]
    return [(s.at[4 * px + 2 * py + pc], l.at[4 * x + 2 * y + c], (px, py, pc)) for s, l in zip(srcs, lands) for px, py, pc in peers]


def _everyone_plan(srcs, lands):
    x, y, c, _ = _place()
    peers = [(1 - x if k & 4 else x, 1 - y if k & 2 else y, 1 - c if k & 1 else c) for k in range(1, N_DEV)]
    return [(s, l.at[4 * x + 2 * y + c], peer) for s, l in zip(srcs, lands) for peer in peers]


def _sum_parts(got, own, me, name):
    def body(me_ref, got_ref, own_ref, o_ref):
        acc = jnp.where(me_ref[0] == 0, own_ref[...], got_ref[0])
        for d in range(1, N_DEV):
            acc = acc + jnp.where(me_ref[0] == d, own_ref[...], got_ref[d])
        o_ref[...] = acc

    return pl.pallas_call(
        body, name=name,
        grid_spec=pltpu.PrefetchScalarGridSpec(
            num_scalar_prefetch=1, grid=(1,),
            in_specs=[pl.BlockSpec(got.shape, lambda i, me_ref: (0, 0, 0)), pl.BlockSpec(own.shape, lambda i, me_ref: (0, 0))],
            out_specs=pl.BlockSpec(own.shape, lambda i, me_ref: (0, 0))),
        out_shape=jax.ShapeDtypeStruct(own.shape, F32),
    )(me.astype(jnp.int32).reshape(1), got, own)


def _chip_plan(srcs, lands):
    x, y, c, chips = _place()
    return [(s.at[2 * px + py], l.at[2 * x + y], (px, py, c)) for s, l in zip(srcs, lands) for px, py in chips]


PIECES = 8


def _row_pieces(rows):
    for k in (PIECES, 4, 2):
        if rows % (16 * k) == 0:
            return [pl.ds(q * (rows // k), rows // k) for q in range(k)]
    return [pl.ds(0, rows)]


def _put_own(shard, zone, me, name):
    r, c = shard.shape
    tr = r if r <= 256 else _pick(r, (256, 64))

    def body(me_ref, s_ref, z_ref, o_ref):
        o_ref[...] = s_ref[...]

    return pl.pallas_call(
        body, name=name,
        grid_spec=pltpu.PrefetchScalarGridSpec(
            num_scalar_prefetch=1, grid=(r // tr,),
            in_specs=[pl.BlockSpec((tr, c), lambda i, me_ref: (i, 0)), ANY],
            out_specs=pl.BlockSpec((None, tr, c), lambda i, me_ref: (me_ref[0], i, 0))),
        out_shape=jax.ShapeDtypeStruct(zone.shape, zone.dtype), input_output_aliases={2: 0},
        compiler_params=_params(("arbitrary",)),
    )(me.astype(jnp.int32).reshape(1), shard, zone)


def _gather_finish(lands, name):
    n = len(lands)

    def body(*refs):
        zones, outs = refs[:n], refs[n:2 * n]
        send_sems, recv_sems = refs[2 * n:]
        x, y, c, chips = _place()
        cps = [pltpu.make_async_remote_copy(
            src_ref=zones[a].at[4 * px + 2 * py + c], dst_ref=outs[a].at[4 * px + 2 * py + c], send_sem=send_sems.at[a, j],
            recv_sem=recv_sems.at[a, j], device_id=(x, y, 1 - c), device_id_type=MESH) for j, (px, py) in enumerate(chips) for a in range(n)]
        for cp in cps:
            cp.start()
        for cp in cps:
            cp.wait_recv()
        for cp in cps:
            cp.wait_send()

    return pl.pallas_call(
        body, name=name, in_specs=[ANY] * n, out_specs=[ANY] * n,
        out_shape=[jax.ShapeDtypeStruct(l.shape, l.dtype) for l in lands],
        input_output_aliases={a: a for a in range(n)},
        scratch_shapes=[pltpu.SemaphoreType.DMA((n, 3)), pltpu.SemaphoreType.DMA((n, 3))],
    )(*lands)


def _gather_relay(lands, name):
    n = len(lands)

    def body(*refs):
        zones, outs = refs[:n], refs[n:2 * n]
        send_sems, recv_sems = refs[2 * n:]
        x, y, c, _ = _place()
        south = c == 0
        near_x, near_y, across = 4 * (1 - x) + 2 * y + c, 4 * x + 2 * (1 - y) + c, 4 * (1 - x) + 2 * (1 - y) + c
        passed = jnp.where(south, near_y, near_x)
        onward = (jnp.where(south, 1 - x, x), jnp.where(south, y, 1 - y), c)

        def copy(a, k, slot, to):
            return pltpu.make_async_remote_copy(src_ref=zones[a].at[slot], dst_ref=outs[a].at[slot], send_sem=send_sems.at[a, k],
                                                recv_sem=recv_sems.at[a, k], device_id=to, device_id_type=MESH)

        first = [copy(a, 0, passed, onward) for a in range(n)]
        first += [copy(a, 1 + j, slot, (x, y, 1 - c)) for j, slot in enumerate((near_x, near_y)) for a in range(n)]
        for cp in first:
            cp.start()
        last = []
        for a in range(n):
            copy(a, 0, across, onward).wait_recv()
            last.append(copy(a, 3, across, (x, y, 1 - c)))
            last[-1].start()
        for cp in first[n:] + last:
            cp.wait_recv()
        for cp in first + last:
            cp.wait_send()

    return pl.pallas_call(
        body, name=name, in_specs=[ANY] * n, out_specs=[ANY] * n,
        out_shape=[jax.ShapeDtypeStruct(l.shape, l.dtype) for l in lands],
        input_output_aliases={a: a for a in range(n)},
        scratch_shapes=[pltpu.SemaphoreType.DMA((n, 4)), pltpu.SemaphoreType.DMA((n, 4))],
    )(*lands)


def _sum_everywhere(v, name):
    rows = v.shape[0]

    def body(v_ref, o_ref, buf, send_sems, recv_sems):
        x, y, c, _ = _place()
        me = 4 * x + 2 * y + c
        buf[me] = v_ref[...]
        cps = []
        for k in range(1, N_DEV):
            fx, fy, fc = (k >> 2) & 1, (k >> 1) & 1, k & 1
            to = (1 - x if fx else x, 1 - y if fy else y, 1 - c if fc else c)
            cps.append(pltpu.make_async_remote_copy(src_ref=buf.at[me], dst_ref=buf.at[me], send_sem=send_sems.at[k - 1],
                                                    recv_sem=recv_sems.at[k - 1], device_id=to, device_id_type=MESH))
        for cp in cps:
            cp.start()
        for cp in cps:
            cp.wait_recv()
        for cp in cps:
            cp.wait_send()
        acc = buf[0]
        for d in range(1, N_DEV):
            acc = acc + buf[d]
        o_ref[...] = acc

    vm = pl.BlockSpec(memory_space=pltpu.VMEM)
    return pl.pallas_call(
        body, name=name, in_specs=[vm], out_specs=vm, out_shape=jax.ShapeDtypeStruct(v.shape, F32),
        scratch_shapes=[pltpu.VMEM((N_DEV, rows, 128), F32), pltpu.SemaphoreType.DMA((N_DEV - 1,)), pltpu.SemaphoreType.DMA((N_DEV - 1,))],
    )(v)


def _adamw_parts(w, got, mine, me, m, v, name, after=()):
    r, c = w.shape
    n_parts = got.shape[0]
    steps, blk, at = _tiles(r, c)

    def body(me_ref, w_ref, got_ref, own_ref, m_ref, v_ref, *rest):
        go_ref, d_ref, mo_ref, vo_ref = rest[len(after):]
        own = own_ref[...].astype(F32)
        gv = jnp.where(me_ref[0] == 0, own, got_ref[0].astype(F32))
        for d in range(1, n_parts):
            gv = gv + jnp.where(me_ref[0] == d, own, got_ref[d].astype(F32))
        _adamw_math(gv, w_ref, m_ref, v_ref, go_ref, d_ref, mo_ref, vo_ref)

    tile = pl.BlockSpec(blk, lambda i, me_ref: at(i))
    out = jax.ShapeDtypeStruct((r, c), F32)
    return pl.pallas_call(
        body, name=name,
        grid_spec=pltpu.PrefetchScalarGridSpec(
            num_scalar_prefetch=1, grid=(steps,),
            in_specs=[tile, pl.BlockSpec((n_parts,) + blk, lambda i, me_ref: (0,) + at(i)),
                      pl.BlockSpec((None,) + blk, lambda i, me_ref: (me_ref[0],) + at(i)), tile, tile] + [ANY] * len(after),
            out_specs=[tile] * 4),
        out_shape=[out] * 4, compiler_params=_params(("parallel",)),
    )(me.astype(jnp.int32).reshape(1), w, got, mine, m, v, *after)


def _adamw_math(gv, w_ref, m_ref, v_ref, go_ref, d_ref, mo_ref, vo_ref):
    mn = B1 * m_ref[...] + (1.0 - B1) * gv
    vn = B2 * v_ref[...] + (1.0 - B2) * (gv * gv)
    m_hat = mn / (1.0 - B1 ** STEP)
    v_hat = vn / (1.0 - B2 ** STEP)
    go_ref[...] = gv
    d_ref[...] = -LR * (m_hat / (jnp.sqrt(v_hat) + ADAM_EPS) + WD * w_ref[...])
    mo_ref[...] = mn
    vo_ref[...] = vn


def _adamw(w, g, m, v, name):
    r, c = w.shape
    parts = g.ndim == 3
    tr = r if r <= 128 else _pick(r, (128, 64))

    def body(w_ref, g_ref, m_ref, v_ref, go_ref, d_ref, mo_ref, vo_ref):
        if parts:
            gv = g_ref[0].astype(F32)
            for d in range(1, g.shape[0]):
                gv = gv + g_ref[d].astype(F32)
        else:
            gv = g_ref[...]
        mn = B1 * m_ref[...] + (1.0 - B1) * gv
        vn = B2 * v_ref[...] + (1.0 - B2) * (gv * gv)
        m_hat = mn / (1.0 - B1 ** STEP)
        v_hat = vn / (1.0 - B2 ** STEP)
        go_ref[...] = gv
        d_ref[...] = -LR * (m_hat / (jnp.sqrt(v_hat) + ADAM_EPS) + WD * w_ref[...])
        mo_ref[...] = mn
        vo_ref[...] = vn

    tile = pl.BlockSpec((tr, c), lambda i: (i, 0))
    g_spec = pl.BlockSpec((g.shape[0], tr, c), lambda i: (0, i, 0)) if parts else tile
    out = jax.ShapeDtypeStruct((r, c), F32)
    return pl.pallas_call(
        body, name=name, grid=(r // tr,), in_specs=[tile, g_spec, tile, tile], out_specs=[tile] * 4, out_shape=[out] * 4,
        compiler_params=_params(("parallel",)),
    )(w, g, m, v)


BIG = ["w_in", "w_a_out", "w_b_out", "w_mix_out", "w_ff_gate", "w_ff_up", "w_ff_down", "w_ple_gate", "w_ple_proj"]
TRANSPOSED = ["w_in", "w_ff_gate", "w_ff_up"]
GRAD_OF = dict(w_ple_proj="pp", w_ple_gate="pg", w_ff_down="down", w_ff_gate="gate_t", w_ff_up="up_t", w_mix_out="mix", w_a_out="a_out",
               w_b_out="b_out")
SMALL = ["conv_w", "w_alpha_up", "b_alpha_up", "gla_head_gain", "g_pre_mix", "g_post_mix", "g_pre_ffn", "g_post_ffn", "g_pre_ple", "g_post_ple"]
WEIGHTS = ["w_in", "conv_w", "w_a_out", "w_alpha_up", "b_alpha_up", "gla_head_gain", "w_b_out", "w_mix_out", "g_pre_mix", "g_post_mix",
           "g_pre_ffn", "g_post_ffn", "w_ff_gate", "w_ff_up", "w_ff_down", "g_pre_ple", "g_post_ple", "w_ple_gate", "w_ple_proj"]


def _in_t_from_blocks(z):
    w = z.reshape(-1, z.shape[-1])
    return jnp.concatenate([w[R_GA:R_END], w[:R_ALR], w[R_ALR:R_GA], jnp.zeros((128 - GATE_RANK, w.shape[1]), w.dtype)], axis=0)


def _blocks_from_in_t(g):
    per = R_END // N_DEV

    def rows(lo, hi):
        out = []
        for n0, n1, p0 in ((0, R_ALR, C_AX), (R_ALR, R_GA, C_ALR), (R_GA, R_END, 0)):
            a, e = max(lo, n0), min(hi, n1)
            if a < e:
                out.append(g[p0 + a - n0:p0 + e - n0])
        return out

    return jnp.stack([jnp.concatenate(rows(b * per, (b + 1) * per), axis=0) for b in range(N_DEV)])


def _cols_to_full(g8):
    n, r, c = g8.shape
    return jnp.transpose(g8, (1, 0, 2)).reshape(r, n * c)


def _full_to_cols(a):
    r, c = a.shape
    return jnp.transpose(a.reshape(r, N_DEV, c // N_DEV), (1, 0, 2))


def _pack(arrs, rows):
    flat = jnp.concatenate([a.reshape(-1) for a in arrs])
    return jnp.pad(flat, (0, rows * 128 - flat.shape[0])).reshape(rows, 128)


def _unpack(packed, shapes):
    flat, out, o = packed.reshape(-1), [], 0
    for s in shapes:
        size = 1
        for d in s:
            size *= d
        out.append(flat[o:o + size].reshape(s))
        o += size
    return out


def kernel(x, p, w_in, conv_w, w_a_out, w_alpha_up, b_alpha_up, gla_head_gain, w_b_out, w_mix_out, g_pre_mix, g_post_mix, g_pre_ffn, g_post_ffn, w_ff_gate, w_ff_up, w_ff_down, g_pre_ple, g_post_ple, w_ple_gate, w_ple_proj, loss_target, m_w_in, m_conv_w, m_w_a_out, m_w_alpha_up, m_b_alpha_up, m_gla_head_gain, m_w_b_out, m_w_mix_out, m_g_pre_mix, m_g_post_mix, m_g_pre_ffn, m_g_post_ffn, m_w_ff_gate, m_w_ff_up, m_w_ff_down, m_g_pre_ple, m_g_post_ple, m_w_ple_gate, m_w_ple_proj, v_w_in, v_conv_w, v_w_a_out, v_w_alpha_up, v_b_alpha_up, v_gla_head_gain, v_w_b_out, v_w_mix_out, v_g_pre_mix, v_g_post_mix, v_g_pre_ffn, v_g_post_ffn, v_w_ff_gate, v_w_ff_up, v_w_ff_down, v_g_pre_ple, v_g_post_ple, v_w_ple_gate, v_w_ple_proj):
    args = dict(locals())
    shard = lambda n, a: jnp.transpose(a[0]) if n in TRANSPOSED else a[0]
    wts = {n: shard(n, args[n]) for n in WEIGHTS}
    mom = {n: shard(n, args["m_" + n]) for n in WEIGHTS}
    var = {n: shard(n, args["v_" + n]) for n in WEIGHTS}
    me =4 * lax.axis_index("x") + 2 * lax.axis_index("y") + lax.axis_index("c")

    groups = [["w_in", "conv_w", "w_alpha_up"], ["w_a_out", "w_b_out", "w_mix_out"], ["w_ff_gate", "w_ff_up"], ["w_ff_down"],
              ["w_ple_gate", "w_ple_proj"]]
    grad_groups = []
    rows_full = lambda g: g.reshape(-1, g.shape[-1])
    gathers, scatters = {}, {}

    def gather_start(gi, after):
        if gi not in gathers:
            shards = [wts[n].astype(BF16) if n in BIG else wts[n] for n in groups[gi]]
            zones = [jax.ShapeDtypeStruct((N_DEV,) + s.shape, s.dtype) for s in shards]
            plan, peers = (_gather_plan_near, 3) if gi == 0 else (_gather_plan, 4)
            gathers[gi] = (shards, _copies_start(plan, peers * len(shards), shards, zones, "gather_start_%d" % gi, after))
        return gathers[gi][1][-1]

    def gather_finish(gi, after):
        shards, state = gathers[gi]
        shards, zones = _copies_wait(_gather_plan_near if gi == 0 else _gather_plan, state, len(shards), "gather_wait_%d" % gi, after)
        if gi == 0:
            zones = _gather_relay(zones, "gather_relay_%d" % gi)
            gather_start(1, (zones[0],))
        else:
            zones = _gather_finish(zones, "gather_finish_%d" % gi)
        g8 = {n: _put_own(s, z, me, "gather_own_" + n) for n, s, z in zip(groups[gi], shards, zones)}
        if gi == 0:
            return dict(in_t=_in_t_from_blocks(g8["w_in"]),
                        conv_w=_cols_to_full(g8["conv_w"]), w_alpha_up=_cols_to_full(g8["w_alpha_up"]))
        if gi == 1:
            return dict(a_out=g8["w_a_out"], b_out=g8["w_b_out"], mix=rows_full(g8["w_mix_out"]))
        if gi == 2:
            return dict(gate_t=rows_full(g8["w_ff_gate"]), up_t=rows_full(g8["w_ff_up"]))
        if gi == 3:
            return dict(down=rows_full(g8["w_ff_down"]))
        return dict(pg=rows_full(g8["w_ple_gate"]), pp=g8["w_ple_proj"])

    def scatter_start(names, gw):
        gi = len(grad_groups)
        grad_groups.append(names)
        full = {n: _blocks_from_in_t(gw["in_t"]) if n == "w_in" else gw[GRAD_OF[n]] for n in names}
        parts = [full[n] if full[n].ndim == 3 else full[n].reshape(N_DEV, -1, full[n].shape[-1]) for n in names]
        if names == ["w_in"]:
            from_sibling = _sibling_exchange(parts, "scatter_sibling_%d" % gi)
            parts = [_pair_add(a, b, "scatter_add_%d_%s" % (gi, n)) for n, a, b in zip(names, parts, from_sibling)]
            scatters[gi] = _copies_start(_chip_plan, 3 * len(parts), parts, parts, "scatter_start_%d" % gi)
        else:
            scatters[gi] = _copies_start(_scatter_plan, (N_DEV - 1) * len(parts), parts, parts, "scatter_start_%d" % gi)
        return scatters[gi][-1]

    small = {n: wts[n].reshape(1, -1) for n in SMALL[2:]}

    loss_rows, grad_x, gs = _local_step(x[0], p[0, 0], loss_target[0], gather_start, gather_finish, scatter_start, small)
    gs["loss"] = jnp.sum(loss_rows).reshape(1, 1)

    small_shapes = [gs[n].shape for n in SMALL]
    gs_packed = _pack([gs[n] for n in SMALL + ["loss"]], 192)
    small_state = _copies_start(_everyone_plan, N_DEV - 1, [gs_packed], [jax.ShapeDtypeStruct((N_DEV,) + gs_packed.shape, F32)],
                                "small_start", (grad_x,))

    res, done = {}, (small_state[-1],)
    for gi, names in enumerate(grad_groups):
        plan, slot = (_chip_plan, me // 2) if names == ["w_in"] else (_scatter_plan, me)
        mine, got = _copies_wait(plan, scatters[gi], len(names), "scatter_wait_%d" % gi, done)
        for n, g, own in zip(names, got, mine):
            res[n] = _adamw_parts(wts[n], g, own, slot, mom[n], var[n], "adamw_" + n)
        done = tuple(res[n][1] for n in names)

    (gs_own,), (gs_got,) = _copies_wait(_everyone_plan, small_state, 1, "small_wait", done)
    gsum = dict(zip(SMALL + ["loss"], _unpack(_sum_parts(gs_got, gs_own, me, "small_sum"), small_shapes + [(1, 1)])))
    loss = gsum["loss"].reshape(())
    gsum["conv_w"] = lax.dynamic_index_in_dim(gsum["conv_w"].reshape(3, N_DEV, -1), me, axis=1, keepdims=False)
    gsum["w_alpha_up"] = lax.dynamic_index_in_dim(gsum["w_alpha_up"].reshape(GATE_RANK, N_DEV, -1), me, axis=1, keepdims=False)

    shard_shapes = [wts[n].shape for n in SMALL]
    packed = [_pack([d[n] for n in SMALL], 120) for d in (wts, gsum, mom, var)]
    outs = [_unpack(o, shard_shapes) for o in _adamw(*packed, "adamw_small")]
    for i, n in enumerate(SMALL):
        res[n] = [o[i] for o in outs]

    back = lambda n, a: (jnp.transpose(a) if n in TRANSPOSED else a)[None]
    return (loss, grad_x[None], *[back(n, res[n][i]) for i in range(4) for n in WEIGHTS])
```

```python
import functools

import jax
import jax.numpy as jnp
from jax import lax
from jax.experimental import pallas as pl
from jax.experimental.pallas import tpu as pltpu
from jax.experimental.pallas import tpu_sc as plsc

F32, BF16 = jnp.float32, jnp.bfloat16
EPS = 1e-6
CHUNK = 64
STEP_CHUNKS = 2
STEP_ROWS = STEP_CHUNKS * CHUNK
HEADS, DK, DV = 4, 128, 256
GATE_RANK = 16
TAU = 16.0
LR, B1, B2, ADAM_EPS, WD, STEP = 0.001, 0.9, 0.999, 1e-08, 0.01, 10
N_DEV = 8
MESH = pl.DeviceIdType.MESH
VMEM_LIMIT = 56 * 1024 * 1024
ANY = pl.BlockSpec(memory_space=pl.ANY)

C_GA, C_GB, C_AX, C_AB, C_AC, C_Q, C_K, C_V, C_OG, C_ALR = 0, 2048, 4096, 5120, 6144, 7168, 7680, 8192, 9216, 10240
IN_PAD = 10368
R_AX, R_AB, R_AC, R_Q, R_K, R_V, R_OG, R_ALR, R_GA, R_GB, R_END = 0, 1024, 2048, 3072, 3584, 4096, 5120, 6144, 6160, 8208, 10256


def _params(sem):
    return pltpu.CompilerParams(dimension_semantics=sem, vmem_limit_bytes=VMEM_LIMIT)


def _pick(n, cands):
    for c in cands:
        if n % c == 0:
            return c
    return n


def _tiles(r, c):
    for tr in (128, 64):
        if r % tr == 0:
            return r // tr, (tr, c), lambda i: (i, 0)
    tc = _pick(c, (256, 128))
    return c // tc, (r, tc), lambda i: (0, i)


def _mm(a, b, mode, out_dtype, name, after=(), add=None, b3=False, out3=False, tm=None, tk=None):
    bshape = (b.shape[1], N_DEV * b.shape[2]) if b3 else b.shape
    if mode == "nn":
        (m, k), (k2, n) = a.shape, bshape
    elif mode == "nt":
        (m, k), (n, k2) = a.shape, bshape
    else:
        (k, m), (k2, n) = a.shape, bshape
    assert k == k2 and a.dtype == BF16 and b.dtype == BF16, (name, a.shape, b.shape, a.dtype, b.dtype)
    tm = tm if tm and m % tm == 0 else _pick(m, (2048, 1024, 512, 256))
    tn = _pick(n, (1152, 1024, 1408, 512, 256))
    tk = tk if tk and k % tk == 0 else _pick(k, (2048, 1408, 1152, 1024, 512, 256))
    if out3 or (b3 and mode == "nn"):
        tn = n // N_DEV
    if b3 and mode == "nt":
        tk = k // N_DEV
    nk = k // tk
    dims = {"nn": (((1,), (0,)), ((), ())), "nt": (((1,), (1,)), ((), ())), "tn": (((0,), (0,)), ((), ()))}[mode]
    n_extra = len(after) + (add is not None)

    def body(a_ref, b_ref, *rest):
        o_ref = rest[n_extra]
        prod = lax.dot_general(a_ref[...], b_ref[...], dims, preferred_element_type=F32)
        if nk == 1:
            o_ref[...] = (prod if add is None else prod + rest[0][...]).astype(o_ref.dtype)
            return
        acc_ref = rest[n_extra + 1]
        kk = pl.program_id(2)

        @pl.when(kk == 0)
        def _():
            acc_ref[...] = prod if add is None else prod + rest[0][...]

        @pl.when((kk > 0) & (kk < nk - 1))
        def _():
            acc_ref[...] += prod

        @pl.when(kk == nk - 1)
        def _():
            o_ref[...] = (acc_ref[...] + prod).astype(o_ref.dtype)

    a_spec = pl.BlockSpec((tk, tm), lambda i, j, kk: (kk, i)) if mode == "tn" else pl.BlockSpec((tm, tk), lambda i, j, kk: (i, kk))
    if b3:
        b_spec = (pl.BlockSpec((None, tn, tk), lambda i, j, kk: (kk, j, 0)) if mode == "nt"
                  else pl.BlockSpec((None, tk, tn), lambda i, j, kk: (j, kk, 0)))
    else:
        b_spec = pl.BlockSpec((tn, tk), lambda i, j, kk: (j, kk)) if mode == "nt" else pl.BlockSpec((tk, tn), lambda i, j, kk: (kk, j))
    tile = pl.BlockSpec((tm, tn), lambda i, j, kk: (i, j))
    out_spec = pl.BlockSpec((None, tm, tn), lambda i, j, kk: (j, i, 0)) if out3 else tile
    return pl.pallas_call(
        body, name=name, grid=(m // tm, n // tn, nk),
        in_specs=[a_spec, b_spec] + ([tile] if add is not None else []) + [ANY] * len(after), out_specs=out_spec,
        out_shape=jax.ShapeDtypeStruct((N_DEV, m, tn) if out3 else (m, n), out_dtype),
        scratch_shapes=[pltpu.VMEM((tm, tn), F32)] if nk > 1 else [],
        compiler_params=_params(("parallel", "parallel", "arbitrary")),
    )(a, b, *([add] if add is not None else []), *after)


def _rows(body, t, tr, ins, outs, name):
    in_specs = []
    for arr, sp in ins:
        if sp[0] == "t":
            in_specs.append(pl.BlockSpec((tr, sp[1]), lambda i, cb=sp[2]: (i, cb)))
        else:
            in_specs.append(pl.BlockSpec(arr.shape, lambda i, nd=arr.ndim: (0,) * nd))
    out_specs, out_shape = [], []
    for shape, dt, kind in outs:
        out_specs.append(pl.BlockSpec((tr, shape[1]), lambda i: (i, 0)) if kind == "t" else pl.BlockSpec(shape, lambda i: (0, 0)))
        out_shape.append(jax.ShapeDtypeStruct(shape, dt))
    return pl.pallas_call(
        body, name=name, grid=(t // tr,), in_specs=in_specs, out_specs=out_specs, out_shape=out_shape,
        compiler_params=_params(("arbitrary",)),
    )(*[arr for arr, _ in ins])


def _rinv(v):
    return lax.rsqrt(jnp.mean(v * v, axis=-1, keepdims=True) + EPS)


def _sig(v):
    return 1.0 / (1.0 + jnp.exp(-v))


def _acc(ref, val):
    @pl.when(pl.program_id(0) == 0)
    def _():
        ref[...] = jnp.zeros_like(ref)

    ref[...] += jnp.sum(val, axis=0, keepdims=True)


def _rms_fwd(x, g, name):
    t, d = x.shape

    def body(x_ref, g_ref, h_ref):
        xv = x_ref[...]
        h_ref[...] = (xv * _rinv(xv) * g_ref[...]).astype(BF16)

    return _rows(body, t, 256, [(x, ("t", d, 0)), (g, ("b",))], [((t, d), BF16, "t")], name)[0]


def _post_pre(x, m, g_post, g_pre, name):
    t, d = x.shape

    def body(x_ref, m_ref, gp_ref, gn_ref, xo_ref, h_ref):
        mv = m_ref[...]
        xn = x_ref[...] + mv * _rinv(mv) * gp_ref[...]
        xo_ref[...] = xn
        h_ref[...] = (xn * _rinv(xn) * gn_ref[...]).astype(BF16)

    return _rows(body, t, 128, [(x, ("t", d, 0)), (m, ("t", d, 0)), (g_post, ("b",)), (g_pre, ("b",))],
                 [((t, d), F32, "t"), ((t, d), BF16, "t")], name)


def _mix_fwd(proj, ya, yb, name):
    t, d = ya.shape

    def body(ga_ref, gb_ref, ya_ref, yb_ref, o_ref):
        o_ref[...] = (_sig(ga_ref[...].astype(F32)) * ya_ref[...].astype(F32)
                      + _sig(gb_ref[...].astype(F32)) * yb_ref[...].astype(F32)).astype(BF16)

    return _rows(body, t, 256, [(proj, ("t", d, C_GA // d)), (proj, ("t", d, C_GB // d)), (ya, ("t", d, 0)), (yb, ("t", d, 0))],
                 [((t, d), BF16, "t")], name)[0]


def _mix_bwd(dmix, proj, ya, yb, name):
    t, d = ya.shape

    def body(dm_ref, ga_ref, gb_ref, ya_ref, yb_ref, dg_ref, dya_ref, dyb_ref):
        dm = dm_ref[...]
        sa, sb = _sig(ga_ref[...].astype(F32)), _sig(gb_ref[...].astype(F32))
        dg_ref[:, :d] = (dm * ya_ref[...].astype(F32) * sa * (1.0 - sa)).astype(BF16)
        dg_ref[:, d:] = (dm * yb_ref[...].astype(F32) * sb * (1.0 - sb)).astype(BF16)
        dya_ref[...] = (dm * sa).astype(BF16)
        dyb_ref[...] = (dm * sb).astype(BF16)

    return _rows(body, t, 128,
                 [(dmix, ("t", d, 0)), (proj, ("t", d, C_GA // d)), (proj, ("t", d, C_GB // d)), (ya, ("t", d, 0)), (yb, ("t", d, 0))],
                 [((t, 2 * d), BF16, "t"), ((t, d), BF16, "t"), ((t, d), BF16, "t")], name)


def _swiglu_call(body, ins, n_out, name):
    t, f = ins[0].shape
    tc = _pick(f, (1408, 512))
    tile = pl.BlockSpec((512, tc), lambda i, j: (i, j))
    return pl.pallas_call(
        body, name=name, grid=(t // 512, f // tc), in_specs=[tile] * len(ins), out_specs=[tile] * n_out,
        out_shape=[jax.ShapeDtypeStruct((t, f), BF16)] * n_out, compiler_params=_params(("parallel", "parallel")),
    )(*ins)


def _swiglu_fwd(fg, fu, name):
    def body(g_ref, u_ref, s_ref):
        gv = g_ref[...].astype(F32)
        s_ref[...] = (gv * _sig(gv) * u_ref[...].astype(F32)).astype(BF16)

    return _swiglu_call(body, [fg, fu], 1, name)[0]


def _swiglu_bwd(ds, fg, fu, name):
    def body(ds_ref, g_ref, u_ref, dg_ref, du_ref):
        dsv, gv, uv = ds_ref[...].astype(F32), g_ref[...].astype(F32), u_ref[...].astype(F32)
        sg = _sig(gv)
        dg_ref[...] = (dsv * uv * sg * (1.0 + gv * (1.0 - sg))).astype(BF16)
        du_ref[...] = (dsv * gv * sg).astype(BF16)

    return _swiglu_call(body, [ds, fg, fu], 2, name)


def _ple_final(x2, pg, pp, tgt, g_post, name):
    t, d = x2.shape

    def body(x_ref, pg_ref, pp_ref, t_ref, g_ref, loss_ref, d3_ref, dpg_ref, dpp_ref, dg_ref):
        sg, ppv, g = _sig(pg_ref[...]), pp_ref[...], g_ref[...]
        e = sg * ppv
        r = _rinv(e)
        eh = e * r
        diff = x_ref[...] + eh * g - t_ref[...]
        loss_ref[...] = 0.5 * jnp.mean(diff * diff, axis=-1, keepdims=True)
        d3 = diff * (1.0 / d)
        d3_ref[...] = d3
        gd = d3 * g
        de = r * (gd - eh * jnp.mean(gd * eh, axis=-1, keepdims=True))
        dpg_ref[...] = (de * ppv * sg * (1.0 - sg)).astype(BF16)
        dpp_ref[...] = (de * sg).astype(BF16)
        _acc(dg_ref, d3 * eh)

    return _rows(body, t, 128, [(x2, ("t", d, 0)), (pg, ("t", d, 0)), (pp, ("t", d, 0)), (tgt, ("t", d, 0)), (g_post, ("b",))],
                 [((t, 1), F32, "t"), ((t, d), F32, "t"), ((t, d), BF16, "t"), ((t, d), BF16, "t"), ((1, d), F32, "a")], name)


def _norm_bwd(dn, dh, x, g_pre, fm, g_post, name):
    t, d = x.shape
    two = fm is not None

    def body(*refs):
        if two:
            dn_ref, dh_ref, x_ref, gp_ref, f_ref, gq_ref, dx_ref, df_ref, dgp_ref, dgq_ref = refs
        else:
            dn_ref, dh_ref, x_ref, gp_ref, dx_ref, dgp_ref = refs
        xv, dhv = x_ref[...], dh_ref[...]
        r = _rinv(xv)
        xh = xv * r
        gd = dhv * gp_ref[...]
        dx = dn_ref[...] + r * (gd - xh * jnp.mean(gd * xh, axis=-1, keepdims=True))
        dx_ref[...] = dx
        _acc(dgp_ref, dhv * xh)
        if two:
            fv = f_ref[...]
            rf = _rinv(fv)
            fh = fv * rf
            gd2 = dx * gq_ref[...]
            df_ref[...] = (rf * (gd2 - fh * jnp.mean(gd2 * fh, axis=-1, keepdims=True))).astype(BF16)
            _acc(dgq_ref, dx * fh)

    ins = [(dn, ("t", d, 0)), (dh, ("t", d, 0)), (x, ("t", d, 0)), (g_pre, ("b",))]
    outs = [((t, d), F32, "t")]
    if two:
        ins += [(fm, ("t", d, 0)), (g_post, ("b",))]
        outs += [((t, d), BF16, "t"), ((1, d), F32, "a"), ((1, d), F32, "a")]
    else:
        outs += [((1, d), F32, "a")]
    return _rows(body, t, 128, ins, outs, name)


CONV_TC = 256


def _shift_down(v, s):
    rows = lax.broadcasted_iota(jnp.int32, v.shape, 0)
    return jnp.where(rows >= s, pltpu.roll(v, s, 0), 0.0)


def _shift_up(v, s):
    n = v.shape[0]
    rows = lax.broadcasted_iota(jnp.int32, v.shape, 0)
    return jnp.where(rows < n - s, pltpu.roll(v, n - s, 0), 0.0)


def _conv_specs(t):
    nb = 1024 // CONV_TC
    seg = lambda c0: pl.BlockSpec((t, CONV_TC), lambda j, cb=c0 // CONV_TC: (0, cb + j))
    own = pl.BlockSpec((t, CONV_TC), lambda j: (0, j))
    wspec = pl.BlockSpec((3, CONV_TC), lambda j: (0, j))
    return nb, seg, own, wspec


def _conv_fwd(proj, conv_w, name, after=()):
    t = proj.shape[0]
    nb, seg, own, wspec = _conv_specs(t)

    def body(ax_ref, ab_ref, ac_ref, w_ref, *rest):
        za_ref = rest[len(after)]
        u = ac_ref[...].astype(F32) * ax_ref[...].astype(F32)
        w = w_ref[...]
        yc = w[0:1] * _shift_down(u, 2) + w[1:2] * _shift_down(u, 1) + w[2:3] * u
        za_ref[...] = (ab_ref[...].astype(F32) * yc).astype(BF16)

    return pl.pallas_call(
        body, name=name, grid=(nb,), in_specs=[seg(C_AX), seg(C_AB), seg(C_AC), wspec] + [ANY] * len(after), out_specs=own,
        out_shape=jax.ShapeDtypeStruct((t, 1024), BF16), compiler_params=_params(("parallel",)),
    )(proj, proj, proj, conv_w, *after)


def _conv_bwd(dza, proj, conv_w, name):
    t = proj.shape[0]
    nb, seg, own, wspec = _conv_specs(t)

    def body(dz_ref, ax_ref, ab_ref, ac_ref, w_ref, dax_ref, dab_ref, dac_ref, dw_ref):
        ax, ab, ac, dz = ax_ref[...].astype(F32), ab_ref[...].astype(F32), ac_ref[...].astype(F32), dz_ref[...].astype(F32)
        w = w_ref[...]
        u = ac * ax
        u1, u2 = _shift_down(u, 1), _shift_down(u, 2)
        yc = w[0:1] * u2 + w[1:2] * u1 + w[2:3] * u
        dab_ref[...] = (dz * yc).astype(BF16)
        dyc = dz * ab
        du = w[2:3] * dyc + w[1:2] * _shift_up(dyc, 1) + w[0:1] * _shift_up(dyc, 2)
        dax_ref[...] = (du * ac).astype(BF16)
        dac_ref[...] = (du * ax).astype(BF16)
        dw_ref[0:1, :] = jnp.sum(dyc * u2, axis=0, keepdims=True)
        dw_ref[1:2, :] = jnp.sum(dyc * u1, axis=0, keepdims=True)
        dw_ref[2:3, :] = jnp.sum(dyc * u, axis=0, keepdims=True)

    act = jax.ShapeDtypeStruct((t, 1024), BF16)
    return pl.pallas_call(
        body, name=name, grid=(nb,), in_specs=[own, seg(C_AX), seg(C_AB), seg(C_AC), wspec], out_specs=[own, own, own, wspec],
        out_shape=[act, act, act, jax.ShapeDtypeStruct((3, 1024), F32)], compiler_params=_params(("parallel",)),
    )(dza, proj, proj, proj, conv_w)


def _dot(a, b, dims, precision=None):
    return lax.dot_general(a, b, (dims, ((), ())), precision=precision, preferred_element_type=F32)


_CONTRACT = {"nn": ((1,), (0,)), "nt": ((1,), (1,)), "tn": ((0,), (0,))}


def _bdot_raw(a, b, mode):
    return _dot(a.astype(BF16), b.astype(BF16), _CONTRACT[mode])


@functools.partial(jax.custom_vjp, nondiff_argnums=(2,))
def _bdot(a, b, mode):
    return _bdot_raw(a, b, mode)


def _bdot_fwd(a, b, mode):
    return _bdot_raw(a, b, mode), (a, b)


def _bdot_bwd(mode, res, ct):
    a, b = res
    if mode == "nn":
        return _bdot_raw(ct, b, "nt"), _bdot_raw(a, ct, "tn")
    if mode == "nt":
        return _bdot_raw(ct, b, "nn"), _bdot_raw(ct, a, "tn")
    return _bdot_raw(b, ct, "nt"), _bdot_raw(a, ct, "nn")


_bdot.defvjp(_bdot_fwd, _bdot_bwd)


@functools.partial(jax.custom_vjp, nondiff_argnums=(2,))
def _sum_dot(ones, x, mode):
    head = x.astype(BF16)
    tail = x - head.astype(F32)
    if mode == "nn":
        return _bdot_raw(ones, head, "nn") + _bdot_raw(ones, tail, "nn")
    return _bdot_raw(head, ones, "tn") + _bdot_raw(tail, ones, "tn")


def _sum_dot_fwd(ones, x, mode):
    return _sum_dot(ones, x, mode), ones


def _sum_dot_bwd(mode, ones, ct):
    return jnp.zeros_like(ones), (_bdot_raw(ones, ct, "tn") if mode == "nn" else _bdot_raw(ones, ct, "nt"))


_sum_dot.defvjp(_sum_dot_fwd, _sum_dot_bwd)


def _gla_chunk(q, k, v, og, alr, s_in, wa, ba, gain):
    c = q.shape[0]
    z =_bdot(alr, wa, "nn") + ba
    la = (jnp.minimum(z, 0.0) - jnp.log(1.0 + jnp.exp(-jnp.abs(z)))) * (1.0 / TAU)
    row = lax.broadcasted_iota(jnp.int32, (c, c), 0)
    col = lax.broadcasted_iota(jnp.int32, (c, c), 1)
    lower = row >= col
    b = _sum_dot(lower.astype(F32), la, "nn")
    trow = lax.broadcasted_iota(jnp.int32, la.shape, 0)
    mid = jnp.sum(jnp.where(trow <= c // 2, la, 0.0), axis=0, keepdims=True)
    blast = jnp.sum(la, axis=0, keepdims=True)
    qs = q * (DK ** -0.5)
    e_up, e_dn = jnp.exp(b - mid), jnp.exp(mid - b)
    a_fwd = _bdot(qs * e_up, k * e_dn, "nt")
    a_rev = _bdot(qs * e_dn, k * e_up, "nt")
    att = jnp.where(lower, a_fwd, a_rev)
    o = _bdot(att, v, "nn") + _bdot(qs * jnp.exp(b), s_in, "nn")
    upd = _bdot(k * jnp.exp(blast - b), v, "tn")
    blast_col = _sum_dot(jnp.ones((c, DV), F32), la, "tn")
    s_out = jnp.exp(blast_col) * s_in + upd
    on = o * _rinv(o) * gain
    return on * og * _sig(og), s_out


def _gla_specs(t, rev):
    n = t // STEP_ROWS
    ch = (lambda i: n - 1 - i) if rev else (lambda i: i)
    col = lambda w, c0: pl.BlockSpec((STEP_ROWS, HEADS * w), lambda i, cb=c0 // (HEADS * w): (ch(i), cb))
    whole = lambda shape: pl.BlockSpec(shape, lambda i, nd=len(shape): (0,) * nd)
    specs = dict(
        q=col(DK, C_Q), k=col(DK, C_K), v=col(DV, C_V), og=col(DV, C_OG),
        alr=pl.BlockSpec((STEP_ROWS, 128), lambda i: (ch(i), C_ALR // 128)),
        wa=whole((128, HEADS * DK)), ba=whole((1, HEADS * DK)), gain=whole((1, DV)),
        state=pl.BlockSpec((STEP_CHUNKS, HEADS, DK, DV), lambda i: (ch(i), 0, 0, 0)),
        odk=pl.BlockSpec((STEP_ROWS, HEADS * DK), lambda i: (ch(i), 0)), odv=pl.BlockSpec((STEP_ROWS, HEADS * DV), lambda i: (ch(i), 0)),
        oalr=pl.BlockSpec((STEP_ROWS, 128), lambda i: (ch(i), 0)), whole=whole,
    )
    return n, specs


def _head_cols(h):
    return slice(h * DK, (h + 1) * DK), slice(h * DV, (h + 1) * DV)


def _gla_fwd(proj, wa, ba, gain, name):
    t = proj.shape[0]
    n, sp = _gla_specs(t, False)

    def body(q_ref, k_ref, v_ref, og_ref, alr_ref, wa_ref, ba_ref, g_ref, zb_ref, st_ref, s_scr):
        @pl.when(pl.program_id(0) == 0)
        def _():
            s_scr[...] = jnp.zeros_like(s_scr)

        state = [s_scr[h] for h in range(HEADS)]
        for c in range(STEP_CHUNKS):
            rows = slice(c * CHUNK, (c + 1) * CHUNK)
            alr = alr_ref[rows, :].astype(F32)
            for h in range(HEADS):
                kc, vc = _head_cols(h)
                st_ref[c, h] = state[h]
                zb, state[h] = _gla_chunk(q_ref[rows, kc].astype(F32), k_ref[rows, kc].astype(F32), v_ref[rows, vc].astype(F32),
                                          og_ref[rows, vc].astype(F32), alr, state[h], wa_ref[:, kc].astype(F32), ba_ref[:, kc], g_ref[...])
                zb_ref[rows, vc] = zb.astype(BF16)
        for h in range(HEADS):
            s_scr[h] = state[h]

    return pl.pallas_call(
        body, name=name, grid=(n,),
        in_specs=[sp["q"], sp["k"], sp["v"], sp["og"], sp["alr"], sp["wa"], sp["ba"], sp["gain"]],
        out_specs=[sp["odv"], sp["state"]],
        out_shape=[jax.ShapeDtypeStruct((t, HEADS * DV), BF16), jax.ShapeDtypeStruct((t // CHUNK, HEADS, DK, DV), F32)],
        scratch_shapes=[pltpu.VMEM((HEADS, DK, DV), F32)],
        compiler_params=_params(("arbitrary",)),
    )(proj, proj, proj, proj, proj, wa, ba, gain)


def _gla_bwd(dzb, proj, states, wa, ba, gain, name):
    t = proj.shape[0]
    n, sp = _gla_specs(t, True)

    def body(dz_ref, q_ref, k_ref, v_ref, og_ref, alr_ref, st_ref, wa_ref, ba_ref, g_ref,
             dq_ref, dk_ref, dv_ref, dog_ref, dalr_ref, dwa_ref, dba_ref, dg_ref, ds_scr):
        @pl.when(pl.program_id(0) == 0)
        def _():
            ds_scr[...] = jnp.zeros_like(ds_scr)
            dwa_ref[...] = jnp.zeros_like(dwa_ref)
            dba_ref[...] = jnp.zeros_like(dba_ref)
            dg_ref[...] = jnp.zeros_like(dg_ref)

        dstate = [ds_scr[h] for h in range(HEADS)]
        dwa_sum, dba_sum, dgain_sum = [None] * HEADS, [None] * HEADS, None
        for c in reversed(range(STEP_CHUNKS)):
            rows = slice(c * CHUNK, (c + 1) * CHUNK)
            alr = alr_ref[rows, :].astype(F32)
            dalr_sum = None
            for h in range(HEADS):
                kc, vc = _head_cols(h)
                args = (q_ref[rows, kc].astype(F32), k_ref[rows, kc].astype(F32), v_ref[rows, vc].astype(F32), og_ref[rows, vc].astype(F32),
                        alr, st_ref[c, h], wa_ref[:, kc].astype(F32), ba_ref[:, kc], g_ref[...])
                _, vjp = jax.vjp(_gla_chunk, *args)
                dq, dk, dv, dog, dalr, dstate[h], dwa, dba, dgain = vjp((dz_ref[rows, vc].astype(F32), dstate[h]))
                dq_ref[rows, kc] = dq.astype(BF16)
                dk_ref[rows, kc] = dk.astype(BF16)
                dv_ref[rows, vc] = dv.astype(BF16)
                dog_ref[rows, vc] = dog.astype(BF16)
                dwa_sum[h] = dwa if dwa_sum[h] is None else dwa_sum[h] + dwa
                dba_sum[h] = dba if dba_sum[h] is None else dba_sum[h] + dba
                dalr_sum = dalr if dalr_sum is None else dalr_sum + dalr
                dgain_sum = dgain if dgain_sum is None else dgain_sum + dgain
            dalr_ref[rows, :] = dalr_sum
        for h in range(HEADS):
            ds_scr[h] = dstate[h]
            dwa_ref[h] += dwa_sum[h]
            dba_ref[h] += dba_sum[h]
        dg_ref[...] += dgain_sum

    whole = sp["whole"]
    return pl.pallas_call(
        body, name=name, grid=(n,),
        in_specs=[sp["odv"], sp["q"], sp["k"], sp["v"], sp["og"], sp["alr"], sp["state"], sp["wa"], sp["ba"], sp["gain"]],
        out_specs=[sp["odk"], sp["odk"], sp["odv"], sp["odv"], sp["oalr"], whole((HEADS, 128, DK)), whole((HEADS, 1, DK)), whole((1, DV))],
        out_shape=[jax.ShapeDtypeStruct((t, HEADS * DK), BF16), jax.ShapeDtypeStruct((t, HEADS * DK), BF16),
                   jax.ShapeDtypeStruct((t, HEADS * DV), BF16), jax.ShapeDtypeStruct((t, HEADS * DV), BF16),
                   jax.ShapeDtypeStruct((t, 128), F32), jax.ShapeDtypeStruct((HEADS, 128, DK), F32),
                   jax.ShapeDtypeStruct((HEADS, 1, DK), F32), jax.ShapeDtypeStruct((1, DV), F32)],
        scratch_shapes=[pltpu.VMEM((HEADS, DK, DV), F32)],
        compiler_params=_params(("arbitrary",)),
    )(dzb, proj, proj, proj, proj, proj, states, wa, ba, gain)


def _local_step(x, p, tgt, gather_start, gather_finish, scatter_start, small):
    b_alpha, gain = small["b_alpha_up"], small["gla_head_gain"]
    gather_start(0, ())
    w = dict(gather_finish(0, ()))
    conv_w, w_alpha = w["conv_w"], w["w_alpha_up"]
    wa_p = jnp.zeros((128, HEADS * DK), BF16).at[:GATE_RANK].set(w_alpha.astype(BF16))

    t2 = gather_start(2, (w["in_t"], gather_start(1, ())))
    h1 = _rms_fwd(x, small["g_pre_mix"], "rms_pre_mix")
    proj = _mm(h1, w["in_t"], "nt", BF16, "mm_proj", after=(t2,))
    za = _conv_fwd(proj, conv_w, "conv_fwd")
    zb, states = _gla_fwd(proj, wa_p, b_alpha, gain, "gla_fwd")
    t3 = gather_start(3, (zb, za))
    w.update(gather_finish(1, (t3,)))
    ya = _mm(za, w["a_out"], "nn", BF16, "mm_ya")
    yb = _mm(zb, w["b_out"], "nn", BF16, "mm_yb")
    mix = _mix_fwd(proj, ya, yb, "mix_fwd")
    m2 = _mm(mix, w["mix"], "nn", F32, "mm_mix")
    t4 = gather_start(4, (m2,))
    x1, h2 = _post_pre(x, m2, small["g_post_mix"], small["g_pre_ffn"], "norm_mix_ffn")
    w.update(gather_finish(2, (h2, t4)))
    fg = _mm(h2, w["gate_t"], "nt", BF16, "mm_gate")
    fu = _mm(h2, w["up_t"], "nt", BF16, "mm_up")
    s = _swiglu_fwd(fg, fu, "swiglu_fwd")
    w.update(gather_finish(3, (s,)))
    f = _mm(s, w["down"], "nn", F32, "mm_down")
    x2, h3 = _post_pre(x1, f, small["g_post_ffn"], small["g_pre_ple"], "norm_ffn_ple")
    w.update(gather_finish(4, (h3,)))
    pg = _mm(h3, w["pg"], "nn", F32, "mm_pg")
    p_bf = p.astype(BF16)
    pp = _mm(p_bf, w["pp"], "nn", F32, "mm_pp", b3=True, tm=2048)
    loss_rows, d3, dpg, dpp, dg_post_ple = _ple_final(x2, pg, pp, tgt, small["g_post_ple"], "ple_final")

    gw = {}
    gw["pp"] = _mm(p_bf, dpp, "tn", BF16, "mm_dw_pp", out3=True)
    gw["pg"] = _mm(h3, dpg, "tn", BF16, "mm_dw_pg")
    dh3 = _mm(dpg, w["pg"], "nt", F32, "mm_dh3", after=(scatter_start(["w_ple_proj", "w_ple_gate"], gw),))
    d2, df, dg_pre_ple, dg_post_ffn = _norm_bwd(d3, dh3, x2, small["g_pre_ple"], f, small["g_post_ffn"], "norm_bwd_ple_ffn")
    gw["down"] = _mm(s, df, "tn", BF16, "mm_dw_down", tm=1408)
    ds = _mm(df, w["down"], "nt", BF16, "mm_ds", after=(scatter_start(["w_ff_down"], gw),))
    dfg, dfu = _swiglu_bwd(ds, fg, fu, "swiglu_bwd")
    gw["gate_t"] = _mm(dfg, h2, "tn", BF16, "mm_dw_gate", tm=1408)
    gw["up_t"] = _mm(dfu, h2, "tn", BF16, "mm_dw_up", after=(gw["gate_t"],), tm=1408)
    dh2 = _mm(dfg, w["gate_t"], "nn", F32, "mm_dh2_gate", after=(scatter_start(["w_ff_gate", "w_ff_up"], gw),))
    dh2 = _mm(dfu, w["up_t"], "nn", F32, "mm_dh2_up", add=dh2, tm=1024)
    d1, dm2, dg_pre_ffn, dg_post_mix = _norm_bwd(d2, dh2, x1, small["g_pre_ffn"], m2, small["g_post_mix"], "norm_bwd_ffn_mix")
    dmix = _mm(dm2, w["mix"], "nt", F32, "mm_dmix")
    gw["mix"] = _mm(mix, dm2, "tn", BF16, "mm_dw_mix")
    dgab, dya, dyb = _mix_bwd(dmix, proj, ya, yb, "mix_bwd")
    dza = _mm(dya, w["a_out"], "nt", BF16, "mm_dza", after=(scatter_start(["w_mix_out"], gw),))
    gw["a_out"] = _mm(za, dya, "tn", BF16, "mm_dw_a_out")
    gw["b_out"] = _mm(zb, dyb, "tn", BF16, "mm_dw_b_out", after=(gw["a_out"],))
    dzb = _mm(dyb, w["b_out"], "nt", BF16, "mm_dzb", after=(scatter_start(["w_a_out", "w_b_out"], gw),))
    dax, dab, dac, dconv = _conv_bwd(dza, proj, conv_w, "conv_bwd")
    dq, dk, dv, dog, dalr, dwa, dba, dgain = _gla_bwd(dzb, proj, states, wa_p, b_alpha, gain, "gla_bwd")
    dproj = jnp.concatenate([dgab, dax, dab, dac, dq, dk, dv, dog, dalr.astype(BF16)], axis=1)
    gw["in_t"] = _mm(dproj, h1, "tn", BF16, "mm_dw_in", tm=1152)
    dh1 = _mm(dproj, w["in_t"], "nn", F32, "mm_dh1", after=(scatter_start(["w_in"], gw),))
    grad_x, dg_pre_mix = _norm_bwd(d1, dh1, x, small["g_pre_mix"], None, None, "norm_bwd_mix")

    gs = dict(
        conv_w=dconv,
        w_alpha_up=jnp.transpose(dwa[:, :GATE_RANK, :], (1, 0, 2)).reshape(GATE_RANK, HEADS * DK),
        b_alpha_up=dba.reshape(1, HEADS * DK), gla_head_gain=dgain,
        g_pre_mix=dg_pre_mix, g_post_mix=dg_post_mix, g_pre_ffn=dg_pre_ffn, g_post_ffn=dg_post_ffn,
        g_pre_ple=dg_pre_ple, g_post_ple=dg_post_ple,
    )
    return loss_rows, grad_x, gs


def _place():
    x, y, c = lax.axis_index("x"), lax.axis_index("y"), lax.axis_index("c")
    return x, y, c, [(1 - x, y), (x, 1 - y), (1 - x, 1 - y)]


def _all_gather(shards, name, cid=None):
    n = len(shards)

    def body(*refs):
        ins, outs = refs[:n], refs[n:2 * n]
        send_sems, recv_sems, local_sems = refs[2 * n:]
        x, y, c, chips = _place()
        me, sibling = (x, y, c), (x, y, 1 - c)

        def slot(px, py, pc):
            return 4 * px + 2 * py + pc

        def copy(a, k, block, to, src=None):
            dst = outs[a].at[slot(*block)]
            return pltpu.make_async_remote_copy(src_ref=dst if src is None else src, dst_ref=dst, send_sem=send_sems.at[a, k],
                                                recv_sem=recv_sems.at[a, k], device_id=to, device_id_type=MESH)

        mine = [pltpu.make_async_copy(ins[a], outs[a].at[slot(*me)], local_sems.at[a]) for a in range(n)]
        for cp in mine:
            cp.start()
        first = []
        for j, chip in enumerate(chips):
            first += [copy(a, 1 + j, me, (*chip, c), src=ins[a]) for a in range(n)]
        first += [copy(a, 0, me, sibling, src=ins[a]) for a in range(n)]
        for cp in first:
            cp.start()
        passed = []
        for j, chip in enumerate(chips):
            for a in range(n):
                copy(a, 1 + j, (*chip, c), me).wait_recv()
                cp = copy(a, 4 + j, (*chip, c), sibling)
                cp.start()
                passed.append(cp)
        for a in range(n):
            copy(a, 0, sibling, me).wait_recv()
        for j, chip in enumerate(chips):
            for a in range(n):
                copy(a, 4 + j, (*chip, 1 - c), me).wait_recv()
        for cp in first + passed:
            cp.wait_send()
        for cp in mine:
            cp.wait()

    if cid is None:
        return pl.pallas_call(
            body, name=name, in_specs=[ANY] * n, out_specs=[ANY] * n,
            out_shape=[jax.ShapeDtypeStruct((N_DEV,) + s.shape, s.dtype) for s in shards],
            scratch_shapes=[pltpu.SemaphoreType.DMA((n, 7)), pltpu.SemaphoreType.DMA((n, 7)), pltpu.SemaphoreType.DMA((n,))],
        )(*shards)

    src = [jax.new_ref(s, memory_space=pltpu.MemorySpace.HBM) for s in shards]
    dst = [jax.empty_ref(jax.ShapeDtypeStruct((N_DEV,) + s.shape, s.dtype), memory_space=pltpu.MemorySpace.HBM) for s in shards]

    @pl.kernel(mesh=plsc.ScalarSubcoreMesh(axis_name="seq", num_cores=1), name=name,
               scratch_types=(pltpu.SemaphoreType.DMA((n, 7)), pltpu.SemaphoreType.DMA((n, 7)), pltpu.SemaphoreType.DMA((n,))),
               compiler_params=pltpu.CompilerParams(collective_id=cid))
    def launch(send_sems, recv_sems, local_sems):
        x, y, c, chips = _place()
        barrier = pltpu.get_barrier_semaphore()
        for peer in [(x, y, 1 - c)] + [(*chip, c) for chip in chips]:
            pl.semaphore_signal(barrier, inc=1, device_id=peer, device_id_type=MESH)
        pl.semaphore_wait(barrier, 4)
        body(*src, *dst, send_sems, recv_sems, local_sems)

    launch()
    return [r[...] for r in dst]


def _reduce_scatter(parts, name, cid):
    n = len(parts)
    src = [jax.new_ref(s, memory_space=pltpu.MemorySpace.HBM) for s in parts]
    dst = [jax.empty_ref(jax.ShapeDtypeStruct(s.shape, s.dtype), memory_space=pltpu.MemorySpace.HBM) for s in parts]

    @pl.kernel(mesh=plsc.ScalarSubcoreMesh(axis_name="seq", num_cores=1), name=name,
               scratch_types=(pltpu.SemaphoreType.DMA((n, N_DEV - 1)), pltpu.SemaphoreType.DMA((n, N_DEV - 1)), pltpu.SemaphoreType.DMA((n,))),
               compiler_params=pltpu.CompilerParams(collective_id=cid))
    def launch(send_sems, recv_sems, local_sems):
        x, y, c, _ = _place()
        me = 4 * x + 2 * y + c
        peers = [(1 - x if k & 4 else x, 1 - y if k & 2 else y, 1 - c if k & 1 else c) for k in range(1, N_DEV)]
        barrier = pltpu.get_barrier_semaphore()
        for peer in peers:
            pl.semaphore_signal(barrier, inc=1, device_id=peer, device_id_type=MESH)
        pl.semaphore_wait(barrier, N_DEV - 1)
        mine = [pltpu.make_async_copy(src[a].at[me], dst[a].at[me], local_sems.at[a]) for a in range(n)]
        for cp in mine:
            cp.start()
        cps = []
        for a in range(n):
            for k, (px, py, pc) in enumerate(peers):
                cps.append(pltpu.make_async_remote_copy(src_ref=src[a].at[4 * px + 2 * py + pc], dst_ref=dst[a].at[me], send_sem=send_sems.at[a, k],
                                                        recv_sem=recv_sems.at[a, k], device_id=(px, py, pc), device_id_type=MESH))
        for cp in cps:
            cp.start()
        for cp in cps:
            cp.wait_recv()
        for cp in cps:
            cp.wait_send()
        for cp in mine:
            cp.wait()

    launch()
    return [r[...] for r in dst]


def _sibling_exchange(parts, name):
    n = len(parts)
    pieces = [_row_pieces(s.shape[1]) for s in parts]

    def body(*refs):
        ins, outs = refs[:n], refs[n:2 * n]
        send_sems, recv_sems = refs[2 * n:]
        x, y, c, _ = _place()

        def copy(a, ch, q, rows):
            return pltpu.make_async_remote_copy(src_ref=ins[a].at[2 * ch + 1 - c, rows], dst_ref=outs[a].at[ch, rows], send_sem=send_sems.at[a, ch, q],
                                                recv_sem=recv_sems.at[a, ch, q], device_id=(x, y, 1 - c), device_id_type=MESH)

        cps = [copy(a, ch, q, rows) for ch in range(4) for a in range(n) for q, rows in enumerate(pieces[a])]
        for cp in cps:
            cp.start()
        for cp in cps:
            cp.wait_recv()
        for cp in cps:
            cp.wait_send()

    return pl.pallas_call(
        body, name=name, in_specs=[ANY] * n, out_specs=[ANY] * n,
        out_shape=[jax.ShapeDtypeStruct((4,) + s.shape[1:], s.dtype) for s in parts],
        scratch_shapes=[pltpu.SemaphoreType.DMA((n, 4, PIECES)), pltpu.SemaphoreType.DMA((n, 4, PIECES))],
    )(*parts)


def _chip_exchange(parts, name):
    n = len(parts)

    def body(*refs):
        ins, outs = refs[:n], refs[n:2 * n]
        send_sems, recv_sems, local_sems = refs[2 * n:]
        x, y, c, chips = _place()
        my_chip = 2 * x + y

        def copy(a, j):
            px, py = chips[j]
            return pltpu.make_async_remote_copy(src_ref=ins[a].at[2 * px + py], dst_ref=outs[a].at[my_chip], send_sem=send_sems.at[a, j],
                                                recv_sem=recv_sems.at[a, j], device_id=(px, py, c), device_id_type=MESH)

        def landing(a, j):
            px, py = chips[j]
            return pltpu.make_async_remote_copy(src_ref=ins[a].at[my_chip], dst_ref=outs[a].at[2 * px + py], send_sem=send_sems.at[a, j],
                                                recv_sem=recv_sems.at[a, j], device_id=(px, py, c), device_id_type=MESH)

        mine = [pltpu.make_async_copy(ins[a].at[my_chip], outs[a].at[my_chip], local_sems.at[a]) for a in range(n)]
        for cp in mine:
            cp.start()
        cps = [copy(a, j) for j in range(3) for a in range(n)]
        for cp in cps:
            cp.start()
        for j in range(3):
            for a in range(n):
                landing(a, j).wait_recv()
        for cp in cps:
            cp.wait_send()
        for cp in mine:
            cp.wait()

    return pl.pallas_call(
        body, name=name, in_specs=[ANY] * n, out_specs=[ANY] * n,
        out_shape=[jax.ShapeDtypeStruct(s.shape, s.dtype) for s in parts],
        scratch_shapes=[pltpu.SemaphoreType.DMA((n, 3)), pltpu.SemaphoreType.DMA((n, 3)), pltpu.SemaphoreType.DMA((n,))],
    )(*parts)


def _pair_add(mine8, got4, name):
    _, r, cols = mine8.shape
    steps, blk, at = _tiles(r, cols)
    core = lax.axis_index("c").astype(jnp.int32).reshape(1)

    def body(c_ref, a_ref, b_ref, o_ref):
        o_ref[...] = (a_ref[...].astype(F32) + b_ref[...].astype(F32)).astype(BF16)

    return pl.pallas_call(
        body, name=name,
        grid_spec=pltpu.PrefetchScalarGridSpec(
            num_scalar_prefetch=1, grid=(4, steps),
            in_specs=[pl.BlockSpec((None,) + blk, lambda ch, i, c_ref: (2 * ch + c_ref[0],) + at(i)),
                      pl.BlockSpec((None,) + blk, lambda ch, i, c_ref: (ch,) + at(i))],
            out_specs=pl.BlockSpec((None,) + blk, lambda ch, i, c_ref: (ch,) + at(i))),
        out_shape=jax.ShapeDtypeStruct((4, r, cols), BF16),
        compiler_params=_params(("parallel", "parallel")),
    )(core, mine8, got4)


HBM = pl.BlockSpec(memory_space=pltpu.HBM)
SEM = pl.BlockSpec(memory_space=pltpu.SEMAPHORE)
EFFECT = pltpu.SideEffectType.DATAFLOW_SIDE_EFFECTING


def _in_hbm(a):
    return pltpu.with_memory_space_constraint(a, pltpu.HBM)


def _remote_copies(plan, srcs, lands, send_sems, recv_sems):
    return [pltpu.make_async_remote_copy(src_ref=s, dst_ref=d, send_sem=send_sems.at[i], recv_sem=recv_sems.at[i], device_id=peer,
                                         device_id_type=MESH) for i, (s, d, peer) in enumerate(plan(srcs, lands))]


def _copies_start(plan, n_copies, srcs, land_shapes, name, after=()):
    ns, nl = len(srcs), len(land_shapes)

    def body(*refs):
        send_sems, recv_sems = refs[ns + nl + len(after):ns + nl + len(after) + 2]
        for cp in _remote_copies(plan, refs[:ns], refs[ns:ns + nl], send_sems, recv_sems):
            cp.start()
        refs[-1][...] = jnp.zeros((8, 128), F32)

    sems = pltpu.SemaphoreType.DMA((n_copies,))
    return pl.pallas_call(
        body, name=name,
        out_shape=(sems, sems, *[pltpu.HBM(s.shape, s.dtype) for s in srcs], *[pltpu.HBM(s.shape, s.dtype) for s in land_shapes],
                   jax.ShapeDtypeStruct((8, 128), F32)),
        in_specs=[HBM] * (ns + nl) + [ANY] * len(after),
        out_specs=(SEM, SEM, *[HBM] * (ns + nl), pl.BlockSpec(memory_space=pltpu.VMEM)),
        input_output_aliases={i: 2 + i for i in range(ns + nl)},
        compiler_params=pltpu.CompilerParams(has_side_effects=EFFECT),
    )(*[_in_hbm(s) for s in srcs], *[_in_hbm(lax.empty(s.shape, s.dtype)) for s in land_shapes], *after)


def _copies_wait(plan, state, ns, name, after=()):
    send_sems, recv_sems, *arrs = state[:-1]
    n = len(arrs)

    def body(*refs):
        cps = _remote_copies(plan, refs[:ns], refs[ns:n], refs[n], refs[n + 1])
        for cp in cps:
            cp.wait_send()
        for cp in cps:
            cp.wait_recv()

    out = pl.pallas_call(
        body, name=name, out_shape=tuple(pltpu.HBM(a.shape, a.dtype) for a in arrs),
        in_specs=[HBM] * n + [SEM, SEM] + [ANY] * len(after), out_specs=tuple([HBM] * n),
        input_output_aliases={i: i for i in range(n)},
        compiler_params=pltpu.CompilerParams(has_side_effects=EFFECT),
    )(*arrs, send_sems, recv_sems, *after)
    return list(out[:ns]), list(out[ns:])


def _gather_plan(srcs, lands):
    x, y, c, chips = _place()
    peers = [(x, y, 1 - c)] + [(*chip, c) for chip in chips]
    return [(s, l.at[4 * x + 2 * y + c], peer) for s, l in zip(srcs, lands) for peer in peers]


def _gather_plan_near(srcs, lands):
    x, y, c, _ = _place()
    peers = [(x, y, 1 - c), (1 - x, y, c), (x, 1 - y, c)]
    return [(s, l.at[4 * x + 2 * y + c], peer) for s, l in zip(srcs, lands) for peer in peers]


def _scatter_plan(srcs, lands):
    x, y, c, _ = _place()
    peers = [(1 - x if k & 4 else x, 1 - y if k & 2 else y, 1 - c if k & 1 else c) for k in range(1, N_DEV)]
    return [(s.at[4 * px + 2 * py + pc], l.at[4 * x + 2 * y + c], (px, py, pc)) for s, l in zip(srcs, lands) for px, py, pc in peers]


def _everyone_plan(srcs, lands):
    x, y, c, _ = _place()
    peers = [(1 - x if k & 4 else x, 1 - y if k & 2 else y, 1 - c if k & 1 else c) for k in range(1, N_DEV)]
    return [(s, l.at[4 * x + 2 * y + c], peer) for s, l in zip(srcs, lands) for peer in peers]


def _sum_parts(got, own, me, name):
    def body(me_ref, got_ref, own_ref, o_ref):
        acc = jnp.where(me_ref[0] == 0, own_ref[...], got_ref[0])
        for d in range(1, N_DEV):
            acc = acc + jnp.where(me_ref[0] == d, own_ref[...], got_ref[d])
        o_ref[...] = acc

    return pl.pallas_call(
        body, name=name,
        grid_spec=pltpu.PrefetchScalarGridSpec(
            num_scalar_prefetch=1, grid=(1,),
            in_specs=[pl.BlockSpec(got.shape, lambda i, me_ref: (0, 0, 0)), pl.BlockSpec(own.shape, lambda i, me_ref: (0, 0))],
            out_specs=pl.BlockSpec(own.shape, lambda i, me_ref: (0, 0))),
        out_shape=jax.ShapeDtypeStruct(own.shape, F32),
    )(me.astype(jnp.int32).reshape(1), got, own)


def _chip_plan(srcs, lands):
    x, y, c, chips = _place()
    return [(s.at[2 * px + py], l.at[2 * x + y], (px, py, c)) for s, l in zip(srcs, lands) for px, py in chips]


PIECES = 8


def _row_pieces(rows):
    for k in (PIECES, 4, 2):
        if rows % (16 * k) == 0:
            return [pl.ds(q * (rows // k), rows // k) for q in range(k)]
    return [pl.ds(0, rows)]


def _put_own(shard, zone, me, name):
    r, c = shard.shape
    tr = r if r <= 256 else _pick(r, (256, 64))

    def body(me_ref, s_ref, z_ref, o_ref):
        o_ref[...] = s_ref[...]

    return pl.pallas_call(
        body, name=name,
        grid_spec=pltpu.PrefetchScalarGridSpec(
            num_scalar_prefetch=1, grid=(r // tr,),
            in_specs=[pl.BlockSpec((tr, c), lambda i, me_ref: (i, 0)), ANY],
            out_specs=pl.BlockSpec((None, tr, c), lambda i, me_ref: (me_ref[0], i, 0))),
        out_shape=jax.ShapeDtypeStruct(zone.shape, zone.dtype), input_output_aliases={2: 0},
        compiler_params=_params(("arbitrary",)),
    )(me.astype(jnp.int32).reshape(1), shard, zone)


def _gather_finish(lands, name):
    n = len(lands)

    def body(*refs):
        zones, outs = refs[:n], refs[n:2 * n]
        send_sems, recv_sems = refs[2 * n:]
        x, y, c, chips = _place()
        cps = [pltpu.make_async_remote_copy(
            src_ref=zones[a].at[4 * px + 2 * py + c], dst_ref=outs[a].at[4 * px + 2 * py + c], send_sem=send_sems.at[a, j],
            recv_sem=recv_sems.at[a, j], device_id=(x, y, 1 - c), device_id_type=MESH) for j, (px, py) in enumerate(chips) for a in range(n)]
        for cp in cps:
            cp.start()
        for cp in cps:
            cp.wait_recv()
        for cp in cps:
            cp.wait_send()

    return pl.pallas_call(
        body, name=name, in_specs=[ANY] * n, out_specs=[ANY] * n,
        out_shape=[jax.ShapeDtypeStruct(l.shape, l.dtype) for l in lands],
        input_output_aliases={a: a for a in range(n)},
        scratch_shapes=[pltpu.SemaphoreType.DMA((n, 3)), pltpu.SemaphoreType.DMA((n, 3))],
    )(*lands)


def _gather_relay(lands, name):
    n = len(lands)

    def body(*refs):
        zones, outs = refs[:n], refs[n:2 * n]
        send_sems, recv_sems = refs[2 * n:]
        x, y, c, _ = _place()
        south = c == 0
        near_x, near_y, across = 4 * (1 - x) + 2 * y + c, 4 * x + 2 * (1 - y) + c, 4 * (1 - x) + 2 * (1 - y) + c
        passed = jnp.where(south, near_y, near_x)
        onward = (jnp.where(south, 1 - x, x), jnp.where(south, y, 1 - y), c)

        def copy(a, k, slot, to):
            return pltpu.make_async_remote_copy(src_ref=zones[a].at[slot], dst_ref=outs[a].at[slot], send_sem=send_sems.at[a, k],
                                                recv_sem=recv_sems.at[a, k], device_id=to, device_id_type=MESH)

        first = [copy(a, 0, passed, onward) for a in range(n)]
        first += [copy(a, 1 + j, slot, (x, y, 1 - c)) for j, slot in enumerate((near_x, near_y)) for a in range(n)]
        for cp in first:
            cp.start()
        last = []
        for a in range(n):
            copy(a, 0, across, onward).wait_recv()
            last.append(copy(a, 3, across, (x, y, 1 - c)))
            last[-1].start()
        for cp in first[n:] + last:
            cp.wait_recv()
        for cp in first + last:
            cp.wait_send()

    return pl.pallas_call(
        body, name=name, in_specs=[ANY] * n, out_specs=[ANY] * n,
        out_shape=[jax.ShapeDtypeStruct(l.shape, l.dtype) for l in lands],
        input_output_aliases={a: a for a in range(n)},
        scratch_shapes=[pltpu.SemaphoreType.DMA((n, 4)), pltpu.SemaphoreType.DMA((n, 4))],
    )(*lands)


def _sum_everywhere(v, name):
    rows = v.shape[0]

    def body(v_ref, o_ref, buf, send_sems, recv_sems):
        x, y, c, _ = _place()
        me = 4 * x + 2 * y + c
        buf[me] = v_ref[...]
        cps = []
        for k in range(1, N_DEV):
            fx, fy, fc = (k >> 2) & 1, (k >> 1) & 1, k & 1
            to = (1 - x if fx else x, 1 - y if fy else y, 1 - c if fc else c)
            cps.append(pltpu.make_async_remote_copy(src_ref=buf.at[me], dst_ref=buf.at[me], send_sem=send_sems.at[k - 1],
                                                    recv_sem=recv_sems.at[k - 1], device_id=to, device_id_type=MESH))
        for cp in cps:
            cp.start()
        for cp in cps:
            cp.wait_recv()
        for cp in cps:
            cp.wait_send()
        acc = buf[0]
        for d in range(1, N_DEV):
            acc = acc + buf[d]
        o_ref[...] = acc

    vm = pl.BlockSpec(memory_space=pltpu.VMEM)
    return pl.pallas_call(
        body, name=name, in_specs=[vm], out_specs=vm, out_shape=jax.ShapeDtypeStruct(v.shape, F32),
        scratch_shapes=[pltpu.VMEM((N_DEV, rows, 128), F32), pltpu.SemaphoreType.DMA((N_DEV - 1,)), pltpu.SemaphoreType.DMA((N_DEV - 1,))],
    )(v)


def _adamw_parts(w, got, mine, me, m, v, name, after=()):
    r, c = w.shape
    n_parts = got.shape[0]
    steps, blk, at = _tiles(r, c)

    def body(me_ref, w_ref, got_ref, own_ref, m_ref, v_ref, *rest):
        go_ref, d_ref, mo_ref, vo_ref = rest[len(after):]
        own = own_ref[...].astype(F32)
        gv = jnp.where(me_ref[0] == 0, own, got_ref[0].astype(F32))
        for d in range(1, n_parts):
            gv = gv + jnp.where(me_ref[0] == d, own, got_ref[d].astype(F32))
        _adamw_math(gv, w_ref, m_ref, v_ref, go_ref, d_ref, mo_ref, vo_ref)

    tile = pl.BlockSpec(blk, lambda i, me_ref: at(i))
    out = jax.ShapeDtypeStruct((r, c), F32)
    return pl.pallas_call(
        body, name=name,
        grid_spec=pltpu.PrefetchScalarGridSpec(
            num_scalar_prefetch=1, grid=(steps,),
            in_specs=[tile, pl.BlockSpec((n_parts,) + blk, lambda i, me_ref: (0,) + at(i)),
                      pl.BlockSpec((None,) + blk, lambda i, me_ref: (me_ref[0],) + at(i)), tile, tile] + [ANY] * len(after),
            out_specs=[tile] * 4),
        out_shape=[out] * 4, compiler_params=_params(("parallel",)),
    )(me.astype(jnp.int32).reshape(1), w, got, mine, m, v, *after)


def _adamw_math(gv, w_ref, m_ref, v_ref, go_ref, d_ref, mo_ref, vo_ref):
    mn = B1 * m_ref[...] + (1.0 - B1) * gv
    vn = B2 * v_ref[...] + (1.0 - B2) * (gv * gv)
    m_hat = mn / (1.0 - B1 ** STEP)
    v_hat = vn / (1.0 - B2 ** STEP)
    go_ref[...] = gv
    d_ref[...] = -LR * (m_hat / (jnp.sqrt(v_hat) + ADAM_EPS) + WD * w_ref[...])
    mo_ref[...] = mn
    vo_ref[...] = vn


def _adamw(w, g, m, v, name):
    r, c = w.shape
    parts = g.ndim == 3
    tr = r if r <= 128 else _pick(r, (128, 64))

    def body(w_ref, g_ref, m_ref, v_ref, go_ref, d_ref, mo_ref, vo_ref):
        if parts:
            gv = g_ref[0].astype(F32)
            for d in range(1, g.shape[0]):
                gv = gv + g_ref[d].astype(F32)
        else:
            gv = g_ref[...]
        mn = B1 * m_ref[...] + (1.0 - B1) * gv
        vn = B2 * v_ref[...] + (1.0 - B2) * (gv * gv)
        m_hat = mn / (1.0 - B1 ** STEP)
        v_hat = vn / (1.0 - B2 ** STEP)
        go_ref[...] = gv
        d_ref[...] = -LR * (m_hat / (jnp.sqrt(v_hat) + ADAM_EPS) + WD * w_ref[...])
        mo_ref[...] = mn
        vo_ref[...] = vn

    tile = pl.BlockSpec((tr, c), lambda i: (i, 0))
    g_spec = pl.BlockSpec((g.shape[0], tr, c), lambda i: (0, i, 0)) if parts else tile
    out = jax.ShapeDtypeStruct((r, c), F32)
    return pl.pallas_call(
        body, name=name, grid=(r // tr,), in_specs=[tile, g_spec, tile, tile], out_specs=[tile] * 4, out_shape=[out] * 4,
        compiler_params=_params(("parallel",)),
    )(w, g, m, v)


BIG = ["w_in", "w_a_out", "w_b_out", "w_mix_out", "w_ff_gate", "w_ff_up", "w_ff_down", "w_ple_gate", "w_ple_proj"]
TRANSPOSED = ["w_in", "w_ff_gate", "w_ff_up"]
GRAD_OF = dict(w_ple_proj="pp", w_ple_gate="pg", w_ff_down="down", w_ff_gate="gate_t", w_ff_up="up_t", w_mix_out="mix", w_a_out="a_out",
               w_b_out="b_out")
SMALL = ["conv_w", "w_alpha_up", "b_alpha_up", "gla_head_gain", "g_pre_mix", "g_post_mix", "g_pre_ffn", "g_post_ffn", "g_pre_ple", "g_post_ple"]
WEIGHTS = ["w_in", "conv_w", "w_a_out", "w_alpha_up", "b_alpha_up", "gla_head_gain", "w_b_out", "w_mix_out", "g_pre_mix", "g_post_mix",
           "g_pre_ffn", "g_post_ffn", "w_ff_gate", "w_ff_up", "w_ff_down", "g_pre_ple", "g_post_ple", "w_ple_gate", "w_ple_proj"]


def _in_t_from_blocks(z):
    w = z.reshape(-1, z.shape[-1])
    return jnp.concatenate([w[R_GA:R_END], w[:R_ALR], w[R_ALR:R_GA], jnp.zeros((128 - GATE_RANK, w.shape[1]), w.dtype)], axis=0)


def _blocks_from_in_t(g):
    per = R_END // N_DEV

    def rows(lo, hi):
        out = []
        for n0, n1, p0 in ((0, R_ALR, C_AX), (R_ALR, R_GA, C_ALR), (R_GA, R_END, 0)):
            a, e = max(lo, n0), min(hi, n1)
            if a < e:
                out.append(g[p0 + a - n0:p0 + e - n0])
        return out

    return jnp.stack([jnp.concatenate(rows(b * per, (b + 1) * per), axis=0) for b in range(N_DEV)])


def _cols_to_full(g8):
    n, r, c = g8.shape
    return jnp.transpose(g8, (1, 0, 2)).reshape(r, n * c)


def _full_to_cols(a):
    r, c = a.shape
    return jnp.transpose(a.reshape(r, N_DEV, c // N_DEV), (1, 0, 2))


def _pack(arrs, rows):
    flat = jnp.concatenate([a.reshape(-1) for a in arrs])
    return jnp.pad(flat, (0, rows * 128 - flat.shape[0])).reshape(rows, 128)


def _unpack(packed, shapes):
    flat, out, o = packed.reshape(-1), [], 0
    for s in shapes:
        size = 1
        for d in s:
            size *= d
        out.append(flat[o:o + size].reshape(s))
        o += size
    return out


def kernel(x, p, w_in, conv_w, w_a_out, w_alpha_up, b_alpha_up, gla_head_gain, w_b_out, w_mix_out, g_pre_mix, g_post_mix, g_pre_ffn, g_post_ffn, w_ff_gate, w_ff_up, w_ff_down, g_pre_ple, g_post_ple, w_ple_gate, w_ple_proj, loss_target, m_w_in, m_conv_w, m_w_a_out, m_w_alpha_up, m_b_alpha_up, m_gla_head_gain, m_w_b_out, m_w_mix_out, m_g_pre_mix, m_g_post_mix, m_g_pre_ffn, m_g_post_ffn, m_w_ff_gate, m_w_ff_up, m_w_ff_down, m_g_pre_ple, m_g_post_ple, m_w_ple_gate, m_w_ple_proj, v_w_in, v_conv_w, v_w_a_out, v_w_alpha_up, v_b_alpha_up, v_gla_head_gain, v_w_b_out, v_w_mix_out, v_g_pre_mix, v_g_post_mix, v_g_pre_ffn, v_g_post_ffn, v_w_ff_gate, v_w_ff_up, v_w_ff_down, v_g_pre_ple, v_g_post_ple, v_w_ple_gate, v_w_ple_proj):
    args = dict(locals())
    shard = lambda n, a: jnp.transpose(a[0]) if n in TRANSPOSED else a[0]
    wts = {n: shard(n, args[n]) for n in WEIGHTS}
    mom = {n: shard(n, args["m_" + n]) for n in WEIGHTS}
    var = {n: shard(n, args["v_" + n]) for n in WEIGHTS}
    me =4 * lax.axis_index("x") + 2 * lax.axis_index("y") + lax.axis_index("c")

    groups = [["w_in", "conv_w", "w_alpha_up"], ["w_a_out", "w_b_out", "w_mix_out"], ["w_ff_gate", "w_ff_up"], ["w_ff_down"],
              ["w_ple_gate", "w_ple_proj"]]
    grad_groups = []
    rows_full = lambda g: g.reshape(-1, g.shape[-1])
    gathers, scatters = {}, {}

    def gather_start(gi, after):
        if gi not in gathers:
            shards = [wts[n].astype(BF16) if n in BIG else wts[n] for n in groups[gi]]
            zones = [jax.ShapeDtypeStruct((N_DEV,) + s.shape, s.dtype) for s in shards]
            plan, peers = (_gather_plan_near, 3) if gi == 0 else (_gather_plan, 4)
            gathers[gi] = (shards, _copies_start(plan, peers * len(shards), shards, zones, "gather_start_%d" % gi, after))
        return gathers[gi][1][-1]

    def gather_finish(gi, after):
        shards, state = gathers[gi]
        shards, zones = _copies_wait(_gather_plan_near if gi == 0 else _gather_plan, state, len(shards), "gather_wait_%d" % gi, after)
        if gi == 0:
            zones = _gather_relay(zones, "gather_relay_%d" % gi)
            gather_start(1, (zones[0],))
        else:
            zones = _gather_finish(zones, "gather_finish_%d" % gi)
        g8 = {n: _put_own(s, z, me, "gather_own_" + n) for n, s, z in zip(groups[gi], shards, zones)}
        if gi == 0:
            return dict(in_t=_in_t_from_blocks(g8["w_in"]),
                        conv_w=_cols_to_full(g8["conv_w"]), w_alpha_up=_cols_to_full(g8["w_alpha_up"]))
        if gi == 1:
            return dict(a_out=_cols_to_full(g8["w_a_out"]), b_out=_cols_to_full(g8["w_b_out"]), mix=rows_full(g8["w_mix_out"]))
        if gi == 2:
            return dict(gate_t=rows_full(g8["w_ff_gate"]), up_t=rows_full(g8["w_ff_up"]))
        if gi == 3:
            return dict(down=rows_full(g8["w_ff_down"]))
        return dict(pg=rows_full(g8["w_ple_gate"]), pp=g8["w_ple_proj"])

    def scatter_start(names, gw):
        gi = len(grad_groups)
        grad_groups.append(names)
        full = {n: _blocks_from_in_t(gw["in_t"]) if n == "w_in" else gw[GRAD_OF[n]] for n in names}
        for n in names:
            if n in ("w_a_out", "w_b_out"):
                full[n] = _full_to_cols(full[n])
        parts = [full[n] if full[n].ndim == 3 else full[n].reshape(N_DEV, -1, full[n].shape[-1]) for n in names]
        if names == ["w_in"]:
            from_sibling = _sibling_exchange(parts, "scatter_sibling_%d" % gi)
            parts = [_pair_add(a, b, "scatter_add_%d_%s" % (gi, n)) for n, a, b in zip(names, parts, from_sibling)]
            scatters[gi] = _copies_start(_chip_plan, 3 * len(parts), parts, parts, "scatter_start_%d" % gi)
        else:
            scatters[gi] = _copies_start(_scatter_plan, (N_DEV - 1) * len(parts), parts, parts, "scatter_start_%d" % gi)
        return scatters[gi][-1]

    small = {n: wts[n].reshape(1, -1) for n in SMALL[2:]}

    loss_rows, grad_x, gs = _local_step(x[0], p[0, 0], loss_target[0], gather_start, gather_finish, scatter_start, small)
    gs["loss"] = jnp.sum(loss_rows).reshape(1, 1)

    small_shapes = [gs[n].shape for n in SMALL]
    gs_packed = _pack([gs[n] for n in SMALL + ["loss"]], 192)
    small_state = _copies_start(_everyone_plan, N_DEV - 1, [gs_packed], [jax.ShapeDtypeStruct((N_DEV,) + gs_packed.shape, F32)],
                                "small_start", (grad_x,))

    res, done = {}, (small_state[-1],)
    for gi, names in enumerate(grad_groups):
        plan, slot = (_chip_plan, me // 2) if names == ["w_in"] else (_scatter_plan, me)
        mine, got = _copies_wait(plan, scatters[gi], len(names), "scatter_wait_%d" % gi, done)
        for n, g, own in zip(names, got, mine):
            res[n] = _adamw_parts(wts[n], g, own, slot, mom[n], var[n], "adamw_" + n)
        done = tuple(res[n][1] for n in names)

    (gs_own,), (gs_got,) = _copies_wait(_everyone_plan, small_state, 1, "small_wait", done)
    gsum = dict(zip(SMALL + ["loss"], _unpack(_sum_parts(gs_got, gs_own, me, "small_sum"), small_shapes + [(1, 1)])))
    loss = gsum["loss"].reshape(())
    gsum["conv_w"] = lax.dynamic_index_in_dim(gsum["conv_w"].reshape(3, N_DEV, -1), me, axis=1, keepdims=False)
    gsum["w_alpha_up"] = lax.dynamic_index_in_dim(gsum["w_alpha_up"].reshape(GATE_RANK, N_DEV, -1), me, axis=1, keepdims=False)

    shard_shapes = [wts[n].shape for n in SMALL]
    packed = [_pack([d[n] for n in SMALL], 120) for d in (wts, gsum, mom, var)]
    outs = [_unpack(o, shard_shapes) for o in _adamw(*packed, "adamw_small")]
    for i, n in enumerate(SMALL):
        res[n] = [o[i] for o in outs]

    back = lambda n, a: (jnp.transpose(a) if n in TRANSPOSED else a)[None]
    return (loss, grad_x[None], *[back(n, res[n][i]) for i in range(4) for n in WEIGHTS])
```

```python
import functools

import jax
import jax.numpy as jnp
from jax import lax
from jax.experimental import pallas as pl
from jax.experimental.pallas import tpu as pltpu
from jax.experimental.pallas import tpu_sc as plsc

F32, BF16 = jnp.float32, jnp.bfloat16
EPS = 1e-6
CHUNK = 64
STEP_CHUNKS = 2
STEP_ROWS = STEP_CHUNKS * CHUNK
HEADS, DK, DV = 4, 128, 256
GATE_RANK = 16
TAU = 16.0
LR, B1, B2, ADAM_EPS, WD, STEP = 0.001, 0.9, 0.999, 1e-08, 0.01, 10
N_DEV = 8
MESH = pl.DeviceIdType.MESH
VMEM_LIMIT = 56 * 1024 * 1024
ANY = pl.BlockSpec(memory_space=pl.ANY)

C_GA, C_GB = 0, 2048
C_AX, C_AB, C_AC, C_Q, C_K, C_V, C_OG, C_ALR = 0, 1024, 2048, 3072, 3584, 4096, 5120, 6144
REST = 6272
R_AX, R_AB, R_AC, R_Q, R_K, R_V, R_OG, R_ALR, R_GA, R_GB, R_END = 0, 1024, 2048, 3072, 3584, 4096, 5120, 6144, 6160, 8208, 10256


def _params(sem):
    return pltpu.CompilerParams(dimension_semantics=sem, vmem_limit_bytes=VMEM_LIMIT)


def _pick(n, cands):
    for c in cands:
        if n % c == 0:
            return c
    return n


def _tiles(r, c):
    for tr in (128, 64):
        if r % tr == 0:
            return r // tr, (tr, c), lambda i: (i, 0)
    tc = _pick(c, (256, 128))
    return c // tc, (r, tc), lambda i: (0, i)


def _mm(a, b, mode, out_dtype, name, after=(), add=None, b3=False, out3=False, tm=None, tk=None):
    bshape = (b.shape[1], N_DEV * b.shape[2]) if b3 else b.shape
    if mode == "nn":
        (m, k), (k2, n) = a.shape, bshape
    elif mode == "nt":
        (m, k), (n, k2) = a.shape, bshape
    else:
        (k, m), (k2, n) = a.shape, bshape
    assert k == k2 and a.dtype == BF16 and b.dtype == BF16, (name, a.shape, b.shape, a.dtype, b.dtype)
    tm = tm if tm and m % tm == 0 else _pick(m, (2048, 1024, 512, 256))
    tn = _pick(n, (1152, 1024, 1408, 896, 512, 256))
    tk = tk if tk and k % tk == 0 else _pick(k, (2048, 1408, 1152, 1024, 896, 512, 256))
    if out3 or (b3 and mode == "nn"):
        tn = n // N_DEV
    if b3 and mode == "nt":
        tk = k // N_DEV
    nk = k // tk
    dims = {"nn": (((1,), (0,)), ((), ())), "nt": (((1,), (1,)), ((), ())), "tn": (((0,), (0,)), ((), ()))}[mode]
    n_extra = len(after) + (add is not None)

    def body(a_ref, b_ref, *rest):
        o_ref = rest[n_extra]
        prod = lax.dot_general(a_ref[...], b_ref[...], dims, preferred_element_type=F32)
        if nk == 1:
            o_ref[...] = (prod if add is None else prod + rest[0][...]).astype(o_ref.dtype)
            return
        acc_ref = rest[n_extra + 1]
        kk = pl.program_id(2)

        @pl.when(kk == 0)
        def _():
            acc_ref[...] = prod if add is None else prod + rest[0][...]

        @pl.when((kk > 0) & (kk < nk - 1))
        def _():
            acc_ref[...] += prod

        @pl.when(kk == nk - 1)
        def _():
            o_ref[...] = (acc_ref[...] + prod).astype(o_ref.dtype)

    a_spec = pl.BlockSpec((tk, tm), lambda i, j, kk: (kk, i)) if mode == "tn" else pl.BlockSpec((tm, tk), lambda i, j, kk: (i, kk))
    if b3:
        b_spec = (pl.BlockSpec((None, tn, tk), lambda i, j, kk: (kk, j, 0)) if mode == "nt"
                  else pl.BlockSpec((None, tk, tn), lambda i, j, kk: (j, kk, 0)))
    else:
        b_spec = pl.BlockSpec((tn, tk), lambda i, j, kk: (j, kk)) if mode == "nt" else pl.BlockSpec((tk, tn), lambda i, j, kk: (kk, j))
    tile = pl.BlockSpec((tm, tn), lambda i, j, kk: (i, j))
    out_spec = pl.BlockSpec((None, tm, tn), lambda i, j, kk: (j, i, 0)) if out3 else tile
    return pl.pallas_call(
        body, name=name, grid=(m // tm, n // tn, nk),
        in_specs=[a_spec, b_spec] + ([tile] if add is not None else []) + [ANY] * len(after), out_specs=out_spec,
        out_shape=jax.ShapeDtypeStruct((N_DEV, m, tn) if out3 else (m, n), out_dtype),
        scratch_shapes=[pltpu.VMEM((tm, tn), F32)] if nk > 1 else [],
        compiler_params=_params(("parallel", "parallel", "arbitrary")),
    )(a, b, *([add] if add is not None else []), *after)


def _rows(body, t, tr, ins, outs, name):
    in_specs = []
    for arr, sp in ins:
        if sp[0] == "t":
            in_specs.append(pl.BlockSpec((tr, sp[1]), lambda i, cb=sp[2]: (i, cb)))
        else:
            in_specs.append(pl.BlockSpec(arr.shape, lambda i, nd=arr.ndim: (0,) * nd))
    out_specs, out_shape = [], []
    for shape, dt, kind in outs:
        out_specs.append(pl.BlockSpec((tr, shape[1]), lambda i: (i, 0)) if kind == "t" else pl.BlockSpec(shape, lambda i: (0, 0)))
        out_shape.append(jax.ShapeDtypeStruct(shape, dt))
    return pl.pallas_call(
        body, name=name, grid=(t // tr,), in_specs=in_specs, out_specs=out_specs, out_shape=out_shape,
        compiler_params=_params(("arbitrary",)),
    )(*[arr for arr, _ in ins])


def _rinv(v):
    return lax.rsqrt(jnp.mean(v * v, axis=-1, keepdims=True) + EPS)


def _sig(v):
    return 1.0 / (1.0 + jnp.exp(-v))


def _acc(ref, val):
    @pl.when(pl.program_id(0) == 0)
    def _():
        ref[...] = jnp.zeros_like(ref)

    ref[...] += jnp.sum(val, axis=0, keepdims=True)


def _rms_fwd(x, g, name):
    t, d = x.shape

    def body(x_ref, g_ref, h_ref):
        xv = x_ref[...]
        h_ref[...] = (xv * _rinv(xv) * g_ref[...]).astype(BF16)

    return _rows(body, t, 256, [(x, ("t", d, 0)), (g, ("b",))], [((t, d), BF16, "t")], name)[0]


def _post_pre(x, m, g_post, g_pre, name):
    t, d = x.shape

    def body(x_ref, m_ref, gp_ref, gn_ref, xo_ref, h_ref):
        mv = m_ref[...]
        xn = x_ref[...] + mv * _rinv(mv) * gp_ref[...]
        xo_ref[...] = xn
        h_ref[...] = (xn * _rinv(xn) * gn_ref[...]).astype(BF16)

    return _rows(body, t, 128, [(x, ("t", d, 0)), (m, ("t", d, 0)), (g_post, ("b",)), (g_pre, ("b",))],
                 [((t, d), F32, "t"), ((t, d), BF16, "t")], name)


def _mix_fwd(proj, ya, yb, name):
    t, d = ya.shape

    def body(ga_ref, gb_ref, ya_ref, yb_ref, o_ref):
        o_ref[...] = (_sig(ga_ref[...].astype(F32)) * ya_ref[...].astype(F32)
                      + _sig(gb_ref[...].astype(F32)) * yb_ref[...].astype(F32)).astype(BF16)

    return _rows(body, t, 256, [(proj, ("t", d, C_GA // d)), (proj, ("t", d, C_GB // d)), (ya, ("t", d, 0)), (yb, ("t", d, 0))],
                 [((t, d), BF16, "t")], name)[0]


def _mix_bwd(dmix, proj, ya, yb, name):
    t, d = ya.shape

    def body(dm_ref, ga_ref, gb_ref, ya_ref, yb_ref, dg_ref, dya_ref, dyb_ref):
        dm = dm_ref[...]
        sa, sb = _sig(ga_ref[...].astype(F32)), _sig(gb_ref[...].astype(F32))
        dg_ref[:, :d] = (dm * ya_ref[...].astype(F32) * sa * (1.0 - sa)).astype(BF16)
        dg_ref[:, d:] = (dm * yb_ref[...].astype(F32) * sb * (1.0 - sb)).astype(BF16)
        dya_ref[...] = (dm * sa).astype(BF16)
        dyb_ref[...] = (dm * sb).astype(BF16)

    return _rows(body, t, 128,
                 [(dmix, ("t", d, 0)), (proj, ("t", d, C_GA // d)), (proj, ("t", d, C_GB // d)), (ya, ("t", d, 0)), (yb, ("t", d, 0))],
                 [((t, 2 * d), BF16, "t"), ((t, d), BF16, "t"), ((t, d), BF16, "t")], name)


def _swiglu_call(body, ins, n_out, name):
    t, f = ins[0].shape
    tc = _pick(f, (1408, 512))
    tile = pl.BlockSpec((512, tc), lambda i, j: (i, j))
    return pl.pallas_call(
        body, name=name, grid=(t // 512, f // tc), in_specs=[tile] * len(ins), out_specs=[tile] * n_out,
        out_shape=[jax.ShapeDtypeStruct((t, f), BF16)] * n_out, compiler_params=_params(("parallel", "parallel")),
    )(*ins)


def _swiglu_fwd(fg, fu, name):
    def body(g_ref, u_ref, s_ref):
        gv = g_ref[...].astype(F32)
        s_ref[...] = (gv * _sig(gv) * u_ref[...].astype(F32)).astype(BF16)

    return _swiglu_call(body, [fg, fu], 1, name)[0]


def _swiglu_bwd(ds, fg, fu, name):
    def body(ds_ref, g_ref, u_ref, dg_ref, du_ref):
        dsv, gv, uv = ds_ref[...].astype(F32), g_ref[...].astype(F32), u_ref[...].astype(F32)
        sg = _sig(gv)
        dg_ref[...] = (dsv * uv * sg * (1.0 + gv * (1.0 - sg))).astype(BF16)
        du_ref[...] = (dsv * gv * sg).astype(BF16)

    return _swiglu_call(body, [ds, fg, fu], 2, name)


def _ple_final(x2, pg, pp, tgt, g_post, name):
    t, d = x2.shape

    def body(x_ref, pg_ref, pp_ref, t_ref, g_ref, loss_ref, d3_ref, dpg_ref, dpp_ref, dg_ref):
        sg, ppv, g = _sig(pg_ref[...]), pp_ref[...], g_ref[...]
        e = sg * ppv
        r = _rinv(e)
        eh = e * r
        diff = x_ref[...] + eh * g - t_ref[...]
        loss_ref[...] = 0.5 * jnp.mean(diff * diff, axis=-1, keepdims=True)
        d3 = diff * (1.0 / d)
        d3_ref[...] = d3
        gd = d3 * g
        de = r * (gd - eh * jnp.mean(gd * eh, axis=-1, keepdims=True))
        dpg_ref[...] = (de * ppv * sg * (1.0 - sg)).astype(BF16)
        dpp_ref[...] = (de * sg).astype(BF16)
        _acc(dg_ref, d3 * eh)

    return _rows(body, t, 128, [(x2, ("t", d, 0)), (pg, ("t", d, 0)), (pp, ("t", d, 0)), (tgt, ("t", d, 0)), (g_post, ("b",))],
                 [((t, 1), F32, "t"), ((t, d), F32, "t"), ((t, d), BF16, "t"), ((t, d), BF16, "t"), ((1, d), F32, "a")], name)


def _norm_bwd(dn, dh, x, g_pre, fm, g_post, name):
    t, d = x.shape
    two = fm is not None

    def body(*refs):
        if two:
            dn_ref, dh_ref, x_ref, gp_ref, f_ref, gq_ref, dx_ref, df_ref, dgp_ref, dgq_ref = refs
        else:
            dn_ref, dh_ref, x_ref, gp_ref, dx_ref, dgp_ref = refs
        xv, dhv = x_ref[...], dh_ref[...]
        r = _rinv(xv)
        xh = xv * r
        gd = dhv * gp_ref[...]
        dx = dn_ref[...] + r * (gd - xh * jnp.mean(gd * xh, axis=-1, keepdims=True))
        dx_ref[...] = dx
        _acc(dgp_ref, dhv * xh)
        if two:
            fv = f_ref[...]
            rf = _rinv(fv)
            fh = fv * rf
            gd2 = dx * gq_ref[...]
            df_ref[...] = (rf * (gd2 - fh * jnp.mean(gd2 * fh, axis=-1, keepdims=True))).astype(BF16)
            _acc(dgq_ref, dx * fh)

    ins = [(dn, ("t", d, 0)), (dh, ("t", d, 0)), (x, ("t", d, 0)), (g_pre, ("b",))]
    outs = [((t, d), F32, "t")]
    if two:
        ins += [(fm, ("t", d, 0)), (g_post, ("b",))]
        outs += [((t, d), BF16, "t"), ((1, d), F32, "a"), ((1, d), F32, "a")]
    else:
        outs += [((1, d), F32, "a")]
    return _rows(body, t, 128, ins, outs, name)


CONV_TC = 256


def _shift_down(v, s):
    rows = lax.broadcasted_iota(jnp.int32, v.shape, 0)
    return jnp.where(rows >= s, pltpu.roll(v, s, 0), 0.0)


def _shift_up(v, s):
    n = v.shape[0]
    rows = lax.broadcasted_iota(jnp.int32, v.shape, 0)
    return jnp.where(rows < n - s, pltpu.roll(v, n - s, 0), 0.0)


def _conv_specs(t):
    nb = 1024 // CONV_TC
    seg = lambda c0: pl.BlockSpec((t, CONV_TC), lambda j, cb=c0 // CONV_TC: (0, cb + j))
    own = pl.BlockSpec((t, CONV_TC), lambda j: (0, j))
    wspec = pl.BlockSpec((3, CONV_TC), lambda j: (0, j))
    return nb, seg, own, wspec


def _conv_fwd(proj, conv_w, name, after=()):
    t = proj.shape[0]
    nb, seg, own, wspec = _conv_specs(t)

    def body(ax_ref, ab_ref, ac_ref, w_ref, *rest):
        za_ref = rest[len(after)]
        u = ac_ref[...].astype(F32) * ax_ref[...].astype(F32)
        w = w_ref[...]
        yc = w[0:1] * _shift_down(u, 2) + w[1:2] * _shift_down(u, 1) + w[2:3] * u
        za_ref[...] = (ab_ref[...].astype(F32) * yc).astype(BF16)

    return pl.pallas_call(
        body, name=name, grid=(nb,), in_specs=[seg(C_AX), seg(C_AB), seg(C_AC), wspec] + [ANY] * len(after), out_specs=own,
        out_shape=jax.ShapeDtypeStruct((t, 1024), BF16), compiler_params=_params(("parallel",)),
    )(proj, proj, proj, conv_w, *after)


def _conv_bwd(dza, proj, conv_w, name):
    t = proj.shape[0]
    nb, seg, own, wspec = _conv_specs(t)

    def body(dz_ref, ax_ref, ab_ref, ac_ref, w_ref, dax_ref, dab_ref, dac_ref, dw_ref):
        ax, ab, ac, dz = ax_ref[...].astype(F32), ab_ref[...].astype(F32), ac_ref[...].astype(F32), dz_ref[...].astype(F32)
        w = w_ref[...]
        u = ac * ax
        u1, u2 = _shift_down(u, 1), _shift_down(u, 2)
        yc = w[0:1] * u2 + w[1:2] * u1 + w[2:3] * u
        dab_ref[...] = (dz * yc).astype(BF16)
        dyc = dz * ab
        du = w[2:3] * dyc + w[1:2] * _shift_up(dyc, 1) + w[0:1] * _shift_up(dyc, 2)
        dax_ref[...] = (du * ac).astype(BF16)
        dac_ref[...] = (du * ax).astype(BF16)
        dw_ref[0:1, :] = jnp.sum(dyc * u2, axis=0, keepdims=True)
        dw_ref[1:2, :] = jnp.sum(dyc * u1, axis=0, keepdims=True)
        dw_ref[2:3, :] = jnp.sum(dyc * u, axis=0, keepdims=True)

    act = jax.ShapeDtypeStruct((t, 1024), BF16)
    return pl.pallas_call(
        body, name=name, grid=(nb,), in_specs=[own, seg(C_AX), seg(C_AB), seg(C_AC), wspec], out_specs=[own, own, own, wspec],
        out_shape=[act, act, act, jax.ShapeDtypeStruct((3, 1024), F32)], compiler_params=_params(("parallel",)),
    )(dza, proj, proj, proj, conv_w)


def _dot(a, b, dims, precision=None):
    return lax.dot_general(a, b, (dims, ((), ())), precision=precision, preferred_element_type=F32)


_CONTRACT = {"nn": ((1,), (0,)), "nt": ((1,), (1,)), "tn": ((0,), (0,))}


def _bdot_raw(a, b, mode):
    return _dot(a.astype(BF16), b.astype(BF16), _CONTRACT[mode])


@functools.partial(jax.custom_vjp, nondiff_argnums=(2,))
def _bdot(a, b, mode):
    return _bdot_raw(a, b, mode)


def _bdot_fwd(a, b, mode):
    return _bdot_raw(a, b, mode), (a, b)


def _bdot_bwd(mode, res, ct):
    a, b = res
    if mode == "nn":
        return _bdot_raw(ct, b, "nt"), _bdot_raw(a, ct, "tn")
    if mode == "nt":
        return _bdot_raw(ct, b, "nn"), _bdot_raw(ct, a, "tn")
    return _bdot_raw(b, ct, "nt"), _bdot_raw(a, ct, "nn")


_bdot.defvjp(_bdot_fwd, _bdot_bwd)


@functools.partial(jax.custom_vjp, nondiff_argnums=(2,))
def _sum_dot(ones, x, mode):
    head = x.astype(BF16)
    tail = x - head.astype(F32)
    if mode == "nn":
        return _bdot_raw(ones, head, "nn") + _bdot_raw(ones, tail, "nn")
    return _bdot_raw(head, ones, "tn") + _bdot_raw(tail, ones, "tn")


def _sum_dot_fwd(ones, x, mode):
    return _sum_dot(ones, x, mode), ones


def _sum_dot_bwd(mode, ones, ct):
    return jnp.zeros_like(ones), (_bdot_raw(ones, ct, "tn") if mode == "nn" else _bdot_raw(ones, ct, "nt"))


_sum_dot.defvjp(_sum_dot_fwd, _sum_dot_bwd)


def _gla_chunk(q, k, v, og, alr, s_in, wa, ba, gain):
    c = q.shape[0]
    z =_bdot(alr, wa, "nn") + ba
    la = (jnp.minimum(z, 0.0) - jnp.log(1.0 + jnp.exp(-jnp.abs(z)))) * (1.0 / TAU)
    row = lax.broadcasted_iota(jnp.int32, (c, c), 0)
    col = lax.broadcasted_iota(jnp.int32, (c, c), 1)
    lower = row >= col
    b = _sum_dot(lower.astype(F32), la, "nn")
    trow = lax.broadcasted_iota(jnp.int32, la.shape, 0)
    mid = jnp.sum(jnp.where(trow <= c // 2, la, 0.0), axis=0, keepdims=True)
    blast = jnp.sum(la, axis=0, keepdims=True)
    qs = q * (DK ** -0.5)
    e_up, e_dn = jnp.exp(b - mid), jnp.exp(mid - b)
    a_fwd = _bdot(qs * e_up, k * e_dn, "nt")
    a_rev = _bdot(qs * e_dn, k * e_up, "nt")
    att = jnp.where(lower, a_fwd, a_rev)
    o = _bdot(att, v, "nn") + _bdot(qs * jnp.exp(b), s_in, "nn")
    upd = _bdot(k * jnp.exp(blast - b), v, "tn")
    blast_col = _sum_dot(jnp.ones((c, DV), F32), la, "tn")
    s_out = jnp.exp(blast_col) * s_in + upd
    on = o * _rinv(o) * gain
    return on * og * _sig(og), s_out


def _gla_specs(t, rev):
    n = t // STEP_ROWS
    ch = (lambda i: n - 1 - i) if rev else (lambda i: i)
    col = lambda w, c0: pl.BlockSpec((STEP_ROWS, HEADS * w), lambda i, cb=c0 // (HEADS * w): (ch(i), cb))
    whole = lambda shape: pl.BlockSpec(shape, lambda i, nd=len(shape): (0,) * nd)
    specs = dict(
        q=col(DK, C_Q), k=col(DK, C_K), v=col(DV, C_V), og=col(DV, C_OG),
        alr=pl.BlockSpec((STEP_ROWS, 128), lambda i: (ch(i), C_ALR // 128)),
        wa=whole((128, HEADS * DK)), ba=whole((1, HEADS * DK)), gain=whole((1, DV)),
        state=pl.BlockSpec((STEP_CHUNKS, HEADS, DK, DV), lambda i: (ch(i), 0, 0, 0)),
        odk=pl.BlockSpec((STEP_ROWS, HEADS * DK), lambda i: (ch(i), 0)), odv=pl.BlockSpec((STEP_ROWS, HEADS * DV), lambda i: (ch(i), 0)),
        oalr=pl.BlockSpec((STEP_ROWS, 128), lambda i: (ch(i), 0)), whole=whole,
    )
    return n, specs


def _head_cols(h):
    return slice(h * DK, (h + 1) * DK), slice(h * DV, (h + 1) * DV)


def _gla_fwd(proj, wa, ba, gain, name):
    t = proj.shape[0]
    n, sp = _gla_specs(t, False)

    def body(q_ref, k_ref, v_ref, og_ref, alr_ref, wa_ref, ba_ref, g_ref, zb_ref, st_ref, s_scr):
        @pl.when(pl.program_id(0) == 0)
        def _():
            s_scr[...] = jnp.zeros_like(s_scr)

        state = [s_scr[h] for h in range(HEADS)]
        for c in range(STEP_CHUNKS):
            rows = slice(c * CHUNK, (c + 1) * CHUNK)
            alr = alr_ref[rows, :].astype(F32)
            for h in range(HEADS):
                kc, vc = _head_cols(h)
                st_ref[c, h] = state[h]
                zb, state[h] = _gla_chunk(q_ref[rows, kc].astype(F32), k_ref[rows, kc].astype(F32), v_ref[rows, vc].astype(F32),
                                          og_ref[rows, vc].astype(F32), alr, state[h], wa_ref[:, kc].astype(F32), ba_ref[:, kc], g_ref[...])
                zb_ref[rows, vc] = zb.astype(BF16)
        for h in range(HEADS):
            s_scr[h] = state[h]

    return pl.pallas_call(
        body, name=name, grid=(n,),
        in_specs=[sp["q"], sp["k"], sp["v"], sp["og"], sp["alr"], sp["wa"], sp["ba"], sp["gain"]],
        out_specs=[sp["odv"], sp["state"]],
        out_shape=[jax.ShapeDtypeStruct((t, HEADS * DV), BF16), jax.ShapeDtypeStruct((t // CHUNK, HEADS, DK, DV), F32)],
        scratch_shapes=[pltpu.VMEM((HEADS, DK, DV), F32)],
        compiler_params=_params(("arbitrary",)),
    )(proj, proj, proj, proj, proj, wa, ba, gain)


def _gla_bwd(dzb, proj, states, wa, ba, gain, name):
    t = proj.shape[0]
    n, sp = _gla_specs(t, True)

    def body(dz_ref, q_ref, k_ref, v_ref, og_ref, alr_ref, st_ref, wa_ref, ba_ref, g_ref,
             dq_ref, dk_ref, dv_ref, dog_ref, dalr_ref, dwa_ref, dba_ref, dg_ref, ds_scr):
        @pl.when(pl.program_id(0) == 0)
        def _():
            ds_scr[...] = jnp.zeros_like(ds_scr)
            dwa_ref[...] = jnp.zeros_like(dwa_ref)
            dba_ref[...] = jnp.zeros_like(dba_ref)
            dg_ref[...] = jnp.zeros_like(dg_ref)

        dstate = [ds_scr[h] for h in range(HEADS)]
        dwa_sum, dba_sum, dgain_sum = [None] * HEADS, [None] * HEADS, None
        for c in reversed(range(STEP_CHUNKS)):
            rows = slice(c * CHUNK, (c + 1) * CHUNK)
            alr = alr_ref[rows, :].astype(F32)
            dalr_sum = None
            for h in range(HEADS):
                kc, vc = _head_cols(h)
                args = (q_ref[rows, kc].astype(F32), k_ref[rows, kc].astype(F32), v_ref[rows, vc].astype(F32), og_ref[rows, vc].astype(F32),
                        alr, st_ref[c, h], wa_ref[:, kc].astype(F32), ba_ref[:, kc], g_ref[...])
                _, vjp = jax.vjp(_gla_chunk, *args)
                dq, dk, dv, dog, dalr, dstate[h], dwa, dba, dgain = vjp((dz_ref[rows, vc].astype(F32), dstate[h]))
                dq_ref[rows, kc] = dq.astype(BF16)
                dk_ref[rows, kc] = dk.astype(BF16)
                dv_ref[rows, vc] = dv.astype(BF16)
                dog_ref[rows, vc] = dog.astype(BF16)
                dwa_sum[h] = dwa if dwa_sum[h] is None else dwa_sum[h] + dwa
                dba_sum[h] = dba if dba_sum[h] is None else dba_sum[h] + dba
                dalr_sum = dalr if dalr_sum is None else dalr_sum + dalr
                dgain_sum = dgain if dgain_sum is None else dgain_sum + dgain
            dalr_ref[rows, :] = dalr_sum
        for h in range(HEADS):
            ds_scr[h] = dstate[h]
            dwa_ref[h] += dwa_sum[h]
            dba_ref[h] += dba_sum[h]
        dg_ref[...] += dgain_sum

    whole = sp["whole"]
    return pl.pallas_call(
        body, name=name, grid=(n,),
        in_specs=[sp["odv"], sp["q"], sp["k"], sp["v"], sp["og"], sp["alr"], sp["state"], sp["wa"], sp["ba"], sp["gain"]],
        out_specs=[sp["odk"], sp["odk"], sp["odv"], sp["odv"], sp["oalr"], whole((HEADS, 128, DK)), whole((HEADS, 1, DK)), whole((1, DV))],
        out_shape=[jax.ShapeDtypeStruct((t, HEADS * DK), BF16), jax.ShapeDtypeStruct((t, HEADS * DK), BF16),
                   jax.ShapeDtypeStruct((t, HEADS * DV), BF16), jax.ShapeDtypeStruct((t, HEADS * DV), BF16),
                   jax.ShapeDtypeStruct((t, 128), F32), jax.ShapeDtypeStruct((HEADS, 128, DK), F32),
                   jax.ShapeDtypeStruct((HEADS, 1, DK), F32), jax.ShapeDtypeStruct((1, DV), F32)],
        scratch_shapes=[pltpu.VMEM((HEADS, DK, DV), F32)],
        compiler_params=_params(("arbitrary",)),
    )(dzb, proj, proj, proj, proj, proj, states, wa, ba, gain)


def _local_step(x, p, tgt, gather_start, gather_finish, scatter_start, small):
    b_alpha, gain = small["b_alpha_up"], small["gla_head_gain"]
    gather_start(0, ())
    w = dict(gather_finish(0, ()))
    conv_w, w_alpha = w["conv_w"], w["w_alpha_up"]
    wa_p = jnp.zeros((128, HEADS * DK), BF16).at[:GATE_RANK].set(w_alpha.astype(BF16))

    t2 = gather_start(2, (w["in_rest_t"], gather_start(1, ())))
    h1 = _rms_fwd(x, small["g_pre_mix"], "rms_pre_mix")
    proj = _mm(h1, w["in_rest_t"], "nt", BF16, "mm_proj", after=(t2,))
    proj_gates = _mm(h1, w["in_gates_t"], "nt", BF16, "mm_proj_gates", after=(t2,))
    za = _conv_fwd(proj, conv_w, "conv_fwd")
    zb, states = _gla_fwd(proj, wa_p, b_alpha, gain, "gla_fwd")
    t3 = gather_start(3, (zb, za))
    w.update(gather_finish(1, (t3,)))
    ya = _mm(za, w["a_out"], "nn", BF16, "mm_ya")
    yb = _mm(zb, w["b_out"], "nn", BF16, "mm_yb")
    mix = _mix_fwd(proj_gates, ya, yb, "mix_fwd")
    m2 = _mm(mix, w["mix"], "nn", F32, "mm_mix")
    t4 = gather_start(4, (m2,))
    x1, h2 = _post_pre(x, m2, small["g_post_mix"], small["g_pre_ffn"], "norm_mix_ffn")
    w.update(gather_finish(2, (h2, t4)))
    fg = _mm(h2, w["gate_t"], "nt", BF16, "mm_gate")
    fu = _mm(h2, w["up_t"], "nt", BF16, "mm_up")
    s = _swiglu_fwd(fg, fu, "swiglu_fwd")
    w.update(gather_finish(3, (s,)))
    f = _mm(s, w["down"], "nn", F32, "mm_down")
    x2, h3 = _post_pre(x1, f, small["g_post_ffn"], small["g_pre_ple"], "norm_ffn_ple")
    w.update(gather_finish(4, (h3,)))
    pg = _mm(h3, w["pg"], "nn", F32, "mm_pg")
    p_bf = p.astype(BF16)
    pp = _mm(p_bf, w["pp"], "nn", F32, "mm_pp", b3=True, tm=2048)
    loss_rows, d3, dpg, dpp, dg_post_ple = _ple_final(x2, pg, pp, tgt, small["g_post_ple"], "ple_final")

    gw = {}
    gw["pp"] = _mm(p_bf, dpp, "tn", BF16, "mm_dw_pp", out3=True)
    gw["pg"] = _mm(h3, dpg, "tn", BF16, "mm_dw_pg")
    dh3 = _mm(dpg, w["pg"], "nt", F32, "mm_dh3", after=(scatter_start(["w_ple_proj", "w_ple_gate"], gw),))
    d2, df, dg_pre_ple, dg_post_ffn = _norm_bwd(d3, dh3, x2, small["g_pre_ple"], f, small["g_post_ffn"], "norm_bwd_ple_ffn")
    gw["down"] = _mm(s, df, "tn", BF16, "mm_dw_down", tm=1408)
    ds = _mm(df, w["down"], "nt", BF16, "mm_ds", after=(scatter_start(["w_ff_down"], gw),))
    dfg, dfu = _swiglu_bwd(ds, fg, fu, "swiglu_bwd")
    gw["gate_t"] = _mm(dfg, h2, "tn", BF16, "mm_dw_gate", tm=1408)
    gw["up_t"] = _mm(dfu, h2, "tn", BF16, "mm_dw_up", after=(gw["gate_t"],), tm=1408)
    dh2 = _mm(dfg, w["gate_t"], "nn", F32, "mm_dh2_gate", after=(scatter_start(["w_ff_gate", "w_ff_up"], gw),))
    dh2 = _mm(dfu, w["up_t"], "nn", F32, "mm_dh2_up", add=dh2, tm=1024)
    d1, dm2, dg_pre_ffn, dg_post_mix = _norm_bwd(d2, dh2, x1, small["g_pre_ffn"], m2, small["g_post_mix"], "norm_bwd_ffn_mix")
    dmix = _mm(dm2, w["mix"], "nt", F32, "mm_dmix")
    gw["mix"] = _mm(mix, dm2, "tn", BF16, "mm_dw_mix")
    dgab, dya, dyb = _mix_bwd(dmix, proj_gates, ya, yb, "mix_bwd")
    gw["in_gates_t"] = _mm(dgab, h1, "tn", BF16, "mm_dw_in_gates")
    dza = _mm(dya, w["a_out"], "nt", BF16, "mm_dza", after=(scatter_start(["w_mix_out"], gw), gw["in_gates_t"]))
    gw["a_out"] = _mm(za, dya, "tn", BF16, "mm_dw_a_out")
    gw["b_out"] = _mm(zb, dyb, "tn", BF16, "mm_dw_b_out", after=(gw["a_out"],))
    dzb = _mm(dyb, w["b_out"], "nt", BF16, "mm_dzb", after=(scatter_start(["w_a_out", "w_b_out"], gw),))
    dax, dab, dac, dconv = _conv_bwd(dza, proj, conv_w, "conv_bwd")
    dq, dk, dv, dog, dalr, dwa, dba, dgain = _gla_bwd(dzb, proj, states, wa_p, b_alpha, gain, "gla_bwd")
    drest = jnp.concatenate([dax, dab, dac, dq, dk, dv, dog, dalr.astype(BF16)], axis=1)
    gw["in_rest_t"] = _mm(drest, h1, "tn", BF16, "mm_dw_in_rest", tm=896)
    dh1 = _mm(dgab, w["in_gates_t"], "nn", F32, "mm_dh1_gates", after=(scatter_start(["w_in"], gw),), tm=1024)
    dh1 = _mm(drest, w["in_rest_t"], "nn", F32, "mm_dh1_rest", add=dh1, tm=1024)
    grad_x, dg_pre_mix = _norm_bwd(d1, dh1, x, small["g_pre_mix"], None, None, "norm_bwd_mix")

    gs = dict(
        conv_w=dconv,
        w_alpha_up=jnp.transpose(dwa[:, :GATE_RANK, :], (1, 0, 2)).reshape(GATE_RANK, HEADS * DK),
        b_alpha_up=dba.reshape(1, HEADS * DK), gla_head_gain=dgain,
        g_pre_mix=dg_pre_mix, g_post_mix=dg_post_mix, g_pre_ffn=dg_pre_ffn, g_post_ffn=dg_post_ffn,
        g_pre_ple=dg_pre_ple, g_post_ple=dg_post_ple,
    )
    return loss_rows, grad_x, gs


def _place():
    x, y, c = lax.axis_index("x"), lax.axis_index("y"), lax.axis_index("c")
    return x, y, c, [(1 - x, y), (x, 1 - y), (1 - x, 1 - y)]


def _all_gather(shards, name, cid=None):
    n = len(shards)

    def body(*refs):
        ins, outs = refs[:n], refs[n:2 * n]
        send_sems, recv_sems, local_sems = refs[2 * n:]
        x, y, c, chips = _place()
        me, sibling = (x, y, c), (x, y, 1 - c)

        def slot(px, py, pc):
            return 4 * px + 2 * py + pc

        def copy(a, k, block, to, src=None):
            dst = outs[a].at[slot(*block)]
            return pltpu.make_async_remote_copy(src_ref=dst if src is None else src, dst_ref=dst, send_sem=send_sems.at[a, k],
                                                recv_sem=recv_sems.at[a, k], device_id=to, device_id_type=MESH)

        mine = [pltpu.make_async_copy(ins[a], outs[a].at[slot(*me)], local_sems.at[a]) for a in range(n)]
        for cp in mine:
            cp.start()
        first = []
        for j, chip in enumerate(chips):
            first += [copy(a, 1 + j, me, (*chip, c), src=ins[a]) for a in range(n)]
        first += [copy(a, 0, me, sibling, src=ins[a]) for a in range(n)]
        for cp in first:
            cp.start()
        passed = []
        for j, chip in enumerate(chips):
            for a in range(n):
                copy(a, 1 + j, (*chip, c), me).wait_recv()
                cp = copy(a, 4 + j, (*chip, c), sibling)
                cp.start()
                passed.append(cp)
        for a in range(n):
            copy(a, 0, sibling, me).wait_recv()
        for j, chip in enumerate(chips):
            for a in range(n):
                copy(a, 4 + j, (*chip, 1 - c), me).wait_recv()
        for cp in first + passed:
            cp.wait_send()
        for cp in mine:
            cp.wait()

    if cid is None:
        return pl.pallas_call(
            body, name=name, in_specs=[ANY] * n, out_specs=[ANY] * n,
            out_shape=[jax.ShapeDtypeStruct((N_DEV,) + s.shape, s.dtype) for s in shards],
            scratch_shapes=[pltpu.SemaphoreType.DMA((n, 7)), pltpu.SemaphoreType.DMA((n, 7)), pltpu.SemaphoreType.DMA((n,))],
        )(*shards)

    src = [jax.new_ref(s, memory_space=pltpu.MemorySpace.HBM) for s in shards]
    dst = [jax.empty_ref(jax.ShapeDtypeStruct((N_DEV,) + s.shape, s.dtype), memory_space=pltpu.MemorySpace.HBM) for s in shards]

    @pl.kernel(mesh=plsc.ScalarSubcoreMesh(axis_name="seq", num_cores=1), name=name,
               scratch_types=(pltpu.SemaphoreType.DMA((n, 7)), pltpu.SemaphoreType.DMA((n, 7)), pltpu.SemaphoreType.DMA((n,))),
               compiler_params=pltpu.CompilerParams(collective_id=cid))
    def launch(send_sems, recv_sems, local_sems):
        x, y, c, chips = _place()
        barrier = pltpu.get_barrier_semaphore()
        for peer in [(x, y, 1 - c)] + [(*chip, c) for chip in chips]:
            pl.semaphore_signal(barrier, inc=1, device_id=peer, device_id_type=MESH)
        pl.semaphore_wait(barrier, 4)
        body(*src, *dst, send_sems, recv_sems, local_sems)

    launch()
    return [r[...] for r in dst]


def _reduce_scatter(parts, name, cid):
    n = len(parts)
    src = [jax.new_ref(s, memory_space=pltpu.MemorySpace.HBM) for s in parts]
    dst = [jax.empty_ref(jax.ShapeDtypeStruct(s.shape, s.dtype), memory_space=pltpu.MemorySpace.HBM) for s in parts]

    @pl.kernel(mesh=plsc.ScalarSubcoreMesh(axis_name="seq", num_cores=1), name=name,
               scratch_types=(pltpu.SemaphoreType.DMA((n, N_DEV - 1)), pltpu.SemaphoreType.DMA((n, N_DEV - 1)), pltpu.SemaphoreType.DMA((n,))),
               compiler_params=pltpu.CompilerParams(collective_id=cid))
    def launch(send_sems, recv_sems, local_sems):
        x, y, c, _ = _place()
        me = 4 * x + 2 * y + c
        peers = [(1 - x if k & 4 else x, 1 - y if k & 2 else y, 1 - c if k & 1 else c) for k in range(1, N_DEV)]
        barrier = pltpu.get_barrier_semaphore()
        for peer in peers:
            pl.semaphore_signal(barrier, inc=1, device_id=peer, device_id_type=MESH)
        pl.semaphore_wait(barrier, N_DEV - 1)
        mine = [pltpu.make_async_copy(src[a].at[me], dst[a].at[me], local_sems.at[a]) for a in range(n)]
        for cp in mine:
            cp.start()
        cps = []
        for a in range(n):
            for k, (px, py, pc) in enumerate(peers):
                cps.append(pltpu.make_async_remote_copy(src_ref=src[a].at[4 * px + 2 * py + pc], dst_ref=dst[a].at[me], send_sem=send_sems.at[a, k],
                                                        recv_sem=recv_sems.at[a, k], device_id=(px, py, pc), device_id_type=MESH))
        for cp in cps:
            cp.start()
        for cp in cps:
            cp.wait_recv()
        for cp in cps:
            cp.wait_send()
        for cp in mine:
            cp.wait()

    launch()
    return [r[...] for r in dst]


def _sibling_exchange(parts, name):
    n = len(parts)
    pieces = [_row_pieces(s.shape[1]) for s in parts]

    def body(*refs):
        ins, outs = refs[:n], refs[n:2 * n]
        send_sems, recv_sems = refs[2 * n:]
        x, y, c, _ = _place()

        def copy(a, ch, q, rows):
            return pltpu.make_async_remote_copy(src_ref=ins[a].at[2 * ch + 1 - c, rows], dst_ref=outs[a].at[ch, rows], send_sem=send_sems.at[a, ch, q],
                                                recv_sem=recv_sems.at[a, ch, q], device_id=(x, y, 1 - c), device_id_type=MESH)

        cps = [copy(a, ch, q, rows) for ch in range(4) for a in range(n) for q, rows in enumerate(pieces[a])]
        for cp in cps:
            cp.start()
        for cp in cps:
            cp.wait_recv()
        for cp in cps:
            cp.wait_send()

    return pl.pallas_call(
        body, name=name, in_specs=[ANY] * n, out_specs=[ANY] * n,
        out_shape=[jax.ShapeDtypeStruct((4,) + s.shape[1:], s.dtype) for s in parts],
        scratch_shapes=[pltpu.SemaphoreType.DMA((n, 4, PIECES)), pltpu.SemaphoreType.DMA((n, 4, PIECES))],
    )(*parts)


def _chip_exchange(parts, name):
    n = len(parts)

    def body(*refs):
        ins, outs = refs[:n], refs[n:2 * n]
        send_sems, recv_sems, local_sems = refs[2 * n:]
        x, y, c, chips = _place()
        my_chip = 2 * x + y

        def copy(a, j):
            px, py = chips[j]
            return pltpu.make_async_remote_copy(src_ref=ins[a].at[2 * px + py], dst_ref=outs[a].at[my_chip], send_sem=send_sems.at[a, j],
                                                recv_sem=recv_sems.at[a, j], device_id=(px, py, c), device_id_type=MESH)

        def landing(a, j):
            px, py = chips[j]
            return pltpu.make_async_remote_copy(src_ref=ins[a].at[my_chip], dst_ref=outs[a].at[2 * px + py], send_sem=send_sems.at[a, j],
                                                recv_sem=recv_sems.at[a, j], device_id=(px, py, c), device_id_type=MESH)

        mine = [pltpu.make_async_copy(ins[a].at[my_chip], outs[a].at[my_chip], local_sems.at[a]) for a in range(n)]
        for cp in mine:
            cp.start()
        cps = [copy(a, j) for j in range(3) for a in range(n)]
        for cp in cps:
            cp.start()
        for j in range(3):
            for a in range(n):
                landing(a, j).wait_recv()
        for cp in cps:
            cp.wait_send()
        for cp in mine:
            cp.wait()

    return pl.pallas_call(
        body, name=name, in_specs=[ANY] * n, out_specs=[ANY] * n,
        out_shape=[jax.ShapeDtypeStruct(s.shape, s.dtype) for s in parts],
        scratch_shapes=[pltpu.SemaphoreType.DMA((n, 3)), pltpu.SemaphoreType.DMA((n, 3)), pltpu.SemaphoreType.DMA((n,))],
    )(*parts)


def _pair_add(mine8, got4, name):
    _, r, cols = mine8.shape
    steps, blk, at = _tiles(r, cols)
    core = lax.axis_index("c").astype(jnp.int32).reshape(1)

    def body(c_ref, a_ref, b_ref, o_ref):
        o_ref[...] = (a_ref[...].astype(F32) + b_ref[...].astype(F32)).astype(BF16)

    return pl.pallas_call(
        body, name=name,
        grid_spec=pltpu.PrefetchScalarGridSpec(
            num_scalar_prefetch=1, grid=(4, steps),
            in_specs=[pl.BlockSpec((None,) + blk, lambda ch, i, c_ref: (2 * ch + c_ref[0],) + at(i)),
                      pl.BlockSpec((None,) + blk, lambda ch, i, c_ref: (ch,) + at(i))],
            out_specs=pl.BlockSpec((None,) + blk, lambda ch, i, c_ref: (ch,) + at(i))),
        out_shape=jax.ShapeDtypeStruct((4, r, cols), BF16),
        compiler_params=_params(("parallel", "parallel")),
    )(core, mine8, got4)


HBM = pl.BlockSpec(memory_space=pltpu.HBM)
SEM = pl.BlockSpec(memory_space=pltpu.SEMAPHORE)
EFFECT = pltpu.SideEffectType.DATAFLOW_SIDE_EFFECTING


def _in_hbm(a):
    return pltpu.with_memory_space_constraint(a, pltpu.HBM)


def _remote_copies(plan, srcs, lands, send_sems, recv_sems):
    return [pltpu.make_async_remote_copy(src_ref=s, dst_ref=d, send_sem=send_sems.at[i], recv_sem=recv_sems.at[i], device_id=peer,
                                         device_id_type=MESH) for i, (s, d, peer) in enumerate(plan(srcs, lands))]


def _copies_start(plan, n_copies, srcs, land_shapes, name, after=()):
    ns, nl = len(srcs), len(land_shapes)

    def body(*refs):
        send_sems, recv_sems = refs[ns + nl + len(after):ns + nl + len(after) + 2]
        for cp in _remote_copies(plan, refs[:ns], refs[ns:ns + nl], send_sems, recv_sems):
            cp.start()
        refs[-1][...] = jnp.zeros((8, 128), F32)

    sems = pltpu.SemaphoreType.DMA((n_copies,))
    return pl.pallas_call(
        body, name=name,
        out_shape=(sems, sems, *[pltpu.HBM(s.shape, s.dtype) for s in srcs], *[pltpu.HBM(s.shape, s.dtype) for s in land_shapes],
                   jax.ShapeDtypeStruct((8, 128), F32)),
        in_specs=[HBM] * (ns + nl) + [ANY] * len(after),
        out_specs=(SEM, SEM, *[HBM] * (ns + nl), pl.BlockSpec(memory_space=pltpu.VMEM)),
        input_output_aliases={i: 2 + i for i in range(ns + nl)},
        compiler_params=pltpu.CompilerParams(has_side_effects=EFFECT),
    )(*[_in_hbm(s) for s in srcs], *[_in_hbm(lax.empty(s.shape, s.dtype)) for s in land_shapes], *after)


def _copies_wait(plan, state, ns, name, after=()):
    send_sems, recv_sems, *arrs = state[:-1]
    n = len(arrs)

    def body(*refs):
        cps = _remote_copies(plan, refs[:ns], refs[ns:n], refs[n], refs[n + 1])
        for cp in cps:
            cp.wait_send()
        for cp in cps:
            cp.wait_recv()

    out = pl.pallas_call(
        body, name=name, out_shape=tuple(pltpu.HBM(a.shape, a.dtype) for a in arrs),
        in_specs=[HBM] * n + [SEM, SEM] + [ANY] * len(after), out_specs=tuple([HBM] * n),
        input_output_aliases={i: i for i in range(n)},
        compiler_params=pltpu.CompilerParams(has_side_effects=EFFECT),
    )(*arrs, send_sems, recv_sems, *after)
    return list(out[:ns]), list(out[ns:])


def _gather_plan(srcs, lands):
    x, y, c, chips = _place()
    peers = [(x, y, 1 - c)] + [(*chip, c) for chip in chips]
    return [(s, l.at[4 * x + 2 * y + c], peer) for s, l in zip(srcs, lands) for peer in peers]


def _gather_plan_near(srcs, lands):
    x, y, c, _ = _place()
    peers = [(x, y, 1 - c), (1 - x, y, c), (x, 1 - y, c)]
    return [(s, l.at[4 * x + 2 * y + c], peer) for s, l in zip(srcs, lands) for peer in peers]


def _scatter_plan(srcs, lands):
    x, y, c, _ = _place()
    peers = [(1 - x if k & 4 else x, 1 - y if k & 2 else y, 1 - c if k & 1 else c) for k in range(1, N_DEV)]
    return [(s.at[4 * px + 2 * py + pc], l.at[4 * x + 2 * y + c], (px, py, pc)) for s, l in zip(srcs, lands) for px, py, pc in peers]


def _everyone_plan(srcs, lands):
    x, y, c, _ = _place()
    peers = [(1 - x if k & 4 else x, 1 - y if k & 2 else y, 1 - c if k & 1 else c) for k in range(1, N_DEV)]
    return [(s, l.at[4 * x + 2 * y + c], peer) for s, l in zip(srcs, lands) for peer in peers]


def _sum_parts(got, own, me, name):
    def body(me_ref, got_ref, own_ref, o_ref):
        acc = jnp.where(me_ref[0] == 0, own_ref[...], got_ref[0])
        for d in range(1, N_DEV):
            acc = acc + jnp.where(me_ref[0] == d, own_ref[...], got_ref[d])
        o_ref[...] = acc

    return pl.pallas_call(
        body, name=name,
        grid_spec=pltpu.PrefetchScalarGridSpec(
            num_scalar_prefetch=1, grid=(1,),
            in_specs=[pl.BlockSpec(got.shape, lambda i, me_ref: (0, 0, 0)), pl.BlockSpec(own.shape, lambda i, me_ref: (0, 0))],
            out_specs=pl.BlockSpec(own.shape, lambda i, me_ref: (0, 0))),
        out_shape=jax.ShapeDtypeStruct(own.shape, F32),
    )(me.astype(jnp.int32).reshape(1), got, own)


def _chip_plan(srcs, lands):
    x, y, c, chips = _place()
    return [(s.at[2 * px + py], l.at[2 * x + y], (px, py, c)) for s, l in zip(srcs, lands) for px, py in chips]


PIECES = 8


def _row_pieces(rows):
    for k in (PIECES, 4, 2):
        if rows % (16 * k) == 0:
            return [pl.ds(q * (rows // k), rows // k) for q in range(k)]
    return [pl.ds(0, rows)]


def _put_own(shard, zone, me, name):
    r, c = shard.shape
    tr = r if r <= 256 else _pick(r, (256, 64))

    def body(me_ref, s_ref, z_ref, o_ref):
        o_ref[...] = s_ref[...]

    return pl.pallas_call(
        body, name=name,
        grid_spec=pltpu.PrefetchScalarGridSpec(
            num_scalar_prefetch=1, grid=(r // tr,),
            in_specs=[pl.BlockSpec((tr, c), lambda i, me_ref: (i, 0)), ANY],
            out_specs=pl.BlockSpec((None, tr, c), lambda i, me_ref: (me_ref[0], i, 0))),
        out_shape=jax.ShapeDtypeStruct(zone.shape, zone.dtype), input_output_aliases={2: 0},
        compiler_params=_params(("arbitrary",)),
    )(me.astype(jnp.int32).reshape(1), shard, zone)


def _gather_finish(lands, name):
    n = len(lands)

    def body(*refs):
        zones, outs = refs[:n], refs[n:2 * n]
        send_sems, recv_sems = refs[2 * n:]
        x, y, c, chips = _place()
        cps = [pltpu.make_async_remote_copy(
            src_ref=zones[a].at[4 * px + 2 * py + c], dst_ref=outs[a].at[4 * px + 2 * py + c], send_sem=send_sems.at[a, j],
            recv_sem=recv_sems.at[a, j], device_id=(x, y, 1 - c), device_id_type=MESH) for j, (px, py) in enumerate(chips) for a in range(n)]
        for cp in cps:
            cp.start()
        for cp in cps:
            cp.wait_recv()
        for cp in cps:
            cp.wait_send()

    return pl.pallas_call(
        body, name=name, in_specs=[ANY] * n, out_specs=[ANY] * n,
        out_shape=[jax.ShapeDtypeStruct(l.shape, l.dtype) for l in lands],
        input_output_aliases={a: a for a in range(n)},
        scratch_shapes=[pltpu.SemaphoreType.DMA((n, 3)), pltpu.SemaphoreType.DMA((n, 3))],
    )(*lands)


def _gather_relay(lands, name):
    n = len(lands)

    def body(*refs):
        zones, outs = refs[:n], refs[n:2 * n]
        send_sems, recv_sems = refs[2 * n:]
        x, y, c, _ = _place()
        south = c == 0
        near_x, near_y, across = 4 * (1 - x) + 2 * y + c, 4 * x + 2 * (1 - y) + c, 4 * (1 - x) + 2 * (1 - y) + c
        passed = jnp.where(south, near_y, near_x)
        onward = (jnp.where(south, 1 - x, x), jnp.where(south, y, 1 - y), c)

        def copy(a, k, slot, to):
            return pltpu.make_async_remote_copy(src_ref=zones[a].at[slot], dst_ref=outs[a].at[slot], send_sem=send_sems.at[a, k],
                                                recv_sem=recv_sems.at[a, k], device_id=to, device_id_type=MESH)

        first = [copy(a, 0, passed, onward) for a in range(n)]
        first += [copy(a, 1 + j, slot, (x, y, 1 - c)) for j, slot in enumerate((near_x, near_y)) for a in range(n)]
        for cp in first:
            cp.start()
        last = []
        for a in range(n):
            copy(a, 0, across, onward).wait_recv()
            last.append(copy(a, 3, across, (x, y, 1 - c)))
            last[-1].start()
        for cp in first[n:] + last:
            cp.wait_recv()
        for cp in first + last:
            cp.wait_send()

    return pl.pallas_call(
        body, name=name, in_specs=[ANY] * n, out_specs=[ANY] * n,
        out_shape=[jax.ShapeDtypeStruct(l.shape, l.dtype) for l in lands],
        input_output_aliases={a: a for a in range(n)},
        scratch_shapes=[pltpu.SemaphoreType.DMA((n, 4)), pltpu.SemaphoreType.DMA((n, 4))],
    )(*lands)


def _sum_everywhere(v, name):
    rows = v.shape[0]

    def body(v_ref, o_ref, buf, send_sems, recv_sems):
        x, y, c, _ = _place()
        me = 4 * x + 2 * y + c
        buf[me] = v_ref[...]
        cps = []
        for k in range(1, N_DEV):
            fx, fy, fc = (k >> 2) & 1, (k >> 1) & 1, k & 1
            to = (1 - x if fx else x, 1 - y if fy else y, 1 - c if fc else c)
            cps.append(pltpu.make_async_remote_copy(src_ref=buf.at[me], dst_ref=buf.at[me], send_sem=send_sems.at[k - 1],
                                                    recv_sem=recv_sems.at[k - 1], device_id=to, device_id_type=MESH))
        for cp in cps:
            cp.start()
        for cp in cps:
            cp.wait_recv()
        for cp in cps:
            cp.wait_send()
        acc = buf[0]
        for d in range(1, N_DEV):
            acc = acc + buf[d]
        o_ref[...] = acc

    vm = pl.BlockSpec(memory_space=pltpu.VMEM)
    return pl.pallas_call(
        body, name=name, in_specs=[vm], out_specs=vm, out_shape=jax.ShapeDtypeStruct(v.shape, F32),
        scratch_shapes=[pltpu.VMEM((N_DEV, rows, 128), F32), pltpu.SemaphoreType.DMA((N_DEV - 1,)), pltpu.SemaphoreType.DMA((N_DEV - 1,))],
    )(v)


def _adamw_parts(w, got, mine, me, m, v, name, after=()):
    r, c = w.shape
    n_parts = got.shape[0]
    steps, blk, at = _tiles(r, c)

    def body(me_ref, w_ref, got_ref, own_ref, m_ref, v_ref, *rest):
        go_ref, d_ref, mo_ref, vo_ref = rest[len(after):]
        own = own_ref[...].astype(F32)
        gv = jnp.where(me_ref[0] == 0, own, got_ref[0].astype(F32))
        for d in range(1, n_parts):
            gv = gv + jnp.where(me_ref[0] == d, own, got_ref[d].astype(F32))
        _adamw_math(gv, w_ref, m_ref, v_ref, go_ref, d_ref, mo_ref, vo_ref)

    tile = pl.BlockSpec(blk, lambda i, me_ref: at(i))
    out = jax.ShapeDtypeStruct((r, c), F32)
    return pl.pallas_call(
        body, name=name,
        grid_spec=pltpu.PrefetchScalarGridSpec(
            num_scalar_prefetch=1, grid=(steps,),
            in_specs=[tile, pl.BlockSpec((n_parts,) + blk, lambda i, me_ref: (0,) + at(i)),
                      pl.BlockSpec((None,) + blk, lambda i, me_ref: (me_ref[0],) + at(i)), tile, tile] + [ANY] * len(after),
            out_specs=[tile] * 4),
        out_shape=[out] * 4, compiler_params=_params(("parallel",)),
    )(me.astype(jnp.int32).reshape(1), w, got, mine, m, v, *after)


def _adamw_math(gv, w_ref, m_ref, v_ref, go_ref, d_ref, mo_ref, vo_ref):
    mn = B1 * m_ref[...] + (1.0 - B1) * gv
    vn = B2 * v_ref[...] + (1.0 - B2) * (gv * gv)
    m_hat = mn / (1.0 - B1 ** STEP)
    v_hat = vn / (1.0 - B2 ** STEP)
    go_ref[...] = gv
    d_ref[...] = -LR * (m_hat / (jnp.sqrt(v_hat) + ADAM_EPS) + WD * w_ref[...])
    mo_ref[...] = mn
    vo_ref[...] = vn


def _adamw(w, g, m, v, name):
    r, c = w.shape
    parts = g.ndim == 3
    tr = r if r <= 128 else _pick(r, (128, 64))

    def body(w_ref, g_ref, m_ref, v_ref, go_ref, d_ref, mo_ref, vo_ref):
        if parts:
            gv = g_ref[0].astype(F32)
            for d in range(1, g.shape[0]):
                gv = gv + g_ref[d].astype(F32)
        else:
            gv = g_ref[...]
        mn = B1 * m_ref[...] + (1.0 - B1) * gv
        vn = B2 * v_ref[...] + (1.0 - B2) * (gv * gv)
        m_hat = mn / (1.0 - B1 ** STEP)
        v_hat = vn / (1.0 - B2 ** STEP)
        go_ref[...] = gv
        d_ref[...] = -LR * (m_hat / (jnp.sqrt(v_hat) + ADAM_EPS) + WD * w_ref[...])
        mo_ref[...] = mn
        vo_ref[...] = vn

    tile = pl.BlockSpec((tr, c), lambda i: (i, 0))
    g_spec = pl.BlockSpec((g.shape[0], tr, c), lambda i: (0, i, 0)) if parts else tile
    out = jax.ShapeDtypeStruct((r, c), F32)
    return pl.pallas_call(
        body, name=name, grid=(r // tr,), in_specs=[tile, g_spec, tile, tile], out_specs=[tile] * 4, out_shape=[out] * 4,
        compiler_params=_params(("parallel",)),
    )(w, g, m, v)


BIG = ["w_in", "w_a_out", "w_b_out", "w_mix_out", "w_ff_gate", "w_ff_up", "w_ff_down", "w_ple_gate", "w_ple_proj"]
TRANSPOSED = ["w_in", "w_ff_gate", "w_ff_up"]
GRAD_OF = dict(w_ple_proj="pp", w_ple_gate="pg", w_ff_down="down", w_ff_gate="gate_t", w_ff_up="up_t", w_mix_out="mix", w_a_out="a_out",
               w_b_out="b_out")
SMALL = ["conv_w", "w_alpha_up", "b_alpha_up", "gla_head_gain", "g_pre_mix", "g_post_mix", "g_pre_ffn", "g_post_ffn", "g_pre_ple", "g_post_ple"]
WEIGHTS = ["w_in", "conv_w", "w_a_out", "w_alpha_up", "b_alpha_up", "gla_head_gain", "w_b_out", "w_mix_out", "g_pre_mix", "g_post_mix",
           "g_pre_ffn", "g_post_ffn", "w_ff_gate", "w_ff_up", "w_ff_down", "g_pre_ple", "g_post_ple", "w_ple_gate", "w_ple_proj"]


def _in_t_from_blocks(z):
    w = z.reshape(-1, z.shape[-1])
    return w[R_GA:R_END], jnp.concatenate([w[:R_GA], jnp.zeros((REST - R_GA, w.shape[1]), w.dtype)], axis=0)


def _blocks_from_in_t(g_gates, g_rest):
    per = R_END // N_DEV

    def rows(lo, hi):
        out = []
        for n0, n1, g in ((0, R_GA, g_rest), (R_GA, R_END, g_gates)):
            a, e = max(lo, n0), min(hi, n1)
            if a < e:
                out.append(g[a - n0:e - n0])
        return out

    return jnp.stack([jnp.concatenate(rows(b * per, (b + 1) * per), axis=0) for b in range(N_DEV)])


def _cols_to_full(g8):
    n, r, c = g8.shape
    return jnp.transpose(g8, (1, 0, 2)).reshape(r, n * c)


def _full_to_cols(a):
    r, c = a.shape
    return jnp.transpose(a.reshape(r, N_DEV, c // N_DEV), (1, 0, 2))


def _pack(arrs, rows):
    flat = jnp.concatenate([a.reshape(-1) for a in arrs])
    return jnp.pad(flat, (0, rows * 128 - flat.shape[0])).reshape(rows, 128)


def _unpack(packed, shapes):
    flat, out, o = packed.reshape(-1), [], 0
    for s in shapes:
        size = 1
        for d in s:
            size *= d
        out.append(flat[o:o + size].reshape(s))
        o += size
    return out


def kernel(x, p, w_in, conv_w, w_a_out, w_alpha_up, b_alpha_up, gla_head_gain, w_b_out, w_mix_out, g_pre_mix, g_post_mix, g_pre_ffn, g_post_ffn, w_ff_gate, w_ff_up, w_ff_down, g_pre_ple, g_post_ple, w_ple_gate, w_ple_proj, loss_target, m_w_in, m_conv_w, m_w_a_out, m_w_alpha_up, m_b_alpha_up, m_gla_head_gain, m_w_b_out, m_w_mix_out, m_g_pre_mix, m_g_post_mix, m_g_pre_ffn, m_g_post_ffn, m_w_ff_gate, m_w_ff_up, m_w_ff_down, m_g_pre_ple, m_g_post_ple, m_w_ple_gate, m_w_ple_proj, v_w_in, v_conv_w, v_w_a_out, v_w_alpha_up, v_b_alpha_up, v_gla_head_gain, v_w_b_out, v_w_mix_out, v_g_pre_mix, v_g_post_mix, v_g_pre_ffn, v_g_post_ffn, v_w_ff_gate, v_w_ff_up, v_w_ff_down, v_g_pre_ple, v_g_post_ple, v_w_ple_gate, v_w_ple_proj):
    args = dict(locals())
    shard = lambda n, a: jnp.transpose(a[0]) if n in TRANSPOSED else a[0]
    wts = {n: shard(n, args[n]) for n in WEIGHTS}
    mom = {n: shard(n, args["m_" + n]) for n in WEIGHTS}
    var = {n: shard(n, args["v_" + n]) for n in WEIGHTS}
    me =4 * lax.axis_index("x") + 2 * lax.axis_index("y") + lax.axis_index("c")

    groups = [["w_in", "conv_w", "w_alpha_up"], ["w_a_out", "w_b_out", "w_mix_out"], ["w_ff_gate", "w_ff_up"], ["w_ff_down"],
              ["w_ple_gate", "w_ple_proj"]]
    grad_groups = []
    rows_full = lambda g: g.reshape(-1, g.shape[-1])
    gathers, scatters = {}, {}

    def gather_start(gi, after):
        if gi not in gathers:
            shards = [wts[n].astype(BF16) if n in BIG else wts[n] for n in groups[gi]]
            zones = [jax.ShapeDtypeStruct((N_DEV,) + s.shape, s.dtype) for s in shards]
            plan, peers = (_gather_plan_near, 3) if gi == 0 else (_gather_plan, 4)
            gathers[gi] = (shards, _copies_start(plan, peers * len(shards), shards, zones, "gather_start_%d" % gi, after))
        return gathers[gi][1][-1]

    def gather_finish(gi, after):
        shards, state = gathers[gi]
        shards, zones = _copies_wait(_gather_plan_near if gi == 0 else _gather_plan, state, len(shards), "gather_wait_%d" % gi, after)
        if gi == 0:
            zones = _gather_relay(zones, "gather_relay_%d" % gi)
            gather_start(1, (zones[0],))
        else:
            zones = _gather_finish(zones, "gather_finish_%d" % gi)
        g8 = {n: _put_own(s, z, me, "gather_own_" + n) for n, s, z in zip(groups[gi], shards, zones)}
        if gi == 0:
            in_gates_t, in_rest_t = _in_t_from_blocks(g8["w_in"])
            return dict(in_gates_t=in_gates_t, in_rest_t=in_rest_t,
                        conv_w=_cols_to_full(g8["conv_w"]), w_alpha_up=_cols_to_full(g8["w_alpha_up"]))
        if gi == 1:
            return dict(a_out=_cols_to_full(g8["w_a_out"]), b_out=_cols_to_full(g8["w_b_out"]), mix=rows_full(g8["w_mix_out"]))
        if gi == 2:
            return dict(gate_t=rows_full(g8["w_ff_gate"]), up_t=rows_full(g8["w_ff_up"]))
        if gi == 3:
            return dict(down=rows_full(g8["w_ff_down"]))
        return dict(pg=rows_full(g8["w_ple_gate"]), pp=g8["w_ple_proj"])

    def scatter_start(names, gw):
        gi = len(grad_groups)
        grad_groups.append(names)
        full = {n: _blocks_from_in_t(gw["in_gates_t"], gw["in_rest_t"]) if n == "w_in" else gw[GRAD_OF[n]] for n in names}
        for n in names:
            if n in ("w_a_out", "w_b_out"):
                full[n] = _full_to_cols(full[n])
        parts = [full[n] if full[n].ndim == 3 else full[n].reshape(N_DEV, -1, full[n].shape[-1]) for n in names]
        if names == ["w_in"]:
            from_sibling = _sibling_exchange(parts, "scatter_sibling_%d" % gi)
            parts = [_pair_add(a, b, "scatter_add_%d_%s" % (gi, n)) for n, a, b in zip(names, parts, from_sibling)]
            scatters[gi] = _copies_start(_chip_plan, 3 * len(parts), parts, parts, "scatter_start_%d" % gi)
        else:
            scatters[gi] = _copies_start(_scatter_plan, (N_DEV - 1) * len(parts), parts, parts, "scatter_start_%d" % gi)
        return scatters[gi][-1]

    small = {n: wts[n].reshape(1, -1) for n in SMALL[2:]}

    loss_rows, grad_x, gs = _local_step(x[0], p[0, 0], loss_target[0], gather_start, gather_finish, scatter_start, small)
    gs["loss"] = jnp.sum(loss_rows).reshape(1, 1)

    small_shapes = [gs[n].shape for n in SMALL]
    gs_packed = _pack([gs[n] for n in SMALL + ["loss"]], 192)
    small_state = _copies_start(_everyone_plan, N_DEV - 1, [gs_packed], [jax.ShapeDtypeStruct((N_DEV,) + gs_packed.shape, F32)],
                                "small_start", (grad_x,))

    res, done = {}, (small_state[-1],)
    for gi, names in enumerate(grad_groups):
        plan, slot = (_chip_plan, me // 2) if names == ["w_in"] else (_scatter_plan, me)
        mine, got = _copies_wait(plan, scatters[gi], len(names), "scatter_wait_%d" % gi, done)
        for n, g, own in zip(names, got, mine):
            res[n] = _adamw_parts(wts[n], g, own, slot, mom[n], var[n], "adamw_" + n)
        done = tuple(res[n][1] for n in names)

    (gs_own,), (gs_got,) = _copies_wait(_everyone_plan, small_state, 1, "small_wait", done)
    gsum = dict(zip(SMALL + ["loss"], _unpack(_sum_parts(gs_got, gs_own, me, "small_sum"), small_shapes + [(1, 1)])))
    loss = gsum["loss"].reshape(())
    gsum["conv_w"] = lax.dynamic_index_in_dim(gsum["conv_w"].reshape(3, N_DEV, -1), me, axis=1, keepdims=False)
    gsum["w_alpha_up"] = lax.dynamic_index_in_dim(gsum["w_alpha_up"].reshape(GATE_RANK, N_DEV, -1), me, axis=1, keepdims=False)

    shard_shapes = [wts[n].shape for n in SMALL]
    packed = [_pack([d[n] for n in SMALL], 120) for d in (wts, gsum, mom, var)]
    outs = [_unpack(o, shard_shapes) for o in _adamw(*packed, "adamw_small")]
    for i, n in enumerate(SMALL):
        res[n] = [o[i] for o in outs]

    back = lambda n, a: (jnp.transpose(a) if n in TRANSPOSED else a)[None]
    return (loss, grad_x[None], *[back(n, res[n][i]) for i in range(4) for n in WEIGHTS])
```

```python
import functools

import jax
import jax.numpy as jnp
from jax import lax
from jax.experimental import pallas as pl
from jax.experimental.pallas import tpu as pltpu
from jax.experimental.pallas import tpu_sc as plsc

F32, BF16 = jnp.float32, jnp.bfloat16
EPS = 1e-6
CHUNK = 64
STEP_CHUNKS = 2
STEP_ROWS = STEP_CHUNKS * CHUNK
HEADS, DK, DV = 4, 128, 256
GATE_RANK = 16
TAU = 16.0
LR, B1, B2, ADAM_EPS, WD, STEP = 0.001, 0.9, 0.999, 1e-08, 0.01, 10
N_DEV = 8
MESH = pl.DeviceIdType.MESH
VMEM_LIMIT = 56 * 1024 * 1024
ANY = pl.BlockSpec(memory_space=pl.ANY)

C_GA, C_GB = 0, 2048
C_AX, C_AB, C_AC, C_Q, C_K, C_V, C_OG, C_ALR = 0, 1024, 2048, 3072, 3584, 4096, 5120, 6144
REST = 6272
R_AX, R_AB, R_AC, R_Q, R_K, R_V, R_OG, R_ALR, R_GA, R_GB, R_END = 0, 1024, 2048, 3072, 3584, 4096, 5120, 6144, 6160, 8208, 10256


def _params(sem):
    return pltpu.CompilerParams(dimension_semantics=sem, vmem_limit_bytes=VMEM_LIMIT)


def _pick(n, cands):
    for c in cands:
        if n % c == 0:
            return c
    return n


def _tiles(r, c):
    for tr in (128, 64):
        if r % tr == 0:
            return r // tr, (tr, c), lambda i: (i, 0)
    tc = _pick(c, (256, 128))
    return c // tc, (r, tc), lambda i: (0, i)


def _mm(a, b, mode, out_dtype, name, after=(), add=None, b3=False, out3=False, tm=None, tk=None):
    bshape = (b.shape[1], N_DEV * b.shape[2]) if b3 else b.shape
    if mode == "nn":
        (m, k), (k2, n) = a.shape, bshape
    elif mode == "nt":
        (m, k), (n, k2) = a.shape, bshape
    else:
        (k, m), (k2, n) = a.shape, bshape
    assert k == k2 and a.dtype == BF16 and b.dtype == BF16, (name, a.shape, b.shape, a.dtype, b.dtype)
    tm = tm if tm and m % tm == 0 else _pick(m, (2048, 1024, 512, 256))
    tn = _pick(n, (1152, 1024, 1408, 896, 512, 256))
    tk = tk if tk and k % tk == 0 else _pick(k, (2048, 1408, 1152, 1024, 896, 512, 256))
    if out3 or (b3 and mode == "nn"):
        tn = n // N_DEV
    if b3 and mode == "nt":
        tk = k // N_DEV
    nk = k // tk
    dims = {"nn": (((1,), (0,)), ((), ())), "nt": (((1,), (1,)), ((), ())), "tn": (((0,), (0,)), ((), ()))}[mode]
    n_extra = len(after) + (add is not None)

    def body(a_ref, b_ref, *rest):
        o_ref = rest[n_extra]
        prod = lax.dot_general(a_ref[...], b_ref[...], dims, preferred_element_type=F32)
        if nk == 1:
            o_ref[...] = (prod if add is None else prod + rest[0][...]).astype(o_ref.dtype)
            return
        acc_ref = rest[n_extra + 1]
        kk = pl.program_id(2)

        @pl.when(kk == 0)
        def _():
            acc_ref[...] = prod if add is None else prod + rest[0][...]

        @pl.when((kk > 0) & (kk < nk - 1))
        def _():
            acc_ref[...] += prod

        @pl.when(kk == nk - 1)
        def _():
            o_ref[...] = (acc_ref[...] + prod).astype(o_ref.dtype)

    a_spec = pl.BlockSpec((tk, tm), lambda i, j, kk: (kk, i)) if mode == "tn" else pl.BlockSpec((tm, tk), lambda i, j, kk: (i, kk))
    if b3:
        b_spec = (pl.BlockSpec((None, tn, tk), lambda i, j, kk: (kk, j, 0)) if mode == "nt"
                  else pl.BlockSpec((None, tk, tn), lambda i, j, kk: (j, kk, 0)))
    else:
        b_spec = pl.BlockSpec((tn, tk), lambda i, j, kk: (j, kk)) if mode == "nt" else pl.BlockSpec((tk, tn), lambda i, j, kk: (kk, j))
    tile = pl.BlockSpec((tm, tn), lambda i, j, kk: (i, j))
    out_spec = pl.BlockSpec((None, tm, tn), lambda i, j, kk: (j, i, 0)) if out3 else tile
    return pl.pallas_call(
        body, name=name, grid=(m // tm, n // tn, nk),
        in_specs=[a_spec, b_spec] + ([tile] if add is not None else []) + [ANY] * len(after), out_specs=out_spec,
        out_shape=jax.ShapeDtypeStruct((N_DEV, m, tn) if out3 else (m, n), out_dtype),
        scratch_shapes=[pltpu.VMEM((tm, tn), F32)] if nk > 1 else [],
        compiler_params=_params(("parallel", "parallel", "arbitrary")),
    )(a, b, *([add] if add is not None else []), *after)


def _rows(body, t, tr, ins, outs, name, after=()):
    in_specs = []
    for arr, sp in ins:
        if sp[0] == "t":
            in_specs.append(pl.BlockSpec((tr, sp[1]), lambda i, cb=sp[2]: (i, cb)))
        else:
            in_specs.append(pl.BlockSpec(arr.shape, lambda i, nd=arr.ndim: (0,) * nd))
    out_specs, out_shape = [], []
    for shape, dt, kind in outs:
        out_specs.append(pl.BlockSpec((tr, shape[1]), lambda i: (i, 0)) if kind == "t" else pl.BlockSpec(shape, lambda i: (0, 0)))
        out_shape.append(jax.ShapeDtypeStruct(shape, dt))
    return pl.pallas_call(
        body, name=name, grid=(t // tr,), in_specs=in_specs + [ANY] * len(after), out_specs=out_specs, out_shape=out_shape,
        compiler_params=_params(("arbitrary",)),
    )(*[arr for arr, _ in ins], *after)


def _rinv(v):
    return lax.rsqrt(jnp.mean(v * v, axis=-1, keepdims=True) + EPS)


def _sig(v):
    return 1.0 / (1.0 + jnp.exp(-v))


def _acc(ref, val):
    @pl.when(pl.program_id(0) == 0)
    def _():
        ref[...] = jnp.zeros_like(ref)

    ref[...] += jnp.sum(val, axis=0, keepdims=True)


def _rms_fwd(x, g, name):
    t, d = x.shape

    def body(x_ref, g_ref, h_ref):
        xv = x_ref[...]
        h_ref[...] = (xv * _rinv(xv) * g_ref[...]).astype(BF16)

    return _rows(body, t, 256, [(x, ("t", d, 0)), (g, ("b",))], [((t, d), BF16, "t")], name)[0]


def _post_pre(x, m, g_post, g_pre, name, after=()):
    t, d = x.shape

    def body(x_ref, m_ref, gp_ref, gn_ref, *rest):
        xo_ref, h_ref = rest[len(after):]
        mv = m_ref[...]
        xn = x_ref[...] + mv * _rinv(mv) * gp_ref[...]
        xo_ref[...] = xn
        h_ref[...] = (xn * _rinv(xn) * gn_ref[...]).astype(BF16)

    return _rows(body, t, 128, [(x, ("t", d, 0)), (m, ("t", d, 0)), (g_post, ("b",)), (g_pre, ("b",))],
                 [((t, d), F32, "t"), ((t, d), BF16, "t")], name, after)


def _mix_fwd(proj, ya, yb, name):
    t, d = ya.shape

    def body(ga_ref, gb_ref, ya_ref, yb_ref, o_ref):
        o_ref[...] = (_sig(ga_ref[...].astype(F32)) * ya_ref[...].astype(F32)
                      + _sig(gb_ref[...].astype(F32)) * yb_ref[...].astype(F32)).astype(BF16)

    return _rows(body, t, 256, [(proj, ("t", d, C_GA // d)), (proj, ("t", d, C_GB // d)), (ya, ("t", d, 0)), (yb, ("t", d, 0))],
                 [((t, d), BF16, "t")], name)[0]


def _mix_bwd(dmix, proj, ya, yb, name):
    t, d = ya.shape

    def body(dm_ref, ga_ref, gb_ref, ya_ref, yb_ref, dg_ref, dya_ref, dyb_ref):
        dm = dm_ref[...]
        sa, sb = _sig(ga_ref[...].astype(F32)), _sig(gb_ref[...].astype(F32))
        dg_ref[:, :d] = (dm * ya_ref[...].astype(F32) * sa * (1.0 - sa)).astype(BF16)
        dg_ref[:, d:] = (dm * yb_ref[...].astype(F32) * sb * (1.0 - sb)).astype(BF16)
        dya_ref[...] = (dm * sa).astype(BF16)
        dyb_ref[...] = (dm * sb).astype(BF16)

    return _rows(body, t, 128,
                 [(dmix, ("t", d, 0)), (proj, ("t", d, C_GA // d)), (proj, ("t", d, C_GB // d)), (ya, ("t", d, 0)), (yb, ("t", d, 0))],
                 [((t, 2 * d), BF16, "t"), ((t, d), BF16, "t"), ((t, d), BF16, "t")], name)


def _swiglu_call(body, ins, n_out, name):
    t, f = ins[0].shape
    tc = _pick(f, (1408, 512))
    tile = pl.BlockSpec((512, tc), lambda i, j: (i, j))
    return pl.pallas_call(
        body, name=name, grid=(t // 512, f // tc), in_specs=[tile] * len(ins), out_specs=[tile] * n_out,
        out_shape=[jax.ShapeDtypeStruct((t, f), BF16)] * n_out, compiler_params=_params(("parallel", "parallel")),
    )(*ins)


def _swiglu_fwd(fg, fu, name):
    def body(g_ref, u_ref, s_ref):
        gv = g_ref[...].astype(F32)
        s_ref[...] = (gv * _sig(gv) * u_ref[...].astype(F32)).astype(BF16)

    return _swiglu_call(body, [fg, fu], 1, name)[0]


def _swiglu_bwd(ds, fg, fu, name):
    def body(ds_ref, g_ref, u_ref, dg_ref, du_ref):
        dsv, gv, uv = ds_ref[...].astype(F32), g_ref[...].astype(F32), u_ref[...].astype(F32)
        sg = _sig(gv)
        dg_ref[...] = (dsv * uv * sg * (1.0 + gv * (1.0 - sg))).astype(BF16)
        du_ref[...] = (dsv * gv * sg).astype(BF16)

    return _swiglu_call(body, [ds, fg, fu], 2, name)


def _ple_final(x2, pg, pp, tgt, g_post, name):
    t, d = x2.shape

    def body(x_ref, pg_ref, pp_ref, t_ref, g_ref, loss_ref, d3_ref, dpg_ref, dpp_ref, dg_ref):
        sg, ppv, g = _sig(pg_ref[...]), pp_ref[...], g_ref[...]
        e = sg * ppv
        r = _rinv(e)
        eh = e * r
        diff = x_ref[...] + eh * g - t_ref[...]
        loss_ref[...] = 0.5 * jnp.mean(diff * diff, axis=-1, keepdims=True)
        d3 = diff * (1.0 / d)
        d3_ref[...] = d3
        gd = d3 * g
        de = r * (gd - eh * jnp.mean(gd * eh, axis=-1, keepdims=True))
        dpg_ref[...] = (de * ppv * sg * (1.0 - sg)).astype(BF16)
        dpp_ref[...] = (de * sg).astype(BF16)
        _acc(dg_ref, d3 * eh)

    return _rows(body, t, 128, [(x2, ("t", d, 0)), (pg, ("t", d, 0)), (pp, ("t", d, 0)), (tgt, ("t", d, 0)), (g_post, ("b",))],
                 [((t, 1), F32, "t"), ((t, d), F32, "t"), ((t, d), BF16, "t"), ((t, d), BF16, "t"), ((1, d), F32, "a")], name)


def _norm_bwd(dn, dh, x, g_pre, fm, g_post, name):
    t, d = x.shape
    two = fm is not None

    def body(*refs):
        if two:
            dn_ref, dh_ref, x_ref, gp_ref, f_ref, gq_ref, dx_ref, df_ref, dgp_ref, dgq_ref = refs
        else:
            dn_ref, dh_ref, x_ref, gp_ref, dx_ref, dgp_ref = refs
        xv, dhv = x_ref[...], dh_ref[...]
        r = _rinv(xv)
        xh = xv * r
        gd = dhv * gp_ref[...]
        dx = dn_ref[...] + r * (gd - xh * jnp.mean(gd * xh, axis=-1, keepdims=True))
        dx_ref[...] = dx
        _acc(dgp_ref, dhv * xh)
        if two:
            fv = f_ref[...]
            rf = _rinv(fv)
            fh = fv * rf
            gd2 = dx * gq_ref[...]
            df_ref[...] = (rf * (gd2 - fh * jnp.mean(gd2 * fh, axis=-1, keepdims=True))).astype(BF16)
            _acc(dgq_ref, dx * fh)

    ins = [(dn, ("t", d, 0)), (dh, ("t", d, 0)), (x, ("t", d, 0)), (g_pre, ("b",))]
    outs = [((t, d), F32, "t")]
    if two:
        ins += [(fm, ("t", d, 0)), (g_post, ("b",))]
        outs += [((t, d), BF16, "t"), ((1, d), F32, "a"), ((1, d), F32, "a")]
    else:
        outs += [((1, d), F32, "a")]
    return _rows(body, t, 128, ins, outs, name)


CONV_TC = 256


def _shift_down(v, s):
    rows = lax.broadcasted_iota(jnp.int32, v.shape, 0)
    return jnp.where(rows >= s, pltpu.roll(v, s, 0), 0.0)


def _shift_up(v, s):
    n = v.shape[0]
    rows = lax.broadcasted_iota(jnp.int32, v.shape, 0)
    return jnp.where(rows < n - s, pltpu.roll(v, n - s, 0), 0.0)


def _conv_specs(t):
    nb = 1024 // CONV_TC
    seg = lambda c0: pl.BlockSpec((t, CONV_TC), lambda j, cb=c0 // CONV_TC: (0, cb + j))
    own = pl.BlockSpec((t, CONV_TC), lambda j: (0, j))
    wspec = pl.BlockSpec((3, CONV_TC), lambda j: (0, j))
    return nb, seg, own, wspec


def _conv_fwd(proj, conv_w, name, after=()):
    t = proj.shape[0]
    nb, seg, own, wspec = _conv_specs(t)

    def body(ax_ref, ab_ref, ac_ref, w_ref, *rest):
        za_ref = rest[len(after)]
        u = ac_ref[...].astype(F32) * ax_ref[...].astype(F32)
        w = w_ref[...]
        yc = w[0:1] * _shift_down(u, 2) + w[1:2] * _shift_down(u, 1) + w[2:3] * u
        za_ref[...] = (ab_ref[...].astype(F32) * yc).astype(BF16)

    return pl.pallas_call(
        body, name=name, grid=(nb,), in_specs=[seg(C_AX), seg(C_AB), seg(C_AC), wspec] + [ANY] * len(after), out_specs=own,
        out_shape=jax.ShapeDtypeStruct((t, 1024), BF16), compiler_params=_params(("parallel",)),
    )(proj, proj, proj, conv_w, *after)


def _conv_bwd(dza, proj, conv_w, name):
    t = proj.shape[0]
    nb, seg, own, wspec = _conv_specs(t)

    def body(dz_ref, ax_ref, ab_ref, ac_ref, w_ref, dax_ref, dab_ref, dac_ref, dw_ref):
        ax, ab, ac, dz = ax_ref[...].astype(F32), ab_ref[...].astype(F32), ac_ref[...].astype(F32), dz_ref[...].astype(F32)
        w = w_ref[...]
        u = ac * ax
        u1, u2 = _shift_down(u, 1), _shift_down(u, 2)
        yc = w[0:1] * u2 + w[1:2] * u1 + w[2:3] * u
        dab_ref[...] = (dz * yc).astype(BF16)
        dyc = dz * ab
        du = w[2:3] * dyc + w[1:2] * _shift_up(dyc, 1) + w[0:1] * _shift_up(dyc, 2)
        dax_ref[...] = (du * ac).astype(BF16)
        dac_ref[...] = (du * ax).astype(BF16)
        dw_ref[0:1, :] = jnp.sum(dyc * u2, axis=0, keepdims=True)
        dw_ref[1:2, :] = jnp.sum(dyc * u1, axis=0, keepdims=True)
        dw_ref[2:3, :] = jnp.sum(dyc * u, axis=0, keepdims=True)

    act = jax.ShapeDtypeStruct((t, 1024), BF16)
    return pl.pallas_call(
        body, name=name, grid=(nb,), in_specs=[own, seg(C_AX), seg(C_AB), seg(C_AC), wspec], out_specs=[own, own, own, wspec],
        out_shape=[act, act, act, jax.ShapeDtypeStruct((3, 1024), F32)], compiler_params=_params(("parallel",)),
    )(dza, proj, proj, proj, conv_w)


def _dot(a, b, dims, precision=None):
    return lax.dot_general(a, b, (dims, ((), ())), precision=precision, preferred_element_type=F32)


_CONTRACT = {"nn": ((1,), (0,)), "nt": ((1,), (1,)), "tn": ((0,), (0,))}


def _bdot_raw(a, b, mode):
    return _dot(a.astype(BF16), b.astype(BF16), _CONTRACT[mode])


@functools.partial(jax.custom_vjp, nondiff_argnums=(2,))
def _bdot(a, b, mode):
    return _bdot_raw(a, b, mode)


def _bdot_fwd(a, b, mode):
    return _bdot_raw(a, b, mode), (a, b)


def _bdot_bwd(mode, res, ct):
    a, b = res
    if mode == "nn":
        return _bdot_raw(ct, b, "nt"), _bdot_raw(a, ct, "tn")
    if mode == "nt":
        return _bdot_raw(ct, b, "nn"), _bdot_raw(ct, a, "tn")
    return _bdot_raw(b, ct, "nt"), _bdot_raw(a, ct, "nn")


_bdot.defvjp(_bdot_fwd, _bdot_bwd)


@functools.partial(jax.custom_vjp, nondiff_argnums=(2,))
def _sum_dot(ones, x, mode):
    head = x.astype(BF16)
    tail = x - head.astype(F32)
    if mode == "nn":
        return _bdot_raw(ones, head, "nn") + _bdot_raw(ones, tail, "nn")
    return _bdot_raw(head, ones, "tn") + _bdot_raw(tail, ones, "tn")


def _sum_dot_fwd(ones, x, mode):
    return _sum_dot(ones, x, mode), ones


def _sum_dot_bwd(mode, ones, ct):
    return jnp.zeros_like(ones), (_bdot_raw(ones, ct, "tn") if mode == "nn" else _bdot_raw(ones, ct, "nt"))


_sum_dot.defvjp(_sum_dot_fwd, _sum_dot_bwd)


def _gla_chunk(q, k, v, og, alr, s_in, wa, ba, gain):
    c = q.shape[0]
    z =_bdot(alr, wa, "nn") + ba
    la = (jnp.minimum(z, 0.0) - jnp.log(1.0 + jnp.exp(-jnp.abs(z)))) * (1.0 / TAU)
    row = lax.broadcasted_iota(jnp.int32, (c, c), 0)
    col = lax.broadcasted_iota(jnp.int32, (c, c), 1)
    lower = row >= col
    b = _sum_dot(lower.astype(F32), la, "nn")
    trow = lax.broadcasted_iota(jnp.int32, la.shape, 0)
    mid = jnp.sum(jnp.where(trow <= c // 2, la, 0.0), axis=0, keepdims=True)
    blast = jnp.sum(la, axis=0, keepdims=True)
    qs = q * (DK ** -0.5)
    e_up, e_dn = jnp.exp(b - mid), jnp.exp(mid - b)
    a_fwd = _bdot(qs * e_up, k * e_dn, "nt")
    a_rev = _bdot(qs * e_dn, k * e_up, "nt")
    att = jnp.where(lower, a_fwd, a_rev)
    o = _bdot(att, v, "nn") + _bdot(qs * jnp.exp(b), s_in, "nn")
    upd = _bdot(k * jnp.exp(blast - b), v, "tn")
    blast_col = _sum_dot(jnp.ones((c, DV), F32), la, "tn")
    s_out = jnp.exp(blast_col) * s_in + upd
    on = o * _rinv(o) * gain
    return on * og * _sig(og), s_out


def _gla_specs(t, rev):
    n = t // STEP_ROWS
    ch = (lambda i: n - 1 - i) if rev else (lambda i: i)
    col = lambda w, c0: pl.BlockSpec((STEP_ROWS, HEADS * w), lambda i, cb=c0 // (HEADS * w): (ch(i), cb))
    whole = lambda shape: pl.BlockSpec(shape, lambda i, nd=len(shape): (0,) * nd)
    specs = dict(
        q=col(DK, C_Q), k=col(DK, C_K), v=col(DV, C_V), og=col(DV, C_OG),
        alr=pl.BlockSpec((STEP_ROWS, 128), lambda i: (ch(i), C_ALR // 128)),
        wa=whole((128, HEADS * DK)), ba=whole((1, HEADS * DK)), gain=whole((1, DV)),
        state=pl.BlockSpec((STEP_CHUNKS, HEADS, DK, DV), lambda i: (ch(i), 0, 0, 0)),
        odk=pl.BlockSpec((STEP_ROWS, HEADS * DK), lambda i: (ch(i), 0)), odv=pl.BlockSpec((STEP_ROWS, HEADS * DV), lambda i: (ch(i), 0)),
        oalr=pl.BlockSpec((STEP_ROWS, 128), lambda i: (ch(i), 0)), whole=whole,
    )
    return n, specs


def _head_cols(h):
    return slice(h * DK, (h + 1) * DK), slice(h * DV, (h + 1) * DV)


def _gla_fwd(proj, wa, ba, gain, name):
    t = proj.shape[0]
    n, sp = _gla_specs(t, False)

    def body(q_ref, k_ref, v_ref, og_ref, alr_ref, wa_ref, ba_ref, g_ref, zb_ref, st_ref, s_scr):
        @pl.when(pl.program_id(0) == 0)
        def _():
            s_scr[...] = jnp.zeros_like(s_scr)

        state = [s_scr[h] for h in range(HEADS)]
        for c in range(STEP_CHUNKS):
            rows = slice(c * CHUNK, (c + 1) * CHUNK)
            alr = alr_ref[rows, :].astype(F32)
            for h in range(HEADS):
                kc, vc = _head_cols(h)
                st_ref[c, h] = state[h]
                zb, state[h] = _gla_chunk(q_ref[rows, kc].astype(F32), k_ref[rows, kc].astype(F32), v_ref[rows, vc].astype(F32),
                                          og_ref[rows, vc].astype(F32), alr, state[h], wa_ref[:, kc].astype(F32), ba_ref[:, kc], g_ref[...])
                zb_ref[rows, vc] = zb.astype(BF16)
        for h in range(HEADS):
            s_scr[h] = state[h]

    return pl.pallas_call(
        body, name=name, grid=(n,),
        in_specs=[sp["q"], sp["k"], sp["v"], sp["og"], sp["alr"], sp["wa"], sp["ba"], sp["gain"]],
        out_specs=[sp["odv"], sp["state"]],
        out_shape=[jax.ShapeDtypeStruct((t, HEADS * DV), BF16), jax.ShapeDtypeStruct((t // CHUNK, HEADS, DK, DV), F32)],
        scratch_shapes=[pltpu.VMEM((HEADS, DK, DV), F32)],
        compiler_params=_params(("arbitrary",)),
    )(proj, proj, proj, proj, proj, wa, ba, gain)


def _gla_bwd(dzb, proj, states, wa, ba, gain, name):
    t = proj.shape[0]
    n, sp = _gla_specs(t, True)

    def body(dz_ref, q_ref, k_ref, v_ref, og_ref, alr_ref, st_ref, wa_ref, ba_ref, g_ref,
             dq_ref, dk_ref, dv_ref, dog_ref, dalr_ref, dwa_ref, dba_ref, dg_ref, ds_scr):
        @pl.when(pl.program_id(0) == 0)
        def _():
            ds_scr[...] = jnp.zeros_like(ds_scr)
            dwa_ref[...] = jnp.zeros_like(dwa_ref)
            dba_ref[...] = jnp.zeros_like(dba_ref)
            dg_ref[...] = jnp.zeros_like(dg_ref)

        dstate = [ds_scr[h] for h in range(HEADS)]
        dwa_sum, dba_sum, dgain_sum = [None] * HEADS, [None] * HEADS, None
        for c in reversed(range(STEP_CHUNKS)):
            rows = slice(c * CHUNK, (c + 1) * CHUNK)
            alr = alr_ref[rows, :].astype(F32)
            dalr_sum = None
            for h in range(HEADS):
                kc, vc = _head_cols(h)
                args = (q_ref[rows, kc].astype(F32), k_ref[rows, kc].astype(F32), v_ref[rows, vc].astype(F32), og_ref[rows, vc].astype(F32),
                        alr, st_ref[c, h], wa_ref[:, kc].astype(F32), ba_ref[:, kc], g_ref[...])
                _, vjp = jax.vjp(_gla_chunk, *args)
                dq, dk, dv, dog, dalr, dstate[h], dwa, dba, dgain = vjp((dz_ref[rows, vc].astype(F32), dstate[h]))
                dq_ref[rows, kc] = dq.astype(BF16)
                dk_ref[rows, kc] = dk.astype(BF16)
                dv_ref[rows, vc] = dv.astype(BF16)
                dog_ref[rows, vc] = dog.astype(BF16)
                dwa_sum[h] = dwa if dwa_sum[h] is None else dwa_sum[h] + dwa
                dba_sum[h] = dba if dba_sum[h] is None else dba_sum[h] + dba
                dalr_sum = dalr if dalr_sum is None else dalr_sum + dalr
                dgain_sum = dgain if dgain_sum is None else dgain_sum + dgain
            dalr_ref[rows, :] = dalr_sum
        for h in range(HEADS):
            ds_scr[h] = dstate[h]
            dwa_ref[h] += dwa_sum[h]
            dba_ref[h] += dba_sum[h]
        dg_ref[...] += dgain_sum

    whole = sp["whole"]
    return pl.pallas_call(
        body, name=name, grid=(n,),
        in_specs=[sp["odv"], sp["q"], sp["k"], sp["v"], sp["og"], sp["alr"], sp["state"], sp["wa"], sp["ba"], sp["gain"]],
        out_specs=[sp["odk"], sp["odk"], sp["odv"], sp["odv"], sp["oalr"], whole((HEADS, 128, DK)), whole((HEADS, 1, DK)), whole((1, DV))],
        out_shape=[jax.ShapeDtypeStruct((t, HEADS * DK), BF16), jax.ShapeDtypeStruct((t, HEADS * DK), BF16),
                   jax.ShapeDtypeStruct((t, HEADS * DV), BF16), jax.ShapeDtypeStruct((t, HEADS * DV), BF16),
                   jax.ShapeDtypeStruct((t, 128), F32), jax.ShapeDtypeStruct((HEADS, 128, DK), F32),
                   jax.ShapeDtypeStruct((HEADS, 1, DK), F32), jax.ShapeDtypeStruct((1, DV), F32)],
        scratch_shapes=[pltpu.VMEM((HEADS, DK, DV), F32)],
        compiler_params=_params(("arbitrary",)),
    )(dzb, proj, proj, proj, proj, proj, states, wa, ba, gain)


def _local_step(x, p, tgt, gather_start, gather_pass_on, gather_finish, scatter_start, small):
    b_alpha, gain = small["b_alpha_up"], small["gla_head_gain"]
    gather_start(0, ())
    w = dict(gather_finish(0, ()))
    conv_w, w_alpha = w["conv_w"], w["w_alpha_up"]
    wa_p = jnp.zeros((128, HEADS * DK), BF16).at[:GATE_RANK].set(w_alpha.astype(BF16))

    t2 = gather_start(2, (w["in_rest_t"], gather_start(1, ())))
    h1 = _rms_fwd(x, small["g_pre_mix"], "rms_pre_mix")
    proj = _mm(h1, w["in_rest_t"], "nt", BF16, "mm_proj", after=(t2,))
    proj_gates = _mm(h1, w["in_gates_t"], "nt", BF16, "mm_proj_gates", after=(t2,))
    za = _conv_fwd(proj, conv_w, "conv_fwd", after=(gather_pass_on(1, (proj,)),))
    zb, states = _gla_fwd(proj, wa_p, b_alpha, gain, "gla_fwd")
    t3 = gather_start(3, (zb, za))
    w.update(gather_finish(1, (t3,)))
    ya = _mm(za, w["a_out"], "nn", BF16, "mm_ya")
    yb = _mm(zb, w["b_out"], "nn", BF16, "mm_yb")
    mix = _mix_fwd(proj_gates, ya, yb, "mix_fwd")
    m2 = _mm(mix, w["mix"], "nn", F32, "mm_mix")
    t4 = gather_start(4, (m2,))
    x1, h2 = _post_pre(x, m2, small["g_post_mix"], small["g_pre_ffn"], "norm_mix_ffn", after=(gather_pass_on(2, (m2,)),))
    w.update(gather_finish(2, (h2, t4)))
    fu = _mm(h2, w["up_t"], "nt", BF16, "mm_up")
    fg = _mm(h2, w["gate_t"], "nt", BF16, "mm_gate", after=(gather_pass_on(3, (fu,)),))
    s = _swiglu_fwd(fg, fu, "swiglu_fwd")
    w.update(gather_finish(3, (s,)))
    f = _mm(s, w["down"], "nn", F32, "mm_down", after=(gather_pass_on(4, (s,)),))
    x2, h3 = _post_pre(x1, f, small["g_post_ffn"], small["g_pre_ple"], "norm_ffn_ple")
    w.update(gather_finish(4, (h3,)))
    pg = _mm(h3, w["pg"], "nn", F32, "mm_pg")
    p_bf = p.astype(BF16)
    pp = _mm(p_bf, w["pp"], "nn", F32, "mm_pp", b3=True, tm=2048)
    loss_rows, d3, dpg, dpp, dg_post_ple = _ple_final(x2, pg, pp, tgt, small["g_post_ple"], "ple_final")

    gw = {}
    gw["pp"] = _mm(p_bf, dpp, "tn", BF16, "mm_dw_pp", out3=True)
    gw["pg"] = _mm(h3, dpg, "tn", BF16, "mm_dw_pg")
    dh3 = _mm(dpg, w["pg"], "nt", F32, "mm_dh3", after=(scatter_start(["w_ple_proj", "w_ple_gate"], gw),))
    d2, df, dg_pre_ple, dg_post_ffn = _norm_bwd(d3, dh3, x2, small["g_pre_ple"], f, small["g_post_ffn"], "norm_bwd_ple_ffn")
    gw["down"] = _mm(s, df, "tn", BF16, "mm_dw_down", tm=1408)
    ds = _mm(df, w["down"], "nt", BF16, "mm_ds", after=(scatter_start(["w_ff_down"], gw),))
    dfg, dfu = _swiglu_bwd(ds, fg, fu, "swiglu_bwd")
    gw["gate_t"] = _mm(dfg, h2, "tn", BF16, "mm_dw_gate", tm=1408)
    gw["up_t"] = _mm(dfu, h2, "tn", BF16, "mm_dw_up", after=(gw["gate_t"],), tm=1408)
    dh2 = _mm(dfg, w["gate_t"], "nn", F32, "mm_dh2_gate", after=(scatter_start(["w_ff_gate", "w_ff_up"], gw),))
    dh2 = _mm(dfu, w["up_t"], "nn", F32, "mm_dh2_up", add=dh2, tm=1024)
    d1, dm2, dg_pre_ffn, dg_post_mix = _norm_bwd(d2, dh2, x1, small["g_pre_ffn"], m2, small["g_post_mix"], "norm_bwd_ffn_mix")
    dmix = _mm(dm2, w["mix"], "nt", F32, "mm_dmix")
    gw["mix"] = _mm(mix, dm2, "tn", BF16, "mm_dw_mix")
    dgab, dya, dyb = _mix_bwd(dmix, proj_gates, ya, yb, "mix_bwd")
    gw["in_gates_t"] = _mm(dgab, h1, "tn", BF16, "mm_dw_in_gates")
    dza = _mm(dya, w["a_out"], "nt", BF16, "mm_dza", after=(scatter_start(["w_mix_out"], gw), gw["in_gates_t"]))
    gw["a_out"] = _mm(za, dya, "tn", BF16, "mm_dw_a_out")
    gw["b_out"] = _mm(zb, dyb, "tn", BF16, "mm_dw_b_out", after=(gw["a_out"],))
    dzb = _mm(dyb, w["b_out"], "nt", BF16, "mm_dzb", after=(scatter_start(["w_a_out", "w_b_out"], gw),))
    dax, dab, dac, dconv = _conv_bwd(dza, proj, conv_w, "conv_bwd")
    dq, dk, dv, dog, dalr, dwa, dba, dgain = _gla_bwd(dzb, proj, states, wa_p, b_alpha, gain, "gla_bwd")
    drest = jnp.concatenate([dax, dab, dac, dq, dk, dv, dog, dalr.astype(BF16)], axis=1)
    gw["in_rest_t"] = _mm(drest, h1, "tn", BF16, "mm_dw_in_rest", tm=896)
    dh1 = _mm(dgab, w["in_gates_t"], "nn", F32, "mm_dh1_gates", after=(scatter_start(["w_in"], gw),), tm=1024)
    dh1 = _mm(drest, w["in_rest_t"], "nn", F32, "mm_dh1_rest", add=dh1, tm=1024)
    grad_x, dg_pre_mix = _norm_bwd(d1, dh1, x, small["g_pre_mix"], None, None, "norm_bwd_mix")

    gs = dict(
        conv_w=dconv,
        w_alpha_up=jnp.transpose(dwa[:, :GATE_RANK, :], (1, 0, 2)).reshape(GATE_RANK, HEADS * DK),
        b_alpha_up=dba.reshape(1, HEADS * DK), gla_head_gain=dgain,
        g_pre_mix=dg_pre_mix, g_post_mix=dg_post_mix, g_pre_ffn=dg_pre_ffn, g_post_ffn=dg_post_ffn,
        g_pre_ple=dg_pre_ple, g_post_ple=dg_post_ple,
    )
    return loss_rows, grad_x, gs


def _place():
    x, y, c = lax.axis_index("x"), lax.axis_index("y"), lax.axis_index("c")
    return x, y, c, [(1 - x, y), (x, 1 - y), (1 - x, 1 - y)]


def _all_gather(shards, name, cid=None):
    n = len(shards)

    def body(*refs):
        ins, outs = refs[:n], refs[n:2 * n]
        send_sems, recv_sems, local_sems = refs[2 * n:]
        x, y, c, chips = _place()
        me, sibling = (x, y, c), (x, y, 1 - c)

        def slot(px, py, pc):
            return 4 * px + 2 * py + pc

        def copy(a, k, block, to, src=None):
            dst = outs[a].at[slot(*block)]
            return pltpu.make_async_remote_copy(src_ref=dst if src is None else src, dst_ref=dst, send_sem=send_sems.at[a, k],
                                                recv_sem=recv_sems.at[a, k], device_id=to, device_id_type=MESH)

        mine = [pltpu.make_async_copy(ins[a], outs[a].at[slot(*me)], local_sems.at[a]) for a in range(n)]
        for cp in mine:
            cp.start()
        first = []
        for j, chip in enumerate(chips):
            first += [copy(a, 1 + j, me, (*chip, c), src=ins[a]) for a in range(n)]
        first += [copy(a, 0, me, sibling, src=ins[a]) for a in range(n)]
        for cp in first:
            cp.start()
        passed = []
        for j, chip in enumerate(chips):
            for a in range(n):
                copy(a, 1 + j, (*chip, c), me).wait_recv()
                cp = copy(a, 4 + j, (*chip, c), sibling)
                cp.start()
                passed.append(cp)
        for a in range(n):
            copy(a, 0, sibling, me).wait_recv()
        for j, chip in enumerate(chips):
            for a in range(n):
                copy(a, 4 + j, (*chip, 1 - c), me).wait_recv()
        for cp in first + passed:
            cp.wait_send()
        for cp in mine:
            cp.wait()

    if cid is None:
        return pl.pallas_call(
            body, name=name, in_specs=[ANY] * n, out_specs=[ANY] * n,
            out_shape=[jax.ShapeDtypeStruct((N_DEV,) + s.shape, s.dtype) for s in shards],
            scratch_shapes=[pltpu.SemaphoreType.DMA((n, 7)), pltpu.SemaphoreType.DMA((n, 7)), pltpu.SemaphoreType.DMA((n,))],
        )(*shards)

    src = [jax.new_ref(s, memory_space=pltpu.MemorySpace.HBM) for s in shards]
    dst = [jax.empty_ref(jax.ShapeDtypeStruct((N_DEV,) + s.shape, s.dtype), memory_space=pltpu.MemorySpace.HBM) for s in shards]

    @pl.kernel(mesh=plsc.ScalarSubcoreMesh(axis_name="seq", num_cores=1), name=name,
               scratch_types=(pltpu.SemaphoreType.DMA((n, 7)), pltpu.SemaphoreType.DMA((n, 7)), pltpu.SemaphoreType.DMA((n,))),
               compiler_params=pltpu.CompilerParams(collective_id=cid))
    def launch(send_sems, recv_sems, local_sems):
        x, y, c, chips = _place()
        barrier = pltpu.get_barrier_semaphore()
        for peer in [(x, y, 1 - c)] + [(*chip, c) for chip in chips]:
            pl.semaphore_signal(barrier, inc=1, device_id=peer, device_id_type=MESH)
        pl.semaphore_wait(barrier, 4)
        body(*src, *dst, send_sems, recv_sems, local_sems)

    launch()
    return [r[...] for r in dst]


def _reduce_scatter(parts, name, cid):
    n = len(parts)
    src = [jax.new_ref(s, memory_space=pltpu.MemorySpace.HBM) for s in parts]
    dst = [jax.empty_ref(jax.ShapeDtypeStruct(s.shape, s.dtype), memory_space=pltpu.MemorySpace.HBM) for s in parts]

    @pl.kernel(mesh=plsc.ScalarSubcoreMesh(axis_name="seq", num_cores=1), name=name,
               scratch_types=(pltpu.SemaphoreType.DMA((n, N_DEV - 1)), pltpu.SemaphoreType.DMA((n, N_DEV - 1)), pltpu.SemaphoreType.DMA((n,))),
               compiler_params=pltpu.CompilerParams(collective_id=cid))
    def launch(send_sems, recv_sems, local_sems):
        x, y, c, _ = _place()
        me = 4 * x + 2 * y + c
        peers = [(1 - x if k & 4 else x, 1 - y if k & 2 else y, 1 - c if k & 1 else c) for k in range(1, N_DEV)]
        barrier = pltpu.get_barrier_semaphore()
        for peer in peers:
            pl.semaphore_signal(barrier, inc=1, device_id=peer, device_id_type=MESH)
        pl.semaphore_wait(barrier, N_DEV - 1)
        mine = [pltpu.make_async_copy(src[a].at[me], dst[a].at[me], local_sems.at[a]) for a in range(n)]
        for cp in mine:
            cp.start()
        cps = []
        for a in range(n):
            for k, (px, py, pc) in enumerate(peers):
                cps.append(pltpu.make_async_remote_copy(src_ref=src[a].at[4 * px + 2 * py + pc], dst_ref=dst[a].at[me], send_sem=send_sems.at[a, k],
                                                        recv_sem=recv_sems.at[a, k], device_id=(px, py, pc), device_id_type=MESH))
        for cp in cps:
            cp.start()
        for cp in cps:
            cp.wait_recv()
        for cp in cps:
            cp.wait_send()
        for cp in mine:
            cp.wait()

    launch()
    return [r[...] for r in dst]


def _sibling_exchange(parts, name):
    n = len(parts)
    pieces = [_row_pieces(s.shape[1]) for s in parts]

    def body(*refs):
        ins, outs = refs[:n], refs[n:2 * n]
        send_sems, recv_sems = refs[2 * n:]
        x, y, c, _ = _place()

        def copy(a, ch, q, rows):
            return pltpu.make_async_remote_copy(src_ref=ins[a].at[2 * ch + 1 - c, rows], dst_ref=outs[a].at[ch, rows], send_sem=send_sems.at[a, ch, q],
                                                recv_sem=recv_sems.at[a, ch, q], device_id=(x, y, 1 - c), device_id_type=MESH)

        cps = [copy(a, ch, q, rows) for ch in range(4) for a in range(n) for q, rows in enumerate(pieces[a])]
        for cp in cps:
            cp.start()
        for cp in cps:
            cp.wait_recv()
        for cp in cps:
            cp.wait_send()

    return pl.pallas_call(
        body, name=name, in_specs=[ANY] * n, out_specs=[ANY] * n,
        out_shape=[jax.ShapeDtypeStruct((4,) + s.shape[1:], s.dtype) for s in parts],
        scratch_shapes=[pltpu.SemaphoreType.DMA((n, 4, PIECES)), pltpu.SemaphoreType.DMA((n, 4, PIECES))],
    )(*parts)


def _chip_exchange(parts, name):
    n = len(parts)

    def body(*refs):
        ins, outs = refs[:n], refs[n:2 * n]
        send_sems, recv_sems, local_sems = refs[2 * n:]
        x, y, c, chips = _place()
        my_chip = 2 * x + y

        def copy(a, j):
            px, py = chips[j]
            return pltpu.make_async_remote_copy(src_ref=ins[a].at[2 * px + py], dst_ref=outs[a].at[my_chip], send_sem=send_sems.at[a, j],
                                                recv_sem=recv_sems.at[a, j], device_id=(px, py, c), device_id_type=MESH)

        def landing(a, j):
            px, py = chips[j]
            return pltpu.make_async_remote_copy(src_ref=ins[a].at[my_chip], dst_ref=outs[a].at[2 * px + py], send_sem=send_sems.at[a, j],
                                                recv_sem=recv_sems.at[a, j], device_id=(px, py, c), device_id_type=MESH)

        mine = [pltpu.make_async_copy(ins[a].at[my_chip], outs[a].at[my_chip], local_sems.at[a]) for a in range(n)]
        for cp in mine:
            cp.start()
        cps = [copy(a, j) for j in range(3) for a in range(n)]
        for cp in cps:
            cp.start()
        for j in range(3):
            for a in range(n):
                landing(a, j).wait_recv()
        for cp in cps:
            cp.wait_send()
        for cp in mine:
            cp.wait()

    return pl.pallas_call(
        body, name=name, in_specs=[ANY] * n, out_specs=[ANY] * n,
        out_shape=[jax.ShapeDtypeStruct(s.shape, s.dtype) for s in parts],
        scratch_shapes=[pltpu.SemaphoreType.DMA((n, 3)), pltpu.SemaphoreType.DMA((n, 3)), pltpu.SemaphoreType.DMA((n,))],
    )(*parts)


def _pair_add(mine8, got4, name):
    _, r, cols = mine8.shape
    steps, blk, at = _tiles(r, cols)
    core = lax.axis_index("c").astype(jnp.int32).reshape(1)

    def body(c_ref, a_ref, b_ref, o_ref):
        o_ref[...] = (a_ref[...].astype(F32) + b_ref[...].astype(F32)).astype(BF16)

    return pl.pallas_call(
        body, name=name,
        grid_spec=pltpu.PrefetchScalarGridSpec(
            num_scalar_prefetch=1, grid=(4, steps),
            in_specs=[pl.BlockSpec((None,) + blk, lambda ch, i, c_ref: (2 * ch + c_ref[0],) + at(i)),
                      pl.BlockSpec((None,) + blk, lambda ch, i, c_ref: (ch,) + at(i))],
            out_specs=pl.BlockSpec((None,) + blk, lambda ch, i, c_ref: (ch,) + at(i))),
        out_shape=jax.ShapeDtypeStruct((4, r, cols), BF16),
        compiler_params=_params(("parallel", "parallel")),
    )(core, mine8, got4)


HBM = pl.BlockSpec(memory_space=pltpu.HBM)
SEM = pl.BlockSpec(memory_space=pltpu.SEMAPHORE)
EFFECT = pltpu.SideEffectType.DATAFLOW_SIDE_EFFECTING


def _in_hbm(a):
    return pltpu.with_memory_space_constraint(a, pltpu.HBM)


def _remote_copies(plan, srcs, lands, send_sems, recv_sems):
    return [pltpu.make_async_remote_copy(src_ref=s, dst_ref=d, send_sem=send_sems.at[i], recv_sem=recv_sems.at[i], device_id=peer,
                                         device_id_type=MESH) for i, (s, d, peer) in enumerate(plan(srcs, lands))]


def _copies_start(plan, n_copies, srcs, land_shapes, name, after=()):
    ns, nl = len(srcs), len(land_shapes)

    def body(*refs):
        send_sems, recv_sems = refs[ns + nl + len(after):ns + nl + len(after) + 2]
        for cp in _remote_copies(plan, refs[:ns], refs[ns:ns + nl], send_sems, recv_sems):
            cp.start()
        refs[-1][...] = jnp.zeros((8, 128), F32)

    sems = pltpu.SemaphoreType.DMA((n_copies,))
    return pl.pallas_call(
        body, name=name,
        out_shape=(sems, sems, *[pltpu.HBM(s.shape, s.dtype) for s in srcs], *[pltpu.HBM(s.shape, s.dtype) for s in land_shapes],
                   jax.ShapeDtypeStruct((8, 128), F32)),
        in_specs=[HBM] * (ns + nl) + [ANY] * len(after),
        out_specs=(SEM, SEM, *[HBM] * (ns + nl), pl.BlockSpec(memory_space=pltpu.VMEM)),
        input_output_aliases={i: 2 + i for i in range(ns + nl)},
        compiler_params=pltpu.CompilerParams(has_side_effects=EFFECT),
    )(*[_in_hbm(s) for s in srcs], *[_in_hbm(lax.empty(s.shape, s.dtype)) for s in land_shapes], *after)


def _copies_wait(plan, state, ns, name, after=()):
    send_sems, recv_sems, *arrs = state[:-1]
    n = len(arrs)

    def body(*refs):
        cps = _remote_copies(plan, refs[:ns], refs[ns:n], refs[n], refs[n + 1])
        for cp in cps:
            cp.wait_send()
        for cp in cps:
            cp.wait_recv()

    out = pl.pallas_call(
        body, name=name, out_shape=tuple(pltpu.HBM(a.shape, a.dtype) for a in arrs),
        in_specs=[HBM] * n + [SEM, SEM] + [ANY] * len(after), out_specs=tuple([HBM] * n),
        input_output_aliases={i: i for i in range(n)},
        compiler_params=pltpu.CompilerParams(has_side_effects=EFFECT),
    )(*arrs, send_sems, recv_sems, *after)
    return list(out[:ns]), list(out[ns:])


def _copies_relay(plan, state, ns, next_plan, n_next, name, after=()):
    send_sems, recv_sems, *arrs = state[:-1]
    n = len(arrs)

    def body(*refs):
        cps = _remote_copies(plan, refs[:ns], refs[ns:n], refs[n], refs[n + 1])
        for cp in cps:
            cp.wait_send()
        for cp in cps:
            cp.wait_recv()
        outs = refs[n + 2 + len(after):]
        for cp in _remote_copies(next_plan, refs[:ns], refs[ns:n], outs[0], outs[1]):
            cp.start()
        outs[-1][...] = jnp.zeros((8, 128), F32)

    sems = pltpu.SemaphoreType.DMA((n_next,))
    return pl.pallas_call(
        body, name=name,
        out_shape=(sems, sems, *[pltpu.HBM(a.shape, a.dtype) for a in arrs], jax.ShapeDtypeStruct((8, 128), F32)),
        in_specs=[HBM] * n + [SEM, SEM] + [ANY] * len(after),
        out_specs=(SEM, SEM, *[HBM] * n, pl.BlockSpec(memory_space=pltpu.VMEM)),
        input_output_aliases={i: 2 + i for i in range(n)},
        compiler_params=pltpu.CompilerParams(has_side_effects=EFFECT),
    )(*arrs, send_sems, recv_sems, *after)


def _pass_on_plan(srcs, lands):
    x, y, c, chips = _place()
    return [(l.at[4 * px + 2 * py + c], l.at[4 * px + 2 * py + c], (x, y, 1 - c)) for l in lands for px, py in chips]


def _gather_plan(srcs, lands):
    x, y, c, chips = _place()
    peers = [(x, y, 1 - c)] + [(*chip, c) for chip in chips]
    return [(s, l.at[4 * x + 2 * y + c], peer) for s, l in zip(srcs, lands) for peer in peers]


def _gather_plan_near(srcs, lands):
    x, y, c, _ = _place()
    peers = [(x, y, 1 - c), (1 - x, y, c), (x, 1 - y, c)]
    return [(s, l.at[4 * x + 2 * y + c], peer) for s, l in zip(srcs, lands) for peer in peers]


def _scatter_plan(srcs, lands):
    x, y, c, _ = _place()
    peers = [(1 - x if k & 4 else x, 1 - y if k & 2 else y, 1 - c if k & 1 else c) for k in range(1, N_DEV)]
    return [(s.at[4 * px + 2 * py + pc], l.at[4 * x + 2 * y + c], (px, py, pc)) for s, l in zip(srcs, lands) for px, py, pc in peers]


def _everyone_plan(srcs, lands):
    x, y, c, _ = _place()
    peers = [(1 - x if k & 4 else x, 1 - y if k & 2 else y, 1 - c if k & 1 else c) for k in range(1, N_DEV)]
    return [(s, l.at[4 * x + 2 * y + c], peer) for s, l in zip(srcs, lands) for peer in peers]


def _sum_parts(got, own, me, name):
    def body(me_ref, got_ref, own_ref, o_ref):
        acc = jnp.where(me_ref[0] == 0, own_ref[...], got_ref[0])
        for d in range(1, N_DEV):
            acc = acc + jnp.where(me_ref[0] == d, own_ref[...], got_ref[d])
        o_ref[...] = acc

    return pl.pallas_call(
        body, name=name,
        grid_spec=pltpu.PrefetchScalarGridSpec(
            num_scalar_prefetch=1, grid=(1,),
            in_specs=[pl.BlockSpec(got.shape, lambda i, me_ref: (0, 0, 0)), pl.BlockSpec(own.shape, lambda i, me_ref: (0, 0))],
            out_specs=pl.BlockSpec(own.shape, lambda i, me_ref: (0, 0))),
        out_shape=jax.ShapeDtypeStruct(own.shape, F32),
    )(me.astype(jnp.int32).reshape(1), got, own)


def _chip_plan(srcs, lands):
    x, y, c, chips = _place()
    return [(s.at[2 * px + py], l.at[2 * x + y], (px, py, c)) for s, l in zip(srcs, lands) for px, py in chips]


PIECES = 8


def _row_pieces(rows):
    for k in (PIECES, 4, 2):
        if rows % (16 * k) == 0:
            return [pl.ds(q * (rows // k), rows // k) for q in range(k)]
    return [pl.ds(0, rows)]


def _put_own(shard, zone, me, name):
    r, c = shard.shape
    tr = r if r <= 256 else _pick(r, (256, 64))

    def body(me_ref, s_ref, z_ref, o_ref):
        o_ref[...] = s_ref[...]

    return pl.pallas_call(
        body, name=name,
        grid_spec=pltpu.PrefetchScalarGridSpec(
            num_scalar_prefetch=1, grid=(r // tr,),
            in_specs=[pl.BlockSpec((tr, c), lambda i, me_ref: (i, 0)), ANY],
            out_specs=pl.BlockSpec((None, tr, c), lambda i, me_ref: (me_ref[0], i, 0))),
        out_shape=jax.ShapeDtypeStruct(zone.shape, zone.dtype), input_output_aliases={2: 0},
        compiler_params=_params(("arbitrary",)),
    )(me.astype(jnp.int32).reshape(1), shard, zone)


def _gather_finish(lands, name):
    n = len(lands)

    def body(*refs):
        zones, outs = refs[:n], refs[n:2 * n]
        send_sems, recv_sems = refs[2 * n:]
        x, y, c, chips = _place()
        cps = [pltpu.make_async_remote_copy(
            src_ref=zones[a].at[4 * px + 2 * py + c], dst_ref=outs[a].at[4 * px + 2 * py + c], send_sem=send_sems.at[a, j],
            recv_sem=recv_sems.at[a, j], device_id=(x, y, 1 - c), device_id_type=MESH) for j, (px, py) in enumerate(chips) for a in range(n)]
        for cp in cps:
            cp.start()
        for cp in cps:
            cp.wait_recv()
        for cp in cps:
            cp.wait_send()

    return pl.pallas_call(
        body, name=name, in_specs=[ANY] * n, out_specs=[ANY] * n,
        out_shape=[jax.ShapeDtypeStruct(l.shape, l.dtype) for l in lands],
        input_output_aliases={a: a for a in range(n)},
        scratch_shapes=[pltpu.SemaphoreType.DMA((n, 3)), pltpu.SemaphoreType.DMA((n, 3))],
    )(*lands)


def _gather_relay(lands, name):
    n = len(lands)

    def body(*refs):
        zones, outs = refs[:n], refs[n:2 * n]
        send_sems, recv_sems = refs[2 * n:]
        x, y, c, _ = _place()
        south = c == 0
        near_x, near_y, across = 4 * (1 - x) + 2 * y + c, 4 * x + 2 * (1 - y) + c, 4 * (1 - x) + 2 * (1 - y) + c
        passed = jnp.where(south, near_y, near_x)
        onward = (jnp.where(south, 1 - x, x), jnp.where(south, y, 1 - y), c)

        def copy(a, k, slot, to):
            return pltpu.make_async_remote_copy(src_ref=zones[a].at[slot], dst_ref=outs[a].at[slot], send_sem=send_sems.at[a, k],
                                                recv_sem=recv_sems.at[a, k], device_id=to, device_id_type=MESH)

        first = [copy(a, 0, passed, onward) for a in range(n)]
        first += [copy(a, 1 + j, slot, (x, y, 1 - c)) for j, slot in enumerate((near_x, near_y)) for a in range(n)]
        for cp in first:
            cp.start()
        last = []
        for a in range(n):
            copy(a, 0, across, onward).wait_recv()
            last.append(copy(a, 3, across, (x, y, 1 - c)))
            last[-1].start()
        for cp in first[n:] + last:
            cp.wait_recv()
        for cp in first + last:
            cp.wait_send()

    return pl.pallas_call(
        body, name=name, in_specs=[ANY] * n, out_specs=[ANY] * n,
        out_shape=[jax.ShapeDtypeStruct(l.shape, l.dtype) for l in lands],
        input_output_aliases={a: a for a in range(n)},
        scratch_shapes=[pltpu.SemaphoreType.DMA((n, 4)), pltpu.SemaphoreType.DMA((n, 4))],
    )(*lands)


def _sum_everywhere(v, name):
    rows = v.shape[0]

    def body(v_ref, o_ref, buf, send_sems, recv_sems):
        x, y, c, _ = _place()
        me = 4 * x + 2 * y + c
        buf[me] = v_ref[...]
        cps = []
        for k in range(1, N_DEV):
            fx, fy, fc = (k >> 2) & 1, (k >> 1) & 1, k & 1
            to = (1 - x if fx else x, 1 - y if fy else y, 1 - c if fc else c)
            cps.append(pltpu.make_async_remote_copy(src_ref=buf.at[me], dst_ref=buf.at[me], send_sem=send_sems.at[k - 1],
                                                    recv_sem=recv_sems.at[k - 1], device_id=to, device_id_type=MESH))
        for cp in cps:
            cp.start()
        for cp in cps:
            cp.wait_recv()
        for cp in cps:
            cp.wait_send()
        acc = buf[0]
        for d in range(1, N_DEV):
            acc = acc + buf[d]
        o_ref[...] = acc

    vm = pl.BlockSpec(memory_space=pltpu.VMEM)
    return pl.pallas_call(
        body, name=name, in_specs=[vm], out_specs=vm, out_shape=jax.ShapeDtypeStruct(v.shape, F32),
        scratch_shapes=[pltpu.VMEM((N_DEV, rows, 128), F32), pltpu.SemaphoreType.DMA((N_DEV - 1,)), pltpu.SemaphoreType.DMA((N_DEV - 1,))],
    )(v)


def _adamw_parts(w, got, mine, me, m, v, name, after=()):
    r, c = w.shape
    n_parts = got.shape[0]
    steps, blk, at = _tiles(r, c)

    def body(me_ref, w_ref, got_ref, own_ref, m_ref, v_ref, *rest):
        go_ref, d_ref, mo_ref, vo_ref = rest[len(after):]
        own = own_ref[...].astype(F32)
        gv = jnp.where(me_ref[0] == 0, own, got_ref[0].astype(F32))
        for d in range(1, n_parts):
            gv = gv + jnp.where(me_ref[0] == d, own, got_ref[d].astype(F32))
        _adamw_math(gv, w_ref, m_ref, v_ref, go_ref, d_ref, mo_ref, vo_ref)

    tile = pl.BlockSpec(blk, lambda i, me_ref: at(i))
    out = jax.ShapeDtypeStruct((r, c), F32)
    return pl.pallas_call(
        body, name=name,
        grid_spec=pltpu.PrefetchScalarGridSpec(
            num_scalar_prefetch=1, grid=(steps,),
            in_specs=[tile, pl.BlockSpec((n_parts,) + blk, lambda i, me_ref: (0,) + at(i)),
                      pl.BlockSpec((None,) + blk, lambda i, me_ref: (me_ref[0],) + at(i)), tile, tile] + [ANY] * len(after),
            out_specs=[tile] * 4),
        out_shape=[out] * 4, compiler_params=_params(("parallel",)),
    )(me.astype(jnp.int32).reshape(1), w, got, mine, m, v, *after)


def _adamw_math(gv, w_ref, m_ref, v_ref, go_ref, d_ref, mo_ref, vo_ref):
    mn = B1 * m_ref[...] + (1.0 - B1) * gv
    vn = B2 * v_ref[...] + (1.0 - B2) * (gv * gv)
    m_hat = mn / (1.0 - B1 ** STEP)
    v_hat = vn / (1.0 - B2 ** STEP)
    go_ref[...] = gv
    d_ref[...] = -LR * (m_hat / (jnp.sqrt(v_hat) + ADAM_EPS) + WD * w_ref[...])
    mo_ref[...] = mn
    vo_ref[...] = vn


def _adamw(w, g, m, v, name):
    r, c = w.shape
    parts = g.ndim == 3
    tr = r if r <= 128 else _pick(r, (128, 64))

    def body(w_ref, g_ref, m_ref, v_ref, go_ref, d_ref, mo_ref, vo_ref):
        if parts:
            gv = g_ref[0].astype(F32)
            for d in range(1, g.shape[0]):
                gv = gv + g_ref[d].astype(F32)
        else:
            gv = g_ref[...]
        mn = B1 * m_ref[...] + (1.0 - B1) * gv
        vn = B2 * v_ref[...] + (1.0 - B2) * (gv * gv)
        m_hat = mn / (1.0 - B1 ** STEP)
        v_hat = vn / (1.0 - B2 ** STEP)
        go_ref[...] = gv
        d_ref[...] = -LR * (m_hat / (jnp.sqrt(v_hat) + ADAM_EPS) + WD * w_ref[...])
        mo_ref[...] = mn
        vo_ref[...] = vn

    tile = pl.BlockSpec((tr, c), lambda i: (i, 0))
    g_spec = pl.BlockSpec((g.shape[0], tr, c), lambda i: (0, i, 0)) if parts else tile
    out = jax.ShapeDtypeStruct((r, c), F32)
    return pl.pallas_call(
        body, name=name, grid=(r // tr,), in_specs=[tile, g_spec, tile, tile], out_specs=[tile] * 4, out_shape=[out] * 4,
        compiler_params=_params(("parallel",)),
    )(w, g, m, v)


BIG = ["w_in", "w_a_out", "w_b_out", "w_mix_out", "w_ff_gate", "w_ff_up", "w_ff_down", "w_ple_gate", "w_ple_proj"]
TRANSPOSED = ["w_in", "w_ff_gate", "w_ff_up"]
GRAD_OF = dict(w_ple_proj="pp", w_ple_gate="pg", w_ff_down="down", w_ff_gate="gate_t", w_ff_up="up_t", w_mix_out="mix", w_a_out="a_out",
               w_b_out="b_out")
SMALL = ["conv_w", "w_alpha_up", "b_alpha_up", "gla_head_gain", "g_pre_mix", "g_post_mix", "g_pre_ffn", "g_post_ffn", "g_pre_ple", "g_post_ple"]
WEIGHTS = ["w_in", "conv_w", "w_a_out", "w_alpha_up", "b_alpha_up", "gla_head_gain", "w_b_out", "w_mix_out", "g_pre_mix", "g_post_mix",
           "g_pre_ffn", "g_post_ffn", "w_ff_gate", "w_ff_up", "w_ff_down", "g_pre_ple", "g_post_ple", "w_ple_gate", "w_ple_proj"]


def _in_t_from_blocks(z):
    w = z.reshape(-1, z.shape[-1])
    return w[R_GA:R_END], jnp.concatenate([w[:R_GA], jnp.zeros((REST - R_GA, w.shape[1]), w.dtype)], axis=0)


def _blocks_from_in_t(g_gates, g_rest):
    per = R_END // N_DEV

    def rows(lo, hi):
        out = []
        for n0, n1, g in ((0, R_GA, g_rest), (R_GA, R_END, g_gates)):
            a, e = max(lo, n0), min(hi, n1)
            if a < e:
                out.append(g[a - n0:e - n0])
        return out

    return jnp.stack([jnp.concatenate(rows(b * per, (b + 1) * per), axis=0) for b in range(N_DEV)])


def _cols_to_full(g8):
    n, r, c = g8.shape
    return jnp.transpose(g8, (1, 0, 2)).reshape(r, n * c)


def _full_to_cols(a):
    r, c = a.shape
    return jnp.transpose(a.reshape(r, N_DEV, c // N_DEV), (1, 0, 2))


def _pack(arrs, rows):
    flat = jnp.concatenate([a.reshape(-1) for a in arrs])
    return jnp.pad(flat, (0, rows * 128 - flat.shape[0])).reshape(rows, 128)


def _unpack(packed, shapes):
    flat, out, o = packed.reshape(-1), [], 0
    for s in shapes:
        size = 1
        for d in s:
            size *= d
        out.append(flat[o:o + size].reshape(s))
        o += size
    return out


def kernel(x, p, w_in, conv_w, w_a_out, w_alpha_up, b_alpha_up, gla_head_gain, w_b_out, w_mix_out, g_pre_mix, g_post_mix, g_pre_ffn, g_post_ffn, w_ff_gate, w_ff_up, w_ff_down, g_pre_ple, g_post_ple, w_ple_gate, w_ple_proj, loss_target, m_w_in, m_conv_w, m_w_a_out, m_w_alpha_up, m_b_alpha_up, m_gla_head_gain, m_w_b_out, m_w_mix_out, m_g_pre_mix, m_g_post_mix, m_g_pre_ffn, m_g_post_ffn, m_w_ff_gate, m_w_ff_up, m_w_ff_down, m_g_pre_ple, m_g_post_ple, m_w_ple_gate, m_w_ple_proj, v_w_in, v_conv_w, v_w_a_out, v_w_alpha_up, v_b_alpha_up, v_gla_head_gain, v_w_b_out, v_w_mix_out, v_g_pre_mix, v_g_post_mix, v_g_pre_ffn, v_g_post_ffn, v_w_ff_gate, v_w_ff_up, v_w_ff_down, v_g_pre_ple, v_g_post_ple, v_w_ple_gate, v_w_ple_proj):
    args = dict(locals())
    shard = lambda n, a: jnp.transpose(a[0]) if n in TRANSPOSED else a[0]
    wts = {n: shard(n, args[n]) for n in WEIGHTS}
    mom = {n: shard(n, args["m_" + n]) for n in WEIGHTS}
    var = {n: shard(n, args["v_" + n]) for n in WEIGHTS}
    me =4 * lax.axis_index("x") + 2 * lax.axis_index("y") + lax.axis_index("c")

    groups = [["w_in", "conv_w", "w_alpha_up"], ["w_a_out", "w_b_out", "w_mix_out"], ["w_ff_gate", "w_ff_up"], ["w_ff_down"],
              ["w_ple_gate", "w_ple_proj"]]
    grad_groups = []
    rows_full = lambda g: g.reshape(-1, g.shape[-1])
    gathers, scatters = {}, {}

    def gather_start(gi, after):
        if gi not in gathers:
            shards = [wts[n].astype(BF16) if n in BIG else wts[n] for n in groups[gi]]
            zones = [jax.ShapeDtypeStruct((N_DEV,) + s.shape, s.dtype) for s in shards]
            plan, peers = (_gather_plan_near, 3) if gi == 0 else (_gather_plan, 4)
            gathers[gi] = (shards, _copies_start(plan, peers * len(shards), shards, zones, "gather_start_%d" % gi, after))
        return gathers[gi][1][-1]

    def gather_pass_on(gi, after):
        shards, state = gathers[gi]
        gathers[gi] = (shards, _copies_relay(_gather_plan, state, len(shards), _pass_on_plan, 3 * len(shards), "gather_pass_on_%d" % gi, after))
        return gathers[gi][1][-1]

    def gather_finish(gi, after):
        shards, state = gathers[gi]
        shards, zones = _copies_wait(_gather_plan_near if gi == 0 else _pass_on_plan, state, len(shards), "gather_wait_%d" % gi, after)
        if gi == 0:
            zones = _gather_relay(zones, "gather_relay_%d" % gi)
            gather_start(1, (zones[0],))
        g8 = {n: _put_own(s, z, me, "gather_own_" + n) for n, s, z in zip(groups[gi], shards, zones)}
        if gi == 0:
            in_gates_t, in_rest_t = _in_t_from_blocks(g8["w_in"])
            return dict(in_gates_t=in_gates_t, in_rest_t=in_rest_t,
                        conv_w=_cols_to_full(g8["conv_w"]), w_alpha_up=_cols_to_full(g8["w_alpha_up"]))
        if gi == 1:
            return dict(a_out=_cols_to_full(g8["w_a_out"]), b_out=_cols_to_full(g8["w_b_out"]), mix=rows_full(g8["w_mix_out"]))
        if gi == 2:
            return dict(gate_t=rows_full(g8["w_ff_gate"]), up_t=rows_full(g8["w_ff_up"]))
        if gi == 3:
            return dict(down=rows_full(g8["w_ff_down"]))
        return dict(pg=rows_full(g8["w_ple_gate"]), pp=g8["w_ple_proj"])

    def scatter_start(names, gw):
        gi = len(grad_groups)
        grad_groups.append(names)
        full = {n: _blocks_from_in_t(gw["in_gates_t"], gw["in_rest_t"]) if n == "w_in" else gw[GRAD_OF[n]] for n in names}
        for n in names:
            if n in ("w_a_out", "w_b_out"):
                full[n] = _full_to_cols(full[n])
        parts = [full[n] if full[n].ndim == 3 else full[n].reshape(N_DEV, -1, full[n].shape[-1]) for n in names]
        if names == ["w_in"]:
            from_sibling = _sibling_exchange(parts, "scatter_sibling_%d" % gi)
            parts = [_pair_add(a, b, "scatter_add_%d_%s" % (gi, n)) for n, a, b in zip(names, parts, from_sibling)]
            scatters[gi] = _copies_start(_chip_plan, 3 * len(parts), parts, parts, "scatter_start_%d" % gi)
        else:
            scatters[gi] = _copies_start(_scatter_plan, (N_DEV - 1) * len(parts), parts, parts, "scatter_start_%d" % gi)
        return scatters[gi][-1]

    small = {n: wts[n].reshape(1, -1) for n in SMALL[2:]}

    loss_rows, grad_x, gs = _local_step(x[0], p[0, 0], loss_target[0], gather_start, gather_pass_on, gather_finish, scatter_start, small)
    gs["loss"] = jnp.sum(loss_rows).reshape(1, 1)

    small_shapes = [gs[n].shape for n in SMALL]
    gs_packed = _pack([gs[n] for n in SMALL + ["loss"]], 192)
    small_state = _copies_start(_everyone_plan, N_DEV - 1, [gs_packed], [jax.ShapeDtypeStruct((N_DEV,) + gs_packed.shape, F32)],
                                "small_start", (grad_x,))

    res, done = {}, (small_state[-1],)
    for gi, names in enumerate(grad_groups):
        plan, slot = (_chip_plan, me // 2) if names == ["w_in"] else (_scatter_plan, me)
        mine, got = _copies_wait(plan, scatters[gi], len(names), "scatter_wait_%d" % gi, done)
        for n, g, own in zip(names, got, mine):
            res[n] = _adamw_parts(wts[n], g, own, slot, mom[n], var[n], "adamw_" + n)
        done = tuple(res[n][1] for n in names)

    (gs_own,), (gs_got,) = _copies_wait(_everyone_plan, small_state, 1, "small_wait", done)
    gsum = dict(zip(SMALL + ["loss"], _unpack(_sum_parts(gs_got, gs_own, me, "small_sum"), small_shapes + [(1, 1)])))
    loss = gsum["loss"].reshape(())
    gsum["conv_w"] = lax.dynamic_index_in_dim(gsum["conv_w"].reshape(3, N_DEV, -1), me, axis=1, keepdims=False)
    gsum["w_alpha_up"] = lax.dynamic_index_in_dim(gsum["w_alpha_up"].reshape(GATE_RANK, N_DEV, -1), me, axis=1, keepdims=False)

    shard_shapes = [wts[n].shape for n in SMALL]
    packed = [_pack([d[n] for n in SMALL], 120) for d in (wts, gsum, mom, var)]
    outs = [_unpack(o, shard_shapes) for o in _adamw(*packed, "adamw_small")]
    for i, n in enumerate(SMALL):
        res[n] = [o[i] for o in outs]

    back = lambda n, a: (jnp.transpose(a) if n in TRANSPOSED else a)[None]
    return (loss, grad_x[None], *[back(n, res[n][i]) for i in range(4) for n in WEIGHTS])
```

```python
import functools

import jax
import jax.numpy as jnp
from jax import lax
from jax.experimental import pallas as pl
from jax.experimental.pallas import tpu as pltpu
from jax.experimental.pallas import tpu_sc as plsc

F32, BF16 = jnp.float32, jnp.bfloat16
EPS = 1e-6
CHUNK = 64
STEP_CHUNKS = 2
STEP_ROWS = STEP_CHUNKS * CHUNK
HEADS, DK, DV = 4, 128, 256
GATE_RANK = 16
TAU = 16.0
LR, B1, B2, ADAM_EPS, WD, STEP = 0.001, 0.9, 0.999, 1e-08, 0.01, 10
N_DEV = 8
MESH = pl.DeviceIdType.MESH
VMEM_LIMIT = 56 * 1024 * 1024
ANY = pl.BlockSpec(memory_space=pl.ANY)

C_GA, C_GB = 0, 2048
C_AX, C_AB, C_AC, C_Q, C_K, C_V, C_OG, C_ALR = 0, 1024, 2048, 3072, 3584, 4096, 5120, 6144
REST = 6272
R_AX, R_AB, R_AC, R_Q, R_K, R_V, R_OG, R_ALR, R_GA, R_GB, R_END = 0, 1024, 2048, 3072, 3584, 4096, 5120, 6144, 6160, 8208, 10256


def _params(sem):
    return pltpu.CompilerParams(dimension_semantics=sem, vmem_limit_bytes=VMEM_LIMIT)


def _pick(n, cands):
    for c in cands:
        if n % c == 0:
            return c
    return n


def _tiles(r, c):
    for tr in (128, 64):
        if r % tr == 0:
            return r // tr, (tr, c), lambda i: (i, 0)
    tc = _pick(c, (256, 128))
    return c // tc, (r, tc), lambda i: (0, i)


def _mm(a, b, mode, out_dtype, name, after=(), add=None, b3=False, out3=False, tm=None, tk=None):
    bshape = (b.shape[1], N_DEV * b.shape[2]) if b3 else b.shape
    if mode == "nn":
        (m, k), (k2, n) = a.shape, bshape
    elif mode == "nt":
        (m, k), (n, k2) = a.shape, bshape
    else:
        (k, m), (k2, n) = a.shape, bshape
    assert k == k2 and a.dtype == BF16 and b.dtype == BF16, (name, a.shape, b.shape, a.dtype, b.dtype)
    tm = tm if tm and m % tm == 0 else _pick(m, (2048, 1024, 512, 256))
    tn = _pick(n, (1152, 1024, 1408, 896, 512, 256))
    tk = tk if tk and k % tk == 0 else _pick(k, (2048, 1408, 1152, 1024, 896, 512, 256))
    if out3 or (b3 and mode == "nn"):
        tn = n // N_DEV
    if b3 and mode == "nt":
        tk = k // N_DEV
    nk = k // tk
    dims = {"nn": (((1,), (0,)), ((), ())), "nt": (((1,), (1,)), ((), ())), "tn": (((0,), (0,)), ((), ()))}[mode]
    n_extra = len(after) + (add is not None)

    def body(a_ref, b_ref, *rest):
        o_ref = rest[n_extra]
        prod = lax.dot_general(a_ref[...], b_ref[...], dims, preferred_element_type=F32)
        if nk == 1:
            o_ref[...] = (prod if add is None else prod + rest[0][...]).astype(o_ref.dtype)
            return
        acc_ref = rest[n_extra + 1]
        kk = pl.program_id(2)

        @pl.when(kk == 0)
        def _():
            acc_ref[...] = prod if add is None else prod + rest[0][...]

        @pl.when((kk > 0) & (kk < nk - 1))
        def _():
            acc_ref[...] += prod

        @pl.when(kk == nk - 1)
        def _():
            o_ref[...] = (acc_ref[...] + prod).astype(o_ref.dtype)

    a_spec = pl.BlockSpec((tk, tm), lambda i, j, kk: (kk, i)) if mode == "tn" else pl.BlockSpec((tm, tk), lambda i, j, kk: (i, kk))
    if b3:
        b_spec = (pl.BlockSpec((None, tn, tk), lambda i, j, kk: (kk, j, 0)) if mode == "nt"
                  else pl.BlockSpec((None, tk, tn), lambda i, j, kk: (j, kk, 0)))
    else:
        b_spec = pl.BlockSpec((tn, tk), lambda i, j, kk: (j, kk)) if mode == "nt" else pl.BlockSpec((tk, tn), lambda i, j, kk: (kk, j))
    tile = pl.BlockSpec((tm, tn), lambda i, j, kk: (i, j))
    out_spec = pl.BlockSpec((None, tm, tn), lambda i, j, kk: (j, i, 0)) if out3 else tile
    return pl.pallas_call(
        body, name=name, grid=(m // tm, n // tn, nk),
        in_specs=[a_spec, b_spec] + ([tile] if add is not None else []) + [ANY] * len(after), out_specs=out_spec,
        out_shape=jax.ShapeDtypeStruct((N_DEV, m, tn) if out3 else (m, n), out_dtype),
        scratch_shapes=[pltpu.VMEM((tm, tn), F32)] if nk > 1 else [],
        compiler_params=_params(("parallel", "parallel", "arbitrary")),
    )(a, b, *([add] if add is not None else []), *after)


def _rows(body, t, tr, ins, outs, name, after=()):
    in_specs = []
    for arr, sp in ins:
        if sp[0] == "t":
            in_specs.append(pl.BlockSpec((tr, sp[1]), lambda i, cb=sp[2]: (i, cb)))
        else:
            in_specs.append(pl.BlockSpec(arr.shape, lambda i, nd=arr.ndim: (0,) * nd))
    out_specs, out_shape = [], []
    for shape, dt, kind in outs:
        out_specs.append(pl.BlockSpec((tr, shape[1]), lambda i: (i, 0)) if kind == "t" else pl.BlockSpec(shape, lambda i: (0, 0)))
        out_shape.append(jax.ShapeDtypeStruct(shape, dt))
    return pl.pallas_call(
        body, name=name, grid=(t // tr,), in_specs=in_specs + [ANY] * len(after), out_specs=out_specs, out_shape=out_shape,
        compiler_params=_params(("arbitrary",)),
    )(*[arr for arr, _ in ins], *after)


def _rinv(v):
    return lax.rsqrt(jnp.mean(v * v, axis=-1, keepdims=True) + EPS)


def _sig(v):
    return 1.0 / (1.0 + jnp.exp(-v))


def _acc(ref, val):
    @pl.when(pl.program_id(0) == 0)
    def _():
        ref[...] = jnp.zeros_like(ref)

    ref[...] += jnp.sum(val, axis=0, keepdims=True)


def _rms_fwd(x, g, name):
    t, d = x.shape

    def body(x_ref, g_ref, h_ref):
        xv = x_ref[...]
        h_ref[...] = (xv * _rinv(xv) * g_ref[...]).astype(BF16)

    return _rows(body, t, 256, [(x, ("t", d, 0)), (g, ("b",))], [((t, d), BF16, "t")], name)[0]


def _post_pre(x, m, g_post, g_pre, name, after=()):
    t, d = x.shape

    def body(x_ref, m_ref, gp_ref, gn_ref, *rest):
        xo_ref, h_ref = rest[len(after):]
        mv = m_ref[...]
        xn = x_ref[...] + mv * _rinv(mv) * gp_ref[...]
        xo_ref[...] = xn
        h_ref[...] = (xn * _rinv(xn) * gn_ref[...]).astype(BF16)

    return _rows(body, t, 128, [(x, ("t", d, 0)), (m, ("t", d, 0)), (g_post, ("b",)), (g_pre, ("b",))],
                 [((t, d), F32, "t"), ((t, d), BF16, "t")], name, after)


def _mix_fwd(proj, ya, yb, name):
    t, d = ya.shape

    def body(ga_ref, gb_ref, ya_ref, yb_ref, o_ref):
        o_ref[...] = (_sig(ga_ref[...].astype(F32)) * ya_ref[...].astype(F32)
                      + _sig(gb_ref[...].astype(F32)) * yb_ref[...].astype(F32)).astype(BF16)

    return _rows(body, t, 256, [(proj, ("t", d, C_GA // d)), (proj, ("t", d, C_GB // d)), (ya, ("t", d, 0)), (yb, ("t", d, 0))],
                 [((t, d), BF16, "t")], name)[0]


def _mix_bwd(dmix, proj, ya, yb, name):
    t, d = ya.shape

    def body(dm_ref, ga_ref, gb_ref, ya_ref, yb_ref, dg_ref, dya_ref, dyb_ref):
        dm = dm_ref[...]
        sa, sb = _sig(ga_ref[...].astype(F32)), _sig(gb_ref[...].astype(F32))
        dg_ref[:, :d] = (dm * ya_ref[...].astype(F32) * sa * (1.0 - sa)).astype(BF16)
        dg_ref[:, d:] = (dm * yb_ref[...].astype(F32) * sb * (1.0 - sb)).astype(BF16)
        dya_ref[...] = (dm * sa).astype(BF16)
        dyb_ref[...] = (dm * sb).astype(BF16)

    return _rows(body, t, 128,
                 [(dmix, ("t", d, 0)), (proj, ("t", d, C_GA // d)), (proj, ("t", d, C_GB // d)), (ya, ("t", d, 0)), (yb, ("t", d, 0))],
                 [((t, 2 * d), BF16, "t"), ((t, d), BF16, "t"), ((t, d), BF16, "t")], name)


def _swiglu_call(body, ins, n_out, name):
    t, f = ins[0].shape
    tc = _pick(f, (1408, 512))
    tile = pl.BlockSpec((512, tc), lambda i, j: (i, j))
    return pl.pallas_call(
        body, name=name, grid=(t // 512, f // tc), in_specs=[tile] * len(ins), out_specs=[tile] * n_out,
        out_shape=[jax.ShapeDtypeStruct((t, f), BF16)] * n_out, compiler_params=_params(("parallel", "parallel")),
    )(*ins)


def _swiglu_fwd(fg, fu, name):
    def body(g_ref, u_ref, s_ref):
        gv = g_ref[...].astype(F32)
        s_ref[...] = (gv * _sig(gv) * u_ref[...].astype(F32)).astype(BF16)

    return _swiglu_call(body, [fg, fu], 1, name)[0]


def _swiglu_bwd(ds, fg, fu, name):
    def body(ds_ref, g_ref, u_ref, dg_ref, du_ref):
        dsv, gv, uv = ds_ref[...].astype(F32), g_ref[...].astype(F32), u_ref[...].astype(F32)
        sg = _sig(gv)
        dg_ref[...] = (dsv * uv * sg * (1.0 + gv * (1.0 - sg))).astype(BF16)
        du_ref[...] = (dsv * gv * sg).astype(BF16)

    return _swiglu_call(body, [ds, fg, fu], 2, name)


def _ple_final(x2, pg, pp, tgt, g_post, name):
    t, d = x2.shape

    def body(x_ref, pg_ref, pp_ref, t_ref, g_ref, loss_ref, d3_ref, dpg_ref, dpp_ref, dg_ref):
        sg, ppv, g = _sig(pg_ref[...]), pp_ref[...], g_ref[...]
        e = sg * ppv
        r = _rinv(e)
        eh = e * r
        diff = x_ref[...] + eh * g - t_ref[...]
        loss_ref[...] = 0.5 * jnp.mean(diff * diff, axis=-1, keepdims=True)
        d3 = diff * (1.0 / d)
        d3_ref[...] = d3
        gd = d3 * g
        de = r * (gd - eh * jnp.mean(gd * eh, axis=-1, keepdims=True))
        dpg_ref[...] = (de * ppv * sg * (1.0 - sg)).astype(BF16)
        dpp_ref[...] = (de * sg).astype(BF16)
        _acc(dg_ref, d3 * eh)

    return _rows(body, t, 128, [(x2, ("t", d, 0)), (pg, ("t", d, 0)), (pp, ("t", d, 0)), (tgt, ("t", d, 0)), (g_post, ("b",))],
                 [((t, 1), F32, "t"), ((t, d), F32, "t"), ((t, d), BF16, "t"), ((t, d), BF16, "t"), ((1, d), F32, "a")], name)


def _norm_bwd(dn, dh, x, g_pre, fm, g_post, name):
    t, d = x.shape
    two = fm is not None

    def body(*refs):
        if two:
            dn_ref, dh_ref, x_ref, gp_ref, f_ref, gq_ref, dx_ref, df_ref, dgp_ref, dgq_ref = refs
        else:
            dn_ref, dh_ref, x_ref, gp_ref, dx_ref, dgp_ref = refs
        xv, dhv = x_ref[...], dh_ref[...]
        r = _rinv(xv)
        xh = xv * r
        gd = dhv * gp_ref[...]
        dx = dn_ref[...] + r * (gd - xh * jnp.mean(gd * xh, axis=-1, keepdims=True))
        dx_ref[...] = dx
        _acc(dgp_ref, dhv * xh)
        if two:
            fv = f_ref[...]
            rf = _rinv(fv)
            fh = fv * rf
            gd2 = dx * gq_ref[...]
            df_ref[...] = (rf * (gd2 - fh * jnp.mean(gd2 * fh, axis=-1, keepdims=True))).astype(BF16)
            _acc(dgq_ref, dx * fh)

    ins = [(dn, ("t", d, 0)), (dh, ("t", d, 0)), (x, ("t", d, 0)), (g_pre, ("b",))]
    outs = [((t, d), F32, "t")]
    if two:
        ins += [(fm, ("t", d, 0)), (g_post, ("b",))]
        outs += [((t, d), BF16, "t"), ((1, d), F32, "a"), ((1, d), F32, "a")]
    else:
        outs += [((1, d), F32, "a")]
    return _rows(body, t, 128, ins, outs, name)


CONV_TC = 256


def _shift_down(v, s):
    rows = lax.broadcasted_iota(jnp.int32, v.shape, 0)
    return jnp.where(rows >= s, pltpu.roll(v, s, 0), 0.0)


def _shift_up(v, s):
    n = v.shape[0]
    rows = lax.broadcasted_iota(jnp.int32, v.shape, 0)
    return jnp.where(rows < n - s, pltpu.roll(v, n - s, 0), 0.0)


def _conv_specs(t):
    nb = 1024 // CONV_TC
    seg = lambda c0: pl.BlockSpec((t, CONV_TC), lambda j, cb=c0 // CONV_TC: (0, cb + j))
    own = pl.BlockSpec((t, CONV_TC), lambda j: (0, j))
    wspec = pl.BlockSpec((3, CONV_TC), lambda j: (0, j))
    return nb, seg, own, wspec


def _conv_fwd(proj, conv_w, name, after=()):
    t = proj.shape[0]
    nb, seg, own, wspec = _conv_specs(t)

    def body(ax_ref, ab_ref, ac_ref, w_ref, *rest):
        za_ref = rest[len(after)]
        u = ac_ref[...].astype(F32) * ax_ref[...].astype(F32)
        w = w_ref[...]
        yc = w[0:1] * _shift_down(u, 2) + w[1:2] * _shift_down(u, 1) + w[2:3] * u
        za_ref[...] = (ab_ref[...].astype(F32) * yc).astype(BF16)

    return pl.pallas_call(
        body, name=name, grid=(nb,), in_specs=[seg(C_AX), seg(C_AB), seg(C_AC), wspec] + [ANY] * len(after), out_specs=own,
        out_shape=jax.ShapeDtypeStruct((t, 1024), BF16), compiler_params=_params(("parallel",)),
    )(proj, proj, proj, conv_w, *after)


def _conv_bwd(dza, proj, conv_w, name):
    t = proj.shape[0]
    nb, seg, own, wspec = _conv_specs(t)

    def body(dz_ref, ax_ref, ab_ref, ac_ref, w_ref, dax_ref, dab_ref, dac_ref, dw_ref):
        ax, ab, ac, dz = ax_ref[...].astype(F32), ab_ref[...].astype(F32), ac_ref[...].astype(F32), dz_ref[...].astype(F32)
        w = w_ref[...]
        u = ac * ax
        u1, u2 = _shift_down(u, 1), _shift_down(u, 2)
        yc = w[0:1] * u2 + w[1:2] * u1 + w[2:3] * u
        dab_ref[...] = (dz * yc).astype(BF16)
        dyc = dz * ab
        du = w[2:3] * dyc + w[1:2] * _shift_up(dyc, 1) + w[0:1] * _shift_up(dyc, 2)
        dax_ref[...] = (du * ac).astype(BF16)
        dac_ref[...] = (du * ax).astype(BF16)
        dw_ref[0:1, :] = jnp.sum(dyc * u2, axis=0, keepdims=True)
        dw_ref[1:2, :] = jnp.sum(dyc * u1, axis=0, keepdims=True)
        dw_ref[2:3, :] = jnp.sum(dyc * u, axis=0, keepdims=True)

    act = jax.ShapeDtypeStruct((t, 1024), BF16)
    return pl.pallas_call(
        body, name=name, grid=(nb,), in_specs=[own, seg(C_AX), seg(C_AB), seg(C_AC), wspec], out_specs=[own, own, own, wspec],
        out_shape=[act, act, act, jax.ShapeDtypeStruct((3, 1024), F32)], compiler_params=_params(("parallel",)),
    )(dza, proj, proj, proj, conv_w)


def _dot(a, b, dims, precision=None):
    return lax.dot_general(a, b, (dims, ((), ())), precision=precision, preferred_element_type=F32)


_CONTRACT = {"nn": ((1,), (0,)), "nt": ((1,), (1,)), "tn": ((0,), (0,))}


def _bdot_raw(a, b, mode):
    return _dot(a.astype(BF16), b.astype(BF16), _CONTRACT[mode])


@functools.partial(jax.custom_vjp, nondiff_argnums=(2,))
def _bdot(a, b, mode):
    return _bdot_raw(a, b, mode)


def _bdot_fwd(a, b, mode):
    return _bdot_raw(a, b, mode), (a, b)


def _bdot_bwd(mode, res, ct):
    a, b = res
    if mode == "nn":
        return _bdot_raw(ct, b, "nt"), _bdot_raw(a, ct, "tn")
    if mode == "nt":
        return _bdot_raw(ct, b, "nn"), _bdot_raw(ct, a, "tn")
    return _bdot_raw(b, ct, "nt"), _bdot_raw(a, ct, "nn")


_bdot.defvjp(_bdot_fwd, _bdot_bwd)


@functools.partial(jax.custom_vjp, nondiff_argnums=(2,))
def _sum_dot(ones, x, mode):
    head = x.astype(BF16)
    tail = x - head.astype(F32)
    if mode == "nn":
        return _bdot_raw(ones, head, "nn") + _bdot_raw(ones, tail, "nn")
    return _bdot_raw(head, ones, "tn") + _bdot_raw(tail, ones, "tn")


def _sum_dot_fwd(ones, x, mode):
    return _sum_dot(ones, x, mode), ones


def _sum_dot_bwd(mode, ones, ct):
    return jnp.zeros_like(ones), (_bdot_raw(ones, ct, "tn") if mode == "nn" else _bdot_raw(ones, ct, "nt"))


_sum_dot.defvjp(_sum_dot_fwd, _sum_dot_bwd)


def _gla_chunk(q, k, v, og, alr, s_in, wa, ba, gain):
    c = q.shape[0]
    z =_bdot(alr, wa, "nn") + ba
    la = (jnp.minimum(z, 0.0) - jnp.log(1.0 + jnp.exp(-jnp.abs(z)))) * (1.0 / TAU)
    row = lax.broadcasted_iota(jnp.int32, (c, c), 0)
    col = lax.broadcasted_iota(jnp.int32, (c, c), 1)
    lower = row >= col
    b = _sum_dot(lower.astype(F32), la, "nn")
    trow = lax.broadcasted_iota(jnp.int32, la.shape, 0)
    mid = jnp.sum(jnp.where(trow <= c // 2, la, 0.0), axis=0, keepdims=True)
    blast = jnp.sum(la, axis=0, keepdims=True)
    qs = q * (DK ** -0.5)
    e_up, e_dn = jnp.exp(b - mid), jnp.exp(mid - b)
    a_fwd = _bdot(qs * e_up, k * e_dn, "nt")
    a_rev = _bdot(qs * e_dn, k * e_up, "nt")
    att = jnp.where(lower, a_fwd, a_rev)
    o = _bdot(att, v, "nn") + _bdot(qs * jnp.exp(b), s_in, "nn")
    upd = _bdot(k * jnp.exp(blast - b), v, "tn")
    blast_col = _sum_dot(jnp.ones((c, DV), F32), la, "tn")
    s_out = jnp.exp(blast_col) * s_in + upd
    on = o * _rinv(o) * gain
    return on * og * _sig(og), s_out


def _gla_specs(t, rev):
    n = t // STEP_ROWS
    ch = (lambda i: n - 1 - i) if rev else (lambda i: i)
    col = lambda w, c0: pl.BlockSpec((STEP_ROWS, HEADS * w), lambda i, cb=c0 // (HEADS * w): (ch(i), cb))
    whole = lambda shape: pl.BlockSpec(shape, lambda i, nd=len(shape): (0,) * nd)
    specs = dict(
        q=col(DK, C_Q), k=col(DK, C_K), v=col(DV, C_V), og=col(DV, C_OG),
        alr=pl.BlockSpec((STEP_ROWS, 128), lambda i: (ch(i), C_ALR // 128)),
        wa=whole((128, HEADS * DK)), ba=whole((1, HEADS * DK)), gain=whole((1, DV)),
        state=pl.BlockSpec((STEP_CHUNKS, HEADS, DK, DV), lambda i: (ch(i), 0, 0, 0)),
        odk=pl.BlockSpec((STEP_ROWS, HEADS * DK), lambda i: (ch(i), 0)), odv=pl.BlockSpec((STEP_ROWS, HEADS * DV), lambda i: (ch(i), 0)),
        oalr=pl.BlockSpec((STEP_ROWS, 128), lambda i: (ch(i), 0)), whole=whole,
    )
    return n, specs


def _head_cols(h):
    return slice(h * DK, (h + 1) * DK), slice(h * DV, (h + 1) * DV)


def _gla_fwd(proj, wa, ba, gain, name):
    t = proj.shape[0]
    n, sp = _gla_specs(t, False)

    def body(q_ref, k_ref, v_ref, og_ref, alr_ref, wa_ref, ba_ref, g_ref, zb_ref, st_ref, s_scr):
        @pl.when(pl.program_id(0) == 0)
        def _():
            s_scr[...] = jnp.zeros_like(s_scr)

        state = [s_scr[h] for h in range(HEADS)]
        for c in range(STEP_CHUNKS):
            rows = slice(c * CHUNK, (c + 1) * CHUNK)
            alr = alr_ref[rows, :].astype(F32)
            for h in range(HEADS):
                kc, vc = _head_cols(h)
                st_ref[c, h] = state[h]
                zb, state[h] = _gla_chunk(q_ref[rows, kc].astype(F32), k_ref[rows, kc].astype(F32), v_ref[rows, vc].astype(F32),
                                          og_ref[rows, vc].astype(F32), alr, state[h], wa_ref[:, kc].astype(F32), ba_ref[:, kc], g_ref[...])
                zb_ref[rows, vc] = zb.astype(BF16)
        for h in range(HEADS):
            s_scr[h] = state[h]

    return pl.pallas_call(
        body, name=name, grid=(n,),
        in_specs=[sp["q"], sp["k"], sp["v"], sp["og"], sp["alr"], sp["wa"], sp["ba"], sp["gain"]],
        out_specs=[sp["odv"], sp["state"]],
        out_shape=[jax.ShapeDtypeStruct((t, HEADS * DV), BF16), jax.ShapeDtypeStruct((t // CHUNK, HEADS, DK, DV), F32)],
        scratch_shapes=[pltpu.VMEM((HEADS, DK, DV), F32)],
        compiler_params=_params(("arbitrary",)),
    )(proj, proj, proj, proj, proj, wa, ba, gain)


def _gla_bwd(dzb, proj, states, wa, ba, gain, name):
    t = proj.shape[0]
    n, sp = _gla_specs(t, True)

    def body(dz_ref, q_ref, k_ref, v_ref, og_ref, alr_ref, st_ref, wa_ref, ba_ref, g_ref,
             dq_ref, dk_ref, dv_ref, dog_ref, dalr_ref, dwa_ref, dba_ref, dg_ref, ds_scr):
        @pl.when(pl.program_id(0) == 0)
        def _():
            ds_scr[...] = jnp.zeros_like(ds_scr)
            dwa_ref[...] = jnp.zeros_like(dwa_ref)
            dba_ref[...] = jnp.zeros_like(dba_ref)
            dg_ref[...] = jnp.zeros_like(dg_ref)

        dstate = [ds_scr[h] for h in range(HEADS)]
        dwa_sum, dba_sum, dgain_sum = [None] * HEADS, [None] * HEADS, None
        for c in reversed(range(STEP_CHUNKS)):
            rows = slice(c * CHUNK, (c + 1) * CHUNK)
            alr = alr_ref[rows, :].astype(F32)
            dalr_sum = None
            for h in range(HEADS):
                kc, vc = _head_cols(h)
                args = (q_ref[rows, kc].astype(F32), k_ref[rows, kc].astype(F32), v_ref[rows, vc].astype(F32), og_ref[rows, vc].astype(F32),
                        alr, st_ref[c, h], wa_ref[:, kc].astype(F32), ba_ref[:, kc], g_ref[...])
                _, vjp = jax.vjp(_gla_chunk, *args)
                dq, dk, dv, dog, dalr, dstate[h], dwa, dba, dgain = vjp((dz_ref[rows, vc].astype(F32), dstate[h]))
                dq_ref[rows, kc] = dq.astype(BF16)
                dk_ref[rows, kc] = dk.astype(BF16)
                dv_ref[rows, vc] = dv.astype(BF16)
                dog_ref[rows, vc] = dog.astype(BF16)
                dwa_sum[h] = dwa if dwa_sum[h] is None else dwa_sum[h] + dwa
                dba_sum[h] = dba if dba_sum[h] is None else dba_sum[h] + dba
                dalr_sum = dalr if dalr_sum is None else dalr_sum + dalr
                dgain_sum = dgain if dgain_sum is None else dgain_sum + dgain
            dalr_ref[rows, :] = dalr_sum
        for h in range(HEADS):
            ds_scr[h] = dstate[h]
            dwa_ref[h] += dwa_sum[h]
            dba_ref[h] += dba_sum[h]
        dg_ref[...] += dgain_sum

    whole = sp["whole"]
    return pl.pallas_call(
        body, name=name, grid=(n,),
        in_specs=[sp["odv"], sp["q"], sp["k"], sp["v"], sp["og"], sp["alr"], sp["state"], sp["wa"], sp["ba"], sp["gain"]],
        out_specs=[sp["odk"], sp["odk"], sp["odv"], sp["odv"], sp["oalr"], whole((HEADS, 128, DK)), whole((HEADS, 1, DK)), whole((1, DV))],
        out_shape=[jax.ShapeDtypeStruct((t, HEADS * DK), BF16), jax.ShapeDtypeStruct((t, HEADS * DK), BF16),
                   jax.ShapeDtypeStruct((t, HEADS * DV), BF16), jax.ShapeDtypeStruct((t, HEADS * DV), BF16),
                   jax.ShapeDtypeStruct((t, 128), F32), jax.ShapeDtypeStruct((HEADS, 128, DK), F32),
                   jax.ShapeDtypeStruct((HEADS, 1, DK), F32), jax.ShapeDtypeStruct((1, DV), F32)],
        scratch_shapes=[pltpu.VMEM((HEADS, DK, DV), F32)],
        compiler_params=_params(("arbitrary",)),
    )(dzb, proj, proj, proj, proj, proj, states, wa, ba, gain)


def _local_step(x, p, tgt, gather_start, gather_pass_on, gather_finish, scatter_start, scatter_next, small):
    b_alpha, gain = small["b_alpha_up"], small["gla_head_gain"]
    gather_start(0, ())
    w = dict(gather_finish(0, ()))
    conv_w, w_alpha = w["conv_w"], w["w_alpha_up"]
    wa_p = jnp.zeros((128, HEADS * DK), BF16).at[:GATE_RANK].set(w_alpha.astype(BF16))

    t2 = gather_start(2, (w["in_rest_t"], gather_start(1, ())))
    h1 = _rms_fwd(x, small["g_pre_mix"], "rms_pre_mix")
    proj = _mm(h1, w["in_rest_t"], "nt", BF16, "mm_proj", after=(t2,))
    proj_gates = _mm(h1, w["in_gates_t"], "nt", BF16, "mm_proj_gates", after=(t2,))
    za = _conv_fwd(proj, conv_w, "conv_fwd", after=(gather_pass_on(1, (proj,)),))
    zb, states = _gla_fwd(proj, wa_p, b_alpha, gain, "gla_fwd")
    t3 = gather_start(3, (zb, za))
    w.update(gather_finish(1, (t3,)))
    ya = _mm(za, w["a_out"], "nn", BF16, "mm_ya")
    yb = _mm(zb, w["b_out"], "nn", BF16, "mm_yb")
    mix = _mix_fwd(proj_gates, ya, yb, "mix_fwd")
    m2 = _mm(mix, w["mix"], "nn", F32, "mm_mix")
    t4 = gather_start(4, (m2,))
    x1, h2 = _post_pre(x, m2, small["g_post_mix"], small["g_pre_ffn"], "norm_mix_ffn", after=(gather_pass_on(2, (m2,)),))
    w.update(gather_finish(2, (h2, t4)))
    fu = _mm(h2, w["up_t"], "nt", BF16, "mm_up")
    fg = _mm(h2, w["gate_t"], "nt", BF16, "mm_gate", after=(gather_pass_on(3, (fu,)),))
    s = _swiglu_fwd(fg, fu, "swiglu_fwd")
    w.update(gather_finish(3, (s,)))
    f = _mm(s, w["down"], "nn", F32, "mm_down", after=(gather_pass_on(4, (s,)),))
    x2, h3 = _post_pre(x1, f, small["g_post_ffn"], small["g_pre_ple"], "norm_ffn_ple")
    w.update(gather_finish(4, (h3,)))
    pg = _mm(h3, w["pg"], "nn", F32, "mm_pg")
    p_bf = p.astype(BF16)
    pp = _mm(p_bf, w["pp"], "nn", F32, "mm_pp", b3=True, tm=2048)
    loss_rows, d3, dpg, dpp, dg_post_ple = _ple_final(x2, pg, pp, tgt, small["g_post_ple"], "ple_final")

    gw = {}
    gw["pp"] = _mm(p_bf, dpp, "tn", BF16, "mm_dw_pp", out3=True)
    gw["pg"] = _mm(h3, dpg, "tn", BF16, "mm_dw_pg")
    dh3 = _mm(dpg, w["pg"], "nt", F32, "mm_dh3", after=(scatter_start(["w_ple_proj", "w_ple_gate"], gw),))
    d2, df, dg_pre_ple, dg_post_ffn = _norm_bwd(d3, dh3, x2, small["g_pre_ple"], f, small["g_post_ffn"], "norm_bwd_ple_ffn")
    gw["down"] = _mm(s, df, "tn", BF16, "mm_dw_down", tm=1408)
    ds = _mm(df, w["down"], "nt", BF16, "mm_ds", after=(scatter_start(["w_ff_down"], gw),))
    dfg, dfu = _swiglu_bwd(ds, fg, fu, "swiglu_bwd")
    gw["gate_t"] = _mm(dfg, h2, "tn", BF16, "mm_dw_gate", tm=1408)
    gw["up_t"] = _mm(dfu, h2, "tn", BF16, "mm_dw_up", after=(gw["gate_t"],), tm=1408)
    dh2 = _mm(dfg, w["gate_t"], "nn", F32, "mm_dh2_gate", after=(scatter_start(["w_ff_gate", "w_ff_up"], gw),))
    dh2 = _mm(dfu, w["up_t"], "nn", F32, "mm_dh2_up", add=dh2, tm=1024)
    d1, dm2, dg_pre_ffn, dg_post_mix = _norm_bwd(d2, dh2, x1, small["g_pre_ffn"], m2, small["g_post_mix"], "norm_bwd_ffn_mix")
    dmix = _mm(dm2, w["mix"], "nt", F32, "mm_dmix")
    gw["mix"] = _mm(mix, dm2, "tn", BF16, "mm_dw_mix")
    dgab, dya, dyb = _mix_bwd(dmix, proj_gates, ya, yb, "mix_bwd")
    gw["in_gates_t"] = _mm(dgab, h1, "tn", BF16, "mm_dw_in_gates")
    dza = _mm(dya, w["a_out"], "nt", BF16, "mm_dza", after=(scatter_start(["w_mix_out"], gw), gw["in_gates_t"]))
    gw["a_out"] = _mm(za, dya, "tn", BF16, "mm_dw_a_out")
    gw["b_out"] = _mm(zb, dyb, "tn", BF16, "mm_dw_b_out", after=(gw["a_out"],))
    dzb = _mm(dyb, w["b_out"], "nt", BF16, "mm_dzb", after=(scatter_start(["w_a_out", "w_b_out"], gw),))
    dax, dab, dac, dconv = _conv_bwd(dza, proj, conv_w, "conv_bwd")
    dq, dk, dv, dog, dalr, dwa, dba, dgain = _gla_bwd(dzb, proj, states, wa_p, b_alpha, gain, "gla_bwd")
    drest = jnp.concatenate([dax, dab, dac, dq, dk, dv, dog, dalr.astype(BF16)], axis=1)
    gw["in_rest_t"] = _mm(drest, h1, "tn", BF16, "mm_dw_in_rest", tm=896)
    dh1 = _mm(dgab, w["in_gates_t"], "nn", F32, "mm_dh1_gates", after=(scatter_start(["w_in"], gw),), tm=1024)
    dh1 = _mm(drest, w["in_rest_t"], "nn", F32, "mm_dh1_rest", add=dh1, tm=1024, after=(scatter_next((dh1,)),))
    grad_x, dg_pre_mix = _norm_bwd(d1, dh1, x, small["g_pre_mix"], None, None, "norm_bwd_mix")

    gs = dict(
        conv_w=dconv,
        w_alpha_up=jnp.transpose(dwa[:, :GATE_RANK, :], (1, 0, 2)).reshape(GATE_RANK, HEADS * DK),
        b_alpha_up=dba.reshape(1, HEADS * DK), gla_head_gain=dgain,
        g_pre_mix=dg_pre_mix, g_post_mix=dg_post_mix, g_pre_ffn=dg_pre_ffn, g_post_ffn=dg_post_ffn,
        g_pre_ple=dg_pre_ple, g_post_ple=dg_post_ple,
    )
    return loss_rows, grad_x, gs


def _place():
    x, y, c = lax.axis_index("x"), lax.axis_index("y"), lax.axis_index("c")
    return x, y, c, [(1 - x, y), (x, 1 - y), (1 - x, 1 - y)]


def _all_gather(shards, name, cid=None):
    n = len(shards)

    def body(*refs):
        ins, outs = refs[:n], refs[n:2 * n]
        send_sems, recv_sems, local_sems = refs[2 * n:]
        x, y, c, chips = _place()
        me, sibling = (x, y, c), (x, y, 1 - c)

        def slot(px, py, pc):
            return 4 * px + 2 * py + pc

        def copy(a, k, block, to, src=None):
            dst = outs[a].at[slot(*block)]
            return pltpu.make_async_remote_copy(src_ref=dst if src is None else src, dst_ref=dst, send_sem=send_sems.at[a, k],
                                                recv_sem=recv_sems.at[a, k], device_id=to, device_id_type=MESH)

        mine = [pltpu.make_async_copy(ins[a], outs[a].at[slot(*me)], local_sems.at[a]) for a in range(n)]
        for cp in mine:
            cp.start()
        first = []
        for j, chip in enumerate(chips):
            first += [copy(a, 1 + j, me, (*chip, c), src=ins[a]) for a in range(n)]
        first += [copy(a, 0, me, sibling, src=ins[a]) for a in range(n)]
        for cp in first:
            cp.start()
        passed = []
        for j, chip in enumerate(chips):
            for a in range(n):
                copy(a, 1 + j, (*chip, c), me).wait_recv()
                cp = copy(a, 4 + j, (*chip, c), sibling)
                cp.start()
                passed.append(cp)
        for a in range(n):
            copy(a, 0, sibling, me).wait_recv()
        for j, chip in enumerate(chips):
            for a in range(n):
                copy(a, 4 + j, (*chip, 1 - c), me).wait_recv()
        for cp in first + passed:
            cp.wait_send()
        for cp in mine:
            cp.wait()

    if cid is None:
        return pl.pallas_call(
            body, name=name, in_specs=[ANY] * n, out_specs=[ANY] * n,
            out_shape=[jax.ShapeDtypeStruct((N_DEV,) + s.shape, s.dtype) for s in shards],
            scratch_shapes=[pltpu.SemaphoreType.DMA((n, 7)), pltpu.SemaphoreType.DMA((n, 7)), pltpu.SemaphoreType.DMA((n,))],
        )(*shards)

    src = [jax.new_ref(s, memory_space=pltpu.MemorySpace.HBM) for s in shards]
    dst = [jax.empty_ref(jax.ShapeDtypeStruct((N_DEV,) + s.shape, s.dtype), memory_space=pltpu.MemorySpace.HBM) for s in shards]

    @pl.kernel(mesh=plsc.ScalarSubcoreMesh(axis_name="seq", num_cores=1), name=name,
               scratch_types=(pltpu.SemaphoreType.DMA((n, 7)), pltpu.SemaphoreType.DMA((n, 7)), pltpu.SemaphoreType.DMA((n,))),
               compiler_params=pltpu.CompilerParams(collective_id=cid))
    def launch(send_sems, recv_sems, local_sems):
        x, y, c, chips = _place()
        barrier = pltpu.get_barrier_semaphore()
        for peer in [(x, y, 1 - c)] + [(*chip, c) for chip in chips]:
            pl.semaphore_signal(barrier, inc=1, device_id=peer, device_id_type=MESH)
        pl.semaphore_wait(barrier, 4)
        body(*src, *dst, send_sems, recv_sems, local_sems)

    launch()
    return [r[...] for r in dst]


def _reduce_scatter(parts, name, cid):
    n = len(parts)
    src = [jax.new_ref(s, memory_space=pltpu.MemorySpace.HBM) for s in parts]
    dst = [jax.empty_ref(jax.ShapeDtypeStruct(s.shape, s.dtype), memory_space=pltpu.MemorySpace.HBM) for s in parts]

    @pl.kernel(mesh=plsc.ScalarSubcoreMesh(axis_name="seq", num_cores=1), name=name,
               scratch_types=(pltpu.SemaphoreType.DMA((n, N_DEV - 1)), pltpu.SemaphoreType.DMA((n, N_DEV - 1)), pltpu.SemaphoreType.DMA((n,))),
               compiler_params=pltpu.CompilerParams(collective_id=cid))
    def launch(send_sems, recv_sems, local_sems):
        x, y, c, _ = _place()
        me = 4 * x + 2 * y + c
        peers = [(1 - x if k & 4 else x, 1 - y if k & 2 else y, 1 - c if k & 1 else c) for k in range(1, N_DEV)]
        barrier = pltpu.get_barrier_semaphore()
        for peer in peers:
            pl.semaphore_signal(barrier, inc=1, device_id=peer, device_id_type=MESH)
        pl.semaphore_wait(barrier, N_DEV - 1)
        mine = [pltpu.make_async_copy(src[a].at[me], dst[a].at[me], local_sems.at[a]) for a in range(n)]
        for cp in mine:
            cp.start()
        cps = []
        for a in range(n):
            for k, (px, py, pc) in enumerate(peers):
                cps.append(pltpu.make_async_remote_copy(src_ref=src[a].at[4 * px + 2 * py + pc], dst_ref=dst[a].at[me], send_sem=send_sems.at[a, k],
                                                        recv_sem=recv_sems.at[a, k], device_id=(px, py, pc), device_id_type=MESH))
        for cp in cps:
            cp.start()
        for cp in cps:
            cp.wait_recv()
        for cp in cps:
            cp.wait_send()
        for cp in mine:
            cp.wait()

    launch()
    return [r[...] for r in dst]


def _sibling_exchange(parts, name):
    n = len(parts)
    pieces = [_row_pieces(s.shape[1]) for s in parts]

    def body(*refs):
        ins, outs = refs[:n], refs[n:2 * n]
        send_sems, recv_sems = refs[2 * n:]
        x, y, c, _ = _place()

        def copy(a, ch, q, rows):
            return pltpu.make_async_remote_copy(src_ref=ins[a].at[2 * ch + 1 - c, rows], dst_ref=outs[a].at[ch, rows], send_sem=send_sems.at[a, ch, q],
                                                recv_sem=recv_sems.at[a, ch, q], device_id=(x, y, 1 - c), device_id_type=MESH)

        cps = [copy(a, ch, q, rows) for ch in range(4) for a in range(n) for q, rows in enumerate(pieces[a])]
        for cp in cps:
            cp.start()
        for cp in cps:
            cp.wait_recv()
        for cp in cps:
            cp.wait_send()

    return pl.pallas_call(
        body, name=name, in_specs=[ANY] * n, out_specs=[ANY] * n,
        out_shape=[jax.ShapeDtypeStruct((4,) + s.shape[1:], s.dtype) for s in parts],
        scratch_shapes=[pltpu.SemaphoreType.DMA((n, 4, PIECES)), pltpu.SemaphoreType.DMA((n, 4, PIECES))],
    )(*parts)


def _chip_exchange(parts, name):
    n = len(parts)

    def body(*refs):
        ins, outs = refs[:n], refs[n:2 * n]
        send_sems, recv_sems, local_sems = refs[2 * n:]
        x, y, c, chips = _place()
        my_chip = 2 * x + y

        def copy(a, j):
            px, py = chips[j]
            return pltpu.make_async_remote_copy(src_ref=ins[a].at[2 * px + py], dst_ref=outs[a].at[my_chip], send_sem=send_sems.at[a, j],
                                                recv_sem=recv_sems.at[a, j], device_id=(px, py, c), device_id_type=MESH)

        def landing(a, j):
            px, py = chips[j]
            return pltpu.make_async_remote_copy(src_ref=ins[a].at[my_chip], dst_ref=outs[a].at[2 * px + py], send_sem=send_sems.at[a, j],
                                                recv_sem=recv_sems.at[a, j], device_id=(px, py, c), device_id_type=MESH)

        mine = [pltpu.make_async_copy(ins[a].at[my_chip], outs[a].at[my_chip], local_sems.at[a]) for a in range(n)]
        for cp in mine:
            cp.start()
        cps = [copy(a, j) for j in range(3) for a in range(n)]
        for cp in cps:
            cp.start()
        for j in range(3):
            for a in range(n):
                landing(a, j).wait_recv()
        for cp in cps:
            cp.wait_send()
        for cp in mine:
            cp.wait()

    return pl.pallas_call(
        body, name=name, in_specs=[ANY] * n, out_specs=[ANY] * n,
        out_shape=[jax.ShapeDtypeStruct(s.shape, s.dtype) for s in parts],
        scratch_shapes=[pltpu.SemaphoreType.DMA((n, 3)), pltpu.SemaphoreType.DMA((n, 3)), pltpu.SemaphoreType.DMA((n,))],
    )(*parts)


def _pair_add(mine8, got4, name):
    _, r, cols = mine8.shape
    steps, blk, at = _tiles(r, cols)
    core = lax.axis_index("c").astype(jnp.int32).reshape(1)

    def body(c_ref, a_ref, b_ref, o_ref):
        o_ref[...] = (a_ref[...].astype(F32) + b_ref[...].astype(F32)).astype(BF16)

    return pl.pallas_call(
        body, name=name,
        grid_spec=pltpu.PrefetchScalarGridSpec(
            num_scalar_prefetch=1, grid=(4, steps),
            in_specs=[pl.BlockSpec((None,) + blk, lambda ch, i, c_ref: (2 * ch + c_ref[0],) + at(i)),
                      pl.BlockSpec((None,) + blk, lambda ch, i, c_ref: (ch,) + at(i))],
            out_specs=pl.BlockSpec((None,) + blk, lambda ch, i, c_ref: (ch,) + at(i))),
        out_shape=jax.ShapeDtypeStruct((4, r, cols), BF16),
        compiler_params=_params(("parallel", "parallel")),
    )(core, mine8, got4)


HBM = pl.BlockSpec(memory_space=pltpu.HBM)
SEM = pl.BlockSpec(memory_space=pltpu.SEMAPHORE)
EFFECT = pltpu.SideEffectType.DATAFLOW_SIDE_EFFECTING


def _in_hbm(a):
    return pltpu.with_memory_space_constraint(a, pltpu.HBM)


def _remote_copies(plan, srcs, lands, send_sems, recv_sems):
    return [pltpu.make_async_remote_copy(src_ref=s, dst_ref=d, send_sem=send_sems.at[i], recv_sem=recv_sems.at[i], device_id=peer,
                                         device_id_type=MESH) for i, (s, d, peer) in enumerate(plan(srcs, lands))]


def _copies_start(plan, n_copies, srcs, land_shapes, name, after=()):
    ns, nl = len(srcs), len(land_shapes)

    def body(*refs):
        send_sems, recv_sems = refs[ns + nl + len(after):ns + nl + len(after) + 2]
        for cp in _remote_copies(plan, refs[:ns], refs[ns:ns + nl], send_sems, recv_sems):
            cp.start()
        refs[-1][...] = jnp.zeros((8, 128), F32)

    sems = pltpu.SemaphoreType.DMA((n_copies,))
    return pl.pallas_call(
        body, name=name,
        out_shape=(sems, sems, *[pltpu.HBM(s.shape, s.dtype) for s in srcs], *[pltpu.HBM(s.shape, s.dtype) for s in land_shapes],
                   jax.ShapeDtypeStruct((8, 128), F32)),
        in_specs=[HBM] * (ns + nl) + [ANY] * len(after),
        out_specs=(SEM, SEM, *[HBM] * (ns + nl), pl.BlockSpec(memory_space=pltpu.VMEM)),
        input_output_aliases={i: 2 + i for i in range(ns + nl)},
        compiler_params=pltpu.CompilerParams(has_side_effects=EFFECT),
    )(*[_in_hbm(s) for s in srcs], *[_in_hbm(lax.empty(s.shape, s.dtype)) for s in land_shapes], *after)


def _copies_wait(plan, state, ns, name, after=()):
    send_sems, recv_sems, *arrs = state[:-1]
    n = len(arrs)

    def body(*refs):
        cps = _remote_copies(plan, refs[:ns], refs[ns:n], refs[n], refs[n + 1])
        for cp in cps:
            cp.wait_send()
        for cp in cps:
            cp.wait_recv()

    out = pl.pallas_call(
        body, name=name, out_shape=tuple(pltpu.HBM(a.shape, a.dtype) for a in arrs),
        in_specs=[HBM] * n + [SEM, SEM] + [ANY] * len(after), out_specs=tuple([HBM] * n),
        input_output_aliases={i: i for i in range(n)},
        compiler_params=pltpu.CompilerParams(has_side_effects=EFFECT),
    )(*arrs, send_sems, recv_sems, *after)
    return list(out[:ns]), list(out[ns:])


def _copies_relay(plan, state, ns, next_plan, n_next, name, after=()):
    send_sems, recv_sems, *arrs = state[:-1]
    n = len(arrs)

    def body(*refs):
        cps = _remote_copies(plan, refs[:ns], refs[ns:n], refs[n], refs[n + 1])
        for cp in cps:
            cp.wait_send()
        for cp in cps:
            cp.wait_recv()
        outs = refs[n + 2 + len(after):]
        for cp in _remote_copies(next_plan, refs[:ns], refs[ns:n], outs[0], outs[1]):
            cp.start()
        outs[-1][...] = jnp.zeros((8, 128), F32)

    sems = pltpu.SemaphoreType.DMA((n_next,))
    return pl.pallas_call(
        body, name=name,
        out_shape=(sems, sems, *[pltpu.HBM(a.shape, a.dtype) for a in arrs], jax.ShapeDtypeStruct((8, 128), F32)),
        in_specs=[HBM] * n + [SEM, SEM] + [ANY] * len(after),
        out_specs=(SEM, SEM, *[HBM] * n, pl.BlockSpec(memory_space=pltpu.VMEM)),
        input_output_aliases={i: 2 + i for i in range(n)},
        compiler_params=pltpu.CompilerParams(has_side_effects=EFFECT),
    )(*arrs, send_sems, recv_sems, *after)


def _pass_on_plan(srcs, lands):
    x, y, c, chips = _place()
    return [(l.at[4 * px + 2 * py + c], l.at[4 * px + 2 * py + c], (x, y, 1 - c)) for l in lands for px, py in chips]


def _gather_plan(srcs, lands):
    x, y, c, chips = _place()
    peers = [(x, y, 1 - c)] + [(*chip, c) for chip in chips]
    return [(s, l.at[4 * x + 2 * y + c], peer) for s, l in zip(srcs, lands) for peer in peers]


def _gather_plan_near(srcs, lands):
    x, y, c, _ = _place()
    peers = [(x, y, 1 - c), (1 - x, y, c), (x, 1 - y, c)]
    return [(s, l.at[4 * x + 2 * y + c], peer) for s, l in zip(srcs, lands) for peer in peers]


def _scatter_plan(srcs, lands):
    x, y, c, _ = _place()
    peers = [(1 - x if k & 4 else x, 1 - y if k & 2 else y, 1 - c if k & 1 else c) for k in range(1, N_DEV)]
    return [(s.at[4 * px + 2 * py + pc], l.at[4 * x + 2 * y + c], (px, py, pc)) for s, l in zip(srcs, lands) for px, py, pc in peers]


def _everyone_plan(srcs, lands):
    x, y, c, _ = _place()
    peers = [(1 - x if k & 4 else x, 1 - y if k & 2 else y, 1 - c if k & 1 else c) for k in range(1, N_DEV)]
    return [(s, l.at[4 * x + 2 * y + c], peer) for s, l in zip(srcs, lands) for peer in peers]


def _sum_parts(got, own, me, name):
    def body(me_ref, got_ref, own_ref, o_ref):
        acc = jnp.where(me_ref[0] == 0, own_ref[...], got_ref[0])
        for d in range(1, N_DEV):
            acc = acc + jnp.where(me_ref[0] == d, own_ref[...], got_ref[d])
        o_ref[...] = acc

    return pl.pallas_call(
        body, name=name,
        grid_spec=pltpu.PrefetchScalarGridSpec(
            num_scalar_prefetch=1, grid=(1,),
            in_specs=[pl.BlockSpec(got.shape, lambda i, me_ref: (0, 0, 0)), pl.BlockSpec(own.shape, lambda i, me_ref: (0, 0))],
            out_specs=pl.BlockSpec(own.shape, lambda i, me_ref: (0, 0))),
        out_shape=jax.ShapeDtypeStruct(own.shape, F32),
    )(me.astype(jnp.int32).reshape(1), got, own)


def _sibling_plan(srcs, lands):
    x, y, c, _ = _place()
    return [(s.at[2 * ch + 1 - c], l.at[ch], (x, y, 1 - c)) for s, l in zip(srcs, lands) for ch in range(4)]


def _chip_plan(srcs, lands):
    x, y, c, chips = _place()
    return [(s.at[2 * px + py], l.at[2 * x + y], (px, py, c)) for s, l in zip(srcs, lands) for px, py in chips]


PIECES = 8


def _row_pieces(rows):
    for k in (PIECES, 4, 2):
        if rows % (16 * k) == 0:
            return [pl.ds(q * (rows // k), rows // k) for q in range(k)]
    return [pl.ds(0, rows)]


def _put_own(shard, zone, me, name):
    r, c = shard.shape
    tr = r if r <= 256 else _pick(r, (256, 64))

    def body(me_ref, s_ref, z_ref, o_ref):
        o_ref[...] = s_ref[...]

    return pl.pallas_call(
        body, name=name,
        grid_spec=pltpu.PrefetchScalarGridSpec(
            num_scalar_prefetch=1, grid=(r // tr,),
            in_specs=[pl.BlockSpec((tr, c), lambda i, me_ref: (i, 0)), ANY],
            out_specs=pl.BlockSpec((None, tr, c), lambda i, me_ref: (me_ref[0], i, 0))),
        out_shape=jax.ShapeDtypeStruct(zone.shape, zone.dtype), input_output_aliases={2: 0},
        compiler_params=_params(("arbitrary",)),
    )(me.astype(jnp.int32).reshape(1), shard, zone)


def _gather_finish(lands, name):
    n = len(lands)

    def body(*refs):
        zones, outs = refs[:n], refs[n:2 * n]
        send_sems, recv_sems = refs[2 * n:]
        x, y, c, chips = _place()
        cps = [pltpu.make_async_remote_copy(
            src_ref=zones[a].at[4 * px + 2 * py + c], dst_ref=outs[a].at[4 * px + 2 * py + c], send_sem=send_sems.at[a, j],
            recv_sem=recv_sems.at[a, j], device_id=(x, y, 1 - c), device_id_type=MESH) for j, (px, py) in enumerate(chips) for a in range(n)]
        for cp in cps:
            cp.start()
        for cp in cps:
            cp.wait_recv()
        for cp in cps:
            cp.wait_send()

    return pl.pallas_call(
        body, name=name, in_specs=[ANY] * n, out_specs=[ANY] * n,
        out_shape=[jax.ShapeDtypeStruct(l.shape, l.dtype) for l in lands],
        input_output_aliases={a: a for a in range(n)},
        scratch_shapes=[pltpu.SemaphoreType.DMA((n, 3)), pltpu.SemaphoreType.DMA((n, 3))],
    )(*lands)


def _gather_relay(lands, name):
    n = len(lands)

    def body(*refs):
        zones, outs = refs[:n], refs[n:2 * n]
        send_sems, recv_sems = refs[2 * n:]
        x, y, c, _ = _place()
        south = c == 0
        near_x, near_y, across = 4 * (1 - x) + 2 * y + c, 4 * x + 2 * (1 - y) + c, 4 * (1 - x) + 2 * (1 - y) + c
        passed = jnp.where(south, near_y, near_x)
        onward = (jnp.where(south, 1 - x, x), jnp.where(south, y, 1 - y), c)

        def copy(a, k, slot, to):
            return pltpu.make_async_remote_copy(src_ref=zones[a].at[slot], dst_ref=outs[a].at[slot], send_sem=send_sems.at[a, k],
                                                recv_sem=recv_sems.at[a, k], device_id=to, device_id_type=MESH)

        first = [copy(a, 0, passed, onward) for a in range(n)]
        first += [copy(a, 1 + j, slot, (x, y, 1 - c)) for j, slot in enumerate((near_x, near_y)) for a in range(n)]
        for cp in first:
            cp.start()
        last = []
        for a in range(n):
            copy(a, 0, across, onward).wait_recv()
            last.append(copy(a, 3, across, (x, y, 1 - c)))
            last[-1].start()
        for cp in first[n:] + last:
            cp.wait_recv()
        for cp in first + last:
            cp.wait_send()

    return pl.pallas_call(
        body, name=name, in_specs=[ANY] * n, out_specs=[ANY] * n,
        out_shape=[jax.ShapeDtypeStruct(l.shape, l.dtype) for l in lands],
        input_output_aliases={a: a for a in range(n)},
        scratch_shapes=[pltpu.SemaphoreType.DMA((n, 4)), pltpu.SemaphoreType.DMA((n, 4))],
    )(*lands)


def _sum_everywhere(v, name):
    rows = v.shape[0]

    def body(v_ref, o_ref, buf, send_sems, recv_sems):
        x, y, c, _ = _place()
        me = 4 * x + 2 * y + c
        buf[me] = v_ref[...]
        cps = []
        for k in range(1, N_DEV):
            fx, fy, fc = (k >> 2) & 1, (k >> 1) & 1, k & 1
            to = (1 - x if fx else x, 1 - y if fy else y, 1 - c if fc else c)
            cps.append(pltpu.make_async_remote_copy(src_ref=buf.at[me], dst_ref=buf.at[me], send_sem=send_sems.at[k - 1],
                                                    recv_sem=recv_sems.at[k - 1], device_id=to, device_id_type=MESH))
        for cp in cps:
            cp.start()
        for cp in cps:
            cp.wait_recv()
        for cp in cps:
            cp.wait_send()
        acc = buf[0]
        for d in range(1, N_DEV):
            acc = acc + buf[d]
        o_ref[...] = acc

    vm = pl.BlockSpec(memory_space=pltpu.VMEM)
    return pl.pallas_call(
        body, name=name, in_specs=[vm], out_specs=vm, out_shape=jax.ShapeDtypeStruct(v.shape, F32),
        scratch_shapes=[pltpu.VMEM((N_DEV, rows, 128), F32), pltpu.SemaphoreType.DMA((N_DEV - 1,)), pltpu.SemaphoreType.DMA((N_DEV - 1,))],
    )(v)


def _adamw_parts(w, got, mine, me, m, v, name, after=()):
    r, c = w.shape
    n_parts = got.shape[0]
    steps, blk, at = _tiles(r, c)

    def body(me_ref, w_ref, got_ref, own_ref, m_ref, v_ref, *rest):
        go_ref, d_ref, mo_ref, vo_ref = rest[len(after):]
        own = own_ref[...].astype(F32)
        gv = jnp.where(me_ref[0] == 0, own, got_ref[0].astype(F32))
        for d in range(1, n_parts):
            gv = gv + jnp.where(me_ref[0] == d, own, got_ref[d].astype(F32))
        _adamw_math(gv, w_ref, m_ref, v_ref, go_ref, d_ref, mo_ref, vo_ref)

    tile = pl.BlockSpec(blk, lambda i, me_ref: at(i))
    out = jax.ShapeDtypeStruct((r, c), F32)
    return pl.pallas_call(
        body, name=name,
        grid_spec=pltpu.PrefetchScalarGridSpec(
            num_scalar_prefetch=1, grid=(steps,),
            in_specs=[tile, pl.BlockSpec((n_parts,) + blk, lambda i, me_ref: (0,) + at(i)),
                      pl.BlockSpec((None,) + blk, lambda i, me_ref: (me_ref[0],) + at(i)), tile, tile] + [ANY] * len(after),
            out_specs=[tile] * 4),
        out_shape=[out] * 4, compiler_params=_params(("parallel",)),
    )(me.astype(jnp.int32).reshape(1), w, got, mine, m, v, *after)


def _adamw_math(gv, w_ref, m_ref, v_ref, go_ref, d_ref, mo_ref, vo_ref):
    mn = B1 * m_ref[...] + (1.0 - B1) * gv
    vn = B2 * v_ref[...] + (1.0 - B2) * (gv * gv)
    m_hat = mn / (1.0 - B1 ** STEP)
    v_hat = vn / (1.0 - B2 ** STEP)
    go_ref[...] = gv
    d_ref[...] = -LR * (m_hat / (jnp.sqrt(v_hat) + ADAM_EPS) + WD * w_ref[...])
    mo_ref[...] = mn
    vo_ref[...] = vn


def _adamw(w, g, m, v, name):
    r, c = w.shape
    parts = g.ndim == 3
    tr = r if r <= 128 else _pick(r, (128, 64))

    def body(w_ref, g_ref, m_ref, v_ref, go_ref, d_ref, mo_ref, vo_ref):
        if parts:
            gv = g_ref[0].astype(F32)
            for d in range(1, g.shape[0]):
                gv = gv + g_ref[d].astype(F32)
        else:
            gv = g_ref[...]
        mn = B1 * m_ref[...] + (1.0 - B1) * gv
        vn = B2 * v_ref[...] + (1.0 - B2) * (gv * gv)
        m_hat = mn / (1.0 - B1 ** STEP)
        v_hat = vn / (1.0 - B2 ** STEP)
        go_ref[...] = gv
        d_ref[...] = -LR * (m_hat / (jnp.sqrt(v_hat) + ADAM_EPS) + WD * w_ref[...])
        mo_ref[...] = mn
        vo_ref[...] = vn

    tile = pl.BlockSpec((tr, c), lambda i: (i, 0))
    g_spec = pl.BlockSpec((g.shape[0], tr, c), lambda i: (0, i, 0)) if parts else tile
    out = jax.ShapeDtypeStruct((r, c), F32)
    return pl.pallas_call(
        body, name=name, grid=(r // tr,), in_specs=[tile, g_spec, tile, tile], out_specs=[tile] * 4, out_shape=[out] * 4,
        compiler_params=_params(("parallel",)),
    )(w, g, m, v)


BIG = ["w_in", "w_a_out", "w_b_out", "w_mix_out", "w_ff_gate", "w_ff_up", "w_ff_down", "w_ple_gate", "w_ple_proj"]
TRANSPOSED = ["w_in", "w_ff_gate", "w_ff_up"]
GRAD_OF = dict(w_ple_proj="pp", w_ple_gate="pg", w_ff_down="down", w_ff_gate="gate_t", w_ff_up="up_t", w_mix_out="mix", w_a_out="a_out",
               w_b_out="b_out")
SMALL = ["conv_w", "w_alpha_up", "b_alpha_up", "gla_head_gain", "g_pre_mix", "g_post_mix", "g_pre_ffn", "g_post_ffn", "g_pre_ple", "g_post_ple"]
WEIGHTS = ["w_in", "conv_w", "w_a_out", "w_alpha_up", "b_alpha_up", "gla_head_gain", "w_b_out", "w_mix_out", "g_pre_mix", "g_post_mix",
           "g_pre_ffn", "g_post_ffn", "w_ff_gate", "w_ff_up", "w_ff_down", "g_pre_ple", "g_post_ple", "w_ple_gate", "w_ple_proj"]


def _in_t_from_blocks(z):
    w = z.reshape(-1, z.shape[-1])
    return w[R_GA:R_END], jnp.concatenate([w[:R_GA], jnp.zeros((REST - R_GA, w.shape[1]), w.dtype)], axis=0)


def _blocks_from_in_t(g_gates, g_rest):
    per = R_END // N_DEV

    def rows(lo, hi):
        out = []
        for n0, n1, g in ((0, R_GA, g_rest), (R_GA, R_END, g_gates)):
            a, e = max(lo, n0), min(hi, n1)
            if a < e:
                out.append(g[a - n0:e - n0])
        return out

    return jnp.stack([jnp.concatenate(rows(b * per, (b + 1) * per), axis=0) for b in range(N_DEV)])


def _cols_to_full(g8):
    n, r, c = g8.shape
    return jnp.transpose(g8, (1, 0, 2)).reshape(r, n * c)


def _full_to_cols(a):
    r, c = a.shape
    return jnp.transpose(a.reshape(r, N_DEV, c // N_DEV), (1, 0, 2))


def _pack(arrs, rows):
    flat = jnp.concatenate([a.reshape(-1) for a in arrs])
    return jnp.pad(flat, (0, rows * 128 - flat.shape[0])).reshape(rows, 128)


def _unpack(packed, shapes):
    flat, out, o = packed.reshape(-1), [], 0
    for s in shapes:
        size = 1
        for d in s:
            size *= d
        out.append(flat[o:o + size].reshape(s))
        o += size
    return out


def kernel(x, p, w_in, conv_w, w_a_out, w_alpha_up, b_alpha_up, gla_head_gain, w_b_out, w_mix_out, g_pre_mix, g_post_mix, g_pre_ffn, g_post_ffn, w_ff_gate, w_ff_up, w_ff_down, g_pre_ple, g_post_ple, w_ple_gate, w_ple_proj, loss_target, m_w_in, m_conv_w, m_w_a_out, m_w_alpha_up, m_b_alpha_up, m_gla_head_gain, m_w_b_out, m_w_mix_out, m_g_pre_mix, m_g_post_mix, m_g_pre_ffn, m_g_post_ffn, m_w_ff_gate, m_w_ff_up, m_w_ff_down, m_g_pre_ple, m_g_post_ple, m_w_ple_gate, m_w_ple_proj, v_w_in, v_conv_w, v_w_a_out, v_w_alpha_up, v_b_alpha_up, v_gla_head_gain, v_w_b_out, v_w_mix_out, v_g_pre_mix, v_g_post_mix, v_g_pre_ffn, v_g_post_ffn, v_w_ff_gate, v_w_ff_up, v_w_ff_down, v_g_pre_ple, v_g_post_ple, v_w_ple_gate, v_w_ple_proj):
    args = dict(locals())
    shard = lambda n, a: jnp.transpose(a[0]) if n in TRANSPOSED else a[0]
    wts = {n: shard(n, args[n]) for n in WEIGHTS}
    mom = {n: shard(n, args["m_" + n]) for n in WEIGHTS}
    var = {n: shard(n, args["v_" + n]) for n in WEIGHTS}
    me =4 * lax.axis_index("x") + 2 * lax.axis_index("y") + lax.axis_index("c")

    groups = [["w_in", "conv_w", "w_alpha_up"], ["w_a_out", "w_b_out", "w_mix_out"], ["w_ff_gate", "w_ff_up"], ["w_ff_down"],
              ["w_ple_gate", "w_ple_proj"]]
    grad_groups = []
    rows_full = lambda g: g.reshape(-1, g.shape[-1])
    gathers, scatters = {}, {}

    def gather_start(gi, after):
        if gi not in gathers:
            shards = [wts[n].astype(BF16) if n in BIG else wts[n] for n in groups[gi]]
            zones = [jax.ShapeDtypeStruct((N_DEV,) + s.shape, s.dtype) for s in shards]
            plan, peers = (_gather_plan_near, 3) if gi == 0 else (_gather_plan, 4)
            gathers[gi] = (shards, _copies_start(plan, peers * len(shards), shards, zones, "gather_start_%d" % gi, after))
        return gathers[gi][1][-1]

    def gather_pass_on(gi, after):
        shards, state = gathers[gi]
        gathers[gi] = (shards, _copies_relay(_gather_plan, state, len(shards), _pass_on_plan, 3 * len(shards), "gather_pass_on_%d" % gi, after))
        return gathers[gi][1][-1]

    def gather_finish(gi, after):
        shards, state = gathers[gi]
        shards, zones = _copies_wait(_gather_plan_near if gi == 0 else _pass_on_plan, state, len(shards), "gather_wait_%d" % gi, after)
        if gi == 0:
            zones = _gather_relay(zones, "gather_relay_%d" % gi)
            gather_start(1, (zones[0],))
        g8 = {n: _put_own(s, z, me, "gather_own_" + n) for n, s, z in zip(groups[gi], shards, zones)}
        if gi == 0:
            in_gates_t, in_rest_t = _in_t_from_blocks(g8["w_in"])
            return dict(in_gates_t=in_gates_t, in_rest_t=in_rest_t,
                        conv_w=_cols_to_full(g8["conv_w"]), w_alpha_up=_cols_to_full(g8["w_alpha_up"]))
        if gi == 1:
            return dict(a_out=_cols_to_full(g8["w_a_out"]), b_out=_cols_to_full(g8["w_b_out"]), mix=rows_full(g8["w_mix_out"]))
        if gi == 2:
            return dict(gate_t=rows_full(g8["w_ff_gate"]), up_t=rows_full(g8["w_ff_up"]))
        if gi == 3:
            return dict(down=rows_full(g8["w_ff_down"]))
        return dict(pg=rows_full(g8["w_ple_gate"]), pp=g8["w_ple_proj"])

    def scatter_start(names, gw):
        gi = len(grad_groups)
        grad_groups.append(names)
        full = {n: _blocks_from_in_t(gw["in_gates_t"], gw["in_rest_t"]) if n == "w_in" else gw[GRAD_OF[n]] for n in names}
        for n in names:
            if n in ("w_a_out", "w_b_out"):
                full[n] = _full_to_cols(full[n])
        parts = [full[n] if full[n].ndim == 3 else full[n].reshape(N_DEV, -1, full[n].shape[-1]) for n in names]
        if names == ["w_in"]:
            quarter = [jax.ShapeDtypeStruct((4,) + a.shape[1:], a.dtype) for a in parts]
            scatters[gi] = _copies_start(_sibling_plan, 4 * len(parts), parts, quarter, "scatter_sibling_start_%d" % gi)
        else:
            scatters[gi] = _copies_start(_scatter_plan, (N_DEV - 1) * len(parts), parts, parts, "scatter_start_%d" % gi)
        return scatters[gi][-1]

    def scatter_next(after):
        gi, names = len(grad_groups) - 1, grad_groups[-1]
        parts, from_sibling = _copies_wait(_sibling_plan, scatters[gi], len(names), "scatter_sibling_wait_%d" % gi, after)
        parts = [_pair_add(a, b, "scatter_add_%d_%s" % (gi, n)) for n, a, b in zip(names, parts, from_sibling)]
        scatters[gi] = _copies_start(_chip_plan, 3 * len(parts), parts, parts, "scatter_start_%d" % gi)
        return scatters[gi][-1]

    small = {n: wts[n].reshape(1, -1) for n in SMALL[2:]}

    loss_rows, grad_x, gs = _local_step(x[0], p[0, 0], loss_target[0], gather_start, gather_pass_on, gather_finish, scatter_start,
                                        scatter_next, small)
    gs["loss"] = jnp.sum(loss_rows).reshape(1, 1)

    small_shapes = [gs[n].shape for n in SMALL]
    gs_packed = _pack([gs[n] for n in SMALL + ["loss"]], 192)
    small_state = _copies_start(_everyone_plan, N_DEV - 1, [gs_packed], [jax.ShapeDtypeStruct((N_DEV,) + gs_packed.shape, F32)],
                                "small_start", (grad_x,))

    res, done = {}, (small_state[-1],)
    for gi, names in enumerate(grad_groups):
        plan, slot = (_chip_plan, me // 2) if names == ["w_in"] else (_scatter_plan, me)
        mine, got = _copies_wait(plan, scatters[gi], len(names), "scatter_wait_%d" % gi, done)
        for n, g, own in zip(names, got, mine):
            res[n] = _adamw_parts(wts[n], g, own, slot, mom[n], var[n], "adamw_" + n)
        done = tuple(res[n][1] for n in names)

    (gs_own,), (gs_got,) = _copies_wait(_everyone_plan, small_state, 1, "small_wait", done)
    gsum = dict(zip(SMALL + ["loss"], _unpack(_sum_parts(gs_got, gs_own, me, "small_sum"), small_shapes + [(1, 1)])))
    loss = gsum["loss"].reshape(())
    gsum["conv_w"] = lax.dynamic_index_in_dim(gsum["conv_w"].reshape(3, N_DEV, -1), me, axis=1, keepdims=False)
    gsum["w_alpha_up"] = lax.dynamic_index_in_dim(gsum["w_alpha_up"].reshape(GATE_RANK, N_DEV, -1), me, axis=1, keepdims=False)

    shard_shapes = [wts[n].shape for n in SMALL]
    packed = [_pack([d[n] for n in SMALL], 120) for d in (wts, gsum, mom, var)]
    outs = [_unpack(o, shard_shapes) for o in _adamw(*packed, "adamw_small")]
    for i, n in enumerate(SMALL):
        res[n] = [o[i] for o in outs]

    back = lambda n, a: (jnp.transpose(a) if n in TRANSPOSED else a)[None]
    return (loss, grad_x[None], *[back(n, res[n][i]) for i in range(4) for n in WEIGHTS])
```

```python
import functools

import jax
import jax.numpy as jnp
from jax import lax
from jax.experimental import pallas as pl
from jax.experimental.pallas import tpu as pltpu
from jax.experimental.pallas import tpu_sc as plsc

F32, BF16 = jnp.float32, jnp.bfloat16
EPS = 1e-6
CHUNK = 64
STEP_CHUNKS = 2
STEP_ROWS = STEP_CHUNKS * CHUNK
HEADS, DK, DV = 4, 128, 256
GATE_RANK = 16
TAU = 16.0
LR, B1, B2, ADAM_EPS, WD, STEP = 0.001, 0.9, 0.999, 1e-08, 0.01, 10
N_DEV = 8
MESH = pl.DeviceIdType.MESH
VMEM_LIMIT = 56 * 1024 * 1024
ANY = pl.BlockSpec(memory_space=pl.ANY)

C_GA, C_GB = 0, 2048
C_AX, C_AB, C_AC, C_Q, C_K, C_V, C_OG, C_ALR = 0, 1024, 2048, 3072, 3584, 4096, 5120, 6144
REST = 6272
R_AX, R_AB, R_AC, R_Q, R_K, R_V, R_OG, R_ALR, R_GA, R_GB, R_END = 0, 1024, 2048, 3072, 3584, 4096, 5120, 6144, 6160, 8208, 10256


def _params(sem):
    return pltpu.CompilerParams(dimension_semantics=sem, vmem_limit_bytes=VMEM_LIMIT)


def _pick(n, cands):
    for c in cands:
        if n % c == 0:
            return c
    return n


def _tiles(r, c):
    for tr in (128, 64):
        if r % tr == 0:
            return r // tr, (tr, c), lambda i: (i, 0)
    tc = _pick(c, (256, 128))
    return c // tc, (r, tc), lambda i: (0, i)


def _mm(a, b, mode, out_dtype, name, after=(), add=None, b3=False, out3=False, tm=None, tk=None, n_outer=False):
    bshape = (b.shape[1], N_DEV * b.shape[2]) if b3 else b.shape
    if mode == "nn":
        (m, k), (k2, n) = a.shape, bshape
    elif mode == "nt":
        (m, k), (n, k2) = a.shape, bshape
    else:
        (k, m), (k2, n) = a.shape, bshape
    assert k == k2 and a.dtype == BF16 and b.dtype == BF16, (name, a.shape, b.shape, a.dtype, b.dtype)
    tm = tm if tm and m % tm == 0 else _pick(m, (2048, 1024, 512, 256))
    tn = _pick(n, (1152, 1024, 1408, 896, 512, 256))
    tk = min(tk, k) if tk and k % min(tk, k) == 0 else _pick(k, (2048, 1408, 1152, 1024, 896, 512, 256))
    if out3 or (b3 and mode == "nn"):
        tn = n // N_DEV
    if b3 and mode == "nt":
        tk = k // N_DEV
    nk = k // tk
    dims = {"nn": (((1,), (0,)), ((), ())), "nt": (((1,), (1,)), ((), ())), "tn": (((0,), (0,)), ((), ()))}[mode]
    n_extra = len(after) + (add is not None)

    def body(a_ref, b_ref, *rest):
        o_ref = rest[n_extra]
        prod = lax.dot_general(a_ref[...], b_ref[...], dims, preferred_element_type=F32)
        if nk == 1:
            o_ref[...] = (prod if add is None else prod + rest[0][...]).astype(o_ref.dtype)
            return
        acc_ref = rest[n_extra + 1]
        kk = pl.program_id(2)

        @pl.when(kk == 0)
        def _():
            acc_ref[...] = prod if add is None else prod + rest[0][...]

        @pl.when((kk > 0) & (kk < nk - 1))
        def _():
            acc_ref[...] += prod

        @pl.when(kk == nk - 1)
        def _():
            o_ref[...] = (acc_ref[...] + prod).astype(o_ref.dtype)

    def spec(block, at):
        return pl.BlockSpec(block, (lambda g0, g1, kk: at(g1, g0, kk)) if n_outer else at)

    a_spec = spec((tk, tm), lambda i, j, kk: (kk, i)) if mode == "tn" else spec((tm, tk), lambda i, j, kk: (i, kk))
    if b3:
        b_spec = spec((None, tn, tk), lambda i, j, kk: (kk, j, 0)) if mode == "nt" else spec((None, tk, tn), lambda i, j, kk: (j, kk, 0))
    else:
        b_spec = spec((tn, tk), lambda i, j, kk: (j, kk)) if mode == "nt" else spec((tk, tn), lambda i, j, kk: (kk, j))
    tile = spec((tm, tn), lambda i, j, kk: (i, j))
    out_spec = spec((None, tm, tn), lambda i, j, kk: (j, i, 0)) if out3 else tile
    return pl.pallas_call(
        body, name=name, grid=(n // tn, m // tm, nk) if n_outer else (m // tm, n // tn, nk),
        in_specs=[a_spec, b_spec] + ([tile] if add is not None else []) + [ANY] * len(after), out_specs=out_spec,
        out_shape=jax.ShapeDtypeStruct((N_DEV, m, tn) if out3 else (m, n), out_dtype),
        scratch_shapes=[pltpu.VMEM((tm, tn), F32)] if nk > 1 else [],
        compiler_params=_params(("parallel", "parallel", "arbitrary")),
    )(a, b, *([add] if add is not None else []), *after)


LONG_K = dict(tm=512, tk=8192, n_outer=True)


def _rows(body, t, tr, ins, outs, name, after=()):
    in_specs = []
    for arr, sp in ins:
        if sp[0] == "t":
            in_specs.append(pl.BlockSpec((tr, sp[1]), lambda i, cb=sp[2]: (i, cb)))
        else:
            in_specs.append(pl.BlockSpec(arr.shape, lambda i, nd=arr.ndim: (0,) * nd))
    out_specs, out_shape = [], []
    for shape, dt, kind in outs:
        out_specs.append(pl.BlockSpec((tr, shape[1]), lambda i: (i, 0)) if kind == "t" else pl.BlockSpec(shape, lambda i: (0, 0)))
        out_shape.append(jax.ShapeDtypeStruct(shape, dt))
    return pl.pallas_call(
        body, name=name, grid=(t // tr,), in_specs=in_specs + [ANY] * len(after), out_specs=out_specs, out_shape=out_shape,
        compiler_params=_params(("arbitrary",)),
    )(*[arr for arr, _ in ins], *after)


def _rinv(v):
    return lax.rsqrt(jnp.mean(v * v, axis=-1, keepdims=True) + EPS)


def _sig(v):
    return 1.0 / (1.0 + jnp.exp(-v))


def _acc(ref, val):
    @pl.when(pl.program_id(0) == 0)
    def _():
        ref[...] = jnp.zeros_like(ref)

    ref[...] += jnp.sum(val, axis=0, keepdims=True)


def _rms_fwd(x, g, name):
    t, d = x.shape

    def body(x_ref, g_ref, h_ref):
        xv = x_ref[...]
        h_ref[...] = (xv * _rinv(xv) * g_ref[...]).astype(BF16)

    return _rows(body, t, 256, [(x, ("t", d, 0)), (g, ("b",))], [((t, d), BF16, "t")], name)[0]


def _post_pre(x, m, g_post, g_pre, name, after=()):
    t, d = x.shape

    def body(x_ref, m_ref, gp_ref, gn_ref, *rest):
        xo_ref, h_ref = rest[len(after):]
        mv = m_ref[...]
        xn = x_ref[...] + mv * _rinv(mv) * gp_ref[...]
        xo_ref[...] = xn
        h_ref[...] = (xn * _rinv(xn) * gn_ref[...]).astype(BF16)

    return _rows(body, t, 128, [(x, ("t", d, 0)), (m, ("t", d, 0)), (g_post, ("b",)), (g_pre, ("b",))],
                 [((t, d), F32, "t"), ((t, d), BF16, "t")], name, after)


def _mix_fwd(proj, ya, yb, name):
    t, d = ya.shape

    def body(ga_ref, gb_ref, ya_ref, yb_ref, o_ref):
        o_ref[...] = (_sig(ga_ref[...].astype(F32)) * ya_ref[...].astype(F32)
                      + _sig(gb_ref[...].astype(F32)) * yb_ref[...].astype(F32)).astype(BF16)

    return _rows(body, t, 256, [(proj, ("t", d, C_GA // d)), (proj, ("t", d, C_GB // d)), (ya, ("t", d, 0)), (yb, ("t", d, 0))],
                 [((t, d), BF16, "t")], name)[0]


def _mix_bwd(dmix, proj, ya, yb, name):
    t, d = ya.shape

    def body(dm_ref, ga_ref, gb_ref, ya_ref, yb_ref, dg_ref, dya_ref, dyb_ref):
        dm = dm_ref[...]
        sa, sb = _sig(ga_ref[...].astype(F32)), _sig(gb_ref[...].astype(F32))
        dg_ref[:, :d] = (dm * ya_ref[...].astype(F32) * sa * (1.0 - sa)).astype(BF16)
        dg_ref[:, d:] = (dm * yb_ref[...].astype(F32) * sb * (1.0 - sb)).astype(BF16)
        dya_ref[...] = (dm * sa).astype(BF16)
        dyb_ref[...] = (dm * sb).astype(BF16)

    return _rows(body, t, 128,
                 [(dmix, ("t", d, 0)), (proj, ("t", d, C_GA // d)), (proj, ("t", d, C_GB // d)), (ya, ("t", d, 0)), (yb, ("t", d, 0))],
                 [((t, 2 * d), BF16, "t"), ((t, d), BF16, "t"), ((t, d), BF16, "t")], name)


def _swiglu_call(body, ins, n_out, name):
    t, f = ins[0].shape
    tc = _pick(f, (1408, 512))
    tile = pl.BlockSpec((512, tc), lambda i, j: (i, j))
    return pl.pallas_call(
        body, name=name, grid=(t // 512, f // tc), in_specs=[tile] * len(ins), out_specs=[tile] * n_out,
        out_shape=[jax.ShapeDtypeStruct((t, f), BF16)] * n_out, compiler_params=_params(("parallel", "parallel")),
    )(*ins)


def _swiglu_fwd(fg, fu, name):
    def body(g_ref, u_ref, s_ref):
        gv = g_ref[...].astype(F32)
        s_ref[...] = (gv * _sig(gv) * u_ref[...].astype(F32)).astype(BF16)

    return _swiglu_call(body, [fg, fu], 1, name)[0]


def _swiglu_bwd(ds, fg, fu, name):
    def body(ds_ref, g_ref, u_ref, dg_ref, du_ref):
        dsv, gv, uv = ds_ref[...].astype(F32), g_ref[...].astype(F32), u_ref[...].astype(F32)
        sg = _sig(gv)
        dg_ref[...] = (dsv * uv * sg * (1.0 + gv * (1.0 - sg))).astype(BF16)
        du_ref[...] = (dsv * gv * sg).astype(BF16)

    return _swiglu_call(body, [ds, fg, fu], 2, name)


def _ple_final(x2, pg, pp, tgt, g_post, name):
    t, d = x2.shape

    def body(x_ref, pg_ref, pp_ref, t_ref, g_ref, loss_ref, d3_ref, dpg_ref, dpp_ref, dg_ref):
        sg, ppv, g = _sig(pg_ref[...]), pp_ref[...], g_ref[...]
        e = sg * ppv
        r = _rinv(e)
        eh = e * r
        diff = x_ref[...] + eh * g - t_ref[...]
        loss_ref[...] = 0.5 * jnp.mean(diff * diff, axis=-1, keepdims=True)
        d3 = diff * (1.0 / d)
        d3_ref[...] = d3
        gd = d3 * g
        de = r * (gd - eh * jnp.mean(gd * eh, axis=-1, keepdims=True))
        dpg_ref[...] = (de * ppv * sg * (1.0 - sg)).astype(BF16)
        dpp_ref[...] = (de * sg).astype(BF16)
        _acc(dg_ref, d3 * eh)

    return _rows(body, t, 128, [(x2, ("t", d, 0)), (pg, ("t", d, 0)), (pp, ("t", d, 0)), (tgt, ("t", d, 0)), (g_post, ("b",))],
                 [((t, 1), F32, "t"), ((t, d), F32, "t"), ((t, d), BF16, "t"), ((t, d), BF16, "t"), ((1, d), F32, "a")], name)


def _norm_bwd(dn, dh, x, g_pre, fm, g_post, name):
    t, d = x.shape
    two = fm is not None

    def body(*refs):
        if two:
            dn_ref, dh_ref, x_ref, gp_ref, f_ref, gq_ref, dx_ref, df_ref, dgp_ref, dgq_ref = refs
        else:
            dn_ref, dh_ref, x_ref, gp_ref, dx_ref, dgp_ref = refs
        xv, dhv = x_ref[...], dh_ref[...]
        r = _rinv(xv)
        xh = xv * r
        gd = dhv * gp_ref[...]
        dx = dn_ref[...] + r * (gd - xh * jnp.mean(gd * xh, axis=-1, keepdims=True))
        dx_ref[...] = dx
        _acc(dgp_ref, dhv * xh)
        if two:
            fv = f_ref[...]
            rf = _rinv(fv)
            fh = fv * rf
            gd2 = dx * gq_ref[...]
            df_ref[...] = (rf * (gd2 - fh * jnp.mean(gd2 * fh, axis=-1, keepdims=True))).astype(BF16)
            _acc(dgq_ref, dx * fh)

    ins = [(dn, ("t", d, 0)), (dh, ("t", d, 0)), (x, ("t", d, 0)), (g_pre, ("b",))]
    outs = [((t, d), F32, "t")]
    if two:
        ins += [(fm, ("t", d, 0)), (g_post, ("b",))]
        outs += [((t, d), BF16, "t"), ((1, d), F32, "a"), ((1, d), F32, "a")]
    else:
        outs += [((1, d), F32, "a")]
    return _rows(body, t, 128, ins, outs, name)


CONV_TC = 256


def _shift_down(v, s):
    rows = lax.broadcasted_iota(jnp.int32, v.shape, 0)
    return jnp.where(rows >= s, pltpu.roll(v, s, 0), 0.0)


def _shift_up(v, s):
    n = v.shape[0]
    rows = lax.broadcasted_iota(jnp.int32, v.shape, 0)
    return jnp.where(rows < n - s, pltpu.roll(v, n - s, 0), 0.0)


def _conv_specs(t):
    nb = 1024 // CONV_TC
    seg = lambda c0: pl.BlockSpec((t, CONV_TC), lambda j, cb=c0 // CONV_TC: (0, cb + j))
    own = pl.BlockSpec((t, CONV_TC), lambda j: (0, j))
    wspec = pl.BlockSpec((3, CONV_TC), lambda j: (0, j))
    return nb, seg, own, wspec


def _conv_fwd(proj, conv_w, name, after=()):
    t = proj.shape[0]
    nb, seg, own, wspec = _conv_specs(t)

    def body(ax_ref, ab_ref, ac_ref, w_ref, *rest):
        za_ref = rest[len(after)]
        u = ac_ref[...].astype(F32) * ax_ref[...].astype(F32)
        w = w_ref[...]
        yc = w[0:1] * _shift_down(u, 2) + w[1:2] * _shift_down(u, 1) + w[2:3] * u
        za_ref[...] = (ab_ref[...].astype(F32) * yc).astype(BF16)

    return pl.pallas_call(
        body, name=name, grid=(nb,), in_specs=[seg(C_AX), seg(C_AB), seg(C_AC), wspec] + [ANY] * len(after), out_specs=own,
        out_shape=jax.ShapeDtypeStruct((t, 1024), BF16), compiler_params=_params(("parallel",)),
    )(proj, proj, proj, conv_w, *after)


def _conv_bwd(dza, proj, conv_w, name):
    t = proj.shape[0]
    nb, seg, own, wspec = _conv_specs(t)

    def body(dz_ref, ax_ref, ab_ref, ac_ref, w_ref, dax_ref, dab_ref, dac_ref, dw_ref):
        ax, ab, ac, dz = ax_ref[...].astype(F32), ab_ref[...].astype(F32), ac_ref[...].astype(F32), dz_ref[...].astype(F32)
        w = w_ref[...]
        u = ac * ax
        u1, u2 = _shift_down(u, 1), _shift_down(u, 2)
        yc = w[0:1] * u2 + w[1:2] * u1 + w[2:3] * u
        dab_ref[...] = (dz * yc).astype(BF16)
        dyc = dz * ab
        du = w[2:3] * dyc + w[1:2] * _shift_up(dyc, 1) + w[0:1] * _shift_up(dyc, 2)
        dax_ref[...] = (du * ac).astype(BF16)
        dac_ref[...] = (du * ax).astype(BF16)
        dw_ref[0:1, :] = jnp.sum(dyc * u2, axis=0, keepdims=True)
        dw_ref[1:2, :] = jnp.sum(dyc * u1, axis=0, keepdims=True)
        dw_ref[2:3, :] = jnp.sum(dyc * u, axis=0, keepdims=True)

    act = jax.ShapeDtypeStruct((t, 1024), BF16)
    return pl.pallas_call(
        body, name=name, grid=(nb,), in_specs=[own, seg(C_AX), seg(C_AB), seg(C_AC), wspec], out_specs=[own, own, own, wspec],
        out_shape=[act, act, act, jax.ShapeDtypeStruct((3, 1024), F32)], compiler_params=_params(("parallel",)),
    )(dza, proj, proj, proj, conv_w)


def _dot(a, b, dims, precision=None):
    return lax.dot_general(a, b, (dims, ((), ())), precision=precision, preferred_element_type=F32)


_CONTRACT = {"nn": ((1,), (0,)), "nt": ((1,), (1,)), "tn": ((0,), (0,))}


def _bdot_raw(a, b, mode):
    return _dot(a.astype(BF16), b.astype(BF16), _CONTRACT[mode])


@functools.partial(jax.custom_vjp, nondiff_argnums=(2,))
def _bdot(a, b, mode):
    return _bdot_raw(a, b, mode)


def _bdot_fwd(a, b, mode):
    return _bdot_raw(a, b, mode), (a, b)


def _bdot_bwd(mode, res, ct):
    a, b = res
    if mode == "nn":
        return _bdot_raw(ct, b, "nt"), _bdot_raw(a, ct, "tn")
    if mode == "nt":
        return _bdot_raw(ct, b, "nn"), _bdot_raw(ct, a, "tn")
    return _bdot_raw(b, ct, "nt"), _bdot_raw(a, ct, "nn")


_bdot.defvjp(_bdot_fwd, _bdot_bwd)


@functools.partial(jax.custom_vjp, nondiff_argnums=(2,))
def _sum_dot(ones, x, mode):
    head = x.astype(BF16)
    tail = x - head.astype(F32)
    if mode == "nn":
        return _bdot_raw(ones, head, "nn") + _bdot_raw(ones, tail, "nn")
    return _bdot_raw(head, ones, "tn") + _bdot_raw(tail, ones, "tn")


def _sum_dot_fwd(ones, x, mode):
    return _sum_dot(ones, x, mode), ones


def _sum_dot_bwd(mode, ones, ct):
    return jnp.zeros_like(ones), (_bdot_raw(ones, ct, "tn") if mode == "nn" else _bdot_raw(ones, ct, "nt"))


_sum_dot.defvjp(_sum_dot_fwd, _sum_dot_bwd)


def _gla_chunk(q, k, v, og, alr, s_in, wa, ba, gain):
    c = q.shape[0]
    z =_bdot(alr, wa, "nn") + ba
    la = (jnp.minimum(z, 0.0) - jnp.log(1.0 + jnp.exp(-jnp.abs(z)))) * (1.0 / TAU)
    row = lax.broadcasted_iota(jnp.int32, (c, c), 0)
    col = lax.broadcasted_iota(jnp.int32, (c, c), 1)
    lower = row >= col
    b = _sum_dot(lower.astype(F32), la, "nn")
    trow = lax.broadcasted_iota(jnp.int32, la.shape, 0)
    mid = jnp.sum(jnp.where(trow <= c // 2, la, 0.0), axis=0, keepdims=True)
    blast = jnp.sum(la, axis=0, keepdims=True)
    qs = q * (DK ** -0.5)
    e_up, e_dn = jnp.exp(b - mid), jnp.exp(mid - b)
    a_fwd = _bdot(qs * e_up, k * e_dn, "nt")
    a_rev = _bdot(qs * e_dn, k * e_up, "nt")
    att = jnp.where(lower, a_fwd, a_rev)
    o = _bdot(att, v, "nn") + _bdot(qs * jnp.exp(b), s_in, "nn")
    upd = _bdot(k * jnp.exp(blast - b), v, "tn")
    blast_col = _sum_dot(jnp.ones((c, DV), F32), la, "tn")
    s_out = jnp.exp(blast_col) * s_in + upd
    on = o * _rinv(o) * gain
    return on * og * _sig(og), s_out


def _gla_specs(t, rev):
    n = t // STEP_ROWS
    ch = (lambda i: n - 1 - i) if rev else (lambda i: i)
    col = lambda w, c0: pl.BlockSpec((STEP_ROWS, HEADS * w), lambda i, cb=c0 // (HEADS * w): (ch(i), cb))
    whole = lambda shape: pl.BlockSpec(shape, lambda i, nd=len(shape): (0,) * nd)
    specs = dict(
        q=col(DK, C_Q), k=col(DK, C_K), v=col(DV, C_V), og=col(DV, C_OG),
        alr=pl.BlockSpec((STEP_ROWS, 128), lambda i: (ch(i), C_ALR // 128)),
        wa=whole((128, HEADS * DK)), ba=whole((1, HEADS * DK)), gain=whole((1, DV)),
        state=pl.BlockSpec((STEP_CHUNKS, HEADS, DK, DV), lambda i: (ch(i), 0, 0, 0)),
        odk=pl.BlockSpec((STEP_ROWS, HEADS * DK), lambda i: (ch(i), 0)), odv=pl.BlockSpec((STEP_ROWS, HEADS * DV), lambda i: (ch(i), 0)),
        oalr=pl.BlockSpec((STEP_ROWS, 128), lambda i: (ch(i), 0)), whole=whole,
    )
    return n, specs


def _head_cols(h):
    return slice(h * DK, (h + 1) * DK), slice(h * DV, (h + 1) * DV)


def _gla_fwd(proj, wa, ba, gain, name):
    t = proj.shape[0]
    n, sp = _gla_specs(t, False)

    def body(q_ref, k_ref, v_ref, og_ref, alr_ref, wa_ref, ba_ref, g_ref, zb_ref, st_ref, s_scr):
        @pl.when(pl.program_id(0) == 0)
        def _():
            s_scr[...] = jnp.zeros_like(s_scr)

        state = [s_scr[h] for h in range(HEADS)]
        for c in range(STEP_CHUNKS):
            rows = slice(c * CHUNK, (c + 1) * CHUNK)
            alr = alr_ref[rows, :].astype(F32)
            for h in range(HEADS):
                kc, vc = _head_cols(h)
                st_ref[c, h] = state[h]
                zb, state[h] = _gla_chunk(q_ref[rows, kc].astype(F32), k_ref[rows, kc].astype(F32), v_ref[rows, vc].astype(F32),
                                          og_ref[rows, vc].astype(F32), alr, state[h], wa_ref[:, kc].astype(F32), ba_ref[:, kc], g_ref[...])
                zb_ref[rows, vc] = zb.astype(BF16)
        for h in range(HEADS):
            s_scr[h] = state[h]

    return pl.pallas_call(
        body, name=name, grid=(n,),
        in_specs=[sp["q"], sp["k"], sp["v"], sp["og"], sp["alr"], sp["wa"], sp["ba"], sp["gain"]],
        out_specs=[sp["odv"], sp["state"]],
        out_shape=[jax.ShapeDtypeStruct((t, HEADS * DV), BF16), jax.ShapeDtypeStruct((t // CHUNK, HEADS, DK, DV), F32)],
        scratch_shapes=[pltpu.VMEM((HEADS, DK, DV), F32)],
        compiler_params=_params(("arbitrary",)),
    )(proj, proj, proj, proj, proj, wa, ba, gain)


def _gla_bwd(dzb, proj, states, wa, ba, gain, name):
    t = proj.shape[0]
    n, sp = _gla_specs(t, True)

    def body(dz_ref, q_ref, k_ref, v_ref, og_ref, alr_ref, st_ref, wa_ref, ba_ref, g_ref,
             dq_ref, dk_ref, dv_ref, dog_ref, dalr_ref, dwa_ref, dba_ref, dg_ref, ds_scr):
        @pl.when(pl.program_id(0) == 0)
        def _():
            ds_scr[...] = jnp.zeros_like(ds_scr)
            dwa_ref[...] = jnp.zeros_like(dwa_ref)
            dba_ref[...] = jnp.zeros_like(dba_ref)
            dg_ref[...] = jnp.zeros_like(dg_ref)

        dstate = [ds_scr[h] for h in range(HEADS)]
        dwa_sum, dba_sum, dgain_sum = [None] * HEADS, [None] * HEADS, None
        for c in reversed(range(STEP_CHUNKS)):
            rows = slice(c * CHUNK, (c + 1) * CHUNK)
            alr = alr_ref[rows, :].astype(F32)
            dalr_sum = None
            for h in range(HEADS):
                kc, vc = _head_cols(h)
                args = (q_ref[rows, kc].astype(F32), k_ref[rows, kc].astype(F32), v_ref[rows, vc].astype(F32), og_ref[rows, vc].astype(F32),
                        alr, st_ref[c, h], wa_ref[:, kc].astype(F32), ba_ref[:, kc], g_ref[...])
                _, vjp = jax.vjp(_gla_chunk, *args)
                dq, dk, dv, dog, dalr, dstate[h], dwa, dba, dgain = vjp((dz_ref[rows, vc].astype(F32), dstate[h]))
                dq_ref[rows, kc] = dq.astype(BF16)
                dk_ref[rows, kc] = dk.astype(BF16)
                dv_ref[rows, vc] = dv.astype(BF16)
                dog_ref[rows, vc] = dog.astype(BF16)
                dwa_sum[h] = dwa if dwa_sum[h] is None else dwa_sum[h] + dwa
                dba_sum[h] = dba if dba_sum[h] is None else dba_sum[h] + dba
                dalr_sum = dalr if dalr_sum is None else dalr_sum + dalr
                dgain_sum = dgain if dgain_sum is None else dgain_sum + dgain
            dalr_ref[rows, :] = dalr_sum
        for h in range(HEADS):
            ds_scr[h] = dstate[h]
            dwa_ref[h] += dwa_sum[h]
            dba_ref[h] += dba_sum[h]
        dg_ref[...] += dgain_sum

    whole = sp["whole"]
    return pl.pallas_call(
        body, name=name, grid=(n,),
        in_specs=[sp["odv"], sp["q"], sp["k"], sp["v"], sp["og"], sp["alr"], sp["state"], sp["wa"], sp["ba"], sp["gain"]],
        out_specs=[sp["odk"], sp["odk"], sp["odv"], sp["odv"], sp["oalr"], whole((HEADS, 128, DK)), whole((HEADS, 1, DK)), whole((1, DV))],
        out_shape=[jax.ShapeDtypeStruct((t, HEADS * DK), BF16), jax.ShapeDtypeStruct((t, HEADS * DK), BF16),
                   jax.ShapeDtypeStruct((t, HEADS * DV), BF16), jax.ShapeDtypeStruct((t, HEADS * DV), BF16),
                   jax.ShapeDtypeStruct((t, 128), F32), jax.ShapeDtypeStruct((HEADS, 128, DK), F32),
                   jax.ShapeDtypeStruct((HEADS, 1, DK), F32), jax.ShapeDtypeStruct((1, DV), F32)],
        scratch_shapes=[pltpu.VMEM((HEADS, DK, DV), F32)],
        compiler_params=_params(("arbitrary",)),
    )(dzb, proj, proj, proj, proj, proj, states, wa, ba, gain)


def _local_step(x, p, tgt, gather_start, gather_pass_on, gather_finish, scatter_start, scatter_next, small):
    b_alpha, gain = small["b_alpha_up"], small["gla_head_gain"]
    gather_start(0, ())
    w = dict(gather_finish(0, ()))
    conv_w, w_alpha = w["conv_w"], w["w_alpha_up"]
    wa_p = jnp.zeros((128, HEADS * DK), BF16).at[:GATE_RANK].set(w_alpha.astype(BF16))

    t2 = gather_start(2, (w["in_rest_t"], gather_start(1, ())))
    h1 = _rms_fwd(x, small["g_pre_mix"], "rms_pre_mix")
    proj = _mm(h1, w["in_rest_t"], "nt", BF16, "mm_proj", after=(t2,))
    proj_gates = _mm(h1, w["in_gates_t"], "nt", BF16, "mm_proj_gates", after=(t2,))
    za = _conv_fwd(proj, conv_w, "conv_fwd", after=(gather_pass_on(1, (proj,)),))
    zb, states = _gla_fwd(proj, wa_p, b_alpha, gain, "gla_fwd")
    t3 = gather_start(3, (zb, za))
    w.update(gather_finish(1, (t3,)))
    ya = _mm(za, w["a_out"], "nn", BF16, "mm_ya")
    yb = _mm(zb, w["b_out"], "nn", BF16, "mm_yb")
    mix = _mix_fwd(proj_gates, ya, yb, "mix_fwd")
    m2 = _mm(mix, w["mix"], "nn", F32, "mm_mix")
    t4 = gather_start(4, (m2,))
    x1, h2 = _post_pre(x, m2, small["g_post_mix"], small["g_pre_ffn"], "norm_mix_ffn", after=(gather_pass_on(2, (m2,)),))
    w.update(gather_finish(2, (h2, t4)))
    fu = _mm(h2, w["up_t"], "nt", BF16, "mm_up")
    fg = _mm(h2, w["gate_t"], "nt", BF16, "mm_gate", after=(gather_pass_on(3, (fu,)),))
    s = _swiglu_fwd(fg, fu, "swiglu_fwd")
    w.update(gather_finish(3, (s,)))
    f = _mm(s, w["down"], "nn", F32, "mm_down", after=(gather_pass_on(4, (s,)),), **LONG_K)
    x2, h3 = _post_pre(x1, f, small["g_post_ffn"], small["g_pre_ple"], "norm_ffn_ple")
    w.update(gather_finish(4, (h3,)))
    pg = _mm(h3, w["pg"], "nn", F32, "mm_pg")
    p_bf = p.astype(BF16)
    pp = _mm(p_bf, w["pp"], "nn", F32, "mm_pp", b3=True, tm=2048)
    loss_rows, d3, dpg, dpp, dg_post_ple = _ple_final(x2, pg, pp, tgt, small["g_post_ple"], "ple_final")

    gw = {}
    gw["pp"] = _mm(p_bf, dpp, "tn", BF16, "mm_dw_pp", out3=True)
    gw["pg"] = _mm(h3, dpg, "tn", BF16, "mm_dw_pg")
    dh3 = _mm(dpg, w["pg"], "nt", F32, "mm_dh3", after=(scatter_start(["w_ple_proj", "w_ple_gate"], gw),))
    d2, df, dg_pre_ple, dg_post_ffn = _norm_bwd(d3, dh3, x2, small["g_pre_ple"], f, small["g_post_ffn"], "norm_bwd_ple_ffn")
    gw["down"] = _mm(s, df, "tn", BF16, "mm_dw_down", tm=1408)
    ds = _mm(df, w["down"], "nt", BF16, "mm_ds", after=(scatter_start(["w_ff_down"], gw),))
    dfg, dfu = _swiglu_bwd(ds, fg, fu, "swiglu_bwd")
    gw["gate_t"] = _mm(dfg, h2, "tn", BF16, "mm_dw_gate", tm=1408)
    gw["up_t"] = _mm(dfu, h2, "tn", BF16, "mm_dw_up", after=(gw["gate_t"],), tm=1408)
    dh2 = _mm(dfg, w["gate_t"], "nn", F32, "mm_dh2_gate", after=(scatter_start(["w_ff_gate", "w_ff_up"], gw),), **LONG_K)
    dh2 = _mm(dfu, w["up_t"], "nn", F32, "mm_dh2_up", add=dh2, **LONG_K)
    d1, dm2, dg_pre_ffn, dg_post_mix = _norm_bwd(d2, dh2, x1, small["g_pre_ffn"], m2, small["g_post_mix"], "norm_bwd_ffn_mix")
    dmix = _mm(dm2, w["mix"], "nt", F32, "mm_dmix")
    gw["mix"] = _mm(mix, dm2, "tn", BF16, "mm_dw_mix")
    dgab, dya, dyb = _mix_bwd(dmix, proj_gates, ya, yb, "mix_bwd")
    gw["in_gates_t"] = _mm(dgab, h1, "tn", BF16, "mm_dw_in_gates")
    dza = _mm(dya, w["a_out"], "nt", BF16, "mm_dza", after=(scatter_start(["w_mix_out"], gw), gw["in_gates_t"]))
    gw["a_out"] = _mm(za, dya, "tn", BF16, "mm_dw_a_out")
    gw["b_out"] = _mm(zb, dyb, "tn", BF16, "mm_dw_b_out", after=(gw["a_out"],))
    dzb = _mm(dyb, w["b_out"], "nt", BF16, "mm_dzb", after=(scatter_start(["w_a_out", "w_b_out"], gw),))
    dax, dab, dac, dconv = _conv_bwd(dza, proj, conv_w, "conv_bwd")
    dq, dk, dv, dog, dalr, dwa, dba, dgain = _gla_bwd(dzb, proj, states, wa_p, b_alpha, gain, "gla_bwd")
    drest = jnp.concatenate([dax, dab, dac, dq, dk, dv, dog, dalr.astype(BF16)], axis=1)
    gw["in_rest_t"] = _mm(drest, h1, "tn", BF16, "mm_dw_in_rest", tm=896)
    dh1 = _mm(dgab, w["in_gates_t"], "nn", F32, "mm_dh1_gates", after=(scatter_start(["w_in"], gw),), tm=1024, tk=4096, n_outer=True)
    dh1 = _mm(drest, w["in_rest_t"], "nn", F32, "mm_dh1_rest", add=dh1, after=(scatter_next((dh1,)),), **LONG_K)
    grad_x, dg_pre_mix = _norm_bwd(d1, dh1, x, small["g_pre_mix"], None, None, "norm_bwd_mix")

    gs = dict(
        conv_w=dconv,
        w_alpha_up=jnp.transpose(dwa[:, :GATE_RANK, :], (1, 0, 2)).reshape(GATE_RANK, HEADS * DK),
        b_alpha_up=dba.reshape(1, HEADS * DK), gla_head_gain=dgain,
        g_pre_mix=dg_pre_mix, g_post_mix=dg_post_mix, g_pre_ffn=dg_pre_ffn, g_post_ffn=dg_post_ffn,
        g_pre_ple=dg_pre_ple, g_post_ple=dg_post_ple,
    )
    return loss_rows, grad_x, gs


def _place():
    x, y, c = lax.axis_index("x"), lax.axis_index("y"), lax.axis_index("c")
    return x, y, c, [(1 - x, y), (x, 1 - y), (1 - x, 1 - y)]


def _all_gather(shards, name, cid=None):
    n = len(shards)

    def body(*refs):
        ins, outs = refs[:n], refs[n:2 * n]
        send_sems, recv_sems, local_sems = refs[2 * n:]
        x, y, c, chips = _place()
        me, sibling = (x, y, c), (x, y, 1 - c)

        def slot(px, py, pc):
            return 4 * px + 2 * py + pc

        def copy(a, k, block, to, src=None):
            dst = outs[a].at[slot(*block)]
            return pltpu.make_async_remote_copy(src_ref=dst if src is None else src, dst_ref=dst, send_sem=send_sems.at[a, k],
                                                recv_sem=recv_sems.at[a, k], device_id=to, device_id_type=MESH)

        mine = [pltpu.make_async_copy(ins[a], outs[a].at[slot(*me)], local_sems.at[a]) for a in range(n)]
        for cp in mine:
            cp.start()
        first = []
        for j, chip in enumerate(chips):
            first += [copy(a, 1 + j, me, (*chip, c), src=ins[a]) for a in range(n)]
        first += [copy(a, 0, me, sibling, src=ins[a]) for a in range(n)]
        for cp in first:
            cp.start()
        passed = []
        for j, chip in enumerate(chips):
            for a in range(n):
                copy(a, 1 + j, (*chip, c), me).wait_recv()
                cp = copy(a, 4 + j, (*chip, c), sibling)
                cp.start()
                passed.append(cp)
        for a in range(n):
            copy(a, 0, sibling, me).wait_recv()
        for j, chip in enumerate(chips):
            for a in range(n):
                copy(a, 4 + j, (*chip, 1 - c), me).wait_recv()
        for cp in first + passed:
            cp.wait_send()
        for cp in mine:
            cp.wait()

    if cid is None:
        return pl.pallas_call(
            body, name=name, in_specs=[ANY] * n, out_specs=[ANY] * n,
            out_shape=[jax.ShapeDtypeStruct((N_DEV,) + s.shape, s.dtype) for s in shards],
            scratch_shapes=[pltpu.SemaphoreType.DMA((n, 7)), pltpu.SemaphoreType.DMA((n, 7)), pltpu.SemaphoreType.DMA((n,))],
        )(*shards)

    src = [jax.new_ref(s, memory_space=pltpu.MemorySpace.HBM) for s in shards]
    dst = [jax.empty_ref(jax.ShapeDtypeStruct((N_DEV,) + s.shape, s.dtype), memory_space=pltpu.MemorySpace.HBM) for s in shards]

    @pl.kernel(mesh=plsc.ScalarSubcoreMesh(axis_name="seq", num_cores=1), name=name,
               scratch_types=(pltpu.SemaphoreType.DMA((n, 7)), pltpu.SemaphoreType.DMA((n, 7)), pltpu.SemaphoreType.DMA((n,))),
               compiler_params=pltpu.CompilerParams(collective_id=cid))
    def launch(send_sems, recv_sems, local_sems):
        x, y, c, chips = _place()
        barrier = pltpu.get_barrier_semaphore()
        for peer in [(x, y, 1 - c)] + [(*chip, c) for chip in chips]:
            pl.semaphore_signal(barrier, inc=1, device_id=peer, device_id_type=MESH)
        pl.semaphore_wait(barrier, 4)
        body(*src, *dst, send_sems, recv_sems, local_sems)

    launch()
    return [r[...] for r in dst]


def _reduce_scatter(parts, name, cid):
    n = len(parts)
    src = [jax.new_ref(s, memory_space=pltpu.MemorySpace.HBM) for s in parts]
    dst = [jax.empty_ref(jax.ShapeDtypeStruct(s.shape, s.dtype), memory_space=pltpu.MemorySpace.HBM) for s in parts]

    @pl.kernel(mesh=plsc.ScalarSubcoreMesh(axis_name="seq", num_cores=1), name=name,
               scratch_types=(pltpu.SemaphoreType.DMA((n, N_DEV - 1)), pltpu.SemaphoreType.DMA((n, N_DEV - 1)), pltpu.SemaphoreType.DMA((n,))),
               compiler_params=pltpu.CompilerParams(collective_id=cid))
    def launch(send_sems, recv_sems, local_sems):
        x, y, c, _ = _place()
        me = 4 * x + 2 * y + c
        peers = [(1 - x if k & 4 else x, 1 - y if k & 2 else y, 1 - c if k & 1 else c) for k in range(1, N_DEV)]
        barrier = pltpu.get_barrier_semaphore()
        for peer in peers:
            pl.semaphore_signal(barrier, inc=1, device_id=peer, device_id_type=MESH)
        pl.semaphore_wait(barrier, N_DEV - 1)
        mine = [pltpu.make_async_copy(src[a].at[me], dst[a].at[me], local_sems.at[a]) for a in range(n)]
        for cp in mine:
            cp.start()
        cps = []
        for a in range(n):
            for k, (px, py, pc) in enumerate(peers):
                cps.append(pltpu.make_async_remote_copy(src_ref=src[a].at[4 * px + 2 * py + pc], dst_ref=dst[a].at[me], send_sem=send_sems.at[a, k],
                                                        recv_sem=recv_sems.at[a, k], device_id=(px, py, pc), device_id_type=MESH))
        for cp in cps:
            cp.start()
        for cp in cps:
            cp.wait_recv()
        for cp in cps:
            cp.wait_send()
        for cp in mine:
            cp.wait()

    launch()
    return [r[...] for r in dst]


def _sibling_exchange(parts, name):
    n = len(parts)
    pieces = [_row_pieces(s.shape[1]) for s in parts]

    def body(*refs):
        ins, outs = refs[:n], refs[n:2 * n]
        send_sems, recv_sems = refs[2 * n:]
        x, y, c, _ = _place()

        def copy(a, ch, q, rows):
            return pltpu.make_async_remote_copy(src_ref=ins[a].at[2 * ch + 1 - c, rows], dst_ref=outs[a].at[ch, rows], send_sem=send_sems.at[a, ch, q],
                                                recv_sem=recv_sems.at[a, ch, q], device_id=(x, y, 1 - c), device_id_type=MESH)

        cps = [copy(a, ch, q, rows) for ch in range(4) for a in range(n) for q, rows in enumerate(pieces[a])]
        for cp in cps:
            cp.start()
        for cp in cps:
            cp.wait_recv()
        for cp in cps:
            cp.wait_send()

    return pl.pallas_call(
        body, name=name, in_specs=[ANY] * n, out_specs=[ANY] * n,
        out_shape=[jax.ShapeDtypeStruct((4,) + s.shape[1:], s.dtype) for s in parts],
        scratch_shapes=[pltpu.SemaphoreType.DMA((n, 4, PIECES)), pltpu.SemaphoreType.DMA((n, 4, PIECES))],
    )(*parts)


def _chip_exchange(parts, name):
    n = len(parts)

    def body(*refs):
        ins, outs = refs[:n], refs[n:2 * n]
        send_sems, recv_sems, local_sems = refs[2 * n:]
        x, y, c, chips = _place()
        my_chip = 2 * x + y

        def copy(a, j):
            px, py = chips[j]
            return pltpu.make_async_remote_copy(src_ref=ins[a].at[2 * px + py], dst_ref=outs[a].at[my_chip], send_sem=send_sems.at[a, j],
                                                recv_sem=recv_sems.at[a, j], device_id=(px, py, c), device_id_type=MESH)

        def landing(a, j):
            px, py = chips[j]
            return pltpu.make_async_remote_copy(src_ref=ins[a].at[my_chip], dst_ref=outs[a].at[2 * px + py], send_sem=send_sems.at[a, j],
                                                recv_sem=recv_sems.at[a, j], device_id=(px, py, c), device_id_type=MESH)

        mine = [pltpu.make_async_copy(ins[a].at[my_chip], outs[a].at[my_chip], local_sems.at[a]) for a in range(n)]
        for cp in mine:
            cp.start()
        cps = [copy(a, j) for j in range(3) for a in range(n)]
        for cp in cps:
            cp.start()
        for j in range(3):
            for a in range(n):
                landing(a, j).wait_recv()
        for cp in cps:
            cp.wait_send()
        for cp in mine:
            cp.wait()

    return pl.pallas_call(
        body, name=name, in_specs=[ANY] * n, out_specs=[ANY] * n,
        out_shape=[jax.ShapeDtypeStruct(s.shape, s.dtype) for s in parts],
        scratch_shapes=[pltpu.SemaphoreType.DMA((n, 3)), pltpu.SemaphoreType.DMA((n, 3)), pltpu.SemaphoreType.DMA((n,))],
    )(*parts)


def _pair_add(mine8, got4, name):
    _, r, cols = mine8.shape
    steps, blk, at = _tiles(r, cols)
    core = lax.axis_index("c").astype(jnp.int32).reshape(1)

    def body(c_ref, a_ref, b_ref, o_ref):
        o_ref[...] = (a_ref[...].astype(F32) + b_ref[...].astype(F32)).astype(BF16)

    return pl.pallas_call(
        body, name=name,
        grid_spec=pltpu.PrefetchScalarGridSpec(
            num_scalar_prefetch=1, grid=(4, steps),
            in_specs=[pl.BlockSpec((None,) + blk, lambda ch, i, c_ref: (2 * ch + c_ref[0],) + at(i)),
                      pl.BlockSpec((None,) + blk, lambda ch, i, c_ref: (ch,) + at(i))],
            out_specs=pl.BlockSpec((None,) + blk, lambda ch, i, c_ref: (ch,) + at(i))),
        out_shape=jax.ShapeDtypeStruct((4, r, cols), BF16),
        compiler_params=_params(("parallel", "parallel")),
    )(core, mine8, got4)


HBM = pl.BlockSpec(memory_space=pltpu.HBM)
SEM = pl.BlockSpec(memory_space=pltpu.SEMAPHORE)
EFFECT = pltpu.SideEffectType.DATAFLOW_SIDE_EFFECTING


def _in_hbm(a):
    return pltpu.with_memory_space_constraint(a, pltpu.HBM)


def _remote_copies(plan, srcs, lands, send_sems, recv_sems):
    return [pltpu.make_async_remote_copy(src_ref=s, dst_ref=d, send_sem=send_sems.at[i], recv_sem=recv_sems.at[i], device_id=peer,
                                         device_id_type=MESH) for i, (s, d, peer) in enumerate(plan(srcs, lands))]


def _copies_start(plan, n_copies, srcs, land_shapes, name, after=()):
    ns, nl = len(srcs), len(land_shapes)

    def body(*refs):
        send_sems, recv_sems = refs[ns + nl + len(after):ns + nl + len(after) + 2]
        for cp in _remote_copies(plan, refs[:ns], refs[ns:ns + nl], send_sems, recv_sems):
            cp.start()
        refs[-1][...] = jnp.zeros((8, 128), F32)

    sems = pltpu.SemaphoreType.DMA((n_copies,))
    return pl.pallas_call(
        body, name=name,
        out_shape=(sems, sems, *[pltpu.HBM(s.shape, s.dtype) for s in srcs], *[pltpu.HBM(s.shape, s.dtype) for s in land_shapes],
                   jax.ShapeDtypeStruct((8, 128), F32)),
        in_specs=[HBM] * (ns + nl) + [ANY] * len(after),
        out_specs=(SEM, SEM, *[HBM] * (ns + nl), pl.BlockSpec(memory_space=pltpu.VMEM)),
        input_output_aliases={i: 2 + i for i in range(ns + nl)},
        compiler_params=pltpu.CompilerParams(has_side_effects=EFFECT),
    )(*[_in_hbm(s) for s in srcs], *[_in_hbm(lax.empty(s.shape, s.dtype)) for s in land_shapes], *after)


def _copies_wait(plan, state, ns, name, after=()):
    send_sems, recv_sems, *arrs = state[:-1]
    n = len(arrs)

    def body(*refs):
        cps = _remote_copies(plan, refs[:ns], refs[ns:n], refs[n], refs[n + 1])
        for cp in cps:
            cp.wait_send()
        for cp in cps:
            cp.wait_recv()

    out = pl.pallas_call(
        body, name=name, out_shape=tuple(pltpu.HBM(a.shape, a.dtype) for a in arrs),
        in_specs=[HBM] * n + [SEM, SEM] + [ANY] * len(after), out_specs=tuple([HBM] * n),
        input_output_aliases={i: i for i in range(n)},
        compiler_params=pltpu.CompilerParams(has_side_effects=EFFECT),
    )(*arrs, send_sems, recv_sems, *after)
    return list(out[:ns]), list(out[ns:])


def _copies_relay(plan, state, ns, next_plan, n_next, name, after=()):
    send_sems, recv_sems, *arrs = state[:-1]
    n = len(arrs)

    def body(*refs):
        cps = _remote_copies(plan, refs[:ns], refs[ns:n], refs[n], refs[n + 1])
        for cp in cps:
            cp.wait_send()
        for cp in cps:
            cp.wait_recv()
        outs = refs[n + 2 + len(after):]
        for cp in _remote_copies(next_plan, refs[:ns], refs[ns:n], outs[0], outs[1]):
            cp.start()
        outs[-1][...] = jnp.zeros((8, 128), F32)

    sems = pltpu.SemaphoreType.DMA((n_next,))
    return pl.pallas_call(
        body, name=name,
        out_shape=(sems, sems, *[pltpu.HBM(a.shape, a.dtype) for a in arrs], jax.ShapeDtypeStruct((8, 128), F32)),
        in_specs=[HBM] * n + [SEM, SEM] + [ANY] * len(after),
        out_specs=(SEM, SEM, *[HBM] * n, pl.BlockSpec(memory_space=pltpu.VMEM)),
        input_output_aliases={i: 2 + i for i in range(n)},
        compiler_params=pltpu.CompilerParams(has_side_effects=EFFECT),
    )(*arrs, send_sems, recv_sems, *after)


def _pass_on_plan(srcs, lands):
    x, y, c, chips = _place()
    return [(l.at[4 * px + 2 * py + c], l.at[4 * px + 2 * py + c], (x, y, 1 - c)) for l in lands for px, py in chips]


def _gather_plan(srcs, lands):
    x, y, c, chips = _place()
    peers = [(x, y, 1 - c)] + [(*chip, c) for chip in chips]
    return [(s, l.at[4 * x + 2 * y + c], peer) for s, l in zip(srcs, lands) for peer in peers]


def _gather_plan_near(srcs, lands):
    x, y, c, _ = _place()
    peers = [(x, y, 1 - c), (1 - x, y, c), (x, 1 - y, c)]
    return [(s, l.at[4 * x + 2 * y + c], peer) for s, l in zip(srcs, lands) for peer in peers]


def _scatter_plan(srcs, lands):
    x, y, c, _ = _place()
    peers = [(1 - x if k & 4 else x, 1 - y if k & 2 else y, 1 - c if k & 1 else c) for k in range(1, N_DEV)]
    return [(s.at[4 * px + 2 * py + pc], l.at[4 * x + 2 * y + c], (px, py, pc)) for s, l in zip(srcs, lands) for px, py, pc in peers]


def _everyone_plan(srcs, lands):
    x, y, c, _ = _place()
    peers = [(1 - x if k & 4 else x, 1 - y if k & 2 else y, 1 - c if k & 1 else c) for k in range(1, N_DEV)]
    return [(s, l.at[4 * x + 2 * y + c], peer) for s, l in zip(srcs, lands) for peer in peers]


def _sum_parts(got, own, me, name):
    def body(me_ref, got_ref, own_ref, o_ref):
        acc = jnp.where(me_ref[0] == 0, own_ref[...], got_ref[0])
        for d in range(1, N_DEV):
            acc = acc + jnp.where(me_ref[0] == d, own_ref[...], got_ref[d])
        o_ref[...] = acc

    return pl.pallas_call(
        body, name=name,
        grid_spec=pltpu.PrefetchScalarGridSpec(
            num_scalar_prefetch=1, grid=(1,),
            in_specs=[pl.BlockSpec(got.shape, lambda i, me_ref: (0, 0, 0)), pl.BlockSpec(own.shape, lambda i, me_ref: (0, 0))],
            out_specs=pl.BlockSpec(own.shape, lambda i, me_ref: (0, 0))),
        out_shape=jax.ShapeDtypeStruct(own.shape, F32),
    )(me.astype(jnp.int32).reshape(1), got, own)


def _sibling_plan(srcs, lands):
    x, y, c, _ = _place()
    return [(s.at[2 * ch + 1 - c], l.at[ch], (x, y, 1 - c)) for s, l in zip(srcs, lands) for ch in range(4)]


def _chip_plan(srcs, lands):
    x, y, c, chips = _place()
    return [(s.at[2 * px + py], l.at[2 * x + y], (px, py, c)) for s, l in zip(srcs, lands) for px, py in chips]


PIECES = 8


def _row_pieces(rows):
    for k in (PIECES, 4, 2):
        if rows % (16 * k) == 0:
            return [pl.ds(q * (rows // k), rows // k) for q in range(k)]
    return [pl.ds(0, rows)]


def _put_own(shard, zone, me, name):
    r, c = shard.shape
    tr = r if r <= 256 else _pick(r, (256, 64))

    def body(me_ref, s_ref, z_ref, o_ref):
        o_ref[...] = s_ref[...]

    return pl.pallas_call(
        body, name=name,
        grid_spec=pltpu.PrefetchScalarGridSpec(
            num_scalar_prefetch=1, grid=(r // tr,),
            in_specs=[pl.BlockSpec((tr, c), lambda i, me_ref: (i, 0)), ANY],
            out_specs=pl.BlockSpec((None, tr, c), lambda i, me_ref: (me_ref[0], i, 0))),
        out_shape=jax.ShapeDtypeStruct(zone.shape, zone.dtype), input_output_aliases={2: 0},
        compiler_params=_params(("arbitrary",)),
    )(me.astype(jnp.int32).reshape(1), shard, zone)


def _gather_finish(lands, name):
    n = len(lands)

    def body(*refs):
        zones, outs = refs[:n], refs[n:2 * n]
        send_sems, recv_sems = refs[2 * n:]
        x, y, c, chips = _place()
        cps = [pltpu.make_async_remote_copy(
            src_ref=zones[a].at[4 * px + 2 * py + c], dst_ref=outs[a].at[4 * px + 2 * py + c], send_sem=send_sems.at[a, j],
            recv_sem=recv_sems.at[a, j], device_id=(x, y, 1 - c), device_id_type=MESH) for j, (px, py) in enumerate(chips) for a in range(n)]
        for cp in cps:
            cp.start()
        for cp in cps:
            cp.wait_recv()
        for cp in cps:
            cp.wait_send()

    return pl.pallas_call(
        body, name=name, in_specs=[ANY] * n, out_specs=[ANY] * n,
        out_shape=[jax.ShapeDtypeStruct(l.shape, l.dtype) for l in lands],
        input_output_aliases={a: a for a in range(n)},
        scratch_shapes=[pltpu.SemaphoreType.DMA((n, 3)), pltpu.SemaphoreType.DMA((n, 3))],
    )(*lands)


def _gather_relay(lands, name):
    n = len(lands)

    def body(*refs):
        zones, outs = refs[:n], refs[n:2 * n]
        send_sems, recv_sems = refs[2 * n:]
        x, y, c, _ = _place()
        south = c == 0
        near_x, near_y, across = 4 * (1 - x) + 2 * y + c, 4 * x + 2 * (1 - y) + c, 4 * (1 - x) + 2 * (1 - y) + c
        passed = jnp.where(south, near_y, near_x)
        onward = (jnp.where(south, 1 - x, x), jnp.where(south, y, 1 - y), c)

        def copy(a, k, slot, to):
            return pltpu.make_async_remote_copy(src_ref=zones[a].at[slot], dst_ref=outs[a].at[slot], send_sem=send_sems.at[a, k],
                                                recv_sem=recv_sems.at[a, k], device_id=to, device_id_type=MESH)

        first = [copy(a, 0, passed, onward) for a in range(n)]
        first += [copy(a, 1 + j, slot, (x, y, 1 - c)) for j, slot in enumerate((near_x, near_y)) for a in range(n)]
        for cp in first:
            cp.start()
        last = []
        for a in range(n):
            copy(a, 0, across, onward).wait_recv()
            last.append(copy(a, 3, across, (x, y, 1 - c)))
            last[-1].start()
        for cp in first[n:] + last:
            cp.wait_recv()
        for cp in first + last:
            cp.wait_send()

    return pl.pallas_call(
        body, name=name, in_specs=[ANY] * n, out_specs=[ANY] * n,
        out_shape=[jax.ShapeDtypeStruct(l.shape, l.dtype) for l in lands],
        input_output_aliases={a: a for a in range(n)},
        scratch_shapes=[pltpu.SemaphoreType.DMA((n, 4)), pltpu.SemaphoreType.DMA((n, 4))],
    )(*lands)


def _sum_everywhere(v, name):
    rows = v.shape[0]

    def body(v_ref, o_ref, buf, send_sems, recv_sems):
        x, y, c, _ = _place()
        me = 4 * x + 2 * y + c
        buf[me] = v_ref[...]
        cps = []
        for k in range(1, N_DEV):
            fx, fy, fc = (k >> 2) & 1, (k >> 1) & 1, k & 1
            to = (1 - x if fx else x, 1 - y if fy else y, 1 - c if fc else c)
            cps.append(pltpu.make_async_remote_copy(src_ref=buf.at[me], dst_ref=buf.at[me], send_sem=send_sems.at[k - 1],
                                                    recv_sem=recv_sems.at[k - 1], device_id=to, device_id_type=MESH))
        for cp in cps:
            cp.start()
        for cp in cps:
            cp.wait_recv()
        for cp in cps:
            cp.wait_send()
        acc = buf[0]
        for d in range(1, N_DEV):
            acc = acc + buf[d]
        o_ref[...] = acc

    vm = pl.BlockSpec(memory_space=pltpu.VMEM)
    return pl.pallas_call(
        body, name=name, in_specs=[vm], out_specs=vm, out_shape=jax.ShapeDtypeStruct(v.shape, F32),
        scratch_shapes=[pltpu.VMEM((N_DEV, rows, 128), F32), pltpu.SemaphoreType.DMA((N_DEV - 1,)), pltpu.SemaphoreType.DMA((N_DEV - 1,))],
    )(v)


def _adamw_parts(w, got, mine, me, m, v, name, after=()):
    r, c = w.shape
    n_parts = got.shape[0]
    steps, blk, at = _tiles(r, c)

    def body(me_ref, w_ref, got_ref, own_ref, m_ref, v_ref, *rest):
        go_ref, d_ref, mo_ref, vo_ref = rest[len(after):]
        own = own_ref[...].astype(F32)
        gv = jnp.where(me_ref[0] == 0, own, got_ref[0].astype(F32))
        for d in range(1, n_parts):
            gv = gv + jnp.where(me_ref[0] == d, own, got_ref[d].astype(F32))
        _adamw_math(gv, w_ref, m_ref, v_ref, go_ref, d_ref, mo_ref, vo_ref)

    tile = pl.BlockSpec(blk, lambda i, me_ref: at(i))
    out = jax.ShapeDtypeStruct((r, c), F32)
    return pl.pallas_call(
        body, name=name,
        grid_spec=pltpu.PrefetchScalarGridSpec(
            num_scalar_prefetch=1, grid=(steps,),
            in_specs=[tile, pl.BlockSpec((n_parts,) + blk, lambda i, me_ref: (0,) + at(i)),
                      pl.BlockSpec((None,) + blk, lambda i, me_ref: (me_ref[0],) + at(i)), tile, tile] + [ANY] * len(after),
            out_specs=[tile] * 4),
        out_shape=[out] * 4, compiler_params=_params(("parallel",)),
    )(me.astype(jnp.int32).reshape(1), w, got, mine, m, v, *after)


def _adamw_math(gv, w_ref, m_ref, v_ref, go_ref, d_ref, mo_ref, vo_ref):
    mn = B1 * m_ref[...] + (1.0 - B1) * gv
    vn = B2 * v_ref[...] + (1.0 - B2) * (gv * gv)
    m_hat = mn / (1.0 - B1 ** STEP)
    v_hat = vn / (1.0 - B2 ** STEP)
    go_ref[...] = gv
    d_ref[...] = -LR * (m_hat / (jnp.sqrt(v_hat) + ADAM_EPS) + WD * w_ref[...])
    mo_ref[...] = mn
    vo_ref[...] = vn


def _adamw(w, g, m, v, name):
    r, c = w.shape
    parts = g.ndim == 3
    tr = r if r <= 128 else _pick(r, (128, 64))

    def body(w_ref, g_ref, m_ref, v_ref, go_ref, d_ref, mo_ref, vo_ref):
        if parts:
            gv = g_ref[0].astype(F32)
            for d in range(1, g.shape[0]):
                gv = gv + g_ref[d].astype(F32)
        else:
            gv = g_ref[...]
        mn = B1 * m_ref[...] + (1.0 - B1) * gv
        vn = B2 * v_ref[...] + (1.0 - B2) * (gv * gv)
        m_hat = mn / (1.0 - B1 ** STEP)
        v_hat = vn / (1.0 - B2 ** STEP)
        go_ref[...] = gv
        d_ref[...] = -LR * (m_hat / (jnp.sqrt(v_hat) + ADAM_EPS) + WD * w_ref[...])
        mo_ref[...] = mn
        vo_ref[...] = vn

    tile = pl.BlockSpec((tr, c), lambda i: (i, 0))
    g_spec = pl.BlockSpec((g.shape[0], tr, c), lambda i: (0, i, 0)) if parts else tile
    out = jax.ShapeDtypeStruct((r, c), F32)
    return pl.pallas_call(
        body, name=name, grid=(r // tr,), in_specs=[tile, g_spec, tile, tile], out_specs=[tile] * 4, out_shape=[out] * 4,
        compiler_params=_params(("parallel",)),
    )(w, g, m, v)


BIG = ["w_in", "w_a_out", "w_b_out", "w_mix_out", "w_ff_gate", "w_ff_up", "w_ff_down", "w_ple_gate", "w_ple_proj"]
TRANSPOSED = ["w_in", "w_ff_gate", "w_ff_up"]
GRAD_OF = dict(w_ple_proj="pp", w_ple_gate="pg", w_ff_down="down", w_ff_gate="gate_t", w_ff_up="up_t", w_mix_out="mix", w_a_out="a_out",
               w_b_out="b_out")
SMALL = ["conv_w", "w_alpha_up", "b_alpha_up", "gla_head_gain", "g_pre_mix", "g_post_mix", "g_pre_ffn", "g_post_ffn", "g_pre_ple", "g_post_ple"]
WEIGHTS = ["w_in", "conv_w", "w_a_out", "w_alpha_up", "b_alpha_up", "gla_head_gain", "w_b_out", "w_mix_out", "g_pre_mix", "g_post_mix",
           "g_pre_ffn", "g_post_ffn", "w_ff_gate", "w_ff_up", "w_ff_down", "g_pre_ple", "g_post_ple", "w_ple_gate", "w_ple_proj"]


def _in_t_from_blocks(z):
    w = z.reshape(-1, z.shape[-1])
    return w[R_GA:R_END], jnp.concatenate([w[:R_GA], jnp.zeros((REST - R_GA, w.shape[1]), w.dtype)], axis=0)


def _blocks_from_in_t(g_gates, g_rest):
    per = R_END // N_DEV

    def rows(lo, hi):
        out = []
        for n0, n1, g in ((0, R_GA, g_rest), (R_GA, R_END, g_gates)):
            a, e = max(lo, n0), min(hi, n1)
            if a < e:
                out.append(g[a - n0:e - n0])
        return out

    return jnp.stack([jnp.concatenate(rows(b * per, (b + 1) * per), axis=0) for b in range(N_DEV)])


def _cols_to_full(g8):
    n, r, c = g8.shape
    return jnp.transpose(g8, (1, 0, 2)).reshape(r, n * c)


def _full_to_cols(a):
    r, c = a.shape
    return jnp.transpose(a.reshape(r, N_DEV, c // N_DEV), (1, 0, 2))


def _pack(arrs, rows):
    flat = jnp.concatenate([a.reshape(-1) for a in arrs])
    return jnp.pad(flat, (0, rows * 128 - flat.shape[0])).reshape(rows, 128)


def _unpack(packed, shapes):
    flat, out, o = packed.reshape(-1), [], 0
    for s in shapes:
        size = 1
        for d in s:
            size *= d
        out.append(flat[o:o + size].reshape(s))
        o += size
    return out


def kernel(x, p, w_in, conv_w, w_a_out, w_alpha_up, b_alpha_up, gla_head_gain, w_b_out, w_mix_out, g_pre_mix, g_post_mix, g_pre_ffn, g_post_ffn, w_ff_gate, w_ff_up, w_ff_down, g_pre_ple, g_post_ple, w_ple_gate, w_ple_proj, loss_target, m_w_in, m_conv_w, m_w_a_out, m_w_alpha_up, m_b_alpha_up, m_gla_head_gain, m_w_b_out, m_w_mix_out, m_g_pre_mix, m_g_post_mix, m_g_pre_ffn, m_g_post_ffn, m_w_ff_gate, m_w_ff_up, m_w_ff_down, m_g_pre_ple, m_g_post_ple, m_w_ple_gate, m_w_ple_proj, v_w_in, v_conv_w, v_w_a_out, v_w_alpha_up, v_b_alpha_up, v_gla_head_gain, v_w_b_out, v_w_mix_out, v_g_pre_mix, v_g_post_mix, v_g_pre_ffn, v_g_post_ffn, v_w_ff_gate, v_w_ff_up, v_w_ff_down, v_g_pre_ple, v_g_post_ple, v_w_ple_gate, v_w_ple_proj):
    args = dict(locals())
    shard = lambda n, a: jnp.transpose(a[0]) if n in TRANSPOSED else a[0]
    wts = {n: shard(n, args[n]) for n in WEIGHTS}
    mom = {n: shard(n, args["m_" + n]) for n in WEIGHTS}
    var = {n: shard(n, args["v_" + n]) for n in WEIGHTS}
    me =4 * lax.axis_index("x") + 2 * lax.axis_index("y") + lax.axis_index("c")

    groups = [["w_in", "conv_w", "w_alpha_up"], ["w_a_out", "w_b_out", "w_mix_out"], ["w_ff_gate", "w_ff_up"], ["w_ff_down"],
              ["w_ple_gate", "w_ple_proj"]]
    grad_groups = []
    rows_full = lambda g: g.reshape(-1, g.shape[-1])
    gathers, scatters = {}, {}

    def gather_start(gi, after):
        if gi not in gathers:
            shards = [wts[n].astype(BF16) if n in BIG else wts[n] for n in groups[gi]]
            zones = [jax.ShapeDtypeStruct((N_DEV,) + s.shape, s.dtype) for s in shards]
            plan, peers = (_gather_plan_near, 3) if gi == 0 else (_gather_plan, 4)
            gathers[gi] = (shards, _copies_start(plan, peers * len(shards), shards, zones, "gather_start_%d" % gi, after))
        return gathers[gi][1][-1]

    def gather_pass_on(gi, after):
        shards, state = gathers[gi]
        gathers[gi] = (shards, _copies_relay(_gather_plan, state, len(shards), _pass_on_plan, 3 * len(shards), "gather_pass_on_%d" % gi, after))
        return gathers[gi][1][-1]

    def gather_finish(gi, after):
        shards, state = gathers[gi]
        shards, zones = _copies_wait(_gather_plan_near if gi == 0 else _pass_on_plan, state, len(shards), "gather_wait_%d" % gi, after)
        if gi == 0:
            zones = _gather_relay(zones, "gather_relay_%d" % gi)
            gather_start(1, (zones[0],))
        g8 = {n: _put_own(s, z, me, "gather_own_" + n) for n, s, z in zip(groups[gi], shards, zones)}
        if gi == 0:
            in_gates_t, in_rest_t = _in_t_from_blocks(g8["w_in"])
            return dict(in_gates_t=in_gates_t, in_rest_t=in_rest_t,
                        conv_w=_cols_to_full(g8["conv_w"]), w_alpha_up=_cols_to_full(g8["w_alpha_up"]))
        if gi == 1:
            return dict(a_out=_cols_to_full(g8["w_a_out"]), b_out=_cols_to_full(g8["w_b_out"]), mix=rows_full(g8["w_mix_out"]))
        if gi == 2:
            return dict(gate_t=rows_full(g8["w_ff_gate"]), up_t=rows_full(g8["w_ff_up"]))
        if gi == 3:
            return dict(down=rows_full(g8["w_ff_down"]))
        return dict(pg=rows_full(g8["w_ple_gate"]), pp=g8["w_ple_proj"])

    def scatter_start(names, gw):
        gi = len(grad_groups)
        grad_groups.append(names)
        full = {n: _blocks_from_in_t(gw["in_gates_t"], gw["in_rest_t"]) if n == "w_in" else gw[GRAD_OF[n]] for n in names}
        for n in names:
            if n in ("w_a_out", "w_b_out"):
                full[n] = _full_to_cols(full[n])
        parts = [full[n] if full[n].ndim == 3 else full[n].reshape(N_DEV, -1, full[n].shape[-1]) for n in names]
        if names == ["w_in"]:
            quarter = [jax.ShapeDtypeStruct((4,) + a.shape[1:], a.dtype) for a in parts]
            scatters[gi] = _copies_start(_sibling_plan, 4 * len(parts), parts, quarter, "scatter_sibling_start_%d" % gi)
        else:
            scatters[gi] = _copies_start(_scatter_plan, (N_DEV - 1) * len(parts), parts, parts, "scatter_start_%d" % gi)
        return scatters[gi][-1]

    def scatter_next(after):
        gi, names = len(grad_groups) - 1, grad_groups[-1]
        parts, from_sibling = _copies_wait(_sibling_plan, scatters[gi], len(names), "scatter_sibling_wait_%d" % gi, after)
        parts = [_pair_add(a, b, "scatter_add_%d_%s" % (gi, n)) for n, a, b in zip(names, parts, from_sibling)]
        scatters[gi] = _copies_start(_chip_plan, 3 * len(parts), parts, parts, "scatter_start_%d" % gi)
        return scatters[gi][-1]

    small = {n: wts[n].reshape(1, -1) for n in SMALL[2:]}

    loss_rows, grad_x, gs = _local_step(x[0], p[0, 0], loss_target[0], gather_start, gather_pass_on, gather_finish, scatter_start,
                                        scatter_next, small)
    gs["loss"] = jnp.sum(loss_rows).reshape(1, 1)

    small_shapes = [gs[n].shape for n in SMALL]
    gs_packed = _pack([gs[n] for n in SMALL + ["loss"]], 192)
    small_state = _copies_start(_everyone_plan, N_DEV - 1, [gs_packed], [jax.ShapeDtypeStruct((N_DEV,) + gs_packed.shape, F32)],
                                "small_start", (grad_x,))

    res, done = {}, (small_state[-1],)
    for gi, names in enumerate(grad_groups):
        plan, slot = (_chip_plan, me // 2) if names == ["w_in"] else (_scatter_plan, me)
        mine, got = _copies_wait(plan, scatters[gi], len(names), "scatter_wait_%d" % gi, done)
        for n, g, own in zip(names, got, mine):
            res[n] = _adamw_parts(wts[n], g, own, slot, mom[n], var[n], "adamw_" + n)
        done = tuple(res[n][1] for n in names)

    (gs_own,), (gs_got,) = _copies_wait(_everyone_plan, small_state, 1, "small_wait", done)
    gsum = dict(zip(SMALL + ["loss"], _unpack(_sum_parts(gs_got, gs_own, me, "small_sum"), small_shapes + [(1, 1)])))
    loss = gsum["loss"].reshape(())
    gsum["conv_w"] = lax.dynamic_index_in_dim(gsum["conv_w"].reshape(3, N_DEV, -1), me, axis=1, keepdims=False)
    gsum["w_alpha_up"] = lax.dynamic_index_in_dim(gsum["w_alpha_up"].reshape(GATE_RANK, N_DEV, -1), me, axis=1, keepdims=False)

    shard_shapes = [wts[n].shape for n in SMALL]
    packed = [_pack([d[n] for n in SMALL], 120) for d in (wts, gsum, mom, var)]
    outs = [_unpack(o, shard_shapes) for o in _adamw(*packed, "adamw_small")]
    for i, n in enumerate(SMALL):
        res[n] = [o[i] for o in outs]

    back = lambda n, a: (jnp.transpose(a) if n in TRANSPOSED else a)[None]
    return (loss, grad_x[None], *[back(n, res[n][i]) for i in range(4) for n in WEIGHTS])
```

```python
import functools

import jax
import jax.numpy as jnp
from jax import lax
from jax.experimental import pallas as pl
from jax.experimental.pallas import tpu as pltpu

F32, BF16 = jnp.float32, jnp.bfloat16
EPS = 1e-6
CHUNK = 64
STEP_CHUNKS = 2
STEP_ROWS = STEP_CHUNKS * CHUNK
HEADS, DK, DV = 4, 128, 256
GATE_RANK = 16
TAU = 16.0
LR, B1, B2, ADAM_EPS, WD, STEP = 0.001, 0.9, 0.999, 1e-08, 0.01, 10
N_DEV = 8
MESH = pl.DeviceIdType.MESH
VMEM_LIMIT = 56 * 1024 * 1024
ANY = pl.BlockSpec(memory_space=pl.ANY)

C_GA, C_GB = 0, 2048
C_AX, C_AB, C_AC, C_Q, C_K, C_V, C_OG, C_ALR = 0, 1024, 2048, 3072, 3584, 4096, 5120, 6144
REST = 6272
R_GA, R_END = 6160, 10256


def _params(sem):
    return pltpu.CompilerParams(dimension_semantics=sem, vmem_limit_bytes=VMEM_LIMIT)


def _pick(n, cands):
    for c in cands:
        if n % c == 0:
            return c
    return n


def _tiles(r, c):
    for tr in (128, 64):
        if r % tr == 0:
            return r // tr, (tr, c), lambda i: (i, 0)
    tc = _pick(c, (256, 128))
    return c // tc, (r, tc), lambda i: (0, i)


def _mm(a, b, mode, out_dtype, name, after=(), add=None, b3=False, out3=False, tm=None, tk=None, n_outer=False):
    bshape = (b.shape[1], N_DEV * b.shape[2]) if b3 else b.shape
    if mode == "nn":
        (m, k), (k2, n) = a.shape, bshape
    elif mode == "nt":
        (m, k), (n, k2) = a.shape, bshape
    else:
        (k, m), (k2, n) = a.shape, bshape
    assert k == k2 and a.dtype == BF16 and b.dtype == BF16, (name, a.shape, b.shape, a.dtype, b.dtype)
    tm = tm if tm and m % tm == 0 else _pick(m, (2048, 1024, 512, 256))
    tn = _pick(n, (1152, 1024, 1408, 896, 512, 256))
    tk = min(tk, k) if tk and k % min(tk, k) == 0 else _pick(k, (2048, 1408, 1152, 1024, 896, 512, 256))
    if out3 or (b3 and mode == "nn"):
        tn = n // N_DEV
    if b3 and mode == "nt":
        tk = k // N_DEV
    nk = k // tk
    dims = {"nn": (((1,), (0,)), ((), ())), "nt": (((1,), (1,)), ((), ())), "tn": (((0,), (0,)), ((), ()))}[mode]
    n_extra = len(after) + (add is not None)

    def body(a_ref, b_ref, *rest):
        o_ref = rest[n_extra]
        prod = lax.dot_general(a_ref[...], b_ref[...], dims, preferred_element_type=F32)
        if nk == 1:
            o_ref[...] = (prod if add is None else prod + rest[0][...]).astype(o_ref.dtype)
            return
        acc_ref = rest[n_extra + 1]
        kk = pl.program_id(2)

        @pl.when(kk == 0)
        def _():
            acc_ref[...] = prod if add is None else prod + rest[0][...]

        @pl.when((kk > 0) & (kk < nk - 1))
        def _():
            acc_ref[...] += prod

        @pl.when(kk == nk - 1)
        def _():
            o_ref[...] = (acc_ref[...] + prod).astype(o_ref.dtype)

    def spec(block, at):
        return pl.BlockSpec(block, (lambda g0, g1, kk: at(g1, g0, kk)) if n_outer else at)

    a_spec = spec((tk, tm), lambda i, j, kk: (kk, i)) if mode == "tn" else spec((tm, tk), lambda i, j, kk: (i, kk))
    if b3:
        b_spec = spec((None, tn, tk), lambda i, j, kk: (kk, j, 0)) if mode == "nt" else spec((None, tk, tn), lambda i, j, kk: (j, kk, 0))
    else:
        b_spec = spec((tn, tk), lambda i, j, kk: (j, kk)) if mode == "nt" else spec((tk, tn), lambda i, j, kk: (kk, j))
    tile = spec((tm, tn), lambda i, j, kk: (i, j))
    out_spec = spec((None, tm, tn), lambda i, j, kk: (j, i, 0)) if out3 else tile
    return pl.pallas_call(
        body, name=name, grid=(n // tn, m // tm, nk) if n_outer else (m // tm, n // tn, nk),
        in_specs=[a_spec, b_spec] + ([tile] if add is not None else []) + [ANY] * len(after), out_specs=out_spec,
        out_shape=jax.ShapeDtypeStruct((N_DEV, m, tn) if out3 else (m, n), out_dtype),
        scratch_shapes=[pltpu.VMEM((tm, tn), F32)] if nk > 1 else [],
        compiler_params=_params(("parallel", "parallel", "arbitrary")),
    )(a, b, *([add] if add is not None else []), *after)


LONG_K = dict(tm=512, tk=8192, n_outer=True)


def _rows(body, t, tr, ins, outs, name, after=()):
    in_specs = []
    for arr, sp in ins:
        if sp[0] == "t":
            in_specs.append(pl.BlockSpec((tr, sp[1]), lambda i, cb=sp[2]: (i, cb)))
        else:
            in_specs.append(pl.BlockSpec(arr.shape, lambda i, nd=arr.ndim: (0,) * nd))
    out_specs, out_shape = [], []
    for shape, dt, kind in outs:
        out_specs.append(pl.BlockSpec((tr, shape[1]), lambda i: (i, 0)) if kind == "t" else pl.BlockSpec(shape, lambda i: (0, 0)))
        out_shape.append(jax.ShapeDtypeStruct(shape, dt))
    return pl.pallas_call(
        body, name=name, grid=(t // tr,), in_specs=in_specs + [ANY] * len(after), out_specs=out_specs, out_shape=out_shape,
        compiler_params=_params(("arbitrary",)),
    )(*[arr for arr, _ in ins], *after)


def _rinv(v):
    return lax.rsqrt(jnp.mean(v * v, axis=-1, keepdims=True) + EPS)


def _sig(v):
    return 1.0 / (1.0 + jnp.exp(-v))


def _acc(ref, val):
    @pl.when(pl.program_id(0) == 0)
    def _():
        ref[...] = jnp.zeros_like(ref)

    ref[...] += jnp.sum(val, axis=0, keepdims=True)


def _rms_fwd(x, g, name):
    t, d = x.shape

    def body(x_ref, g_ref, h_ref):
        xv = x_ref[...]
        h_ref[...] = (xv * _rinv(xv) * g_ref[...]).astype(BF16)

    return _rows(body, t, 256, [(x, ("t", d, 0)), (g, ("b",))], [((t, d), BF16, "t")], name)[0]


def _post_pre(x, m, g_post, g_pre, name, after=()):
    t, d = x.shape

    def body(x_ref, m_ref, gp_ref, gn_ref, *rest):
        xo_ref, h_ref = rest[len(after):]
        mv = m_ref[...]
        xn = x_ref[...] + mv * _rinv(mv) * gp_ref[...]
        xo_ref[...] = xn
        h_ref[...] = (xn * _rinv(xn) * gn_ref[...]).astype(BF16)

    return _rows(body, t, 128, [(x, ("t", d, 0)), (m, ("t", d, 0)), (g_post, ("b",)), (g_pre, ("b",))],
                 [((t, d), F32, "t"), ((t, d), BF16, "t")], name, after)


def _mix_fwd(proj, ya, yb, name):
    t, d = ya.shape

    def body(ga_ref, gb_ref, ya_ref, yb_ref, o_ref):
        o_ref[...] = (_sig(ga_ref[...].astype(F32)) * ya_ref[...].astype(F32)
                      + _sig(gb_ref[...].astype(F32)) * yb_ref[...].astype(F32)).astype(BF16)

    return _rows(body, t, 256, [(proj, ("t", d, C_GA // d)), (proj, ("t", d, C_GB // d)), (ya, ("t", d, 0)), (yb, ("t", d, 0))],
                 [((t, d), BF16, "t")], name)[0]


def _mix_bwd(dmix, proj, ya, yb, name):
    t, d = ya.shape

    def body(dm_ref, ga_ref, gb_ref, ya_ref, yb_ref, dg_ref, dya_ref, dyb_ref):
        dm = dm_ref[...]
        sa, sb = _sig(ga_ref[...].astype(F32)), _sig(gb_ref[...].astype(F32))
        dg_ref[:, :d] = (dm * ya_ref[...].astype(F32) * sa * (1.0 - sa)).astype(BF16)
        dg_ref[:, d:] = (dm * yb_ref[...].astype(F32) * sb * (1.0 - sb)).astype(BF16)
        dya_ref[...] = (dm * sa).astype(BF16)
        dyb_ref[...] = (dm * sb).astype(BF16)

    return _rows(body, t, 128,
                 [(dmix, ("t", d, 0)), (proj, ("t", d, C_GA // d)), (proj, ("t", d, C_GB // d)), (ya, ("t", d, 0)), (yb, ("t", d, 0))],
                 [((t, 2 * d), BF16, "t"), ((t, d), BF16, "t"), ((t, d), BF16, "t")], name)


def _swiglu_call(body, ins, n_out, name):
    t, f = ins[0].shape
    tc = _pick(f, (1408, 512))
    tile = pl.BlockSpec((512, tc), lambda i, j: (i, j))
    return pl.pallas_call(
        body, name=name, grid=(t // 512, f // tc), in_specs=[tile] * len(ins), out_specs=[tile] * n_out,
        out_shape=[jax.ShapeDtypeStruct((t, f), BF16)] * n_out, compiler_params=_params(("parallel", "parallel")),
    )(*ins)


def _swiglu_fwd(fg, fu, name):
    def body(g_ref, u_ref, s_ref):
        gv = g_ref[...].astype(F32)
        s_ref[...] = (gv * _sig(gv) * u_ref[...].astype(F32)).astype(BF16)

    return _swiglu_call(body, [fg, fu], 1, name)[0]


def _swiglu_bwd(ds, fg, fu, name):
    def body(ds_ref, g_ref, u_ref, dg_ref, du_ref):
        dsv, gv, uv = ds_ref[...].astype(F32), g_ref[...].astype(F32), u_ref[...].astype(F32)
        sg = _sig(gv)
        dg_ref[...] = (dsv * uv * sg * (1.0 + gv * (1.0 - sg))).astype(BF16)
        du_ref[...] = (dsv * gv * sg).astype(BF16)

    return _swiglu_call(body, [ds, fg, fu], 2, name)


def _ple_final(x2, pg, pp, tgt, g_post, name):
    t, d = x2.shape

    def body(x_ref, pg_ref, pp_ref, t_ref, g_ref, loss_ref, d3_ref, dpg_ref, dpp_ref, dg_ref):
        sg, ppv, g = _sig(pg_ref[...]), pp_ref[...], g_ref[...]
        e = sg * ppv
        r = _rinv(e)
        eh = e * r
        diff = x_ref[...] + eh * g - t_ref[...]
        loss_ref[...] = 0.5 * jnp.mean(diff * diff, axis=-1, keepdims=True)
        d3 = diff * (1.0 / d)
        d3_ref[...] = d3
        gd = d3 * g
        de = r * (gd - eh * jnp.mean(gd * eh, axis=-1, keepdims=True))
        dpg_ref[...] = (de * ppv * sg * (1.0 - sg)).astype(BF16)
        dpp_ref[...] = (de * sg).astype(BF16)
        _acc(dg_ref, d3 * eh)

    return _rows(body, t, 128, [(x2, ("t", d, 0)), (pg, ("t", d, 0)), (pp, ("t", d, 0)), (tgt, ("t", d, 0)), (g_post, ("b",))],
                 [((t, 1), F32, "t"), ((t, d), F32, "t"), ((t, d), BF16, "t"), ((t, d), BF16, "t"), ((1, d), F32, "a")], name)


def _norm_bwd(dn, dh, x, g_pre, fm, g_post, name):
    t, d = x.shape
    two = fm is not None

    def body(*refs):
        if two:
            dn_ref, dh_ref, x_ref, gp_ref, f_ref, gq_ref, dx_ref, df_ref, dgp_ref, dgq_ref = refs
        else:
            dn_ref, dh_ref, x_ref, gp_ref, dx_ref, dgp_ref = refs
        xv, dhv = x_ref[...], dh_ref[...]
        r = _rinv(xv)
        xh = xv * r
        gd = dhv * gp_ref[...]
        dx = dn_ref[...] + r * (gd - xh * jnp.mean(gd * xh, axis=-1, keepdims=True))
        dx_ref[...] = dx
        _acc(dgp_ref, dhv * xh)
        if two:
            fv = f_ref[...]
            rf = _rinv(fv)
            fh = fv * rf
            gd2 = dx * gq_ref[...]
            df_ref[...] = (rf * (gd2 - fh * jnp.mean(gd2 * fh, axis=-1, keepdims=True))).astype(BF16)
            _acc(dgq_ref, dx * fh)

    ins = [(dn, ("t", d, 0)), (dh, ("t", d, 0)), (x, ("t", d, 0)), (g_pre, ("b",))]
    outs = [((t, d), F32, "t")]
    if two:
        ins += [(fm, ("t", d, 0)), (g_post, ("b",))]
        outs += [((t, d), BF16, "t"), ((1, d), F32, "a"), ((1, d), F32, "a")]
    else:
        outs += [((1, d), F32, "a")]
    return _rows(body, t, 128, ins, outs, name)


CONV_TC = 256


def _shift_down(v, s):
    rows = lax.broadcasted_iota(jnp.int32, v.shape, 0)
    return jnp.where(rows >= s, pltpu.roll(v, s, 0), 0.0)


def _shift_up(v, s):
    n = v.shape[0]
    rows = lax.broadcasted_iota(jnp.int32, v.shape, 0)
    return jnp.where(rows < n - s, pltpu.roll(v, n - s, 0), 0.0)


def _conv_specs(t):
    nb = 1024 // CONV_TC
    seg = lambda c0: pl.BlockSpec((t, CONV_TC), lambda j, cb=c0 // CONV_TC: (0, cb + j))
    own = pl.BlockSpec((t, CONV_TC), lambda j: (0, j))
    wspec = pl.BlockSpec((3, CONV_TC), lambda j: (0, j))
    return nb, seg, own, wspec


def _conv_fwd(proj, conv_w, name, after=()):
    t = proj.shape[0]
    nb, seg, own, wspec = _conv_specs(t)

    def body(ax_ref, ab_ref, ac_ref, w_ref, *rest):
        za_ref = rest[len(after)]
        u = ac_ref[...].astype(F32) * ax_ref[...].astype(F32)
        w = w_ref[...]
        yc = w[0:1] * _shift_down(u, 2) + w[1:2] * _shift_down(u, 1) + w[2:3] * u
        za_ref[...] = (ab_ref[...].astype(F32) * yc).astype(BF16)

    return pl.pallas_call(
        body, name=name, grid=(nb,), in_specs=[seg(C_AX), seg(C_AB), seg(C_AC), wspec] + [ANY] * len(after), out_specs=own,
        out_shape=jax.ShapeDtypeStruct((t, 1024), BF16), compiler_params=_params(("parallel",)),
    )(proj, proj, proj, conv_w, *after)


def _conv_bwd(dza, proj, conv_w, name):
    t = proj.shape[0]
    nb, seg, own, wspec = _conv_specs(t)

    def body(dz_ref, ax_ref, ab_ref, ac_ref, w_ref, dax_ref, dab_ref, dac_ref, dw_ref):
        ax, ab, ac, dz = ax_ref[...].astype(F32), ab_ref[...].astype(F32), ac_ref[...].astype(F32), dz_ref[...].astype(F32)
        w = w_ref[...]
        u = ac * ax
        u1, u2 = _shift_down(u, 1), _shift_down(u, 2)
        yc = w[0:1] * u2 + w[1:2] * u1 + w[2:3] * u
        dab_ref[...] = (dz * yc).astype(BF16)
        dyc = dz * ab
        du = w[2:3] * dyc + w[1:2] * _shift_up(dyc, 1) + w[0:1] * _shift_up(dyc, 2)
        dax_ref[...] = (du * ac).astype(BF16)
        dac_ref[...] = (du * ax).astype(BF16)
        dw_ref[0:1, :] = jnp.sum(dyc * u2, axis=0, keepdims=True)
        dw_ref[1:2, :] = jnp.sum(dyc * u1, axis=0, keepdims=True)
        dw_ref[2:3, :] = jnp.sum(dyc * u, axis=0, keepdims=True)

    act = jax.ShapeDtypeStruct((t, 1024), BF16)
    return pl.pallas_call(
        body, name=name, grid=(nb,), in_specs=[own, seg(C_AX), seg(C_AB), seg(C_AC), wspec], out_specs=[own, own, own, wspec],
        out_shape=[act, act, act, jax.ShapeDtypeStruct((3, 1024), F32)], compiler_params=_params(("parallel",)),
    )(dza, proj, proj, proj, conv_w)


def _dot(a, b, dims, precision=None):
    return lax.dot_general(a, b, (dims, ((), ())), precision=precision, preferred_element_type=F32)


_CONTRACT = {"nn": ((1,), (0,)), "nt": ((1,), (1,)), "tn": ((0,), (0,))}


def _bdot_raw(a, b, mode):
    return _dot(a.astype(BF16), b.astype(BF16), _CONTRACT[mode])


@functools.partial(jax.custom_vjp, nondiff_argnums=(2,))
def _bdot(a, b, mode):
    return _bdot_raw(a, b, mode)


def _bdot_fwd(a, b, mode):
    return _bdot_raw(a, b, mode), (a, b)


def _bdot_bwd(mode, res, ct):
    a, b = res
    if mode == "nn":
        return _bdot_raw(ct, b, "nt"), _bdot_raw(a, ct, "tn")
    if mode == "nt":
        return _bdot_raw(ct, b, "nn"), _bdot_raw(ct, a, "tn")
    return _bdot_raw(b, ct, "nt"), _bdot_raw(a, ct, "nn")


_bdot.defvjp(_bdot_fwd, _bdot_bwd)


@functools.partial(jax.custom_vjp, nondiff_argnums=(2,))
def _sum_dot(ones, x, mode):
    head = x.astype(BF16)
    tail = x - head.astype(F32)
    if mode == "nn":
        return _bdot_raw(ones, head, "nn") + _bdot_raw(ones, tail, "nn")
    return _bdot_raw(head, ones, "tn") + _bdot_raw(tail, ones, "tn")


def _sum_dot_fwd(ones, x, mode):
    return _sum_dot(ones, x, mode), ones


def _sum_dot_bwd(mode, ones, ct):
    return jnp.zeros_like(ones), (_bdot_raw(ones, ct, "tn") if mode == "nn" else _bdot_raw(ones, ct, "nt"))


_sum_dot.defvjp(_sum_dot_fwd, _sum_dot_bwd)


def _gla_chunk(q, k, v, og, alr, s_in, wa, ba, gain):
    c = q.shape[0]
    z =_bdot(alr, wa, "nn") + ba
    la = (jnp.minimum(z, 0.0) - jnp.log(1.0 + jnp.exp(-jnp.abs(z)))) * (1.0 / TAU)
    row = lax.broadcasted_iota(jnp.int32, (c, c), 0)
    col = lax.broadcasted_iota(jnp.int32, (c, c), 1)
    lower = row >= col
    b = _sum_dot(lower.astype(F32), la, "nn")
    trow = lax.broadcasted_iota(jnp.int32, la.shape, 0)
    mid = jnp.sum(jnp.where(trow <= c // 2, la, 0.0), axis=0, keepdims=True)
    blast = jnp.sum(la, axis=0, keepdims=True)
    qs = q * (DK ** -0.5)
    e_up, e_dn = jnp.exp(b - mid), jnp.exp(mid - b)
    a_fwd = _bdot(qs * e_up, k * e_dn, "nt")
    a_rev = _bdot(qs * e_dn, k * e_up, "nt")
    att = jnp.where(lower, a_fwd, a_rev)
    o = _bdot(att, v, "nn") + _bdot(qs * jnp.exp(b), s_in, "nn")
    upd = _bdot(k * jnp.exp(blast - b), v, "tn")
    blast_col = _sum_dot(jnp.ones((c, DV), F32), la, "tn")
    s_out = jnp.exp(blast_col) * s_in + upd
    on = o * _rinv(o) * gain
    return on * og * _sig(og), s_out


def _gla_specs(t, rev):
    n = t // STEP_ROWS
    ch = (lambda i: n - 1 - i) if rev else (lambda i: i)
    col = lambda w, c0: pl.BlockSpec((STEP_ROWS, HEADS * w), lambda i, cb=c0 // (HEADS * w): (ch(i), cb))
    whole = lambda shape: pl.BlockSpec(shape, lambda i, nd=len(shape): (0,) * nd)
    specs = dict(
        q=col(DK, C_Q), k=col(DK, C_K), v=col(DV, C_V), og=col(DV, C_OG),
        alr=pl.BlockSpec((STEP_ROWS, 128), lambda i: (ch(i), C_ALR // 128)),
        wa=whole((128, HEADS * DK)), ba=whole((1, HEADS * DK)), gain=whole((1, DV)),
        state=pl.BlockSpec((STEP_CHUNKS, HEADS, DK, DV), lambda i: (ch(i), 0, 0, 0)),
        odk=pl.BlockSpec((STEP_ROWS, HEADS * DK), lambda i: (ch(i), 0)), odv=pl.BlockSpec((STEP_ROWS, HEADS * DV), lambda i: (ch(i), 0)),
        oalr=pl.BlockSpec((STEP_ROWS, 128), lambda i: (ch(i), 0)), whole=whole,
    )
    return n, specs


def _head_cols(h):
    return slice(h * DK, (h + 1) * DK), slice(h * DV, (h + 1) * DV)


def _gla_fwd(proj, wa, ba, gain, name):
    t = proj.shape[0]
    n, sp = _gla_specs(t, False)

    def body(q_ref, k_ref, v_ref, og_ref, alr_ref, wa_ref, ba_ref, g_ref, zb_ref, st_ref, s_scr):
        @pl.when(pl.program_id(0) == 0)
        def _():
            s_scr[...] = jnp.zeros_like(s_scr)

        state = [s_scr[h] for h in range(HEADS)]
        for c in range(STEP_CHUNKS):
            rows = slice(c * CHUNK, (c + 1) * CHUNK)
            alr = alr_ref[rows, :].astype(F32)
            for h in range(HEADS):
                kc, vc = _head_cols(h)
                st_ref[c, h] = state[h]
                zb, state[h] = _gla_chunk(q_ref[rows, kc].astype(F32), k_ref[rows, kc].astype(F32), v_ref[rows, vc].astype(F32),
                                          og_ref[rows, vc].astype(F32), alr, state[h], wa_ref[:, kc].astype(F32), ba_ref[:, kc], g_ref[...])
                zb_ref[rows, vc] = zb.astype(BF16)
        for h in range(HEADS):
            s_scr[h] = state[h]

    return pl.pallas_call(
        body, name=name, grid=(n,),
        in_specs=[sp["q"], sp["k"], sp["v"], sp["og"], sp["alr"], sp["wa"], sp["ba"], sp["gain"]],
        out_specs=[sp["odv"], sp["state"]],
        out_shape=[jax.ShapeDtypeStruct((t, HEADS * DV), BF16), jax.ShapeDtypeStruct((t // CHUNK, HEADS, DK, DV), F32)],
        scratch_shapes=[pltpu.VMEM((HEADS, DK, DV), F32)],
        compiler_params=_params(("arbitrary",)),
    )(proj, proj, proj, proj, proj, wa, ba, gain)


def _gla_bwd(dzb, proj, states, wa, ba, gain, name):
    t = proj.shape[0]
    n, sp = _gla_specs(t, True)

    def body(dz_ref, q_ref, k_ref, v_ref, og_ref, alr_ref, st_ref, wa_ref, ba_ref, g_ref,
             dq_ref, dk_ref, dv_ref, dog_ref, dalr_ref, dwa_ref, dba_ref, dg_ref, ds_scr):
        @pl.when(pl.program_id(0) == 0)
        def _():
            ds_scr[...] = jnp.zeros_like(ds_scr)
            dwa_ref[...] = jnp.zeros_like(dwa_ref)
            dba_ref[...] = jnp.zeros_like(dba_ref)
            dg_ref[...] = jnp.zeros_like(dg_ref)

        dstate = [ds_scr[h] for h in range(HEADS)]
        dwa_sum, dba_sum, dgain_sum = [None] * HEADS, [None] * HEADS, None
        for c in reversed(range(STEP_CHUNKS)):
            rows = slice(c * CHUNK, (c + 1) * CHUNK)
            alr = alr_ref[rows, :].astype(F32)
            dalr_sum = None
            for h in range(HEADS):
                kc, vc = _head_cols(h)
                args = (q_ref[rows, kc].astype(F32), k_ref[rows, kc].astype(F32), v_ref[rows, vc].astype(F32), og_ref[rows, vc].astype(F32),
                        alr, st_ref[c, h], wa_ref[:, kc].astype(F32), ba_ref[:, kc], g_ref[...])
                _, vjp = jax.vjp(_gla_chunk, *args)
                dq, dk, dv, dog, dalr, dstate[h], dwa, dba, dgain = vjp((dz_ref[rows, vc].astype(F32), dstate[h]))
                dq_ref[rows, kc] = dq.astype(BF16)
                dk_ref[rows, kc] = dk.astype(BF16)
                dv_ref[rows, vc] = dv.astype(BF16)
                dog_ref[rows, vc] = dog.astype(BF16)
                dwa_sum[h] = dwa if dwa_sum[h] is None else dwa_sum[h] + dwa
                dba_sum[h] = dba if dba_sum[h] is None else dba_sum[h] + dba
                dalr_sum = dalr if dalr_sum is None else dalr_sum + dalr
                dgain_sum = dgain if dgain_sum is None else dgain_sum + dgain
            dalr_ref[rows, :] = dalr_sum
        for h in range(HEADS):
            ds_scr[h] = dstate[h]
            dwa_ref[h] += dwa_sum[h]
            dba_ref[h] += dba_sum[h]
        dg_ref[...] += dgain_sum

    whole = sp["whole"]
    return pl.pallas_call(
        body, name=name, grid=(n,),
        in_specs=[sp["odv"], sp["q"], sp["k"], sp["v"], sp["og"], sp["alr"], sp["state"], sp["wa"], sp["ba"], sp["gain"]],
        out_specs=[sp["odk"], sp["odk"], sp["odv"], sp["odv"], sp["oalr"], whole((HEADS, 128, DK)), whole((HEADS, 1, DK)), whole((1, DV))],
        out_shape=[jax.ShapeDtypeStruct((t, HEADS * DK), BF16), jax.ShapeDtypeStruct((t, HEADS * DK), BF16),
                   jax.ShapeDtypeStruct((t, HEADS * DV), BF16), jax.ShapeDtypeStruct((t, HEADS * DV), BF16),
                   jax.ShapeDtypeStruct((t, 128), F32), jax.ShapeDtypeStruct((HEADS, 128, DK), F32),
                   jax.ShapeDtypeStruct((HEADS, 1, DK), F32), jax.ShapeDtypeStruct((1, DV), F32)],
        scratch_shapes=[pltpu.VMEM((HEADS, DK, DV), F32)],
        compiler_params=_params(("arbitrary",)),
    )(dzb, proj, proj, proj, proj, proj, states, wa, ba, gain)


def _local_step(x, p, tgt, gather_start, gather_pass_on, gather_finish, scatter_start, scatter_next, small):
    b_alpha, gain = small["b_alpha_up"], small["gla_head_gain"]
    gather_start(0, ())
    w = dict(gather_finish(0, ()))
    conv_w, w_alpha = w["conv_w"], w["w_alpha_up"]
    wa_p = jnp.zeros((128, HEADS * DK), BF16).at[:GATE_RANK].set(w_alpha.astype(BF16))

    t2 = gather_start(2, (w["in_rest_t"], gather_start(1, ())))
    h1 = _rms_fwd(x, small["g_pre_mix"], "rms_pre_mix")
    proj = _mm(h1, w["in_rest_t"], "nt", BF16, "mm_proj", after=(t2,))
    proj_gates = _mm(h1, w["in_gates_t"], "nt", BF16, "mm_proj_gates", after=(t2,))
    za = _conv_fwd(proj, conv_w, "conv_fwd", after=(gather_pass_on(1, (proj,)),))
    zb, states = _gla_fwd(proj, wa_p, b_alpha, gain, "gla_fwd")
    t3 = gather_start(3, (zb, za))
    w.update(gather_finish(1, (t3,)))
    ya = _mm(za, w["a_out"], "nn", BF16, "mm_ya")
    yb = _mm(zb, w["b_out"], "nn", BF16, "mm_yb")
    mix = _mix_fwd(proj_gates, ya, yb, "mix_fwd")
    m2 = _mm(mix, w["mix"], "nn", F32, "mm_mix")
    t4 = gather_start(4, (m2,))
    x1, h2 = _post_pre(x, m2, small["g_post_mix"], small["g_pre_ffn"], "norm_mix_ffn", after=(gather_pass_on(2, (m2,)),))
    w.update(gather_finish(2, (h2, t4)))
    fu = _mm(h2, w["up_t"], "nt", BF16, "mm_up")
    fg = _mm(h2, w["gate_t"], "nt", BF16, "mm_gate", after=(gather_pass_on(3, (fu,)),))
    s = _swiglu_fwd(fg, fu, "swiglu_fwd")
    w.update(gather_finish(3, (s,)))
    f = _mm(s, w["down"], "nn", F32, "mm_down", after=(gather_pass_on(4, (s,)),), **LONG_K)
    x2, h3 = _post_pre(x1, f, small["g_post_ffn"], small["g_pre_ple"], "norm_ffn_ple")
    w.update(gather_finish(4, (h3,)))
    pg = _mm(h3, w["pg"], "nn", F32, "mm_pg")
    p_bf = p.astype(BF16)
    pp = _mm(p_bf, w["pp"], "nn", F32, "mm_pp", b3=True, tm=2048)
    loss_rows, d3, dpg, dpp, dg_post_ple = _ple_final(x2, pg, pp, tgt, small["g_post_ple"], "ple_final")

    gw = {}
    gw["pp"] = _mm(p_bf, dpp, "tn", BF16, "mm_dw_pp", out3=True)
    gw["pg"] = _mm(h3, dpg, "tn", BF16, "mm_dw_pg")
    dh3 = _mm(dpg, w["pg"], "nt", F32, "mm_dh3", after=(scatter_start(["w_ple_proj", "w_ple_gate"], gw),))
    d2, df, dg_pre_ple, dg_post_ffn = _norm_bwd(d3, dh3, x2, small["g_pre_ple"], f, small["g_post_ffn"], "norm_bwd_ple_ffn")
    gw["down"] = _mm(s, df, "tn", BF16, "mm_dw_down", tm=1408)
    ds = _mm(df, w["down"], "nt", BF16, "mm_ds", after=(scatter_start(["w_ff_down"], gw),))
    dfg, dfu = _swiglu_bwd(ds, fg, fu, "swiglu_bwd")
    gw["gate_t"] = _mm(dfg, h2, "tn", BF16, "mm_dw_gate", tm=1408)
    gw["up_t"] = _mm(dfu, h2, "tn", BF16, "mm_dw_up", after=(gw["gate_t"],), tm=1408)
    dh2 = _mm(dfg, w["gate_t"], "nn", F32, "mm_dh2_gate", after=(scatter_start(["w_ff_gate", "w_ff_up"], gw),), **LONG_K)
    dh2 = _mm(dfu, w["up_t"], "nn", F32, "mm_dh2_up", add=dh2, **LONG_K)
    d1, dm2, dg_pre_ffn, dg_post_mix = _norm_bwd(d2, dh2, x1, small["g_pre_ffn"], m2, small["g_post_mix"], "norm_bwd_ffn_mix")
    dmix = _mm(dm2, w["mix"], "nt", F32, "mm_dmix")
    gw["mix"] = _mm(mix, dm2, "tn", BF16, "mm_dw_mix")
    dgab, dya, dyb = _mix_bwd(dmix, proj_gates, ya, yb, "mix_bwd")
    gw["in_gates_t"] = _mm(dgab, h1, "tn", BF16, "mm_dw_in_gates")
    dza = _mm(dya, w["a_out"], "nt", BF16, "mm_dza", after=(scatter_start(["w_mix_out"], gw), gw["in_gates_t"]))
    gw["a_out"] = _mm(za, dya, "tn", BF16, "mm_dw_a_out")
    gw["b_out"] = _mm(zb, dyb, "tn", BF16, "mm_dw_b_out", after=(gw["a_out"],))
    dzb = _mm(dyb, w["b_out"], "nt", BF16, "mm_dzb", after=(scatter_start(["w_a_out", "w_b_out"], gw),))
    dax, dab, dac, dconv = _conv_bwd(dza, proj, conv_w, "conv_bwd")
    dq, dk, dv, dog, dalr, dwa, dba, dgain = _gla_bwd(dzb, proj, states, wa_p, b_alpha, gain, "gla_bwd")
    drest = jnp.concatenate([dax, dab, dac, dq, dk, dv, dog, dalr.astype(BF16)], axis=1)
    gw["in_rest_t"] = _mm(drest, h1, "tn", BF16, "mm_dw_in_rest", tm=896)
    dh1 = _mm(dgab, w["in_gates_t"], "nn", F32, "mm_dh1_gates", after=(scatter_start(["w_in"], gw),), tm=1024, tk=4096, n_outer=True)
    dh1 = _mm(drest, w["in_rest_t"], "nn", F32, "mm_dh1_rest", add=dh1, after=(scatter_next((dh1,)),), **LONG_K)
    grad_x, dg_pre_mix = _norm_bwd(d1, dh1, x, small["g_pre_mix"], None, None, "norm_bwd_mix")

    gs = dict(
        conv_w=dconv,
        w_alpha_up=jnp.transpose(dwa[:, :GATE_RANK, :], (1, 0, 2)).reshape(GATE_RANK, HEADS * DK),
        b_alpha_up=dba.reshape(1, HEADS * DK), gla_head_gain=dgain,
        g_pre_mix=dg_pre_mix, g_post_mix=dg_post_mix, g_pre_ffn=dg_pre_ffn, g_post_ffn=dg_post_ffn,
        g_pre_ple=dg_pre_ple, g_post_ple=dg_post_ple,
    )
    return loss_rows, grad_x, gs


def _place():
    x, y, c = lax.axis_index("x"), lax.axis_index("y"), lax.axis_index("c")
    return x, y, c, [(1 - x, y), (x, 1 - y), (1 - x, 1 - y)]


def _pair_add(mine8, got4, name):
    _, r, cols = mine8.shape
    steps, blk, at = _tiles(r, cols)
    core = lax.axis_index("c").astype(jnp.int32).reshape(1)

    def body(c_ref, a_ref, b_ref, o_ref):
        o_ref[...] = (a_ref[...].astype(F32) + b_ref[...].astype(F32)).astype(BF16)

    return pl.pallas_call(
        body, name=name,
        grid_spec=pltpu.PrefetchScalarGridSpec(
            num_scalar_prefetch=1, grid=(4, steps),
            in_specs=[pl.BlockSpec((None,) + blk, lambda ch, i, c_ref: (2 * ch + c_ref[0],) + at(i)),
                      pl.BlockSpec((None,) + blk, lambda ch, i, c_ref: (ch,) + at(i))],
            out_specs=pl.BlockSpec((None,) + blk, lambda ch, i, c_ref: (ch,) + at(i))),
        out_shape=jax.ShapeDtypeStruct((4, r, cols), BF16),
        compiler_params=_params(("parallel", "parallel")),
    )(core, mine8, got4)


HBM = pl.BlockSpec(memory_space=pltpu.HBM)
SEM = pl.BlockSpec(memory_space=pltpu.SEMAPHORE)
EFFECT = pltpu.SideEffectType.DATAFLOW_SIDE_EFFECTING


def _in_hbm(a):
    return pltpu.with_memory_space_constraint(a, pltpu.HBM)


def _remote_copies(plan, srcs, lands, send_sems, recv_sems):
    return [pltpu.make_async_remote_copy(src_ref=s, dst_ref=d, send_sem=send_sems.at[i], recv_sem=recv_sems.at[i], device_id=peer,
                                         device_id_type=MESH) for i, (s, d, peer) in enumerate(plan(srcs, lands))]


def _copies_start(plan, n_copies, srcs, land_shapes, name, after=()):
    ns, nl = len(srcs), len(land_shapes)

    def body(*refs):
        send_sems, recv_sems = refs[ns + nl + len(after):ns + nl + len(after) + 2]
        for cp in _remote_copies(plan, refs[:ns], refs[ns:ns + nl], send_sems, recv_sems):
            cp.start()
        refs[-1][...] = jnp.zeros((8, 128), F32)

    sems = pltpu.SemaphoreType.DMA((n_copies,))
    return pl.pallas_call(
        body, name=name,
        out_shape=(sems, sems, *[pltpu.HBM(s.shape, s.dtype) for s in srcs], *[pltpu.HBM(s.shape, s.dtype) for s in land_shapes],
                   jax.ShapeDtypeStruct((8, 128), F32)),
        in_specs=[HBM] * (ns + nl) + [ANY] * len(after),
        out_specs=(SEM, SEM, *[HBM] * (ns + nl), pl.BlockSpec(memory_space=pltpu.VMEM)),
        input_output_aliases={i: 2 + i for i in range(ns + nl)},
        compiler_params=pltpu.CompilerParams(has_side_effects=EFFECT),
    )(*[_in_hbm(s) for s in srcs], *[_in_hbm(lax.empty(s.shape, s.dtype)) for s in land_shapes], *after)


def _copies_wait(plan, state, ns, name, after=()):
    send_sems, recv_sems, *arrs = state[:-1]
    n = len(arrs)

    def body(*refs):
        cps = _remote_copies(plan, refs[:ns], refs[ns:n], refs[n], refs[n + 1])
        for cp in cps:
            cp.wait_send()
        for cp in cps:
            cp.wait_recv()

    out = pl.pallas_call(
        body, name=name, out_shape=tuple(pltpu.HBM(a.shape, a.dtype) for a in arrs),
        in_specs=[HBM] * n + [SEM, SEM] + [ANY] * len(after), out_specs=tuple([HBM] * n),
        input_output_aliases={i: i for i in range(n)},
        compiler_params=pltpu.CompilerParams(has_side_effects=EFFECT),
    )(*arrs, send_sems, recv_sems, *after)
    return list(out[:ns]), list(out[ns:])


def _copies_relay(plan, state, ns, next_plan, n_next, name, after=()):
    send_sems, recv_sems, *arrs = state[:-1]
    n = len(arrs)

    def body(*refs):
        cps = _remote_copies(plan, refs[:ns], refs[ns:n], refs[n], refs[n + 1])
        for cp in cps:
            cp.wait_send()
        for cp in cps:
            cp.wait_recv()
        outs = refs[n + 2 + len(after):]
        for cp in _remote_copies(next_plan, refs[:ns], refs[ns:n], outs[0], outs[1]):
            cp.start()
        outs[-1][...] = jnp.zeros((8, 128), F32)

    sems = pltpu.SemaphoreType.DMA((n_next,))
    return pl.pallas_call(
        body, name=name,
        out_shape=(sems, sems, *[pltpu.HBM(a.shape, a.dtype) for a in arrs], jax.ShapeDtypeStruct((8, 128), F32)),
        in_specs=[HBM] * n + [SEM, SEM] + [ANY] * len(after),
        out_specs=(SEM, SEM, *[HBM] * n, pl.BlockSpec(memory_space=pltpu.VMEM)),
        input_output_aliases={i: 2 + i for i in range(n)},
        compiler_params=pltpu.CompilerParams(has_side_effects=EFFECT),
    )(*arrs, send_sems, recv_sems, *after)


def _pass_on_plan(srcs, lands):
    x, y, c, chips = _place()
    return [(l.at[4 * px + 2 * py + c], l.at[4 * px + 2 * py + c], (x, y, 1 - c)) for l in lands for px, py in chips]


def _gather_plan(srcs, lands):
    x, y, c, chips = _place()
    peers = [(x, y, 1 - c)] + [(*chip, c) for chip in chips]
    return [(s, l.at[4 * x + 2 * y + c], peer) for s, l in zip(srcs, lands) for peer in peers]


def _gather_plan_near(srcs, lands):
    x, y, c, _ = _place()
    peers = [(x, y, 1 - c), (1 - x, y, c), (x, 1 - y, c)]
    return [(s, l.at[4 * x + 2 * y + c], peer) for s, l in zip(srcs, lands) for peer in peers]


def _scatter_plan(srcs, lands):
    x, y, c, _ = _place()
    peers = [(1 - x if k & 4 else x, 1 - y if k & 2 else y, 1 - c if k & 1 else c) for k in range(1, N_DEV)]
    return [(s.at[4 * px + 2 * py + pc], l.at[4 * x + 2 * y + c], (px, py, pc)) for s, l in zip(srcs, lands) for px, py, pc in peers]


def _everyone_plan(srcs, lands):
    x, y, c, _ = _place()
    peers = [(1 - x if k & 4 else x, 1 - y if k & 2 else y, 1 - c if k & 1 else c) for k in range(1, N_DEV)]
    return [(s, l.at[4 * x + 2 * y + c], peer) for s, l in zip(srcs, lands) for peer in peers]


def _sum_parts(got, own, me, name):
    def body(me_ref, got_ref, own_ref, o_ref):
        acc = jnp.where(me_ref[0] == 0, own_ref[...], got_ref[0])
        for d in range(1, N_DEV):
            acc = acc + jnp.where(me_ref[0] == d, own_ref[...], got_ref[d])
        o_ref[...] = acc

    return pl.pallas_call(
        body, name=name,
        grid_spec=pltpu.PrefetchScalarGridSpec(
            num_scalar_prefetch=1, grid=(1,),
            in_specs=[pl.BlockSpec(got.shape, lambda i, me_ref: (0, 0, 0)), pl.BlockSpec(own.shape, lambda i, me_ref: (0, 0))],
            out_specs=pl.BlockSpec(own.shape, lambda i, me_ref: (0, 0))),
        out_shape=jax.ShapeDtypeStruct(own.shape, F32),
    )(me.astype(jnp.int32).reshape(1), got, own)


def _sibling_plan(srcs, lands):
    x, y, c, _ = _place()
    return [(s.at[2 * ch + 1 - c], l.at[ch], (x, y, 1 - c)) for s, l in zip(srcs, lands) for ch in range(4)]


def _chip_plan(srcs, lands):
    x, y, c, chips = _place()
    return [(s.at[2 * px + py], l.at[2 * x + y], (px, py, c)) for s, l in zip(srcs, lands) for px, py in chips]


def _put_own(shard, zone, me, name):
    r, c = shard.shape
    tr = r if r <= 256 else _pick(r, (256, 64))

    def body(me_ref, s_ref, z_ref, o_ref):
        o_ref[...] = s_ref[...]

    return pl.pallas_call(
        body, name=name,
        grid_spec=pltpu.PrefetchScalarGridSpec(
            num_scalar_prefetch=1, grid=(r // tr,),
            in_specs=[pl.BlockSpec((tr, c), lambda i, me_ref: (i, 0)), ANY],
            out_specs=pl.BlockSpec((None, tr, c), lambda i, me_ref: (me_ref[0], i, 0))),
        out_shape=jax.ShapeDtypeStruct(zone.shape, zone.dtype), input_output_aliases={2: 0},
        compiler_params=_params(("arbitrary",)),
    )(me.astype(jnp.int32).reshape(1), shard, zone)


def _gather_relay(lands, name):
    n = len(lands)

    def body(*refs):
        zones, outs = refs[:n], refs[n:2 * n]
        send_sems, recv_sems = refs[2 * n:]
        x, y, c, _ = _place()
        south = c == 0
        near_x, near_y, across = 4 * (1 - x) + 2 * y + c, 4 * x + 2 * (1 - y) + c, 4 * (1 - x) + 2 * (1 - y) + c
        passed = jnp.where(south, near_y, near_x)
        onward = (jnp.where(south, 1 - x, x), jnp.where(south, y, 1 - y), c)

        def copy(a, k, slot, to):
            return pltpu.make_async_remote_copy(src_ref=zones[a].at[slot], dst_ref=outs[a].at[slot], send_sem=send_sems.at[a, k],
                                                recv_sem=recv_sems.at[a, k], device_id=to, device_id_type=MESH)

        first = [copy(a, 0, passed, onward) for a in range(n)]
        first += [copy(a, 1 + j, slot, (x, y, 1 - c)) for j, slot in enumerate((near_x, near_y)) for a in range(n)]
        for cp in first:
            cp.start()
        last = []
        for a in range(n):
            copy(a, 0, across, onward).wait_recv()
            last.append(copy(a, 3, across, (x, y, 1 - c)))
            last[-1].start()
        for cp in first[n:] + last:
            cp.wait_recv()
        for cp in first + last:
            cp.wait_send()

    return pl.pallas_call(
        body, name=name, in_specs=[ANY] * n, out_specs=[ANY] * n,
        out_shape=[jax.ShapeDtypeStruct(l.shape, l.dtype) for l in lands],
        input_output_aliases={a: a for a in range(n)},
        scratch_shapes=[pltpu.SemaphoreType.DMA((n, 4)), pltpu.SemaphoreType.DMA((n, 4))],
    )(*lands)


def _adamw_parts(w, got, mine, me, m, v, name, after=()):
    r, c = w.shape
    n_parts = got.shape[0]
    steps, blk, at = _tiles(r, c)

    def body(me_ref, w_ref, got_ref, own_ref, m_ref, v_ref, *rest):
        go_ref, d_ref, mo_ref, vo_ref = rest[len(after):]
        own = own_ref[...].astype(F32)
        gv = jnp.where(me_ref[0] == 0, own, got_ref[0].astype(F32))
        for d in range(1, n_parts):
            gv = gv + jnp.where(me_ref[0] == d, own, got_ref[d].astype(F32))
        _adamw_math(gv, w_ref, m_ref, v_ref, go_ref, d_ref, mo_ref, vo_ref)

    tile = pl.BlockSpec(blk, lambda i, me_ref: at(i))
    out = jax.ShapeDtypeStruct((r, c), F32)
    return pl.pallas_call(
        body, name=name,
        grid_spec=pltpu.PrefetchScalarGridSpec(
            num_scalar_prefetch=1, grid=(steps,),
            in_specs=[tile, pl.BlockSpec((n_parts,) + blk, lambda i, me_ref: (0,) + at(i)),
                      pl.BlockSpec((None,) + blk, lambda i, me_ref: (me_ref[0],) + at(i)), tile, tile] + [ANY] * len(after),
            out_specs=[tile] * 4),
        out_shape=[out] * 4, compiler_params=_params(("parallel",)),
    )(me.astype(jnp.int32).reshape(1), w, got, mine, m, v, *after)


def _adamw_math(gv, w_ref, m_ref, v_ref, go_ref, d_ref, mo_ref, vo_ref):
    mn = B1 * m_ref[...] + (1.0 - B1) * gv
    vn = B2 * v_ref[...] + (1.0 - B2) * (gv * gv)
    m_hat = mn / (1.0 - B1 ** STEP)
    v_hat = vn / (1.0 - B2 ** STEP)
    go_ref[...] = gv
    d_ref[...] = -LR * (m_hat / (jnp.sqrt(v_hat) + ADAM_EPS) + WD * w_ref[...])
    mo_ref[...] = mn
    vo_ref[...] = vn


def _adamw(w, g, m, v, name):
    def body(w_ref, g_ref, m_ref, v_ref, go_ref, d_ref, mo_ref, vo_ref):
        _adamw_math(g_ref[...], w_ref, m_ref, v_ref, go_ref, d_ref, mo_ref, vo_ref)

    tile = pl.BlockSpec(w.shape, lambda i: (0, 0))
    out = jax.ShapeDtypeStruct(w.shape, F32)
    return pl.pallas_call(
        body, name=name, grid=(1,), in_specs=[tile] * 4, out_specs=[tile] * 4, out_shape=[out] * 4,
        compiler_params=_params(("parallel",)),
    )(w, g, m, v)


BIG = ["w_in", "w_a_out", "w_b_out", "w_mix_out", "w_ff_gate", "w_ff_up", "w_ff_down", "w_ple_gate", "w_ple_proj"]
TRANSPOSED = ["w_in", "w_ff_gate", "w_ff_up"]
GRAD_OF = dict(w_ple_proj="pp", w_ple_gate="pg", w_ff_down="down", w_ff_gate="gate_t", w_ff_up="up_t", w_mix_out="mix", w_a_out="a_out",
               w_b_out="b_out")
SMALL = ["conv_w", "w_alpha_up", "b_alpha_up", "gla_head_gain", "g_pre_mix", "g_post_mix", "g_pre_ffn", "g_post_ffn", "g_pre_ple", "g_post_ple"]
WEIGHTS = ["w_in", "conv_w", "w_a_out", "w_alpha_up", "b_alpha_up", "gla_head_gain", "w_b_out", "w_mix_out", "g_pre_mix", "g_post_mix",
           "g_pre_ffn", "g_post_ffn", "w_ff_gate", "w_ff_up", "w_ff_down", "g_pre_ple", "g_post_ple", "w_ple_gate", "w_ple_proj"]


def _in_t_from_blocks(z):
    w = z.reshape(-1, z.shape[-1])
    return w[R_GA:R_END], jnp.concatenate([w[:R_GA], jnp.zeros((REST - R_GA, w.shape[1]), w.dtype)], axis=0)


def _blocks_from_in_t(g_gates, g_rest):
    per = R_END // N_DEV

    def rows(lo, hi):
        out = []
        for n0, n1, g in ((0, R_GA, g_rest), (R_GA, R_END, g_gates)):
            a, e = max(lo, n0), min(hi, n1)
            if a < e:
                out.append(g[a - n0:e - n0])
        return out

    return jnp.stack([jnp.concatenate(rows(b * per, (b + 1) * per), axis=0) for b in range(N_DEV)])


def _cols_to_full(g8):
    n, r, c = g8.shape
    return jnp.transpose(g8, (1, 0, 2)).reshape(r, n * c)


def _full_to_cols(a):
    r, c = a.shape
    return jnp.transpose(a.reshape(r, N_DEV, c // N_DEV), (1, 0, 2))


def _pack(arrs, rows):
    flat = jnp.concatenate([a.reshape(-1) for a in arrs])
    return jnp.pad(flat, (0, rows * 128 - flat.shape[0])).reshape(rows, 128)


def _unpack(packed, shapes):
    flat, out, o = packed.reshape(-1), [], 0
    for s in shapes:
        size = 1
        for d in s:
            size *= d
        out.append(flat[o:o + size].reshape(s))
        o += size
    return out


def kernel(x, p, w_in, conv_w, w_a_out, w_alpha_up, b_alpha_up, gla_head_gain, w_b_out, w_mix_out, g_pre_mix, g_post_mix, g_pre_ffn, g_post_ffn, w_ff_gate, w_ff_up, w_ff_down, g_pre_ple, g_post_ple, w_ple_gate, w_ple_proj, loss_target, m_w_in, m_conv_w, m_w_a_out, m_w_alpha_up, m_b_alpha_up, m_gla_head_gain, m_w_b_out, m_w_mix_out, m_g_pre_mix, m_g_post_mix, m_g_pre_ffn, m_g_post_ffn, m_w_ff_gate, m_w_ff_up, m_w_ff_down, m_g_pre_ple, m_g_post_ple, m_w_ple_gate, m_w_ple_proj, v_w_in, v_conv_w, v_w_a_out, v_w_alpha_up, v_b_alpha_up, v_gla_head_gain, v_w_b_out, v_w_mix_out, v_g_pre_mix, v_g_post_mix, v_g_pre_ffn, v_g_post_ffn, v_w_ff_gate, v_w_ff_up, v_w_ff_down, v_g_pre_ple, v_g_post_ple, v_w_ple_gate, v_w_ple_proj):
    args = dict(locals())
    shard = lambda n, a: jnp.transpose(a[0]) if n in TRANSPOSED else a[0]
    wts = {n: shard(n, args[n]) for n in WEIGHTS}
    mom = {n: shard(n, args["m_" + n]) for n in WEIGHTS}
    var = {n: shard(n, args["v_" + n]) for n in WEIGHTS}
    me =4 * lax.axis_index("x") + 2 * lax.axis_index("y") + lax.axis_index("c")

    groups = [["w_in", "conv_w", "w_alpha_up"], ["w_a_out", "w_b_out", "w_mix_out"], ["w_ff_gate", "w_ff_up"], ["w_ff_down"],
              ["w_ple_gate", "w_ple_proj"]]
    grad_groups = []
    rows_full = lambda g: g.reshape(-1, g.shape[-1])
    gathers, scatters = {}, {}

    def gather_start(gi, after):
        if gi not in gathers:
            shards = [wts[n].astype(BF16) if n in BIG else wts[n] for n in groups[gi]]
            zones = [jax.ShapeDtypeStruct((N_DEV,) + s.shape, s.dtype) for s in shards]
            plan, peers = (_gather_plan_near, 3) if gi == 0 else (_gather_plan, 4)
            gathers[gi] = (shards, _copies_start(plan, peers * len(shards), shards, zones, "gather_start_%d" % gi, after))
        return gathers[gi][1][-1]

    def gather_pass_on(gi, after):
        shards, state = gathers[gi]
        gathers[gi] = (shards, _copies_relay(_gather_plan, state, len(shards), _pass_on_plan, 3 * len(shards), "gather_pass_on_%d" % gi, after))
        return gathers[gi][1][-1]

    def gather_finish(gi, after):
        shards, state = gathers[gi]
        shards, zones = _copies_wait(_gather_plan_near if gi == 0 else _pass_on_plan, state, len(shards), "gather_wait_%d" % gi, after)
        if gi == 0:
            zones = _gather_relay(zones, "gather_relay_%d" % gi)
            gather_start(1, (zones[0],))
        g8 = {n: _put_own(s, z, me, "gather_own_" + n) for n, s, z in zip(groups[gi], shards, zones)}
        if gi == 0:
            in_gates_t, in_rest_t = _in_t_from_blocks(g8["w_in"])
            return dict(in_gates_t=in_gates_t, in_rest_t=in_rest_t,
                        conv_w=_cols_to_full(g8["conv_w"]), w_alpha_up=_cols_to_full(g8["w_alpha_up"]))
        if gi == 1:
            return dict(a_out=_cols_to_full(g8["w_a_out"]), b_out=_cols_to_full(g8["w_b_out"]), mix=rows_full(g8["w_mix_out"]))
        if gi == 2:
            return dict(gate_t=rows_full(g8["w_ff_gate"]), up_t=rows_full(g8["w_ff_up"]))
        if gi == 3:
            return dict(down=rows_full(g8["w_ff_down"]))
        return dict(pg=rows_full(g8["w_ple_gate"]), pp=g8["w_ple_proj"])

    def scatter_start(names, gw):
        gi = len(grad_groups)
        grad_groups.append(names)
        full = {n: _blocks_from_in_t(gw["in_gates_t"], gw["in_rest_t"]) if n == "w_in" else gw[GRAD_OF[n]] for n in names}
        for n in names:
            if n in ("w_a_out", "w_b_out"):
                full[n] = _full_to_cols(full[n])
        parts = [full[n] if full[n].ndim == 3 else full[n].reshape(N_DEV, -1, full[n].shape[-1]) for n in names]
        if names == ["w_in"]:
            quarter = [jax.ShapeDtypeStruct((4,) + a.shape[1:], a.dtype) for a in parts]
            scatters[gi] = _copies_start(_sibling_plan, 4 * len(parts), parts, quarter, "scatter_sibling_start_%d" % gi)
        else:
            scatters[gi] = _copies_start(_scatter_plan, (N_DEV - 1) * len(parts), parts, parts, "scatter_start_%d" % gi)
        return scatters[gi][-1]

    def scatter_next(after):
        gi, names = len(grad_groups) - 1, grad_groups[-1]
        parts, from_sibling = _copies_wait(_sibling_plan, scatters[gi], len(names), "scatter_sibling_wait_%d" % gi, after)
        parts = [_pair_add(a, b, "scatter_add_%d_%s" % (gi, n)) for n, a, b in zip(names, parts, from_sibling)]
        scatters[gi] = _copies_start(_chip_plan, 3 * len(parts), parts, parts, "scatter_start_%d" % gi)
        return scatters[gi][-1]

    small = {n: wts[n].reshape(1, -1) for n in SMALL[2:]}

    loss_rows, grad_x, gs = _local_step(x[0], p[0, 0], loss_target[0], gather_start, gather_pass_on, gather_finish, scatter_start,
                                        scatter_next, small)
    gs["loss"] = jnp.sum(loss_rows).reshape(1, 1)

    small_shapes = [gs[n].shape for n in SMALL]
    gs_packed = _pack([gs[n] for n in SMALL + ["loss"]], 192)
    small_state = _copies_start(_everyone_plan, N_DEV - 1, [gs_packed], [jax.ShapeDtypeStruct((N_DEV,) + gs_packed.shape, F32)],
                                "small_start", (grad_x,))

    res, done = {}, (small_state[-1],)
    for gi, names in enumerate(grad_groups):
        plan, slot = (_chip_plan, me // 2) if names == ["w_in"] else (_scatter_plan, me)
        mine, got = _copies_wait(plan, scatters[gi], len(names), "scatter_wait_%d" % gi, done)
        for n, g, own in zip(names, got, mine):
            res[n] = _adamw_parts(wts[n], g, own, slot, mom[n], var[n], "adamw_" + n)
        done = tuple(res[n][1] for n in names)

    (gs_own,), (gs_got,) = _copies_wait(_everyone_plan, small_state, 1, "small_wait", done)
    gsum = dict(zip(SMALL + ["loss"], _unpack(_sum_parts(gs_got, gs_own, me, "small_sum"), small_shapes + [(1, 1)])))
    loss = gsum["loss"].reshape(())
    gsum["conv_w"] = lax.dynamic_index_in_dim(gsum["conv_w"].reshape(3, N_DEV, -1), me, axis=1, keepdims=False)
    gsum["w_alpha_up"] = lax.dynamic_index_in_dim(gsum["w_alpha_up"].reshape(GATE_RANK, N_DEV, -1), me, axis=1, keepdims=False)

    shard_shapes = [wts[n].shape for n in SMALL]
    packed = [_pack([d[n] for n in SMALL], 120) for d in (wts, gsum, mom, var)]
    outs = [_unpack(o, shard_shapes) for o in _adamw(*packed, "adamw_small")]
    for i, n in enumerate(SMALL):
        res[n] = [o[i] for o in outs]

    back = lambda n, a: (jnp.transpose(a) if n in TRANSPOSED else a)[None]
    return (loss, grad_x[None], *[back(n, res[n][i]) for i in range(4) for n in WEIGHTS])
```

```python
import functools

import jax
import jax.numpy as jnp
from jax import lax
from jax.experimental import pallas as pl
from jax.experimental.pallas import tpu as pltpu

F32, BF16 = jnp.float32, jnp.bfloat16
EPS = 1e-6
CHUNK = 64
STEP_CHUNKS = 2
STEP_ROWS = STEP_CHUNKS * CHUNK
HEADS, DK, DV = 4, 128, 256
GATE_RANK = 16
TAU = 16.0
LR, B1, B2, ADAM_EPS, WD, STEP = 0.001, 0.9, 0.999, 1e-08, 0.01, 10
N_DEV = 8
MESH = pl.DeviceIdType.MESH
VMEM_LIMIT = 56 * 1024 * 1024
ANY = pl.BlockSpec(memory_space=pl.ANY)

C_GA, C_GB = 0, 2048
C_AX, C_AB, C_AC, C_Q, C_K, C_V, C_OG, C_ALR = 0, 1024, 2048, 3072, 3584, 4096, 5120, 6144
REST = 6272
R_GA, R_END = 6160, 10256


def _params(sem):
    return pltpu.CompilerParams(dimension_semantics=sem, vmem_limit_bytes=VMEM_LIMIT)


def _pick(n, cands):
    for c in cands:
        if n % c == 0:
            return c
    return n


def _tiles(r, c):
    for tr in (128, 64):
        if r % tr == 0:
            return r // tr, (tr, c), lambda i: (i, 0)
    tc = _pick(c, (256, 128))
    return c // tc, (r, tc), lambda i: (0, i)


def _mm(a, b, mode, out_dtype, name, after=(), add=None, b3=False, out3=False, tm=None, tk=None, n_outer=False):
    bshape = (b.shape[1], N_DEV * b.shape[2]) if b3 else b.shape
    if mode == "nn":
        (m, k), (k2, n) = a.shape, bshape
    elif mode == "nt":
        (m, k), (n, k2) = a.shape, bshape
    else:
        (k, m), (k2, n) = a.shape, bshape
    assert k == k2 and a.dtype == BF16 and b.dtype == BF16, (name, a.shape, b.shape, a.dtype, b.dtype)
    tm = tm if tm and m % tm == 0 else _pick(m, (2048, 1024, 512, 256))
    tn = _pick(n, (1152, 1024, 1408, 896, 512, 256))
    tk = min(tk, k) if tk and k % min(tk, k) == 0 else _pick(k, (2048, 1408, 1152, 1024, 896, 512, 256))
    if out3 or (b3 and mode == "nn"):
        tn = n // N_DEV
    if b3 and mode == "nt":
        tk = k // N_DEV
    nk = k // tk
    dims = {"nn": (((1,), (0,)), ((), ())), "nt": (((1,), (1,)), ((), ())), "tn": (((0,), (0,)), ((), ()))}[mode]
    n_extra = len(after) + (add is not None)

    def body(a_ref, b_ref, *rest):
        o_ref = rest[n_extra]
        prod = lax.dot_general(a_ref[...], b_ref[...], dims, preferred_element_type=F32)
        if nk == 1:
            o_ref[...] = (prod if add is None else prod + rest[0][...]).astype(o_ref.dtype)
            return
        acc_ref = rest[n_extra + 1]
        kk = pl.program_id(2)

        @pl.when(kk == 0)
        def _():
            acc_ref[...] = prod if add is None else prod + rest[0][...]

        @pl.when((kk > 0) & (kk < nk - 1))
        def _():
            acc_ref[...] += prod

        @pl.when(kk == nk - 1)
        def _():
            o_ref[...] = (acc_ref[...] + prod).astype(o_ref.dtype)

    def spec(block, at):
        return pl.BlockSpec(block, (lambda g0, g1, kk: at(g1, g0, kk)) if n_outer else at)

    a_spec = spec((tk, tm), lambda i, j, kk: (kk, i)) if mode == "tn" else spec((tm, tk), lambda i, j, kk: (i, kk))
    if b3:
        b_spec = spec((None, tn, tk), lambda i, j, kk: (kk, j, 0)) if mode == "nt" else spec((None, tk, tn), lambda i, j, kk: (j, kk, 0))
    else:
        b_spec = spec((tn, tk), lambda i, j, kk: (j, kk)) if mode == "nt" else spec((tk, tn), lambda i, j, kk: (kk, j))
    tile = spec((tm, tn), lambda i, j, kk: (i, j))
    out_spec = spec((None, tm, tn), lambda i, j, kk: (j, i, 0)) if out3 else tile
    return pl.pallas_call(
        body, name=name, grid=(n // tn, m // tm, nk) if n_outer else (m // tm, n // tn, nk),
        in_specs=[a_spec, b_spec] + ([tile] if add is not None else []) + [ANY] * len(after), out_specs=out_spec,
        out_shape=jax.ShapeDtypeStruct((N_DEV, m, tn) if out3 else (m, n), out_dtype),
        scratch_shapes=[pltpu.VMEM((tm, tn), F32)] if nk > 1 else [],
        compiler_params=_params(("parallel", "parallel", "arbitrary")),
    )(a, b, *([add] if add is not None else []), *after)


LONG_K = dict(tm=512, tk=8192, n_outer=True)


def _rows(body, t, tr, ins, outs, name, after=()):
    in_specs = []
    for arr, sp in ins:
        if sp[0] == "t":
            in_specs.append(pl.BlockSpec((tr, sp[1]), lambda i, cb=sp[2]: (i, cb)))
        else:
            in_specs.append(pl.BlockSpec(arr.shape, lambda i, nd=arr.ndim: (0,) * nd))
    out_specs, out_shape = [], []
    for shape, dt, kind in outs:
        out_specs.append(pl.BlockSpec((tr, shape[1]), lambda i: (i, 0)) if kind == "t" else pl.BlockSpec(shape, lambda i: (0, 0)))
        out_shape.append(jax.ShapeDtypeStruct(shape, dt))
    return pl.pallas_call(
        body, name=name, grid=(t // tr,), in_specs=in_specs + [ANY] * len(after), out_specs=out_specs, out_shape=out_shape,
        compiler_params=_params(("arbitrary",)),
    )(*[arr for arr, _ in ins], *after)


def _rinv(v):
    return lax.rsqrt(jnp.mean(v * v, axis=-1, keepdims=True) + EPS)


def _sig(v):
    return 1.0 / (1.0 + jnp.exp(-v))


def _acc(ref, val):
    @pl.when(pl.program_id(0) == 0)
    def _():
        ref[...] = jnp.zeros_like(ref)

    ref[...] += jnp.sum(val, axis=0, keepdims=True)


def _rms_fwd(x, g, name):
    t, d = x.shape

    def body(x_ref, g_ref, h_ref):
        xv = x_ref[...]
        h_ref[...] = (xv * _rinv(xv) * g_ref[...]).astype(BF16)

    return _rows(body, t, 256, [(x, ("t", d, 0)), (g, ("b",))], [((t, d), BF16, "t")], name)[0]


def _post_pre(x, m, g_post, g_pre, name, after=()):
    t, d = x.shape

    def body(x_ref, m_ref, gp_ref, gn_ref, *rest):
        xo_ref, h_ref = rest[len(after):]
        mv = m_ref[...]
        xn = x_ref[...] + mv * _rinv(mv) * gp_ref[...]
        xo_ref[...] = xn
        h_ref[...] = (xn * _rinv(xn) * gn_ref[...]).astype(BF16)

    return _rows(body, t, 256, [(x, ("t", d, 0)), (m, ("t", d, 0)), (g_post, ("b",)), (g_pre, ("b",))],
                 [((t, d), F32, "t"), ((t, d), BF16, "t")], name, after)


def _mix_fwd(proj, ya, yb, name):
    t, d = ya.shape

    def body(ga_ref, gb_ref, ya_ref, yb_ref, o_ref):
        o_ref[...] = (_sig(ga_ref[...].astype(F32)) * ya_ref[...].astype(F32)
                      + _sig(gb_ref[...].astype(F32)) * yb_ref[...].astype(F32)).astype(BF16)

    return _rows(body, t, 256, [(proj, ("t", d, C_GA // d)), (proj, ("t", d, C_GB // d)), (ya, ("t", d, 0)), (yb, ("t", d, 0))],
                 [((t, d), BF16, "t")], name)[0]


def _mix_bwd(dmix, proj, ya, yb, name):
    t, d = ya.shape

    def body(dm_ref, ga_ref, gb_ref, ya_ref, yb_ref, dg_ref, dya_ref, dyb_ref):
        dm = dm_ref[...]
        sa, sb = _sig(ga_ref[...].astype(F32)), _sig(gb_ref[...].astype(F32))
        dg_ref[:, :d] = (dm * ya_ref[...].astype(F32) * sa * (1.0 - sa)).astype(BF16)
        dg_ref[:, d:] = (dm * yb_ref[...].astype(F32) * sb * (1.0 - sb)).astype(BF16)
        dya_ref[...] = (dm * sa).astype(BF16)
        dyb_ref[...] = (dm * sb).astype(BF16)

    return _rows(body, t, 256,
                 [(dmix, ("t", d, 0)), (proj, ("t", d, C_GA // d)), (proj, ("t", d, C_GB // d)), (ya, ("t", d, 0)), (yb, ("t", d, 0))],
                 [((t, 2 * d), BF16, "t"), ((t, d), BF16, "t"), ((t, d), BF16, "t")], name)


def _swiglu_call(body, ins, n_out, name):
    t, f = ins[0].shape
    tc = _pick(f, (1408, 512))
    tile = pl.BlockSpec((512, tc), lambda i, j: (i, j))
    return pl.pallas_call(
        body, name=name, grid=(t // 512, f // tc), in_specs=[tile] * len(ins), out_specs=[tile] * n_out,
        out_shape=[jax.ShapeDtypeStruct((t, f), BF16)] * n_out, compiler_params=_params(("parallel", "parallel")),
    )(*ins)


def _swiglu_fwd(fg, fu, name):
    def body(g_ref, u_ref, s_ref):
        gv = g_ref[...].astype(F32)
        s_ref[...] = (gv * _sig(gv) * u_ref[...].astype(F32)).astype(BF16)

    return _swiglu_call(body, [fg, fu], 1, name)[0]


def _swiglu_bwd(ds, fg, fu, name):
    def body(ds_ref, g_ref, u_ref, dg_ref, du_ref):
        dsv, gv, uv = ds_ref[...].astype(F32), g_ref[...].astype(F32), u_ref[...].astype(F32)
        sg = _sig(gv)
        dg_ref[...] = (dsv * uv * sg * (1.0 + gv * (1.0 - sg))).astype(BF16)
        du_ref[...] = (dsv * gv * sg).astype(BF16)

    return _swiglu_call(body, [ds, fg, fu], 2, name)


def _ple_final(x2, pg, pp, tgt, g_post, name):
    t, d = x2.shape

    def body(x_ref, pg_ref, pp_ref, t_ref, g_ref, loss_ref, d3_ref, dpg_ref, dpp_ref, dg_ref):
        sg, ppv, g = _sig(pg_ref[...]), pp_ref[...], g_ref[...]
        e = sg * ppv
        r = _rinv(e)
        eh = e * r
        diff = x_ref[...] + eh * g - t_ref[...]
        loss_ref[...] = 0.5 * jnp.mean(diff * diff, axis=-1, keepdims=True)
        d3 = diff * (1.0 / d)
        d3_ref[...] = d3
        gd = d3 * g
        de = r * (gd - eh * jnp.mean(gd * eh, axis=-1, keepdims=True))
        dpg_ref[...] = (de * ppv * sg * (1.0 - sg)).astype(BF16)
        dpp_ref[...] = (de * sg).astype(BF16)
        _acc(dg_ref, d3 * eh)

    return _rows(body, t, 256, [(x2, ("t", d, 0)), (pg, ("t", d, 0)), (pp, ("t", d, 0)), (tgt, ("t", d, 0)), (g_post, ("b",))],
                 [((t, 1), F32, "t"), ((t, d), F32, "t"), ((t, d), BF16, "t"), ((t, d), BF16, "t"), ((1, d), F32, "a")], name)


def _norm_bwd(dn, dh, x, g_pre, fm, g_post, name):
    t, d = x.shape
    two = fm is not None

    def body(*refs):
        if two:
            dn_ref, dh_ref, x_ref, gp_ref, f_ref, gq_ref, dx_ref, df_ref, dgp_ref, dgq_ref = refs
        else:
            dn_ref, dh_ref, x_ref, gp_ref, dx_ref, dgp_ref = refs
        xv, dhv = x_ref[...], dh_ref[...]
        r = _rinv(xv)
        xh = xv * r
        gd = dhv * gp_ref[...]
        dx = dn_ref[...] + r * (gd - xh * jnp.mean(gd * xh, axis=-1, keepdims=True))
        dx_ref[...] = dx
        _acc(dgp_ref, dhv * xh)
        if two:
            fv = f_ref[...]
            rf = _rinv(fv)
            fh = fv * rf
            gd2 = dx * gq_ref[...]
            df_ref[...] = (rf * (gd2 - fh * jnp.mean(gd2 * fh, axis=-1, keepdims=True))).astype(BF16)
            _acc(dgq_ref, dx * fh)

    ins = [(dn, ("t", d, 0)), (dh, ("t", d, 0)), (x, ("t", d, 0)), (g_pre, ("b",))]
    outs = [((t, d), F32, "t")]
    if two:
        ins += [(fm, ("t", d, 0)), (g_post, ("b",))]
        outs += [((t, d), BF16, "t"), ((1, d), F32, "a"), ((1, d), F32, "a")]
    else:
        outs += [((1, d), F32, "a")]
    return _rows(body, t, 256, ins, outs, name)


CONV_TC = 256


def _shift_down(v, s):
    rows = lax.broadcasted_iota(jnp.int32, v.shape, 0)
    return jnp.where(rows >= s, pltpu.roll(v, s, 0), 0.0)


def _shift_up(v, s):
    n = v.shape[0]
    rows = lax.broadcasted_iota(jnp.int32, v.shape, 0)
    return jnp.where(rows < n - s, pltpu.roll(v, n - s, 0), 0.0)


def _conv_specs(t):
    nb = 1024 // CONV_TC
    seg = lambda c0: pl.BlockSpec((t, CONV_TC), lambda j, cb=c0 // CONV_TC: (0, cb + j))
    own = pl.BlockSpec((t, CONV_TC), lambda j: (0, j))
    wspec = pl.BlockSpec((3, CONV_TC), lambda j: (0, j))
    return nb, seg, own, wspec


def _conv_fwd(proj, conv_w, name, after=()):
    t = proj.shape[0]
    nb, seg, own, wspec = _conv_specs(t)

    def body(ax_ref, ab_ref, ac_ref, w_ref, *rest):
        za_ref = rest[len(after)]
        u = ac_ref[...].astype(F32) * ax_ref[...].astype(F32)
        w = w_ref[...]
        yc = w[0:1] * _shift_down(u, 2) + w[1:2] * _shift_down(u, 1) + w[2:3] * u
        za_ref[...] = (ab_ref[...].astype(F32) * yc).astype(BF16)

    return pl.pallas_call(
        body, name=name, grid=(nb,), in_specs=[seg(C_AX), seg(C_AB), seg(C_AC), wspec] + [ANY] * len(after), out_specs=own,
        out_shape=jax.ShapeDtypeStruct((t, 1024), BF16), compiler_params=_params(("parallel",)),
    )(proj, proj, proj, conv_w, *after)


def _conv_bwd(dza, proj, conv_w, name):
    t = proj.shape[0]
    nb, seg, own, wspec = _conv_specs(t)

    def body(dz_ref, ax_ref, ab_ref, ac_ref, w_ref, dax_ref, dab_ref, dac_ref, dw_ref):
        ax, ab, ac, dz = ax_ref[...].astype(F32), ab_ref[...].astype(F32), ac_ref[...].astype(F32), dz_ref[...].astype(F32)
        w = w_ref[...]
        u = ac * ax
        u1, u2 = _shift_down(u, 1), _shift_down(u, 2)
        yc = w[0:1] * u2 + w[1:2] * u1 + w[2:3] * u
        dab_ref[...] = (dz * yc).astype(BF16)
        dyc = dz * ab
        du = w[2:3] * dyc + w[1:2] * _shift_up(dyc, 1) + w[0:1] * _shift_up(dyc, 2)
        dax_ref[...] = (du * ac).astype(BF16)
        dac_ref[...] = (du * ax).astype(BF16)
        dw_ref[0:1, :] = jnp.sum(dyc * u2, axis=0, keepdims=True)
        dw_ref[1:2, :] = jnp.sum(dyc * u1, axis=0, keepdims=True)
        dw_ref[2:3, :] = jnp.sum(dyc * u, axis=0, keepdims=True)

    act = jax.ShapeDtypeStruct((t, 1024), BF16)
    return pl.pallas_call(
        body, name=name, grid=(nb,), in_specs=[own, seg(C_AX), seg(C_AB), seg(C_AC), wspec], out_specs=[own, own, own, wspec],
        out_shape=[act, act, act, jax.ShapeDtypeStruct((3, 1024), F32)], compiler_params=_params(("parallel",)),
    )(dza, proj, proj, proj, conv_w)


def _dot(a, b, dims, precision=None):
    return lax.dot_general(a, b, (dims, ((), ())), precision=precision, preferred_element_type=F32)


_CONTRACT = {"nn": ((1,), (0,)), "nt": ((1,), (1,)), "tn": ((0,), (0,))}


def _bdot_raw(a, b, mode):
    return _dot(a.astype(BF16), b.astype(BF16), _CONTRACT[mode])


@functools.partial(jax.custom_vjp, nondiff_argnums=(2,))
def _bdot(a, b, mode):
    return _bdot_raw(a, b, mode)


def _bdot_fwd(a, b, mode):
    return _bdot_raw(a, b, mode), (a, b)


def _bdot_bwd(mode, res, ct):
    a, b = res
    if mode == "nn":
        return _bdot_raw(ct, b, "nt"), _bdot_raw(a, ct, "tn")
    if mode == "nt":
        return _bdot_raw(ct, b, "nn"), _bdot_raw(ct, a, "tn")
    return _bdot_raw(b, ct, "nt"), _bdot_raw(a, ct, "nn")


_bdot.defvjp(_bdot_fwd, _bdot_bwd)


@functools.partial(jax.custom_vjp, nondiff_argnums=(2,))
def _sum_dot(ones, x, mode):
    head = x.astype(BF16)
    tail = x - head.astype(F32)
    if mode == "nn":
        return _bdot_raw(ones, head, "nn") + _bdot_raw(ones, tail, "nn")
    return _bdot_raw(head, ones, "tn") + _bdot_raw(tail, ones, "tn")


def _sum_dot_fwd(ones, x, mode):
    return _sum_dot(ones, x, mode), ones


def _sum_dot_bwd(mode, ones, ct):
    return jnp.zeros_like(ones), (_bdot_raw(ones, ct, "tn") if mode == "nn" else _bdot_raw(ones, ct, "nt"))


_sum_dot.defvjp(_sum_dot_fwd, _sum_dot_bwd)


def _gla_chunk(q, k, v, og, alr, s_in, wa, ba, gain):
    c = q.shape[0]
    z =_bdot(alr, wa, "nn") + ba
    la = (jnp.minimum(z, 0.0) - jnp.log(1.0 + jnp.exp(-jnp.abs(z)))) * (1.0 / TAU)
    row = lax.broadcasted_iota(jnp.int32, (c, c), 0)
    col = lax.broadcasted_iota(jnp.int32, (c, c), 1)
    lower = row >= col
    b = _sum_dot(lower.astype(F32), la, "nn")
    trow = lax.broadcasted_iota(jnp.int32, la.shape, 0)
    mid = jnp.sum(jnp.where(trow <= c // 2, la, 0.0), axis=0, keepdims=True)
    blast = jnp.sum(la, axis=0, keepdims=True)
    qs = q * (DK ** -0.5)
    e_up, e_dn = jnp.exp(b - mid), jnp.exp(mid - b)
    a_fwd = _bdot(qs * e_up, k * e_dn, "nt")
    a_rev = _bdot(qs * e_dn, k * e_up, "nt")
    att = jnp.where(lower, a_fwd, a_rev)
    o = _bdot(att, v, "nn") + _bdot(qs * jnp.exp(b), s_in, "nn")
    upd = _bdot(k * jnp.exp(blast - b), v, "tn")
    blast_col = _sum_dot(jnp.ones((c, DV), F32), la, "tn")
    s_out = jnp.exp(blast_col) * s_in + upd
    on = o * _rinv(o) * gain
    return on * og * _sig(og), s_out


def _gla_specs(t, rev):
    n = t // STEP_ROWS
    ch = (lambda i: n - 1 - i) if rev else (lambda i: i)
    col = lambda w, c0: pl.BlockSpec((STEP_ROWS, HEADS * w), lambda i, cb=c0 // (HEADS * w): (ch(i), cb))
    whole = lambda shape: pl.BlockSpec(shape, lambda i, nd=len(shape): (0,) * nd)
    specs = dict(
        q=col(DK, C_Q), k=col(DK, C_K), v=col(DV, C_V), og=col(DV, C_OG),
        alr=pl.BlockSpec((STEP_ROWS, 128), lambda i: (ch(i), C_ALR // 128)),
        wa=whole((128, HEADS * DK)), ba=whole((1, HEADS * DK)), gain=whole((1, DV)),
        state=pl.BlockSpec((STEP_CHUNKS, HEADS, DK, DV), lambda i: (ch(i), 0, 0, 0)),
        odk=pl.BlockSpec((STEP_ROWS, HEADS * DK), lambda i: (ch(i), 0)), odv=pl.BlockSpec((STEP_ROWS, HEADS * DV), lambda i: (ch(i), 0)),
        oalr=pl.BlockSpec((STEP_ROWS, 128), lambda i: (ch(i), 0)), whole=whole,
    )
    return n, specs


def _head_cols(h):
    return slice(h * DK, (h + 1) * DK), slice(h * DV, (h + 1) * DV)


def _gla_fwd(proj, wa, ba, gain, name):
    t = proj.shape[0]
    n, sp = _gla_specs(t, False)

    def body(q_ref, k_ref, v_ref, og_ref, alr_ref, wa_ref, ba_ref, g_ref, zb_ref, st_ref, s_scr):
        @pl.when(pl.program_id(0) == 0)
        def _():
            s_scr[...] = jnp.zeros_like(s_scr)

        state = [s_scr[h] for h in range(HEADS)]
        for c in range(STEP_CHUNKS):
            rows = slice(c * CHUNK, (c + 1) * CHUNK)
            alr = alr_ref[rows, :].astype(F32)
            for h in range(HEADS):
                kc, vc = _head_cols(h)
                st_ref[c, h] = state[h]
                zb, state[h] = _gla_chunk(q_ref[rows, kc].astype(F32), k_ref[rows, kc].astype(F32), v_ref[rows, vc].astype(F32),
                                          og_ref[rows, vc].astype(F32), alr, state[h], wa_ref[:, kc].astype(F32), ba_ref[:, kc], g_ref[...])
                zb_ref[rows, vc] = zb.astype(BF16)
        for h in range(HEADS):
            s_scr[h] = state[h]

    return pl.pallas_call(
        body, name=name, grid=(n,),
        in_specs=[sp["q"], sp["k"], sp["v"], sp["og"], sp["alr"], sp["wa"], sp["ba"], sp["gain"]],
        out_specs=[sp["odv"], sp["state"]],
        out_shape=[jax.ShapeDtypeStruct((t, HEADS * DV), BF16), jax.ShapeDtypeStruct((t // CHUNK, HEADS, DK, DV), F32)],
        scratch_shapes=[pltpu.VMEM((HEADS, DK, DV), F32)],
        compiler_params=_params(("arbitrary",)),
    )(proj, proj, proj, proj, proj, wa, ba, gain)


def _gla_bwd(dzb, proj, states, wa, ba, gain, name):
    t = proj.shape[0]
    n, sp = _gla_specs(t, True)

    def body(dz_ref, q_ref, k_ref, v_ref, og_ref, alr_ref, st_ref, wa_ref, ba_ref, g_ref,
             dq_ref, dk_ref, dv_ref, dog_ref, dalr_ref, dwa_ref, dba_ref, dg_ref, ds_scr):
        @pl.when(pl.program_id(0) == 0)
        def _():
            ds_scr[...] = jnp.zeros_like(ds_scr)
            dwa_ref[...] = jnp.zeros_like(dwa_ref)
            dba_ref[...] = jnp.zeros_like(dba_ref)
            dg_ref[...] = jnp.zeros_like(dg_ref)

        dstate = [ds_scr[h] for h in range(HEADS)]
        dwa_sum, dba_sum, dgain_sum = [None] * HEADS, [None] * HEADS, None
        for c in reversed(range(STEP_CHUNKS)):
            rows = slice(c * CHUNK, (c + 1) * CHUNK)
            alr = alr_ref[rows, :].astype(F32)
            dalr_sum = None
            for h in range(HEADS):
                kc, vc = _head_cols(h)
                args = (q_ref[rows, kc].astype(F32), k_ref[rows, kc].astype(F32), v_ref[rows, vc].astype(F32), og_ref[rows, vc].astype(F32),
                        alr, st_ref[c, h], wa_ref[:, kc].astype(F32), ba_ref[:, kc], g_ref[...])
                _, vjp = jax.vjp(_gla_chunk, *args)
                dq, dk, dv, dog, dalr, dstate[h], dwa, dba, dgain = vjp((dz_ref[rows, vc].astype(F32), dstate[h]))
                dq_ref[rows, kc] = dq.astype(BF16)
                dk_ref[rows, kc] = dk.astype(BF16)
                dv_ref[rows, vc] = dv.astype(BF16)
                dog_ref[rows, vc] = dog.astype(BF16)
                dwa_sum[h] = dwa if dwa_sum[h] is None else dwa_sum[h] + dwa
                dba_sum[h] = dba if dba_sum[h] is None else dba_sum[h] + dba
                dalr_sum = dalr if dalr_sum is None else dalr_sum + dalr
                dgain_sum = dgain if dgain_sum is None else dgain_sum + dgain
            dalr_ref[rows, :] = dalr_sum
        for h in range(HEADS):
            ds_scr[h] = dstate[h]
            dwa_ref[h] += dwa_sum[h]
            dba_ref[h] += dba_sum[h]
        dg_ref[...] += dgain_sum

    whole = sp["whole"]
    return pl.pallas_call(
        body, name=name, grid=(n,),
        in_specs=[sp["odv"], sp["q"], sp["k"], sp["v"], sp["og"], sp["alr"], sp["state"], sp["wa"], sp["ba"], sp["gain"]],
        out_specs=[sp["odk"], sp["odk"], sp["odv"], sp["odv"], sp["oalr"], whole((HEADS, 128, DK)), whole((HEADS, 1, DK)), whole((1, DV))],
        out_shape=[jax.ShapeDtypeStruct((t, HEADS * DK), BF16), jax.ShapeDtypeStruct((t, HEADS * DK), BF16),
                   jax.ShapeDtypeStruct((t, HEADS * DV), BF16), jax.ShapeDtypeStruct((t, HEADS * DV), BF16),
                   jax.ShapeDtypeStruct((t, 128), F32), jax.ShapeDtypeStruct((HEADS, 128, DK), F32),
                   jax.ShapeDtypeStruct((HEADS, 1, DK), F32), jax.ShapeDtypeStruct((1, DV), F32)],
        scratch_shapes=[pltpu.VMEM((HEADS, DK, DV), F32)],
        compiler_params=_params(("arbitrary",)),
    )(dzb, proj, proj, proj, proj, proj, states, wa, ba, gain)


def _local_step(x, p, tgt, gather_start, gather_pass_on, gather_finish, scatter_start, scatter_next, small):
    b_alpha, gain = small["b_alpha_up"], small["gla_head_gain"]
    gather_start(0, ())
    w = dict(gather_finish(0, ()))
    conv_w, w_alpha = w["conv_w"], w["w_alpha_up"]
    wa_p = jnp.zeros((128, HEADS * DK), BF16).at[:GATE_RANK].set(w_alpha.astype(BF16))

    t2 = gather_start(2, (w["in_rest_t"], gather_start(1, ())))
    h1 = _rms_fwd(x, small["g_pre_mix"], "rms_pre_mix")
    proj = _mm(h1, w["in_rest_t"], "nt", BF16, "mm_proj", after=(t2,))
    proj_gates = _mm(h1, w["in_gates_t"], "nt", BF16, "mm_proj_gates", after=(t2,))
    za = _conv_fwd(proj, conv_w, "conv_fwd", after=(gather_pass_on(1, (proj,)),))
    zb, states = _gla_fwd(proj, wa_p, b_alpha, gain, "gla_fwd")
    t3 = gather_start(3, (zb, za))
    w.update(gather_finish(1, (t3,)))
    ya = _mm(za, w["a_out"], "nn", BF16, "mm_ya")
    yb = _mm(zb, w["b_out"], "nn", BF16, "mm_yb")
    mix = _mix_fwd(proj_gates, ya, yb, "mix_fwd")
    m2 = _mm(mix, w["mix"], "nn", F32, "mm_mix")
    t4 = gather_start(4, (m2,))
    x1, h2 = _post_pre(x, m2, small["g_post_mix"], small["g_pre_ffn"], "norm_mix_ffn", after=(gather_pass_on(2, (m2,)),))
    w.update(gather_finish(2, (h2, t4)))
    fu = _mm(h2, w["up_t"], "nt", BF16, "mm_up")
    fg = _mm(h2, w["gate_t"], "nt", BF16, "mm_gate", after=(gather_pass_on(3, (fu,)),))
    s = _swiglu_fwd(fg, fu, "swiglu_fwd")
    w.update(gather_finish(3, (s,)))
    f = _mm(s, w["down"], "nn", F32, "mm_down", after=(gather_pass_on(4, (s,)),), **LONG_K)
    x2, h3 = _post_pre(x1, f, small["g_post_ffn"], small["g_pre_ple"], "norm_ffn_ple")
    w.update(gather_finish(4, (h3,)))
    pg = _mm(h3, w["pg"], "nn", F32, "mm_pg")
    p_bf = p.astype(BF16)
    pp = _mm(p_bf, w["pp"], "nn", F32, "mm_pp", b3=True, tm=2048)
    loss_rows, d3, dpg, dpp, dg_post_ple = _ple_final(x2, pg, pp, tgt, small["g_post_ple"], "ple_final")

    gw = {}
    gw["pp"] = _mm(p_bf, dpp, "tn", BF16, "mm_dw_pp", out3=True)
    gw["pg"] = _mm(h3, dpg, "tn", BF16, "mm_dw_pg")
    dh3 = _mm(dpg, w["pg"], "nt", F32, "mm_dh3", after=(scatter_start(["w_ple_proj", "w_ple_gate"], gw),))
    d2, df, dg_pre_ple, dg_post_ffn = _norm_bwd(d3, dh3, x2, small["g_pre_ple"], f, small["g_post_ffn"], "norm_bwd_ple_ffn")
    gw["down"] = _mm(s, df, "tn", BF16, "mm_dw_down", tm=1408)
    ds = _mm(df, w["down"], "nt", BF16, "mm_ds", after=(scatter_start(["w_ff_down"], gw),))
    dfg, dfu = _swiglu_bwd(ds, fg, fu, "swiglu_bwd")
    gw["gate_t"] = _mm(dfg, h2, "tn", BF16, "mm_dw_gate", tm=1408)
    gw["up_t"] = _mm(dfu, h2, "tn", BF16, "mm_dw_up", after=(gw["gate_t"],), tm=1408)
    dh2 = _mm(dfg, w["gate_t"], "nn", F32, "mm_dh2_gate", after=(scatter_start(["w_ff_gate", "w_ff_up"], gw),), **LONG_K)
    dh2 = _mm(dfu, w["up_t"], "nn", F32, "mm_dh2_up", add=dh2, **LONG_K)
    d1, dm2, dg_pre_ffn, dg_post_mix = _norm_bwd(d2, dh2, x1, small["g_pre_ffn"], m2, small["g_post_mix"], "norm_bwd_ffn_mix")
    dmix = _mm(dm2, w["mix"], "nt", F32, "mm_dmix")
    gw["mix"] = _mm(mix, dm2, "tn", BF16, "mm_dw_mix")
    dgab, dya, dyb = _mix_bwd(dmix, proj_gates, ya, yb, "mix_bwd")
    gw["in_gates_t"] = _mm(dgab, h1, "tn", BF16, "mm_dw_in_gates")
    dza = _mm(dya, w["a_out"], "nt", BF16, "mm_dza", after=(scatter_start(["w_mix_out"], gw), gw["in_gates_t"]))
    gw["a_out"] = _mm(za, dya, "tn", BF16, "mm_dw_a_out")
    gw["b_out"] = _mm(zb, dyb, "tn", BF16, "mm_dw_b_out", after=(gw["a_out"],))
    dzb = _mm(dyb, w["b_out"], "nt", BF16, "mm_dzb", after=(scatter_start(["w_a_out", "w_b_out"], gw),))
    dax, dab, dac, dconv = _conv_bwd(dza, proj, conv_w, "conv_bwd")
    dq, dk, dv, dog, dalr, dwa, dba, dgain = _gla_bwd(dzb, proj, states, wa_p, b_alpha, gain, "gla_bwd")
    drest = jnp.concatenate([dax, dab, dac, dq, dk, dv, dog, dalr.astype(BF16)], axis=1)
    gw["in_rest_t"] = _mm(drest, h1, "tn", BF16, "mm_dw_in_rest", tm=896)
    dh1 = _mm(dgab, w["in_gates_t"], "nn", F32, "mm_dh1_gates", after=(scatter_start(["w_in"], gw),), tm=1024, tk=4096, n_outer=True)
    dh1 = _mm(drest, w["in_rest_t"], "nn", F32, "mm_dh1_rest", add=dh1, after=(scatter_next((dh1,)),), **LONG_K)
    grad_x, dg_pre_mix = _norm_bwd(d1, dh1, x, small["g_pre_mix"], None, None, "norm_bwd_mix")

    gs = dict(
        conv_w=dconv,
        w_alpha_up=jnp.transpose(dwa[:, :GATE_RANK, :], (1, 0, 2)).reshape(GATE_RANK, HEADS * DK),
        b_alpha_up=dba.reshape(1, HEADS * DK), gla_head_gain=dgain,
        g_pre_mix=dg_pre_mix, g_post_mix=dg_post_mix, g_pre_ffn=dg_pre_ffn, g_post_ffn=dg_post_ffn,
        g_pre_ple=dg_pre_ple, g_post_ple=dg_post_ple,
    )
    return loss_rows, grad_x, gs


def _place():
    x, y, c = lax.axis_index("x"), lax.axis_index("y"), lax.axis_index("c")
    return x, y, c, [(1 - x, y), (x, 1 - y), (1 - x, 1 - y)]


def _pair_add(mine8, got4, name):
    _, r, cols = mine8.shape
    steps, blk, at = _tiles(r, cols)
    core = lax.axis_index("c").astype(jnp.int32).reshape(1)

    def body(c_ref, a_ref, b_ref, o_ref):
        o_ref[...] = (a_ref[...].astype(F32) + b_ref[...].astype(F32)).astype(BF16)

    return pl.pallas_call(
        body, name=name,
        grid_spec=pltpu.PrefetchScalarGridSpec(
            num_scalar_prefetch=1, grid=(4, steps),
            in_specs=[pl.BlockSpec((None,) + blk, lambda ch, i, c_ref: (2 * ch + c_ref[0],) + at(i)),
                      pl.BlockSpec((None,) + blk, lambda ch, i, c_ref: (ch,) + at(i))],
            out_specs=pl.BlockSpec((None,) + blk, lambda ch, i, c_ref: (ch,) + at(i))),
        out_shape=jax.ShapeDtypeStruct((4, r, cols), BF16),
        compiler_params=_params(("parallel", "parallel")),
    )(core, mine8, got4)


HBM = pl.BlockSpec(memory_space=pltpu.HBM)
SEM = pl.BlockSpec(memory_space=pltpu.SEMAPHORE)
EFFECT = pltpu.SideEffectType.DATAFLOW_SIDE_EFFECTING


def _in_hbm(a):
    return pltpu.with_memory_space_constraint(a, pltpu.HBM)


def _remote_copies(plan, srcs, lands, send_sems, recv_sems):
    return [pltpu.make_async_remote_copy(src_ref=s, dst_ref=d, send_sem=send_sems.at[i], recv_sem=recv_sems.at[i], device_id=peer,
                                         device_id_type=MESH) for i, (s, d, peer) in enumerate(plan(srcs, lands))]


def _copies_start(plan, n_copies, srcs, land_shapes, name, after=()):
    ns, nl = len(srcs), len(land_shapes)

    def body(*refs):
        send_sems, recv_sems = refs[ns + nl + len(after):ns + nl + len(after) + 2]
        for cp in _remote_copies(plan, refs[:ns], refs[ns:ns + nl], send_sems, recv_sems):
            cp.start()
        refs[-1][...] = jnp.zeros((8, 128), F32)

    sems = pltpu.SemaphoreType.DMA((n_copies,))
    return pl.pallas_call(
        body, name=name,
        out_shape=(sems, sems, *[pltpu.HBM(s.shape, s.dtype) for s in srcs], *[pltpu.HBM(s.shape, s.dtype) for s in land_shapes],
                   jax.ShapeDtypeStruct((8, 128), F32)),
        in_specs=[HBM] * (ns + nl) + [ANY] * len(after),
        out_specs=(SEM, SEM, *[HBM] * (ns + nl), pl.BlockSpec(memory_space=pltpu.VMEM)),
        input_output_aliases={i: 2 + i for i in range(ns + nl)},
        compiler_params=pltpu.CompilerParams(has_side_effects=EFFECT),
    )(*[_in_hbm(s) for s in srcs], *[_in_hbm(lax.empty(s.shape, s.dtype)) for s in land_shapes], *after)


def _copies_wait(plan, state, ns, name, after=()):
    send_sems, recv_sems, *arrs = state[:-1]
    n = len(arrs)

    def body(*refs):
        cps = _remote_copies(plan, refs[:ns], refs[ns:n], refs[n], refs[n + 1])
        for cp in cps:
            cp.wait_send()
        for cp in cps:
            cp.wait_recv()

    out = pl.pallas_call(
        body, name=name, out_shape=tuple(pltpu.HBM(a.shape, a.dtype) for a in arrs),
        in_specs=[HBM] * n + [SEM, SEM] + [ANY] * len(after), out_specs=tuple([HBM] * n),
        input_output_aliases={i: i for i in range(n)},
        compiler_params=pltpu.CompilerParams(has_side_effects=EFFECT),
    )(*arrs, send_sems, recv_sems, *after)
    return list(out[:ns]), list(out[ns:])


def _copies_relay(plan, state, ns, next_plan, n_next, name, after=()):
    send_sems, recv_sems, *arrs = state[:-1]
    n = len(arrs)

    def body(*refs):
        cps = _remote_copies(plan, refs[:ns], refs[ns:n], refs[n], refs[n + 1])
        for cp in cps:
            cp.wait_send()
        for cp in cps:
            cp.wait_recv()
        outs = refs[n + 2 + len(after):]
        for cp in _remote_copies(next_plan, refs[:ns], refs[ns:n], outs[0], outs[1]):
            cp.start()
        outs[-1][...] = jnp.zeros((8, 128), F32)

    sems = pltpu.SemaphoreType.DMA((n_next,))
    return pl.pallas_call(
        body, name=name,
        out_shape=(sems, sems, *[pltpu.HBM(a.shape, a.dtype) for a in arrs], jax.ShapeDtypeStruct((8, 128), F32)),
        in_specs=[HBM] * n + [SEM, SEM] + [ANY] * len(after),
        out_specs=(SEM, SEM, *[HBM] * n, pl.BlockSpec(memory_space=pltpu.VMEM)),
        input_output_aliases={i: 2 + i for i in range(n)},
        compiler_params=pltpu.CompilerParams(has_side_effects=EFFECT),
    )(*arrs, send_sems, recv_sems, *after)


def _pass_on_plan(srcs, lands):
    x, y, c, chips = _place()
    return [(l.at[4 * px + 2 * py + c], l.at[4 * px + 2 * py + c], (x, y, 1 - c)) for l in lands for px, py in chips]


def _gather_plan(srcs, lands):
    x, y, c, chips = _place()
    peers = [(x, y, 1 - c)] + [(*chip, c) for chip in chips]
    return [(s, l.at[4 * x + 2 * y + c], peer) for s, l in zip(srcs, lands) for peer in peers]


def _gather_plan_near(srcs, lands):
    x, y, c, _ = _place()
    peers = [(x, y, 1 - c), (1 - x, y, c), (x, 1 - y, c)]
    return [(s, l.at[4 * x + 2 * y + c], peer) for s, l in zip(srcs, lands) for peer in peers]


def _scatter_plan(srcs, lands):
    x, y, c, _ = _place()
    peers = [(1 - x if k & 4 else x, 1 - y if k & 2 else y, 1 - c if k & 1 else c) for k in range(1, N_DEV)]
    return [(s.at[4 * px + 2 * py + pc], l.at[4 * x + 2 * y + c], (px, py, pc)) for s, l in zip(srcs, lands) for px, py, pc in peers]


def _everyone_plan(srcs, lands):
    x, y, c, _ = _place()
    peers = [(1 - x if k & 4 else x, 1 - y if k & 2 else y, 1 - c if k & 1 else c) for k in range(1, N_DEV)]
    return [(s, l.at[4 * x + 2 * y + c], peer) for s, l in zip(srcs, lands) for peer in peers]


def _sum_parts(got, own, me, name):
    def body(me_ref, got_ref, own_ref, o_ref):
        acc = jnp.where(me_ref[0] == 0, own_ref[...], got_ref[0])
        for d in range(1, N_DEV):
            acc = acc + jnp.where(me_ref[0] == d, own_ref[...], got_ref[d])
        o_ref[...] = acc

    return pl.pallas_call(
        body, name=name,
        grid_spec=pltpu.PrefetchScalarGridSpec(
            num_scalar_prefetch=1, grid=(1,),
            in_specs=[pl.BlockSpec(got.shape, lambda i, me_ref: (0, 0, 0)), pl.BlockSpec(own.shape, lambda i, me_ref: (0, 0))],
            out_specs=pl.BlockSpec(own.shape, lambda i, me_ref: (0, 0))),
        out_shape=jax.ShapeDtypeStruct(own.shape, F32),
    )(me.astype(jnp.int32).reshape(1), got, own)


def _sibling_plan(srcs, lands):
    x, y, c, _ = _place()
    return [(s.at[2 * ch + 1 - c], l.at[ch], (x, y, 1 - c)) for s, l in zip(srcs, lands) for ch in range(4)]


def _chip_plan(srcs, lands):
    x, y, c, chips = _place()
    return [(s.at[2 * px + py], l.at[2 * x + y], (px, py, c)) for s, l in zip(srcs, lands) for px, py in chips]


def _put_own(shard, zone, me, name):
    r, c = shard.shape
    tr = r if r <= 256 else _pick(r, (256, 64))

    def body(me_ref, s_ref, z_ref, o_ref):
        o_ref[...] = s_ref[...]

    return pl.pallas_call(
        body, name=name,
        grid_spec=pltpu.PrefetchScalarGridSpec(
            num_scalar_prefetch=1, grid=(r // tr,),
            in_specs=[pl.BlockSpec((tr, c), lambda i, me_ref: (i, 0)), ANY],
            out_specs=pl.BlockSpec((None, tr, c), lambda i, me_ref: (me_ref[0], i, 0))),
        out_shape=jax.ShapeDtypeStruct(zone.shape, zone.dtype), input_output_aliases={2: 0},
        compiler_params=_params(("arbitrary",)),
    )(me.astype(jnp.int32).reshape(1), shard, zone)


def _gather_relay(lands, name):
    n = len(lands)

    def body(*refs):
        zones, outs = refs[:n], refs[n:2 * n]
        send_sems, recv_sems = refs[2 * n:]
        x, y, c, _ = _place()
        south = c == 0
        near_x, near_y, across = 4 * (1 - x) + 2 * y + c, 4 * x + 2 * (1 - y) + c, 4 * (1 - x) + 2 * (1 - y) + c
        passed = jnp.where(south, near_y, near_x)
        onward = (jnp.where(south, 1 - x, x), jnp.where(south, y, 1 - y), c)

        def copy(a, k, slot, to):
            return pltpu.make_async_remote_copy(src_ref=zones[a].at[slot], dst_ref=outs[a].at[slot], send_sem=send_sems.at[a, k],
                                                recv_sem=recv_sems.at[a, k], device_id=to, device_id_type=MESH)

        first = [copy(a, 0, passed, onward) for a in range(n)]
        first += [copy(a, 1 + j, slot, (x, y, 1 - c)) for j, slot in enumerate((near_x, near_y)) for a in range(n)]
        for cp in first:
            cp.start()
        last = []
        for a in range(n):
            copy(a, 0, across, onward).wait_recv()
            last.append(copy(a, 3, across, (x, y, 1 - c)))
            last[-1].start()
        for cp in first[n:] + last:
            cp.wait_recv()
        for cp in first + last:
            cp.wait_send()

    return pl.pallas_call(
        body, name=name, in_specs=[ANY] * n, out_specs=[ANY] * n,
        out_shape=[jax.ShapeDtypeStruct(l.shape, l.dtype) for l in lands],
        input_output_aliases={a: a for a in range(n)},
        scratch_shapes=[pltpu.SemaphoreType.DMA((n, 4)), pltpu.SemaphoreType.DMA((n, 4))],
    )(*lands)


def _adamw_parts(w, got, mine, me, m, v, name, after=()):
    r, c = w.shape
    n_parts = got.shape[0]
    steps, blk, at = _tiles(r, c)

    def body(me_ref, w_ref, got_ref, own_ref, m_ref, v_ref, *rest):
        go_ref, d_ref, mo_ref, vo_ref = rest[len(after):]
        own = own_ref[...].astype(F32)
        gv = jnp.where(me_ref[0] == 0, own, got_ref[0].astype(F32))
        for d in range(1, n_parts):
            gv = gv + jnp.where(me_ref[0] == d, own, got_ref[d].astype(F32))
        _adamw_math(gv, w_ref, m_ref, v_ref, go_ref, d_ref, mo_ref, vo_ref)

    tile = pl.BlockSpec(blk, lambda i, me_ref: at(i))
    out = jax.ShapeDtypeStruct((r, c), F32)
    return pl.pallas_call(
        body, name=name,
        grid_spec=pltpu.PrefetchScalarGridSpec(
            num_scalar_prefetch=1, grid=(steps,),
            in_specs=[tile, pl.BlockSpec((n_parts,) + blk, lambda i, me_ref: (0,) + at(i)),
                      pl.BlockSpec((None,) + blk, lambda i, me_ref: (me_ref[0],) + at(i)), tile, tile] + [ANY] * len(after),
            out_specs=[tile] * 4),
        out_shape=[out] * 4, compiler_params=_params(("parallel",)),
    )(me.astype(jnp.int32).reshape(1), w, got, mine, m, v, *after)


def _adamw_math(gv, w_ref, m_ref, v_ref, go_ref, d_ref, mo_ref, vo_ref):
    mn = B1 * m_ref[...] + (1.0 - B1) * gv
    vn = B2 * v_ref[...] + (1.0 - B2) * (gv * gv)
    m_hat = mn / (1.0 - B1 ** STEP)
    v_hat = vn / (1.0 - B2 ** STEP)
    go_ref[...] = gv
    d_ref[...] = -LR * (m_hat / (jnp.sqrt(v_hat) + ADAM_EPS) + WD * w_ref[...])
    mo_ref[...] = mn
    vo_ref[...] = vn


def _adamw(w, g, m, v, name):
    def body(w_ref, g_ref, m_ref, v_ref, go_ref, d_ref, mo_ref, vo_ref):
        _adamw_math(g_ref[...], w_ref, m_ref, v_ref, go_ref, d_ref, mo_ref, vo_ref)

    tile = pl.BlockSpec(w.shape, lambda i: (0, 0))
    out = jax.ShapeDtypeStruct(w.shape, F32)
    return pl.pallas_call(
        body, name=name, grid=(1,), in_specs=[tile] * 4, out_specs=[tile] * 4, out_shape=[out] * 4,
        compiler_params=_params(("parallel",)),
    )(w, g, m, v)


BIG = ["w_in", "w_a_out", "w_b_out", "w_mix_out", "w_ff_gate", "w_ff_up", "w_ff_down", "w_ple_gate", "w_ple_proj"]
TRANSPOSED = ["w_in", "w_ff_gate", "w_ff_up"]
GRAD_OF = dict(w_ple_proj="pp", w_ple_gate="pg", w_ff_down="down", w_ff_gate="gate_t", w_ff_up="up_t", w_mix_out="mix", w_a_out="a_out",
               w_b_out="b_out")
SMALL = ["conv_w", "w_alpha_up", "b_alpha_up", "gla_head_gain", "g_pre_mix", "g_post_mix", "g_pre_ffn", "g_post_ffn", "g_pre_ple", "g_post_ple"]
WEIGHTS = ["w_in", "conv_w", "w_a_out", "w_alpha_up", "b_alpha_up", "gla_head_gain", "w_b_out", "w_mix_out", "g_pre_mix", "g_post_mix",
           "g_pre_ffn", "g_post_ffn", "w_ff_gate", "w_ff_up", "w_ff_down", "g_pre_ple", "g_post_ple", "w_ple_gate", "w_ple_proj"]


def _in_t_from_blocks(z):
    w = z.reshape(-1, z.shape[-1])
    return w[R_GA:R_END], jnp.concatenate([w[:R_GA], jnp.zeros((REST - R_GA, w.shape[1]), w.dtype)], axis=0)


def _blocks_from_in_t(g_gates, g_rest):
    per = R_END // N_DEV

    def rows(lo, hi):
        out = []
        for n0, n1, g in ((0, R_GA, g_rest), (R_GA, R_END, g_gates)):
            a, e = max(lo, n0), min(hi, n1)
            if a < e:
                out.append(g[a - n0:e - n0])
        return out

    return jnp.stack([jnp.concatenate(rows(b * per, (b + 1) * per), axis=0) for b in range(N_DEV)])


def _cols_to_full(g8):
    n, r, c = g8.shape
    return jnp.transpose(g8, (1, 0, 2)).reshape(r, n * c)


def _full_to_cols(a):
    r, c = a.shape
    return jnp.transpose(a.reshape(r, N_DEV, c // N_DEV), (1, 0, 2))


def _pack(arrs, rows):
    flat = jnp.concatenate([a.reshape(-1) for a in arrs])
    return jnp.pad(flat, (0, rows * 128 - flat.shape[0])).reshape(rows, 128)


def _unpack(packed, shapes):
    flat, out, o = packed.reshape(-1), [], 0
    for s in shapes:
        size = 1
        for d in s:
            size *= d
        out.append(flat[o:o + size].reshape(s))
        o += size
    return out


def kernel(x, p, w_in, conv_w, w_a_out, w_alpha_up, b_alpha_up, gla_head_gain, w_b_out, w_mix_out, g_pre_mix, g_post_mix, g_pre_ffn, g_post_ffn, w_ff_gate, w_ff_up, w_ff_down, g_pre_ple, g_post_ple, w_ple_gate, w_ple_proj, loss_target, m_w_in, m_conv_w, m_w_a_out, m_w_alpha_up, m_b_alpha_up, m_gla_head_gain, m_w_b_out, m_w_mix_out, m_g_pre_mix, m_g_post_mix, m_g_pre_ffn, m_g_post_ffn, m_w_ff_gate, m_w_ff_up, m_w_ff_down, m_g_pre_ple, m_g_post_ple, m_w_ple_gate, m_w_ple_proj, v_w_in, v_conv_w, v_w_a_out, v_w_alpha_up, v_b_alpha_up, v_gla_head_gain, v_w_b_out, v_w_mix_out, v_g_pre_mix, v_g_post_mix, v_g_pre_ffn, v_g_post_ffn, v_w_ff_gate, v_w_ff_up, v_w_ff_down, v_g_pre_ple, v_g_post_ple, v_w_ple_gate, v_w_ple_proj):
    args = dict(locals())
    shard = lambda n, a: jnp.transpose(a[0]) if n in TRANSPOSED else a[0]
    wts = {n: shard(n, args[n]) for n in WEIGHTS}
    mom = {n: shard(n, args["m_" + n]) for n in WEIGHTS}
    var = {n: shard(n, args["v_" + n]) for n in WEIGHTS}
    me =4 * lax.axis_index("x") + 2 * lax.axis_index("y") + lax.axis_index("c")

    groups = [["w_in", "conv_w", "w_alpha_up"], ["w_a_out", "w_b_out", "w_mix_out"], ["w_ff_gate", "w_ff_up"], ["w_ff_down"],
              ["w_ple_gate", "w_ple_proj"]]
    grad_groups = []
    rows_full = lambda g: g.reshape(-1, g.shape[-1])
    gathers, scatters = {}, {}

    def gather_start(gi, after):
        if gi not in gathers:
            shards = [wts[n].astype(BF16) if n in BIG else wts[n] for n in groups[gi]]
            zones = [jax.ShapeDtypeStruct((N_DEV,) + s.shape, s.dtype) for s in shards]
            plan, peers = (_gather_plan_near, 3) if gi == 0 else (_gather_plan, 4)
            gathers[gi] = (shards, _copies_start(plan, peers * len(shards), shards, zones, "gather_start_%d" % gi, after))
        return gathers[gi][1][-1]

    def gather_pass_on(gi, after):
        shards, state = gathers[gi]
        gathers[gi] = (shards, _copies_relay(_gather_plan, state, len(shards), _pass_on_plan, 3 * len(shards), "gather_pass_on_%d" % gi, after))
        return gathers[gi][1][-1]

    def gather_finish(gi, after):
        shards, state = gathers[gi]
        shards, zones = _copies_wait(_gather_plan_near if gi == 0 else _pass_on_plan, state, len(shards), "gather_wait_%d" % gi, after)
        if gi == 0:
            zones = _gather_relay(zones, "gather_relay_%d" % gi)
            gather_start(1, (zones[0],))
        g8 = {n: _put_own(s, z, me, "gather_own_" + n) for n, s, z in zip(groups[gi], shards, zones)}
        if gi == 0:
            in_gates_t, in_rest_t = _in_t_from_blocks(g8["w_in"])
            return dict(in_gates_t=in_gates_t, in_rest_t=in_rest_t,
                        conv_w=_cols_to_full(g8["conv_w"]), w_alpha_up=_cols_to_full(g8["w_alpha_up"]))
        if gi == 1:
            return dict(a_out=_cols_to_full(g8["w_a_out"]), b_out=_cols_to_full(g8["w_b_out"]), mix=rows_full(g8["w_mix_out"]))
        if gi == 2:
            return dict(gate_t=rows_full(g8["w_ff_gate"]), up_t=rows_full(g8["w_ff_up"]))
        if gi == 3:
            return dict(down=rows_full(g8["w_ff_down"]))
        return dict(pg=rows_full(g8["w_ple_gate"]), pp=g8["w_ple_proj"])

    def scatter_start(names, gw):
        gi = len(grad_groups)
        grad_groups.append(names)
        full = {n: _blocks_from_in_t(gw["in_gates_t"], gw["in_rest_t"]) if n == "w_in" else gw[GRAD_OF[n]] for n in names}
        for n in names:
            if n in ("w_a_out", "w_b_out"):
                full[n] = _full_to_cols(full[n])
        parts = [full[n] if full[n].ndim == 3 else full[n].reshape(N_DEV, -1, full[n].shape[-1]) for n in names]
        if names == ["w_in"]:
            quarter = [jax.ShapeDtypeStruct((4,) + a.shape[1:], a.dtype) for a in parts]
            scatters[gi] = _copies_start(_sibling_plan, 4 * len(parts), parts, quarter, "scatter_sibling_start_%d" % gi)
        else:
            scatters[gi] = _copies_start(_scatter_plan, (N_DEV - 1) * len(parts), parts, parts, "scatter_start_%d" % gi)
        return scatters[gi][-1]

    def scatter_next(after):
        gi, names = len(grad_groups) - 1, grad_groups[-1]
        parts, from_sibling = _copies_wait(_sibling_plan, scatters[gi], len(names), "scatter_sibling_wait_%d" % gi, after)
        parts = [_pair_add(a, b, "scatter_add_%d_%s" % (gi, n)) for n, a, b in zip(names, parts, from_sibling)]
        scatters[gi] = _copies_start(_chip_plan, 3 * len(parts), parts, parts, "scatter_start_%d" % gi)
        return scatters[gi][-1]

    small = {n: wts[n].reshape(1, -1) for n in SMALL[2:]}

    loss_rows, grad_x, gs = _local_step(x[0], p[0, 0], loss_target[0], gather_start, gather_pass_on, gather_finish, scatter_start,
                                        scatter_next, small)
    gs["loss"] = jnp.sum(loss_rows).reshape(1, 1)

    small_shapes = [gs[n].shape for n in SMALL]
    gs_packed = _pack([gs[n] for n in SMALL + ["loss"]], 192)
    small_state = _copies_start(_everyone_plan, N_DEV - 1, [gs_packed], [jax.ShapeDtypeStruct((N_DEV,) + gs_packed.shape, F32)],
                                "small_start", (grad_x,))

    res, done = {}, (small_state[-1],)
    for gi, names in enumerate(grad_groups):
        plan, slot = (_chip_plan, me // 2) if names == ["w_in"] else (_scatter_plan, me)
        mine, got = _copies_wait(plan, scatters[gi], len(names), "scatter_wait_%d" % gi, done)
        for n, g, own in zip(names, got, mine):
            res[n] = _adamw_parts(wts[n], g, own, slot, mom[n], var[n], "adamw_" + n)
        done = tuple(res[n][1] for n in names)

    (gs_own,), (gs_got,) = _copies_wait(_everyone_plan, small_state, 1, "small_wait", done)
    gsum = dict(zip(SMALL + ["loss"], _unpack(_sum_parts(gs_got, gs_own, me, "small_sum"), small_shapes + [(1, 1)])))
    loss = gsum["loss"].reshape(())
    gsum["conv_w"] = lax.dynamic_index_in_dim(gsum["conv_w"].reshape(3, N_DEV, -1), me, axis=1, keepdims=False)
    gsum["w_alpha_up"] = lax.dynamic_index_in_dim(gsum["w_alpha_up"].reshape(GATE_RANK, N_DEV, -1), me, axis=1, keepdims=False)

    shard_shapes = [wts[n].shape for n in SMALL]
    packed = [_pack([d[n] for n in SMALL], 120) for d in (wts, gsum, mom, var)]
    outs = [_unpack(o, shard_shapes) for o in _adamw(*packed, "adamw_small")]
    for i, n in enumerate(SMALL):
        res[n] = [o[i] for o in outs]

    back = lambda n, a: (jnp.transpose(a) if n in TRANSPOSED else a)[None]
    return (loss, grad_x[None], *[back(n, res[n][i]) for i in range(4) for n in WEIGHTS])
```

```python
import functools

import jax
import jax.numpy as jnp
from jax import lax
from jax.experimental import pallas as pl
from jax.experimental.pallas import tpu as pltpu

F32, BF16 = jnp.float32, jnp.bfloat16
EPS = 1e-6
CHUNK = 64
STEP_CHUNKS = 2
STEP_ROWS = STEP_CHUNKS * CHUNK
HEADS, DK, DV = 4, 128, 256
GATE_RANK = 16
TAU = 16.0
LR, B1, B2, ADAM_EPS, WD, STEP = 0.001, 0.9, 0.999, 1e-08, 0.01, 10
N_DEV = 8
MESH = pl.DeviceIdType.MESH
VMEM_LIMIT = 56 * 1024 * 1024
ANY = pl.BlockSpec(memory_space=pl.ANY)

C_GA, C_GB = 0, 2048
C_AX, C_AB, C_AC, C_Q, C_K, C_V, C_OG, C_ALR = 0, 1024, 2048, 3072, 3584, 4096, 5120, 6144
REST = 6272
R_GA, R_END = 6160, 10256


def _params(sem):
    return pltpu.CompilerParams(dimension_semantics=sem, vmem_limit_bytes=VMEM_LIMIT)


def _pick(n, cands):
    for c in cands:
        if n % c == 0:
            return c
    return n


def _tiles(r, c):
    for tr in (128, 64):
        if r % tr == 0:
            return r // tr, (tr, c), lambda i: (i, 0)
    tc = _pick(c, (256, 128))
    return c // tc, (r, tc), lambda i: (0, i)


def _mm(a, b, mode, out_dtype, name, after=(), add=None, b3=False, out3=False, tm=None, tk=None, n_outer=False):
    bshape = (b.shape[1], N_DEV * b.shape[2]) if b3 else b.shape
    if mode == "nn":
        (m, k), (k2, n) = a.shape, bshape
    elif mode == "nt":
        (m, k), (n, k2) = a.shape, bshape
    else:
        (k, m), (k2, n) = a.shape, bshape
    assert k == k2 and a.dtype == BF16 and b.dtype == BF16, (name, a.shape, b.shape, a.dtype, b.dtype)
    tm = tm if tm and m % tm == 0 else _pick(m, (2048, 1024, 512, 256))
    tn = _pick(n, (1152, 1024, 1408, 896, 512, 256))
    tk = min(tk, k) if tk and k % min(tk, k) == 0 else _pick(k, (2048, 1408, 1152, 1024, 896, 512, 256))
    if out3 or (b3 and mode == "nn"):
        tn = n // N_DEV
    if b3 and mode == "nt":
        tk = k // N_DEV
    nk = k // tk
    dims = {"nn": (((1,), (0,)), ((), ())), "nt": (((1,), (1,)), ((), ())), "tn": (((0,), (0,)), ((), ()))}[mode]
    n_extra = len(after) + (add is not None)

    def body(a_ref, b_ref, *rest):
        o_ref = rest[n_extra]
        prod = lax.dot_general(a_ref[...], b_ref[...], dims, preferred_element_type=F32)
        if nk == 1:
            o_ref[...] = (prod if add is None else prod + rest[0][...]).astype(o_ref.dtype)
            return
        acc_ref = rest[n_extra + 1]
        kk = pl.program_id(2)

        @pl.when(kk == 0)
        def _():
            acc_ref[...] = prod if add is None else prod + rest[0][...]

        @pl.when((kk > 0) & (kk < nk - 1))
        def _():
            acc_ref[...] += prod

        @pl.when(kk == nk - 1)
        def _():
            o_ref[...] = (acc_ref[...] + prod).astype(o_ref.dtype)

    def spec(block, at):
        return pl.BlockSpec(block, (lambda g0, g1, kk: at(g1, g0, kk)) if n_outer else at)

    a_spec = spec((tk, tm), lambda i, j, kk: (kk, i)) if mode == "tn" else spec((tm, tk), lambda i, j, kk: (i, kk))
    if b3:
        b_spec = spec((None, tn, tk), lambda i, j, kk: (kk, j, 0)) if mode == "nt" else spec((None, tk, tn), lambda i, j, kk: (j, kk, 0))
    else:
        b_spec = spec((tn, tk), lambda i, j, kk: (j, kk)) if mode == "nt" else spec((tk, tn), lambda i, j, kk: (kk, j))
    tile = spec((tm, tn), lambda i, j, kk: (i, j))
    out_spec = spec((None, tm, tn), lambda i, j, kk: (j, i, 0)) if out3 else tile
    return pl.pallas_call(
        body, name=name, grid=(n // tn, m // tm, nk) if n_outer else (m // tm, n // tn, nk),
        in_specs=[a_spec, b_spec] + ([tile] if add is not None else []) + [ANY] * len(after), out_specs=out_spec,
        out_shape=jax.ShapeDtypeStruct((N_DEV, m, tn) if out3 else (m, n), out_dtype),
        scratch_shapes=[pltpu.VMEM((tm, tn), F32)] if nk > 1 else [],
        compiler_params=_params(("parallel", "parallel", "arbitrary")),
    )(a, b, *([add] if add is not None else []), *after)


LONG_K = dict(tm=512, tk=8192, n_outer=True)


def _rows(body, t, tr, ins, outs, name, after=()):
    in_specs = []
    for arr, sp in ins:
        if sp[0] == "t":
            in_specs.append(pl.BlockSpec((tr, sp[1]), lambda i, cb=sp[2]: (i, cb)))
        else:
            in_specs.append(pl.BlockSpec(arr.shape, lambda i, nd=arr.ndim: (0,) * nd))
    out_specs, out_shape = [], []
    for shape, dt, kind in outs:
        out_specs.append(pl.BlockSpec((tr, shape[1]), lambda i: (i, 0)) if kind == "t" else pl.BlockSpec(shape, lambda i: (0, 0)))
        out_shape.append(jax.ShapeDtypeStruct(shape, dt))
    return pl.pallas_call(
        body, name=name, grid=(t // tr,), in_specs=in_specs + [ANY] * len(after), out_specs=out_specs, out_shape=out_shape,
        compiler_params=_params(("arbitrary",)),
    )(*[arr for arr, _ in ins], *after)


def _rinv(v):
    return lax.rsqrt(jnp.mean(v * v, axis=-1, keepdims=True) + EPS)


def _sig(v):
    return 1.0 / (1.0 + jnp.exp(-v))


def _acc(ref, val):
    @pl.when(pl.program_id(0) == 0)
    def _():
        ref[...] = jnp.zeros_like(ref)

    ref[...] += jnp.sum(val, axis=0, keepdims=True)


def _rms_fwd(x, g, name):
    t, d = x.shape

    def body(x_ref, g_ref, h_ref):
        xv = x_ref[...]
        h_ref[...] = (xv * _rinv(xv) * g_ref[...]).astype(BF16)

    return _rows(body, t, 256, [(x, ("t", d, 0)), (g, ("b",))], [((t, d), BF16, "t")], name)[0]


def _post_pre(x, m, g_post, g_pre, name, after=()):
    t, d = x.shape

    def body(x_ref, m_ref, gp_ref, gn_ref, *rest):
        xo_ref, h_ref = rest[len(after):]
        mv = m_ref[...]
        xn = x_ref[...] + mv * _rinv(mv) * gp_ref[...]
        xo_ref[...] = xn
        h_ref[...] = (xn * _rinv(xn) * gn_ref[...]).astype(BF16)

    return _rows(body, t, 256, [(x, ("t", d, 0)), (m, ("t", d, 0)), (g_post, ("b",)), (g_pre, ("b",))],
                 [((t, d), F32, "t"), ((t, d), BF16, "t")], name, after)


def _mix_fwd(proj, ya, yb, name):
    t, d = ya.shape

    def body(ga_ref, gb_ref, ya_ref, yb_ref, o_ref):
        o_ref[...] = (_sig(ga_ref[...].astype(F32)) * ya_ref[...].astype(F32)
                      + _sig(gb_ref[...].astype(F32)) * yb_ref[...].astype(F32)).astype(BF16)

    return _rows(body, t, 256, [(proj, ("t", d, C_GA // d)), (proj, ("t", d, C_GB // d)), (ya, ("t", d, 0)), (yb, ("t", d, 0))],
                 [((t, d), BF16, "t")], name)[0]


def _mix_bwd(dmix, proj, ya, yb, name):
    t, d = ya.shape

    def body(dm_ref, ga_ref, gb_ref, ya_ref, yb_ref, dg_ref, dya_ref, dyb_ref):
        dm = dm_ref[...]
        sa, sb = _sig(ga_ref[...].astype(F32)), _sig(gb_ref[...].astype(F32))
        dg_ref[:, :d] = (dm * ya_ref[...].astype(F32) * sa * (1.0 - sa)).astype(BF16)
        dg_ref[:, d:] = (dm * yb_ref[...].astype(F32) * sb * (1.0 - sb)).astype(BF16)
        dya_ref[...] = (dm * sa).astype(BF16)
        dyb_ref[...] = (dm * sb).astype(BF16)

    return _rows(body, t, 256,
                 [(dmix, ("t", d, 0)), (proj, ("t", d, C_GA // d)), (proj, ("t", d, C_GB // d)), (ya, ("t", d, 0)), (yb, ("t", d, 0))],
                 [((t, 2 * d), BF16, "t"), ((t, d), BF16, "t"), ((t, d), BF16, "t")], name)


def _swiglu_call(body, ins, n_out, name):
    t, f = ins[0].shape
    tc = _pick(f, (1408, 512))
    tile = pl.BlockSpec((512, tc), lambda i, j: (i, j))
    return pl.pallas_call(
        body, name=name, grid=(t // 512, f // tc), in_specs=[tile] * len(ins), out_specs=[tile] * n_out,
        out_shape=[jax.ShapeDtypeStruct((t, f), BF16)] * n_out, compiler_params=_params(("parallel", "parallel")),
    )(*ins)


def _swiglu_fwd(fg, fu, name):
    def body(g_ref, u_ref, s_ref):
        gv = g_ref[...].astype(F32)
        s_ref[...] = (gv * _sig(gv) * u_ref[...].astype(F32)).astype(BF16)

    return _swiglu_call(body, [fg, fu], 1, name)[0]


def _swiglu_bwd(ds, fg, fu, name):
    def body(ds_ref, g_ref, u_ref, dg_ref, du_ref):
        dsv, gv, uv = ds_ref[...].astype(F32), g_ref[...].astype(F32), u_ref[...].astype(F32)
        sg = _sig(gv)
        dg_ref[...] = (dsv * uv * sg * (1.0 + gv * (1.0 - sg))).astype(BF16)
        du_ref[...] = (dsv * gv * sg).astype(BF16)

    return _swiglu_call(body, [ds, fg, fu], 2, name)


def _ple_final(x2, pg, pp, tgt, g_post, name):
    t, d = x2.shape

    def body(x_ref, pg_ref, pp_ref, t_ref, g_ref, loss_ref, d3_ref, dpg_ref, dpp_ref, dg_ref):
        sg, ppv, g = _sig(pg_ref[...]), pp_ref[...], g_ref[...]
        e = sg * ppv
        r = _rinv(e)
        eh = e * r
        diff = x_ref[...] + eh * g - t_ref[...]
        loss_ref[...] = 0.5 * jnp.mean(diff * diff, axis=-1, keepdims=True)
        d3 = diff * (1.0 / d)
        d3_ref[...] = d3
        gd = d3 * g
        de = r * (gd - eh * jnp.mean(gd * eh, axis=-1, keepdims=True))
        dpg_ref[...] = (de * ppv * sg * (1.0 - sg)).astype(BF16)
        dpp_ref[...] = (de * sg).astype(BF16)
        _acc(dg_ref, d3 * eh)

    return _rows(body, t, 256, [(x2, ("t", d, 0)), (pg, ("t", d, 0)), (pp, ("t", d, 0)), (tgt, ("t", d, 0)), (g_post, ("b",))],
                 [((t, 1), F32, "t"), ((t, d), F32, "t"), ((t, d), BF16, "t"), ((t, d), BF16, "t"), ((1, d), F32, "a")], name)


def _norm_bwd(dn, dh, x, g_pre, fm, g_post, name):
    t, d = x.shape
    two = fm is not None

    def body(*refs):
        if two:
            dn_ref, dh_ref, x_ref, gp_ref, f_ref, gq_ref, dx_ref, df_ref, dgp_ref, dgq_ref = refs
        else:
            dn_ref, dh_ref, x_ref, gp_ref, dx_ref, dgp_ref = refs
        xv, dhv = x_ref[...], dh_ref[...]
        r = _rinv(xv)
        xh = xv * r
        gd = dhv * gp_ref[...]
        dx = dn_ref[...] + r * (gd - xh * jnp.mean(gd * xh, axis=-1, keepdims=True))
        dx_ref[...] = dx
        _acc(dgp_ref, dhv * xh)
        if two:
            fv = f_ref[...]
            rf = _rinv(fv)
            fh = fv * rf
            gd2 = dx * gq_ref[...]
            df_ref[...] = (rf * (gd2 - fh * jnp.mean(gd2 * fh, axis=-1, keepdims=True))).astype(BF16)
            _acc(dgq_ref, dx * fh)

    ins = [(dn, ("t", d, 0)), (dh, ("t", d, 0)), (x, ("t", d, 0)), (g_pre, ("b",))]
    outs = [((t, d), F32, "t")]
    if two:
        ins += [(fm, ("t", d, 0)), (g_post, ("b",))]
        outs += [((t, d), BF16, "t"), ((1, d), F32, "a"), ((1, d), F32, "a")]
    else:
        outs += [((1, d), F32, "a")]
    return _rows(body, t, 256, ins, outs, name)


CONV_TC = 256


def _shift_down(v, s):
    rows = lax.broadcasted_iota(jnp.int32, v.shape, 0)
    return jnp.where(rows >= s, pltpu.roll(v, s, 0), 0.0)


def _shift_up(v, s):
    n = v.shape[0]
    rows = lax.broadcasted_iota(jnp.int32, v.shape, 0)
    return jnp.where(rows < n - s, pltpu.roll(v, n - s, 0), 0.0)


def _conv_specs(t):
    nb = 1024 // CONV_TC
    seg = lambda c0: pl.BlockSpec((t, CONV_TC), lambda j, cb=c0 // CONV_TC: (0, cb + j))
    own = pl.BlockSpec((t, CONV_TC), lambda j: (0, j))
    wspec = pl.BlockSpec((3, CONV_TC), lambda j: (0, j))
    return nb, seg, own, wspec


def _conv_fwd(proj, conv_w, name, after=()):
    t = proj.shape[0]
    nb, seg, own, wspec = _conv_specs(t)

    def body(ax_ref, ab_ref, ac_ref, w_ref, *rest):
        za_ref = rest[len(after)]
        u = ac_ref[...].astype(F32) * ax_ref[...].astype(F32)
        w = w_ref[...]
        yc = w[0:1] * _shift_down(u, 2) + w[1:2] * _shift_down(u, 1) + w[2:3] * u
        za_ref[...] = (ab_ref[...].astype(F32) * yc).astype(BF16)

    return pl.pallas_call(
        body, name=name, grid=(nb,), in_specs=[seg(C_AX), seg(C_AB), seg(C_AC), wspec] + [ANY] * len(after), out_specs=own,
        out_shape=jax.ShapeDtypeStruct((t, 1024), BF16), compiler_params=_params(("parallel",)),
    )(proj, proj, proj, conv_w, *after)


def _conv_bwd(dza, proj, conv_w, name):
    t = proj.shape[0]
    nb, seg, own, wspec = _conv_specs(t)

    def body(dz_ref, ax_ref, ab_ref, ac_ref, w_ref, dax_ref, dab_ref, dac_ref, dw_ref):
        ax, ab, ac, dz = ax_ref[...].astype(F32), ab_ref[...].astype(F32), ac_ref[...].astype(F32), dz_ref[...].astype(F32)
        w = w_ref[...]
        u = ac * ax
        u1, u2 = _shift_down(u, 1), _shift_down(u, 2)
        yc = w[0:1] * u2 + w[1:2] * u1 + w[2:3] * u
        dab_ref[...] = (dz * yc).astype(BF16)
        dyc = dz * ab
        du = w[2:3] * dyc + w[1:2] * _shift_up(dyc, 1) + w[0:1] * _shift_up(dyc, 2)
        dax_ref[...] = (du * ac).astype(BF16)
        dac_ref[...] = (du * ax).astype(BF16)
        dw_ref[0:1, :] = jnp.sum(dyc * u2, axis=0, keepdims=True)
        dw_ref[1:2, :] = jnp.sum(dyc * u1, axis=0, keepdims=True)
        dw_ref[2:3, :] = jnp.sum(dyc * u, axis=0, keepdims=True)

    act = jax.ShapeDtypeStruct((t, 1024), BF16)
    return pl.pallas_call(
        body, name=name, grid=(nb,), in_specs=[own, seg(C_AX), seg(C_AB), seg(C_AC), wspec], out_specs=[own, own, own, wspec],
        out_shape=[act, act, act, jax.ShapeDtypeStruct((3, 1024), F32)], compiler_params=_params(("parallel",)),
    )(dza, proj, proj, proj, conv_w)


def _dot(a, b, dims, precision=None):
    return lax.dot_general(a, b, (dims, ((), ())), precision=precision, preferred_element_type=F32)


_CONTRACT = {"nn": ((1,), (0,)), "nt": ((1,), (1,)), "tn": ((0,), (0,))}


def _bdot_raw(a, b, mode):
    return _dot(a.astype(BF16), b.astype(BF16), _CONTRACT[mode])


@functools.partial(jax.custom_vjp, nondiff_argnums=(2,))
def _bdot(a, b, mode):
    return _bdot_raw(a, b, mode)


def _bdot_fwd(a, b, mode):
    return _bdot_raw(a, b, mode), (a, b)


def _bdot_bwd(mode, res, ct):
    a, b = res
    if mode == "nn":
        return _bdot_raw(ct, b, "nt"), _bdot_raw(a, ct, "tn")
    if mode == "nt":
        return _bdot_raw(ct, b, "nn"), _bdot_raw(ct, a, "tn")
    return _bdot_raw(b, ct, "nt"), _bdot_raw(a, ct, "nn")


_bdot.defvjp(_bdot_fwd, _bdot_bwd)


@functools.partial(jax.custom_vjp, nondiff_argnums=(2,))
def _sum_dot(ones, x, mode):
    head = x.astype(BF16)
    tail = x - head.astype(F32)
    if mode == "nn":
        return _bdot_raw(ones, head, "nn") + _bdot_raw(ones, tail, "nn")
    return _bdot_raw(head, ones, "tn") + _bdot_raw(tail, ones, "tn")


def _sum_dot_fwd(ones, x, mode):
    return _sum_dot(ones, x, mode), ones


def _sum_dot_bwd(mode, ones, ct):
    return jnp.zeros_like(ones), (_bdot_raw(ones, ct, "tn") if mode == "nn" else _bdot_raw(ones, ct, "nt"))


_sum_dot.defvjp(_sum_dot_fwd, _sum_dot_bwd)


def _gla_chunk(q, k, v, og, alr, s_in, wa, ba, gain):
    c = q.shape[0]
    z =_bdot(alr, wa, "nn") + ba
    la = (jnp.minimum(z, 0.0) - jnp.log(1.0 + jnp.exp(-jnp.abs(z)))) * (1.0 / TAU)
    row = lax.broadcasted_iota(jnp.int32, (c, c), 0)
    col = lax.broadcasted_iota(jnp.int32, (c, c), 1)
    lower = row >= col
    b = _sum_dot(lower.astype(F32), la, "nn")
    trow = lax.broadcasted_iota(jnp.int32, la.shape, 0)
    mid = jnp.sum(jnp.where(trow <= c // 2, la, 0.0), axis=0, keepdims=True)
    blast = jnp.sum(la, axis=0, keepdims=True)
    qs = q * (DK ** -0.5)
    e_up, e_dn = jnp.exp(b - mid), jnp.exp(mid - b)
    a_fwd = _bdot(qs * e_up, k * e_dn, "nt")
    a_rev = _bdot(qs * e_dn, k * e_up, "nt")
    att = jnp.where(lower, a_fwd, a_rev)
    o = _bdot(att, v, "nn") + _bdot(qs * jnp.exp(b), s_in, "nn")
    upd = _bdot(k * jnp.exp(blast - b), v, "tn")
    blast_col = _sum_dot(jnp.ones((c, DV), F32), la, "tn")
    s_out = jnp.exp(blast_col) * s_in + upd
    on = o * _rinv(o) * gain
    return on * og * _sig(og), s_out


def _gla_specs(t, rev):
    n = t // STEP_ROWS
    ch = (lambda i: n - 1 - i) if rev else (lambda i: i)
    col = lambda w, c0: pl.BlockSpec((STEP_ROWS, HEADS * w), lambda i, cb=c0 // (HEADS * w): (ch(i), cb))
    whole = lambda shape: pl.BlockSpec(shape, lambda i, nd=len(shape): (0,) * nd)
    specs = dict(
        q=col(DK, C_Q), k=col(DK, C_K), v=col(DV, C_V), og=col(DV, C_OG),
        alr=pl.BlockSpec((STEP_ROWS, 128), lambda i: (ch(i), C_ALR // 128)),
        wa=whole((128, HEADS * DK)), ba=whole((1, HEADS * DK)), gain=whole((1, DV)),
        state=pl.BlockSpec((STEP_CHUNKS, HEADS, DK, DV), lambda i: (ch(i), 0, 0, 0)),
        odk=pl.BlockSpec((STEP_ROWS, HEADS * DK), lambda i: (ch(i), 0)), odv=pl.BlockSpec((STEP_ROWS, HEADS * DV), lambda i: (ch(i), 0)),
        oalr=pl.BlockSpec((STEP_ROWS, 128), lambda i: (ch(i), 0)), whole=whole,
    )
    return n, specs


def _head_cols(h):
    return slice(h * DK, (h + 1) * DK), slice(h * DV, (h + 1) * DV)


def _gla_fwd(proj, wa, ba, gain, name):
    t = proj.shape[0]
    n, sp = _gla_specs(t, False)

    def body(q_ref, k_ref, v_ref, og_ref, alr_ref, wa_ref, ba_ref, g_ref, zb_ref, st_ref, s_scr):
        @pl.when(pl.program_id(0) == 0)
        def _():
            s_scr[...] = jnp.zeros_like(s_scr)

        state = [s_scr[h] for h in range(HEADS)]
        for c in range(STEP_CHUNKS):
            rows = slice(c * CHUNK, (c + 1) * CHUNK)
            alr = alr_ref[rows, :].astype(F32)
            for h in range(HEADS):
                kc, vc = _head_cols(h)
                st_ref[c, h] = state[h]
                zb, state[h] = _gla_chunk(q_ref[rows, kc].astype(F32), k_ref[rows, kc].astype(F32), v_ref[rows, vc].astype(F32),
                                          og_ref[rows, vc].astype(F32), alr, state[h], wa_ref[:, kc].astype(F32), ba_ref[:, kc], g_ref[...])
                zb_ref[rows, vc] = zb.astype(BF16)
        for h in range(HEADS):
            s_scr[h] = state[h]

    return pl.pallas_call(
        body, name=name, grid=(n,),
        in_specs=[sp["q"], sp["k"], sp["v"], sp["og"], sp["alr"], sp["wa"], sp["ba"], sp["gain"]],
        out_specs=[sp["odv"], sp["state"]],
        out_shape=[jax.ShapeDtypeStruct((t, HEADS * DV), BF16), jax.ShapeDtypeStruct((t // CHUNK, HEADS, DK, DV), F32)],
        scratch_shapes=[pltpu.VMEM((HEADS, DK, DV), F32)],
        compiler_params=_params(("arbitrary",)),
    )(proj, proj, proj, proj, proj, wa, ba, gain)


def _gla_bwd(dzb, proj, states, wa, ba, gain, name):
    t = proj.shape[0]
    n, sp = _gla_specs(t, True)

    def body(dz_ref, q_ref, k_ref, v_ref, og_ref, alr_ref, st_ref, wa_ref, ba_ref, g_ref,
             dq_ref, dk_ref, dv_ref, dog_ref, dalr_ref, dwa_ref, dba_ref, dg_ref, ds_scr):
        @pl.when(pl.program_id(0) == 0)
        def _():
            ds_scr[...] = jnp.zeros_like(ds_scr)
            dwa_ref[...] = jnp.zeros_like(dwa_ref)
            dba_ref[...] = jnp.zeros_like(dba_ref)
            dg_ref[...] = jnp.zeros_like(dg_ref)

        dstate = [ds_scr[h] for h in range(HEADS)]
        dwa_sum, dba_sum, dgain_sum = [None] * HEADS, [None] * HEADS, None
        for c in reversed(range(STEP_CHUNKS)):
            rows = slice(c * CHUNK, (c + 1) * CHUNK)
            alr = alr_ref[rows, :].astype(F32)
            dalr_sum = None
            for h in range(HEADS):
                kc, vc = _head_cols(h)
                args = (q_ref[rows, kc].astype(F32), k_ref[rows, kc].astype(F32), v_ref[rows, vc].astype(F32), og_ref[rows, vc].astype(F32),
                        alr, st_ref[c, h], wa_ref[:, kc].astype(F32), ba_ref[:, kc], g_ref[...])
                _, vjp = jax.vjp(_gla_chunk, *args)
                dq, dk, dv, dog, dalr, dstate[h], dwa, dba, dgain = vjp((dz_ref[rows, vc].astype(F32), dstate[h]))
                dq_ref[rows, kc] = dq.astype(BF16)
                dk_ref[rows, kc] = dk.astype(BF16)
                dv_ref[rows, vc] = dv.astype(BF16)
                dog_ref[rows, vc] = dog.astype(BF16)
                dwa_sum[h] = dwa if dwa_sum[h] is None else dwa_sum[h] + dwa
                dba_sum[h] = dba if dba_sum[h] is None else dba_sum[h] + dba
                dalr_sum = dalr if dalr_sum is None else dalr_sum + dalr
                dgain_sum = dgain if dgain_sum is None else dgain_sum + dgain
            dalr_ref[rows, :] = dalr_sum
        for h in range(HEADS):
            ds_scr[h] = dstate[h]
            dwa_ref[h] += dwa_sum[h]
            dba_ref[h] += dba_sum[h]
        dg_ref[...] += dgain_sum

    whole = sp["whole"]
    return pl.pallas_call(
        body, name=name, grid=(n,),
        in_specs=[sp["odv"], sp["q"], sp["k"], sp["v"], sp["og"], sp["alr"], sp["state"], sp["wa"], sp["ba"], sp["gain"]],
        out_specs=[sp["odk"], sp["odk"], sp["odv"], sp["odv"], sp["oalr"], whole((HEADS, 128, DK)), whole((HEADS, 1, DK)), whole((1, DV))],
        out_shape=[jax.ShapeDtypeStruct((t, HEADS * DK), BF16), jax.ShapeDtypeStruct((t, HEADS * DK), BF16),
                   jax.ShapeDtypeStruct((t, HEADS * DV), BF16), jax.ShapeDtypeStruct((t, HEADS * DV), BF16),
                   jax.ShapeDtypeStruct((t, 128), F32), jax.ShapeDtypeStruct((HEADS, 128, DK), F32),
                   jax.ShapeDtypeStruct((HEADS, 1, DK), F32), jax.ShapeDtypeStruct((1, DV), F32)],
        scratch_shapes=[pltpu.VMEM((HEADS, DK, DV), F32)],
        compiler_params=_params(("arbitrary",)),
    )(dzb, proj, proj, proj, proj, proj, states, wa, ba, gain)


def _local_step(x, p, tgt, gather_start, gather_pass_on, gather_finish, scatter_start, scatter_next, small):
    b_alpha, gain = small["b_alpha_up"], small["gla_head_gain"]
    gather_start(0, ())
    w = dict(gather_finish(0, ()))
    conv_w, w_alpha = w["conv_w"], w["w_alpha_up"]
    wa_p = jnp.zeros((128, HEADS * DK), BF16).at[:GATE_RANK].set(w_alpha.astype(BF16))

    t2 = gather_start(2, (w["in_rest_t"], gather_start(1, ())))
    h1 = _rms_fwd(x, small["g_pre_mix"], "rms_pre_mix")
    proj = _mm(h1, w["in_rest_t"], "nt", BF16, "mm_proj", after=(t2,))
    proj_gates = _mm(h1, w["in_gates_t"], "nt", BF16, "mm_proj_gates", after=(t2,))
    za = _conv_fwd(proj, conv_w, "conv_fwd", after=(gather_pass_on(1, (proj,)),))
    zb, states = _gla_fwd(proj, wa_p, b_alpha, gain, "gla_fwd")
    t3 = gather_start(3, (zb, za))
    w.update(gather_finish(1, (t3,)))
    ya = _mm(za, w["a_out"], "nn", BF16, "mm_ya")
    yb = _mm(zb, w["b_out"], "nn", BF16, "mm_yb")
    mix = _mix_fwd(proj_gates, ya, yb, "mix_fwd")
    m2 = _mm(mix, w["mix"], "nn", F32, "mm_mix")
    t4 = gather_start(4, (m2,))
    x1, h2 = _post_pre(x, m2, small["g_post_mix"], small["g_pre_ffn"], "norm_mix_ffn", after=(gather_pass_on(2, (m2,)),))
    w.update(gather_finish(2, (h2, t4)))
    fu = _mm(h2, w["up_t"], "nt", BF16, "mm_up")
    fg = _mm(h2, w["gate_t"], "nt", BF16, "mm_gate", after=(gather_pass_on(3, (fu,)),))
    s = _swiglu_fwd(fg, fu, "swiglu_fwd")
    w.update(gather_finish(3, (s,)))
    f = _mm(s, w["down"], "nn", F32, "mm_down", after=(gather_pass_on(4, (s,)),), **LONG_K)
    x2, h3 = _post_pre(x1, f, small["g_post_ffn"], small["g_pre_ple"], "norm_ffn_ple")
    w.update(gather_finish(4, (h3,)))
    pg = _mm(h3, w["pg"], "nn", F32, "mm_pg")
    p_bf = p.astype(BF16)
    pp = _mm(p_bf, w["pp"], "nn", F32, "mm_pp", b3=True, tm=2048)
    loss_rows, d3, dpg, dpp, dg_post_ple = _ple_final(x2, pg, pp, tgt, small["g_post_ple"], "ple_final")

    gw = {}
    gw["pp"] = _mm(p_bf, dpp, "tn", BF16, "mm_dw_pp", out3=True)
    gw["pg"] = _mm(h3, dpg, "tn", BF16, "mm_dw_pg")
    dh3 = _mm(dpg, w["pg"], "nt", F32, "mm_dh3", after=(scatter_start(["w_ple_proj", "w_ple_gate"], gw),))
    d2, df, dg_pre_ple, dg_post_ffn = _norm_bwd(d3, dh3, x2, small["g_pre_ple"], f, small["g_post_ffn"], "norm_bwd_ple_ffn")
    gw["down"] = _mm(s, df, "tn", BF16, "mm_dw_down", tm=1408)
    ds = _mm(df, w["down"], "nt", BF16, "mm_ds", after=(scatter_start(["w_ff_down"], gw),))
    dfg, dfu = _swiglu_bwd(ds, fg, fu, "swiglu_bwd")
    gw["gate_t"] = _mm(dfg, h2, "tn", BF16, "mm_dw_gate", tm=1408)
    gw["up_t"] = _mm(dfu, h2, "tn", BF16, "mm_dw_up", after=(gw["gate_t"],), tm=1408)
    dh2 = _mm(dfg, w["gate_t"], "nn", F32, "mm_dh2_gate", after=(scatter_start(["w_ff_gate", "w_ff_up"], gw),), **LONG_K)
    dh2 = _mm(dfu, w["up_t"], "nn", F32, "mm_dh2_up", add=dh2, **LONG_K)
    d1, dm2, dg_pre_ffn, dg_post_mix = _norm_bwd(d2, dh2, x1, small["g_pre_ffn"], m2, small["g_post_mix"], "norm_bwd_ffn_mix")
    dmix = _mm(dm2, w["mix"], "nt", F32, "mm_dmix")
    gw["mix"] = _mm(mix, dm2, "tn", BF16, "mm_dw_mix")
    dgab, dya, dyb = _mix_bwd(dmix, proj_gates, ya, yb, "mix_bwd")
    gw["in_gates_t"] = _mm(dgab, h1, "tn", BF16, "mm_dw_in_gates")
    dza = _mm(dya, w["a_out"], "nt", BF16, "mm_dza", after=(scatter_start(["w_mix_out"], gw), gw["in_gates_t"]))
    gw["a_out"] = _mm(za, dya, "tn", BF16, "mm_dw_a_out")
    gw["b_out"] = _mm(zb, dyb, "tn", BF16, "mm_dw_b_out", after=(gw["a_out"],))
    dzb = _mm(dyb, w["b_out"], "nt", BF16, "mm_dzb", after=(scatter_start(["w_a_out", "w_b_out"], gw),))
    dax, dab, dac, dconv = _conv_bwd(dza, proj, conv_w, "conv_bwd")
    dq, dk, dv, dog, dalr, dwa, dba, dgain = _gla_bwd(dzb, proj, states, wa_p, b_alpha, gain, "gla_bwd")
    drest = jnp.concatenate([dax, dab, dac, dq, dk, dv, dog, dalr.astype(BF16)], axis=1)
    gw["in_rest_t"] = _mm(drest, h1, "tn", BF16, "mm_dw_in_rest", tm=896)
    dh1 = _mm(dgab, w["in_gates_t"], "nn", F32, "mm_dh1_gates", after=(scatter_start(["w_in"], gw),), tm=1024, tk=4096, n_outer=True)
    dh1 = _mm(drest, w["in_rest_t"], "nn", F32, "mm_dh1_rest", add=dh1, after=(scatter_next((dh1,)),), **LONG_K)
    grad_x, dg_pre_mix = _norm_bwd(d1, dh1, x, small["g_pre_mix"], None, None, "norm_bwd_mix")

    gs = dict(
        conv_w=dconv,
        w_alpha_up=jnp.transpose(dwa[:, :GATE_RANK, :], (1, 0, 2)).reshape(GATE_RANK, HEADS * DK),
        b_alpha_up=dba.reshape(1, HEADS * DK), gla_head_gain=dgain,
        g_pre_mix=dg_pre_mix, g_post_mix=dg_post_mix, g_pre_ffn=dg_pre_ffn, g_post_ffn=dg_post_ffn,
        g_pre_ple=dg_pre_ple, g_post_ple=dg_post_ple,
    )
    return loss_rows, grad_x, gs


def _place():
    x, y, c = lax.axis_index("x"), lax.axis_index("y"), lax.axis_index("c")
    return x, y, c, [(1 - x, y), (x, 1 - y), (1 - x, 1 - y)]


def _pair_add(mine8, got4, name):
    _, r, cols = mine8.shape
    steps, blk, at = _tiles(r, cols)
    core = lax.axis_index("c").astype(jnp.int32).reshape(1)

    def body(c_ref, a_ref, b_ref, o_ref):
        o_ref[...] = (a_ref[...].astype(F32) + b_ref[...].astype(F32)).astype(BF16)

    return pl.pallas_call(
        body, name=name,
        grid_spec=pltpu.PrefetchScalarGridSpec(
            num_scalar_prefetch=1, grid=(4, steps),
            in_specs=[pl.BlockSpec((None,) + blk, lambda ch, i, c_ref: (2 * ch + c_ref[0],) + at(i)),
                      pl.BlockSpec((None,) + blk, lambda ch, i, c_ref: (ch,) + at(i))],
            out_specs=pl.BlockSpec((None,) + blk, lambda ch, i, c_ref: (ch,) + at(i))),
        out_shape=jax.ShapeDtypeStruct((4, r, cols), BF16),
        compiler_params=_params(("parallel", "parallel")),
    )(core, mine8, got4)


HBM = pl.BlockSpec(memory_space=pltpu.HBM)
SEM = pl.BlockSpec(memory_space=pltpu.SEMAPHORE)
EFFECT = pltpu.SideEffectType.DATAFLOW_SIDE_EFFECTING


def _in_hbm(a):
    return pltpu.with_memory_space_constraint(a, pltpu.HBM)


def _remote_copies(plan, srcs, lands, send_sems, recv_sems):
    return [pltpu.make_async_remote_copy(src_ref=s, dst_ref=d, send_sem=send_sems.at[i], recv_sem=recv_sems.at[i], device_id=peer,
                                         device_id_type=MESH) for i, (s, d, peer) in enumerate(plan(srcs, lands))]


def _copies_start(plan, n_copies, srcs, land_shapes, name, after=()):
    ns, nl = len(srcs), len(land_shapes)

    def body(*refs):
        send_sems, recv_sems = refs[ns + nl + len(after):ns + nl + len(after) + 2]
        for cp in _remote_copies(plan, refs[:ns], refs[ns:ns + nl], send_sems, recv_sems):
            cp.start()
        refs[-1][...] = jnp.zeros((8, 128), F32)

    sems = pltpu.SemaphoreType.DMA((n_copies,))
    return pl.pallas_call(
        body, name=name,
        out_shape=(sems, sems, *[pltpu.HBM(s.shape, s.dtype) for s in srcs], *[pltpu.HBM(s.shape, s.dtype) for s in land_shapes],
                   jax.ShapeDtypeStruct((8, 128), F32)),
        in_specs=[HBM] * (ns + nl) + [ANY] * len(after),
        out_specs=(SEM, SEM, *[HBM] * (ns + nl), pl.BlockSpec(memory_space=pltpu.VMEM)),
        input_output_aliases={i: 2 + i for i in range(ns + nl)},
        compiler_params=pltpu.CompilerParams(has_side_effects=EFFECT),
    )(*[_in_hbm(s) for s in srcs], *[_in_hbm(lax.empty(s.shape, s.dtype)) for s in land_shapes], *after)


def _copies_wait(plan, state, ns, name, after=()):
    send_sems, recv_sems, *arrs = state[:-1]
    n = len(arrs)

    def body(*refs):
        cps = _remote_copies(plan, refs[:ns], refs[ns:n], refs[n], refs[n + 1])
        for cp in cps:
            cp.wait_send()
        for cp in cps:
            cp.wait_recv()

    out = pl.pallas_call(
        body, name=name, out_shape=tuple(pltpu.HBM(a.shape, a.dtype) for a in arrs),
        in_specs=[HBM] * n + [SEM, SEM] + [ANY] * len(after), out_specs=tuple([HBM] * n),
        input_output_aliases={i: i for i in range(n)},
        compiler_params=pltpu.CompilerParams(has_side_effects=EFFECT),
    )(*arrs, send_sems, recv_sems, *after)
    return list(out[:ns]), list(out[ns:])


def _copies_relay(plan, state, ns, next_plan, n_next, name, after=()):
    send_sems, recv_sems, *arrs = state[:-1]
    n = len(arrs)

    def body(*refs):
        cps = _remote_copies(plan, refs[:ns], refs[ns:n], refs[n], refs[n + 1])
        for cp in cps:
            cp.wait_send()
        for cp in cps:
            cp.wait_recv()
        outs = refs[n + 2 + len(after):]
        for cp in _remote_copies(next_plan, refs[:ns], refs[ns:n], outs[0], outs[1]):
            cp.start()
        outs[-1][...] = jnp.zeros((8, 128), F32)

    sems = pltpu.SemaphoreType.DMA((n_next,))
    return pl.pallas_call(
        body, name=name,
        out_shape=(sems, sems, *[pltpu.HBM(a.shape, a.dtype) for a in arrs], jax.ShapeDtypeStruct((8, 128), F32)),
        in_specs=[HBM] * n + [SEM, SEM] + [ANY] * len(after),
        out_specs=(SEM, SEM, *[HBM] * n, pl.BlockSpec(memory_space=pltpu.VMEM)),
        input_output_aliases={i: 2 + i for i in range(n)},
        compiler_params=pltpu.CompilerParams(has_side_effects=EFFECT),
    )(*arrs, send_sems, recv_sems, *after)


def _pass_on_plan(srcs, lands):
    x, y, c, chips = _place()
    return [(l.at[4 * px + 2 * py + c], l.at[4 * px + 2 * py + c], (x, y, 1 - c)) for l in lands for px, py in chips]


def _gather_plan(srcs, lands):
    x, y, c, chips = _place()
    peers = [(x, y, 1 - c)] + [(*chip, c) for chip in chips]
    return [(s, l.at[4 * x + 2 * y + c], peer) for s, l in zip(srcs, lands) for peer in peers]


def _gather_plan_near(srcs, lands):
    x, y, c, _ = _place()
    peers = [(x, y, 1 - c), (1 - x, y, c), (x, 1 - y, c)]
    return [(s, l.at[4 * x + 2 * y + c], peer) for s, l in zip(srcs, lands) for peer in peers]


def _scatter_plan(srcs, lands):
    x, y, c, _ = _place()
    peers = [(1 - x if k & 4 else x, 1 - y if k & 2 else y, 1 - c if k & 1 else c) for k in range(1, N_DEV)]
    return [(s.at[4 * px + 2 * py + pc], l.at[4 * x + 2 * y + c], (px, py, pc)) for s, l in zip(srcs, lands) for px, py, pc in peers]


def _everyone_plan(srcs, lands):
    x, y, c, _ = _place()
    peers = [(1 - x if k & 4 else x, 1 - y if k & 2 else y, 1 - c if k & 1 else c) for k in range(1, N_DEV)]
    return [(s, l.at[4 * x + 2 * y + c], peer) for s, l in zip(srcs, lands) for peer in peers]


def _sum_parts(got, own, me, name):
    def body(me_ref, got_ref, own_ref, o_ref):
        acc = jnp.where(me_ref[0] == 0, own_ref[...], got_ref[0])
        for d in range(1, N_DEV):
            acc = acc + jnp.where(me_ref[0] == d, own_ref[...], got_ref[d])
        o_ref[...] = acc

    return pl.pallas_call(
        body, name=name,
        grid_spec=pltpu.PrefetchScalarGridSpec(
            num_scalar_prefetch=1, grid=(1,),
            in_specs=[pl.BlockSpec(got.shape, lambda i, me_ref: (0, 0, 0)), pl.BlockSpec(own.shape, lambda i, me_ref: (0, 0))],
            out_specs=pl.BlockSpec(own.shape, lambda i, me_ref: (0, 0))),
        out_shape=jax.ShapeDtypeStruct(own.shape, F32),
    )(me.astype(jnp.int32).reshape(1), got, own)


def _sibling_plan(srcs, lands):
    x, y, c, _ = _place()
    return [(s.at[2 * ch + 1 - c], l.at[ch], (x, y, 1 - c)) for s, l in zip(srcs, lands) for ch in range(4)]


def _chip_plan(srcs, lands):
    x, y, c, chips = _place()
    return [(s.at[2 * px + py], l.at[2 * x + y], (px, py, c)) for s, l in zip(srcs, lands) for px, py in chips]


def _put_own(shard, zone, me, name):
    r, c = shard.shape
    tr = r if r <= 256 else _pick(r, (256, 64))

    def body(me_ref, s_ref, z_ref, o_ref):
        o_ref[...] = s_ref[...]

    return pl.pallas_call(
        body, name=name,
        grid_spec=pltpu.PrefetchScalarGridSpec(
            num_scalar_prefetch=1, grid=(r // tr,),
            in_specs=[pl.BlockSpec((tr, c), lambda i, me_ref: (i, 0)), ANY],
            out_specs=pl.BlockSpec((None, tr, c), lambda i, me_ref: (me_ref[0], i, 0))),
        out_shape=jax.ShapeDtypeStruct(zone.shape, zone.dtype), input_output_aliases={2: 0},
        compiler_params=_params(("arbitrary",)),
    )(me.astype(jnp.int32).reshape(1), shard, zone)


def _gather_relay(lands, name):
    n = len(lands)

    def body(*refs):
        zones, outs = refs[:n], refs[n:2 * n]
        send_sems, recv_sems = refs[2 * n:]
        x, y, c, _ = _place()
        south = c == 0
        near_x, near_y, across = 4 * (1 - x) + 2 * y + c, 4 * x + 2 * (1 - y) + c, 4 * (1 - x) + 2 * (1 - y) + c
        passed = jnp.where(south, near_y, near_x)
        onward = (jnp.where(south, 1 - x, x), jnp.where(south, y, 1 - y), c)

        def copy(a, k, slot, to):
            return pltpu.make_async_remote_copy(src_ref=zones[a].at[slot], dst_ref=outs[a].at[slot], send_sem=send_sems.at[a, k],
                                                recv_sem=recv_sems.at[a, k], device_id=to, device_id_type=MESH)

        first = [copy(a, 0, passed, onward) for a in range(n)]
        first += [copy(a, 1 + j, slot, (x, y, 1 - c)) for j, slot in enumerate((near_x, near_y)) for a in range(n)]
        for cp in first:
            cp.start()
        last = []
        for a in range(n):
            copy(a, 0, across, onward).wait_recv()
            last.append(copy(a, 3, across, (x, y, 1 - c)))
            last[-1].start()
        for cp in first[n:] + last:
            cp.wait_recv()
        for cp in first + last:
            cp.wait_send()

    return pl.pallas_call(
        body, name=name, in_specs=[ANY] * n, out_specs=[ANY] * n,
        out_shape=[jax.ShapeDtypeStruct(l.shape, l.dtype) for l in lands],
        input_output_aliases={a: a for a in range(n)},
        scratch_shapes=[pltpu.SemaphoreType.DMA((n, 4)), pltpu.SemaphoreType.DMA((n, 4))],
    )(*lands)


def _adamw_parts(w, got, mine, me, m, v, name, after=()):
    r, c = w.shape
    n_parts = got.shape[0]
    steps, blk, at = _tiles(r, c)

    def body(me_ref, w_ref, got_ref, own_ref, m_ref, v_ref, *rest):
        go_ref, d_ref, mo_ref, vo_ref = rest[len(after):]
        own = own_ref[...].astype(F32)
        gv = jnp.where(me_ref[0] == 0, own, got_ref[0].astype(F32))
        for d in range(1, n_parts):
            gv = gv + jnp.where(me_ref[0] == d, own, got_ref[d].astype(F32))
        _adamw_math(gv, w_ref, m_ref, v_ref, go_ref, d_ref, mo_ref, vo_ref)

    tile = pl.BlockSpec(blk, lambda i, me_ref: at(i))
    out = jax.ShapeDtypeStruct((r, c), F32)
    return pl.pallas_call(
        body, name=name,
        grid_spec=pltpu.PrefetchScalarGridSpec(
            num_scalar_prefetch=1, grid=(steps,),
            in_specs=[tile, pl.BlockSpec((n_parts,) + blk, lambda i, me_ref: (0,) + at(i)),
                      pl.BlockSpec((None,) + blk, lambda i, me_ref: (me_ref[0],) + at(i)), tile, tile] + [ANY] * len(after),
            out_specs=[tile] * 4),
        out_shape=[out] * 4, compiler_params=_params(("parallel",)),
    )(me.astype(jnp.int32).reshape(1), w, got, mine, m, v, *after)


def _adamw_math(gv, w_ref, m_ref, v_ref, go_ref, d_ref, mo_ref, vo_ref):
    mn = B1 * m_ref[...] + (1.0 - B1) * gv
    vn = B2 * v_ref[...] + (1.0 - B2) * (gv * gv)
    m_hat = mn / (1.0 - B1 ** STEP)
    v_hat = vn / (1.0 - B2 ** STEP)
    go_ref[...] = gv
    d_ref[...] = -LR * (m_hat / (jnp.sqrt(v_hat) + ADAM_EPS) + WD * w_ref[...])
    mo_ref[...] = mn
    vo_ref[...] = vn


def _adamw(w, g, m, v, name):
    def body(w_ref, g_ref, m_ref, v_ref, go_ref, d_ref, mo_ref, vo_ref):
        _adamw_math(g_ref[...], w_ref, m_ref, v_ref, go_ref, d_ref, mo_ref, vo_ref)

    tile = pl.BlockSpec(w.shape, lambda i: (0, 0))
    out = jax.ShapeDtypeStruct(w.shape, F32)
    return pl.pallas_call(
        body, name=name, grid=(1,), in_specs=[tile] * 4, out_specs=[tile] * 4, out_shape=[out] * 4,
        compiler_params=_params(("parallel",)),
    )(w, g, m, v)


BIG = ["w_in", "w_a_out", "w_b_out", "w_mix_out", "w_ff_gate", "w_ff_up", "w_ff_down", "w_ple_gate", "w_ple_proj"]
TRANSPOSED = ["w_in", "w_ff_gate", "w_ff_up"]
GRAD_OF = dict(w_ple_proj="pp", w_ple_gate="pg", w_ff_down="down", w_ff_gate="gate_t", w_ff_up="up_t", w_mix_out="mix", w_a_out="a_out",
               w_b_out="b_out")
SMALL = ["conv_w", "w_alpha_up", "b_alpha_up", "gla_head_gain", "g_pre_mix", "g_post_mix", "g_pre_ffn", "g_post_ffn", "g_pre_ple", "g_post_ple"]
WEIGHTS = ["w_in", "conv_w", "w_a_out", "w_alpha_up", "b_alpha_up", "gla_head_gain", "w_b_out", "w_mix_out", "g_pre_mix", "g_post_mix",
           "g_pre_ffn", "g_post_ffn", "w_ff_gate", "w_ff_up", "w_ff_down", "g_pre_ple", "g_post_ple", "w_ple_gate", "w_ple_proj"]


def _in_t_from_blocks(z):
    w = z.reshape(-1, z.shape[-1])
    return w[R_GA:R_END], jnp.concatenate([w[:R_GA], jnp.zeros((REST - R_GA, w.shape[1]), w.dtype)], axis=0)


def _blocks_from_in_t(g_gates, g_rest):
    per = R_END // N_DEV

    def rows(lo, hi):
        out = []
        for n0, n1, g in ((0, R_GA, g_rest), (R_GA, R_END, g_gates)):
            a, e = max(lo, n0), min(hi, n1)
            if a < e:
                out.append(g[a - n0:e - n0])
        return out

    return jnp.stack([jnp.concatenate(rows(b * per, (b + 1) * per), axis=0) for b in range(N_DEV)])


def _cols_to_full(g8):
    n, r, c = g8.shape
    return jnp.transpose(g8, (1, 0, 2)).reshape(r, n * c)


def _full_to_cols(a):
    r, c = a.shape
    return jnp.transpose(a.reshape(r, N_DEV, c // N_DEV), (1, 0, 2))


def _pack(arrs, rows):
    flat = jnp.concatenate([a.reshape(-1) for a in arrs])
    return jnp.pad(flat, (0, rows * 128 - flat.shape[0])).reshape(rows, 128)


def _unpack(packed, shapes):
    flat, out, o = packed.reshape(-1), [], 0
    for s in shapes:
        size = 1
        for d in s:
            size *= d
        out.append(flat[o:o + size].reshape(s))
        o += size
    return out


def kernel(x, p, w_in, conv_w, w_a_out, w_alpha_up, b_alpha_up, gla_head_gain, w_b_out, w_mix_out, g_pre_mix, g_post_mix, g_pre_ffn, g_post_ffn, w_ff_gate, w_ff_up, w_ff_down, g_pre_ple, g_post_ple, w_ple_gate, w_ple_proj, loss_target, m_w_in, m_conv_w, m_w_a_out, m_w_alpha_up, m_b_alpha_up, m_gla_head_gain, m_w_b_out, m_w_mix_out, m_g_pre_mix, m_g_post_mix, m_g_pre_ffn, m_g_post_ffn, m_w_ff_gate, m_w_ff_up, m_w_ff_down, m_g_pre_ple, m_g_post_ple, m_w_ple_gate, m_w_ple_proj, v_w_in, v_conv_w, v_w_a_out, v_w_alpha_up, v_b_alpha_up, v_gla_head_gain, v_w_b_out, v_w_mix_out, v_g_pre_mix, v_g_post_mix, v_g_pre_ffn, v_g_post_ffn, v_w_ff_gate, v_w_ff_up, v_w_ff_down, v_g_pre_ple, v_g_post_ple, v_w_ple_gate, v_w_ple_proj):
    args = dict(locals())
    shard = lambda n, a: jnp.transpose(a[0]) if n in TRANSPOSED else a[0]
    wts = {n: shard(n, args[n]) for n in WEIGHTS}
    mom = {n: shard(n, args["m_" + n]) for n in WEIGHTS}
    var = {n: shard(n, args["v_" + n]) for n in WEIGHTS}
    me =4 * lax.axis_index("x") + 2 * lax.axis_index("y") + lax.axis_index("c")

    groups = [["w_in", "conv_w", "w_alpha_up"], ["w_a_out", "w_b_out", "w_mix_out"], ["w_ff_gate", "w_ff_up"], ["w_ff_down"],
              ["w_ple_gate", "w_ple_proj"]]
    grad_groups = []
    rows_full = lambda g: g.reshape(-1, g.shape[-1])
    gathers, scatters = {}, {}

    def gather_start(gi, after):
        if gi not in gathers:
            shards = [wts[n].astype(BF16) if n in BIG else wts[n] for n in groups[gi]]
            zones = [jax.ShapeDtypeStruct((N_DEV,) + s.shape, s.dtype) for s in shards]
            plan, peers = (_gather_plan_near, 3) if gi == 0 else (_gather_plan, 4)
            gathers[gi] = (shards, _copies_start(plan, peers * len(shards), shards, zones, "gather_start_%d" % gi, after))
        return gathers[gi][1][-1]

    def gather_pass_on(gi, after):
        shards, state = gathers[gi]
        gathers[gi] = (shards, _copies_relay(_gather_plan, state, len(shards), _pass_on_plan, 3 * len(shards), "gather_pass_on_%d" % gi, after))
        return gathers[gi][1][-1]

    def gather_finish(gi, after):
        shards, state = gathers[gi]
        shards, zones = _copies_wait(_gather_plan_near if gi == 0 else _pass_on_plan, state, len(shards), "gather_wait_%d" % gi, after)
        if gi == 0:
            zones = _gather_relay(zones, "gather_relay_%d" % gi)
            gather_start(1, (zones[0],))
        g8 = {n: _put_own(s, z, me, "gather_own_" + n) for n, s, z in zip(groups[gi], shards, zones)}
        if gi == 0:
            in_gates_t, in_rest_t = _in_t_from_blocks(g8["w_in"])
            return dict(in_gates_t=in_gates_t, in_rest_t=in_rest_t,
                        conv_w=_cols_to_full(g8["conv_w"]), w_alpha_up=_cols_to_full(g8["w_alpha_up"]))
        if gi == 1:
            return dict(a_out=_cols_to_full(g8["w_a_out"]), b_out=_cols_to_full(g8["w_b_out"]), mix=rows_full(g8["w_mix_out"]))
        if gi == 2:
            return dict(gate_t=rows_full(g8["w_ff_gate"]), up_t=rows_full(g8["w_ff_up"]))
        if gi == 3:
            return dict(down=rows_full(g8["w_ff_down"]))
        return dict(pg=rows_full(g8["w_ple_gate"]), pp=g8["w_ple_proj"])

    def scatter_start(names, gw):
        gi = len(grad_groups)
        grad_groups.append(names)
        full = {n: _blocks_from_in_t(gw["in_gates_t"], gw["in_rest_t"]) if n == "w_in" else gw[GRAD_OF[n]] for n in names}
        for n in names:
            if n in ("w_a_out", "w_b_out"):
                full[n] = _full_to_cols(full[n])
        parts = [full[n] if full[n].ndim == 3 else full[n].reshape(N_DEV, -1, full[n].shape[-1]) for n in names]
        if names == ["w_in"]:
            quarter = [jax.ShapeDtypeStruct((4,) + a.shape[1:], a.dtype) for a in parts]
            scatters[gi] = _copies_start(_sibling_plan, 4 * len(parts), parts, quarter, "scatter_sibling_start_%d" % gi)
        else:
            scatters[gi] = _copies_start(_scatter_plan, (N_DEV - 1) * len(parts), parts, parts, "scatter_start_%d" % gi)
        return scatters[gi][-1]

    def scatter_next(after):
        gi, names = len(grad_groups) - 1, grad_groups[-1]
        parts, from_sibling = _copies_wait(_sibling_plan, scatters[gi], len(names), "scatter_sibling_wait_%d" % gi, after)
        parts = [_pair_add(a, b, "scatter_add_%d_%s" % (gi, n)) for n, a, b in zip(names, parts, from_sibling)]
        scatters[gi] = _copies_start(_chip_plan, 3 * len(parts), parts, parts, "scatter_start_%d" % gi)
        return scatters[gi][-1]

    small = {n: wts[n].reshape(1, -1) for n in SMALL[2:]}

    loss_rows, grad_x, gs = _local_step(x[0], p[0, 0], loss_target[0], gather_start, gather_pass_on, gather_finish, scatter_start,
                                        scatter_next, small)
    gs["loss"] = jnp.sum(loss_rows).reshape(1, 1)

    small_shapes = [gs[n].shape for n in SMALL]
    gs_packed = _pack([gs[n] for n in SMALL + ["loss"]], 192)
    small_state = _copies_start(_everyone_plan, N_DEV - 1, [gs_packed], [jax.ShapeDtypeStruct((N_DEV,) + gs_packed.shape, F32)],
                                "small_start", (grad_x,))

    ffn = [gi for gi, names in enumerate(grad_groups) if names[0].startswith("w_ff")]
    order = [gi for gi in range(len(grad_groups)) if gi not in ffn] + ffn
    res, done = {}, (small_state[-1],)
    for gi in order:
        names = grad_groups[gi]
        plan, slot = (_chip_plan, me // 2) if names == ["w_in"] else (_scatter_plan, me)
        mine, got = _copies_wait(plan, scatters[gi], len(names), "scatter_wait_%d" % gi, done)
        for n, g, own in zip(names, got, mine):
            res[n] = _adamw_parts(wts[n], g, own, slot, mom[n], var[n], "adamw_" + n)
        done = tuple(res[n][1] for n in names)

    (gs_own,), (gs_got,) = _copies_wait(_everyone_plan, small_state, 1, "small_wait", done)
    gsum = dict(zip(SMALL + ["loss"], _unpack(_sum_parts(gs_got, gs_own, me, "small_sum"), small_shapes + [(1, 1)])))
    loss = gsum["loss"].reshape(())
    gsum["conv_w"] = lax.dynamic_index_in_dim(gsum["conv_w"].reshape(3, N_DEV, -1), me, axis=1, keepdims=False)
    gsum["w_alpha_up"] = lax.dynamic_index_in_dim(gsum["w_alpha_up"].reshape(GATE_RANK, N_DEV, -1), me, axis=1, keepdims=False)

    shard_shapes = [wts[n].shape for n in SMALL]
    packed = [_pack([d[n] for n in SMALL], 120) for d in (wts, gsum, mom, var)]
    outs = [_unpack(o, shard_shapes) for o in _adamw(*packed, "adamw_small")]
    for i, n in enumerate(SMALL):
        res[n] = [o[i] for o in outs]

    back = lambda n, a: (jnp.transpose(a) if n in TRANSPOSED else a)[None]
    return (loss, grad_x[None], *[back(n, res[n][i]) for i in range(4) for n in WEIGHTS])
```

```python
import functools

import jax
import jax.numpy as jnp
from jax import lax
from jax.experimental import pallas as pl
from jax.experimental.pallas import tpu as pltpu

F32, BF16 = jnp.float32, jnp.bfloat16
EPS = 1e-6
CHUNK = 64
STEP_CHUNKS = 2
STEP_ROWS = STEP_CHUNKS * CHUNK
HEADS, DK, DV = 4, 128, 256
GATE_RANK = 16
TAU = 16.0
LR, B1, B2, ADAM_EPS, WD, STEP = 0.001, 0.9, 0.999, 1e-08, 0.01, 10
N_DEV = 8
MESH = pl.DeviceIdType.MESH
VMEM_LIMIT = 56 * 1024 * 1024
ANY = pl.BlockSpec(memory_space=pl.ANY)

C_GA, C_GB = 0, 2048
C_AX, C_AB, C_AC, C_Q, C_K, C_V, C_OG, C_ALR = 0, 1024, 2048, 3072, 3584, 4096, 5120, 6144
REST = 6400
R_GA, R_END = 6160, 10256


def _params(sem):
    return pltpu.CompilerParams(dimension_semantics=sem, vmem_limit_bytes=VMEM_LIMIT)


def _pick(n, cands):
    for c in cands:
        if n % c == 0:
            return c
    return n


def _tiles(r, c):
    for tr in (128, 64):
        if r % tr == 0:
            return r // tr, (tr, c), lambda i: (i, 0)
    tc = _pick(c, (256, 128))
    return c // tc, (r, tc), lambda i: (0, i)


def _mm(a, b, mode, out_dtype, name, after=(), add=None, b3=False, out3=False, tm=None, tk=None, n_outer=False):
    bshape = (b.shape[1], N_DEV * b.shape[2]) if b3 else b.shape
    if mode == "nn":
        (m, k), (k2, n) = a.shape, bshape
    elif mode == "nt":
        (m, k), (n, k2) = a.shape, bshape
    else:
        (k, m), (k2, n) = a.shape, bshape
    assert k == k2 and a.dtype == BF16 and b.dtype == BF16, (name, a.shape, b.shape, a.dtype, b.dtype)
    tm = tm if tm and m % tm == 0 else _pick(m, (2048, 1024, 512, 256))
    tn = _pick(n, (1152, 1024, 1280, 1408, 512, 256))
    tk = min(tk, k) if tk and k % min(tk, k) == 0 else _pick(k, (2048, 1408, 1152, 1024, 896, 512, 256))
    if out3 or (b3 and mode == "nn"):
        tn = n // N_DEV
    if b3 and mode == "nt":
        tk = k // N_DEV
    nk = k // tk
    dims = {"nn": (((1,), (0,)), ((), ())), "nt": (((1,), (1,)), ((), ())), "tn": (((0,), (0,)), ((), ()))}[mode]
    n_extra = len(after) + (add is not None)

    def body(a_ref, b_ref, *rest):
        o_ref = rest[n_extra]
        prod = lax.dot_general(a_ref[...], b_ref[...], dims, preferred_element_type=F32)
        if nk == 1:
            o_ref[...] = (prod if add is None else prod + rest[0][...]).astype(o_ref.dtype)
            return
        acc_ref = rest[n_extra + 1]
        kk = pl.program_id(2)

        @pl.when(kk == 0)
        def _():
            acc_ref[...] = prod if add is None else prod + rest[0][...]

        @pl.when((kk > 0) & (kk < nk - 1))
        def _():
            acc_ref[...] += prod

        @pl.when(kk == nk - 1)
        def _():
            o_ref[...] = (acc_ref[...] + prod).astype(o_ref.dtype)

    def spec(block, at):
        return pl.BlockSpec(block, (lambda g0, g1, kk: at(g1, g0, kk)) if n_outer else at)

    a_spec = spec((tk, tm), lambda i, j, kk: (kk, i)) if mode == "tn" else spec((tm, tk), lambda i, j, kk: (i, kk))
    if b3:
        b_spec = spec((None, tn, tk), lambda i, j, kk: (kk, j, 0)) if mode == "nt" else spec((None, tk, tn), lambda i, j, kk: (j, kk, 0))
    else:
        b_spec = spec((tn, tk), lambda i, j, kk: (j, kk)) if mode == "nt" else spec((tk, tn), lambda i, j, kk: (kk, j))
    tile = spec((tm, tn), lambda i, j, kk: (i, j))
    out_spec = spec((None, tm, tn), lambda i, j, kk: (j, i, 0)) if out3 else tile
    return pl.pallas_call(
        body, name=name, grid=(n // tn, m // tm, nk) if n_outer else (m // tm, n // tn, nk),
        in_specs=[a_spec, b_spec] + ([tile] if add is not None else []) + [ANY] * len(after), out_specs=out_spec,
        out_shape=jax.ShapeDtypeStruct((N_DEV, m, tn) if out3 else (m, n), out_dtype),
        scratch_shapes=[pltpu.VMEM((tm, tn), F32)] if nk > 1 else [],
        compiler_params=_params(("parallel", "parallel", "arbitrary")),
    )(a, b, *([add] if add is not None else []), *after)


LONG_K = dict(tm=512, tk=8192, n_outer=True)


def _rows(body, t, tr, ins, outs, name, after=()):
    in_specs = []
    for arr, sp in ins:
        if sp[0] == "t":
            in_specs.append(pl.BlockSpec((tr, sp[1]), lambda i, cb=sp[2]: (i, cb)))
        else:
            in_specs.append(pl.BlockSpec(arr.shape, lambda i, nd=arr.ndim: (0,) * nd))
    out_specs, out_shape = [], []
    for shape, dt, kind in outs:
        out_specs.append(pl.BlockSpec((tr, shape[1]), lambda i: (i, 0)) if kind == "t" else pl.BlockSpec(shape, lambda i: (0, 0)))
        out_shape.append(jax.ShapeDtypeStruct(shape, dt))
    return pl.pallas_call(
        body, name=name, grid=(t // tr,), in_specs=in_specs + [ANY] * len(after), out_specs=out_specs, out_shape=out_shape,
        compiler_params=_params(("arbitrary",)),
    )(*[arr for arr, _ in ins], *after)


def _rinv(v):
    return lax.rsqrt(jnp.mean(v * v, axis=-1, keepdims=True) + EPS)


def _sig(v):
    return 1.0 / (1.0 + jnp.exp(-v))


def _acc(ref, val):
    @pl.when(pl.program_id(0) == 0)
    def _():
        ref[...] = jnp.zeros_like(ref)

    ref[...] += jnp.sum(val, axis=0, keepdims=True)


def _rms_fwd(x, g, name):
    t, d = x.shape

    def body(x_ref, g_ref, h_ref):
        xv = x_ref[...]
        h_ref[...] = (xv * _rinv(xv) * g_ref[...]).astype(BF16)

    return _rows(body, t, 256, [(x, ("t", d, 0)), (g, ("b",))], [((t, d), BF16, "t")], name)[0]


def _post_pre(x, m, g_post, g_pre, name, after=()):
    t, d = x.shape

    def body(x_ref, m_ref, gp_ref, gn_ref, *rest):
        xo_ref, h_ref = rest[len(after):]
        mv = m_ref[...]
        xn = x_ref[...] + mv * _rinv(mv) * gp_ref[...]
        xo_ref[...] = xn
        h_ref[...] = (xn * _rinv(xn) * gn_ref[...]).astype(BF16)

    return _rows(body, t, 256, [(x, ("t", d, 0)), (m, ("t", d, 0)), (g_post, ("b",)), (g_pre, ("b",))],
                 [((t, d), F32, "t"), ((t, d), BF16, "t")], name, after)


def _mix_fwd(proj, ya, yb, name):
    t, d = ya.shape

    def body(ga_ref, gb_ref, ya_ref, yb_ref, o_ref):
        o_ref[...] = (_sig(ga_ref[...].astype(F32)) * ya_ref[...].astype(F32)
                      + _sig(gb_ref[...].astype(F32)) * yb_ref[...].astype(F32)).astype(BF16)

    return _rows(body, t, 256, [(proj, ("t", d, C_GA // d)), (proj, ("t", d, C_GB // d)), (ya, ("t", d, 0)), (yb, ("t", d, 0))],
                 [((t, d), BF16, "t")], name)[0]


def _mix_bwd(dmix, proj, ya, yb, name):
    t, d = ya.shape

    def body(dm_ref, ga_ref, gb_ref, ya_ref, yb_ref, dg_ref, dya_ref, dyb_ref):
        dm = dm_ref[...]
        sa, sb = _sig(ga_ref[...].astype(F32)), _sig(gb_ref[...].astype(F32))
        dg_ref[:, :d] = (dm * ya_ref[...].astype(F32) * sa * (1.0 - sa)).astype(BF16)
        dg_ref[:, d:] = (dm * yb_ref[...].astype(F32) * sb * (1.0 - sb)).astype(BF16)
        dya_ref[...] = (dm * sa).astype(BF16)
        dyb_ref[...] = (dm * sb).astype(BF16)

    return _rows(body, t, 256,
                 [(dmix, ("t", d, 0)), (proj, ("t", d, C_GA // d)), (proj, ("t", d, C_GB // d)), (ya, ("t", d, 0)), (yb, ("t", d, 0))],
                 [((t, 2 * d), BF16, "t"), ((t, d), BF16, "t"), ((t, d), BF16, "t")], name)


def _swiglu_call(body, ins, n_out, name):
    t, f = ins[0].shape
    tc = _pick(f, (1408, 512))
    tile = pl.BlockSpec((512, tc), lambda i, j: (i, j))
    return pl.pallas_call(
        body, name=name, grid=(t // 512, f // tc), in_specs=[tile] * len(ins), out_specs=[tile] * n_out,
        out_shape=[jax.ShapeDtypeStruct((t, f), BF16)] * n_out, compiler_params=_params(("parallel", "parallel")),
    )(*ins)


def _swiglu_fwd(fg, fu, name):
    def body(g_ref, u_ref, s_ref):
        gv = g_ref[...].astype(F32)
        s_ref[...] = (gv * _sig(gv) * u_ref[...].astype(F32)).astype(BF16)

    return _swiglu_call(body, [fg, fu], 1, name)[0]


def _swiglu_bwd(ds, fg, fu, name):
    def body(ds_ref, g_ref, u_ref, dg_ref, du_ref):
        dsv, gv, uv = ds_ref[...].astype(F32), g_ref[...].astype(F32), u_ref[...].astype(F32)
        sg = _sig(gv)
        dg_ref[...] = (dsv * uv * sg * (1.0 + gv * (1.0 - sg))).astype(BF16)
        du_ref[...] = (dsv * gv * sg).astype(BF16)

    return _swiglu_call(body, [ds, fg, fu], 2, name)


def _ple_final(x2, pg, pp, tgt, g_post, name):
    t, d = x2.shape

    def body(x_ref, pg_ref, pp_ref, t_ref, g_ref, loss_ref, d3_ref, dpg_ref, dpp_ref, dg_ref):
        sg, ppv, g = _sig(pg_ref[...]), pp_ref[...], g_ref[...]
        e = sg * ppv
        r = _rinv(e)
        eh = e * r
        diff = x_ref[...] + eh * g - t_ref[...]
        loss_ref[...] = 0.5 * jnp.mean(diff * diff, axis=-1, keepdims=True)
        d3 = diff * (1.0 / d)
        d3_ref[...] = d3
        gd = d3 * g
        de = r * (gd - eh * jnp.mean(gd * eh, axis=-1, keepdims=True))
        dpg_ref[...] = (de * ppv * sg * (1.0 - sg)).astype(BF16)
        dpp_ref[...] = (de * sg).astype(BF16)
        _acc(dg_ref, d3 * eh)

    return _rows(body, t, 256, [(x2, ("t", d, 0)), (pg, ("t", d, 0)), (pp, ("t", d, 0)), (tgt, ("t", d, 0)), (g_post, ("b",))],
                 [((t, 1), F32, "t"), ((t, d), F32, "t"), ((t, d), BF16, "t"), ((t, d), BF16, "t"), ((1, d), F32, "a")], name)


def _norm_bwd(dn, dh, x, g_pre, fm, g_post, name):
    t, d = x.shape
    two = fm is not None

    def body(*refs):
        if two:
            dn_ref, dh_ref, x_ref, gp_ref, f_ref, gq_ref, dx_ref, df_ref, dgp_ref, dgq_ref = refs
        else:
            dn_ref, dh_ref, x_ref, gp_ref, dx_ref, dgp_ref = refs
        xv, dhv = x_ref[...], dh_ref[...]
        r = _rinv(xv)
        xh = xv * r
        gd = dhv * gp_ref[...]
        dx = dn_ref[...] + r * (gd - xh * jnp.mean(gd * xh, axis=-1, keepdims=True))
        dx_ref[...] = dx
        _acc(dgp_ref, dhv * xh)
        if two:
            fv = f_ref[...]
            rf = _rinv(fv)
            fh = fv * rf
            gd2 = dx * gq_ref[...]
            df_ref[...] = (rf * (gd2 - fh * jnp.mean(gd2 * fh, axis=-1, keepdims=True))).astype(BF16)
            _acc(dgq_ref, dx * fh)

    ins = [(dn, ("t", d, 0)), (dh, ("t", d, 0)), (x, ("t", d, 0)), (g_pre, ("b",))]
    outs = [((t, d), F32, "t")]
    if two:
        ins += [(fm, ("t", d, 0)), (g_post, ("b",))]
        outs += [((t, d), BF16, "t"), ((1, d), F32, "a"), ((1, d), F32, "a")]
    else:
        outs += [((1, d), F32, "a")]
    return _rows(body, t, 256, ins, outs, name)


CONV_TC = 256


def _shift_down(v, s):
    rows = lax.broadcasted_iota(jnp.int32, v.shape, 0)
    return jnp.where(rows >= s, pltpu.roll(v, s, 0), 0.0)


def _shift_up(v, s):
    n = v.shape[0]
    rows = lax.broadcasted_iota(jnp.int32, v.shape, 0)
    return jnp.where(rows < n - s, pltpu.roll(v, n - s, 0), 0.0)


def _conv_specs(t):
    nb = 1024 // CONV_TC
    seg = lambda c0: pl.BlockSpec((t, CONV_TC), lambda j, cb=c0 // CONV_TC: (0, cb + j))
    own = pl.BlockSpec((t, CONV_TC), lambda j: (0, j))
    wspec = pl.BlockSpec((3, CONV_TC), lambda j: (0, j))
    return nb, seg, own, wspec


def _conv_fwd(proj, conv_w, name, after=()):
    t = proj.shape[0]
    nb, seg, own, wspec = _conv_specs(t)

    def body(ax_ref, ab_ref, ac_ref, w_ref, *rest):
        za_ref = rest[len(after)]
        u = ac_ref[...].astype(F32) * ax_ref[...].astype(F32)
        w = w_ref[...]
        yc = w[0:1] * _shift_down(u, 2) + w[1:2] * _shift_down(u, 1) + w[2:3] * u
        za_ref[...] = (ab_ref[...].astype(F32) * yc).astype(BF16)

    return pl.pallas_call(
        body, name=name, grid=(nb,), in_specs=[seg(C_AX), seg(C_AB), seg(C_AC), wspec] + [ANY] * len(after), out_specs=own,
        out_shape=jax.ShapeDtypeStruct((t, 1024), BF16), compiler_params=_params(("parallel",)),
    )(proj, proj, proj, conv_w, *after)


def _conv_bwd(dza, proj, conv_w, name):
    t = proj.shape[0]
    nb, seg, own, wspec = _conv_specs(t)

    def body(dz_ref, ax_ref, ab_ref, ac_ref, w_ref, dax_ref, dab_ref, dac_ref, dw_ref):
        ax, ab, ac, dz = ax_ref[...].astype(F32), ab_ref[...].astype(F32), ac_ref[...].astype(F32), dz_ref[...].astype(F32)
        w = w_ref[...]
        u = ac * ax
        u1, u2 = _shift_down(u, 1), _shift_down(u, 2)
        yc = w[0:1] * u2 + w[1:2] * u1 + w[2:3] * u
        dab_ref[...] = (dz * yc).astype(BF16)
        dyc = dz * ab
        du = w[2:3] * dyc + w[1:2] * _shift_up(dyc, 1) + w[0:1] * _shift_up(dyc, 2)
        dax_ref[...] = (du * ac).astype(BF16)
        dac_ref[...] = (du * ax).astype(BF16)
        dw_ref[0:1, :] = jnp.sum(dyc * u2, axis=0, keepdims=True)
        dw_ref[1:2, :] = jnp.sum(dyc * u1, axis=0, keepdims=True)
        dw_ref[2:3, :] = jnp.sum(dyc * u, axis=0, keepdims=True)

    act = jax.ShapeDtypeStruct((t, 1024), BF16)
    return pl.pallas_call(
        body, name=name, grid=(nb,), in_specs=[own, seg(C_AX), seg(C_AB), seg(C_AC), wspec], out_specs=[own, own, own, wspec],
        out_shape=[act, act, act, jax.ShapeDtypeStruct((3, 1024), F32)], compiler_params=_params(("parallel",)),
    )(dza, proj, proj, proj, conv_w)


def _dot(a, b, dims, precision=None):
    return lax.dot_general(a, b, (dims, ((), ())), precision=precision, preferred_element_type=F32)


_CONTRACT = {"nn": ((1,), (0,)), "nt": ((1,), (1,)), "tn": ((0,), (0,))}


def _bdot_raw(a, b, mode):
    return _dot(a.astype(BF16), b.astype(BF16), _CONTRACT[mode])


@functools.partial(jax.custom_vjp, nondiff_argnums=(2,))
def _bdot(a, b, mode):
    return _bdot_raw(a, b, mode)


def _bdot_fwd(a, b, mode):
    return _bdot_raw(a, b, mode), (a, b)


def _bdot_bwd(mode, res, ct):
    a, b = res
    if mode == "nn":
        return _bdot_raw(ct, b, "nt"), _bdot_raw(a, ct, "tn")
    if mode == "nt":
        return _bdot_raw(ct, b, "nn"), _bdot_raw(ct, a, "tn")
    return _bdot_raw(b, ct, "nt"), _bdot_raw(a, ct, "nn")


_bdot.defvjp(_bdot_fwd, _bdot_bwd)


@functools.partial(jax.custom_vjp, nondiff_argnums=(2,))
def _sum_dot(ones, x, mode):
    head = x.astype(BF16)
    tail = x - head.astype(F32)
    if mode == "nn":
        return _bdot_raw(ones, head, "nn") + _bdot_raw(ones, tail, "nn")
    return _bdot_raw(head, ones, "tn") + _bdot_raw(tail, ones, "tn")


def _sum_dot_fwd(ones, x, mode):
    return _sum_dot(ones, x, mode), ones


def _sum_dot_bwd(mode, ones, ct):
    return jnp.zeros_like(ones), (_bdot_raw(ones, ct, "tn") if mode == "nn" else _bdot_raw(ones, ct, "nt"))


_sum_dot.defvjp(_sum_dot_fwd, _sum_dot_bwd)


def _gla_chunk(q, k, v, og, alr, s_in, wa, ba, gain):
    c = q.shape[0]
    z =_bdot(alr, wa, "nn") + ba
    la = (jnp.minimum(z, 0.0) - jnp.log(1.0 + jnp.exp(-jnp.abs(z)))) * (1.0 / TAU)
    row = lax.broadcasted_iota(jnp.int32, (c, c), 0)
    col = lax.broadcasted_iota(jnp.int32, (c, c), 1)
    lower = row >= col
    b = _sum_dot(lower.astype(F32), la, "nn")
    trow = lax.broadcasted_iota(jnp.int32, la.shape, 0)
    mid = jnp.sum(jnp.where(trow <= c // 2, la, 0.0), axis=0, keepdims=True)
    blast = jnp.sum(la, axis=0, keepdims=True)
    qs = q * (DK ** -0.5)
    e_up, e_dn = jnp.exp(b - mid), jnp.exp(mid - b)
    a_fwd = _bdot(qs * e_up, k * e_dn, "nt")
    a_rev = _bdot(qs * e_dn, k * e_up, "nt")
    att = jnp.where(lower, a_fwd, a_rev)
    o = _bdot(att, v, "nn") + _bdot(qs * jnp.exp(b), s_in, "nn")
    upd = _bdot(k * jnp.exp(blast - b), v, "tn")
    blast_col = _sum_dot(jnp.ones((c, DV), F32), la, "tn")
    s_out = jnp.exp(blast_col) * s_in + upd
    on = o * _rinv(o) * gain
    return on * og * _sig(og), s_out


def _gla_specs(t, rev):
    n = t // STEP_ROWS
    ch = (lambda i: n - 1 - i) if rev else (lambda i: i)
    col = lambda w, c0: pl.BlockSpec((STEP_ROWS, HEADS * w), lambda i, cb=c0 // (HEADS * w): (ch(i), cb))
    whole = lambda shape: pl.BlockSpec(shape, lambda i, nd=len(shape): (0,) * nd)
    specs = dict(
        q=col(DK, C_Q), k=col(DK, C_K), v=col(DV, C_V), og=col(DV, C_OG),
        alr=pl.BlockSpec((STEP_ROWS, 128), lambda i: (ch(i), C_ALR // 128)),
        wa=whole((128, HEADS * DK)), ba=whole((1, HEADS * DK)), gain=whole((1, DV)),
        state=pl.BlockSpec((STEP_CHUNKS, HEADS, DK, DV), lambda i: (ch(i), 0, 0, 0)),
        odk=pl.BlockSpec((STEP_ROWS, HEADS * DK), lambda i: (ch(i), 0)), odv=pl.BlockSpec((STEP_ROWS, HEADS * DV), lambda i: (ch(i), 0)),
        oalr=pl.BlockSpec((STEP_ROWS, 128), lambda i: (ch(i), 0)), whole=whole,
    )
    return n, specs


def _head_cols(h):
    return slice(h * DK, (h + 1) * DK), slice(h * DV, (h + 1) * DV)


def _gla_fwd(proj, wa, ba, gain, name):
    t = proj.shape[0]
    n, sp = _gla_specs(t, False)

    def body(q_ref, k_ref, v_ref, og_ref, alr_ref, wa_ref, ba_ref, g_ref, zb_ref, st_ref, s_scr):
        @pl.when(pl.program_id(0) == 0)
        def _():
            s_scr[...] = jnp.zeros_like(s_scr)

        state = [s_scr[h] for h in range(HEADS)]
        for c in range(STEP_CHUNKS):
            rows = slice(c * CHUNK, (c + 1) * CHUNK)
            alr = alr_ref[rows, :].astype(F32)
            for h in range(HEADS):
                kc, vc = _head_cols(h)
                st_ref[c, h] = state[h]
                zb, state[h] = _gla_chunk(q_ref[rows, kc].astype(F32), k_ref[rows, kc].astype(F32), v_ref[rows, vc].astype(F32),
                                          og_ref[rows, vc].astype(F32), alr, state[h], wa_ref[:, kc].astype(F32), ba_ref[:, kc], g_ref[...])
                zb_ref[rows, vc] = zb.astype(BF16)
        for h in range(HEADS):
            s_scr[h] = state[h]

    return pl.pallas_call(
        body, name=name, grid=(n,),
        in_specs=[sp["q"], sp["k"], sp["v"], sp["og"], sp["alr"], sp["wa"], sp["ba"], sp["gain"]],
        out_specs=[sp["odv"], sp["state"]],
        out_shape=[jax.ShapeDtypeStruct((t, HEADS * DV), BF16), jax.ShapeDtypeStruct((t // CHUNK, HEADS, DK, DV), F32)],
        scratch_shapes=[pltpu.VMEM((HEADS, DK, DV), F32)],
        compiler_params=_params(("arbitrary",)),
    )(proj, proj, proj, proj, proj, wa, ba, gain)


def _gla_bwd(dzb, proj, states, wa, ba, gain, name):
    t = proj.shape[0]
    n, sp = _gla_specs(t, True)

    def body(dz_ref, q_ref, k_ref, v_ref, og_ref, alr_ref, st_ref, wa_ref, ba_ref, g_ref,
             dq_ref, dk_ref, dv_ref, dog_ref, dalr_ref, dwa_ref, dba_ref, dg_ref, ds_scr):
        @pl.when(pl.program_id(0) == 0)
        def _():
            ds_scr[...] = jnp.zeros_like(ds_scr)
            dwa_ref[...] = jnp.zeros_like(dwa_ref)
            dba_ref[...] = jnp.zeros_like(dba_ref)
            dg_ref[...] = jnp.zeros_like(dg_ref)

        dstate = [ds_scr[h] for h in range(HEADS)]
        dwa_sum, dba_sum, dgain_sum = [None] * HEADS, [None] * HEADS, None
        for c in reversed(range(STEP_CHUNKS)):
            rows = slice(c * CHUNK, (c + 1) * CHUNK)
            alr = alr_ref[rows, :].astype(F32)
            dalr_sum = None
            for h in range(HEADS):
                kc, vc = _head_cols(h)
                args = (q_ref[rows, kc].astype(F32), k_ref[rows, kc].astype(F32), v_ref[rows, vc].astype(F32), og_ref[rows, vc].astype(F32),
                        alr, st_ref[c, h], wa_ref[:, kc].astype(F32), ba_ref[:, kc], g_ref[...])
                _, vjp = jax.vjp(_gla_chunk, *args)
                dq, dk, dv, dog, dalr, dstate[h], dwa, dba, dgain = vjp((dz_ref[rows, vc].astype(F32), dstate[h]))
                dq_ref[rows, kc] = dq.astype(BF16)
                dk_ref[rows, kc] = dk.astype(BF16)
                dv_ref[rows, vc] = dv.astype(BF16)
                dog_ref[rows, vc] = dog.astype(BF16)
                dwa_sum[h] = dwa if dwa_sum[h] is None else dwa_sum[h] + dwa
                dba_sum[h] = dba if dba_sum[h] is None else dba_sum[h] + dba
                dalr_sum = dalr if dalr_sum is None else dalr_sum + dalr
                dgain_sum = dgain if dgain_sum is None else dgain_sum + dgain
            dalr_ref[rows, :] = dalr_sum
        for h in range(HEADS):
            ds_scr[h] = dstate[h]
            dwa_ref[h] += dwa_sum[h]
            dba_ref[h] += dba_sum[h]
        dg_ref[...] += dgain_sum

    whole = sp["whole"]
    return pl.pallas_call(
        body, name=name, grid=(n,),
        in_specs=[sp["odv"], sp["q"], sp["k"], sp["v"], sp["og"], sp["alr"], sp["state"], sp["wa"], sp["ba"], sp["gain"]],
        out_specs=[sp["odk"], sp["odk"], sp["odv"], sp["odv"], sp["oalr"], whole((HEADS, 128, DK)), whole((HEADS, 1, DK)), whole((1, DV))],
        out_shape=[jax.ShapeDtypeStruct((t, HEADS * DK), BF16), jax.ShapeDtypeStruct((t, HEADS * DK), BF16),
                   jax.ShapeDtypeStruct((t, HEADS * DV), BF16), jax.ShapeDtypeStruct((t, HEADS * DV), BF16),
                   jax.ShapeDtypeStruct((t, 128), F32), jax.ShapeDtypeStruct((HEADS, 128, DK), F32),
                   jax.ShapeDtypeStruct((HEADS, 1, DK), F32), jax.ShapeDtypeStruct((1, DV), F32)],
        scratch_shapes=[pltpu.VMEM((HEADS, DK, DV), F32)],
        compiler_params=_params(("arbitrary",)),
    )(dzb, proj, proj, proj, proj, proj, states, wa, ba, gain)


def _local_step(x, p, tgt, gather_start, gather_pass_on, gather_finish, scatter_start, scatter_next, small):
    b_alpha, gain = small["b_alpha_up"], small["gla_head_gain"]
    gather_start(0, ())
    w = dict(gather_finish(0, ()))
    conv_w, w_alpha = w["conv_w"], w["w_alpha_up"]
    wa_p = jnp.zeros((128, HEADS * DK), BF16).at[:GATE_RANK].set(w_alpha.astype(BF16))

    t2 = gather_start(2, (w["in_rest_t"], gather_start(1, ())))
    h1 = _rms_fwd(x, small["g_pre_mix"], "rms_pre_mix")
    proj = _mm(h1, w["in_rest_t"], "nt", BF16, "mm_proj", after=(t2,))
    proj_gates = _mm(h1, w["in_gates_t"], "nt", BF16, "mm_proj_gates", after=(t2,))
    za = _conv_fwd(proj, conv_w, "conv_fwd", after=(gather_pass_on(1, (proj,)),))
    zb, states = _gla_fwd(proj, wa_p, b_alpha, gain, "gla_fwd")
    t3 = gather_start(3, (zb, za))
    w.update(gather_finish(1, (t3,)))
    ya = _mm(za, w["a_out"], "nn", BF16, "mm_ya")
    yb = _mm(zb, w["b_out"], "nn", BF16, "mm_yb")
    mix = _mix_fwd(proj_gates, ya, yb, "mix_fwd")
    m2 = _mm(mix, w["mix"], "nn", F32, "mm_mix")
    t4 = gather_start(4, (m2,))
    x1, h2 = _post_pre(x, m2, small["g_post_mix"], small["g_pre_ffn"], "norm_mix_ffn", after=(gather_pass_on(2, (m2,)),))
    w.update(gather_finish(2, (h2, t4)))
    fu = _mm(h2, w["up_t"], "nt", BF16, "mm_up")
    fg = _mm(h2, w["gate_t"], "nt", BF16, "mm_gate", after=(gather_pass_on(3, (fu,)),))
    s = _swiglu_fwd(fg, fu, "swiglu_fwd")
    w.update(gather_finish(3, (s,)))
    f = _mm(s, w["down"], "nn", F32, "mm_down", after=(gather_pass_on(4, (s,)),), **LONG_K)
    x2, h3 = _post_pre(x1, f, small["g_post_ffn"], small["g_pre_ple"], "norm_ffn_ple")
    w.update(gather_finish(4, (h3,)))
    pg = _mm(h3, w["pg"], "nn", F32, "mm_pg")
    p_bf = p.astype(BF16)
    pp = _mm(p_bf, w["pp"], "nn", F32, "mm_pp", b3=True, tm=2048)
    loss_rows, d3, dpg, dpp, dg_post_ple = _ple_final(x2, pg, pp, tgt, small["g_post_ple"], "ple_final")

    gw = {}
    gw["pp"] = _mm(p_bf, dpp, "tn", BF16, "mm_dw_pp", out3=True)
    gw["pg"] = _mm(h3, dpg, "tn", BF16, "mm_dw_pg")
    dh3 = _mm(dpg, w["pg"], "nt", F32, "mm_dh3", after=(scatter_start(["w_ple_proj", "w_ple_gate"], gw),))
    d2, df, dg_pre_ple, dg_post_ffn = _norm_bwd(d3, dh3, x2, small["g_pre_ple"], f, small["g_post_ffn"], "norm_bwd_ple_ffn")
    gw["down"] = _mm(s, df, "tn", BF16, "mm_dw_down", tm=1408)
    ds = _mm(df, w["down"], "nt", BF16, "mm_ds", after=(scatter_start(["w_ff_down"], gw),))
    dfg, dfu = _swiglu_bwd(ds, fg, fu, "swiglu_bwd")
    gw["gate_t"] = _mm(dfg, h2, "tn", BF16, "mm_dw_gate", tm=1408)
    gw["up_t"] = _mm(dfu, h2, "tn", BF16, "mm_dw_up", after=(gw["gate_t"],), tm=1408)
    dh2 = _mm(dfg, w["gate_t"], "nn", F32, "mm_dh2_gate", after=(scatter_start(["w_ff_gate", "w_ff_up"], gw),), **LONG_K)
    dh2 = _mm(dfu, w["up_t"], "nn", F32, "mm_dh2_up", add=dh2, **LONG_K)
    d1, dm2, dg_pre_ffn, dg_post_mix = _norm_bwd(d2, dh2, x1, small["g_pre_ffn"], m2, small["g_post_mix"], "norm_bwd_ffn_mix")
    dmix = _mm(dm2, w["mix"], "nt", F32, "mm_dmix")
    gw["mix"] = _mm(mix, dm2, "tn", BF16, "mm_dw_mix")
    dgab, dya, dyb = _mix_bwd(dmix, proj_gates, ya, yb, "mix_bwd")
    gw["in_gates_t"] = _mm(dgab, h1, "tn", BF16, "mm_dw_in_gates")
    dza = _mm(dya, w["a_out"], "nt", BF16, "mm_dza", after=(scatter_start(["w_mix_out"], gw), gw["in_gates_t"]))
    gw["a_out"] = _mm(za, dya, "tn", BF16, "mm_dw_a_out")
    gw["b_out"] = _mm(zb, dyb, "tn", BF16, "mm_dw_b_out", after=(gw["a_out"],))
    dzb = _mm(dyb, w["b_out"], "nt", BF16, "mm_dzb", after=(scatter_start(["w_a_out", "w_b_out"], gw),))
    dax, dab, dac, dconv = _conv_bwd(dza, proj, conv_w, "conv_bwd")
    dq, dk, dv, dog, dalr, dwa, dba, dgain = _gla_bwd(dzb, proj, states, wa_p, b_alpha, gain, "gla_bwd")
    pad = jnp.zeros((dax.shape[0], REST - C_ALR - 128), BF16)
    drest = jnp.concatenate([dax, dab, dac, dq, dk, dv, dog, dalr.astype(BF16), pad], axis=1)
    gw["in_rest_t"] = _mm(drest, h1, "tn", BF16, "mm_dw_in_rest", tm=1280)
    dh1 = _mm(dgab, w["in_gates_t"], "nn", F32, "mm_dh1_gates", after=(scatter_start(["w_in"], gw),), tm=1024, tk=4096, n_outer=True)
    dh1 = _mm(drest, w["in_rest_t"], "nn", F32, "mm_dh1_rest", add=dh1, after=(scatter_next((dh1,)),), **LONG_K)
    grad_x, dg_pre_mix = _norm_bwd(d1, dh1, x, small["g_pre_mix"], None, None, "norm_bwd_mix")

    gs = dict(
        conv_w=dconv,
        w_alpha_up=jnp.transpose(dwa[:, :GATE_RANK, :], (1, 0, 2)).reshape(GATE_RANK, HEADS * DK),
        b_alpha_up=dba.reshape(1, HEADS * DK), gla_head_gain=dgain,
        g_pre_mix=dg_pre_mix, g_post_mix=dg_post_mix, g_pre_ffn=dg_pre_ffn, g_post_ffn=dg_post_ffn,
        g_pre_ple=dg_pre_ple, g_post_ple=dg_post_ple,
    )
    return loss_rows, grad_x, gs


def _place():
    x, y, c = lax.axis_index("x"), lax.axis_index("y"), lax.axis_index("c")
    return x, y, c, [(1 - x, y), (x, 1 - y), (1 - x, 1 - y)]


def _pair_add(mine8, got4, name):
    _, r, cols = mine8.shape
    steps, blk, at = _tiles(r, cols)
    core = lax.axis_index("c").astype(jnp.int32).reshape(1)

    def body(c_ref, a_ref, b_ref, o_ref):
        o_ref[...] = (a_ref[...].astype(F32) + b_ref[...].astype(F32)).astype(BF16)

    return pl.pallas_call(
        body, name=name,
        grid_spec=pltpu.PrefetchScalarGridSpec(
            num_scalar_prefetch=1, grid=(4, steps),
            in_specs=[pl.BlockSpec((None,) + blk, lambda ch, i, c_ref: (2 * ch + c_ref[0],) + at(i)),
                      pl.BlockSpec((None,) + blk, lambda ch, i, c_ref: (ch,) + at(i))],
            out_specs=pl.BlockSpec((None,) + blk, lambda ch, i, c_ref: (ch,) + at(i))),
        out_shape=jax.ShapeDtypeStruct((4, r, cols), BF16),
        compiler_params=_params(("parallel", "parallel")),
    )(core, mine8, got4)


HBM = pl.BlockSpec(memory_space=pltpu.HBM)
SEM = pl.BlockSpec(memory_space=pltpu.SEMAPHORE)
EFFECT = pltpu.SideEffectType.DATAFLOW_SIDE_EFFECTING


def _in_hbm(a):
    return pltpu.with_memory_space_constraint(a, pltpu.HBM)


def _remote_copies(plan, srcs, lands, send_sems, recv_sems):
    return [pltpu.make_async_remote_copy(src_ref=s, dst_ref=d, send_sem=send_sems.at[i], recv_sem=recv_sems.at[i], device_id=peer,
                                         device_id_type=MESH) for i, (s, d, peer) in enumerate(plan(srcs, lands))]


def _copies_start(plan, n_copies, srcs, land_shapes, name, after=()):
    ns, nl = len(srcs), len(land_shapes)

    def body(*refs):
        send_sems, recv_sems = refs[ns + nl + len(after):ns + nl + len(after) + 2]
        for cp in _remote_copies(plan, refs[:ns], refs[ns:ns + nl], send_sems, recv_sems):
            cp.start()
        refs[-1][...] = jnp.zeros((8, 128), F32)

    sems = pltpu.SemaphoreType.DMA((n_copies,))
    return pl.pallas_call(
        body, name=name,
        out_shape=(sems, sems, *[pltpu.HBM(s.shape, s.dtype) for s in srcs], *[pltpu.HBM(s.shape, s.dtype) for s in land_shapes],
                   jax.ShapeDtypeStruct((8, 128), F32)),
        in_specs=[HBM] * (ns + nl) + [ANY] * len(after),
        out_specs=(SEM, SEM, *[HBM] * (ns + nl), pl.BlockSpec(memory_space=pltpu.VMEM)),
        input_output_aliases={i: 2 + i for i in range(ns + nl)},
        compiler_params=pltpu.CompilerParams(has_side_effects=EFFECT),
    )(*[_in_hbm(s) for s in srcs], *[_in_hbm(lax.empty(s.shape, s.dtype)) for s in land_shapes], *after)


def _copies_wait(plan, state, ns, name, after=()):
    send_sems, recv_sems, *arrs = state[:-1]
    n = len(arrs)

    def body(*refs):
        cps = _remote_copies(plan, refs[:ns], refs[ns:n], refs[n], refs[n + 1])
        for cp in cps:
            cp.wait_send()
        for cp in cps:
            cp.wait_recv()

    out = pl.pallas_call(
        body, name=name, out_shape=tuple(pltpu.HBM(a.shape, a.dtype) for a in arrs),
        in_specs=[HBM] * n + [SEM, SEM] + [ANY] * len(after), out_specs=tuple([HBM] * n),
        input_output_aliases={i: i for i in range(n)},
        compiler_params=pltpu.CompilerParams(has_side_effects=EFFECT),
    )(*arrs, send_sems, recv_sems, *after)
    return list(out[:ns]), list(out[ns:])


def _copies_relay(plan, state, ns, next_plan, n_next, name, after=()):
    send_sems, recv_sems, *arrs = state[:-1]
    n = len(arrs)

    def body(*refs):
        cps = _remote_copies(plan, refs[:ns], refs[ns:n], refs[n], refs[n + 1])
        for cp in cps:
            cp.wait_send()
        for cp in cps:
            cp.wait_recv()
        outs = refs[n + 2 + len(after):]
        for cp in _remote_copies(next_plan, refs[:ns], refs[ns:n], outs[0], outs[1]):
            cp.start()
        outs[-1][...] = jnp.zeros((8, 128), F32)

    sems = pltpu.SemaphoreType.DMA((n_next,))
    return pl.pallas_call(
        body, name=name,
        out_shape=(sems, sems, *[pltpu.HBM(a.shape, a.dtype) for a in arrs], jax.ShapeDtypeStruct((8, 128), F32)),
        in_specs=[HBM] * n + [SEM, SEM] + [ANY] * len(after),
        out_specs=(SEM, SEM, *[HBM] * n, pl.BlockSpec(memory_space=pltpu.VMEM)),
        input_output_aliases={i: 2 + i for i in range(n)},
        compiler_params=pltpu.CompilerParams(has_side_effects=EFFECT),
    )(*arrs, send_sems, recv_sems, *after)


def _pass_on_plan(srcs, lands):
    x, y, c, chips = _place()
    return [(l.at[4 * px + 2 * py + c], l.at[4 * px + 2 * py + c], (x, y, 1 - c)) for l in lands for px, py in chips]


def _gather_plan(srcs, lands):
    x, y, c, chips = _place()
    peers = [(x, y, 1 - c)] + [(*chip, c) for chip in chips]
    return [(s, l.at[4 * x + 2 * y + c], peer) for s, l in zip(srcs, lands) for peer in peers]


def _gather_plan_near(srcs, lands):
    x, y, c, _ = _place()
    peers = [(x, y, 1 - c), (1 - x, y, c), (x, 1 - y, c)]
    return [(s, l.at[4 * x + 2 * y + c], peer) for s, l in zip(srcs, lands) for peer in peers]


def _scatter_plan(srcs, lands):
    x, y, c, _ = _place()
    peers = [(1 - x if k & 4 else x, 1 - y if k & 2 else y, 1 - c if k & 1 else c) for k in range(1, N_DEV)]
    return [(s.at[4 * px + 2 * py + pc], l.at[4 * x + 2 * y + c], (px, py, pc)) for s, l in zip(srcs, lands) for px, py, pc in peers]


def _everyone_plan(srcs, lands):
    x, y, c, _ = _place()
    peers = [(1 - x if k & 4 else x, 1 - y if k & 2 else y, 1 - c if k & 1 else c) for k in range(1, N_DEV)]
    return [(s, l.at[4 * x + 2 * y + c], peer) for s, l in zip(srcs, lands) for peer in peers]


def _sum_parts(got, own, me, name):
    def body(me_ref, got_ref, own_ref, o_ref):
        acc = jnp.where(me_ref[0] == 0, own_ref[...], got_ref[0])
        for d in range(1, N_DEV):
            acc = acc + jnp.where(me_ref[0] == d, own_ref[...], got_ref[d])
        o_ref[...] = acc

    return pl.pallas_call(
        body, name=name,
        grid_spec=pltpu.PrefetchScalarGridSpec(
            num_scalar_prefetch=1, grid=(1,),
            in_specs=[pl.BlockSpec(got.shape, lambda i, me_ref: (0, 0, 0)), pl.BlockSpec(own.shape, lambda i, me_ref: (0, 0))],
            out_specs=pl.BlockSpec(own.shape, lambda i, me_ref: (0, 0))),
        out_shape=jax.ShapeDtypeStruct(own.shape, F32),
    )(me.astype(jnp.int32).reshape(1), got, own)


def _sibling_plan(srcs, lands):
    x, y, c, _ = _place()
    return [(s.at[2 * ch + 1 - c], l.at[ch], (x, y, 1 - c)) for s, l in zip(srcs, lands) for ch in range(4)]


def _chip_plan(srcs, lands):
    x, y, c, chips = _place()
    return [(s.at[2 * px + py], l.at[2 * x + y], (px, py, c)) for s, l in zip(srcs, lands) for px, py in chips]


def _put_own(shard, zone, me, name):
    r, c = shard.shape
    tr = r if r <= 256 else _pick(r, (256, 64))

    def body(me_ref, s_ref, z_ref, o_ref):
        o_ref[...] = s_ref[...]

    return pl.pallas_call(
        body, name=name,
        grid_spec=pltpu.PrefetchScalarGridSpec(
            num_scalar_prefetch=1, grid=(r // tr,),
            in_specs=[pl.BlockSpec((tr, c), lambda i, me_ref: (i, 0)), ANY],
            out_specs=pl.BlockSpec((None, tr, c), lambda i, me_ref: (me_ref[0], i, 0))),
        out_shape=jax.ShapeDtypeStruct(zone.shape, zone.dtype), input_output_aliases={2: 0},
        compiler_params=_params(("arbitrary",)),
    )(me.astype(jnp.int32).reshape(1), shard, zone)


def _gather_relay(lands, name):
    n = len(lands)

    def body(*refs):
        zones, outs = refs[:n], refs[n:2 * n]
        send_sems, recv_sems = refs[2 * n:]
        x, y, c, _ = _place()
        south = c == 0
        near_x, near_y, across = 4 * (1 - x) + 2 * y + c, 4 * x + 2 * (1 - y) + c, 4 * (1 - x) + 2 * (1 - y) + c
        passed = jnp.where(south, near_y, near_x)
        onward = (jnp.where(south, 1 - x, x), jnp.where(south, y, 1 - y), c)

        def copy(a, k, slot, to):
            return pltpu.make_async_remote_copy(src_ref=zones[a].at[slot], dst_ref=outs[a].at[slot], send_sem=send_sems.at[a, k],
                                                recv_sem=recv_sems.at[a, k], device_id=to, device_id_type=MESH)

        first = [copy(a, 0, passed, onward) for a in range(n)]
        first += [copy(a, 1 + j, slot, (x, y, 1 - c)) for j, slot in enumerate((near_x, near_y)) for a in range(n)]
        for cp in first:
            cp.start()
        last = []
        for a in range(n):
            copy(a, 0, across, onward).wait_recv()
            last.append(copy(a, 3, across, (x, y, 1 - c)))
            last[-1].start()
        for cp in first[n:] + last:
            cp.wait_recv()
        for cp in first + last:
            cp.wait_send()

    return pl.pallas_call(
        body, name=name, in_specs=[ANY] * n, out_specs=[ANY] * n,
        out_shape=[jax.ShapeDtypeStruct(l.shape, l.dtype) for l in lands],
        input_output_aliases={a: a for a in range(n)},
        scratch_shapes=[pltpu.SemaphoreType.DMA((n, 4)), pltpu.SemaphoreType.DMA((n, 4))],
    )(*lands)


def _adamw_parts(w, got, mine, me, m, v, name, after=()):
    r, c = w.shape
    n_parts = got.shape[0]
    steps, blk, at = _tiles(r, c)

    def body(me_ref, w_ref, got_ref, own_ref, m_ref, v_ref, *rest):
        go_ref, d_ref, mo_ref, vo_ref = rest[len(after):]
        own = own_ref[...].astype(F32)
        gv = jnp.where(me_ref[0] == 0, own, got_ref[0].astype(F32))
        for d in range(1, n_parts):
            gv = gv + jnp.where(me_ref[0] == d, own, got_ref[d].astype(F32))
        _adamw_math(gv, w_ref, m_ref, v_ref, go_ref, d_ref, mo_ref, vo_ref)

    tile = pl.BlockSpec(blk, lambda i, me_ref: at(i))
    out = jax.ShapeDtypeStruct((r, c), F32)
    return pl.pallas_call(
        body, name=name,
        grid_spec=pltpu.PrefetchScalarGridSpec(
            num_scalar_prefetch=1, grid=(steps,),
            in_specs=[tile, pl.BlockSpec((n_parts,) + blk, lambda i, me_ref: (0,) + at(i)),
                      pl.BlockSpec((None,) + blk, lambda i, me_ref: (me_ref[0],) + at(i)), tile, tile] + [ANY] * len(after),
            out_specs=[tile] * 4),
        out_shape=[out] * 4, compiler_params=_params(("parallel",)),
    )(me.astype(jnp.int32).reshape(1), w, got, mine, m, v, *after)


def _adamw_math(gv, w_ref, m_ref, v_ref, go_ref, d_ref, mo_ref, vo_ref):
    mn = B1 * m_ref[...] + (1.0 - B1) * gv
    vn = B2 * v_ref[...] + (1.0 - B2) * (gv * gv)
    m_hat = mn / (1.0 - B1 ** STEP)
    v_hat = vn / (1.0 - B2 ** STEP)
    go_ref[...] = gv
    d_ref[...] = -LR * (m_hat / (jnp.sqrt(v_hat) + ADAM_EPS) + WD * w_ref[...])
    mo_ref[...] = mn
    vo_ref[...] = vn


def _adamw(w, g, m, v, name):
    def body(w_ref, g_ref, m_ref, v_ref, go_ref, d_ref, mo_ref, vo_ref):
        _adamw_math(g_ref[...], w_ref, m_ref, v_ref, go_ref, d_ref, mo_ref, vo_ref)

    tile = pl.BlockSpec(w.shape, lambda i: (0, 0))
    out = jax.ShapeDtypeStruct(w.shape, F32)
    return pl.pallas_call(
        body, name=name, grid=(1,), in_specs=[tile] * 4, out_specs=[tile] * 4, out_shape=[out] * 4,
        compiler_params=_params(("parallel",)),
    )(w, g, m, v)


BIG = ["w_in", "w_a_out", "w_b_out", "w_mix_out", "w_ff_gate", "w_ff_up", "w_ff_down", "w_ple_gate", "w_ple_proj"]
TRANSPOSED = ["w_in", "w_ff_gate", "w_ff_up"]
GRAD_OF = dict(w_ple_proj="pp", w_ple_gate="pg", w_ff_down="down", w_ff_gate="gate_t", w_ff_up="up_t", w_mix_out="mix", w_a_out="a_out",
               w_b_out="b_out")
SMALL = ["conv_w", "w_alpha_up", "b_alpha_up", "gla_head_gain", "g_pre_mix", "g_post_mix", "g_pre_ffn", "g_post_ffn", "g_pre_ple", "g_post_ple"]
WEIGHTS = ["w_in", "conv_w", "w_a_out", "w_alpha_up", "b_alpha_up", "gla_head_gain", "w_b_out", "w_mix_out", "g_pre_mix", "g_post_mix",
           "g_pre_ffn", "g_post_ffn", "w_ff_gate", "w_ff_up", "w_ff_down", "g_pre_ple", "g_post_ple", "w_ple_gate", "w_ple_proj"]


def _in_t_from_blocks(z):
    w = z.reshape(-1, z.shape[-1])
    return w[R_GA:R_END], jnp.concatenate([w[:R_GA], jnp.zeros((REST - R_GA, w.shape[1]), w.dtype)], axis=0)


def _blocks_from_in_t(g_gates, g_rest):
    per = R_END // N_DEV

    def rows(lo, hi):
        out = []
        for n0, n1, g in ((0, R_GA, g_rest), (R_GA, R_END, g_gates)):
            a, e = max(lo, n0), min(hi, n1)
            if a < e:
                out.append(g[a - n0:e - n0])
        return out

    return jnp.stack([jnp.concatenate(rows(b * per, (b + 1) * per), axis=0) for b in range(N_DEV)])


def _cols_to_full(g8):
    n, r, c = g8.shape
    return jnp.transpose(g8, (1, 0, 2)).reshape(r, n * c)


def _full_to_cols(a):
    r, c = a.shape
    return jnp.transpose(a.reshape(r, N_DEV, c // N_DEV), (1, 0, 2))


def _pack(arrs, rows):
    flat = jnp.concatenate([a.reshape(-1) for a in arrs])
    return jnp.pad(flat, (0, rows * 128 - flat.shape[0])).reshape(rows, 128)


def _unpack(packed, shapes):
    flat, out, o = packed.reshape(-1), [], 0
    for s in shapes:
        size = 1
        for d in s:
            size *= d
        out.append(flat[o:o + size].reshape(s))
        o += size
    return out


def kernel(x, p, w_in, conv_w, w_a_out, w_alpha_up, b_alpha_up, gla_head_gain, w_b_out, w_mix_out, g_pre_mix, g_post_mix, g_pre_ffn, g_post_ffn, w_ff_gate, w_ff_up, w_ff_down, g_pre_ple, g_post_ple, w_ple_gate, w_ple_proj, loss_target, m_w_in, m_conv_w, m_w_a_out, m_w_alpha_up, m_b_alpha_up, m_gla_head_gain, m_w_b_out, m_w_mix_out, m_g_pre_mix, m_g_post_mix, m_g_pre_ffn, m_g_post_ffn, m_w_ff_gate, m_w_ff_up, m_w_ff_down, m_g_pre_ple, m_g_post_ple, m_w_ple_gate, m_w_ple_proj, v_w_in, v_conv_w, v_w_a_out, v_w_alpha_up, v_b_alpha_up, v_gla_head_gain, v_w_b_out, v_w_mix_out, v_g_pre_mix, v_g_post_mix, v_g_pre_ffn, v_g_post_ffn, v_w_ff_gate, v_w_ff_up, v_w_ff_down, v_g_pre_ple, v_g_post_ple, v_w_ple_gate, v_w_ple_proj):
    args = dict(locals())
    shard = lambda n, a: jnp.transpose(a[0]) if n in TRANSPOSED else a[0]
    wts = {n: shard(n, args[n]) for n in WEIGHTS}
    mom = {n: shard(n, args["m_" + n]) for n in WEIGHTS}
    var = {n: shard(n, args["v_" + n]) for n in WEIGHTS}
    me =4 * lax.axis_index("x") + 2 * lax.axis_index("y") + lax.axis_index("c")

    groups = [["w_in", "conv_w", "w_alpha_up"], ["w_a_out", "w_b_out", "w_mix_out"], ["w_ff_gate", "w_ff_up"], ["w_ff_down"],
              ["w_ple_gate", "w_ple_proj"]]
    grad_groups = []
    rows_full = lambda g: g.reshape(-1, g.shape[-1])
    gathers, scatters = {}, {}

    def gather_start(gi, after):
        if gi not in gathers:
            shards = [wts[n].astype(BF16) if n in BIG else wts[n] for n in groups[gi]]
            zones = [jax.ShapeDtypeStruct((N_DEV,) + s.shape, s.dtype) for s in shards]
            plan, peers = (_gather_plan_near, 3) if gi == 0 else (_gather_plan, 4)
            gathers[gi] = (shards, _copies_start(plan, peers * len(shards), shards, zones, "gather_start_%d" % gi, after))
        return gathers[gi][1][-1]

    def gather_pass_on(gi, after):
        shards, state = gathers[gi]
        gathers[gi] = (shards, _copies_relay(_gather_plan, state, len(shards), _pass_on_plan, 3 * len(shards), "gather_pass_on_%d" % gi, after))
        return gathers[gi][1][-1]

    def gather_finish(gi, after):
        shards, state = gathers[gi]
        shards, zones = _copies_wait(_gather_plan_near if gi == 0 else _pass_on_plan, state, len(shards), "gather_wait_%d" % gi, after)
        if gi == 0:
            zones = _gather_relay(zones, "gather_relay_%d" % gi)
            gather_start(1, (zones[0],))
        g8 = {n: _put_own(s, z, me, "gather_own_" + n) for n, s, z in zip(groups[gi], shards, zones)}
        if gi == 0:
            in_gates_t, in_rest_t = _in_t_from_blocks(g8["w_in"])
            return dict(in_gates_t=in_gates_t, in_rest_t=in_rest_t,
                        conv_w=_cols_to_full(g8["conv_w"]), w_alpha_up=_cols_to_full(g8["w_alpha_up"]))
        if gi == 1:
            return dict(a_out=_cols_to_full(g8["w_a_out"]), b_out=_cols_to_full(g8["w_b_out"]), mix=rows_full(g8["w_mix_out"]))
        if gi == 2:
            return dict(gate_t=rows_full(g8["w_ff_gate"]), up_t=rows_full(g8["w_ff_up"]))
        if gi == 3:
            return dict(down=rows_full(g8["w_ff_down"]))
        return dict(pg=rows_full(g8["w_ple_gate"]), pp=g8["w_ple_proj"])

    def scatter_start(names, gw):
        gi = len(grad_groups)
        grad_groups.append(names)
        full = {n: _blocks_from_in_t(gw["in_gates_t"], gw["in_rest_t"]) if n == "w_in" else gw[GRAD_OF[n]] for n in names}
        for n in names:
            if n in ("w_a_out", "w_b_out"):
                full[n] = _full_to_cols(full[n])
        parts = [full[n] if full[n].ndim == 3 else full[n].reshape(N_DEV, -1, full[n].shape[-1]) for n in names]
        if names == ["w_in"]:
            quarter = [jax.ShapeDtypeStruct((4,) + a.shape[1:], a.dtype) for a in parts]
            scatters[gi] = _copies_start(_sibling_plan, 4 * len(parts), parts, quarter, "scatter_sibling_start_%d" % gi)
        else:
            scatters[gi] = _copies_start(_scatter_plan, (N_DEV - 1) * len(parts), parts, parts, "scatter_start_%d" % gi)
        return scatters[gi][-1]

    def scatter_next(after):
        gi, names = len(grad_groups) - 1, grad_groups[-1]
        parts, from_sibling = _copies_wait(_sibling_plan, scatters[gi], len(names), "scatter_sibling_wait_%d" % gi, after)
        parts = [_pair_add(a, b, "scatter_add_%d_%s" % (gi, n)) for n, a, b in zip(names, parts, from_sibling)]
        scatters[gi] = _copies_start(_chip_plan, 3 * len(parts), parts, parts, "scatter_start_%d" % gi)
        return scatters[gi][-1]

    small = {n: wts[n].reshape(1, -1) for n in SMALL[2:]}

    loss_rows, grad_x, gs = _local_step(x[0], p[0, 0], loss_target[0], gather_start, gather_pass_on, gather_finish, scatter_start,
                                        scatter_next, small)
    gs["loss"] = jnp.sum(loss_rows).reshape(1, 1)

    small_shapes = [gs[n].shape for n in SMALL]
    gs_packed = _pack([gs[n] for n in SMALL + ["loss"]], 192)
    small_state = _copies_start(_everyone_plan, N_DEV - 1, [gs_packed], [jax.ShapeDtypeStruct((N_DEV,) + gs_packed.shape, F32)],
                                "small_start", (grad_x,))

    res, done = {}, (small_state[-1],)
    for gi, names in enumerate(grad_groups):
        plan, slot = (_chip_plan, me // 2) if names == ["w_in"] else (_scatter_plan, me)
        mine, got = _copies_wait(plan, scatters[gi], len(names), "scatter_wait_%d" % gi, done)
        for n, g, own in zip(names, got, mine):
            res[n] = _adamw_parts(wts[n], g, own, slot, mom[n], var[n], "adamw_" + n)
        done = tuple(res[n][1] for n in names)

    (gs_own,), (gs_got,) = _copies_wait(_everyone_plan, small_state, 1, "small_wait", done)
    gsum = dict(zip(SMALL + ["loss"], _unpack(_sum_parts(gs_got, gs_own, me, "small_sum"), small_shapes + [(1, 1)])))
    loss = gsum["loss"].reshape(())
    gsum["conv_w"] = lax.dynamic_index_in_dim(gsum["conv_w"].reshape(3, N_DEV, -1), me, axis=1, keepdims=False)
    gsum["w_alpha_up"] = lax.dynamic_index_in_dim(gsum["w_alpha_up"].reshape(GATE_RANK, N_DEV, -1), me, axis=1, keepdims=False)

    shard_shapes = [wts[n].shape for n in SMALL]
    packed = [_pack([d[n] for n in SMALL], 120) for d in (wts, gsum, mom, var)]
    outs = [_unpack(o, shard_shapes) for o in _adamw(*packed, "adamw_small")]
    for i, n in enumerate(SMALL):
        res[n] = [o[i] for o in outs]

    back = lambda n, a: (jnp.transpose(a) if n in TRANSPOSED else a)[None]
    return (loss, grad_x[None], *[back(n, res[n][i]) for i in range(4) for n in WEIGHTS])
```

```python
import functools

import jax
import jax.numpy as jnp
from jax import lax
from jax.experimental import pallas as pl
from jax.experimental.pallas import tpu as pltpu

F32, BF16 = jnp.float32, jnp.bfloat16
EPS = 1e-6
CHUNK = 64
STEP_CHUNKS = 2
STEP_ROWS = STEP_CHUNKS * CHUNK
HEADS, DK, DV = 4, 128, 256
GATE_RANK = 16
TAU = 16.0
LR, B1, B2, ADAM_EPS, WD, STEP = 0.001, 0.9, 0.999, 1e-08, 0.01, 10
N_DEV = 8
MESH = pl.DeviceIdType.MESH
VMEM_LIMIT = 56 * 1024 * 1024
ANY = pl.BlockSpec(memory_space=pl.ANY)

C_GA, C_GB = 0, 2048
C_AX, C_AB, C_AC, C_Q, C_K, C_V, C_OG, C_ALR = 0, 1024, 2048, 3072, 3584, 4096, 5120, 6144
REST = 6400
R_GA, R_END = 6160, 10256


def _params(sem):
    return pltpu.CompilerParams(dimension_semantics=sem, vmem_limit_bytes=VMEM_LIMIT)


def _pick(n, cands):
    for c in cands:
        if n % c == 0:
            return c
    return n


def _tiles(r, c):
    for tr in (128, 64):
        if r % tr == 0:
            return r // tr, (tr, c), lambda i: (i, 0)
    tc = _pick(c, (256, 128))
    return c // tc, (r, tc), lambda i: (0, i)


def _mm(a, b, mode, out_dtype, name, after=(), add=None, b3=False, out3=False, tm=None, tk=None, tn=None, n_outer=False):
    bshape = (b.shape[1], N_DEV * b.shape[2]) if b3 else b.shape
    if mode == "nn":
        (m, k), (k2, n) = a.shape, bshape
    elif mode == "nt":
        (m, k), (n, k2) = a.shape, bshape
    else:
        (k, m), (k2, n) = a.shape, bshape
    assert k == k2 and a.dtype == BF16 and b.dtype == BF16, (name, a.shape, b.shape, a.dtype, b.dtype)
    tm = tm if tm and m % tm == 0 else _pick(m, (2048, 1024, 512, 256))
    tn = tn if tn and n % tn == 0 else _pick(n, (1152, 1024, 1280, 1408, 512, 256))
    tk = min(tk, k) if tk and k % min(tk, k) == 0 else _pick(k, (2048, 1408, 1152, 1024, 896, 512, 256))
    if out3 or (b3 and mode == "nn"):
        tn = n // N_DEV
    if b3 and mode == "nt":
        tk = k // N_DEV
    nk = k // tk
    dims = {"nn": (((1,), (0,)), ((), ())), "nt": (((1,), (1,)), ((), ())), "tn": (((0,), (0,)), ((), ()))}[mode]
    n_extra = len(after) + (add is not None)

    def body(a_ref, b_ref, *rest):
        o_ref = rest[n_extra]
        prod = lax.dot_general(a_ref[...], b_ref[...], dims, preferred_element_type=F32)
        if nk == 1:
            o_ref[...] = (prod if add is None else prod + rest[0][...]).astype(o_ref.dtype)
            return
        acc_ref = rest[n_extra + 1]
        kk = pl.program_id(2)

        @pl.when(kk == 0)
        def _():
            acc_ref[...] = prod if add is None else prod + rest[0][...]

        @pl.when((kk > 0) & (kk < nk - 1))
        def _():
            acc_ref[...] += prod

        @pl.when(kk == nk - 1)
        def _():
            o_ref[...] = (acc_ref[...] + prod).astype(o_ref.dtype)

    def spec(block, at):
        return pl.BlockSpec(block, (lambda g0, g1, kk: at(g1, g0, kk)) if n_outer else at)

    a_spec = spec((tk, tm), lambda i, j, kk: (kk, i)) if mode == "tn" else spec((tm, tk), lambda i, j, kk: (i, kk))
    if b3:
        b_spec = spec((None, tn, tk), lambda i, j, kk: (kk, j, 0)) if mode == "nt" else spec((None, tk, tn), lambda i, j, kk: (j, kk, 0))
    else:
        b_spec = spec((tn, tk), lambda i, j, kk: (j, kk)) if mode == "nt" else spec((tk, tn), lambda i, j, kk: (kk, j))
    tile = spec((tm, tn), lambda i, j, kk: (i, j))
    out_spec = spec((None, tm, tn), lambda i, j, kk: (j, i, 0)) if out3 else tile
    return pl.pallas_call(
        body, name=name, grid=(n // tn, m // tm, nk) if n_outer else (m // tm, n // tn, nk),
        in_specs=[a_spec, b_spec] + ([tile] if add is not None else []) + [ANY] * len(after), out_specs=out_spec,
        out_shape=jax.ShapeDtypeStruct((N_DEV, m, tn) if out3 else (m, n), out_dtype),
        scratch_shapes=[pltpu.VMEM((tm, tn), F32)] if nk > 1 else [],
        compiler_params=_params(("parallel", "parallel", "arbitrary")),
    )(a, b, *([add] if add is not None else []), *after)


LONG_K = dict(tm=512, tk=8192, n_outer=True)
WIDE_N = dict(tm=512, tn=2816, n_outer=True)


def _rows(body, t, tr, ins, outs, name, after=()):
    in_specs = []
    for arr, sp in ins:
        if sp[0] == "t":
            in_specs.append(pl.BlockSpec((tr, sp[1]), lambda i, cb=sp[2]: (i, cb)))
        else:
            in_specs.append(pl.BlockSpec(arr.shape, lambda i, nd=arr.ndim: (0,) * nd))
    out_specs, out_shape = [], []
    for shape, dt, kind in outs:
        out_specs.append(pl.BlockSpec((tr, shape[1]), lambda i: (i, 0)) if kind == "t" else pl.BlockSpec(shape, lambda i: (0, 0)))
        out_shape.append(jax.ShapeDtypeStruct(shape, dt))
    return pl.pallas_call(
        body, name=name, grid=(t // tr,), in_specs=in_specs + [ANY] * len(after), out_specs=out_specs, out_shape=out_shape,
        compiler_params=_params(("arbitrary",)),
    )(*[arr for arr, _ in ins], *after)


def _rinv(v):
    return lax.rsqrt(jnp.mean(v * v, axis=-1, keepdims=True) + EPS)


def _sig(v):
    return 1.0 / (1.0 + jnp.exp(-v))


def _acc(ref, val):
    @pl.when(pl.program_id(0) == 0)
    def _():
        ref[...] = jnp.zeros_like(ref)

    ref[...] += jnp.sum(val, axis=0, keepdims=True)


def _rms_fwd(x, g, name):
    t, d = x.shape

    def body(x_ref, g_ref, h_ref):
        xv = x_ref[...]
        h_ref[...] = (xv * _rinv(xv) * g_ref[...]).astype(BF16)

    return _rows(body, t, 256, [(x, ("t", d, 0)), (g, ("b",))], [((t, d), BF16, "t")], name)[0]


def _post_pre(x, m, g_post, g_pre, name, after=()):
    t, d = x.shape

    def body(x_ref, m_ref, gp_ref, gn_ref, *rest):
        xo_ref, h_ref = rest[len(after):]
        mv = m_ref[...]
        xn = x_ref[...] + mv * _rinv(mv) * gp_ref[...]
        xo_ref[...] = xn
        h_ref[...] = (xn * _rinv(xn) * gn_ref[...]).astype(BF16)

    return _rows(body, t, 256, [(x, ("t", d, 0)), (m, ("t", d, 0)), (g_post, ("b",)), (g_pre, ("b",))],
                 [((t, d), F32, "t"), ((t, d), BF16, "t")], name, after)


def _mix_fwd(proj, ya, yb, name):
    t, d = ya.shape

    def body(ga_ref, gb_ref, ya_ref, yb_ref, o_ref):
        o_ref[...] = (_sig(ga_ref[...].astype(F32)) * ya_ref[...].astype(F32)
                      + _sig(gb_ref[...].astype(F32)) * yb_ref[...].astype(F32)).astype(BF16)

    return _rows(body, t, 256, [(proj, ("t", d, C_GA // d)), (proj, ("t", d, C_GB // d)), (ya, ("t", d, 0)), (yb, ("t", d, 0))],
                 [((t, d), BF16, "t")], name)[0]


def _mix_bwd(dmix, proj, ya, yb, name):
    t, d = ya.shape

    def body(dm_ref, ga_ref, gb_ref, ya_ref, yb_ref, dg_ref, dya_ref, dyb_ref):
        dm = dm_ref[...]
        sa, sb = _sig(ga_ref[...].astype(F32)), _sig(gb_ref[...].astype(F32))
        dg_ref[:, :d] = (dm * ya_ref[...].astype(F32) * sa * (1.0 - sa)).astype(BF16)
        dg_ref[:, d:] = (dm * yb_ref[...].astype(F32) * sb * (1.0 - sb)).astype(BF16)
        dya_ref[...] = (dm * sa).astype(BF16)
        dyb_ref[...] = (dm * sb).astype(BF16)

    return _rows(body, t, 256,
                 [(dmix, ("t", d, 0)), (proj, ("t", d, C_GA // d)), (proj, ("t", d, C_GB // d)), (ya, ("t", d, 0)), (yb, ("t", d, 0))],
                 [((t, 2 * d), BF16, "t"), ((t, d), BF16, "t"), ((t, d), BF16, "t")], name)


def _swiglu_call(body, ins, n_out, name):
    t, f = ins[0].shape
    tc = _pick(f, (1408, 512))
    tile = pl.BlockSpec((512, tc), lambda i, j: (i, j))
    return pl.pallas_call(
        body, name=name, grid=(t // 512, f // tc), in_specs=[tile] * len(ins), out_specs=[tile] * n_out,
        out_shape=[jax.ShapeDtypeStruct((t, f), BF16)] * n_out, compiler_params=_params(("parallel", "parallel")),
    )(*ins)


def _swiglu_fwd(fg, fu, name):
    def body(g_ref, u_ref, s_ref):
        gv = g_ref[...].astype(F32)
        s_ref[...] = (gv * _sig(gv) * u_ref[...].astype(F32)).astype(BF16)

    return _swiglu_call(body, [fg, fu], 1, name)[0]


def _swiglu_bwd(ds, fg, fu, name):
    def body(ds_ref, g_ref, u_ref, dg_ref, du_ref):
        dsv, gv, uv = ds_ref[...].astype(F32), g_ref[...].astype(F32), u_ref[...].astype(F32)
        sg = _sig(gv)
        dg_ref[...] = (dsv * uv * sg * (1.0 + gv * (1.0 - sg))).astype(BF16)
        du_ref[...] = (dsv * gv * sg).astype(BF16)

    return _swiglu_call(body, [ds, fg, fu], 2, name)


def _ple_final(x2, pg, pp, tgt, g_post, name):
    t, d = x2.shape

    def body(x_ref, pg_ref, pp_ref, t_ref, g_ref, loss_ref, d3_ref, dpg_ref, dpp_ref, dg_ref):
        sg, ppv, g = _sig(pg_ref[...]), pp_ref[...], g_ref[...]
        e = sg * ppv
        r = _rinv(e)
        eh = e * r
        diff = x_ref[...] + eh * g - t_ref[...]
        loss_ref[...] = 0.5 * jnp.mean(diff * diff, axis=-1, keepdims=True)
        d3 = diff * (1.0 / d)
        d3_ref[...] = d3
        gd = d3 * g
        de = r * (gd - eh * jnp.mean(gd * eh, axis=-1, keepdims=True))
        dpg_ref[...] = (de * ppv * sg * (1.0 - sg)).astype(BF16)
        dpp_ref[...] = (de * sg).astype(BF16)
        _acc(dg_ref, d3 * eh)

    return _rows(body, t, 256, [(x2, ("t", d, 0)), (pg, ("t", d, 0)), (pp, ("t", d, 0)), (tgt, ("t", d, 0)), (g_post, ("b",))],
                 [((t, 1), F32, "t"), ((t, d), F32, "t"), ((t, d), BF16, "t"), ((t, d), BF16, "t"), ((1, d), F32, "a")], name)


def _norm_bwd(dn, dh, x, g_pre, fm, g_post, name):
    t, d = x.shape
    two = fm is not None

    def body(*refs):
        if two:
            dn_ref, dh_ref, x_ref, gp_ref, f_ref, gq_ref, dx_ref, df_ref, dgp_ref, dgq_ref = refs
        else:
            dn_ref, dh_ref, x_ref, gp_ref, dx_ref, dgp_ref = refs
        xv, dhv = x_ref[...], dh_ref[...]
        r = _rinv(xv)
        xh = xv * r
        gd = dhv * gp_ref[...]
        dx = dn_ref[...] + r * (gd - xh * jnp.mean(gd * xh, axis=-1, keepdims=True))
        dx_ref[...] = dx
        _acc(dgp_ref, dhv * xh)
        if two:
            fv = f_ref[...]
            rf = _rinv(fv)
            fh = fv * rf
            gd2 = dx * gq_ref[...]
            df_ref[...] = (rf * (gd2 - fh * jnp.mean(gd2 * fh, axis=-1, keepdims=True))).astype(BF16)
            _acc(dgq_ref, dx * fh)

    ins = [(dn, ("t", d, 0)), (dh, ("t", d, 0)), (x, ("t", d, 0)), (g_pre, ("b",))]
    outs = [((t, d), F32, "t")]
    if two:
        ins += [(fm, ("t", d, 0)), (g_post, ("b",))]
        outs += [((t, d), BF16, "t"), ((1, d), F32, "a"), ((1, d), F32, "a")]
    else:
        outs += [((1, d), F32, "a")]
    return _rows(body, t, 256, ins, outs, name)


CONV_TC = 256


def _shift_down(v, s):
    rows = lax.broadcasted_iota(jnp.int32, v.shape, 0)
    return jnp.where(rows >= s, pltpu.roll(v, s, 0), 0.0)


def _shift_up(v, s):
    n = v.shape[0]
    rows = lax.broadcasted_iota(jnp.int32, v.shape, 0)
    return jnp.where(rows < n - s, pltpu.roll(v, n - s, 0), 0.0)


def _conv_specs(t):
    nb = 1024 // CONV_TC
    seg = lambda c0: pl.BlockSpec((t, CONV_TC), lambda j, cb=c0 // CONV_TC: (0, cb + j))
    own = pl.BlockSpec((t, CONV_TC), lambda j: (0, j))
    wspec = pl.BlockSpec((3, CONV_TC), lambda j: (0, j))
    return nb, seg, own, wspec


def _conv_fwd(proj, conv_w, name, after=()):
    t = proj.shape[0]
    nb, seg, own, wspec = _conv_specs(t)

    def body(ax_ref, ab_ref, ac_ref, w_ref, *rest):
        za_ref = rest[len(after)]
        u = ac_ref[...].astype(F32) * ax_ref[...].astype(F32)
        w = w_ref[...]
        yc = w[0:1] * _shift_down(u, 2) + w[1:2] * _shift_down(u, 1) + w[2:3] * u
        za_ref[...] = (ab_ref[...].astype(F32) * yc).astype(BF16)

    return pl.pallas_call(
        body, name=name, grid=(nb,), in_specs=[seg(C_AX), seg(C_AB), seg(C_AC), wspec] + [ANY] * len(after), out_specs=own,
        out_shape=jax.ShapeDtypeStruct((t, 1024), BF16), compiler_params=_params(("parallel",)),
    )(proj, proj, proj, conv_w, *after)


def _conv_bwd(dza, proj, conv_w, name):
    t = proj.shape[0]
    nb, seg, own, wspec = _conv_specs(t)

    def body(dz_ref, ax_ref, ab_ref, ac_ref, w_ref, dax_ref, dab_ref, dac_ref, dw_ref):
        ax, ab, ac, dz = ax_ref[...].astype(F32), ab_ref[...].astype(F32), ac_ref[...].astype(F32), dz_ref[...].astype(F32)
        w = w_ref[...]
        u = ac * ax
        u1, u2 = _shift_down(u, 1), _shift_down(u, 2)
        yc = w[0:1] * u2 + w[1:2] * u1 + w[2:3] * u
        dab_ref[...] = (dz * yc).astype(BF16)
        dyc = dz * ab
        du = w[2:3] * dyc + w[1:2] * _shift_up(dyc, 1) + w[0:1] * _shift_up(dyc, 2)
        dax_ref[...] = (du * ac).astype(BF16)
        dac_ref[...] = (du * ax).astype(BF16)
        dw_ref[0:1, :] = jnp.sum(dyc * u2, axis=0, keepdims=True)
        dw_ref[1:2, :] = jnp.sum(dyc * u1, axis=0, keepdims=True)
        dw_ref[2:3, :] = jnp.sum(dyc * u, axis=0, keepdims=True)

    act = jax.ShapeDtypeStruct((t, 1024), BF16)
    return pl.pallas_call(
        body, name=name, grid=(nb,), in_specs=[own, seg(C_AX), seg(C_AB), seg(C_AC), wspec], out_specs=[own, own, own, wspec],
        out_shape=[act, act, act, jax.ShapeDtypeStruct((3, 1024), F32)], compiler_params=_params(("parallel",)),
    )(dza, proj, proj, proj, conv_w)


def _dot(a, b, dims, precision=None):
    return lax.dot_general(a, b, (dims, ((), ())), precision=precision, preferred_element_type=F32)


_CONTRACT = {"nn": ((1,), (0,)), "nt": ((1,), (1,)), "tn": ((0,), (0,))}


def _bdot_raw(a, b, mode):
    return _dot(a.astype(BF16), b.astype(BF16), _CONTRACT[mode])


@functools.partial(jax.custom_vjp, nondiff_argnums=(2,))
def _bdot(a, b, mode):
    return _bdot_raw(a, b, mode)


def _bdot_fwd(a, b, mode):
    return _bdot_raw(a, b, mode), (a, b)


def _bdot_bwd(mode, res, ct):
    a, b = res
    if mode == "nn":
        return _bdot_raw(ct, b, "nt"), _bdot_raw(a, ct, "tn")
    if mode == "nt":
        return _bdot_raw(ct, b, "nn"), _bdot_raw(ct, a, "tn")
    return _bdot_raw(b, ct, "nt"), _bdot_raw(a, ct, "nn")


_bdot.defvjp(_bdot_fwd, _bdot_bwd)


@functools.partial(jax.custom_vjp, nondiff_argnums=(2,))
def _sum_dot(ones, x, mode):
    head = x.astype(BF16)
    tail = x - head.astype(F32)
    if mode == "nn":
        return _bdot_raw(ones, head, "nn") + _bdot_raw(ones, tail, "nn")
    return _bdot_raw(head, ones, "tn") + _bdot_raw(tail, ones, "tn")


def _sum_dot_fwd(ones, x, mode):
    return _sum_dot(ones, x, mode), ones


def _sum_dot_bwd(mode, ones, ct):
    return jnp.zeros_like(ones), (_bdot_raw(ones, ct, "tn") if mode == "nn" else _bdot_raw(ones, ct, "nt"))


_sum_dot.defvjp(_sum_dot_fwd, _sum_dot_bwd)


def _gla_chunk(q, k, v, og, alr, s_in, wa, ba, gain):
    c = q.shape[0]
    z =_bdot(alr, wa, "nn") + ba
    la = (jnp.minimum(z, 0.0) - jnp.log(1.0 + jnp.exp(-jnp.abs(z)))) * (1.0 / TAU)
    row = lax.broadcasted_iota(jnp.int32, (c, c), 0)
    col = lax.broadcasted_iota(jnp.int32, (c, c), 1)
    lower = row >= col
    b = _sum_dot(lower.astype(F32), la, "nn")
    trow = lax.broadcasted_iota(jnp.int32, la.shape, 0)
    mid = jnp.sum(jnp.where(trow <= c // 2, la, 0.0), axis=0, keepdims=True)
    blast = jnp.sum(la, axis=0, keepdims=True)
    qs = q * (DK ** -0.5)
    e_up, e_dn = jnp.exp(b - mid), jnp.exp(mid - b)
    a_fwd = _bdot(qs * e_up, k * e_dn, "nt")
    a_rev = _bdot(qs * e_dn, k * e_up, "nt")
    att = jnp.where(lower, a_fwd, a_rev)
    o = _bdot(att, v, "nn") + _bdot(qs * jnp.exp(b), s_in, "nn")
    upd = _bdot(k * jnp.exp(blast - b), v, "tn")
    blast_col = _sum_dot(jnp.ones((c, DV), F32), la, "tn")
    s_out = jnp.exp(blast_col) * s_in + upd
    on = o * _rinv(o) * gain
    return on * og * _sig(og), s_out


def _gla_specs(t, rev):
    n = t // STEP_ROWS
    ch = (lambda i: n - 1 - i) if rev else (lambda i: i)
    col = lambda w, c0: pl.BlockSpec((STEP_ROWS, HEADS * w), lambda i, cb=c0 // (HEADS * w): (ch(i), cb))
    whole = lambda shape: pl.BlockSpec(shape, lambda i, nd=len(shape): (0,) * nd)
    specs = dict(
        q=col(DK, C_Q), k=col(DK, C_K), v=col(DV, C_V), og=col(DV, C_OG),
        alr=pl.BlockSpec((STEP_ROWS, 128), lambda i: (ch(i), C_ALR // 128)),
        wa=whole((128, HEADS * DK)), ba=whole((1, HEADS * DK)), gain=whole((1, DV)),
        state=pl.BlockSpec((STEP_CHUNKS, HEADS, DK, DV), lambda i: (ch(i), 0, 0, 0)),
        odk=pl.BlockSpec((STEP_ROWS, HEADS * DK), lambda i: (ch(i), 0)), odv=pl.BlockSpec((STEP_ROWS, HEADS * DV), lambda i: (ch(i), 0)),
        oalr=pl.BlockSpec((STEP_ROWS, 128), lambda i: (ch(i), 0)), whole=whole,
    )
    return n, specs


def _head_cols(h):
    return slice(h * DK, (h + 1) * DK), slice(h * DV, (h + 1) * DV)


def _gla_fwd(proj, wa, ba, gain, name):
    t = proj.shape[0]
    n, sp = _gla_specs(t, False)

    def body(q_ref, k_ref, v_ref, og_ref, alr_ref, wa_ref, ba_ref, g_ref, zb_ref, st_ref, s_scr):
        @pl.when(pl.program_id(0) == 0)
        def _():
            s_scr[...] = jnp.zeros_like(s_scr)

        state = [s_scr[h] for h in range(HEADS)]
        for c in range(STEP_CHUNKS):
            rows = slice(c * CHUNK, (c + 1) * CHUNK)
            alr = alr_ref[rows, :].astype(F32)
            for h in range(HEADS):
                kc, vc = _head_cols(h)
                st_ref[c, h] = state[h]
                zb, state[h] = _gla_chunk(q_ref[rows, kc].astype(F32), k_ref[rows, kc].astype(F32), v_ref[rows, vc].astype(F32),
                                          og_ref[rows, vc].astype(F32), alr, state[h], wa_ref[:, kc].astype(F32), ba_ref[:, kc], g_ref[...])
                zb_ref[rows, vc] = zb.astype(BF16)
        for h in range(HEADS):
            s_scr[h] = state[h]

    return pl.pallas_call(
        body, name=name, grid=(n,),
        in_specs=[sp["q"], sp["k"], sp["v"], sp["og"], sp["alr"], sp["wa"], sp["ba"], sp["gain"]],
        out_specs=[sp["odv"], sp["state"]],
        out_shape=[jax.ShapeDtypeStruct((t, HEADS * DV), BF16), jax.ShapeDtypeStruct((t // CHUNK, HEADS, DK, DV), F32)],
        scratch_shapes=[pltpu.VMEM((HEADS, DK, DV), F32)],
        compiler_params=_params(("arbitrary",)),
    )(proj, proj, proj, proj, proj, wa, ba, gain)


def _gla_bwd(dzb, proj, states, wa, ba, gain, name):
    t = proj.shape[0]
    n, sp = _gla_specs(t, True)

    def body(dz_ref, q_ref, k_ref, v_ref, og_ref, alr_ref, st_ref, wa_ref, ba_ref, g_ref,
             dq_ref, dk_ref, dv_ref, dog_ref, dalr_ref, dwa_ref, dba_ref, dg_ref, ds_scr):
        @pl.when(pl.program_id(0) == 0)
        def _():
            ds_scr[...] = jnp.zeros_like(ds_scr)
            dwa_ref[...] = jnp.zeros_like(dwa_ref)
            dba_ref[...] = jnp.zeros_like(dba_ref)
            dg_ref[...] = jnp.zeros_like(dg_ref)

        dstate = [ds_scr[h] for h in range(HEADS)]
        dwa_sum, dba_sum, dgain_sum = [None] * HEADS, [None] * HEADS, None
        for c in reversed(range(STEP_CHUNKS)):
            rows = slice(c * CHUNK, (c + 1) * CHUNK)
            alr = alr_ref[rows, :].astype(F32)
            dalr_sum = None
            for h in range(HEADS):
                kc, vc = _head_cols(h)
                args = (q_ref[rows, kc].astype(F32), k_ref[rows, kc].astype(F32), v_ref[rows, vc].astype(F32), og_ref[rows, vc].astype(F32),
                        alr, st_ref[c, h], wa_ref[:, kc].astype(F32), ba_ref[:, kc], g_ref[...])
                _, vjp = jax.vjp(_gla_chunk, *args)
                dq, dk, dv, dog, dalr, dstate[h], dwa, dba, dgain = vjp((dz_ref[rows, vc].astype(F32), dstate[h]))
                dq_ref[rows, kc] = dq.astype(BF16)
                dk_ref[rows, kc] = dk.astype(BF16)
                dv_ref[rows, vc] = dv.astype(BF16)
                dog_ref[rows, vc] = dog.astype(BF16)
                dwa_sum[h] = dwa if dwa_sum[h] is None else dwa_sum[h] + dwa
                dba_sum[h] = dba if dba_sum[h] is None else dba_sum[h] + dba
                dalr_sum = dalr if dalr_sum is None else dalr_sum + dalr
                dgain_sum = dgain if dgain_sum is None else dgain_sum + dgain
            dalr_ref[rows, :] = dalr_sum
        for h in range(HEADS):
            ds_scr[h] = dstate[h]
            dwa_ref[h] += dwa_sum[h]
            dba_ref[h] += dba_sum[h]
        dg_ref[...] += dgain_sum

    whole = sp["whole"]
    return pl.pallas_call(
        body, name=name, grid=(n,),
        in_specs=[sp["odv"], sp["q"], sp["k"], sp["v"], sp["og"], sp["alr"], sp["state"], sp["wa"], sp["ba"], sp["gain"]],
        out_specs=[sp["odk"], sp["odk"], sp["odv"], sp["odv"], sp["oalr"], whole((HEADS, 128, DK)), whole((HEADS, 1, DK)), whole((1, DV))],
        out_shape=[jax.ShapeDtypeStruct((t, HEADS * DK), BF16), jax.ShapeDtypeStruct((t, HEADS * DK), BF16),
                   jax.ShapeDtypeStruct((t, HEADS * DV), BF16), jax.ShapeDtypeStruct((t, HEADS * DV), BF16),
                   jax.ShapeDtypeStruct((t, 128), F32), jax.ShapeDtypeStruct((HEADS, 128, DK), F32),
                   jax.ShapeDtypeStruct((HEADS, 1, DK), F32), jax.ShapeDtypeStruct((1, DV), F32)],
        scratch_shapes=[pltpu.VMEM((HEADS, DK, DV), F32)],
        compiler_params=_params(("arbitrary",)),
    )(dzb, proj, proj, proj, proj, proj, states, wa, ba, gain)


def _local_step(x, p, tgt, gather_start, gather_pass_on, gather_finish, scatter_start, scatter_next, small):
    b_alpha, gain = small["b_alpha_up"], small["gla_head_gain"]
    gather_start(0, ())
    w = dict(gather_finish(0, ()))
    conv_w, w_alpha = w["conv_w"], w["w_alpha_up"]
    wa_p = jnp.zeros((128, HEADS * DK), BF16).at[:GATE_RANK].set(w_alpha.astype(BF16))

    t2 = gather_start(2, (w["in_rest_t"], gather_start(1, ())))
    h1 = _rms_fwd(x, small["g_pre_mix"], "rms_pre_mix")
    proj = _mm(h1, w["in_rest_t"], "nt", BF16, "mm_proj", after=(t2,))
    proj_gates = _mm(h1, w["in_gates_t"], "nt", BF16, "mm_proj_gates", after=(t2,))
    za = _conv_fwd(proj, conv_w, "conv_fwd", after=(gather_pass_on(1, (proj,)),))
    zb, states = _gla_fwd(proj, wa_p, b_alpha, gain, "gla_fwd")
    t3 = gather_start(3, (zb, za))
    w.update(gather_finish(1, (t3,)))
    ya = _mm(za, w["a_out"], "nn", BF16, "mm_ya")
    yb = _mm(zb, w["b_out"], "nn", BF16, "mm_yb")
    mix = _mix_fwd(proj_gates, ya, yb, "mix_fwd")
    m2 = _mm(mix, w["mix"], "nn", F32, "mm_mix")
    t4 = gather_start(4, (m2,))
    x1, h2 = _post_pre(x, m2, small["g_post_mix"], small["g_pre_ffn"], "norm_mix_ffn", after=(gather_pass_on(2, (m2,)),))
    w.update(gather_finish(2, (h2, t4)))
    fu = _mm(h2, w["up_t"], "nt", BF16, "mm_up", **WIDE_N)
    fg = _mm(h2, w["gate_t"], "nt", BF16, "mm_gate", after=(gather_pass_on(3, (fu,)),), **WIDE_N)
    s = _swiglu_fwd(fg, fu, "swiglu_fwd")
    w.update(gather_finish(3, (s,)))
    f = _mm(s, w["down"], "nn", F32, "mm_down", after=(gather_pass_on(4, (s,)),), **LONG_K)
    x2, h3 = _post_pre(x1, f, small["g_post_ffn"], small["g_pre_ple"], "norm_ffn_ple")
    w.update(gather_finish(4, (h3,)))
    pg = _mm(h3, w["pg"], "nn", F32, "mm_pg")
    p_bf = p.astype(BF16)
    pp = _mm(p_bf, w["pp"], "nn", F32, "mm_pp", b3=True, tm=2048)
    loss_rows, d3, dpg, dpp, dg_post_ple = _ple_final(x2, pg, pp, tgt, small["g_post_ple"], "ple_final")

    gw = {}
    gw["pp"] = _mm(p_bf, dpp, "tn", BF16, "mm_dw_pp", out3=True)
    gw["pg"] = _mm(h3, dpg, "tn", BF16, "mm_dw_pg")
    dh3 = _mm(dpg, w["pg"], "nt", F32, "mm_dh3", after=(scatter_start(["w_ple_proj", "w_ple_gate"], gw),))
    d2, df, dg_pre_ple, dg_post_ffn = _norm_bwd(d3, dh3, x2, small["g_pre_ple"], f, small["g_post_ffn"], "norm_bwd_ple_ffn")
    gw["down"] = _mm(s, df, "tn", BF16, "mm_dw_down", tm=1408)
    ds = _mm(df, w["down"], "nt", BF16, "mm_ds", after=(scatter_start(["w_ff_down"], gw),), **WIDE_N)
    dfg, dfu = _swiglu_bwd(ds, fg, fu, "swiglu_bwd")
    gw["gate_t"] = _mm(dfg, h2, "tn", BF16, "mm_dw_gate", tm=1408)
    gw["up_t"] = _mm(dfu, h2, "tn", BF16, "mm_dw_up", after=(gw["gate_t"],), tm=1408)
    dh2 = _mm(dfg, w["gate_t"], "nn", F32, "mm_dh2_gate", after=(scatter_start(["w_ff_gate", "w_ff_up"], gw),), **LONG_K)
    dh2 = _mm(dfu, w["up_t"], "nn", F32, "mm_dh2_up", add=dh2, **LONG_K)
    d1, dm2, dg_pre_ffn, dg_post_mix = _norm_bwd(d2, dh2, x1, small["g_pre_ffn"], m2, small["g_post_mix"], "norm_bwd_ffn_mix")
    dmix = _mm(dm2, w["mix"], "nt", F32, "mm_dmix")
    gw["mix"] = _mm(mix, dm2, "tn", BF16, "mm_dw_mix")
    dgab, dya, dyb = _mix_bwd(dmix, proj_gates, ya, yb, "mix_bwd")
    gw["in_gates_t"] = _mm(dgab, h1, "tn", BF16, "mm_dw_in_gates")
    dza = _mm(dya, w["a_out"], "nt", BF16, "mm_dza", after=(scatter_start(["w_mix_out"], gw), gw["in_gates_t"]))
    gw["a_out"] = _mm(za, dya, "tn", BF16, "mm_dw_a_out")
    gw["b_out"] = _mm(zb, dyb, "tn", BF16, "mm_dw_b_out", after=(gw["a_out"],))
    dzb = _mm(dyb, w["b_out"], "nt", BF16, "mm_dzb", after=(scatter_start(["w_a_out", "w_b_out"], gw),))
    dax, dab, dac, dconv = _conv_bwd(dza, proj, conv_w, "conv_bwd")
    dq, dk, dv, dog, dalr, dwa, dba, dgain = _gla_bwd(dzb, proj, states, wa_p, b_alpha, gain, "gla_bwd")
    pad = jnp.zeros((dax.shape[0], REST - C_ALR - 128), BF16)
    drest = jnp.concatenate([dax, dab, dac, dq, dk, dv, dog, dalr.astype(BF16), pad], axis=1)
    gw["in_rest_t"] = _mm(drest, h1, "tn", BF16, "mm_dw_in_rest", tm=1280)
    dh1 = _mm(dgab, w["in_gates_t"], "nn", F32, "mm_dh1_gates", after=(scatter_start(["w_in"], gw),), tm=1024, tk=4096, n_outer=True)
    dh1 = _mm(drest, w["in_rest_t"], "nn", F32, "mm_dh1_rest", add=dh1, after=(scatter_next((dh1,)),), **LONG_K)
    grad_x, dg_pre_mix = _norm_bwd(d1, dh1, x, small["g_pre_mix"], None, None, "norm_bwd_mix")

    gs = dict(
        conv_w=dconv,
        w_alpha_up=jnp.transpose(dwa[:, :GATE_RANK, :], (1, 0, 2)).reshape(GATE_RANK, HEADS * DK),
        b_alpha_up=dba.reshape(1, HEADS * DK), gla_head_gain=dgain,
        g_pre_mix=dg_pre_mix, g_post_mix=dg_post_mix, g_pre_ffn=dg_pre_ffn, g_post_ffn=dg_post_ffn,
        g_pre_ple=dg_pre_ple, g_post_ple=dg_post_ple,
    )
    return loss_rows, grad_x, gs


def _place():
    x, y, c = lax.axis_index("x"), lax.axis_index("y"), lax.axis_index("c")
    return x, y, c, [(1 - x, y), (x, 1 - y), (1 - x, 1 - y)]


def _pair_add(mine8, got4, name):
    _, r, cols = mine8.shape
    steps, blk, at = _tiles(r, cols)
    core = lax.axis_index("c").astype(jnp.int32).reshape(1)

    def body(c_ref, a_ref, b_ref, o_ref):
        o_ref[...] = (a_ref[...].astype(F32) + b_ref[...].astype(F32)).astype(BF16)

    return pl.pallas_call(
        body, name=name,
        grid_spec=pltpu.PrefetchScalarGridSpec(
            num_scalar_prefetch=1, grid=(4, steps),
            in_specs=[pl.BlockSpec((None,) + blk, lambda ch, i, c_ref: (2 * ch + c_ref[0],) + at(i)),
                      pl.BlockSpec((None,) + blk, lambda ch, i, c_ref: (ch,) + at(i))],
            out_specs=pl.BlockSpec((None,) + blk, lambda ch, i, c_ref: (ch,) + at(i))),
        out_shape=jax.ShapeDtypeStruct((4, r, cols), BF16),
        compiler_params=_params(("parallel", "parallel")),
    )(core, mine8, got4)


HBM = pl.BlockSpec(memory_space=pltpu.HBM)
SEM = pl.BlockSpec(memory_space=pltpu.SEMAPHORE)
EFFECT = pltpu.SideEffectType.DATAFLOW_SIDE_EFFECTING


def _in_hbm(a):
    return pltpu.with_memory_space_constraint(a, pltpu.HBM)


def _remote_copies(plan, srcs, lands, send_sems, recv_sems):
    return [pltpu.make_async_remote_copy(src_ref=s, dst_ref=d, send_sem=send_sems.at[i], recv_sem=recv_sems.at[i], device_id=peer,
                                         device_id_type=MESH) for i, (s, d, peer) in enumerate(plan(srcs, lands))]


def _copies_start(plan, n_copies, srcs, land_shapes, name, after=()):
    ns, nl = len(srcs), len(land_shapes)

    def body(*refs):
        send_sems, recv_sems = refs[ns + nl + len(after):ns + nl + len(after) + 2]
        for cp in _remote_copies(plan, refs[:ns], refs[ns:ns + nl], send_sems, recv_sems):
            cp.start()
        refs[-1][...] = jnp.zeros((8, 128), F32)

    sems = pltpu.SemaphoreType.DMA((n_copies,))
    return pl.pallas_call(
        body, name=name,
        out_shape=(sems, sems, *[pltpu.HBM(s.shape, s.dtype) for s in srcs], *[pltpu.HBM(s.shape, s.dtype) for s in land_shapes],
                   jax.ShapeDtypeStruct((8, 128), F32)),
        in_specs=[HBM] * (ns + nl) + [ANY] * len(after),
        out_specs=(SEM, SEM, *[HBM] * (ns + nl), pl.BlockSpec(memory_space=pltpu.VMEM)),
        input_output_aliases={i: 2 + i for i in range(ns + nl)},
        compiler_params=pltpu.CompilerParams(has_side_effects=EFFECT),
    )(*[_in_hbm(s) for s in srcs], *[_in_hbm(lax.empty(s.shape, s.dtype)) for s in land_shapes], *after)


def _copies_wait(plan, state, ns, name, after=()):
    send_sems, recv_sems, *arrs = state[:-1]
    n = len(arrs)

    def body(*refs):
        cps = _remote_copies(plan, refs[:ns], refs[ns:n], refs[n], refs[n + 1])
        for cp in cps:
            cp.wait_send()
        for cp in cps:
            cp.wait_recv()

    out = pl.pallas_call(
        body, name=name, out_shape=tuple(pltpu.HBM(a.shape, a.dtype) for a in arrs),
        in_specs=[HBM] * n + [SEM, SEM] + [ANY] * len(after), out_specs=tuple([HBM] * n),
        input_output_aliases={i: i for i in range(n)},
        compiler_params=pltpu.CompilerParams(has_side_effects=EFFECT),
    )(*arrs, send_sems, recv_sems, *after)
    return list(out[:ns]), list(out[ns:])


def _copies_relay(plan, state, ns, next_plan, n_next, name, after=()):
    send_sems, recv_sems, *arrs = state[:-1]
    n = len(arrs)

    def body(*refs):
        cps = _remote_copies(plan, refs[:ns], refs[ns:n], refs[n], refs[n + 1])
        for cp in cps:
            cp.wait_send()
        for cp in cps:
            cp.wait_recv()
        outs = refs[n + 2 + len(after):]
        for cp in _remote_copies(next_plan, refs[:ns], refs[ns:n], outs[0], outs[1]):
            cp.start()
        outs[-1][...] = jnp.zeros((8, 128), F32)

    sems = pltpu.SemaphoreType.DMA((n_next,))
    return pl.pallas_call(
        body, name=name,
        out_shape=(sems, sems, *[pltpu.HBM(a.shape, a.dtype) for a in arrs], jax.ShapeDtypeStruct((8, 128), F32)),
        in_specs=[HBM] * n + [SEM, SEM] + [ANY] * len(after),
        out_specs=(SEM, SEM, *[HBM] * n, pl.BlockSpec(memory_space=pltpu.VMEM)),
        input_output_aliases={i: 2 + i for i in range(n)},
        compiler_params=pltpu.CompilerParams(has_side_effects=EFFECT),
    )(*arrs, send_sems, recv_sems, *after)


def _pass_on_plan(srcs, lands):
    x, y, c, chips = _place()
    return [(l.at[4 * px + 2 * py + c], l.at[4 * px + 2 * py + c], (x, y, 1 - c)) for l in lands for px, py in chips]


def _gather_plan(srcs, lands):
    x, y, c, chips = _place()
    peers = [(x, y, 1 - c)] + [(*chip, c) for chip in chips]
    return [(s, l.at[4 * x + 2 * y + c], peer) for s, l in zip(srcs, lands) for peer in peers]


def _gather_plan_near(srcs, lands):
    x, y, c, _ = _place()
    peers = [(x, y, 1 - c), (1 - x, y, c), (x, 1 - y, c)]
    return [(s, l.at[4 * x + 2 * y + c], peer) for s, l in zip(srcs, lands) for peer in peers]


def _scatter_plan(srcs, lands):
    x, y, c, _ = _place()
    peers = [(1 - x if k & 4 else x, 1 - y if k & 2 else y, 1 - c if k & 1 else c) for k in range(1, N_DEV)]
    return [(s.at[4 * px + 2 * py + pc], l.at[4 * x + 2 * y + c], (px, py, pc)) for s, l in zip(srcs, lands) for px, py, pc in peers]


def _everyone_plan(srcs, lands):
    x, y, c, _ = _place()
    peers = [(1 - x if k & 4 else x, 1 - y if k & 2 else y, 1 - c if k & 1 else c) for k in range(1, N_DEV)]
    return [(s, l.at[4 * x + 2 * y + c], peer) for s, l in zip(srcs, lands) for peer in peers]


def _sum_parts(got, own, me, name):
    def body(me_ref, got_ref, own_ref, o_ref):
        acc = jnp.where(me_ref[0] == 0, own_ref[...], got_ref[0])
        for d in range(1, N_DEV):
            acc = acc + jnp.where(me_ref[0] == d, own_ref[...], got_ref[d])
        o_ref[...] = acc

    return pl.pallas_call(
        body, name=name,
        grid_spec=pltpu.PrefetchScalarGridSpec(
            num_scalar_prefetch=1, grid=(1,),
            in_specs=[pl.BlockSpec(got.shape, lambda i, me_ref: (0, 0, 0)), pl.BlockSpec(own.shape, lambda i, me_ref: (0, 0))],
            out_specs=pl.BlockSpec(own.shape, lambda i, me_ref: (0, 0))),
        out_shape=jax.ShapeDtypeStruct(own.shape, F32),
    )(me.astype(jnp.int32).reshape(1), got, own)


def _sibling_plan(srcs, lands):
    x, y, c, _ = _place()
    return [(s.at[2 * ch + 1 - c], l.at[ch], (x, y, 1 - c)) for s, l in zip(srcs, lands) for ch in range(4)]


def _chip_plan(srcs, lands):
    x, y, c, chips = _place()
    return [(s.at[2 * px + py], l.at[2 * x + y], (px, py, c)) for s, l in zip(srcs, lands) for px, py in chips]


def _put_own(shard, zone, me, name):
    r, c = shard.shape
    tr = r if r <= 256 else _pick(r, (256, 64))

    def body(me_ref, s_ref, z_ref, o_ref):
        o_ref[...] = s_ref[...]

    return pl.pallas_call(
        body, name=name,
        grid_spec=pltpu.PrefetchScalarGridSpec(
            num_scalar_prefetch=1, grid=(r // tr,),
            in_specs=[pl.BlockSpec((tr, c), lambda i, me_ref: (i, 0)), ANY],
            out_specs=pl.BlockSpec((None, tr, c), lambda i, me_ref: (me_ref[0], i, 0))),
        out_shape=jax.ShapeDtypeStruct(zone.shape, zone.dtype), input_output_aliases={2: 0},
        compiler_params=_params(("arbitrary",)),
    )(me.astype(jnp.int32).reshape(1), shard, zone)


def _gather_relay(lands, name):
    n = len(lands)

    def body(*refs):
        zones, outs = refs[:n], refs[n:2 * n]
        send_sems, recv_sems = refs[2 * n:]
        x, y, c, _ = _place()
        south = c == 0
        near_x, near_y, across = 4 * (1 - x) + 2 * y + c, 4 * x + 2 * (1 - y) + c, 4 * (1 - x) + 2 * (1 - y) + c
        passed = jnp.where(south, near_y, near_x)
        onward = (jnp.where(south, 1 - x, x), jnp.where(south, y, 1 - y), c)

        def copy(a, k, slot, to):
            return pltpu.make_async_remote_copy(src_ref=zones[a].at[slot], dst_ref=outs[a].at[slot], send_sem=send_sems.at[a, k],
                                                recv_sem=recv_sems.at[a, k], device_id=to, device_id_type=MESH)

        first = [copy(a, 0, passed, onward) for a in range(n)]
        first += [copy(a, 1 + j, slot, (x, y, 1 - c)) for j, slot in enumerate((near_x, near_y)) for a in range(n)]
        for cp in first:
            cp.start()
        last = []
        for a in range(n):
            copy(a, 0, across, onward).wait_recv()
            last.append(copy(a, 3, across, (x, y, 1 - c)))
            last[-1].start()
        for cp in first[n:] + last:
            cp.wait_recv()
        for cp in first + last:
            cp.wait_send()

    return pl.pallas_call(
        body, name=name, in_specs=[ANY] * n, out_specs=[ANY] * n,
        out_shape=[jax.ShapeDtypeStruct(l.shape, l.dtype) for l in lands],
        input_output_aliases={a: a for a in range(n)},
        scratch_shapes=[pltpu.SemaphoreType.DMA((n, 4)), pltpu.SemaphoreType.DMA((n, 4))],
    )(*lands)


def _adamw_parts(w, got, mine, me, m, v, name, after=()):
    r, c = w.shape
    n_parts = got.shape[0]
    steps, blk, at = _tiles(r, c)

    def body(me_ref, w_ref, got_ref, own_ref, m_ref, v_ref, *rest):
        go_ref, d_ref, mo_ref, vo_ref = rest[len(after):]
        own = own_ref[...].astype(F32)
        gv = jnp.where(me_ref[0] == 0, own, got_ref[0].astype(F32))
        for d in range(1, n_parts):
            gv = gv + jnp.where(me_ref[0] == d, own, got_ref[d].astype(F32))
        _adamw_math(gv, w_ref, m_ref, v_ref, go_ref, d_ref, mo_ref, vo_ref)

    tile = pl.BlockSpec(blk, lambda i, me_ref: at(i))
    out = jax.ShapeDtypeStruct((r, c), F32)
    return pl.pallas_call(
        body, name=name,
        grid_spec=pltpu.PrefetchScalarGridSpec(
            num_scalar_prefetch=1, grid=(steps,),
            in_specs=[tile, pl.BlockSpec((n_parts,) + blk, lambda i, me_ref: (0,) + at(i)),
                      pl.BlockSpec((None,) + blk, lambda i, me_ref: (me_ref[0],) + at(i)), tile, tile] + [ANY] * len(after),
            out_specs=[tile] * 4),
        out_shape=[out] * 4, compiler_params=_params(("parallel",)),
    )(me.astype(jnp.int32).reshape(1), w, got, mine, m, v, *after)


def _adamw_math(gv, w_ref, m_ref, v_ref, go_ref, d_ref, mo_ref, vo_ref):
    mn = B1 * m_ref[...] + (1.0 - B1) * gv
    vn = B2 * v_ref[...] + (1.0 - B2) * (gv * gv)
    m_hat = mn / (1.0 - B1 ** STEP)
    v_hat = vn / (1.0 - B2 ** STEP)
    go_ref[...] = gv
    d_ref[...] = -LR * (m_hat / (jnp.sqrt(v_hat) + ADAM_EPS) + WD * w_ref[...])
    mo_ref[...] = mn
    vo_ref[...] = vn


def _adamw(w, g, m, v, name):
    def body(w_ref, g_ref, m_ref, v_ref, go_ref, d_ref, mo_ref, vo_ref):
        _adamw_math(g_ref[...], w_ref, m_ref, v_ref, go_ref, d_ref, mo_ref, vo_ref)

    tile = pl.BlockSpec(w.shape, lambda i: (0, 0))
    out = jax.ShapeDtypeStruct(w.shape, F32)
    return pl.pallas_call(
        body, name=name, grid=(1,), in_specs=[tile] * 4, out_specs=[tile] * 4, out_shape=[out] * 4,
        compiler_params=_params(("parallel",)),
    )(w, g, m, v)


BIG = ["w_in", "w_a_out", "w_b_out", "w_mix_out", "w_ff_gate", "w_ff_up", "w_ff_down", "w_ple_gate", "w_ple_proj"]
TRANSPOSED = ["w_in", "w_ff_gate", "w_ff_up"]
GRAD_OF = dict(w_ple_proj="pp", w_ple_gate="pg", w_ff_down="down", w_ff_gate="gate_t", w_ff_up="up_t", w_mix_out="mix", w_a_out="a_out",
               w_b_out="b_out")
SMALL = ["conv_w", "w_alpha_up", "b_alpha_up", "gla_head_gain", "g_pre_mix", "g_post_mix", "g_pre_ffn", "g_post_ffn", "g_pre_ple", "g_post_ple"]
WEIGHTS = ["w_in", "conv_w", "w_a_out", "w_alpha_up", "b_alpha_up", "gla_head_gain", "w_b_out", "w_mix_out", "g_pre_mix", "g_post_mix",
           "g_pre_ffn", "g_post_ffn", "w_ff_gate", "w_ff_up", "w_ff_down", "g_pre_ple", "g_post_ple", "w_ple_gate", "w_ple_proj"]


def _in_t_from_blocks(z):
    w = z.reshape(-1, z.shape[-1])
    return w[R_GA:R_END], jnp.concatenate([w[:R_GA], jnp.zeros((REST - R_GA, w.shape[1]), w.dtype)], axis=0)


def _blocks_from_in_t(g_gates, g_rest):
    per = R_END // N_DEV

    def rows(lo, hi):
        out = []
        for n0, n1, g in ((0, R_GA, g_rest), (R_GA, R_END, g_gates)):
            a, e = max(lo, n0), min(hi, n1)
            if a < e:
                out.append(g[a - n0:e - n0])
        return out

    return jnp.stack([jnp.concatenate(rows(b * per, (b + 1) * per), axis=0) for b in range(N_DEV)])


def _cols_to_full(g8):
    n, r, c = g8.shape
    return jnp.transpose(g8, (1, 0, 2)).reshape(r, n * c)


def _full_to_cols(a):
    r, c = a.shape
    return jnp.transpose(a.reshape(r, N_DEV, c // N_DEV), (1, 0, 2))


def _pack(arrs, rows):
    flat = jnp.concatenate([a.reshape(-1) for a in arrs])
    return jnp.pad(flat, (0, rows * 128 - flat.shape[0])).reshape(rows, 128)


def _unpack(packed, shapes):
    flat, out, o = packed.reshape(-1), [], 0
    for s in shapes:
        size = 1
        for d in s:
            size *= d
        out.append(flat[o:o + size].reshape(s))
        o += size
    return out


def kernel(x, p, w_in, conv_w, w_a_out, w_alpha_up, b_alpha_up, gla_head_gain, w_b_out, w_mix_out, g_pre_mix, g_post_mix, g_pre_ffn, g_post_ffn, w_ff_gate, w_ff_up, w_ff_down, g_pre_ple, g_post_ple, w_ple_gate, w_ple_proj, loss_target, m_w_in, m_conv_w, m_w_a_out, m_w_alpha_up, m_b_alpha_up, m_gla_head_gain, m_w_b_out, m_w_mix_out, m_g_pre_mix, m_g_post_mix, m_g_pre_ffn, m_g_post_ffn, m_w_ff_gate, m_w_ff_up, m_w_ff_down, m_g_pre_ple, m_g_post_ple, m_w_ple_gate, m_w_ple_proj, v_w_in, v_conv_w, v_w_a_out, v_w_alpha_up, v_b_alpha_up, v_gla_head_gain, v_w_b_out, v_w_mix_out, v_g_pre_mix, v_g_post_mix, v_g_pre_ffn, v_g_post_ffn, v_w_ff_gate, v_w_ff_up, v_w_ff_down, v_g_pre_ple, v_g_post_ple, v_w_ple_gate, v_w_ple_proj):
    args = dict(locals())
    shard = lambda n, a: jnp.transpose(a[0]) if n in TRANSPOSED else a[0]
    wts = {n: shard(n, args[n]) for n in WEIGHTS}
    mom = {n: shard(n, args["m_" + n]) for n in WEIGHTS}
    var = {n: shard(n, args["v_" + n]) for n in WEIGHTS}
    me =4 * lax.axis_index("x") + 2 * lax.axis_index("y") + lax.axis_index("c")

    groups = [["w_in", "conv_w", "w_alpha_up"], ["w_a_out", "w_b_out", "w_mix_out"], ["w_ff_gate", "w_ff_up"], ["w_ff_down"],
              ["w_ple_gate", "w_ple_proj"]]
    grad_groups = []
    rows_full = lambda g: g.reshape(-1, g.shape[-1])
    gathers, scatters = {}, {}

    def gather_start(gi, after):
        if gi not in gathers:
            shards = [wts[n].astype(BF16) if n in BIG else wts[n] for n in groups[gi]]
            zones = [jax.ShapeDtypeStruct((N_DEV,) + s.shape, s.dtype) for s in shards]
            plan, peers = (_gather_plan_near, 3) if gi == 0 else (_gather_plan, 4)
            gathers[gi] = (shards, _copies_start(plan, peers * len(shards), shards, zones, "gather_start_%d" % gi, after))
        return gathers[gi][1][-1]

    def gather_pass_on(gi, after):
        shards, state = gathers[gi]
        gathers[gi] = (shards, _copies_relay(_gather_plan, state, len(shards), _pass_on_plan, 3 * len(shards), "gather_pass_on_%d" % gi, after))
        return gathers[gi][1][-1]

    def gather_finish(gi, after):
        shards, state = gathers[gi]
        shards, zones = _copies_wait(_gather_plan_near if gi == 0 else _pass_on_plan, state, len(shards), "gather_wait_%d" % gi, after)
        if gi == 0:
            zones = _gather_relay(zones, "gather_relay_%d" % gi)
            gather_start(1, (zones[0],))
        g8 = {n: _put_own(s, z, me, "gather_own_" + n) for n, s, z in zip(groups[gi], shards, zones)}
        if gi == 0:
            in_gates_t, in_rest_t = _in_t_from_blocks(g8["w_in"])
            return dict(in_gates_t=in_gates_t, in_rest_t=in_rest_t,
                        conv_w=_cols_to_full(g8["conv_w"]), w_alpha_up=_cols_to_full(g8["w_alpha_up"]))
        if gi == 1:
            return dict(a_out=_cols_to_full(g8["w_a_out"]), b_out=_cols_to_full(g8["w_b_out"]), mix=rows_full(g8["w_mix_out"]))
        if gi == 2:
            return dict(gate_t=rows_full(g8["w_ff_gate"]), up_t=rows_full(g8["w_ff_up"]))
        if gi == 3:
            return dict(down=rows_full(g8["w_ff_down"]))
        return dict(pg=rows_full(g8["w_ple_gate"]), pp=g8["w_ple_proj"])

    def scatter_start(names, gw):
        gi = len(grad_groups)
        grad_groups.append(names)
        full = {n: _blocks_from_in_t(gw["in_gates_t"], gw["in_rest_t"]) if n == "w_in" else gw[GRAD_OF[n]] for n in names}
        for n in names:
            if n in ("w_a_out", "w_b_out"):
                full[n] = _full_to_cols(full[n])
        parts = [full[n] if full[n].ndim == 3 else full[n].reshape(N_DEV, -1, full[n].shape[-1]) for n in names]
        if names == ["w_in"]:
            quarter = [jax.ShapeDtypeStruct((4,) + a.shape[1:], a.dtype) for a in parts]
            scatters[gi] = _copies_start(_sibling_plan, 4 * len(parts), parts, quarter, "scatter_sibling_start_%d" % gi)
        else:
            scatters[gi] = _copies_start(_scatter_plan, (N_DEV - 1) * len(parts), parts, parts, "scatter_start_%d" % gi)
        return scatters[gi][-1]

    def scatter_next(after):
        gi, names = len(grad_groups) - 1, grad_groups[-1]
        parts, from_sibling = _copies_wait(_sibling_plan, scatters[gi], len(names), "scatter_sibling_wait_%d" % gi, after)
        parts = [_pair_add(a, b, "scatter_add_%d_%s" % (gi, n)) for n, a, b in zip(names, parts, from_sibling)]
        scatters[gi] = _copies_start(_chip_plan, 3 * len(parts), parts, parts, "scatter_start_%d" % gi)
        return scatters[gi][-1]

    small = {n: wts[n].reshape(1, -1) for n in SMALL[2:]}

    loss_rows, grad_x, gs = _local_step(x[0], p[0, 0], loss_target[0], gather_start, gather_pass_on, gather_finish, scatter_start,
                                        scatter_next, small)
    gs["loss"] = jnp.sum(loss_rows).reshape(1, 1)

    small_shapes = [gs[n].shape for n in SMALL]
    gs_packed = _pack([gs[n] for n in SMALL + ["loss"]], 192)
    small_state = _copies_start(_everyone_plan, N_DEV - 1, [gs_packed], [jax.ShapeDtypeStruct((N_DEV,) + gs_packed.shape, F32)],
                                "small_start", (grad_x,))

    res, done = {}, (small_state[-1],)
    for gi, names in enumerate(grad_groups):
        plan, slot = (_chip_plan, me // 2) if names == ["w_in"] else (_scatter_plan, me)
        mine, got = _copies_wait(plan, scatters[gi], len(names), "scatter_wait_%d" % gi, done)
        for n, g, own in zip(names, got, mine):
            res[n] = _adamw_parts(wts[n], g, own, slot, mom[n], var[n], "adamw_" + n)
        done = tuple(res[n][1] for n in names)

    (gs_own,), (gs_got,) = _copies_wait(_everyone_plan, small_state, 1, "small_wait", done)
    gsum = dict(zip(SMALL + ["loss"], _unpack(_sum_parts(gs_got, gs_own, me, "small_sum"), small_shapes + [(1, 1)])))
    loss = gsum["loss"].reshape(())
    gsum["conv_w"] = lax.dynamic_index_in_dim(gsum["conv_w"].reshape(3, N_DEV, -1), me, axis=1, keepdims=False)
    gsum["w_alpha_up"] = lax.dynamic_index_in_dim(gsum["w_alpha_up"].reshape(GATE_RANK, N_DEV, -1), me, axis=1, keepdims=False)

    shard_shapes = [wts[n].shape for n in SMALL]
    packed = [_pack([d[n] for n in SMALL], 120) for d in (wts, gsum, mom, var)]
    outs = [_unpack(o, shard_shapes) for o in _adamw(*packed, "adamw_small")]
    for i, n in enumerate(SMALL):
        res[n] = [o[i] for o in outs]

    back = lambda n, a: (jnp.transpose(a) if n in TRANSPOSED else a)[None]
    return (loss, grad_x[None], *[back(n, res[n][i]) for i in range(4) for n in WEIGHTS])
```

```python
import functools

import jax
import jax.numpy as jnp
from jax import lax
from jax.experimental import pallas as pl
from jax.experimental.pallas import tpu as pltpu

F32, BF16 = jnp.float32, jnp.bfloat16
EPS = 1e-6
CHUNK = 64
STEP_CHUNKS = 2
STEP_ROWS = STEP_CHUNKS * CHUNK
HEADS, DK, DV = 4, 128, 256
GATE_RANK = 16
TAU = 16.0
LR, B1, B2, ADAM_EPS, WD, STEP = 0.001, 0.9, 0.999, 1e-08, 0.01, 10
N_DEV = 8
MESH = pl.DeviceIdType.MESH
VMEM_LIMIT = 56 * 1024 * 1024
ANY = pl.BlockSpec(memory_space=pl.ANY)

C_GA, C_GB = 0, 2048
C_AX, C_AB, C_AC, C_Q, C_K, C_V, C_OG, C_ALR = 0, 1024, 2048, 3072, 3584, 4096, 5120, 6144
REST = 6400
R_GA, R_END = 6160, 10256


def _params(sem):
    return pltpu.CompilerParams(dimension_semantics=sem, vmem_limit_bytes=VMEM_LIMIT)


def _pick(n, cands):
    for c in cands:
        if n % c == 0:
            return c
    return n


def _tiles(r, c):
    for tr in (128, 64):
        if r % tr == 0:
            return r // tr, (tr, c), lambda i: (i, 0)
    tc = _pick(c, (256, 128))
    return c // tc, (r, tc), lambda i: (0, i)


def _mm(a, b, mode, out_dtype, name, after=(), add=None, b3=False, out3=False, tm=None, tk=None, tn=None, n_outer=False):
    bshape = (b.shape[1], N_DEV * b.shape[2]) if b3 else b.shape
    if mode == "nn":
        (m, k), (k2, n) = a.shape, bshape
    elif mode == "nt":
        (m, k), (n, k2) = a.shape, bshape
    else:
        (k, m), (k2, n) = a.shape, bshape
    assert k == k2 and a.dtype == BF16 and b.dtype == BF16, (name, a.shape, b.shape, a.dtype, b.dtype)
    tm = tm if tm and m % tm == 0 else _pick(m, (2048, 1024, 512, 256))
    tn = tn if tn and n % tn == 0 else _pick(n, (1152, 1024, 1280, 1408, 512, 256))
    tk = min(tk, k) if tk and k % min(tk, k) == 0 else _pick(k, (2048, 1408, 1152, 1024, 896, 512, 256))
    if out3 or (b3 and mode == "nn"):
        tn = n // N_DEV
    if b3 and mode == "nt":
        tk = k // N_DEV
    nk = k // tk
    dims = {"nn": (((1,), (0,)), ((), ())), "nt": (((1,), (1,)), ((), ())), "tn": (((0,), (0,)), ((), ()))}[mode]
    n_extra = len(after) + (add is not None)

    def body(a_ref, b_ref, *rest):
        o_ref = rest[n_extra]
        prod = lax.dot_general(a_ref[...], b_ref[...], dims, preferred_element_type=F32)
        if nk == 1:
            o_ref[...] = (prod if add is None else prod + rest[0][...]).astype(o_ref.dtype)
            return
        acc_ref = rest[n_extra + 1]
        kk = pl.program_id(2)

        @pl.when(kk == 0)
        def _():
            acc_ref[...] = prod if add is None else prod + rest[0][...]

        @pl.when((kk > 0) & (kk < nk - 1))
        def _():
            acc_ref[...] += prod

        @pl.when(kk == nk - 1)
        def _():
            o_ref[...] = (acc_ref[...] + prod).astype(o_ref.dtype)

    def spec(block, at):
        return pl.BlockSpec(block, (lambda g0, g1, kk: at(g1, g0, kk)) if n_outer else at)

    a_spec = spec((tk, tm), lambda i, j, kk: (kk, i)) if mode == "tn" else spec((tm, tk), lambda i, j, kk: (i, kk))
    if b3:
        b_spec = spec((None, tn, tk), lambda i, j, kk: (kk, j, 0)) if mode == "nt" else spec((None, tk, tn), lambda i, j, kk: (j, kk, 0))
    else:
        b_spec = spec((tn, tk), lambda i, j, kk: (j, kk)) if mode == "nt" else spec((tk, tn), lambda i, j, kk: (kk, j))
    tile = spec((tm, tn), lambda i, j, kk: (i, j))
    out_spec = spec((None, tm, tn), lambda i, j, kk: (j, i, 0)) if out3 else tile
    return pl.pallas_call(
        body, name=name, grid=(n // tn, m // tm, nk) if n_outer else (m // tm, n // tn, nk),
        in_specs=[a_spec, b_spec] + ([tile] if add is not None else []) + [ANY] * len(after), out_specs=out_spec,
        out_shape=jax.ShapeDtypeStruct((N_DEV, m, tn) if out3 else (m, n), out_dtype),
        scratch_shapes=[pltpu.VMEM((tm, tn), F32)] if nk > 1 else [],
        compiler_params=_params(("parallel", "parallel", "arbitrary")),
    )(a, b, *([add] if add is not None else []), *after)


LONG_K = dict(tm=512, tk=8192, n_outer=True)
WIDE_N = dict(tm=512, tn=2816, n_outer=True)
SQUARE = dict(tm=512, n_outer=True)


def _rows(body, t, tr, ins, outs, name, after=()):
    in_specs = []
    for arr, sp in ins:
        if sp[0] == "t":
            in_specs.append(pl.BlockSpec((tr, sp[1]), lambda i, cb=sp[2]: (i, cb)))
        else:
            in_specs.append(pl.BlockSpec(arr.shape, lambda i, nd=arr.ndim: (0,) * nd))
    out_specs, out_shape = [], []
    for shape, dt, kind in outs:
        out_specs.append(pl.BlockSpec((tr, shape[1]), lambda i: (i, 0)) if kind == "t" else pl.BlockSpec(shape, lambda i: (0, 0)))
        out_shape.append(jax.ShapeDtypeStruct(shape, dt))
    return pl.pallas_call(
        body, name=name, grid=(t // tr,), in_specs=in_specs + [ANY] * len(after), out_specs=out_specs, out_shape=out_shape,
        compiler_params=_params(("arbitrary",)),
    )(*[arr for arr, _ in ins], *after)


def _rinv(v):
    return lax.rsqrt(jnp.mean(v * v, axis=-1, keepdims=True) + EPS)


def _sig(v):
    return 1.0 / (1.0 + jnp.exp(-v))


def _acc(ref, val):
    @pl.when(pl.program_id(0) == 0)
    def _():
        ref[...] = jnp.zeros_like(ref)

    ref[...] += jnp.sum(val, axis=0, keepdims=True)


def _rms_fwd(x, g, name):
    t, d = x.shape

    def body(x_ref, g_ref, h_ref):
        xv = x_ref[...]
        h_ref[...] = (xv * _rinv(xv) * g_ref[...]).astype(BF16)

    return _rows(body, t, 256, [(x, ("t", d, 0)), (g, ("b",))], [((t, d), BF16, "t")], name)[0]


def _post_pre(x, m, g_post, g_pre, name, after=()):
    t, d = x.shape

    def body(x_ref, m_ref, gp_ref, gn_ref, *rest):
        xo_ref, h_ref = rest[len(after):]
        mv = m_ref[...]
        xn = x_ref[...] + mv * _rinv(mv) * gp_ref[...]
        xo_ref[...] = xn
        h_ref[...] = (xn * _rinv(xn) * gn_ref[...]).astype(BF16)

    return _rows(body, t, 256, [(x, ("t", d, 0)), (m, ("t", d, 0)), (g_post, ("b",)), (g_pre, ("b",))],
                 [((t, d), F32, "t"), ((t, d), BF16, "t")], name, after)


def _mix_fwd(proj, ya, yb, name):
    t, d = ya.shape

    def body(ga_ref, gb_ref, ya_ref, yb_ref, o_ref):
        o_ref[...] = (_sig(ga_ref[...].astype(F32)) * ya_ref[...].astype(F32)
                      + _sig(gb_ref[...].astype(F32)) * yb_ref[...].astype(F32)).astype(BF16)

    return _rows(body, t, 256, [(proj, ("t", d, C_GA // d)), (proj, ("t", d, C_GB // d)), (ya, ("t", d, 0)), (yb, ("t", d, 0))],
                 [((t, d), BF16, "t")], name)[0]


def _mix_bwd(dmix, proj, ya, yb, name):
    t, d = ya.shape

    def body(dm_ref, ga_ref, gb_ref, ya_ref, yb_ref, dg_ref, dya_ref, dyb_ref):
        dm = dm_ref[...]
        sa, sb = _sig(ga_ref[...].astype(F32)), _sig(gb_ref[...].astype(F32))
        dg_ref[:, :d] = (dm * ya_ref[...].astype(F32) * sa * (1.0 - sa)).astype(BF16)
        dg_ref[:, d:] = (dm * yb_ref[...].astype(F32) * sb * (1.0 - sb)).astype(BF16)
        dya_ref[...] = (dm * sa).astype(BF16)
        dyb_ref[...] = (dm * sb).astype(BF16)

    return _rows(body, t, 256,
                 [(dmix, ("t", d, 0)), (proj, ("t", d, C_GA // d)), (proj, ("t", d, C_GB // d)), (ya, ("t", d, 0)), (yb, ("t", d, 0))],
                 [((t, 2 * d), BF16, "t"), ((t, d), BF16, "t"), ((t, d), BF16, "t")], name)


def _swiglu_call(body, ins, n_out, name):
    t, f = ins[0].shape
    tc = _pick(f, (1408, 512))
    tile = pl.BlockSpec((512, tc), lambda i, j: (i, j))
    return pl.pallas_call(
        body, name=name, grid=(t // 512, f // tc), in_specs=[tile] * len(ins), out_specs=[tile] * n_out,
        out_shape=[jax.ShapeDtypeStruct((t, f), BF16)] * n_out, compiler_params=_params(("parallel", "parallel")),
    )(*ins)


def _swiglu_fwd(fg, fu, name):
    def body(g_ref, u_ref, s_ref):
        gv = g_ref[...].astype(F32)
        s_ref[...] = (gv * _sig(gv) * u_ref[...].astype(F32)).astype(BF16)

    return _swiglu_call(body, [fg, fu], 1, name)[0]


def _swiglu_bwd(ds, fg, fu, name):
    def body(ds_ref, g_ref, u_ref, dg_ref, du_ref):
        dsv, gv, uv = ds_ref[...].astype(F32), g_ref[...].astype(F32), u_ref[...].astype(F32)
        sg = _sig(gv)
        dg_ref[...] = (dsv * uv * sg * (1.0 + gv * (1.0 - sg))).astype(BF16)
        du_ref[...] = (dsv * gv * sg).astype(BF16)

    return _swiglu_call(body, [ds, fg, fu], 2, name)


def _ple_final(x2, pg, pp, tgt, g_post, name):
    t, d = x2.shape

    def body(x_ref, pg_ref, pp_ref, t_ref, g_ref, loss_ref, d3_ref, dpg_ref, dpp_ref, dg_ref):
        sg, ppv, g = _sig(pg_ref[...]), pp_ref[...], g_ref[...]
        e = sg * ppv
        r = _rinv(e)
        eh = e * r
        diff = x_ref[...] + eh * g - t_ref[...]
        loss_ref[...] = 0.5 * jnp.mean(diff * diff, axis=-1, keepdims=True)
        d3 = diff * (1.0 / d)
        d3_ref[...] = d3
        gd = d3 * g
        de = r * (gd - eh * jnp.mean(gd * eh, axis=-1, keepdims=True))
        dpg_ref[...] = (de * ppv * sg * (1.0 - sg)).astype(BF16)
        dpp_ref[...] = (de * sg).astype(BF16)
        _acc(dg_ref, d3 * eh)

    return _rows(body, t, 256, [(x2, ("t", d, 0)), (pg, ("t", d, 0)), (pp, ("t", d, 0)), (tgt, ("t", d, 0)), (g_post, ("b",))],
                 [((t, 1), F32, "t"), ((t, d), F32, "t"), ((t, d), BF16, "t"), ((t, d), BF16, "t"), ((1, d), F32, "a")], name)


def _norm_bwd(dn, dh, x, g_pre, fm, g_post, name):
    t, d = x.shape
    two = fm is not None

    def body(*refs):
        if two:
            dn_ref, dh_ref, x_ref, gp_ref, f_ref, gq_ref, dx_ref, df_ref, dgp_ref, dgq_ref = refs
        else:
            dn_ref, dh_ref, x_ref, gp_ref, dx_ref, dgp_ref = refs
        xv, dhv = x_ref[...], dh_ref[...]
        r = _rinv(xv)
        xh = xv * r
        gd = dhv * gp_ref[...]
        dx = dn_ref[...] + r * (gd - xh * jnp.mean(gd * xh, axis=-1, keepdims=True))
        dx_ref[...] = dx
        _acc(dgp_ref, dhv * xh)
        if two:
            fv = f_ref[...]
            rf = _rinv(fv)
            fh = fv * rf
            gd2 = dx * gq_ref[...]
            df_ref[...] = (rf * (gd2 - fh * jnp.mean(gd2 * fh, axis=-1, keepdims=True))).astype(BF16)
            _acc(dgq_ref, dx * fh)

    ins = [(dn, ("t", d, 0)), (dh, ("t", d, 0)), (x, ("t", d, 0)), (g_pre, ("b",))]
    outs = [((t, d), F32, "t")]
    if two:
        ins += [(fm, ("t", d, 0)), (g_post, ("b",))]
        outs += [((t, d), BF16, "t"), ((1, d), F32, "a"), ((1, d), F32, "a")]
    else:
        outs += [((1, d), F32, "a")]
    return _rows(body, t, 256, ins, outs, name)


CONV_TC = 256


def _shift_down(v, s):
    rows = lax.broadcasted_iota(jnp.int32, v.shape, 0)
    return jnp.where(rows >= s, pltpu.roll(v, s, 0), 0.0)


def _shift_up(v, s):
    n = v.shape[0]
    rows = lax.broadcasted_iota(jnp.int32, v.shape, 0)
    return jnp.where(rows < n - s, pltpu.roll(v, n - s, 0), 0.0)


def _conv_specs(t):
    nb = 1024 // CONV_TC
    seg = lambda c0: pl.BlockSpec((t, CONV_TC), lambda j, cb=c0 // CONV_TC: (0, cb + j))
    own = pl.BlockSpec((t, CONV_TC), lambda j: (0, j))
    wspec = pl.BlockSpec((3, CONV_TC), lambda j: (0, j))
    return nb, seg, own, wspec


def _conv_fwd(proj, conv_w, name, after=()):
    t = proj.shape[0]
    nb, seg, own, wspec = _conv_specs(t)

    def body(ax_ref, ab_ref, ac_ref, w_ref, *rest):
        za_ref = rest[len(after)]
        u = ac_ref[...].astype(F32) * ax_ref[...].astype(F32)
        w = w_ref[...]
        yc = w[0:1] * _shift_down(u, 2) + w[1:2] * _shift_down(u, 1) + w[2:3] * u
        za_ref[...] = (ab_ref[...].astype(F32) * yc).astype(BF16)

    return pl.pallas_call(
        body, name=name, grid=(nb,), in_specs=[seg(C_AX), seg(C_AB), seg(C_AC), wspec] + [ANY] * len(after), out_specs=own,
        out_shape=jax.ShapeDtypeStruct((t, 1024), BF16), compiler_params=_params(("parallel",)),
    )(proj, proj, proj, conv_w, *after)


def _conv_bwd(dza, proj, conv_w, name):
    t = proj.shape[0]
    nb, seg, own, wspec = _conv_specs(t)

    def body(dz_ref, ax_ref, ab_ref, ac_ref, w_ref, dax_ref, dab_ref, dac_ref, dw_ref):
        ax, ab, ac, dz = ax_ref[...].astype(F32), ab_ref[...].astype(F32), ac_ref[...].astype(F32), dz_ref[...].astype(F32)
        w = w_ref[...]
        u = ac * ax
        u1, u2 = _shift_down(u, 1), _shift_down(u, 2)
        yc = w[0:1] * u2 + w[1:2] * u1 + w[2:3] * u
        dab_ref[...] = (dz * yc).astype(BF16)
        dyc = dz * ab
        du = w[2:3] * dyc + w[1:2] * _shift_up(dyc, 1) + w[0:1] * _shift_up(dyc, 2)
        dax_ref[...] = (du * ac).astype(BF16)
        dac_ref[...] = (du * ax).astype(BF16)
        dw_ref[0:1, :] = jnp.sum(dyc * u2, axis=0, keepdims=True)
        dw_ref[1:2, :] = jnp.sum(dyc * u1, axis=0, keepdims=True)
        dw_ref[2:3, :] = jnp.sum(dyc * u, axis=0, keepdims=True)

    act = jax.ShapeDtypeStruct((t, 1024), BF16)
    return pl.pallas_call(
        body, name=name, grid=(nb,), in_specs=[own, seg(C_AX), seg(C_AB), seg(C_AC), wspec], out_specs=[own, own, own, wspec],
        out_shape=[act, act, act, jax.ShapeDtypeStruct((3, 1024), F32)], compiler_params=_params(("parallel",)),
    )(dza, proj, proj, proj, conv_w)


def _dot(a, b, dims, precision=None):
    return lax.dot_general(a, b, (dims, ((), ())), precision=precision, preferred_element_type=F32)


_CONTRACT = {"nn": ((1,), (0,)), "nt": ((1,), (1,)), "tn": ((0,), (0,))}


def _bdot_raw(a, b, mode):
    return _dot(a.astype(BF16), b.astype(BF16), _CONTRACT[mode])


@functools.partial(jax.custom_vjp, nondiff_argnums=(2,))
def _bdot(a, b, mode):
    return _bdot_raw(a, b, mode)


def _bdot_fwd(a, b, mode):
    return _bdot_raw(a, b, mode), (a, b)


def _bdot_bwd(mode, res, ct):
    a, b = res
    if mode == "nn":
        return _bdot_raw(ct, b, "nt"), _bdot_raw(a, ct, "tn")
    if mode == "nt":
        return _bdot_raw(ct, b, "nn"), _bdot_raw(ct, a, "tn")
    return _bdot_raw(b, ct, "nt"), _bdot_raw(a, ct, "nn")


_bdot.defvjp(_bdot_fwd, _bdot_bwd)


@functools.partial(jax.custom_vjp, nondiff_argnums=(2,))
def _sum_dot(ones, x, mode):
    head = x.astype(BF16)
    tail = x - head.astype(F32)
    if mode == "nn":
        return _bdot_raw(ones, head, "nn") + _bdot_raw(ones, tail, "nn")
    return _bdot_raw(head, ones, "tn") + _bdot_raw(tail, ones, "tn")


def _sum_dot_fwd(ones, x, mode):
    return _sum_dot(ones, x, mode), ones


def _sum_dot_bwd(mode, ones, ct):
    return jnp.zeros_like(ones), (_bdot_raw(ones, ct, "tn") if mode == "nn" else _bdot_raw(ones, ct, "nt"))


_sum_dot.defvjp(_sum_dot_fwd, _sum_dot_bwd)


def _gla_chunk(q, k, v, og, alr, s_in, wa, ba, gain):
    c = q.shape[0]
    z =_bdot(alr, wa, "nn") + ba
    la = (jnp.minimum(z, 0.0) - jnp.log(1.0 + jnp.exp(-jnp.abs(z)))) * (1.0 / TAU)
    row = lax.broadcasted_iota(jnp.int32, (c, c), 0)
    col = lax.broadcasted_iota(jnp.int32, (c, c), 1)
    lower = row >= col
    b = _sum_dot(lower.astype(F32), la, "nn")
    trow = lax.broadcasted_iota(jnp.int32, la.shape, 0)
    mid = jnp.sum(jnp.where(trow <= c // 2, la, 0.0), axis=0, keepdims=True)
    blast = jnp.sum(la, axis=0, keepdims=True)
    qs = q * (DK ** -0.5)
    e_up, e_dn = jnp.exp(b - mid), jnp.exp(mid - b)
    a_fwd = _bdot(qs * e_up, k * e_dn, "nt")
    a_rev = _bdot(qs * e_dn, k * e_up, "nt")
    att = jnp.where(lower, a_fwd, a_rev)
    o = _bdot(att, v, "nn") + _bdot(qs * jnp.exp(b), s_in, "nn")
    upd = _bdot(k * jnp.exp(blast - b), v, "tn")
    blast_col = _sum_dot(jnp.ones((c, DV), F32), la, "tn")
    s_out = jnp.exp(blast_col) * s_in + upd
    on = o * _rinv(o) * gain
    return on * og * _sig(og), s_out


def _gla_specs(t, rev):
    n = t // STEP_ROWS
    ch = (lambda i: n - 1 - i) if rev else (lambda i: i)
    col = lambda w, c0: pl.BlockSpec((STEP_ROWS, HEADS * w), lambda i, cb=c0 // (HEADS * w): (ch(i), cb))
    whole = lambda shape: pl.BlockSpec(shape, lambda i, nd=len(shape): (0,) * nd)
    specs = dict(
        q=col(DK, C_Q), k=col(DK, C_K), v=col(DV, C_V), og=col(DV, C_OG),
        alr=pl.BlockSpec((STEP_ROWS, 128), lambda i: (ch(i), C_ALR // 128)),
        wa=whole((128, HEADS * DK)), ba=whole((1, HEADS * DK)), gain=whole((1, DV)),
        state=pl.BlockSpec((STEP_CHUNKS, HEADS, DK, DV), lambda i: (ch(i), 0, 0, 0)),
        odk=pl.BlockSpec((STEP_ROWS, HEADS * DK), lambda i: (ch(i), 0)), odv=pl.BlockSpec((STEP_ROWS, HEADS * DV), lambda i: (ch(i), 0)),
        oalr=pl.BlockSpec((STEP_ROWS, 128), lambda i: (ch(i), 0)), whole=whole,
    )
    return n, specs


def _head_cols(h):
    return slice(h * DK, (h + 1) * DK), slice(h * DV, (h + 1) * DV)


def _gla_fwd(proj, wa, ba, gain, name):
    t = proj.shape[0]
    n, sp = _gla_specs(t, False)

    def body(q_ref, k_ref, v_ref, og_ref, alr_ref, wa_ref, ba_ref, g_ref, zb_ref, st_ref, s_scr):
        @pl.when(pl.program_id(0) == 0)
        def _():
            s_scr[...] = jnp.zeros_like(s_scr)

        state = [s_scr[h] for h in range(HEADS)]
        for c in range(STEP_CHUNKS):
            rows = slice(c * CHUNK, (c + 1) * CHUNK)
            alr = alr_ref[rows, :].astype(F32)
            for h in range(HEADS):
                kc, vc = _head_cols(h)
                st_ref[c, h] = state[h]
                zb, state[h] = _gla_chunk(q_ref[rows, kc].astype(F32), k_ref[rows, kc].astype(F32), v_ref[rows, vc].astype(F32),
                                          og_ref[rows, vc].astype(F32), alr, state[h], wa_ref[:, kc].astype(F32), ba_ref[:, kc], g_ref[...])
                zb_ref[rows, vc] = zb.astype(BF16)
        for h in range(HEADS):
            s_scr[h] = state[h]

    return pl.pallas_call(
        body, name=name, grid=(n,),
        in_specs=[sp["q"], sp["k"], sp["v"], sp["og"], sp["alr"], sp["wa"], sp["ba"], sp["gain"]],
        out_specs=[sp["odv"], sp["state"]],
        out_shape=[jax.ShapeDtypeStruct((t, HEADS * DV), BF16), jax.ShapeDtypeStruct((t // CHUNK, HEADS, DK, DV), F32)],
        scratch_shapes=[pltpu.VMEM((HEADS, DK, DV), F32)],
        compiler_params=_params(("arbitrary",)),
    )(proj, proj, proj, proj, proj, wa, ba, gain)


def _gla_bwd(dzb, proj, states, wa, ba, gain, name):
    t = proj.shape[0]
    n, sp = _gla_specs(t, True)

    def body(dz_ref, q_ref, k_ref, v_ref, og_ref, alr_ref, st_ref, wa_ref, ba_ref, g_ref,
             dq_ref, dk_ref, dv_ref, dog_ref, dalr_ref, dwa_ref, dba_ref, dg_ref, ds_scr):
        @pl.when(pl.program_id(0) == 0)
        def _():
            ds_scr[...] = jnp.zeros_like(ds_scr)
            dwa_ref[...] = jnp.zeros_like(dwa_ref)
            dba_ref[...] = jnp.zeros_like(dba_ref)
            dg_ref[...] = jnp.zeros_like(dg_ref)

        dstate = [ds_scr[h] for h in range(HEADS)]
        dwa_sum, dba_sum, dgain_sum = [None] * HEADS, [None] * HEADS, None
        for c in reversed(range(STEP_CHUNKS)):
            rows = slice(c * CHUNK, (c + 1) * CHUNK)
            alr = alr_ref[rows, :].astype(F32)
            dalr_sum = None
            for h in range(HEADS):
                kc, vc = _head_cols(h)
                args = (q_ref[rows, kc].astype(F32), k_ref[rows, kc].astype(F32), v_ref[rows, vc].astype(F32), og_ref[rows, vc].astype(F32),
                        alr, st_ref[c, h], wa_ref[:, kc].astype(F32), ba_ref[:, kc], g_ref[...])
                _, vjp = jax.vjp(_gla_chunk, *args)
                dq, dk, dv, dog, dalr, dstate[h], dwa, dba, dgain = vjp((dz_ref[rows, vc].astype(F32), dstate[h]))
                dq_ref[rows, kc] = dq.astype(BF16)
                dk_ref[rows, kc] = dk.astype(BF16)
                dv_ref[rows, vc] = dv.astype(BF16)
                dog_ref[rows, vc] = dog.astype(BF16)
                dwa_sum[h] = dwa if dwa_sum[h] is None else dwa_sum[h] + dwa
                dba_sum[h] = dba if dba_sum[h] is None else dba_sum[h] + dba
                dalr_sum = dalr if dalr_sum is None else dalr_sum + dalr
                dgain_sum = dgain if dgain_sum is None else dgain_sum + dgain
            dalr_ref[rows, :] = dalr_sum
        for h in range(HEADS):
            ds_scr[h] = dstate[h]
            dwa_ref[h] += dwa_sum[h]
            dba_ref[h] += dba_sum[h]
        dg_ref[...] += dgain_sum

    whole = sp["whole"]
    return pl.pallas_call(
        body, name=name, grid=(n,),
        in_specs=[sp["odv"], sp["q"], sp["k"], sp["v"], sp["og"], sp["alr"], sp["state"], sp["wa"], sp["ba"], sp["gain"]],
        out_specs=[sp["odk"], sp["odk"], sp["odv"], sp["odv"], sp["oalr"], whole((HEADS, 128, DK)), whole((HEADS, 1, DK)), whole((1, DV))],
        out_shape=[jax.ShapeDtypeStruct((t, HEADS * DK), BF16), jax.ShapeDtypeStruct((t, HEADS * DK), BF16),
                   jax.ShapeDtypeStruct((t, HEADS * DV), BF16), jax.ShapeDtypeStruct((t, HEADS * DV), BF16),
                   jax.ShapeDtypeStruct((t, 128), F32), jax.ShapeDtypeStruct((HEADS, 128, DK), F32),
                   jax.ShapeDtypeStruct((HEADS, 1, DK), F32), jax.ShapeDtypeStruct((1, DV), F32)],
        scratch_shapes=[pltpu.VMEM((HEADS, DK, DV), F32)],
        compiler_params=_params(("arbitrary",)),
    )(dzb, proj, proj, proj, proj, proj, states, wa, ba, gain)


def _local_step(x, p, tgt, gather_start, gather_pass_on, gather_finish, scatter_start, scatter_next, small):
    b_alpha, gain = small["b_alpha_up"], small["gla_head_gain"]
    gather_start(0, ())
    w = dict(gather_finish(0, ()))
    conv_w, w_alpha = w["conv_w"], w["w_alpha_up"]
    wa_p = jnp.zeros((128, HEADS * DK), BF16).at[:GATE_RANK].set(w_alpha.astype(BF16))

    t2 = gather_start(2, (w["in_rest_t"], gather_start(1, ())))
    h1 = _rms_fwd(x, small["g_pre_mix"], "rms_pre_mix")
    proj = _mm(h1, w["in_rest_t"], "nt", BF16, "mm_proj", after=(t2,))
    proj_gates = _mm(h1, w["in_gates_t"], "nt", BF16, "mm_proj_gates", after=(t2,))
    za = _conv_fwd(proj, conv_w, "conv_fwd", after=(gather_pass_on(1, (proj,)),))
    zb, states = _gla_fwd(proj, wa_p, b_alpha, gain, "gla_fwd")
    t3 = gather_start(3, (zb, za))
    w.update(gather_finish(1, (t3,)))
    ya = _mm(za, w["a_out"], "nn", BF16, "mm_ya")
    yb = _mm(zb, w["b_out"], "nn", BF16, "mm_yb")
    mix = _mix_fwd(proj_gates, ya, yb, "mix_fwd")
    m2 = _mm(mix, w["mix"], "nn", F32, "mm_mix", **SQUARE)
    t4 = gather_start(4, (m2,))
    x1, h2 = _post_pre(x, m2, small["g_post_mix"], small["g_pre_ffn"], "norm_mix_ffn", after=(gather_pass_on(2, (m2,)),))
    w.update(gather_finish(2, (h2, t4)))
    fu = _mm(h2, w["up_t"], "nt", BF16, "mm_up", **WIDE_N)
    fg = _mm(h2, w["gate_t"], "nt", BF16, "mm_gate", after=(gather_pass_on(3, (fu,)),), **WIDE_N)
    s = _swiglu_fwd(fg, fu, "swiglu_fwd")
    w.update(gather_finish(3, (s,)))
    f = _mm(s, w["down"], "nn", F32, "mm_down", after=(gather_pass_on(4, (s,)),), **LONG_K)
    x2, h3 = _post_pre(x1, f, small["g_post_ffn"], small["g_pre_ple"], "norm_ffn_ple")
    w.update(gather_finish(4, (h3,)))
    pg = _mm(h3, w["pg"], "nn", F32, "mm_pg", **SQUARE)
    p_bf = p.astype(BF16)
    pp = _mm(p_bf, w["pp"], "nn", F32, "mm_pp", b3=True, tm=2048)
    loss_rows, d3, dpg, dpp, dg_post_ple = _ple_final(x2, pg, pp, tgt, small["g_post_ple"], "ple_final")

    gw = {}
    gw["pp"] = _mm(p_bf, dpp, "tn", BF16, "mm_dw_pp", out3=True)
    gw["pg"] = _mm(h3, dpg, "tn", BF16, "mm_dw_pg", **SQUARE)
    dh3 = _mm(dpg, w["pg"], "nt", F32, "mm_dh3", after=(scatter_start(["w_ple_proj", "w_ple_gate"], gw),), **SQUARE)
    d2, df, dg_pre_ple, dg_post_ffn = _norm_bwd(d3, dh3, x2, small["g_pre_ple"], f, small["g_post_ffn"], "norm_bwd_ple_ffn")
    gw["down"] = _mm(s, df, "tn", BF16, "mm_dw_down", tm=1408)
    ds = _mm(df, w["down"], "nt", BF16, "mm_ds", after=(scatter_start(["w_ff_down"], gw),), **WIDE_N)
    dfg, dfu = _swiglu_bwd(ds, fg, fu, "swiglu_bwd")
    gw["gate_t"] = _mm(dfg, h2, "tn", BF16, "mm_dw_gate", tm=1408)
    gw["up_t"] = _mm(dfu, h2, "tn", BF16, "mm_dw_up", after=(gw["gate_t"],), tm=1408)
    dh2 = _mm(dfg, w["gate_t"], "nn", F32, "mm_dh2_gate", after=(scatter_start(["w_ff_gate", "w_ff_up"], gw),), **LONG_K)
    dh2 = _mm(dfu, w["up_t"], "nn", F32, "mm_dh2_up", add=dh2, **LONG_K)
    d1, dm2, dg_pre_ffn, dg_post_mix = _norm_bwd(d2, dh2, x1, small["g_pre_ffn"], m2, small["g_post_mix"], "norm_bwd_ffn_mix")
    dmix = _mm(dm2, w["mix"], "nt", F32, "mm_dmix", **SQUARE)
    gw["mix"] = _mm(mix, dm2, "tn", BF16, "mm_dw_mix", **SQUARE)
    dgab, dya, dyb = _mix_bwd(dmix, proj_gates, ya, yb, "mix_bwd")
    gw["in_gates_t"] = _mm(dgab, h1, "tn", BF16, "mm_dw_in_gates")
    dza = _mm(dya, w["a_out"], "nt", BF16, "mm_dza", after=(scatter_start(["w_mix_out"], gw), gw["in_gates_t"]))
    gw["a_out"] = _mm(za, dya, "tn", BF16, "mm_dw_a_out")
    gw["b_out"] = _mm(zb, dyb, "tn", BF16, "mm_dw_b_out", after=(gw["a_out"],))
    dzb = _mm(dyb, w["b_out"], "nt", BF16, "mm_dzb", after=(scatter_start(["w_a_out", "w_b_out"], gw),))
    dax, dab, dac, dconv = _conv_bwd(dza, proj, conv_w, "conv_bwd")
    dq, dk, dv, dog, dalr, dwa, dba, dgain = _gla_bwd(dzb, proj, states, wa_p, b_alpha, gain, "gla_bwd")
    pad = jnp.zeros((dax.shape[0], REST - C_ALR - 128), BF16)
    drest = jnp.concatenate([dax, dab, dac, dq, dk, dv, dog, dalr.astype(BF16), pad], axis=1)
    gw["in_rest_t"] = _mm(drest, h1, "tn", BF16, "mm_dw_in_rest", tm=1280)
    dh1 = _mm(dgab, w["in_gates_t"], "nn", F32, "mm_dh1_gates", after=(scatter_start(["w_in"], gw),), tm=1024, tk=4096, n_outer=True)
    dh1 = _mm(drest, w["in_rest_t"], "nn", F32, "mm_dh1_rest", add=dh1, after=(scatter_next((dh1,)),), **LONG_K)
    grad_x, dg_pre_mix = _norm_bwd(d1, dh1, x, small["g_pre_mix"], None, None, "norm_bwd_mix")

    gs = dict(
        conv_w=dconv,
        w_alpha_up=jnp.transpose(dwa[:, :GATE_RANK, :], (1, 0, 2)).reshape(GATE_RANK, HEADS * DK),
        b_alpha_up=dba.reshape(1, HEADS * DK), gla_head_gain=dgain,
        g_pre_mix=dg_pre_mix, g_post_mix=dg_post_mix, g_pre_ffn=dg_pre_ffn, g_post_ffn=dg_post_ffn,
        g_pre_ple=dg_pre_ple, g_post_ple=dg_post_ple,
    )
    return loss_rows, grad_x, gs


def _place():
    x, y, c = lax.axis_index("x"), lax.axis_index("y"), lax.axis_index("c")
    return x, y, c, [(1 - x, y), (x, 1 - y), (1 - x, 1 - y)]


def _pair_add(mine8, got4, name):
    _, r, cols = mine8.shape
    steps, blk, at = _tiles(r, cols)
    core = lax.axis_index("c").astype(jnp.int32).reshape(1)

    def body(c_ref, a_ref, b_ref, o_ref):
        o_ref[...] = (a_ref[...].astype(F32) + b_ref[...].astype(F32)).astype(BF16)

    return pl.pallas_call(
        body, name=name,
        grid_spec=pltpu.PrefetchScalarGridSpec(
            num_scalar_prefetch=1, grid=(4, steps),
            in_specs=[pl.BlockSpec((None,) + blk, lambda ch, i, c_ref: (2 * ch + c_ref[0],) + at(i)),
                      pl.BlockSpec((None,) + blk, lambda ch, i, c_ref: (ch,) + at(i))],
            out_specs=pl.BlockSpec((None,) + blk, lambda ch, i, c_ref: (ch,) + at(i))),
        out_shape=jax.ShapeDtypeStruct((4, r, cols), BF16),
        compiler_params=_params(("parallel", "parallel")),
    )(core, mine8, got4)


HBM = pl.BlockSpec(memory_space=pltpu.HBM)
SEM = pl.BlockSpec(memory_space=pltpu.SEMAPHORE)
EFFECT = pltpu.SideEffectType.DATAFLOW_SIDE_EFFECTING


def _in_hbm(a):
    return pltpu.with_memory_space_constraint(a, pltpu.HBM)


def _remote_copies(plan, srcs, lands, send_sems, recv_sems):
    return [pltpu.make_async_remote_copy(src_ref=s, dst_ref=d, send_sem=send_sems.at[i], recv_sem=recv_sems.at[i], device_id=peer,
                                         device_id_type=MESH) for i, (s, d, peer) in enumerate(plan(srcs, lands))]


def _copies_start(plan, n_copies, srcs, land_shapes, name, after=()):
    ns, nl = len(srcs), len(land_shapes)

    def body(*refs):
        send_sems, recv_sems = refs[ns + nl + len(after):ns + nl + len(after) + 2]
        for cp in _remote_copies(plan, refs[:ns], refs[ns:ns + nl], send_sems, recv_sems):
            cp.start()
        refs[-1][...] = jnp.zeros((8, 128), F32)

    sems = pltpu.SemaphoreType.DMA((n_copies,))
    return pl.pallas_call(
        body, name=name,
        out_shape=(sems, sems, *[pltpu.HBM(s.shape, s.dtype) for s in srcs], *[pltpu.HBM(s.shape, s.dtype) for s in land_shapes],
                   jax.ShapeDtypeStruct((8, 128), F32)),
        in_specs=[HBM] * (ns + nl) + [ANY] * len(after),
        out_specs=(SEM, SEM, *[HBM] * (ns + nl), pl.BlockSpec(memory_space=pltpu.VMEM)),
        input_output_aliases={i: 2 + i for i in range(ns + nl)},
        compiler_params=pltpu.CompilerParams(has_side_effects=EFFECT),
    )(*[_in_hbm(s) for s in srcs], *[_in_hbm(lax.empty(s.shape, s.dtype)) for s in land_shapes], *after)


def _copies_wait(plan, state, ns, name, after=()):
    send_sems, recv_sems, *arrs = state[:-1]
    n = len(arrs)

    def body(*refs):
        cps = _remote_copies(plan, refs[:ns], refs[ns:n], refs[n], refs[n + 1])
        for cp in cps:
            cp.wait_send()
        for cp in cps:
            cp.wait_recv()

    out = pl.pallas_call(
        body, name=name, out_shape=tuple(pltpu.HBM(a.shape, a.dtype) for a in arrs),
        in_specs=[HBM] * n + [SEM, SEM] + [ANY] * len(after), out_specs=tuple([HBM] * n),
        input_output_aliases={i: i for i in range(n)},
        compiler_params=pltpu.CompilerParams(has_side_effects=EFFECT),
    )(*arrs, send_sems, recv_sems, *after)
    return list(out[:ns]), list(out[ns:])


def _copies_relay(plan, state, ns, next_plan, n_next, name, after=()):
    send_sems, recv_sems, *arrs = state[:-1]
    n = len(arrs)

    def body(*refs):
        cps = _remote_copies(plan, refs[:ns], refs[ns:n], refs[n], refs[n + 1])
        for cp in cps:
            cp.wait_send()
        for cp in cps:
            cp.wait_recv()
        outs = refs[n + 2 + len(after):]
        for cp in _remote_copies(next_plan, refs[:ns], refs[ns:n], outs[0], outs[1]):
            cp.start()
        outs[-1][...] = jnp.zeros((8, 128), F32)

    sems = pltpu.SemaphoreType.DMA((n_next,))
    return pl.pallas_call(
        body, name=name,
        out_shape=(sems, sems, *[pltpu.HBM(a.shape, a.dtype) for a in arrs], jax.ShapeDtypeStruct((8, 128), F32)),
        in_specs=[HBM] * n + [SEM, SEM] + [ANY] * len(after),
        out_specs=(SEM, SEM, *[HBM] * n, pl.BlockSpec(memory_space=pltpu.VMEM)),
        input_output_aliases={i: 2 + i for i in range(n)},
        compiler_params=pltpu.CompilerParams(has_side_effects=EFFECT),
    )(*arrs, send_sems, recv_sems, *after)


def _pass_on_plan(srcs, lands):
    x, y, c, chips = _place()
    return [(l.at[4 * px + 2 * py + c], l.at[4 * px + 2 * py + c], (x, y, 1 - c)) for l in lands for px, py in chips]


def _gather_plan(srcs, lands):
    x, y, c, chips = _place()
    peers = [(x, y, 1 - c)] + [(*chip, c) for chip in chips]
    return [(s, l.at[4 * x + 2 * y + c], peer) for s, l in zip(srcs, lands) for peer in peers]


def _gather_plan_near(srcs, lands):
    x, y, c, _ = _place()
    peers = [(x, y, 1 - c), (1 - x, y, c), (x, 1 - y, c)]
    return [(s, l.at[4 * x + 2 * y + c], peer) for s, l in zip(srcs, lands) for peer in peers]


def _scatter_plan(srcs, lands):
    x, y, c, _ = _place()
    peers = [(1 - x if k & 4 else x, 1 - y if k & 2 else y, 1 - c if k & 1 else c) for k in range(1, N_DEV)]
    return [(s.at[4 * px + 2 * py + pc], l.at[4 * x + 2 * y + c], (px, py, pc)) for s, l in zip(srcs, lands) for px, py, pc in peers]


def _everyone_plan(srcs, lands):
    x, y, c, _ = _place()
    peers = [(1 - x if k & 4 else x, 1 - y if k & 2 else y, 1 - c if k & 1 else c) for k in range(1, N_DEV)]
    return [(s, l.at[4 * x + 2 * y + c], peer) for s, l in zip(srcs, lands) for peer in peers]


def _sum_parts(got, own, me, name):
    def body(me_ref, got_ref, own_ref, o_ref):
        acc = jnp.where(me_ref[0] == 0, own_ref[...], got_ref[0])
        for d in range(1, N_DEV):
            acc = acc + jnp.where(me_ref[0] == d, own_ref[...], got_ref[d])
        o_ref[...] = acc

    return pl.pallas_call(
        body, name=name,
        grid_spec=pltpu.PrefetchScalarGridSpec(
            num_scalar_prefetch=1, grid=(1,),
            in_specs=[pl.BlockSpec(got.shape, lambda i, me_ref: (0, 0, 0)), pl.BlockSpec(own.shape, lambda i, me_ref: (0, 0))],
            out_specs=pl.BlockSpec(own.shape, lambda i, me_ref: (0, 0))),
        out_shape=jax.ShapeDtypeStruct(own.shape, F32),
    )(me.astype(jnp.int32).reshape(1), got, own)


def _sibling_plan(srcs, lands):
    x, y, c, _ = _place()
    return [(s.at[2 * ch + 1 - c], l.at[ch], (x, y, 1 - c)) for s, l in zip(srcs, lands) for ch in range(4)]


def _chip_plan(srcs, lands):
    x, y, c, chips = _place()
    return [(s.at[2 * px + py], l.at[2 * x + y], (px, py, c)) for s, l in zip(srcs, lands) for px, py in chips]


def _put_own(shard, zone, me, name):
    r, c = shard.shape
    tr = r if r <= 256 else _pick(r, (256, 64))

    def body(me_ref, s_ref, z_ref, o_ref):
        o_ref[...] = s_ref[...]

    return pl.pallas_call(
        body, name=name,
        grid_spec=pltpu.PrefetchScalarGridSpec(
            num_scalar_prefetch=1, grid=(r // tr,),
            in_specs=[pl.BlockSpec((tr, c), lambda i, me_ref: (i, 0)), ANY],
            out_specs=pl.BlockSpec((None, tr, c), lambda i, me_ref: (me_ref[0], i, 0))),
        out_shape=jax.ShapeDtypeStruct(zone.shape, zone.dtype), input_output_aliases={2: 0},
        compiler_params=_params(("arbitrary",)),
    )(me.astype(jnp.int32).reshape(1), shard, zone)


def _gather_relay(lands, name):
    n = len(lands)

    def body(*refs):
        zones, outs = refs[:n], refs[n:2 * n]
        send_sems, recv_sems = refs[2 * n:]
        x, y, c, _ = _place()
        south = c == 0
        near_x, near_y, across = 4 * (1 - x) + 2 * y + c, 4 * x + 2 * (1 - y) + c, 4 * (1 - x) + 2 * (1 - y) + c
        passed = jnp.where(south, near_y, near_x)
        onward = (jnp.where(south, 1 - x, x), jnp.where(south, y, 1 - y), c)

        def copy(a, k, slot, to):
            return pltpu.make_async_remote_copy(src_ref=zones[a].at[slot], dst_ref=outs[a].at[slot], send_sem=send_sems.at[a, k],
                                                recv_sem=recv_sems.at[a, k], device_id=to, device_id_type=MESH)

        first = [copy(a, 0, passed, onward) for a in range(n)]
        first += [copy(a, 1 + j, slot, (x, y, 1 - c)) for j, slot in enumerate((near_x, near_y)) for a in range(n)]
        for cp in first:
            cp.start()
        last = []
        for a in range(n):
            copy(a, 0, across, onward).wait_recv()
            last.append(copy(a, 3, across, (x, y, 1 - c)))
            last[-1].start()
        for cp in first[n:] + last:
            cp.wait_recv()
        for cp in first + last:
            cp.wait_send()

    return pl.pallas_call(
        body, name=name, in_specs=[ANY] * n, out_specs=[ANY] * n,
        out_shape=[jax.ShapeDtypeStruct(l.shape, l.dtype) for l in lands],
        input_output_aliases={a: a for a in range(n)},
        scratch_shapes=[pltpu.SemaphoreType.DMA((n, 4)), pltpu.SemaphoreType.DMA((n, 4))],
    )(*lands)


def _adamw_parts(w, got, mine, me, m, v, name, after=()):
    r, c = w.shape
    n_parts = got.shape[0]
    steps, blk, at = _tiles(r, c)

    def body(me_ref, w_ref, got_ref, own_ref, m_ref, v_ref, *rest):
        go_ref, d_ref, mo_ref, vo_ref = rest[len(after):]
        own = own_ref[...].astype(F32)
        gv = jnp.where(me_ref[0] == 0, own, got_ref[0].astype(F32))
        for d in range(1, n_parts):
            gv = gv + jnp.where(me_ref[0] == d, own, got_ref[d].astype(F32))
        _adamw_math(gv, w_ref, m_ref, v_ref, go_ref, d_ref, mo_ref, vo_ref)

    tile = pl.BlockSpec(blk, lambda i, me_ref: at(i))
    out = jax.ShapeDtypeStruct((r, c), F32)
    return pl.pallas_call(
        body, name=name,
        grid_spec=pltpu.PrefetchScalarGridSpec(
            num_scalar_prefetch=1, grid=(steps,),
            in_specs=[tile, pl.BlockSpec((n_parts,) + blk, lambda i, me_ref: (0,) + at(i)),
                      pl.BlockSpec((None,) + blk, lambda i, me_ref: (me_ref[0],) + at(i)), tile, tile] + [ANY] * len(after),
            out_specs=[tile] * 4),
        out_shape=[out] * 4, compiler_params=_params(("parallel",)),
    )(me.astype(jnp.int32).reshape(1), w, got, mine, m, v, *after)


def _adamw_math(gv, w_ref, m_ref, v_ref, go_ref, d_ref, mo_ref, vo_ref):
    mn = B1 * m_ref[...] + (1.0 - B1) * gv
    vn = B2 * v_ref[...] + (1.0 - B2) * (gv * gv)
    m_hat = mn / (1.0 - B1 ** STEP)
    v_hat = vn / (1.0 - B2 ** STEP)
    go_ref[...] = gv
    d_ref[...] = -LR * (m_hat / (jnp.sqrt(v_hat) + ADAM_EPS) + WD * w_ref[...])
    mo_ref[...] = mn
    vo_ref[...] = vn


def _adamw(w, g, m, v, name):
    def body(w_ref, g_ref, m_ref, v_ref, go_ref, d_ref, mo_ref, vo_ref):
        _adamw_math(g_ref[...], w_ref, m_ref, v_ref, go_ref, d_ref, mo_ref, vo_ref)

    tile = pl.BlockSpec(w.shape, lambda i: (0, 0))
    out = jax.ShapeDtypeStruct(w.shape, F32)
    return pl.pallas_call(
        body, name=name, grid=(1,), in_specs=[tile] * 4, out_specs=[tile] * 4, out_shape=[out] * 4,
        compiler_params=_params(("parallel",)),
    )(w, g, m, v)


BIG = ["w_in", "w_a_out", "w_b_out", "w_mix_out", "w_ff_gate", "w_ff_up", "w_ff_down", "w_ple_gate", "w_ple_proj"]
TRANSPOSED = ["w_in", "w_ff_gate", "w_ff_up"]
GRAD_OF = dict(w_ple_proj="pp", w_ple_gate="pg", w_ff_down="down", w_ff_gate="gate_t", w_ff_up="up_t", w_mix_out="mix", w_a_out="a_out",
               w_b_out="b_out")
SMALL = ["conv_w", "w_alpha_up", "b_alpha_up", "gla_head_gain", "g_pre_mix", "g_post_mix", "g_pre_ffn", "g_post_ffn", "g_pre_ple", "g_post_ple"]
WEIGHTS = ["w_in", "conv_w", "w_a_out", "w_alpha_up", "b_alpha_up", "gla_head_gain", "w_b_out", "w_mix_out", "g_pre_mix", "g_post_mix",
           "g_pre_ffn", "g_post_ffn", "w_ff_gate", "w_ff_up", "w_ff_down", "g_pre_ple", "g_post_ple", "w_ple_gate", "w_ple_proj"]


def _in_t_from_blocks(z):
    w = z.reshape(-1, z.shape[-1])
    return w[R_GA:R_END], jnp.concatenate([w[:R_GA], jnp.zeros((REST - R_GA, w.shape[1]), w.dtype)], axis=0)


def _blocks_from_in_t(g_gates, g_rest):
    per = R_END // N_DEV

    def rows(lo, hi):
        out = []
        for n0, n1, g in ((0, R_GA, g_rest), (R_GA, R_END, g_gates)):
            a, e = max(lo, n0), min(hi, n1)
            if a < e:
                out.append(g[a - n0:e - n0])
        return out

    return jnp.stack([jnp.concatenate(rows(b * per, (b + 1) * per), axis=0) for b in range(N_DEV)])


def _cols_to_full(g8):
    n, r, c = g8.shape
    return jnp.transpose(g8, (1, 0, 2)).reshape(r, n * c)


def _full_to_cols(a):
    r, c = a.shape
    return jnp.transpose(a.reshape(r, N_DEV, c // N_DEV), (1, 0, 2))


def _pack(arrs, rows):
    flat = jnp.concatenate([a.reshape(-1) for a in arrs])
    return jnp.pad(flat, (0, rows * 128 - flat.shape[0])).reshape(rows, 128)


def _unpack(packed, shapes):
    flat, out, o = packed.reshape(-1), [], 0
    for s in shapes:
        size = 1
        for d in s:
            size *= d
        out.append(flat[o:o + size].reshape(s))
        o += size
    return out


def kernel(x, p, w_in, conv_w, w_a_out, w_alpha_up, b_alpha_up, gla_head_gain, w_b_out, w_mix_out, g_pre_mix, g_post_mix, g_pre_ffn, g_post_ffn, w_ff_gate, w_ff_up, w_ff_down, g_pre_ple, g_post_ple, w_ple_gate, w_ple_proj, loss_target, m_w_in, m_conv_w, m_w_a_out, m_w_alpha_up, m_b_alpha_up, m_gla_head_gain, m_w_b_out, m_w_mix_out, m_g_pre_mix, m_g_post_mix, m_g_pre_ffn, m_g_post_ffn, m_w_ff_gate, m_w_ff_up, m_w_ff_down, m_g_pre_ple, m_g_post_ple, m_w_ple_gate, m_w_ple_proj, v_w_in, v_conv_w, v_w_a_out, v_w_alpha_up, v_b_alpha_up, v_gla_head_gain, v_w_b_out, v_w_mix_out, v_g_pre_mix, v_g_post_mix, v_g_pre_ffn, v_g_post_ffn, v_w_ff_gate, v_w_ff_up, v_w_ff_down, v_g_pre_ple, v_g_post_ple, v_w_ple_gate, v_w_ple_proj):
    args = dict(locals())
    shard = lambda n, a: jnp.transpose(a[0]) if n in TRANSPOSED else a[0]
    wts = {n: shard(n, args[n]) for n in WEIGHTS}
    mom = {n: shard(n, args["m_" + n]) for n in WEIGHTS}
    var = {n: shard(n, args["v_" + n]) for n in WEIGHTS}
    me =4 * lax.axis_index("x") + 2 * lax.axis_index("y") + lax.axis_index("c")

    groups = [["w_in", "conv_w", "w_alpha_up"], ["w_a_out", "w_b_out", "w_mix_out"], ["w_ff_gate", "w_ff_up"], ["w_ff_down"],
              ["w_ple_gate", "w_ple_proj"]]
    grad_groups = []
    rows_full = lambda g: g.reshape(-1, g.shape[-1])
    gathers, scatters = {}, {}

    def gather_start(gi, after):
        if gi not in gathers:
            shards = [wts[n].astype(BF16) if n in BIG else wts[n] for n in groups[gi]]
            zones = [jax.ShapeDtypeStruct((N_DEV,) + s.shape, s.dtype) for s in shards]
            plan, peers = (_gather_plan_near, 3) if gi == 0 else (_gather_plan, 4)
            gathers[gi] = (shards, _copies_start(plan, peers * len(shards), shards, zones, "gather_start_%d" % gi, after))
        return gathers[gi][1][-1]

    def gather_pass_on(gi, after):
        shards, state = gathers[gi]
        gathers[gi] = (shards, _copies_relay(_gather_plan, state, len(shards), _pass_on_plan, 3 * len(shards), "gather_pass_on_%d" % gi, after))
        return gathers[gi][1][-1]

    def gather_finish(gi, after):
        shards, state = gathers[gi]
        shards, zones = _copies_wait(_gather_plan_near if gi == 0 else _pass_on_plan, state, len(shards), "gather_wait_%d" % gi, after)
        if gi == 0:
            zones = _gather_relay(zones, "gather_relay_%d" % gi)
            gather_start(1, (zones[0],))
        g8 = {n: _put_own(s, z, me, "gather_own_" + n) for n, s, z in zip(groups[gi], shards, zones)}
        if gi == 0:
            in_gates_t, in_rest_t = _in_t_from_blocks(g8["w_in"])
            return dict(in_gates_t=in_gates_t, in_rest_t=in_rest_t,
                        conv_w=_cols_to_full(g8["conv_w"]), w_alpha_up=_cols_to_full(g8["w_alpha_up"]))
        if gi == 1:
            return dict(a_out=_cols_to_full(g8["w_a_out"]), b_out=_cols_to_full(g8["w_b_out"]), mix=rows_full(g8["w_mix_out"]))
        if gi == 2:
            return dict(gate_t=rows_full(g8["w_ff_gate"]), up_t=rows_full(g8["w_ff_up"]))
        if gi == 3:
            return dict(down=rows_full(g8["w_ff_down"]))
        return dict(pg=rows_full(g8["w_ple_gate"]), pp=g8["w_ple_proj"])

    def scatter_start(names, gw):
        gi = len(grad_groups)
        grad_groups.append(names)
        full = {n: _blocks_from_in_t(gw["in_gates_t"], gw["in_rest_t"]) if n == "w_in" else gw[GRAD_OF[n]] for n in names}
        for n in names:
            if n in ("w_a_out", "w_b_out"):
                full[n] = _full_to_cols(full[n])
        parts = [full[n] if full[n].ndim == 3 else full[n].reshape(N_DEV, -1, full[n].shape[-1]) for n in names]
        if names == ["w_in"]:
            quarter = [jax.ShapeDtypeStruct((4,) + a.shape[1:], a.dtype) for a in parts]
            scatters[gi] = _copies_start(_sibling_plan, 4 * len(parts), parts, quarter, "scatter_sibling_start_%d" % gi)
        else:
            scatters[gi] = _copies_start(_scatter_plan, (N_DEV - 1) * len(parts), parts, parts, "scatter_start_%d" % gi)
        return scatters[gi][-1]

    def scatter_next(after):
        gi, names = len(grad_groups) - 1, grad_groups[-1]
        parts, from_sibling = _copies_wait(_sibling_plan, scatters[gi], len(names), "scatter_sibling_wait_%d" % gi, after)
        parts = [_pair_add(a, b, "scatter_add_%d_%s" % (gi, n)) for n, a, b in zip(names, parts, from_sibling)]
        scatters[gi] = _copies_start(_chip_plan, 3 * len(parts), parts, parts, "scatter_start_%d" % gi)
        return scatters[gi][-1]

    small = {n: wts[n].reshape(1, -1) for n in SMALL[2:]}

    loss_rows, grad_x, gs = _local_step(x[0], p[0, 0], loss_target[0], gather_start, gather_pass_on, gather_finish, scatter_start,
                                        scatter_next, small)
    gs["loss"] = jnp.sum(loss_rows).reshape(1, 1)

    small_shapes = [gs[n].shape for n in SMALL]
    gs_packed = _pack([gs[n] for n in SMALL + ["loss"]], 192)
    small_state = _copies_start(_everyone_plan, N_DEV - 1, [gs_packed], [jax.ShapeDtypeStruct((N_DEV,) + gs_packed.shape, F32)],
                                "small_start", (grad_x,))

    res, done = {}, (small_state[-1],)
    for gi, names in enumerate(grad_groups):
        plan, slot = (_chip_plan, me // 2) if names == ["w_in"] else (_scatter_plan, me)
        mine, got = _copies_wait(plan, scatters[gi], len(names), "scatter_wait_%d" % gi, done)
        for n, g, own in zip(names, got, mine):
            res[n] = _adamw_parts(wts[n], g, own, slot, mom[n], var[n], "adamw_" + n)
        done = tuple(res[n][1] for n in names)

    (gs_own,), (gs_got,) = _copies_wait(_everyone_plan, small_state, 1, "small_wait", done)
    gsum = dict(zip(SMALL + ["loss"], _unpack(_sum_parts(gs_got, gs_own, me, "small_sum"), small_shapes + [(1, 1)])))
    loss = gsum["loss"].reshape(())
    gsum["conv_w"] = lax.dynamic_index_in_dim(gsum["conv_w"].reshape(3, N_DEV, -1), me, axis=1, keepdims=False)
    gsum["w_alpha_up"] = lax.dynamic_index_in_dim(gsum["w_alpha_up"].reshape(GATE_RANK, N_DEV, -1), me, axis=1, keepdims=False)

    shard_shapes = [wts[n].shape for n in SMALL]
    packed = [_pack([d[n] for n in SMALL], 120) for d in (wts, gsum, mom, var)]
    outs = [_unpack(o, shard_shapes) for o in _adamw(*packed, "adamw_small")]
    for i, n in enumerate(SMALL):
        res[n] = [o[i] for o in outs]

    back = lambda n, a: (jnp.transpose(a) if n in TRANSPOSED else a)[None]
    return (loss, grad_x[None], *[back(n, res[n][i]) for i in range(4) for n in WEIGHTS])
```
